```python
import jax, jax.numpy as jnp
from jax import lax
import numpy as np

D_MODEL = 1024
BATCH = 2
SEQ = 8192
DEPTH = 2

GRID_W = 64
CTX_LEN = 256
EPS = 1e-6
CONV_CH = 512
CONV_WIDTH = 31
NA_HEADS = 8
NA_HEAD_DIM = 64
NA_KR_MAX = 8
NA_KC = 16
EVEN_IN = 2 * CONV_CH + 3 * NA_HEADS * NA_HEAD_DIM
EVEN_MIX = CONV_CH + NA_HEADS * NA_HEAD_DIM
POOL_CH = 512
POOL_WINDOWS = (2, 4, 8, 16)
POOL_GROUP = POOL_CH // len(POOL_WINDOWS)
GLA_HEADS = 4
GLA_DK = 64
GLA_DV = 128
GLA_RANK = 16
GLA_TAU = 16.0
GLA_CHUNK = 64
ODD_IN = POOL_CH + GLA_HEADS * (2 * GLA_DK + 2 * GLA_DV) + 2 * GLA_RANK
ODD_MIX = POOL_CH + GLA_HEADS * GLA_DV
ROPE_BASE = 10000.0
N_EXPERTS = 16
EXPERT_FF = 2816
EC_CAPACITY_FACTOR = 2

kernel_name = 'hybrid_diffusion_conv_na_pool_gla_ecmoe'


def rms_norm(x, g):
    xf = x.astype(jnp.float32)
    y = xf * lax.rsqrt(jnp.mean(xf * xf, axis=-1, keepdims=True) + EPS)
    return (y * g.astype(jnp.float32)).astype(x.dtype)


def layer_norm(x, g, b):
    xf = x.astype(jnp.float32)
    mu = jnp.mean(xf, axis=-1, keepdims=True)
    var = jnp.mean(jnp.square(xf - mu), axis=-1, keepdims=True)
    y = (xf - mu) * lax.rsqrt(var + EPS) * g.astype(jnp.float32) + b.astype(jnp.float32)
    return y.astype(x.dtype)


def adaln(x, g, shift, scale):
    return rms_norm(x, g) * (1.0 + scale) + shift


def axial_rope(x):
    L, dk = x.shape[1], x.shape[-1]
    t = jnp.arange(L)
    pos_r = (t // GRID_W).astype(jnp.float32)
    pos_c = (t % GRID_W).astype(jnp.float32)
    half = dk // 2
    nf = half // 2
    inv = jnp.power(ROPE_BASE, -jnp.arange(nf, dtype=jnp.float32) / nf)

    def rot(xa, pos):
        ang = pos[:, None] * inv[None, :]
        cos = jnp.cos(ang)[None, :, None, :].astype(xa.dtype)
        sin = jnp.sin(ang)[None, :, None, :].astype(xa.dtype)
        x1, x2 = xa[..., :nf], xa[..., nf:]
        return jnp.concatenate([x1 * cos - x2 * sin, x1 * sin + x2 * cos], axis=-1)

    return jnp.concatenate([rot(x[..., :half], pos_r), rot(x[..., half:], pos_c)], axis=-1)


def conformer_conv(u, conv_w, conv_b, ln_g, ln_b):
    glu = u[..., :CONV_CH] * jax.nn.sigmoid(u[..., CONV_CH:])
    y = lax.conv_general_dilated(
        glu, conv_w[:, None, :].astype(glu.dtype), window_strides=(1,),
        padding=[(CONV_WIDTH // 2, CONV_WIDTH // 2)],
        dimension_numbers=('NWC', 'WIO', 'NWC'), feature_group_count=CONV_CH) + conv_b
    return jax.nn.silu(layer_norm(y, ln_g, ln_b))


def neighbourhood_attention(q, k, v, kc, vc, rpb, rows):
    B_, L, H, dh = q.shape
    kr = min(NA_KR_MAX, rows)
    scale = dh ** -0.5
    qg = (q * scale).reshape(B_, rows, GRID_W, H, dh).transpose(1, 0, 3, 2, 4)
    kg = k.reshape(B_, rows, GRID_W, H, dh)
    vg = v.reshape(B_, rows, GRID_W, H, dh)
    col = jnp.arange(GRID_W)
    col_start = jnp.clip(col - NA_KC // 2, 0, GRID_W - NA_KC)
    col_idx = col_start[:, None] + jnp.arange(NA_KC)[None, :]
    col_off = col_idx - col[:, None] + NA_KC - 1

    def row_block(args):
        r, qr = args
        rs = jnp.clip(r - kr // 2, 0, rows - kr)
        kw = lax.dynamic_slice_in_dim(kg, rs, kr, axis=1)[:, :, col_idx]
        vw = lax.dynamic_slice_in_dim(vg, rs, kr, axis=1)[:, :, col_idx]
        row_off = rs + jnp.arange(kr) - r + NA_KR_MAX - 1
        bias = rpb[:, row_off[None, :, None], col_off[:, None, :]]
        s_win = jnp.einsum('bhcd,brckhd->bhcrk', qr, kw) + bias[None].astype(qr.dtype)
        s_ctx = jnp.einsum('bhcd,bnhd->bhcn', qr, kc)
        s = jnp.concatenate([s_win.reshape(B_, H, GRID_W, kr * NA_KC), s_ctx], axis=-1)
        p = jax.nn.softmax(s.astype(jnp.float32), axis=-1).astype(v.dtype)
        p_win = p[..., :kr * NA_KC].reshape(B_, H, GRID_W, kr, NA_KC)
        p_ctx = p[..., kr * NA_KC:]
        return (jnp.einsum('bhcrk,brckhd->bchd', p_win, vw)
                + jnp.einsum('bhcn,bnhd->bchd', p_ctx, vc))

    out = lax.map(row_block, (jnp.arange(rows), qg))
    return out.transpose(1, 0, 2, 3, 4).reshape(B_, L, H, dh)


def context_attention(q, k, v):
    s = jnp.einsum('bqhd,bkhd->bhqk', q, k) * (q.shape[-1] ** -0.5)
    p = jax.nn.softmax(s.astype(jnp.float32), axis=-1).astype(v.dtype)
    return jnp.einsum('bhqk,bkhd->bqhd', p, v)


def multiscale_pool(u, w, scale):
    B_, L, _ = u.shape
    uf = u.astype(jnp.float32)
    cs = jnp.concatenate([jnp.zeros((B_, 1, POOL_CH), jnp.float32), jnp.cumsum(uf, axis=1)], axis=1)
    t = jnp.arange(L)
    diffs = []
    for gi, win in enumerate(POOL_WINDOWS):
        sl = slice(gi * POOL_GROUP, (gi + 1) * POOL_GROUP)
        lo = jnp.clip(t - win // 2, 0, L)
        hi = jnp.clip(t + win - win // 2, 0, L)
        csg = cs[:, :, sl]
        mean = (csg[:, hi] - csg[:, lo]) / (hi - lo).astype(jnp.float32)[None, :, None]
        diffs.append(mean - uf[:, :, sl])
    d = jnp.stack(diffs, axis=2)
    y = jnp.einsum('blgc,gce->blge', d, w.astype(jnp.float32)).reshape(B_, L, POOL_CH)
    return (y * scale.astype(jnp.float32)).astype(u.dtype)


def gla_scan(q, k, v, g, s0):
    B_, L, H, _ = q.shape
    dv = v.shape[-1]
    n = L // GLA_CHUNK
    mask = jnp.tril(jnp.ones((GLA_CHUNK, GLA_CHUNK), dtype=bool))

    def to_chunks(a):
        return a.reshape(B_, n, GLA_CHUNK, H, a.shape[-1]).transpose(1, 0, 3, 2, 4)

    def step(S, inp):
        qc, kc, vc, gc = inp
        b = jnp.cumsum(gc, axis=2)
        inter = jnp.einsum('bhid,bhde->bhie', qc * jnp.exp(b), S)
        diff = b[:, :, :, None, :] - b[:, :, None, :, :]
        decay = jnp.exp(jnp.where(mask[:, :, None], diff, -jnp.inf))
        att = jnp.einsum('bhid,bhjd,bhijd->bhij', qc, kc, decay)
        intra = jnp.einsum('bhij,bhje->bhie', att, vc)
        b_last = b[:, :, -1:, :]
        S_new = (jnp.exp(b_last[:, :, 0, :])[..., None] * S
                 + jnp.einsum('bhjd,bhje->bhde', kc * jnp.exp(b_last - b), vc))
        return S_new, inter + intra

    S_fin, o = lax.scan(step, s0, (to_chunks(q), to_chunks(k), to_chunks(v), to_chunks(g)))
    return o.transpose(1, 0, 3, 2, 4).reshape(B_, L, H, dv), S_fin


def gla_final_state(k, v, g):
    b = jnp.cumsum(g, axis=1)
    return jnp.einsum('blhd,blhe->bhde', k * jnp.exp(b[:, -1:] - b), v)


def flip(a):
    return a[:, ::-1]


def even_mixer(h, h_ctx, rows, w_in, w_out, conv_w, conv_b, ln_g, ln_b, rpb, ctx_out):
    B_, L, _ = h.shape
    N = h_ctx.shape[1]
    hd = NA_HEADS * NA_HEAD_DIM
    q0 = 2 * CONV_CH
    k0 = q0 + hd
    v0 = k0 + hd

    def heads(a):
        return a.reshape(a.shape[0], a.shape[1], NA_HEADS, NA_HEAD_DIM)

    p = h @ w_in
    off = 0 if ctx_out else k0
    pc = h_ctx @ w_in[:, off:]
    kc = heads(pc[..., k0 - off:v0 - off])
    vc = heads(pc[..., v0 - off:])
    a_lat = conformer_conv(p[..., :q0], conv_w, conv_b, ln_g, ln_b)
    na = neighbourhood_attention(heads(p[..., q0:k0]), heads(p[..., k0:v0]), heads(p[..., v0:]), kc, vc, rpb, rows)
    y = jnp.concatenate([a_lat, na.reshape(B_, L, hd)], axis=-1) @ w_out
    if not ctx_out:
        return y, None
    a_ctx = conformer_conv(pc[..., :q0], conv_w, conv_b, ln_g, ln_b)
    att = context_attention(heads(pc[..., q0:k0]), kc, vc)
    y_ctx = jnp.concatenate([a_ctx, att.reshape(B_, N, hd)], axis=-1) @ w_out
    return y, y_ctx


def odd_mixer(h, h_ctx, w_in, w_out, pool_w, pool_scale, gate_w, gate_b, head_g, ctx_out):
    B_, L, _ = h.shape
    N = h_ctx.shape[1]
    qk = GLA_HEADS * GLA_DK
    vd = GLA_HEADS * GLA_DV
    q0 = POOL_CH
    k0 = q0 + qk
    v0 = k0 + qk
    r0 = v0 + vd
    l0 = r0 + vd
    f32 = jnp.float32
    qscale = GLA_DK ** -0.5

    def heads(a):
        return a.reshape(a.shape[0], a.shape[1], GLA_HEADS, -1).astype(f32)

    def gates(lr):
        lr = lr.astype(f32)
        gf = jax.nn.log_sigmoid(lr[..., :GLA_RANK] @ gate_w[0] + gate_b[0]) / GLA_TAU
        gb = jax.nn.log_sigmoid(lr[..., GLA_RANK:] @ gate_w[1] + gate_b[1]) / GLA_TAU
        return heads(gf), heads(gb)

    def gla_out(o, r):
        o = rms_norm(o.astype(h.dtype), head_g.reshape(GLA_HEADS, GLA_DV))
        return o.reshape(o.shape[0], o.shape[1], vd) * jax.nn.silu(r)

    if ctx_out:
        pc = h_ctx @ w_in
        qc = heads(pc[..., q0:k0]) * qscale
        kc, vc = heads(pc[..., k0:v0]), heads(pc[..., v0:r0])
        gcf, gcb = gates(pc[..., l0:])
        s0 = jnp.zeros((B_, GLA_HEADS, GLA_DK, GLA_DV), f32)
        oc_f, s_f = gla_scan(qc, kc, vc, gcf, s0)
        oc_b, s_b = gla_scan(flip(qc), flip(kc), flip(vc), flip(gcb), s0)
        d_ctx = gla_out(oc_f + flip(oc_b), pc[..., r0:l0])
    else:
        pkv = h_ctx @ w_in[:, k0:r0]
        plr = h_ctx @ w_in[:, l0:]
        kc, vc = heads(pkv[..., :qk]), heads(pkv[..., qk:])
        gcf, gcb = gates(plr)
        s_f = gla_final_state(kc, vc, gcf)
        s_b = gla_final_state(flip(kc), flip(vc), flip(gcb))

    p = h @ w_in
    q = axial_rope(heads(p[..., q0:k0])) * qscale
    k = axial_rope(heads(p[..., k0:v0]))
    v = heads(p[..., v0:r0])
    gf, gb = gates(p[..., l0:])
    o_f, _ = gla_scan(q, k, v, gf, s_f)
    o_b, _ = gla_scan(flip(q), flip(k), flip(v), flip(gb), s_b)
    d_lat = gla_out(o_f + flip(o_b), p[..., r0:l0])
    y = jnp.concatenate([multiscale_pool(p[..., :q0], pool_w, pool_scale), d_lat], axis=-1) @ w_out
    if not ctx_out:
        return y, None
    y_ctx = jnp.concatenate([multiscale_pool(pc[..., :q0], pool_w, pool_scale), d_ctx], axis=-1) @ w_out
    return y, y_ctx


def ec_moe(x, router_w, w_gate, w_up, w_down):
    B_, N, D = x.shape
    cap = EC_CAPACITY_FACTOR * N // N_EXPERTS
    aff = jax.nn.softmax((x @ router_w).astype(jnp.float32), axis=-1)
    vals, idx = lax.top_k(jnp.swapaxes(aff, 1, 2), cap)
    xg = jax.vmap(lambda xb, ib: xb[ib])(x, idx)
    hid = jax.nn.silu(jnp.einsum('becd,edf->becf', xg, w_gate)) * jnp.einsum('becd,edf->becf', xg, w_up)
    y = jnp.einsum('becf,efd->becd', hid, w_down) * vals[..., None].astype(x.dtype)
    return jax.vmap(lambda yb, ib: jnp.zeros((N, D), yb.dtype).at[ib.reshape(-1)].add(yb.reshape(-1, D)))(y, idx)


def setup_inputs(seed: int = 0) -> dict:
    key = jax.random.key(seed)
    ks = iter(jax.random.split(key, 40))
    n_even = (DEPTH + 1) // 2
    n_odd = DEPTH // 2
    D = D_MODEL

    def nrm(shape, scale):
        return jax.random.normal(next(ks), shape, jnp.float32) * scale

    def gain(shape):
        return 1.0 + 0.01 * jax.random.normal(next(ks), shape, jnp.float32)

    return {
        'x': nrm((BATCH, SEQ, D), 1.0),
        'c': nrm((BATCH, D), 1.0),
        'ctx': nrm((BATCH, CTX_LEN, D), 1.0),
        'c_ctx': nrm((D,), 1.0),
        'w_mod': nrm((DEPTH, D, 6 * D), 0.5 * D ** -0.5),
        'b_mod': nrm((DEPTH, 6 * D), 0.01),
        'norm_g': gain((DEPTH, 4, D)),
        'w_in_even': nrm((n_even, D, EVEN_IN), D ** -0.5),
        'w_out_even': nrm((n_even, EVEN_MIX, D), EVEN_MIX ** -0.5),
        'conv_w': nrm((n_even, CONV_WIDTH, CONV_CH), CONV_WIDTH ** -0.5),
        'conv_b': nrm((n_even, CONV_CH), 0.01),
        'conv_ln_g': gain((n_even, CONV_CH)),
        'conv_ln_b': nrm((n_even, CONV_CH), 0.01),
        'na_rpb': nrm((n_even, NA_HEADS, 2 * NA_KR_MAX - 1, 2 * NA_KC - 1), 0.02),
        'w_in_odd': nrm((n_odd, D, ODD_IN), D ** -0.5),
        'w_out_odd': nrm((n_odd, ODD_MIX, D), ODD_MIX ** -0.5),
        'pool_w': nrm((n_odd, len(POOL_WINDOWS), POOL_GROUP, POOL_GROUP), POOL_GROUP ** -0.5),
        'pool_scale': gain((n_odd, POOL_CH)),
        'gla_gate_w': nrm((n_odd, 2, GLA_RANK, GLA_HEADS * GLA_DK), GLA_RANK ** -0.5),
        'gla_gate_b': nrm((n_odd, 2, GLA_HEADS * GLA_DK), 0.01),
        'gla_head_g': gain((n_odd, GLA_HEADS * GLA_DV)),
        'router_w': nrm((DEPTH, D, N_EXPERTS), D ** -0.5),
        'expert_w_gate': nrm((DEPTH, N_EXPERTS, D, EXPERT_FF), D ** -0.5),
        'expert_w_up': nrm((DEPTH, N_EXPERTS, D, EXPERT_FF), D ** -0.5),
        'expert_w_down': nrm((DEPTH, N_EXPERTS, EXPERT_FF, D), EXPERT_FF ** -0.5),
    }


def reference(x, c, ctx, c_ctx, w_mod, b_mod, norm_g, w_in_even, w_out_even, conv_w, conv_b,
              conv_ln_g, conv_ln_b, na_rpb, w_in_odd, w_out_odd, pool_w, pool_scale,
              gla_gate_w, gla_gate_b, gla_head_g, router_w, expert_w_gate, expert_w_up, expert_w_down):
    B_, L, D = x.shape
    rows = L // GRID_W
    for i in range(DEPTH):
        ctx_out = i < DEPTH - 1
        j = i // 2
        m = (jax.nn.silu(c) @ w_mod[i] + b_mod[i]).reshape(B_, 6, 1, D)
        mc = (jax.nn.silu(c_ctx) @ w_mod[i] + b_mod[i]).reshape(6, D)
        g = norm_g[i]
        h = adaln(x, g[0], m[:, 0], m[:, 1])
        hc = adaln(ctx, g[0], mc[0], mc[1])
        if i % 2 == 0:
            y, yc = even_mixer(h, hc, rows, w_in_even[j], w_out_even[j], conv_w[j], conv_b[j],
                               conv_ln_g[j], conv_ln_b[j], na_rpb[j], ctx_out)
        else:
            y, yc = odd_mixer(h, hc, w_in_odd[j], w_out_odd[j], pool_w[j], pool_scale[j],
                              gla_gate_w[j], gla_gate_b[j], gla_head_g[j], ctx_out)
        x = x + m[:, 2] * rms_norm(y, g[1])
        f = ec_moe(adaln(x, g[2], m[:, 3], m[:, 4]), router_w[i], expert_w_gate[i], expert_w_up[i], expert_w_down[i])
        x = x + m[:, 5] * rms_norm(f, g[3])
        if ctx_out:
            ctx = ctx + mc[2] * rms_norm(yc, g[1])
            fc = ec_moe(adaln(ctx, g[2], mc[3], mc[4]), router_w[i], expert_w_gate[i], expert_w_up[i], expert_w_down[i])
            ctx = ctx + mc[5] * rms_norm(fc, g[3])
    return x
```

```python
import functools

import jax
import jax.numpy as jnp
from jax import lax
from jax.experimental import pallas as pl
from jax.experimental.pallas import tpu as pltpu

F32 = jnp.float32
BF16 = jnp.bfloat16
HIGHEST = lax.Precision.HIGHEST

D_MODEL = 1024
GRID_W = 64
EPS = 1e-6
CONV_CH = 512
CONV_WIDTH = 31
CONV_HALO = 16
NA_HEADS = 8
NA_HEAD_DIM = 64
NA_KR = 8
NA_KC = 16
NA_ROWS_PER_BLOCK = 4
NA_WIN_ROWS = 12
POOL_CH = 512
POOL_WINDOWS = (2, 4, 8, 16)
POOL_GROUP = 128
POOL_HALO = 8
GLA_HEADS = 4
GLA_DK = 64
GLA_DV = 128
GLA_RANK = 16
GLA_TAU = 16.0
GLA_CHUNK = 64
ROPE_BASE = 10000.0
N_EXPERTS = 16
EXPERT_FF = 2816
EC_CAPACITY_FACTOR = 2
LANES = 128
NEG_BIG = -1e30
VMEM_LIMIT = 56 * 1024 * 1024


def _cparams(sem):
    return pltpu.CompilerParams(dimension_semantics=sem, vmem_limit_bytes=VMEM_LIMIT)


def _rms(x, g):
    return x * lax.rsqrt(jnp.mean(x * x, axis=-1, keepdims=True) + EPS) * g


def _sigmoid(x):
    return 1.0 / (1.0 + jnp.exp(-x))


def _silu(x):
    return x * _sigmoid(x)


def _dot(a, b):
    return jnp.dot(a, b, preferred_element_type=F32)


def _dot_nt(a, b):
    return lax.dot_general(a, b, (((1,), (1,)), ((), ())), preferred_element_type=F32)


def _mod_body(c_ref, w_ref, b_ref, o_ref):
    o_ref[...] = jnp.dot(_silu(c_ref[...]), w_ref[...], precision=HIGHEST,
                         preferred_element_type=F32) + b_ref[...]


def _modulation(rows, w, b):
    n = w.shape[1]
    tn = 1536
    return pl.pallas_call(
        _mod_body,
        grid=(n // tn,),
        in_specs=[pl.BlockSpec((8, D_MODEL), lambda j: (0, 0)),
                  pl.BlockSpec((D_MODEL, tn), lambda j: (0, j)),
                  pl.BlockSpec((1, tn), lambda j: (0, j))],
        out_specs=pl.BlockSpec((8, tn), lambda j: (0, j)),
        out_shape=jax.ShapeDtypeStruct((8, n), F32),
        compiler_params=_cparams(("arbitrary",)),
        name="modulation",
    )(rows, w, b.reshape(1, n))


def _even_in_body(x_ref, g_ref, sh_ref, sc_ref, w_ref, glu_ref, qkv_ref):
    h = (_rms(x_ref[0], g_ref[...]) * (1.0 + sc_ref[0]) + sh_ref[0]).astype(BF16)
    c = CONV_CH
    glu_ref[0] = _dot(h, w_ref[:, 0:c]) * _sigmoid(_dot(h, w_ref[:, c:2 * c]))
    hd = NA_HEADS * NA_HEAD_DIM
    q0 = 2 * c
    qkv_ref[0, :, 0:hd] = (_dot(h, w_ref[:, q0:q0 + hd]) * (NA_HEAD_DIM ** -0.5)).astype(BF16)
    qkv_ref[0, :, hd:3 * hd] = _dot(h, w_ref[:, q0 + hd:q0 + 3 * hd]).astype(BF16)


def _even_in(x, g, shift, scale, w_bf, tm):
    b, l, d = x.shape
    n = w_bf.shape[1]
    hd3 = 3 * NA_HEADS * NA_HEAD_DIM
    vec = pl.BlockSpec((1, 1, d), lambda i, j: (i, 0, 0))
    return pl.pallas_call(
        _even_in_body,
        grid=(b, l // tm),
        in_specs=[pl.BlockSpec((1, tm, d), lambda i, j: (i, j, 0)),
                  pl.BlockSpec((1, d), lambda i, j: (0, 0)),
                  vec, vec,
                  pl.BlockSpec((d, n), lambda i, j: (0, 0))],
        out_specs=[pl.BlockSpec((1, tm, CONV_CH), lambda i, j: (i, j, 0)),
                   pl.BlockSpec((1, tm, hd3), lambda i, j: (i, j, 0))],
        out_shape=[jax.ShapeDtypeStruct((b, l, CONV_CH), F32),
                   jax.ShapeDtypeStruct((b, l, hd3), BF16)],
        compiler_params=_cparams(("parallel", "parallel")),
        name="even_in",
    )(x, g.reshape(1, d), shift.reshape(b, 1, d), scale.reshape(b, 1, d), w_bf)


def _conv_body(cur_ref, prev_ref, next_ref, w_ref, b_ref, lg_ref, lb_ref, o_ref, buf_ref, *, tile, chunk):
    j = pl.program_id(1)
    last = pl.num_programs(1) - 1
    hal = CONV_HALO
    buf_ref[0:hal, :] = jnp.where(j == 0, 0.0, prev_ref[0])
    buf_ref[hal:hal + tile, :] = cur_ref[0]
    buf_ref[hal + tile:hal + tile + hal, :] = jnp.where(j == last, 0.0, next_ref[0])
    first = hal - CONV_WIDTH // 2
    for r0 in range(0, tile, chunk):
        acc = jnp.zeros((chunk, CONV_CH), F32)
        for k in range(CONV_WIDTH):
            acc = acc + buf_ref[r0 + first + k:r0 + first + k + chunk, :] * w_ref[k:k + 1, :]
        y = acc + b_ref[...]
        mu = jnp.mean(y, axis=-1, keepdims=True)
        yc = y - mu
        var = jnp.mean(yc * yc, axis=-1, keepdims=True)
        o_ref[0, r0:r0 + chunk, :] = _silu(yc * lax.rsqrt(var + EPS) * lg_ref[...] + lb_ref[...])


def _conv_branch(glu, conv_w, conv_b, ln_g, ln_b, tile):
    b, l, c = glu.shape
    hal = CONV_HALO
    per = tile // hal
    nh = l // hal
    vec = pl.BlockSpec((1, c), lambda i, j: (0, 0))
    return pl.pallas_call(
        functools.partial(_conv_body, tile=tile, chunk=16),
        grid=(b, l // tile),
        in_specs=[pl.BlockSpec((1, tile, c), lambda i, j: (i, j, 0)),
                  pl.BlockSpec((1, hal, c), lambda i, j: (i, jnp.maximum(j * per - 1, 0), 0)),
                  pl.BlockSpec((1, hal, c), lambda i, j: (i, jnp.minimum((j + 1) * per, nh - 1), 0)),
                  pl.BlockSpec((CONV_WIDTH, c), lambda i, j: (0, 0)),
                  vec, vec, vec],
        out_specs=pl.BlockSpec((1, tile, c), lambda i, j: (i, j, 0)),
        out_shape=jax.ShapeDtypeStruct((b, l, c), F32),
        scratch_shapes=[pltpu.VMEM((tile + 2 * hal, c), F32)],
        compiler_params=_cparams(("parallel", "parallel")),
        name="conv_branch",
    )(glu, glu, glu, conv_w, conv_b.reshape(1, c), ln_g.reshape(1, c), ln_b.reshape(1, c))


def _na_window_start(j, rows):
    rb = NA_ROWS_PER_BLOCK
    return jnp.clip(j * rb - NA_KR // 2, 0, rows - NA_WIN_ROWS)


def _na_body(q_ref, k_ref, v_ref, kc_ref, vc_ref, tab_ref, o_ref, *, rows):
    j = pl.program_id(2)
    nkeys = NA_WIN_ROWS * GRID_W
    start = pl.multiple_of(_na_window_start(j, rows) * GRID_W, GRID_W)
    q = q_ref[0]
    kw = k_ref[0, pl.ds(start, nkeys), :]
    vw = v_ref[0, pl.ds(start, nkeys), :]
    kc = kc_ref[0]
    vc = vc_ref[0]
    lane = lax.broadcasted_iota(jnp.int32, (1, LANES), 1)
    out = jnp.zeros(q.shape, F32)
    for hh in range(LANES // NA_HEAD_DIM):
        in_head = (lane >= hh * NA_HEAD_DIM) & (lane < (hh + 1) * NA_HEAD_DIM)
        qh = jnp.where(in_head, q, jnp.zeros_like(q))
        s = _dot_nt(qh, kw) + tab_ref[0, hh]
        sc = _dot_nt(qh, kc)
        m = jnp.maximum(jnp.max(s, axis=-1, keepdims=True), jnp.max(sc, axis=-1, keepdims=True))
        p = jnp.exp(s - m)
        pc = jnp.exp(sc - m)
        denom = jnp.sum(p, axis=-1, keepdims=True) + jnp.sum(pc, axis=-1, keepdims=True)
        o = (_dot(p.astype(BF16), vw) + _dot(pc.astype(BF16), vc)) / denom
        out = jnp.where(in_head, o, out)
    o_ref[0] = out


def _na_tables(rpb, rows):
    rb = NA_ROWS_PER_BLOCK
    nblk = rows // rb
    wr = NA_WIN_ROWS
    tabs = []
    for jb in (0, 1, nblk - 1):
        ws = min(max(jb * rb - NA_KR // 2, 0), rows - wr)
        qr = jb * rb + jnp.arange(rb)
        qc = jnp.arange(GRID_W)
        kr = ws + jnp.arange(wr)
        kcol = jnp.arange(GRID_W)
        rs = jnp.clip(qr - NA_KR // 2, 0, rows - NA_KR)
        cs = jnp.clip(qc - NA_KC // 2, 0, GRID_W - NA_KC)
        row_ok = (kr[None, :] >= rs[:, None]) & (kr[None, :] < rs[:, None] + NA_KR)
        col_ok = (kcol[None, :] >= cs[:, None]) & (kcol[None, :] < cs[:, None] + NA_KC)
        row_off = jnp.clip(kr[None, :] - qr[:, None] + NA_KR - 1, 0, 2 * NA_KR - 2)
        col_off = jnp.clip(kcol[None, :] - qc[:, None] + NA_KC - 1, 0, 2 * NA_KC - 2)
        bias = rpb[:, row_off[:, None, :, None], col_off[None, :, None, :]]
        ok = row_ok[:, None, :, None] & col_ok[None, :, None, :]
        tab = jnp.where(ok[None], bias, NEG_BIG)
        tabs.append(tab.reshape(NA_HEADS, rb * GRID_W, wr * GRID_W))
    return jnp.stack(tabs).astype(F32)


def _neighbourhood_attention(qkv, qkv_ctx, rpb):
    b, l, _ = qkv.shape
    n_ctx = qkv_ctx.shape[1]
    rows = l // GRID_W
    rb = NA_ROWS_PER_BLOCK
    nblk = rows // rb
    tq = rb * GRID_W
    nkeys = NA_WIN_ROWS * GRID_W
    hp = NA_HEADS * NA_HEAD_DIM // LANES
    tabs = _na_tables(rpb, rows)

    def cls(j):
        return jnp.where(j == 0, 0, jnp.where(j == nblk - 1, 2, 1))

    return pl.pallas_call(
        functools.partial(_na_body, rows=rows),
        grid=(b, hp, nblk),
        in_specs=[pl.BlockSpec((1, tq, LANES), lambda i, h, j: (i, j, h)),
                  pl.BlockSpec((1, l, LANES), lambda i, h, j: (i, 0, hp + h)),
                  pl.BlockSpec((1, l, LANES), lambda i, h, j: (i, 0, 2 * hp + h)),
                  pl.BlockSpec((1, n_ctx, LANES), lambda i, h, j: (i, 0, hp + h)),
                  pl.BlockSpec((1, n_ctx, LANES), lambda i, h, j: (i, 0, 2 * hp + h)),
                  pl.BlockSpec((1, 2, tq, nkeys), lambda i, h, j: (cls(j), h, 0, 0))],
        out_specs=pl.BlockSpec((1, tq, LANES), lambda i, h, j: (i, j, h)),
        out_shape=jax.ShapeDtypeStruct((b, l, NA_HEADS * NA_HEAD_DIM), F32),
        compiler_params=_cparams(("parallel", "parallel", "arbitrary")),
        name="neighbourhood_attention",
    )(qkv, qkv, qkv, qkv_ctx, qkv_ctx, tabs)


def _ctx_attn_body(q_ref, k_ref, v_ref, o_ref):
    q = q_ref[0]
    k = k_ref[0]
    v = v_ref[0]
    lane = lax.broadcasted_iota(jnp.int32, (1, LANES), 1)
    out = jnp.zeros(q.shape, F32)
    for hh in range(LANES // NA_HEAD_DIM):
        in_head = (lane >= hh * NA_HEAD_DIM) & (lane < (hh + 1) * NA_HEAD_DIM)
        qh = jnp.where(in_head, q, jnp.zeros_like(q))
        s = _dot_nt(qh, k)
        p = jnp.exp(s - jnp.max(s, axis=-1, keepdims=True))
        o = _dot(p.astype(BF16), v) / jnp.sum(p, axis=-1, keepdims=True)
        out = jnp.where(in_head, o, out)
    o_ref[0] = out


def _context_attention(qkv_ctx):
    b, n, _ = qkv_ctx.shape
    hp = NA_HEADS * NA_HEAD_DIM // LANES
    return pl.pallas_call(
        _ctx_attn_body,
        grid=(b, hp),
        in_specs=[pl.BlockSpec((1, n, LANES), lambda i, h: (i, 0, h)),
                  pl.BlockSpec((1, n, LANES), lambda i, h: (i, 0, hp + h)),
                  pl.BlockSpec((1, n, LANES), lambda i, h: (i, 0, 2 * hp + h))],
        out_specs=pl.BlockSpec((1, n, LANES), lambda i, h: (i, 0, h)),
        out_shape=jax.ShapeDtypeStruct((b, n, NA_HEADS * NA_HEAD_DIM), F32),
        compiler_params=_cparams(("parallel", "parallel")),
        name="context_attention",
    )(qkv_ctx, qkv_ctx, qkv_ctx)


def _out_body(a_ref, b_ref, x_ref, w_ref, g1_ref, gate_ref, g2_ref, sh_ref, sc_ref, rw_ref,
              xo_ref, h_ref, aff_ref):
    half = a_ref.shape[-1]
    y = _dot(a_ref[0].astype(BF16), w_ref[0:half, :]) + _dot(b_ref[0].astype(BF16), w_ref[half:2 * half, :])
    xn = x_ref[0] + gate_ref[0] * _rms(y, g1_ref[...])
    xo_ref[0] = xn
    h = _rms(xn, g2_ref[...]) * (1.0 + sc_ref[0]) + sh_ref[0]
    h_ref[0] = h.astype(BF16)
    logits = jnp.dot(h, rw_ref[...], precision=HIGHEST, preferred_element_type=F32)
    lane = lax.broadcasted_iota(jnp.int32, (1, LANES), 1)
    logits = jnp.where(lane < N_EXPERTS, logits, NEG_BIG)
    e = jnp.exp(logits - jnp.max(logits, axis=-1, keepdims=True))
    aff = e / jnp.sum(e, axis=-1, keepdims=True)
    aff_ref[0] = aff.T[0:N_EXPERTS, :]


def _out_proj(a, b2, x, w_bf, g1, gate, g2, shift, scale, router_w, tm):
    b, l, d = x.shape
    half = a.shape[-1]
    rw = jnp.pad(router_w, ((0, 0), (0, LANES - N_EXPERTS)))
    vec = pl.BlockSpec((1, d), lambda i, j: (0, 0))
    bvec = pl.BlockSpec((1, 1, d), lambda i, j: (i, 0, 0))
    return pl.pallas_call(
        _out_body,
        grid=(b, l // tm),
        in_specs=[pl.BlockSpec((1, tm, half), lambda i, j: (i, j, 0)),
                  pl.BlockSpec((1, tm, half), lambda i, j: (i, j, 0)),
                  pl.BlockSpec((1, tm, d), lambda i, j: (i, j, 0)),
                  pl.BlockSpec((2 * half, d), lambda i, j: (0, 0)),
                  vec, bvec, vec, bvec, bvec,
                  pl.BlockSpec((d, LANES), lambda i, j: (0, 0))],
        out_specs=[pl.BlockSpec((1, tm, d), lambda i, j: (i, j, 0)),
                   pl.BlockSpec((1, tm, d), lambda i, j: (i, j, 0)),
                   pl.BlockSpec((1, N_EXPERTS, tm), lambda i, j: (i, 0, j))],
        out_shape=[jax.ShapeDtypeStruct((b, l, d), F32),
                   jax.ShapeDtypeStruct((b, l, d), BF16),
                   jax.ShapeDtypeStruct((b, N_EXPERTS, l), F32)],
        compiler_params=_cparams(("parallel", "parallel")),
        name="out_proj",
    )(a, b2, x, w_bf, g1.reshape(1, d), gate.reshape(b, 1, d), g2.reshape(1, d),
      shift.reshape(b, 1, d), scale.reshape(b, 1, d), rw)


def _moe_body(x_ref, val_ref, wg_ref, wu_ref, wd_ref, o_ref, wg_s, wu_s, wd_s, *, chunk):
    f = pl.program_id(1)
    wg_s[...] = wg_ref[0].astype(BF16)
    wu_s[...] = wu_ref[0].astype(BF16)
    wd_s[...] = wd_ref[0].astype(BF16)
    m = x_ref.shape[1]

    def rows(c, carry):
        r = pl.multiple_of(c * chunk, chunk)
        xs = x_ref[0, pl.ds(r, chunk), :]
        hid = (_silu(_dot(xs, wg_s[...])) * _dot(xs, wu_s[...])).astype(BF16)
        y = _dot(hid, wd_s[...])

        @pl.when(f == 0)
        def _():
            o_ref[0, pl.ds(r, chunk), :] = y

        @pl.when(f > 0)
        def _():
            o_ref[0, pl.ds(r, chunk), :] += y

        return carry

    lax.fori_loop(0, m // chunk, rows, 0)

    @pl.when(f == pl.num_programs(1) - 1)
    def _():
        o_ref[0] = o_ref[0] * val_ref[0]


def _expert_ffn(xg, vals, w_gate, w_up, w_down, chunk):
    e, m, d = xg.shape
    ff = w_gate.shape[-1]
    tf = 256
    return pl.pallas_call(
        functools.partial(_moe_body, chunk=chunk),
        grid=(e, ff // tf),
        in_specs=[pl.BlockSpec((1, m, d), lambda i, f: (i, 0, 0)),
                  pl.BlockSpec((1, m, 1), lambda i, f: (i, 0, 0)),
                  pl.BlockSpec((1, d, tf), lambda i, f: (i, 0, f)),
                  pl.BlockSpec((1, d, tf), lambda i, f: (i, 0, f)),
                  pl.BlockSpec((1, tf, d), lambda i, f: (i, f, 0))],
        out_specs=pl.BlockSpec((1, m, d), lambda i, f: (i, 0, 0)),
        out_shape=jax.ShapeDtypeStruct((e, m, d), F32),
        scratch_shapes=[pltpu.VMEM((d, tf), BF16), pltpu.VMEM((d, tf), BF16), pltpu.VMEM((tf, d), BF16)],
        compiler_params=_cparams(("parallel", "arbitrary")),
        name="expert_ffn",
    )(xg, vals, w_gate, w_up, w_down)


def _resid_body(x_ref, f_ref, gate_ref, g_ref, o_ref):
    o_ref[0] = x_ref[0] + gate_ref[0] * _rms(f_ref[0], g_ref[...])


def _gated_residual(x, f, gate, g, tm):
    b, l, d = x.shape
    blk = pl.BlockSpec((1, tm, d), lambda i, j: (i, j, 0))
    return pl.pallas_call(
        _resid_body,
        grid=(b, l // tm),
        in_specs=[blk, blk, pl.BlockSpec((1, 1, d), lambda i, j: (i, 0, 0)),
                  pl.BlockSpec((1, d), lambda i, j: (0, 0))],
        out_specs=blk,
        out_shape=jax.ShapeDtypeStruct((b, l, d), F32),
        compiler_params=_cparams(("parallel", "parallel")),
        name="gated_residual",
    )(x, f, gate.reshape(b, 1, d), g.reshape(1, d))


def _swap_pairs(x):
    nf = GLA_DK // 4
    lane = lax.broadcasted_iota(jnp.int32, (1, LANES), 1)
    up = pltpu.roll(x, LANES - nf, 1)
    down = pltpu.roll(x, nf, 1)
    return jnp.where(lane % (2 * nf) < nf, up, down)


def _odd_in_body(x_ref, g_ref, sh_ref, sc_ref, w_ref, cos_ref, sin_ref, gw_ref, gb_ref,
                 pool_ref, qk_ref, v_ref, r_ref, gate_ref):
    h = (_rms(x_ref[0], g_ref[...]) * (1.0 + sc_ref[0]) + sh_ref[0]).astype(BF16)
    qk = GLA_HEADS * GLA_DK
    vd = GLA_HEADS * GLA_DV
    q0 = POOL_CH
    v0 = q0 + 2 * qk
    r0 = v0 + vd
    l0 = r0 + vd
    pool_ref[0] = _dot(h, w_ref[:, 0:q0])
    for s in range(2 * qk // LANES):
        raw = _dot(h, w_ref[:, q0 + s * LANES:q0 + (s + 1) * LANES])
        c = cos_ref[:, (s * LANES) % qk:(s * LANES) % qk + LANES]
        sn = sin_ref[:, (s * LANES) % qk:(s * LANES) % qk + LANES]
        rot = raw * c + _swap_pairs(raw) * sn
        if s * LANES < qk:
            rot = rot * (GLA_DK ** -0.5)
        qk_ref[0, :, s * LANES:(s + 1) * LANES] = rot
    v_ref[0] = _dot(h, w_ref[:, v0:r0]).astype(BF16)
    r_ref[0] = _dot(h, w_ref[:, r0:l0])
    lr = _dot(h, w_ref[:, l0:l0 + 2 * GLA_RANK])
    z = jnp.dot(lr, gw_ref[...], precision=HIGHEST, preferred_element_type=F32) + gb_ref[...]
    gate_ref[0] = (jnp.minimum(z, 0.0) - jnp.log1p(jnp.exp(-jnp.abs(z)))) * (1.0 / GLA_TAU)


def _odd_in(x, g, shift, scale, w_bf, cos_t, sin_t, gate_w, gate_b, tm):
    b, l, d = x.shape
    n = w_bf.shape[1]
    qk = GLA_HEADS * GLA_DK
    vd = GLA_HEADS * GLA_DV
    gw = jnp.zeros((2 * GLA_RANK, 2 * qk), F32)
    gw = gw.at[:GLA_RANK, :qk].set(gate_w[0]).at[GLA_RANK:, qk:].set(gate_w[1])
    gb = jnp.concatenate([gate_b[0], gate_b[1]]).reshape(1, 2 * qk)
    vec = pl.BlockSpec((1, 1, d), lambda i, j: (i, 0, 0))
    row = lambda w: pl.BlockSpec((1, tm, w), lambda i, j: (i, j, 0))
    return pl.pallas_call(
        _odd_in_body,
        grid=(b, l // tm),
        in_specs=[row(d), pl.BlockSpec((1, d), lambda i, j: (0, 0)), vec, vec,
                  pl.BlockSpec((d, n), lambda i, j: (0, 0)),
                  pl.BlockSpec((tm, qk), lambda i, j: (j, 0)),
                  pl.BlockSpec((tm, qk), lambda i, j: (j, 0)),
                  pl.BlockSpec((2 * GLA_RANK, 2 * qk), lambda i, j: (0, 0)),
                  pl.BlockSpec((1, 2 * qk), lambda i, j: (0, 0))],
        out_specs=[row(POOL_CH), row(2 * qk), row(vd), row(vd), row(2 * qk)],
        out_shape=[jax.ShapeDtypeStruct((b, l, POOL_CH), F32),
                   jax.ShapeDtypeStruct((b, l, 2 * qk), F32),
                   jax.ShapeDtypeStruct((b, l, vd), BF16),
                   jax.ShapeDtypeStruct((b, l, vd), F32),
                   jax.ShapeDtypeStruct((b, l, 2 * qk), F32)],
        compiler_params=_cparams(("parallel", "parallel")),
        name="odd_in",
    )(x, g.reshape(1, d), shift.reshape(b, 1, d), scale.reshape(b, 1, d), w_bf, cos_t, sin_t, gw, gb)


def _rope_tables(l):
    t = jnp.arange(l)
    pos_r = (t // GRID_W).astype(F32)
    pos_c = (t % GRID_W).astype(F32)
    nf = GLA_DK // 4
    inv = jnp.power(ROPE_BASE, -jnp.arange(nf, dtype=F32) / nf)
    ar = pos_r[:, None] * inv[None, :]
    ac = pos_c[:, None] * inv[None, :]
    cos_h = jnp.concatenate([jnp.cos(ar), jnp.cos(ar), jnp.cos(ac), jnp.cos(ac)], axis=-1)
    sin_h = jnp.concatenate([-jnp.sin(ar), jnp.sin(ar), -jnp.sin(ac), jnp.sin(ac)], axis=-1)
    return jnp.tile(cos_h, (1, GLA_HEADS)), jnp.tile(sin_h, (1, GLA_HEADS))


def _gla_chunk(qk, v, g, s_ref, tri, causal_mask, blockdiag, last_row, mid_row):
    hk = GLA_HEADS * GLA_DK
    c = GLA_CHUNK
    q = qk[:, 0:hk]
    k = qk[:, hk:2 * hk]
    bc = jnp.dot(tri, g, precision=HIGHEST, preferred_element_type=F32)
    b_mid = bc[mid_row:mid_row + 1, :]
    b_last = bc[last_row:last_row + 1, :]
    qt = q * jnp.exp(bc - b_mid)
    kt = k * jnp.exp(b_mid - bc)
    qe = (qt * jnp.exp(b_mid)).astype(BF16)
    ke_t = (kt * jnp.exp(b_last - b_mid)).T.astype(BF16)
    ktb = kt.astype(BF16)
    lane = lax.broadcasted_iota(jnp.int32, (1, hk), 1)
    s_old = s_ref[...]
    inter = _dot(qe, s_old.astype(BF16))
    outs = []
    for h in range(GLA_HEADS):
        in_head = (lane >= h * GLA_DK) & (lane < (h + 1) * GLA_DK)
        qh = jnp.where(in_head, qt, 0.0).astype(BF16)
        att = jnp.where(causal_mask, _dot_nt(qh, ktb), 0.0)
        outs.append(_dot(att.astype(BF16), v[:, h * GLA_DV:(h + 1) * GLA_DV]))
    o = inter + jnp.concatenate(outs, axis=-1)
    decay_col = jnp.exp(jnp.sum(g.T, axis=1, keepdims=True))
    upd = _dot(ke_t, v)
    s_ref[...] = decay_col * s_old + jnp.where(blockdiag, upd, 0.0)
    return o


def _gla_body(qkf_ref, qkb_ref, vf_ref, vb_ref, gf_ref, gb_ref, s0f_ref, s0b_ref,
              of_ref, ob_ref, sff_ref, sbf_ref, sf_ref, sb_ref, *, tile):
    n = pl.program_id(1)
    c = GLA_CHUNK
    hk = GLA_HEADS * GLA_DK
    hv = GLA_HEADS * GLA_DV
    ri = lax.broadcasted_iota(jnp.int32, (hk, hv), 0) // GLA_DK
    ci = lax.broadcasted_iota(jnp.int32, (hk, hv), 1) // GLA_DV
    blockdiag = ri == ci

    @pl.when(n == 0)
    def _():
        sf_ref[...] = jnp.zeros((hk, hv), F32)
        sb_ref[...] = jnp.zeros((hk, hv), F32)
        for h in range(GLA_HEADS):
            sf_ref[h * GLA_DK:(h + 1) * GLA_DK, h * GLA_DV:(h + 1) * GLA_DV] = s0f_ref[0, h]
            sb_ref[h * GLA_DK:(h + 1) * GLA_DK, h * GLA_DV:(h + 1) * GLA_DV] = s0b_ref[0, h]

    ii = lax.broadcasted_iota(jnp.int32, (c, c), 0)
    jj = lax.broadcasted_iota(jnp.int32, (c, c), 1)
    lower = jj <= ii
    upper = jj >= ii
    tri_f = lower.astype(F32)
    tri_b = upper.astype(F32)
    nchunks = tile // c
    for cc in range(nchunks):
        sl = slice(cc * c, (cc + 1) * c)
        of_ref[0, sl, :] = _gla_chunk(qkf_ref[0, sl, :], vf_ref[0, sl, :], gf_ref[0, sl, :], sf_ref,
                                      tri_f, lower, blockdiag, c - 1, c // 2 - 1)
        rc = nchunks - 1 - cc
        sl = slice(rc * c, (rc + 1) * c)
        ob_ref[0, sl, :] = _gla_chunk(qkb_ref[0, sl, :], vb_ref[0, sl, :], gb_ref[0, sl, :], sb_ref,
                                      tri_b, upper, blockdiag, 0, c // 2)

    @pl.when(n == pl.num_programs(1) - 1)
    def _():
        for h in range(GLA_HEADS):
            sff_ref[0, h] = sf_ref[h * GLA_DK:(h + 1) * GLA_DK, h * GLA_DV:(h + 1) * GLA_DV]
            sbf_ref[0, h] = sb_ref[h * GLA_DK:(h + 1) * GLA_DK, h * GLA_DV:(h + 1) * GLA_DV]


def _gla(qk, v, gates, s0f, s0b, tile):
    b, l, _ = qk.shape
    hk = GLA_HEADS * GLA_DK
    hv = GLA_HEADS * GLA_DV
    nt = l // tile
    fwd = lambda w, col: pl.BlockSpec((1, tile, w), lambda i, n: (i, n, col))
    bwd = lambda w, col: pl.BlockSpec((1, tile, w), lambda i, n: (i, nt - 1 - n, col))
    st = pl.BlockSpec((1, GLA_HEADS, GLA_DK, GLA_DV), lambda i, n: (i, 0, 0, 0))
    return pl.pallas_call(
        functools.partial(_gla_body, tile=tile),
        grid=(b, nt),
        in_specs=[fwd(2 * hk, 0), bwd(2 * hk, 0), fwd(hv, 0), bwd(hv, 0), fwd(hk, 0), bwd(hk, 1), st, st],
        out_specs=[fwd(hv, 0), bwd(hv, 0), st, st],
        out_shape=[jax.ShapeDtypeStruct((b, l, hv), F32), jax.ShapeDtypeStruct((b, l, hv), F32),
                   jax.ShapeDtypeStruct((b, GLA_HEADS, GLA_DK, GLA_DV), F32),
                   jax.ShapeDtypeStruct((b, GLA_HEADS, GLA_DK, GLA_DV), F32)],
        scratch_shapes=[pltpu.VMEM((hk, hv), F32), pltpu.VMEM((hk, hv), F32)],
        compiler_params=_cparams(("parallel", "arbitrary")),
        name="gla_scan",
    )(qk, qk, v, v, gates, gates, s0f, s0b)


def _odd_mid_body(cur_ref, prev_ref, next_ref, of_ref, ob_ref, r_ref, hg_ref, pw_ref, ps_ref,
                  pool_ref, d_ref, buf_ref, *, tile, seq):
    j = pl.program_id(1)
    last = pl.num_programs(1) - 1
    hal = POOL_HALO
    buf_ref[0:hal, :] = jnp.where(j == 0, 0.0, prev_ref[0])
    buf_ref[hal:hal + tile, :] = cur_ref[0]
    buf_ref[hal + tile:hal + tile + hal, :] = jnp.where(j == last, 0.0, next_ref[0])
    t = j * tile + lax.broadcasted_iota(jnp.int32, (tile, 1), 0)
    for gi, win in enumerate(POOL_WINDOWS):
        cols = slice(gi * POOL_GROUP, (gi + 1) * POOL_GROUP)
        acc = jnp.zeros((tile, POOL_GROUP), F32)
        for off in range(-(win // 2), win - win // 2):
            acc = acc + buf_ref[hal + off:hal + off + tile, cols]
        cnt = jnp.minimum(t + (win - win // 2), seq) - jnp.maximum(t - win // 2, 0)
        diff = acc / cnt.astype(F32) - cur_ref[0, :, cols]
        pool_ref[0, :, cols] = _dot(diff.astype(BF16), pw_ref[gi]) * ps_ref[:, cols]
    for h in range(GLA_HEADS):
        cols = slice(h * GLA_DV, (h + 1) * GLA_DV)
        o = of_ref[0, :, cols] + ob_ref[0, :, cols]
        d_ref[0, :, cols] = _rms(o, hg_ref[:, cols]) * _silu(r_ref[0, :, cols])


def _odd_mid(pool_u, o_f, o_b, r, head_g, pool_w_bf, pool_scale, tile):
    b, l, c = pool_u.shape
    hal = POOL_HALO
    per = tile // hal
    nh = l // hal
    blk = pl.BlockSpec((1, tile, c), lambda i, j: (i, j, 0))
    vec = pl.BlockSpec((1, c), lambda i, j: (0, 0))
    return pl.pallas_call(
        functools.partial(_odd_mid_body, tile=tile, seq=l),
        grid=(b, l // tile),
        in_specs=[blk,
                  pl.BlockSpec((1, hal, c), lambda i, j: (i, jnp.maximum(j * per - 1, 0), 0)),
                  pl.BlockSpec((1, hal, c), lambda i, j: (i, jnp.minimum((j + 1) * per, nh - 1), 0)),
                  blk, blk, blk, vec,
                  pl.BlockSpec((len(POOL_WINDOWS), POOL_GROUP, POOL_GROUP), lambda i, j: (0, 0, 0)),
                  vec],
        out_specs=[blk, blk],
        out_shape=[jax.ShapeDtypeStruct((b, l, c), F32), jax.ShapeDtypeStruct((b, l, c), F32)],
        scratch_shapes=[pltpu.VMEM((tile + 2 * hal, c), F32)],
        compiler_params=_cparams(("parallel", "parallel")),
        name="odd_mid",
    )(pool_u, pool_u, pool_u, o_f, o_b, r, head_g.reshape(1, c), pool_w_bf, pool_scale.reshape(1, c))


def _route(aff_t, h_bf):
    b, e, n = aff_t.shape
    cap = EC_CAPACITY_FACTOR * n // N_EXPERTS
    vals, idx = lax.top_k(aff_t, cap)
    flat = idx + (jnp.arange(b, dtype=idx.dtype) * n)[:, None, None]
    flat = jnp.swapaxes(flat, 0, 1).reshape(e, b * cap)
    vals = jnp.swapaxes(vals, 0, 1).reshape(e, b * cap)
    rows = jnp.take(h_bf.reshape(b * n, -1), flat, axis=0)
    return rows, vals, flat


def _combine(y, flat, b, n):
    d = y.shape[-1]
    out = jnp.zeros((b * n, d), F32).at[flat.reshape(-1)].add(y.reshape(-1, d))
    return out.reshape(b, n, d)


def _moe(parts, w_gate, w_up, w_down):
    routed = [_route(a, h) for a, h in parts]
    xg = jnp.concatenate([r[0] for r in routed], axis=1)
    vals = jnp.concatenate([r[1] for r in routed], axis=1)
    m = xg.shape[1]
    chunk = next(c for c in (512, 528, 352, 256, 128, 64, 32, 16) if m % c == 0)
    y = _expert_ffn(xg, vals[..., None], w_gate, w_up, w_down, chunk)
    outs = []
    off = 0
    for (a, h), r in zip(parts, routed):
        cnt = r[2].shape[1]
        outs.append(_combine(y[:, off:off + cnt], r[2], h.shape[0], h.shape[1]))
        off += cnt
    return outs


def kernel(x, c, ctx, c_ctx, w_mod, b_mod, norm_g, w_in_even, w_out_even, conv_w, conv_b, conv_ln_g,
           conv_ln_b, na_rpb, w_in_odd, w_out_odd, pool_w, pool_scale, gla_gate_w, gla_gate_b, gla_head_g,
           router_w, expert_w_gate, expert_w_up, expert_w_down):
    b, l, d = x.shape
    n_ctx = ctx.shape[1]
    tm = 512

    mod_rows = jnp.concatenate([c, c_ctx[None], jnp.zeros((8 - b - 1, d), F32)], axis=0)

    def modulation(i):
        mm = _modulation(mod_rows, w_mod[i], b_mod[i])
        m = mm[:b].reshape(b, 6, d)
        mc = jnp.broadcast_to(mm[b].reshape(1, 6, d), (b, 6, d))
        return m, mc

    m, mc = modulation(0)
    g = norm_g[0]
    w_in = w_in_even[0].astype(BF16)
    w_out = w_out_even[0].astype(BF16)
    glu, qkv = _even_in(x, g[0], m[:, 0], m[:, 1], w_in, tm)
    glu_c, qkv_c = _even_in(ctx, g[0], mc[:, 0], mc[:, 1], w_in, n_ctx)
    a_lat = _conv_branch(glu, conv_w[0], conv_b[0], conv_ln_g[0], conv_ln_b[0], 256)
    a_ctx = _conv_branch(glu_c, conv_w[0], conv_b[0], conv_ln_g[0], conv_ln_b[0], n_ctx)
    na = _neighbourhood_attention(qkv, qkv_c, na_rpb[0])
    att_c = _context_attention(qkv_c)
    x, h2, aff = _out_proj(a_lat, na, x, w_out, g[1], m[:, 2], g[2], m[:, 3], m[:, 4], router_w[0], tm)
    ctx, h2c, aff_c = _out_proj(a_ctx, att_c, ctx, w_out, g[1], mc[:, 2], g[2], mc[:, 3], mc[:, 4],
                                router_w[0], n_ctx)
    f, f_c = _moe([(aff, h2), (aff_c, h2c)], expert_w_gate[0], expert_w_up[0], expert_w_down[0])
    x = _gated_residual(x, f, m[:, 5], g[3], tm)
    ctx = _gated_residual(ctx, f_c, mc[:, 5], g[3], n_ctx)

    m, mc = modulation(1)
    g = norm_g[1]
    w_in = w_in_odd[0].astype(BF16)
    w_out = w_out_odd[0].astype(BF16)
    cos_t, sin_t = _rope_tables(l)
    ones_t = jnp.ones((n_ctx, GLA_HEADS * GLA_DK), F32)
    _, qk_c, v_c, _, gate_c = _odd_in(ctx, g[0], mc[:, 0], mc[:, 1], w_in, ones_t, jnp.zeros_like(ones_t),
                                      gla_gate_w[0], gla_gate_b[0], n_ctx)
    s_zero = jnp.zeros((b, GLA_HEADS, GLA_DK, GLA_DV), F32)
    _, _, s_f, s_b = _gla(qk_c, v_c, gate_c, s_zero, s_zero, n_ctx)
    pool_u, qk, v, r, gate = _odd_in(x, g[0], m[:, 0], m[:, 1], w_in, cos_t, sin_t,
                                     gla_gate_w[0], gla_gate_b[0], tm)
    o_f, o_b, _, _ = _gla(qk, v, gate, s_f, s_b, 256)
    pool_y, d_lat = _odd_mid(pool_u, o_f, o_b, r, gla_head_g[0], pool_w[0].astype(BF16), pool_scale[0], 256)
    x, h2, aff = _out_proj(pool_y, d_lat, x, w_out, g[1], m[:, 2], g[2], m[:, 3], m[:, 4], router_w[1], tm)
    (f,) = _moe([(aff, h2)], expert_w_gate[1], expert_w_up[1], expert_w_down[1])
    return _gated_residual(x, f, m[:, 5], g[3], tm)
```

```python
import functools

import jax
import jax.numpy as jnp
from jax import lax
from jax.experimental import pallas as pl
from jax.experimental.pallas import tpu as pltpu

F32 = jnp.float32
BF16 = jnp.bfloat16
HIGHEST = lax.Precision.HIGHEST

D_MODEL = 1024
GRID_W = 64
EPS = 1e-6
CONV_CH = 512
CONV_WIDTH = 31
CONV_HALO = 16
NA_HEADS = 8
NA_HEAD_DIM = 64
NA_KR = 8
NA_KC = 16
NA_ROWS_PER_BLOCK = 4
NA_WIN_ROWS = 12
POOL_CH = 512
POOL_WINDOWS = (2, 4, 8, 16)
POOL_GROUP = 128
POOL_HALO = 8
GLA_HEADS = 4
GLA_DK = 64
GLA_DV = 128
GLA_RANK = 16
GLA_TAU = 16.0
GLA_CHUNK = 64
ROPE_BASE = 10000.0
N_EXPERTS = 16
EXPERT_FF = 2816
EC_CAPACITY_FACTOR = 2
LANES = 128
NEG_BIG = -1e30
VMEM_LIMIT = 56 * 1024 * 1024


def _cparams(sem):
    return pltpu.CompilerParams(dimension_semantics=sem, vmem_limit_bytes=VMEM_LIMIT)


def _rms(x, g):
    return x * lax.rsqrt(jnp.mean(x * x, axis=-1, keepdims=True) + EPS) * g


def _sigmoid(x):
    return 1.0 / (1.0 + jnp.exp(-x))


def _silu(x):
    return x * _sigmoid(x)


def _dot(a, b):
    return jnp.dot(a, b, preferred_element_type=F32)


def _dot_nt(a, b):
    return lax.dot_general(a, b, (((1,), (1,)), ((), ())), preferred_element_type=F32)


def _mod_body(c_ref, w_ref, b_ref, o_ref):
    o_ref[...] = jnp.dot(_silu(c_ref[...]), w_ref[...], precision=HIGHEST,
                         preferred_element_type=F32) + b_ref[...]


def _modulation(rows, w, b):
    n = w.shape[1]
    tn = 1536
    return pl.pallas_call(
        _mod_body,
        grid=(n // tn,),
        in_specs=[pl.BlockSpec((8, D_MODEL), lambda j: (0, 0)),
                  pl.BlockSpec((D_MODEL, tn), lambda j: (0, j)),
                  pl.BlockSpec((1, tn), lambda j: (0, j))],
        out_specs=pl.BlockSpec((8, tn), lambda j: (0, j)),
        out_shape=jax.ShapeDtypeStruct((8, n), F32),
        compiler_params=_cparams(("arbitrary",)),
        name="modulation",
    )(rows, w, b.reshape(1, n))


def _even_in_body(x_ref, g_ref, sh_ref, sc_ref, w_ref, glu_ref, qkv_ref):
    h = (_rms(x_ref[0], g_ref[...]) * (1.0 + sc_ref[0]) + sh_ref[0]).astype(BF16)
    c = CONV_CH
    glu_ref[0] = _dot(h, w_ref[:, 0:c]) * _sigmoid(_dot(h, w_ref[:, c:2 * c]))
    hd = NA_HEADS * NA_HEAD_DIM
    q0 = 2 * c
    qkv_ref[0, :, 0:hd] = (_dot(h, w_ref[:, q0:q0 + hd]) * (NA_HEAD_DIM ** -0.5)).astype(BF16)
    qkv_ref[0, :, hd:3 * hd] = _dot(h, w_ref[:, q0 + hd:q0 + 3 * hd]).astype(BF16)


def _even_in(x, g, shift, scale, w_bf, tm):
    b, l, d = x.shape
    n = w_bf.shape[1]
    hd3 = 3 * NA_HEADS * NA_HEAD_DIM
    vec = pl.BlockSpec((1, 1, d), lambda i, j: (i, 0, 0))
    return pl.pallas_call(
        _even_in_body,
        grid=(b, l // tm),
        in_specs=[pl.BlockSpec((1, tm, d), lambda i, j: (i, j, 0)),
                  pl.BlockSpec((1, d), lambda i, j: (0, 0)),
                  vec, vec,
                  pl.BlockSpec((d, n), lambda i, j: (0, 0))],
        out_specs=[pl.BlockSpec((1, tm, CONV_CH), lambda i, j: (i, j, 0)),
                   pl.BlockSpec((1, tm, hd3), lambda i, j: (i, j, 0))],
        out_shape=[jax.ShapeDtypeStruct((b, l, CONV_CH), F32),
                   jax.ShapeDtypeStruct((b, l, hd3), BF16)],
        compiler_params=_cparams(("parallel", "parallel")),
        name="even_in",
    )(x, g.reshape(1, d), shift.reshape(b, 1, d), scale.reshape(b, 1, d), w_bf)


def _conv_body(cur_ref, prev_ref, next_ref, w_ref, b_ref, lg_ref, lb_ref, o_ref, buf_ref, *, tile, chunk):
    j = pl.program_id(1)
    last = pl.num_programs(1) - 1
    hal = CONV_HALO
    buf_ref[0:hal, :] = jnp.where(j == 0, 0.0, prev_ref[0])
    buf_ref[hal:hal + tile, :] = cur_ref[0]
    buf_ref[hal + tile:hal + tile + hal, :] = jnp.where(j == last, 0.0, next_ref[0])
    first = hal - CONV_WIDTH // 2
    for r0 in range(0, tile, chunk):
        acc = jnp.zeros((chunk, CONV_CH), F32)
        for k in range(CONV_WIDTH):
            acc = acc + buf_ref[r0 + first + k:r0 + first + k + chunk, :] * w_ref[k:k + 1, :]
        y = acc + b_ref[...]
        mu = jnp.mean(y, axis=-1, keepdims=True)
        yc = y - mu
        var = jnp.mean(yc * yc, axis=-1, keepdims=True)
        o_ref[0, r0:r0 + chunk, :] = _silu(yc * lax.rsqrt(var + EPS) * lg_ref[...] + lb_ref[...])


def _conv_branch(glu, conv_w, conv_b, ln_g, ln_b, tile):
    b, l, c = glu.shape
    hal = CONV_HALO
    per = tile // hal
    nh = l // hal
    vec = pl.BlockSpec((1, c), lambda i, j: (0, 0))
    return pl.pallas_call(
        functools.partial(_conv_body, tile=tile, chunk=16),
        grid=(b, l // tile),
        in_specs=[pl.BlockSpec((1, tile, c), lambda i, j: (i, j, 0)),
                  pl.BlockSpec((1, hal, c), lambda i, j: (i, jnp.maximum(j * per - 1, 0), 0)),
                  pl.BlockSpec((1, hal, c), lambda i, j: (i, jnp.minimum((j + 1) * per, nh - 1), 0)),
                  pl.BlockSpec((CONV_WIDTH, c), lambda i, j: (0, 0)),
                  vec, vec, vec],
        out_specs=pl.BlockSpec((1, tile, c), lambda i, j: (i, j, 0)),
        out_shape=jax.ShapeDtypeStruct((b, l, c), F32),
        scratch_shapes=[pltpu.VMEM((tile + 2 * hal, c), F32)],
        compiler_params=_cparams(("parallel", "parallel")),
        name="conv_branch",
    )(glu, glu, glu, conv_w, conv_b.reshape(1, c), ln_g.reshape(1, c), ln_b.reshape(1, c))


def _na_window_start(j, rows):
    rb = NA_ROWS_PER_BLOCK
    return jnp.clip(j * rb - NA_KR // 2, 0, rows - NA_WIN_ROWS)


def _na_body(q_ref, k_ref, v_ref, kc_ref, vc_ref, tab_ref, o_ref, *, rows):
    j = pl.program_id(2)
    nkeys = NA_WIN_ROWS * GRID_W
    start = pl.multiple_of(_na_window_start(j, rows) * GRID_W, GRID_W)
    q = q_ref[0]
    kw = k_ref[0, pl.ds(start, nkeys), :]
    vw = v_ref[0, pl.ds(start, nkeys), :]
    kc = kc_ref[0]
    vc = vc_ref[0]
    lane = lax.broadcasted_iota(jnp.int32, (1, LANES), 1)
    out = jnp.zeros(q.shape, F32)
    for hh in range(LANES // NA_HEAD_DIM):
        in_head = (lane >= hh * NA_HEAD_DIM) & (lane < (hh + 1) * NA_HEAD_DIM)
        qh = jnp.where(in_head, q, jnp.zeros_like(q))
        s = _dot_nt(qh, kw) + tab_ref[0, hh]
        sc = _dot_nt(qh, kc)
        m = jnp.maximum(jnp.max(s, axis=-1, keepdims=True), jnp.max(sc, axis=-1, keepdims=True))
        p = jnp.exp(s - m)
        pc = jnp.exp(sc - m)
        denom = jnp.sum(p, axis=-1, keepdims=True) + jnp.sum(pc, axis=-1, keepdims=True)
        o = (_dot(p.astype(BF16), vw) + _dot(pc.astype(BF16), vc)) / denom
        out = jnp.where(in_head, o, out)
    o_ref[0] = out


def _na_tables(rpb, rows):
    rb = NA_ROWS_PER_BLOCK
    nblk = rows // rb
    wr = NA_WIN_ROWS
    qc = jnp.arange(GRID_W)
    cs = jnp.clip(qc - NA_KC // 2, 0, GRID_W - NA_KC)
    col_ok = (qc[None, :] >= cs[:, None]) & (qc[None, :] < cs[:, None] + NA_KC)
    col_off = qc[None, :] - qc[:, None] + NA_KC - 1
    onehot = (col_off[:, :, None] == jnp.arange(2 * NA_KC - 1)[None, None, :]).astype(F32)
    blocks = jnp.einsum('hrd,qkd->hrqk', rpb.astype(F32), onehot, precision=HIGHEST)
    blocks = jnp.where(col_ok[None, None], blocks, NEG_BIG)
    masked = jnp.full((NA_HEADS, GRID_W, GRID_W), NEG_BIG, F32)
    tabs = []
    for jb in (0, 1, nblk - 1):
        ws = min(max(jb * rb - NA_KR // 2, 0), rows - wr)
        q_rows = []
        for qr in range(jb * rb, (jb + 1) * rb):
            rs = min(max(qr - NA_KR // 2, 0), rows - NA_KR)
            row = [blocks[:, kr - qr + NA_KR - 1] if rs <= kr < rs + NA_KR else masked
                   for kr in range(ws, ws + wr)]
            q_rows.append(jnp.concatenate(row, axis=-1))
        tabs.append(jnp.concatenate(q_rows, axis=1))
    return jnp.stack(tabs)


def _neighbourhood_attention(qkv, qkv_ctx, rpb):
    b, l, _ = qkv.shape
    n_ctx = qkv_ctx.shape[1]
    rows = l // GRID_W
    rb = NA_ROWS_PER_BLOCK
    nblk = rows // rb
    tq = rb * GRID_W
    nkeys = NA_WIN_ROWS * GRID_W
    hp = NA_HEADS * NA_HEAD_DIM // LANES
    tabs = _na_tables(rpb, rows)

    def cls(j):
        return jnp.where(j == 0, 0, jnp.where(j == nblk - 1, 2, 1))

    return pl.pallas_call(
        functools.partial(_na_body, rows=rows),
        grid=(b, hp, nblk),
        in_specs=[pl.BlockSpec((1, tq, LANES), lambda i, h, j: (i, j, h)),
                  pl.BlockSpec((1, l, LANES), lambda i, h, j: (i, 0, hp + h)),
                  pl.BlockSpec((1, l, LANES), lambda i, h, j: (i, 0, 2 * hp + h)),
                  pl.BlockSpec((1, n_ctx, LANES), lambda i, h, j: (i, 0, hp + h)),
                  pl.BlockSpec((1, n_ctx, LANES), lambda i, h, j: (i, 0, 2 * hp + h)),
                  pl.BlockSpec((1, 2, tq, nkeys), lambda i, h, j: (cls(j), h, 0, 0))],
        out_specs=pl.BlockSpec((1, tq, LANES), lambda i, h, j: (i, j, h)),
        out_shape=jax.ShapeDtypeStruct((b, l, NA_HEADS * NA_HEAD_DIM), F32),
        compiler_params=_cparams(("parallel", "parallel", "arbitrary")),
        name="neighbourhood_attention",
    )(qkv, qkv, qkv, qkv_ctx, qkv_ctx, tabs)


def _ctx_attn_body(q_ref, k_ref, v_ref, o_ref):
    q = q_ref[0]
    k = k_ref[0]
    v = v_ref[0]
    lane = lax.broadcasted_iota(jnp.int32, (1, LANES), 1)
    out = jnp.zeros(q.shape, F32)
    for hh in range(LANES // NA_HEAD_DIM):
        in_head = (lane >= hh * NA_HEAD_DIM) & (lane < (hh + 1) * NA_HEAD_DIM)
        qh = jnp.where(in_head, q, jnp.zeros_like(q))
        s = _dot_nt(qh, k)
        p = jnp.exp(s - jnp.max(s, axis=-1, keepdims=True))
        o = _dot(p.astype(BF16), v) / jnp.sum(p, axis=-1, keepdims=True)
        out = jnp.where(in_head, o, out)
    o_ref[0] = out


def _context_attention(qkv_ctx):
    b, n, _ = qkv_ctx.shape
    hp = NA_HEADS * NA_HEAD_DIM // LANES
    return pl.pallas_call(
        _ctx_attn_body,
        grid=(b, hp),
        in_specs=[pl.BlockSpec((1, n, LANES), lambda i, h: (i, 0, h)),
                  pl.BlockSpec((1, n, LANES), lambda i, h: (i, 0, hp + h)),
                  pl.BlockSpec((1, n, LANES), lambda i, h: (i, 0, 2 * hp + h))],
        out_specs=pl.BlockSpec((1, n, LANES), lambda i, h: (i, 0, h)),
        out_shape=jax.ShapeDtypeStruct((b, n, NA_HEADS * NA_HEAD_DIM), F32),
        compiler_params=_cparams(("parallel", "parallel")),
        name="context_attention",
    )(qkv_ctx, qkv_ctx, qkv_ctx)


def _out_body(a_ref, b_ref, x_ref, w_ref, g1_ref, gate_ref, g2_ref, sh_ref, sc_ref, rw_ref,
              xo_ref, h_ref, aff_ref):
    half = a_ref.shape[-1]
    y = _dot(a_ref[0].astype(BF16), w_ref[0:half, :]) + _dot(b_ref[0].astype(BF16), w_ref[half:2 * half, :])
    xn = x_ref[0] + gate_ref[0] * _rms(y, g1_ref[...])
    xo_ref[0] = xn
    h = _rms(xn, g2_ref[...]) * (1.0 + sc_ref[0]) + sh_ref[0]
    h_ref[0] = h.astype(BF16)
    logits = jnp.dot(h, rw_ref[...], precision=HIGHEST, preferred_element_type=F32)
    lane = lax.broadcasted_iota(jnp.int32, (1, LANES), 1)
    logits = jnp.where(lane < N_EXPERTS, logits, NEG_BIG)
    e = jnp.exp(logits - jnp.max(logits, axis=-1, keepdims=True))
    aff = e / jnp.sum(e, axis=-1, keepdims=True)
    aff_ref[0] = aff.T[0:N_EXPERTS, :]


def _out_proj(a, b2, x, w_bf, g1, gate, g2, shift, scale, router_w, tm):
    b, l, d = x.shape
    half = a.shape[-1]
    rw = jnp.pad(router_w, ((0, 0), (0, LANES - N_EXPERTS)))
    vec = pl.BlockSpec((1, d), lambda i, j: (0, 0))
    bvec = pl.BlockSpec((1, 1, d), lambda i, j: (i, 0, 0))
    return pl.pallas_call(
        _out_body,
        grid=(b, l // tm),
        in_specs=[pl.BlockSpec((1, tm, half), lambda i, j: (i, j, 0)),
                  pl.BlockSpec((1, tm, half), lambda i, j: (i, j, 0)),
                  pl.BlockSpec((1, tm, d), lambda i, j: (i, j, 0)),
                  pl.BlockSpec((2 * half, d), lambda i, j: (0, 0)),
                  vec, bvec, vec, bvec, bvec,
                  pl.BlockSpec((d, LANES), lambda i, j: (0, 0))],
        out_specs=[pl.BlockSpec((1, tm, d), lambda i, j: (i, j, 0)),
                   pl.BlockSpec((1, tm, d), lambda i, j: (i, j, 0)),
                   pl.BlockSpec((1, N_EXPERTS, tm), lambda i, j: (i, 0, j))],
        out_shape=[jax.ShapeDtypeStruct((b, l, d), F32),
                   jax.ShapeDtypeStruct((b, l, d), BF16),
                   jax.ShapeDtypeStruct((b, N_EXPERTS, l), F32)],
        compiler_params=_cparams(("parallel", "parallel")),
        name="out_proj",
    )(a, b2, x, w_bf, g1.reshape(1, d), gate.reshape(b, 1, d), g2.reshape(1, d),
      shift.reshape(b, 1, d), scale.reshape(b, 1, d), rw)


def _moe_body(x_ref, val_ref, wg_ref, wu_ref, wd_ref, o_ref, wg_s, wu_s, wd_s, *, chunk):
    f = pl.program_id(1)
    wg_s[...] = wg_ref[0].astype(BF16)
    wu_s[...] = wu_ref[0].astype(BF16)
    wd_s[...] = wd_ref[0].astype(BF16)
    m = x_ref.shape[1]

    def rows(c, carry):
        r = pl.multiple_of(c * chunk, chunk)
        xs = x_ref[0, pl.ds(r, chunk), :]
        hid = (_silu(_dot(xs, wg_s[...])) * _dot(xs, wu_s[...])).astype(BF16)
        y = _dot(hid, wd_s[...])

        @pl.when(f == 0)
        def _():
            o_ref[0, pl.ds(r, chunk), :] = y

        @pl.when(f > 0)
        def _():
            o_ref[0, pl.ds(r, chunk), :] += y

        return carry

    lax.fori_loop(0, m // chunk, rows, 0)

    @pl.when(f == pl.num_programs(1) - 1)
    def _():
        o_ref[0] = o_ref[0] * val_ref[0]


def _expert_ffn(xg, vals, w_gate, w_up, w_down, chunk):
    e, m, d = xg.shape
    ff = w_gate.shape[-1]
    tf = 256
    return pl.pallas_call(
        functools.partial(_moe_body, chunk=chunk),
        grid=(e, ff // tf),
        in_specs=[pl.BlockSpec((1, m, d), lambda i, f: (i, 0, 0)),
                  pl.BlockSpec((1, m, 1), lambda i, f: (i, 0, 0)),
                  pl.BlockSpec((1, d, tf), lambda i, f: (i, 0, f)),
                  pl.BlockSpec((1, d, tf), lambda i, f: (i, 0, f)),
                  pl.BlockSpec((1, tf, d), lambda i, f: (i, f, 0))],
        out_specs=pl.BlockSpec((1, m, d), lambda i, f: (i, 0, 0)),
        out_shape=jax.ShapeDtypeStruct((e, m, d), F32),
        scratch_shapes=[pltpu.VMEM((d, tf), BF16), pltpu.VMEM((d, tf), BF16), pltpu.VMEM((tf, d), BF16)],
        compiler_params=_cparams(("parallel", "arbitrary")),
        name="expert_ffn",
    )(xg, vals, w_gate, w_up, w_down)


def _resid_body(x_ref, f_ref, gate_ref, g_ref, o_ref):
    o_ref[0] = x_ref[0] + gate_ref[0] * _rms(f_ref[0], g_ref[...])


def _gated_residual(x, f, gate, g, tm):
    b, l, d = x.shape
    blk = pl.BlockSpec((1, tm, d), lambda i, j: (i, j, 0))
    return pl.pallas_call(
        _resid_body,
        grid=(b, l // tm),
        in_specs=[blk, blk, pl.BlockSpec((1, 1, d), lambda i, j: (i, 0, 0)),
                  pl.BlockSpec((1, d), lambda i, j: (0, 0))],
        out_specs=blk,
        out_shape=jax.ShapeDtypeStruct((b, l, d), F32),
        compiler_params=_cparams(("parallel", "parallel")),
        name="gated_residual",
    )(x, f, gate.reshape(b, 1, d), g.reshape(1, d))


def _swap_pairs(x):
    nf = GLA_DK // 4
    lane = lax.broadcasted_iota(jnp.int32, (1, LANES), 1)
    up = pltpu.roll(x, LANES - nf, 1)
    down = pltpu.roll(x, nf, 1)
    return jnp.where(lane % (2 * nf) < nf, up, down)


def _odd_in_body(x_ref, g_ref, sh_ref, sc_ref, w_ref, cos_ref, sin_ref, gw_ref, gb_ref,
                 pool_ref, qk_ref, v_ref, r_ref, gate_ref):
    h = (_rms(x_ref[0], g_ref[...]) * (1.0 + sc_ref[0]) + sh_ref[0]).astype(BF16)
    qk = GLA_HEADS * GLA_DK
    vd = GLA_HEADS * GLA_DV
    q0 = POOL_CH
    v0 = q0 + 2 * qk
    r0 = v0 + vd
    l0 = r0 + vd
    pool_ref[0] = _dot(h, w_ref[:, 0:q0])
    for s in range(2 * qk // LANES):
        raw = _dot(h, w_ref[:, q0 + s * LANES:q0 + (s + 1) * LANES])
        c = cos_ref[:, (s * LANES) % qk:(s * LANES) % qk + LANES]
        sn = sin_ref[:, (s * LANES) % qk:(s * LANES) % qk + LANES]
        rot = raw * c + _swap_pairs(raw) * sn
        if s * LANES < qk:
            rot = rot * (GLA_DK ** -0.5)
        qk_ref[0, :, s * LANES:(s + 1) * LANES] = rot
    v_ref[0] = _dot(h, w_ref[:, v0:r0]).astype(BF16)
    r_ref[0] = _dot(h, w_ref[:, r0:l0])
    lr = _dot(h, w_ref[:, l0:l0 + 2 * GLA_RANK])
    z = jnp.dot(lr, gw_ref[...], precision=HIGHEST, preferred_element_type=F32) + gb_ref[...]
    gate_ref[0] = (jnp.minimum(z, 0.0) - jnp.log1p(jnp.exp(-jnp.abs(z)))) * (1.0 / GLA_TAU)


def _odd_in(x, g, shift, scale, w_bf, cos_t, sin_t, gate_w, gate_b, tm):
    b, l, d = x.shape
    n = w_bf.shape[1]
    qk = GLA_HEADS * GLA_DK
    vd = GLA_HEADS * GLA_DV
    gw = jnp.zeros((2 * GLA_RANK, 2 * qk), F32)
    gw = gw.at[:GLA_RANK, :qk].set(gate_w[0]).at[GLA_RANK:, qk:].set(gate_w[1])
    gb = jnp.concatenate([gate_b[0], gate_b[1]]).reshape(1, 2 * qk)
    vec = pl.BlockSpec((1, 1, d), lambda i, j: (i, 0, 0))
    row = lambda w: pl.BlockSpec((1, tm, w), lambda i, j: (i, j, 0))
    return pl.pallas_call(
        _odd_in_body,
        grid=(b, l // tm),
        in_specs=[row(d), pl.BlockSpec((1, d), lambda i, j: (0, 0)), vec, vec,
                  pl.BlockSpec((d, n), lambda i, j: (0, 0)),
                  pl.BlockSpec((tm, qk), lambda i, j: (j, 0)),
                  pl.BlockSpec((tm, qk), lambda i, j: (j, 0)),
                  pl.BlockSpec((2 * GLA_RANK, 2 * qk), lambda i, j: (0, 0)),
                  pl.BlockSpec((1, 2 * qk), lambda i, j: (0, 0))],
        out_specs=[row(POOL_CH), row(2 * qk), row(vd), row(vd), row(2 * qk)],
        out_shape=[jax.ShapeDtypeStruct((b, l, POOL_CH), F32),
                   jax.ShapeDtypeStruct((b, l, 2 * qk), F32),
                   jax.ShapeDtypeStruct((b, l, vd), BF16),
                   jax.ShapeDtypeStruct((b, l, vd), F32),
                   jax.ShapeDtypeStruct((b, l, 2 * qk), F32)],
        compiler_params=_cparams(("parallel", "parallel")),
        name="odd_in",
    )(x, g.reshape(1, d), shift.reshape(b, 1, d), scale.reshape(b, 1, d), w_bf, cos_t, sin_t, gw, gb)


def _rope_tables(l):
    t = jnp.arange(l)
    pos_r = (t // GRID_W).astype(F32)
    pos_c = (t % GRID_W).astype(F32)
    nf = GLA_DK // 4
    inv = jnp.power(ROPE_BASE, -jnp.arange(nf, dtype=F32) / nf)
    ar = pos_r[:, None] * inv[None, :]
    ac = pos_c[:, None] * inv[None, :]
    cos_h = jnp.concatenate([jnp.cos(ar), jnp.cos(ar), jnp.cos(ac), jnp.cos(ac)], axis=-1)
    sin_h = jnp.concatenate([-jnp.sin(ar), jnp.sin(ar), -jnp.sin(ac), jnp.sin(ac)], axis=-1)
    return jnp.tile(cos_h, (1, GLA_HEADS)), jnp.tile(sin_h, (1, GLA_HEADS))


def _gla_chunk(qk, v, g, s_ref, tri, causal_mask, blockdiag, last_row, mid_row):
    hk = GLA_HEADS * GLA_DK
    c = GLA_CHUNK
    q = qk[:, 0:hk]
    k = qk[:, hk:2 * hk]
    bc = jnp.dot(tri, g, precision=HIGHEST, preferred_element_type=F32)
    b_mid = bc[mid_row:mid_row + 1, :]
    b_last = bc[last_row:last_row + 1, :]
    qt = q * jnp.exp(bc - b_mid)
    kt = k * jnp.exp(b_mid - bc)
    qe = (qt * jnp.exp(b_mid)).astype(BF16)
    ke_t = (kt * jnp.exp(b_last - b_mid)).T.astype(BF16)
    ktb = kt.astype(BF16)
    lane = lax.broadcasted_iota(jnp.int32, (1, hk), 1)
    s_old = s_ref[...]
    inter = _dot(qe, s_old.astype(BF16))
    outs = []
    for h in range(GLA_HEADS):
        in_head = (lane >= h * GLA_DK) & (lane < (h + 1) * GLA_DK)
        qh = jnp.where(in_head, qt, 0.0).astype(BF16)
        att = jnp.where(causal_mask, _dot_nt(qh, ktb), 0.0)
        outs.append(_dot(att.astype(BF16), v[:, h * GLA_DV:(h + 1) * GLA_DV]))
    o = inter + jnp.concatenate(outs, axis=-1)
    decay_col = jnp.exp(jnp.sum(g.T, axis=1, keepdims=True))
    upd = _dot(ke_t, v)
    s_ref[...] = decay_col * s_old + jnp.where(blockdiag, upd, 0.0)
    return o


def _gla_body(qkf_ref, qkb_ref, vf_ref, vb_ref, gf_ref, gb_ref, s0f_ref, s0b_ref,
              of_ref, ob_ref, sff_ref, sbf_ref, sf_ref, sb_ref, *, tile):
    n = pl.program_id(1)
    c = GLA_CHUNK
    hk = GLA_HEADS * GLA_DK
    hv = GLA_HEADS * GLA_DV
    ri = lax.broadcasted_iota(jnp.int32, (hk, hv), 0) // GLA_DK
    ci = lax.broadcasted_iota(jnp.int32, (hk, hv), 1) // GLA_DV
    blockdiag = ri == ci

    @pl.when(n == 0)
    def _():
        sf_ref[...] = jnp.zeros((hk, hv), F32)
        sb_ref[...] = jnp.zeros((hk, hv), F32)
        for h in range(GLA_HEADS):
            sf_ref[h * GLA_DK:(h + 1) * GLA_DK, h * GLA_DV:(h + 1) * GLA_DV] = s0f_ref[0, h]
            sb_ref[h * GLA_DK:(h + 1) * GLA_DK, h * GLA_DV:(h + 1) * GLA_DV] = s0b_ref[0, h]

    ii = lax.broadcasted_iota(jnp.int32, (c, c), 0)
    jj = lax.broadcasted_iota(jnp.int32, (c, c), 1)
    lower = jj <= ii
    upper = jj >= ii
    tri_f = lower.astype(F32)
    tri_b = upper.astype(F32)
    nchunks = tile // c
    for cc in range(nchunks):
        sl = slice(cc * c, (cc + 1) * c)
        of_ref[0, sl, :] = _gla_chunk(qkf_ref[0, sl, :], vf_ref[0, sl, :], gf_ref[0, sl, :], sf_ref,
                                      tri_f, lower, blockdiag, c - 1, c // 2 - 1)
        rc = nchunks - 1 - cc
        sl = slice(rc * c, (rc + 1) * c)
        ob_ref[0, sl, :] = _gla_chunk(qkb_ref[0, sl, :], vb_ref[0, sl, :], gb_ref[0, sl, :], sb_ref,
                                      tri_b, upper, blockdiag, 0, c // 2)

    @pl.when(n == pl.num_programs(1) - 1)
    def _():
        for h in range(GLA_HEADS):
            sff_ref[0, h] = sf_ref[h * GLA_DK:(h + 1) * GLA_DK, h * GLA_DV:(h + 1) * GLA_DV]
            sbf_ref[0, h] = sb_ref[h * GLA_DK:(h + 1) * GLA_DK, h * GLA_DV:(h + 1) * GLA_DV]


def _gla(qk, v, gates, s0f, s0b, tile):
    b, l, _ = qk.shape
    hk = GLA_HEADS * GLA_DK
    hv = GLA_HEADS * GLA_DV
    nt = l // tile
    fwd = lambda w, col: pl.BlockSpec((1, tile, w), lambda i, n: (i, n, col))
    bwd = lambda w, col: pl.BlockSpec((1, tile, w), lambda i, n: (i, nt - 1 - n, col))
    st = pl.BlockSpec((1, GLA_HEADS, GLA_DK, GLA_DV), lambda i, n: (i, 0, 0, 0))
    return pl.pallas_call(
        functools.partial(_gla_body, tile=tile),
        grid=(b, nt),
        in_specs=[fwd(2 * hk, 0), bwd(2 * hk, 0), fwd(hv, 0), bwd(hv, 0), fwd(hk, 0), bwd(hk, 1), st, st],
        out_specs=[fwd(hv, 0), bwd(hv, 0), st, st],
        out_shape=[jax.ShapeDtypeStruct((b, l, hv), F32), jax.ShapeDtypeStruct((b, l, hv), F32),
                   jax.ShapeDtypeStruct((b, GLA_HEADS, GLA_DK, GLA_DV), F32),
                   jax.ShapeDtypeStruct((b, GLA_HEADS, GLA_DK, GLA_DV), F32)],
        scratch_shapes=[pltpu.VMEM((hk, hv), F32), pltpu.VMEM((hk, hv), F32)],
        compiler_params=_cparams(("parallel", "arbitrary")),
        name="gla_scan",
    )(qk, qk, v, v, gates, gates, s0f, s0b)


def _odd_mid_body(cur_ref, prev_ref, next_ref, of_ref, ob_ref, r_ref, hg_ref, pw_ref, ps_ref,
                  pool_ref, d_ref, buf_ref, *, tile, seq):
    j = pl.program_id(1)
    last = pl.num_programs(1) - 1
    hal = POOL_HALO
    buf_ref[0:hal, :] = jnp.where(j == 0, 0.0, prev_ref[0])
    buf_ref[hal:hal + tile, :] = cur_ref[0]
    buf_ref[hal + tile:hal + tile + hal, :] = jnp.where(j == last, 0.0, next_ref[0])
    t = j * tile + lax.broadcasted_iota(jnp.int32, (tile, 1), 0)
    for gi, win in enumerate(POOL_WINDOWS):
        cols = slice(gi * POOL_GROUP, (gi + 1) * POOL_GROUP)
        acc = jnp.zeros((tile, POOL_GROUP), F32)
        for off in range(-(win // 2), win - win // 2):
            acc = acc + buf_ref[hal + off:hal + off + tile, cols]
        cnt = jnp.minimum(t + (win - win // 2), seq) - jnp.maximum(t - win // 2, 0)
        diff = acc / cnt.astype(F32) - cur_ref[0, :, cols]
        pool_ref[0, :, cols] = _dot(diff.astype(BF16), pw_ref[gi]) * ps_ref[:, cols]
    for h in range(GLA_HEADS):
        cols = slice(h * GLA_DV, (h + 1) * GLA_DV)
        o = of_ref[0, :, cols] + ob_ref[0, :, cols]
        d_ref[0, :, cols] = _rms(o, hg_ref[:, cols]) * _silu(r_ref[0, :, cols])


def _odd_mid(pool_u, o_f, o_b, r, head_g, pool_w_bf, pool_scale, tile):
    b, l, c = pool_u.shape
    hal = POOL_HALO
    per = tile // hal
    nh = l // hal
    blk = pl.BlockSpec((1, tile, c), lambda i, j: (i, j, 0))
    vec = pl.BlockSpec((1, c), lambda i, j: (0, 0))
    return pl.pallas_call(
        functools.partial(_odd_mid_body, tile=tile, seq=l),
        grid=(b, l // tile),
        in_specs=[blk,
                  pl.BlockSpec((1, hal, c), lambda i, j: (i, jnp.maximum(j * per - 1, 0), 0)),
                  pl.BlockSpec((1, hal, c), lambda i, j: (i, jnp.minimum((j + 1) * per, nh - 1), 0)),
                  blk, blk, blk, vec,
                  pl.BlockSpec((len(POOL_WINDOWS), POOL_GROUP, POOL_GROUP), lambda i, j: (0, 0, 0)),
                  vec],
        out_specs=[blk, blk],
        out_shape=[jax.ShapeDtypeStruct((b, l, c), F32), jax.ShapeDtypeStruct((b, l, c), F32)],
        scratch_shapes=[pltpu.VMEM((tile + 2 * hal, c), F32)],
        compiler_params=_cparams(("parallel", "parallel")),
        name="odd_mid",
    )(pool_u, pool_u, pool_u, o_f, o_b, r, head_g.reshape(1, c), pool_w_bf, pool_scale.reshape(1, c))


def _route(aff_t, base):
    b, e, n = aff_t.shape
    cap = EC_CAPACITY_FACTOR * n // N_EXPERTS
    vals, idx = lax.top_k(aff_t, cap)
    flat = idx + (base + jnp.arange(b, dtype=idx.dtype) * n)[:, None, None]
    flat = jnp.swapaxes(flat, 0, 1).reshape(e, b * cap)
    vals = jnp.swapaxes(vals, 0, 1).reshape(e, b * cap)
    return vals, flat


def _moe(parts, w_gate, w_up, w_down):
    d = parts[0][1].shape[-1]
    sizes = [h.shape[0] * h.shape[1] for _, h in parts]
    bases = [sum(sizes[:i]) for i in range(len(parts))]
    routed = [_route(a, base) for (a, _), base in zip(parts, bases)]
    src = jnp.concatenate([h.reshape(-1, d) for _, h in parts], axis=0)
    vals = jnp.concatenate([r[0] for r in routed], axis=1)
    flat = jnp.concatenate([r[1] for r in routed], axis=1)
    xg = jnp.take(src, flat, axis=0)
    m = xg.shape[1]
    chunk = next(c for c in (512, 528, 352, 256, 128, 64, 32, 16) if m % c == 0)
    y = _expert_ffn(xg, vals[..., None], w_gate, w_up, w_down, chunk)
    out = jnp.zeros((sum(sizes), d), F32).at[flat.reshape(-1)].add(y.reshape(-1, d))
    return [out[base:base + size].reshape(h.shape) for (_, h), base, size in zip(parts, bases, sizes)]


def kernel(x, c, ctx, c_ctx, w_mod, b_mod, norm_g, w_in_even, w_out_even, conv_w, conv_b, conv_ln_g,
           conv_ln_b, na_rpb, w_in_odd, w_out_odd, pool_w, pool_scale, gla_gate_w, gla_gate_b, gla_head_g,
           router_w, expert_w_gate, expert_w_up, expert_w_down):
    b, l, d = x.shape
    n_ctx = ctx.shape[1]
    tm = 512

    mod_rows = jnp.concatenate([c, c_ctx[None], jnp.zeros((8 - b - 1, d), F32)], axis=0)

    def modulation(i):
        mm = _modulation(mod_rows, w_mod[i], b_mod[i])
        m = mm[:b].reshape(b, 6, d)
        mc = jnp.broadcast_to(mm[b].reshape(1, 6, d), (b, 6, d))
        return m, mc

    m, mc = modulation(0)
    g = norm_g[0]
    w_in = w_in_even[0].astype(BF16)
    w_out = w_out_even[0].astype(BF16)
    glu, qkv = _even_in(x, g[0], m[:, 0], m[:, 1], w_in, tm)
    glu_c, qkv_c = _even_in(ctx, g[0], mc[:, 0], mc[:, 1], w_in, n_ctx)
    a_lat = _conv_branch(glu, conv_w[0], conv_b[0], conv_ln_g[0], conv_ln_b[0], 256)
    a_ctx = _conv_branch(glu_c, conv_w[0], conv_b[0], conv_ln_g[0], conv_ln_b[0], n_ctx)
    na = _neighbourhood_attention(qkv, qkv_c, na_rpb[0])
    att_c = _context_attention(qkv_c)
    x, h2, aff = _out_proj(a_lat, na, x, w_out, g[1], m[:, 2], g[2], m[:, 3], m[:, 4], router_w[0], tm)
    ctx, h2c, aff_c = _out_proj(a_ctx, att_c, ctx, w_out, g[1], mc[:, 2], g[2], mc[:, 3], mc[:, 4],
                                router_w[0], n_ctx)
    f, f_c = _moe([(aff, h2), (aff_c, h2c)], expert_w_gate[0], expert_w_up[0], expert_w_down[0])
    x = _gated_residual(x, f, m[:, 5], g[3], tm)
    ctx = _gated_residual(ctx, f_c, mc[:, 5], g[3], n_ctx)

    m, mc = modulation(1)
    g = norm_g[1]
    w_in = w_in_odd[0].astype(BF16)
    w_out = w_out_odd[0].astype(BF16)
    cos_t, sin_t = _rope_tables(l)
    ones_t = jnp.ones((n_ctx, GLA_HEADS * GLA_DK), F32)
    _, qk_c, v_c, _, gate_c = _odd_in(ctx, g[0], mc[:, 0], mc[:, 1], w_in, ones_t, jnp.zeros_like(ones_t),
                                      gla_gate_w[0], gla_gate_b[0], n_ctx)
    s_zero = jnp.zeros((b, GLA_HEADS, GLA_DK, GLA_DV), F32)
    _, _, s_f, s_b = _gla(qk_c, v_c, gate_c, s_zero, s_zero, n_ctx)
    pool_u, qk, v, r, gate = _odd_in(x, g[0], m[:, 0], m[:, 1], w_in, cos_t, sin_t,
                                     gla_gate_w[0], gla_gate_b[0], tm)
    o_f, o_b, _, _ = _gla(qk, v, gate, s_f, s_b, 256)
    pool_y, d_lat = _odd_mid(pool_u, o_f, o_b, r, gla_head_g[0], pool_w[0].astype(BF16), pool_scale[0], 256)
    x, h2, aff = _out_proj(pool_y, d_lat, x, w_out, g[1], m[:, 2], g[2], m[:, 3], m[:, 4], router_w[1], tm)
    (f,) = _moe([(aff, h2)], expert_w_gate[1], expert_w_up[1], expert_w_down[1])
    return _gated_residual(x, f, m[:, 5], g[3], tm)
```

```python
import functools

import jax
import jax.numpy as jnp
from jax import lax
from jax.experimental import pallas as pl
from jax.experimental.pallas import tpu as pltpu
from jax.experimental.pallas import tpu_sc as plsc

F32 = jnp.float32
BF16 = jnp.bfloat16
HIGHEST = lax.Precision.HIGHEST

D_MODEL = 1024
GRID_W = 64
EPS = 1e-6
CONV_CH = 512
CONV_WIDTH = 31
CONV_HALO = 16
NA_HEADS = 8
NA_HEAD_DIM = 64
NA_KR = 8
NA_KC = 16
NA_ROWS_PER_BLOCK = 4
NA_WIN_ROWS = 12
POOL_CH = 512
POOL_WINDOWS = (2, 4, 8, 16)
POOL_GROUP = 128
POOL_HALO = 8
GLA_HEADS = 4
GLA_DK = 64
GLA_DV = 128
GLA_RANK = 16
GLA_TAU = 16.0
GLA_CHUNK = 64
ROPE_BASE = 10000.0
N_EXPERTS = 16
EXPERT_FF = 2816
EC_CAPACITY_FACTOR = 2
LANES = 128
NEG_BIG = -1e30
VMEM_LIMIT = 56 * 1024 * 1024
SC_CORES = 2
SC_SUBCORES = 16
SC_TILE_VMEM_BUDGET = 400 * 1024
SC_GATHER_WINDOW = 128


def _cparams(sem):
    return pltpu.CompilerParams(dimension_semantics=sem, vmem_limit_bytes=VMEM_LIMIT)


def _rms(x, g):
    return x * lax.rsqrt(jnp.mean(x * x, axis=-1, keepdims=True) + EPS) * g


def _sigmoid(x):
    return 1.0 / (1.0 + jnp.exp(-x))


def _silu(x):
    return x * _sigmoid(x)


def _dot(a, b):
    return jnp.dot(a, b, preferred_element_type=F32)


def _pack_bf16_pairs(h):
    half = h.shape[-1] // 2
    bits = lax.bitcast_convert_type(h.astype(BF16).astype(F32), jnp.uint32)
    packed = (bits[:, half:] & jnp.uint32(0xFFFF0000)) | (bits[:, :half] >> 16)
    return lax.bitcast_convert_type(packed, jnp.int32)


def _unpack_bf16_pairs(p):
    bits = lax.bitcast_convert_type(p, jnp.uint32)
    lo = lax.bitcast_convert_type(bits << 16, F32).astype(BF16)
    hi = lax.bitcast_convert_type(bits & jnp.uint32(0xFFFF0000), F32).astype(BF16)
    return lo, hi


def _dot_nt(a, b):
    return lax.dot_general(a, b, (((1,), (1,)), ((), ())), preferred_element_type=F32)


def _mod_body(c_ref, w_ref, b_ref, o_ref):
    o_ref[0] = jnp.dot(_silu(c_ref[...]), w_ref[0], precision=HIGHEST,
                       preferred_element_type=F32) + b_ref[0]


def _modulation(rows, w, b):
    depth, _, n = w.shape
    tn = 1536
    return pl.pallas_call(
        _mod_body,
        grid=(depth, n // tn),
        in_specs=[pl.BlockSpec((8, D_MODEL), lambda i, j: (0, 0)),
                  pl.BlockSpec((1, D_MODEL, tn), lambda i, j: (i, 0, j)),
                  pl.BlockSpec((1, 1, tn), lambda i, j: (i, 0, j))],
        out_specs=pl.BlockSpec((1, 8, tn), lambda i, j: (i, 0, j)),
        out_shape=jax.ShapeDtypeStruct((depth, 8, n), F32),
        compiler_params=_cparams(("parallel", "parallel")),
        name="modulation",
    )(rows, w, b.reshape(depth, 1, n))


def _even_in_body(x_ref, g_ref, sh_ref, sc_ref, w_ref, glu_ref, qkv_ref):
    h = (_rms(x_ref[0], g_ref[...]) * (1.0 + sc_ref[0]) + sh_ref[0]).astype(BF16)
    c = CONV_CH
    glu_ref[0] = _dot(h, w_ref[:, 0:c]) * _sigmoid(_dot(h, w_ref[:, c:2 * c]))
    hd = NA_HEADS * NA_HEAD_DIM
    q0 = 2 * c
    qkv_ref[0, :, 0:hd] = (_dot(h, w_ref[:, q0:q0 + hd]) * (NA_HEAD_DIM ** -0.5)).astype(BF16)
    qkv_ref[0, :, hd:3 * hd] = _dot(h, w_ref[:, q0 + hd:q0 + 3 * hd]).astype(BF16)


def _even_in(x, g, shift, scale, w_bf, tm):
    b, l, d = x.shape
    n = w_bf.shape[1]
    hd3 = 3 * NA_HEADS * NA_HEAD_DIM
    vec = pl.BlockSpec((1, 1, d), lambda i, j: (i, 0, 0))
    return pl.pallas_call(
        _even_in_body,
        grid=(b, l // tm),
        in_specs=[pl.BlockSpec((1, tm, d), lambda i, j: (i, j, 0)),
                  pl.BlockSpec((1, d), lambda i, j: (0, 0)),
                  vec, vec,
                  pl.BlockSpec((d, n), lambda i, j: (0, 0))],
        out_specs=[pl.BlockSpec((1, tm, CONV_CH), lambda i, j: (i, j, 0)),
                   pl.BlockSpec((1, tm, hd3), lambda i, j: (i, j, 0))],
        out_shape=[jax.ShapeDtypeStruct((b, l, CONV_CH), F32),
                   jax.ShapeDtypeStruct((b, l, hd3), BF16)],
        compiler_params=_cparams(("parallel", "parallel")),
        name="even_in",
    )(x, g.reshape(1, d), shift.reshape(b, 1, d), scale.reshape(b, 1, d), w_bf)


def _conv_body(cur_ref, prev_ref, next_ref, w_ref, b_ref, lg_ref, lb_ref, o_ref, buf_ref, *, tile, chunk):
    j = pl.program_id(1)
    last = pl.num_programs(1) - 1
    hal = CONV_HALO
    buf_ref[0:hal, :] = jnp.where(j == 0, 0.0, prev_ref[0])
    buf_ref[hal:hal + tile, :] = cur_ref[0]
    buf_ref[hal + tile:hal + tile + hal, :] = jnp.where(j == last, 0.0, next_ref[0])
    first = hal - CONV_WIDTH // 2
    for r0 in range(0, tile, chunk):
        acc = jnp.zeros((chunk, CONV_CH), F32)
        for k in range(CONV_WIDTH):
            acc = acc + buf_ref[r0 + first + k:r0 + first + k + chunk, :] * w_ref[k:k + 1, :]
        y = acc + b_ref[...]
        mu = jnp.mean(y, axis=-1, keepdims=True)
        yc = y - mu
        var = jnp.mean(yc * yc, axis=-1, keepdims=True)
        o_ref[0, r0:r0 + chunk, :] = _silu(yc * lax.rsqrt(var + EPS) * lg_ref[...] + lb_ref[...])


def _conv_branch(glu, conv_w, conv_b, ln_g, ln_b, tile):
    b, l, c = glu.shape
    hal = CONV_HALO
    per = tile // hal
    nh = l // hal
    vec = pl.BlockSpec((1, c), lambda i, j: (0, 0))
    return pl.pallas_call(
        functools.partial(_conv_body, tile=tile, chunk=16),
        grid=(b, l // tile),
        in_specs=[pl.BlockSpec((1, tile, c), lambda i, j: (i, j, 0)),
                  pl.BlockSpec((1, hal, c), lambda i, j: (i, jnp.maximum(j * per - 1, 0), 0)),
                  pl.BlockSpec((1, hal, c), lambda i, j: (i, jnp.minimum((j + 1) * per, nh - 1), 0)),
                  pl.BlockSpec((CONV_WIDTH, c), lambda i, j: (0, 0)),
                  vec, vec, vec],
        out_specs=pl.BlockSpec((1, tile, c), lambda i, j: (i, j, 0)),
        out_shape=jax.ShapeDtypeStruct((b, l, c), F32),
        scratch_shapes=[pltpu.VMEM((tile + 2 * hal, c), F32)],
        compiler_params=_cparams(("parallel", "parallel")),
        name="conv_branch",
    )(glu, glu, glu, conv_w, conv_b.reshape(1, c), ln_g.reshape(1, c), ln_b.reshape(1, c))


def _na_window_start(j, rows):
    rb = NA_ROWS_PER_BLOCK
    return jnp.clip(j * rb - NA_KR // 2, 0, rows - NA_WIN_ROWS)


def _na_body(q_ref, k_ref, v_ref, kc_ref, vc_ref, tab_ref, o_ref, *, rows):
    j = pl.program_id(2)
    nkeys = NA_WIN_ROWS * GRID_W
    start = pl.multiple_of(_na_window_start(j, rows) * GRID_W, GRID_W)
    q = q_ref[0]
    kw = k_ref[0, pl.ds(start, nkeys), :]
    vw = v_ref[0, pl.ds(start, nkeys), :]
    kc = kc_ref[0]
    vc = vc_ref[0]
    lane = lax.broadcasted_iota(jnp.int32, (1, LANES), 1)
    out = jnp.zeros(q.shape, F32)
    for hh in range(LANES // NA_HEAD_DIM):
        in_head = (lane >= hh * NA_HEAD_DIM) & (lane < (hh + 1) * NA_HEAD_DIM)
        qh = jnp.where(in_head, q, jnp.zeros_like(q))
        s = _dot_nt(qh, kw) + tab_ref[0, hh]
        sc = _dot_nt(qh, kc)
        m = jnp.maximum(jnp.max(s, axis=-1, keepdims=True), jnp.max(sc, axis=-1, keepdims=True))
        p = jnp.exp(s - m)
        pc = jnp.exp(sc - m)
        denom = jnp.sum(p, axis=-1, keepdims=True) + jnp.sum(pc, axis=-1, keepdims=True)
        o = (_dot(p.astype(BF16), vw) + _dot(pc.astype(BF16), vc)) / denom
        out = jnp.where(in_head, o, out)
    o_ref[0] = out


def _na_tables(rpb, rows):
    rb = NA_ROWS_PER_BLOCK
    nblk = rows // rb
    wr = NA_WIN_ROWS
    qc = jnp.arange(GRID_W)
    cs = jnp.clip(qc - NA_KC // 2, 0, GRID_W - NA_KC)
    col_ok = (qc[None, :] >= cs[:, None]) & (qc[None, :] < cs[:, None] + NA_KC)
    col_off = qc[None, :] - qc[:, None] + NA_KC - 1
    onehot = (col_off[:, :, None] == jnp.arange(2 * NA_KC - 1)[None, None, :]).astype(F32)
    blocks = jnp.einsum('hrd,qkd->hrqk', rpb.astype(F32), onehot, precision=HIGHEST)
    blocks = jnp.where(col_ok[None, None], blocks, NEG_BIG)
    masked = jnp.full((NA_HEADS, GRID_W, GRID_W), NEG_BIG, F32)
    tabs = []
    for jb in (0, 1, nblk - 1):
        ws = min(max(jb * rb - NA_KR // 2, 0), rows - wr)
        q_rows = []
        for qr in range(jb * rb, (jb + 1) * rb):
            rs = min(max(qr - NA_KR // 2, 0), rows - NA_KR)
            row = [blocks[:, kr - qr + NA_KR - 1] if rs <= kr < rs + NA_KR else masked
                   for kr in range(ws, ws + wr)]
            q_rows.append(jnp.concatenate(row, axis=-1))
        tabs.append(jnp.concatenate(q_rows, axis=1))
    return jnp.stack(tabs)


def _neighbourhood_attention(qkv, qkv_ctx, rpb):
    b, l, _ = qkv.shape
    n_ctx = qkv_ctx.shape[1]
    rows = l // GRID_W
    rb = NA_ROWS_PER_BLOCK
    nblk = rows // rb
    tq = rb * GRID_W
    nkeys = NA_WIN_ROWS * GRID_W
    hp = NA_HEADS * NA_HEAD_DIM // LANES
    tabs = _na_tables(rpb, rows)

    def cls(j):
        return jnp.where(j == 0, 0, jnp.where(j == nblk - 1, 2, 1))

    return pl.pallas_call(
        functools.partial(_na_body, rows=rows),
        grid=(b, hp, nblk),
        in_specs=[pl.BlockSpec((1, tq, LANES), lambda i, h, j: (i, j, h)),
                  pl.BlockSpec((1, l, LANES), lambda i, h, j: (i, 0, hp + h)),
                  pl.BlockSpec((1, l, LANES), lambda i, h, j: (i, 0, 2 * hp + h)),
                  pl.BlockSpec((1, n_ctx, LANES), lambda i, h, j: (i, 0, hp + h)),
                  pl.BlockSpec((1, n_ctx, LANES), lambda i, h, j: (i, 0, 2 * hp + h)),
                  pl.BlockSpec((1, 2, tq, nkeys), lambda i, h, j: (cls(j), h, 0, 0))],
        out_specs=pl.BlockSpec((1, tq, LANES), lambda i, h, j: (i, j, h)),
        out_shape=jax.ShapeDtypeStruct((b, l, NA_HEADS * NA_HEAD_DIM), F32),
        compiler_params=_cparams(("parallel", "parallel", "arbitrary")),
        name="neighbourhood_attention",
    )(qkv, qkv, qkv, qkv_ctx, qkv_ctx, tabs)


def _ctx_attn_body(q_ref, k_ref, v_ref, o_ref):
    q = q_ref[0]
    k = k_ref[0]
    v = v_ref[0]
    lane = lax.broadcasted_iota(jnp.int32, (1, LANES), 1)
    out = jnp.zeros(q.shape, F32)
    for hh in range(LANES // NA_HEAD_DIM):
        in_head = (lane >= hh * NA_HEAD_DIM) & (lane < (hh + 1) * NA_HEAD_DIM)
        qh = jnp.where(in_head, q, jnp.zeros_like(q))
        s = _dot_nt(qh, k)
        p = jnp.exp(s - jnp.max(s, axis=-1, keepdims=True))
        o = _dot(p.astype(BF16), v) / jnp.sum(p, axis=-1, keepdims=True)
        out = jnp.where(in_head, o, out)
    o_ref[0] = out


def _context_attention(qkv_ctx):
    b, n, _ = qkv_ctx.shape
    hp = NA_HEADS * NA_HEAD_DIM // LANES
    return pl.pallas_call(
        _ctx_attn_body,
        grid=(b, hp),
        in_specs=[pl.BlockSpec((1, n, LANES), lambda i, h: (i, 0, h)),
                  pl.BlockSpec((1, n, LANES), lambda i, h: (i, 0, hp + h)),
                  pl.BlockSpec((1, n, LANES), lambda i, h: (i, 0, 2 * hp + h))],
        out_specs=pl.BlockSpec((1, n, LANES), lambda i, h: (i, 0, h)),
        out_shape=jax.ShapeDtypeStruct((b, n, NA_HEADS * NA_HEAD_DIM), F32),
        compiler_params=_cparams(("parallel", "parallel")),
        name="context_attention",
    )(qkv_ctx, qkv_ctx, qkv_ctx)


def _out_body(a_ref, b_ref, x_ref, w_ref, g1_ref, gate_ref, g2_ref, sh_ref, sc_ref, rw_ref,
              xo_ref, h_ref, aff_ref):
    half = a_ref.shape[-1]
    y = _dot(a_ref[0].astype(BF16), w_ref[0:half, :]) + _dot(b_ref[0].astype(BF16), w_ref[half:2 * half, :])
    xn = x_ref[0] + gate_ref[0] * _rms(y, g1_ref[...])
    xo_ref[0] = xn
    h = _rms(xn, g2_ref[...]) * (1.0 + sc_ref[0]) + sh_ref[0]
    h_ref[0] = _pack_bf16_pairs(h)
    logits = jnp.dot(h, rw_ref[...], precision=HIGHEST, preferred_element_type=F32)
    lane = lax.broadcasted_iota(jnp.int32, (1, LANES), 1)
    logits = jnp.where(lane < N_EXPERTS, logits, NEG_BIG)
    e = jnp.exp(logits - jnp.max(logits, axis=-1, keepdims=True))
    aff = e / jnp.sum(e, axis=-1, keepdims=True)
    aff_ref[0] = aff.T[0:N_EXPERTS, :]


def _out_proj(a, b2, x, w_bf, g1, gate, g2, shift, scale, router_w, tm):
    b, l, d = x.shape
    half = a.shape[-1]
    rw = jnp.pad(router_w, ((0, 0), (0, LANES - N_EXPERTS)))
    vec = pl.BlockSpec((1, d), lambda i, j: (0, 0))
    bvec = pl.BlockSpec((1, 1, d), lambda i, j: (i, 0, 0))
    return pl.pallas_call(
        _out_body,
        grid=(b, l // tm),
        in_specs=[pl.BlockSpec((1, tm, half), lambda i, j: (i, j, 0)),
                  pl.BlockSpec((1, tm, half), lambda i, j: (i, j, 0)),
                  pl.BlockSpec((1, tm, d), lambda i, j: (i, j, 0)),
                  pl.BlockSpec((2 * half, d), lambda i, j: (0, 0)),
                  vec, bvec, vec, bvec, bvec,
                  pl.BlockSpec((d, LANES), lambda i, j: (0, 0))],
        out_specs=[pl.BlockSpec((1, tm, d), lambda i, j: (i, j, 0)),
                   pl.BlockSpec((1, tm, d // 2), lambda i, j: (i, j, 0)),
                   pl.BlockSpec((1, N_EXPERTS, tm), lambda i, j: (i, 0, j))],
        out_shape=[jax.ShapeDtypeStruct((b, l, d), F32),
                   jax.ShapeDtypeStruct((b, l, d // 2), jnp.int32),
                   jax.ShapeDtypeStruct((b, N_EXPERTS, l), F32)],
        compiler_params=_cparams(("parallel", "parallel")),
        name="out_proj",
    )(a, b2, x, w_bf, g1.reshape(1, d), gate.reshape(b, 1, d), g2.reshape(1, d),
      shift.reshape(b, 1, d), scale.reshape(b, 1, d), rw)


def _moe_body(x_ref, val_ref, wg_ref, wu_ref, wd_ref, o_ref, x_s, wg_s, wu_s, wd_s, *, chunk):
    f = pl.program_id(1)
    wg_s[...] = wg_ref[0, 0].astype(BF16)
    wu_s[...] = wu_ref[0, 0].astype(BF16)
    wd_s[...] = wd_ref[0, 0].astype(BF16)
    m = x_ref.shape[1]
    half = x_ref.shape[2]

    @pl.when(f == 0)
    def _():
        def unpack(c, carry):
            r = pl.multiple_of(c * chunk, chunk)
            lo, hi = _unpack_bf16_pairs(x_ref[0, pl.ds(r, chunk), :])
            x_s[pl.ds(r, chunk), 0:half] = lo
            x_s[pl.ds(r, chunk), half:2 * half] = hi
            return carry

        lax.fori_loop(0, m // chunk, unpack, 0)

    def rows(c, carry):
        r = pl.multiple_of(c * chunk, chunk)
        xs = x_s[pl.ds(r, chunk), :]
        hid = (_silu(_dot(xs, wg_s[...])) * _dot(xs, wu_s[...])).astype(BF16)
        y = _dot(hid, wd_s[...])

        @pl.when(f == 0)
        def _():
            o_ref[0, pl.ds(r, chunk), :] = y

        @pl.when(f > 0)
        def _():
            o_ref[0, pl.ds(r, chunk), :] += y

        return carry

    lax.fori_loop(0, m // chunk, rows, 0)

    @pl.when(f == pl.num_programs(1) - 1)
    def _():
        o_ref[0] = o_ref[0] * val_ref[0]


def _expert_ffn(xg, vals, w_gate, w_up, w_down, layer, chunk):
    e, m, half = xg.shape
    d = 2 * half
    ff = w_gate.shape[-1]
    tf = 256
    return pl.pallas_call(
        functools.partial(_moe_body, chunk=chunk),
        grid=(e, ff // tf),
        in_specs=[pl.BlockSpec((1, m, half), lambda i, f: (i, 0, 0)),
                  pl.BlockSpec((1, m, 1), lambda i, f: (i, 0, 0)),
                  pl.BlockSpec((1, 1, d, tf), lambda i, f: (layer, i, 0, f)),
                  pl.BlockSpec((1, 1, d, tf), lambda i, f: (layer, i, 0, f)),
                  pl.BlockSpec((1, 1, tf, d), lambda i, f: (layer, i, f, 0))],
        out_specs=pl.BlockSpec((1, m, d), lambda i, f: (i, 0, 0)),
        out_shape=jax.ShapeDtypeStruct((e, m, d), F32),
        scratch_shapes=[pltpu.VMEM((m, d), BF16), pltpu.VMEM((d, tf), BF16), pltpu.VMEM((d, tf), BF16),
                        pltpu.VMEM((tf, d), BF16)],
        compiler_params=_cparams(("parallel", "arbitrary")),
        name="expert_ffn",
    )(xg, vals, w_gate, w_up, w_down)


def _resid_body(x_ref, f_ref, gate_ref, g_ref, o_ref):
    o_ref[0] = x_ref[0] + gate_ref[0] * _rms(f_ref[0], g_ref[...])


def _gated_residual(x, f, gate, g, tm):
    b, l, d = x.shape
    blk = pl.BlockSpec((1, tm, d), lambda i, j: (i, j, 0))
    return pl.pallas_call(
        _resid_body,
        grid=(b, l // tm),
        in_specs=[blk, blk, pl.BlockSpec((1, 1, d), lambda i, j: (i, 0, 0)),
                  pl.BlockSpec((1, d), lambda i, j: (0, 0))],
        out_specs=blk,
        out_shape=jax.ShapeDtypeStruct((b, l, d), F32),
        compiler_params=_cparams(("parallel", "parallel")),
        name="gated_residual",
    )(x, f, gate.reshape(b, 1, d), g.reshape(1, d))


def _swap_pairs(x):
    nf = GLA_DK // 4
    lane = lax.broadcasted_iota(jnp.int32, (1, LANES), 1)
    up = pltpu.roll(x, LANES - nf, 1)
    down = pltpu.roll(x, nf, 1)
    return jnp.where(lane % (2 * nf) < nf, up, down)


def _odd_in_body(x_ref, g_ref, sh_ref, sc_ref, w_ref, cos_ref, sin_ref, gw_ref, gb_ref,
                 pool_ref, qk_ref, v_ref, r_ref, gate_ref):
    h = (_rms(x_ref[0], g_ref[...]) * (1.0 + sc_ref[0]) + sh_ref[0]).astype(BF16)
    qk = GLA_HEADS * GLA_DK
    vd = GLA_HEADS * GLA_DV
    q0 = POOL_CH
    v0 = q0 + 2 * qk
    r0 = v0 + vd
    l0 = r0 + vd
    pool_ref[0] = _dot(h, w_ref[:, 0:q0])
    for s in range(2 * qk // LANES):
        raw = _dot(h, w_ref[:, q0 + s * LANES:q0 + (s + 1) * LANES])
        c = cos_ref[:, (s * LANES) % qk:(s * LANES) % qk + LANES]
        sn = sin_ref[:, (s * LANES) % qk:(s * LANES) % qk + LANES]
        rot = raw * c + _swap_pairs(raw) * sn
        if s * LANES < qk:
            rot = rot * (GLA_DK ** -0.5)
        qk_ref[0, :, s * LANES:(s + 1) * LANES] = rot
    v_ref[0] = _dot(h, w_ref[:, v0:r0]).astype(BF16)
    r_ref[0] = _dot(h, w_ref[:, r0:l0])
    lr = _dot(h, w_ref[:, l0:l0 + 2 * GLA_RANK])
    z = jnp.dot(lr, gw_ref[...], precision=HIGHEST, preferred_element_type=F32) + gb_ref[...]
    gate_ref[0] = (jnp.minimum(z, 0.0) - jnp.log1p(jnp.exp(-jnp.abs(z)))) * (1.0 / GLA_TAU)


def _odd_in(x, g, shift, scale, w_bf, cos_t, sin_t, gate_w, gate_b, tm):
    b, l, d = x.shape
    n = w_bf.shape[1]
    qk = GLA_HEADS * GLA_DK
    vd = GLA_HEADS * GLA_DV
    gw = jnp.zeros((2 * GLA_RANK, 2 * qk), F32)
    gw = gw.at[:GLA_RANK, :qk].set(gate_w[0]).at[GLA_RANK:, qk:].set(gate_w[1])
    gb = jnp.concatenate([gate_b[0], gate_b[1]]).reshape(1, 2 * qk)
    vec = pl.BlockSpec((1, 1, d), lambda i, j: (i, 0, 0))
    row = lambda w: pl.BlockSpec((1, tm, w), lambda i, j: (i, j, 0))
    return pl.pallas_call(
        _odd_in_body,
        grid=(b, l // tm),
        in_specs=[row(d), pl.BlockSpec((1, d), lambda i, j: (0, 0)), vec, vec,
                  pl.BlockSpec((d, n), lambda i, j: (0, 0)),
                  pl.BlockSpec((tm, qk), lambda i, j: (j, 0)),
                  pl.BlockSpec((tm, qk), lambda i, j: (j, 0)),
                  pl.BlockSpec((2 * GLA_RANK, 2 * qk), lambda i, j: (0, 0)),
                  pl.BlockSpec((1, 2 * qk), lambda i, j: (0, 0))],
        out_specs=[row(POOL_CH), row(2 * qk), row(vd), row(vd), row(2 * qk)],
        out_shape=[jax.ShapeDtypeStruct((b, l, POOL_CH), F32),
                   jax.ShapeDtypeStruct((b, l, 2 * qk), F32),
                   jax.ShapeDtypeStruct((b, l, vd), BF16),
                   jax.ShapeDtypeStruct((b, l, vd), F32),
                   jax.ShapeDtypeStruct((b, l, 2 * qk), F32)],
        compiler_params=_cparams(("parallel", "parallel")),
        name="odd_in",
    )(x, g.reshape(1, d), shift.reshape(b, 1, d), scale.reshape(b, 1, d), w_bf, cos_t, sin_t, gw, gb)


def _rope_tables(l):
    t = jnp.arange(l)
    pos_r = (t // GRID_W).astype(F32)
    pos_c = (t % GRID_W).astype(F32)
    nf = GLA_DK // 4
    inv = jnp.power(ROPE_BASE, -jnp.arange(nf, dtype=F32) / nf)
    ar = pos_r[:, None] * inv[None, :]
    ac = pos_c[:, None] * inv[None, :]
    cos_h = jnp.concatenate([jnp.cos(ar), jnp.cos(ar), jnp.cos(ac), jnp.cos(ac)], axis=-1)
    sin_h = jnp.concatenate([-jnp.sin(ar), jnp.sin(ar), -jnp.sin(ac), jnp.sin(ac)], axis=-1)
    return jnp.tile(cos_h, (1, GLA_HEADS)), jnp.tile(sin_h, (1, GLA_HEADS))


def _gla_chunk(qk, v, g, s_ref, tri, causal_mask, blockdiag, last_row, mid_row):
    hk = GLA_HEADS * GLA_DK
    c = GLA_CHUNK
    q = qk[:, 0:hk]
    k = qk[:, hk:2 * hk]
    bc = jnp.dot(tri, g, precision=HIGHEST, preferred_element_type=F32)
    b_mid = bc[mid_row:mid_row + 1, :]
    b_last = bc[last_row:last_row + 1, :]
    qt = q * jnp.exp(bc - b_mid)
    kt = k * jnp.exp(b_mid - bc)
    qe = (qt * jnp.exp(b_mid)).astype(BF16)
    ke_t = (kt * jnp.exp(b_last - b_mid)).T.astype(BF16)
    ktb = kt.astype(BF16)
    lane = lax.broadcasted_iota(jnp.int32, (1, hk), 1)
    s_old = s_ref[...]
    inter = _dot(qe, s_old.astype(BF16))
    outs = []
    for h in range(GLA_HEADS):
        in_head = (lane >= h * GLA_DK) & (lane < (h + 1) * GLA_DK)
        qh = jnp.where(in_head, qt, 0.0).astype(BF16)
        att = jnp.where(causal_mask, _dot_nt(qh, ktb), 0.0)
        outs.append(_dot(att.astype(BF16), v[:, h * GLA_DV:(h + 1) * GLA_DV]))
    o = inter + jnp.concatenate(outs, axis=-1)
    decay_col = jnp.exp(jnp.sum(g.T, axis=1, keepdims=True))
    upd = _dot(ke_t, v)
    s_ref[...] = decay_col * s_old + jnp.where(blockdiag, upd, 0.0)
    return o


def _gla_body(qkf_ref, qkb_ref, vf_ref, vb_ref, gf_ref, gb_ref, s0f_ref, s0b_ref,
              of_ref, ob_ref, sff_ref, sbf_ref, sf_ref, sb_ref, *, tile):
    n = pl.program_id(1)
    c = GLA_CHUNK
    hk = GLA_HEADS * GLA_DK
    hv = GLA_HEADS * GLA_DV
    ri = lax.broadcasted_iota(jnp.int32, (hk, hv), 0) // GLA_DK
    ci = lax.broadcasted_iota(jnp.int32, (hk, hv), 1) // GLA_DV
    blockdiag = ri == ci

    @pl.when(n == 0)
    def _():
        sf_ref[...] = jnp.zeros((hk, hv), F32)
        sb_ref[...] = jnp.zeros((hk, hv), F32)
        for h in range(GLA_HEADS):
            sf_ref[h * GLA_DK:(h + 1) * GLA_DK, h * GLA_DV:(h + 1) * GLA_DV] = s0f_ref[0, h]
            sb_ref[h * GLA_DK:(h + 1) * GLA_DK, h * GLA_DV:(h + 1) * GLA_DV] = s0b_ref[0, h]

    ii = lax.broadcasted_iota(jnp.int32, (c, c), 0)
    jj = lax.broadcasted_iota(jnp.int32, (c, c), 1)
    lower = jj <= ii
    upper = jj >= ii
    tri_f = lower.astype(F32)
    tri_b = upper.astype(F32)
    nchunks = tile // c
    for cc in range(nchunks):
        sl = slice(cc * c, (cc + 1) * c)
        of_ref[0, sl, :] = _gla_chunk(qkf_ref[0, sl, :], vf_ref[0, sl, :], gf_ref[0, sl, :], sf_ref,
                                      tri_f, lower, blockdiag, c - 1, c // 2 - 1)
        rc = nchunks - 1 - cc
        sl = slice(rc * c, (rc + 1) * c)
        ob_ref[0, sl, :] = _gla_chunk(qkb_ref[0, sl, :], vb_ref[0, sl, :], gb_ref[0, sl, :], sb_ref,
                                      tri_b, upper, blockdiag, 0, c // 2)

    @pl.when(n == pl.num_programs(1) - 1)
    def _():
        for h in range(GLA_HEADS):
            sff_ref[0, h] = sf_ref[h * GLA_DK:(h + 1) * GLA_DK, h * GLA_DV:(h + 1) * GLA_DV]
            sbf_ref[0, h] = sb_ref[h * GLA_DK:(h + 1) * GLA_DK, h * GLA_DV:(h + 1) * GLA_DV]


def _gla(qk, v, gates, s0f, s0b, tile):
    b, l, _ = qk.shape
    hk = GLA_HEADS * GLA_DK
    hv = GLA_HEADS * GLA_DV
    nt = l // tile
    fwd = lambda w, col: pl.BlockSpec((1, tile, w), lambda i, n: (i, n, col))
    bwd = lambda w, col: pl.BlockSpec((1, tile, w), lambda i, n: (i, nt - 1 - n, col))
    st = pl.BlockSpec((1, GLA_HEADS, GLA_DK, GLA_DV), lambda i, n: (i, 0, 0, 0))
    return pl.pallas_call(
        functools.partial(_gla_body, tile=tile),
        grid=(b, nt),
        in_specs=[fwd(2 * hk, 0), bwd(2 * hk, 0), fwd(hv, 0), bwd(hv, 0), fwd(hk, 0), bwd(hk, 1), st, st],
        out_specs=[fwd(hv, 0), bwd(hv, 0), st, st],
        out_shape=[jax.ShapeDtypeStruct((b, l, hv), F32), jax.ShapeDtypeStruct((b, l, hv), F32),
                   jax.ShapeDtypeStruct((b, GLA_HEADS, GLA_DK, GLA_DV), F32),
                   jax.ShapeDtypeStruct((b, GLA_HEADS, GLA_DK, GLA_DV), F32)],
        scratch_shapes=[pltpu.VMEM((hk, hv), F32), pltpu.VMEM((hk, hv), F32)],
        compiler_params=_cparams(("parallel", "arbitrary")),
        name="gla_scan",
    )(qk, qk, v, v, gates, gates, s0f, s0b)


def _odd_mid_body(cur_ref, prev_ref, next_ref, of_ref, ob_ref, r_ref, hg_ref, pw_ref, ps_ref,
                  pool_ref, d_ref, buf_ref, *, tile, seq):
    j = pl.program_id(1)
    last = pl.num_programs(1) - 1
    hal = POOL_HALO
    buf_ref[0:hal, :] = jnp.where(j == 0, 0.0, prev_ref[0])
    buf_ref[hal:hal + tile, :] = cur_ref[0]
    buf_ref[hal + tile:hal + tile + hal, :] = jnp.where(j == last, 0.0, next_ref[0])
    t = j * tile + lax.broadcasted_iota(jnp.int32, (tile, 1), 0)
    for gi, win in enumerate(POOL_WINDOWS):
        cols = slice(gi * POOL_GROUP, (gi + 1) * POOL_GROUP)
        acc = jnp.zeros((tile, POOL_GROUP), F32)
        for off in range(-(win // 2), win - win // 2):
            acc = acc + buf_ref[hal + off:hal + off + tile, cols]
        cnt = jnp.minimum(t + (win - win // 2), seq) - jnp.maximum(t - win // 2, 0)
        diff = acc / cnt.astype(F32) - cur_ref[0, :, cols]
        pool_ref[0, :, cols] = _dot(diff.astype(BF16), pw_ref[gi]) * ps_ref[:, cols]
    for h in range(GLA_HEADS):
        cols = slice(h * GLA_DV, (h + 1) * GLA_DV)
        o = of_ref[0, :, cols] + ob_ref[0, :, cols]
        d_ref[0, :, cols] = _rms(o, hg_ref[:, cols]) * _silu(r_ref[0, :, cols])


def _odd_mid(pool_u, o_f, o_b, r, head_g, pool_w_bf, pool_scale, tile):
    b, l, c = pool_u.shape
    hal = POOL_HALO
    per = tile // hal
    nh = l // hal
    blk = pl.BlockSpec((1, tile, c), lambda i, j: (i, j, 0))
    vec = pl.BlockSpec((1, c), lambda i, j: (0, 0))
    return pl.pallas_call(
        functools.partial(_odd_mid_body, tile=tile, seq=l),
        grid=(b, l // tile),
        in_specs=[blk,
                  pl.BlockSpec((1, hal, c), lambda i, j: (i, jnp.maximum(j * per - 1, 0), 0)),
                  pl.BlockSpec((1, hal, c), lambda i, j: (i, jnp.minimum((j + 1) * per, nh - 1), 0)),
                  blk, blk, blk, vec,
                  pl.BlockSpec((len(POOL_WINDOWS), POOL_GROUP, POOL_GROUP), lambda i, j: (0, 0, 0)),
                  vec],
        out_specs=[blk, blk],
        out_shape=[jax.ShapeDtypeStruct((b, l, c), F32), jax.ShapeDtypeStruct((b, l, c), F32)],
        scratch_shapes=[pltpu.VMEM((tile + 2 * hal, c), F32)],
        compiler_params=_cparams(("parallel", "parallel")),
        name="odd_mid",
    )(pool_u, pool_u, pool_u, o_f, o_b, r, head_g.reshape(1, c), pool_w_bf, pool_scale.reshape(1, c))


def _route(aff_t, base):
    b, e, n = aff_t.shape
    cap = EC_CAPACITY_FACTOR * n // N_EXPERTS
    vals, idx = lax.top_k(aff_t, cap)
    flat = idx + (base + jnp.arange(b, dtype=idx.dtype) * n)[:, None, None]
    flat = jnp.swapaxes(flat, 0, 1).reshape(e, b * cap)
    vals = jnp.swapaxes(vals, 0, 1).reshape(e, b * cap)
    return vals, flat


def _gather_rows(src, idx):
    window = SC_GATHER_WINDOW
    split = 1
    while 2 * window * (src.shape[1] // split) * 4 > SC_TILE_VMEM_BUDGET:
        split *= 2
    if split > 1:
        sub = idx[:, None] * split + jnp.arange(split, dtype=idx.dtype)[None, :]
        out = _gather_rows(src.reshape(src.shape[0] * split, src.shape[1] // split), sub.reshape(-1))
        return out.reshape(idx.shape[0], src.shape[1])
    n = idx.shape[0]
    width = src.shape[1]
    assert n % (window * SC_CORES * SC_SUBCORES) == 0, n
    mesh = plsc.VectorSubcoreMesh(core_axis_name="core", subcore_axis_name="subcore",
                                  num_cores=SC_CORES, num_subcores=SC_SUBCORES)

    @pl.kernel(out_type=jax.ShapeDtypeStruct((n, width), src.dtype), mesh=mesh, scratch_types=[],
               name="gather_rows")
    def gather(src_hbm, idx_hbm, out_hbm):
        def body(idx_vmem, out_vmem):
            pltpu.sync_copy(src_hbm.at[idx_vmem.at[0]], out_vmem)

        pltpu.emit_pipeline(
            body,
            grid=(n // window,),
            in_specs=[pl.BlockSpec((1, window), lambda i: (0, i))],
            out_specs=[pl.BlockSpec((window, width), lambda i: (i, 0))],
            core_axis_name=("core", "subcore"),
            dimension_semantics=(pltpu.PARALLEL,),
        )(idx_hbm, out_hbm)

    return gather(src, idx.reshape(1, n))


def _moe(parts, w_gate, w_up, w_down, layer):
    half = parts[0][1].shape[-1]
    d = 2 * half
    sizes = [h.shape[0] * h.shape[1] for _, h in parts]
    bases = [sum(sizes[:i]) for i in range(len(parts))]
    routed = [_route(a, base) for (a, _), base in zip(parts, bases)]
    src = jnp.concatenate([h.reshape(-1, half) for _, h in parts], axis=0)
    vals = jnp.concatenate([r[0] for r in routed], axis=1)
    flat = jnp.concatenate([r[1] for r in routed], axis=1)
    e, m = flat.shape
    pad = -m % LANES
    vals = jnp.pad(vals, ((0, 0), (0, pad)))
    flat = jnp.pad(flat, ((0, 0), (0, pad)))
    m += pad
    xg = _gather_rows(src, flat.reshape(-1)).reshape(e, m, half)
    chunk = next(c for c in (512, 544, 384, 256, 128) if m % c == 0)
    y = _expert_ffn(xg, vals[..., None], w_gate, w_up, w_down, layer, chunk)
    out = jnp.zeros((sum(sizes), d), F32).at[flat.reshape(-1)].add(y.reshape(-1, d))
    return [out[base:base + size].reshape(h.shape[0], h.shape[1], d)
            for (_, h), base, size in zip(parts, bases, sizes)]


def kernel(x, c, ctx, c_ctx, w_mod, b_mod, norm_g, w_in_even, w_out_even, conv_w, conv_b, conv_ln_g,
           conv_ln_b, na_rpb, w_in_odd, w_out_odd, pool_w, pool_scale, gla_gate_w, gla_gate_b, gla_head_g,
           router_w, expert_w_gate, expert_w_up, expert_w_down):
    b, l, d = x.shape
    n_ctx = ctx.shape[1]
    tm = 512

    mod_rows = jnp.concatenate([c, c_ctx[None], jnp.zeros((8 - b - 1, d), F32)], axis=0)

    mod_all = _modulation(mod_rows, w_mod, b_mod)

    def modulation(i):
        mm = mod_all[i]
        m = mm[:b].reshape(b, 6, d)
        mc = jnp.broadcast_to(mm[b].reshape(1, 6, d), (b, 6, d))
        return m, mc

    m, mc = modulation(0)
    g = norm_g[0]
    w_in = w_in_even[0].astype(BF16)
    w_out = w_out_even[0].astype(BF16)
    glu, qkv = _even_in(x, g[0], m[:, 0], m[:, 1], w_in, tm)
    glu_c, qkv_c = _even_in(ctx, g[0], mc[:, 0], mc[:, 1], w_in, n_ctx)
    a_lat = _conv_branch(glu, conv_w[0], conv_b[0], conv_ln_g[0], conv_ln_b[0], 256)
    a_ctx = _conv_branch(glu_c, conv_w[0], conv_b[0], conv_ln_g[0], conv_ln_b[0], n_ctx)
    na = _neighbourhood_attention(qkv, qkv_c, na_rpb[0])
    att_c = _context_attention(qkv_c)
    x, h2, aff = _out_proj(a_lat, na, x, w_out, g[1], m[:, 2], g[2], m[:, 3], m[:, 4], router_w[0], tm)
    ctx, h2c, aff_c = _out_proj(a_ctx, att_c, ctx, w_out, g[1], mc[:, 2], g[2], mc[:, 3], mc[:, 4],
                                router_w[0], n_ctx)
    f, f_c = _moe([(aff, h2), (aff_c, h2c)], expert_w_gate, expert_w_up, expert_w_down, 0)
    x = _gated_residual(x, f, m[:, 5], g[3], tm)
    ctx = _gated_residual(ctx, f_c, mc[:, 5], g[3], n_ctx)

    m, mc = modulation(1)
    g = norm_g[1]
    w_in = w_in_odd[0].astype(BF16)
    w_out = w_out_odd[0].astype(BF16)
    cos_t, sin_t = _rope_tables(l)
    ones_t = jnp.ones((n_ctx, GLA_HEADS * GLA_DK), F32)
    _, qk_c, v_c, _, gate_c = _odd_in(ctx, g[0], mc[:, 0], mc[:, 1], w_in, ones_t, jnp.zeros_like(ones_t),
                                      gla_gate_w[0], gla_gate_b[0], n_ctx)
    s_zero = jnp.zeros((b, GLA_HEADS, GLA_DK, GLA_DV), F32)
    _, _, s_f, s_b = _gla(qk_c, v_c, gate_c, s_zero, s_zero, n_ctx)
    pool_u, qk, v, r, gate = _odd_in(x, g[0], m[:, 0], m[:, 1], w_in, cos_t, sin_t,
                                     gla_gate_w[0], gla_gate_b[0], tm)
    o_f, o_b, _, _ = _gla(qk, v, gate, s_f, s_b, 256)
    pool_y, d_lat = _odd_mid(pool_u, o_f, o_b, r, gla_head_g[0], pool_w[0].astype(BF16), pool_scale[0], 256)
    x, h2, aff = _out_proj(pool_y, d_lat, x, w_out, g[1], m[:, 2], g[2], m[:, 3], m[:, 4], router_w[1], tm)
    (f,) = _moe([(aff, h2)], expert_w_gate, expert_w_up, expert_w_down, 1)
    return _gated_residual(x, f, m[:, 5], g[3], tm)
```

```python
import functools

import jax
import jax.numpy as jnp
from jax import lax
from jax.experimental import pallas as pl
from jax.experimental.pallas import tpu as pltpu
from jax.experimental.pallas import tpu_sc as plsc

F32 = jnp.float32
BF16 = jnp.bfloat16
HIGHEST = lax.Precision.HIGHEST

D_MODEL = 1024
GRID_W = 64
EPS = 1e-6
CONV_CH = 512
CONV_WIDTH = 31
CONV_HALO = 16
NA_HEADS = 8
NA_HEAD_DIM = 64
NA_KR = 8
NA_KC = 16
NA_ROWS_PER_BLOCK = 4
NA_WIN_ROWS = 12
POOL_CH = 512
POOL_WINDOWS = (2, 4, 8, 16)
POOL_GROUP = 128
POOL_HALO = 8
GLA_HEADS = 4
GLA_DK = 64
GLA_DV = 128
GLA_RANK = 16
GLA_TAU = 16.0
GLA_CHUNK = 64
ROPE_BASE = 10000.0
N_EXPERTS = 16
EXPERT_FF = 2816
EC_CAPACITY_FACTOR = 2
LANES = 128
SUBLANES = 8
NEG_BIG = -1e30
VMEM_LIMIT = 56 * 1024 * 1024
SC_CORES = 2
SC_SUBCORES = 16
SC_TILE_VMEM_BUDGET = 400 * 1024
SC_GATHER_WINDOW = 128


def _cparams(sem):
    return pltpu.CompilerParams(dimension_semantics=sem, vmem_limit_bytes=VMEM_LIMIT)


def _rms(x, g):
    return x * lax.rsqrt(jnp.mean(x * x, axis=-1, keepdims=True) + EPS) * g


def _sigmoid(x):
    return 1.0 / (1.0 + jnp.exp(-x))


def _silu(x):
    return x * _sigmoid(x)


def _dot(a, b):
    return jnp.dot(a, b, preferred_element_type=F32)


def _pack_bf16_pairs(h):
    half = h.shape[-1] // 2
    bits = lax.bitcast_convert_type(h.astype(BF16).astype(F32), jnp.uint32)
    packed = (bits[:, half:] & jnp.uint32(0xFFFF0000)) | (bits[:, :half] >> 16)
    return lax.bitcast_convert_type(packed, jnp.int32)


def _unpack_bf16_pairs(p):
    bits = lax.bitcast_convert_type(p, jnp.uint32)
    lo = lax.bitcast_convert_type(bits << 16, F32).astype(BF16)
    hi = lax.bitcast_convert_type(bits & jnp.uint32(0xFFFF0000), F32).astype(BF16)
    return lo, hi


def _dot_nt(a, b):
    return lax.dot_general(a, b, (((1,), (1,)), ((), ())), preferred_element_type=F32)


def _mod_body(c_ref, w_ref, b_ref, o_ref):
    o_ref[0] = jnp.dot(_silu(c_ref[...]), w_ref[0], precision=HIGHEST,
                       preferred_element_type=F32) + b_ref[0]


def _modulation(rows, w, b):
    depth, _, n = w.shape
    tn = 1536
    return pl.pallas_call(
        _mod_body,
        grid=(depth, n // tn),
        in_specs=[pl.BlockSpec((8, D_MODEL), lambda i, j: (0, 0)),
                  pl.BlockSpec((1, D_MODEL, tn), lambda i, j: (i, 0, j)),
                  pl.BlockSpec((1, 1, tn), lambda i, j: (i, 0, j))],
        out_specs=pl.BlockSpec((1, 8, tn), lambda i, j: (i, 0, j)),
        out_shape=jax.ShapeDtypeStruct((depth, 8, n), F32),
        compiler_params=_cparams(("parallel", "parallel")),
        name="modulation",
    )(rows, w, b.reshape(depth, 1, n))


def _even_in_body(x_ref, g_ref, sh_ref, sc_ref, w_ref, glu_ref, qkv_ref):
    h = (_rms(x_ref[0], g_ref[...]) * (1.0 + sc_ref[0]) + sh_ref[0]).astype(BF16)
    c = CONV_CH
    glu_ref[0] = _dot(h, w_ref[:, 0:c]) * _sigmoid(_dot(h, w_ref[:, c:2 * c]))
    hd = NA_HEADS * NA_HEAD_DIM
    q0 = 2 * c
    qkv_ref[0, :, 0:hd] = (_dot(h, w_ref[:, q0:q0 + hd]) * (NA_HEAD_DIM ** -0.5)).astype(BF16)
    qkv_ref[0, :, hd:3 * hd] = _dot(h, w_ref[:, q0 + hd:q0 + 3 * hd]).astype(BF16)


def _even_in(x, g, shift, scale, w_bf, tm):
    b, l, d = x.shape
    n = w_bf.shape[1]
    hd3 = 3 * NA_HEADS * NA_HEAD_DIM
    vec = pl.BlockSpec((1, 1, d), lambda i, j: (i, 0, 0))
    return pl.pallas_call(
        _even_in_body,
        grid=(b, l // tm),
        in_specs=[pl.BlockSpec((1, tm, d), lambda i, j: (i, j, 0)),
                  pl.BlockSpec((1, d), lambda i, j: (0, 0)),
                  vec, vec,
                  pl.BlockSpec((d, n), lambda i, j: (0, 0))],
        out_specs=[pl.BlockSpec((1, tm, CONV_CH), lambda i, j: (i, j, 0)),
                   pl.BlockSpec((1, tm, hd3), lambda i, j: (i, j, 0))],
        out_shape=[jax.ShapeDtypeStruct((b, l, CONV_CH), F32),
                   jax.ShapeDtypeStruct((b, l, hd3), BF16)],
        compiler_params=_cparams(("parallel", "parallel")),
        name="even_in",
    )(x, g.reshape(1, d), shift.reshape(b, 1, d), scale.reshape(b, 1, d), w_bf)


def _conv_body(cur_ref, prev_ref, next_ref, w_ref, b_ref, lg_ref, lb_ref, o_ref, buf_ref, sh_ref, *, tile, chunk):
    j = pl.program_id(1)
    last = pl.num_programs(1) - 1
    hal = CONV_HALO
    buf_ref[0:hal, :] = jnp.where(j == 0, 0.0, prev_ref[0])
    buf_ref[hal:hal + tile, :] = cur_ref[0]
    buf_ref[hal + tile:hal + tile + hal, :] = jnp.where(j == last, 0.0, next_ref[0])
    span = sh_ref.shape[1]
    for s in range(SUBLANES):
        sh_ref[s] = buf_ref[s:s + span, :]
    first = hal - CONV_WIDTH // 2
    reps = chunk // SUBLANES

    def rows(c, carry):
        r0 = pl.multiple_of(c * chunk, chunk)
        accs = [jnp.zeros((chunk, CONV_CH), F32) for _ in range(2)]
        for k in range(CONV_WIDTH):
            a, s = divmod(first + k, SUBLANES)
            wk = jnp.concatenate([w_ref[k]] * reps, axis=0)
            accs[k % 2] = accs[k % 2] + sh_ref[s, pl.ds(r0 + a * SUBLANES, chunk), :] * wk
        o_ref[0, pl.ds(r0, chunk), :] = accs[0] + accs[1]
        return carry

    lax.fori_loop(0, tile // chunk, rows, 0)
    y = o_ref[0] + b_ref[...]
    mu = jnp.mean(y, axis=-1, keepdims=True)
    yc = y - mu
    var = jnp.mean(yc * yc, axis=-1, keepdims=True)
    o_ref[0] = _silu(yc * lax.rsqrt(var + EPS) * lg_ref[...] + lb_ref[...])


def _conv_branch(glu, conv_w, conv_b, ln_g, ln_b, tile):
    b, l, c = glu.shape
    hal = CONV_HALO
    per = tile // hal
    nh = l // hal
    vec = pl.BlockSpec((1, c), lambda i, j: (0, 0))
    return pl.pallas_call(
        functools.partial(_conv_body, tile=tile, chunk=32),
        grid=(b, l // tile),
        in_specs=[pl.BlockSpec((1, tile, c), lambda i, j: (i, j, 0)),
                  pl.BlockSpec((1, hal, c), lambda i, j: (i, jnp.maximum(j * per - 1, 0), 0)),
                  pl.BlockSpec((1, hal, c), lambda i, j: (i, jnp.minimum((j + 1) * per, nh - 1), 0)),
                  pl.BlockSpec((CONV_WIDTH, SUBLANES, c), lambda i, j: (0, 0, 0)),
                  vec, vec, vec],
        out_specs=pl.BlockSpec((1, tile, c), lambda i, j: (i, j, 0)),
        out_shape=jax.ShapeDtypeStruct((b, l, c), F32),
        scratch_shapes=[pltpu.VMEM((tile + 2 * hal, c), F32),
                        pltpu.VMEM((SUBLANES, tile + 2 * hal - SUBLANES, c), F32)],
        compiler_params=_cparams(("parallel", "parallel")),
        name="conv_branch",
    )(glu, glu, glu, jnp.broadcast_to(conv_w[:, None, :], (CONV_WIDTH, SUBLANES, c)),
      conv_b.reshape(1, c), ln_g.reshape(1, c), ln_b.reshape(1, c))


def _na_window_start(j, rows):
    rb = NA_ROWS_PER_BLOCK
    return jnp.clip(j * rb - NA_KR // 2, 0, rows - NA_WIN_ROWS)


def _na_body(q_ref, k_ref, v_ref, kc_ref, vc_ref, tab_ref, o_ref, *, rows):
    j = pl.program_id(2)
    nkeys = NA_WIN_ROWS * GRID_W
    start = pl.multiple_of(_na_window_start(j, rows) * GRID_W, GRID_W)
    q = q_ref[0]
    kw = k_ref[0, pl.ds(start, nkeys), :]
    vw = v_ref[0, pl.ds(start, nkeys), :]
    kc = kc_ref[0]
    vc = vc_ref[0]
    lane = lax.broadcasted_iota(jnp.int32, (1, LANES), 1)
    out = jnp.zeros(q.shape, F32)
    for hh in range(LANES // NA_HEAD_DIM):
        in_head = (lane >= hh * NA_HEAD_DIM) & (lane < (hh + 1) * NA_HEAD_DIM)
        qh = jnp.where(in_head, q, jnp.zeros_like(q))
        s = _dot_nt(qh, kw) + tab_ref[0, hh]
        sc = _dot_nt(qh, kc)
        m = jnp.maximum(jnp.max(s, axis=-1, keepdims=True), jnp.max(sc, axis=-1, keepdims=True))
        p = jnp.exp(s - m)
        pc = jnp.exp(sc - m)
        denom = jnp.sum(p, axis=-1, keepdims=True) + jnp.sum(pc, axis=-1, keepdims=True)
        o = (_dot(p.astype(BF16), vw) + _dot(pc.astype(BF16), vc)) / denom
        out = jnp.where(in_head, o, out)
    o_ref[0] = out


def _na_tables(rpb, rows):
    rb = NA_ROWS_PER_BLOCK
    nblk = rows // rb
    wr = NA_WIN_ROWS
    qc = jnp.arange(GRID_W)
    cs = jnp.clip(qc - NA_KC // 2, 0, GRID_W - NA_KC)
    col_ok = (qc[None, :] >= cs[:, None]) & (qc[None, :] < cs[:, None] + NA_KC)
    col_off = qc[None, :] - qc[:, None] + NA_KC - 1
    onehot = (col_off[:, :, None] == jnp.arange(2 * NA_KC - 1)[None, None, :]).astype(F32)
    blocks = jnp.einsum('hrd,qkd->hrqk', rpb.astype(F32), onehot, precision=HIGHEST)
    blocks = jnp.where(col_ok[None, None], blocks, NEG_BIG)
    masked = jnp.full((NA_HEADS, GRID_W, GRID_W), NEG_BIG, F32)
    tabs = []
    for jb in (0, 1, nblk - 1):
        ws = min(max(jb * rb - NA_KR // 2, 0), rows - wr)
        q_rows = []
        for qr in range(jb * rb, (jb + 1) * rb):
            rs = min(max(qr - NA_KR // 2, 0), rows - NA_KR)
            row = [blocks[:, kr - qr + NA_KR - 1] if rs <= kr < rs + NA_KR else masked
                   for kr in range(ws, ws + wr)]
            q_rows.append(jnp.concatenate(row, axis=-1))
        tabs.append(jnp.concatenate(q_rows, axis=1))
    return jnp.stack(tabs)


def _neighbourhood_attention(qkv, qkv_ctx, rpb):
    b, l, _ = qkv.shape
    n_ctx = qkv_ctx.shape[1]
    rows = l // GRID_W
    rb = NA_ROWS_PER_BLOCK
    nblk = rows // rb
    tq = rb * GRID_W
    nkeys = NA_WIN_ROWS * GRID_W
    hp = NA_HEADS * NA_HEAD_DIM // LANES
    tabs = _na_tables(rpb, rows)

    def cls(j):
        return jnp.where(j == 0, 0, jnp.where(j == nblk - 1, 2, 1))

    return pl.pallas_call(
        functools.partial(_na_body, rows=rows),
        grid=(b, hp, nblk),
        in_specs=[pl.BlockSpec((1, tq, LANES), lambda i, h, j: (i, j, h)),
                  pl.BlockSpec((1, l, LANES), lambda i, h, j: (i, 0, hp + h)),
                  pl.BlockSpec((1, l, LANES), lambda i, h, j: (i, 0, 2 * hp + h)),
                  pl.BlockSpec((1, n_ctx, LANES), lambda i, h, j: (i, 0, hp + h)),
                  pl.BlockSpec((1, n_ctx, LANES), lambda i, h, j: (i, 0, 2 * hp + h)),
                  pl.BlockSpec((1, 2, tq, nkeys), lambda i, h, j: (cls(j), h, 0, 0))],
        out_specs=pl.BlockSpec((1, tq, LANES), lambda i, h, j: (i, j, h)),
        out_shape=jax.ShapeDtypeStruct((b, l, NA_HEADS * NA_HEAD_DIM), F32),
        compiler_params=_cparams(("parallel", "parallel", "arbitrary")),
        name="neighbourhood_attention",
    )(qkv, qkv, qkv, qkv_ctx, qkv_ctx, tabs)


def _ctx_attn_body(q_ref, k_ref, v_ref, o_ref):
    q = q_ref[0]
    k = k_ref[0]
    v = v_ref[0]
    lane = lax.broadcasted_iota(jnp.int32, (1, LANES), 1)
    out = jnp.zeros(q.shape, F32)
    for hh in range(LANES // NA_HEAD_DIM):
        in_head = (lane >= hh * NA_HEAD_DIM) & (lane < (hh + 1) * NA_HEAD_DIM)
        qh = jnp.where(in_head, q, jnp.zeros_like(q))
        s = _dot_nt(qh, k)
        p = jnp.exp(s - jnp.max(s, axis=-1, keepdims=True))
        o = _dot(p.astype(BF16), v) / jnp.sum(p, axis=-1, keepdims=True)
        out = jnp.where(in_head, o, out)
    o_ref[0] = out


def _context_attention(qkv_ctx):
    b, n, _ = qkv_ctx.shape
    hp = NA_HEADS * NA_HEAD_DIM // LANES
    return pl.pallas_call(
        _ctx_attn_body,
        grid=(b, hp),
        in_specs=[pl.BlockSpec((1, n, LANES), lambda i, h: (i, 0, h)),
                  pl.BlockSpec((1, n, LANES), lambda i, h: (i, 0, hp + h)),
                  pl.BlockSpec((1, n, LANES), lambda i, h: (i, 0, 2 * hp + h))],
        out_specs=pl.BlockSpec((1, n, LANES), lambda i, h: (i, 0, h)),
        out_shape=jax.ShapeDtypeStruct((b, n, NA_HEADS * NA_HEAD_DIM), F32),
        compiler_params=_cparams(("parallel", "parallel")),
        name="context_attention",
    )(qkv_ctx, qkv_ctx, qkv_ctx)


def _out_body(a_ref, b_ref, x_ref, w_ref, g1_ref, gate_ref, g2_ref, sh_ref, sc_ref, rwh_ref, rwl_ref,
              xo_ref, h_ref, aff_ref):
    half = a_ref.shape[-1]
    y = _dot(a_ref[0].astype(BF16), w_ref[0:half, :]) + _dot(b_ref[0].astype(BF16), w_ref[half:2 * half, :])
    xn = x_ref[0] + gate_ref[0] * _rms(y, g1_ref[...])
    xo_ref[0] = xn
    h = _rms(xn, g2_ref[...]) * (1.0 + sc_ref[0]) + sh_ref[0]
    packed = _pack_bf16_pairs(h)
    quarter = packed.shape[-1] // 2
    h_ref[0, 0] = packed[:, 0:quarter]
    h_ref[0, 1] = packed[:, quarter:2 * quarter]
    h_hi = h.astype(BF16)
    h_lo = (h - h_hi.astype(F32)).astype(BF16)
    logits = _dot(h_hi, rwh_ref[...]) + (_dot(h_lo, rwh_ref[...]) + _dot(h_hi, rwl_ref[...]))
    lane = lax.broadcasted_iota(jnp.int32, (1, LANES), 1)
    logits = jnp.where(lane < N_EXPERTS, logits, NEG_BIG)
    e = jnp.exp(logits - jnp.max(logits, axis=-1, keepdims=True))
    aff = e / jnp.sum(e, axis=-1, keepdims=True)
    aff_ref[0] = aff.T[0:N_EXPERTS, :]


def _out_proj(a, b2, x, w_bf, g1, gate, g2, shift, scale, router_w, tm):
    b, l, d = x.shape
    half = a.shape[-1]
    rw = jnp.pad(router_w, ((0, 0), (0, LANES - N_EXPERTS)))
    rw_hi = rw.astype(BF16)
    rw_lo = (rw - rw_hi.astype(F32)).astype(BF16)
    vec = pl.BlockSpec((1, d), lambda i, j: (0, 0))
    bvec = pl.BlockSpec((1, 1, d), lambda i, j: (i, 0, 0))
    rspec = pl.BlockSpec((d, LANES), lambda i, j: (0, 0))
    return pl.pallas_call(
        _out_body,
        grid=(b, l // tm),
        in_specs=[pl.BlockSpec((1, tm, half), lambda i, j: (i, j, 0)),
                  pl.BlockSpec((1, tm, half), lambda i, j: (i, j, 0)),
                  pl.BlockSpec((1, tm, d), lambda i, j: (i, j, 0)),
                  pl.BlockSpec((2 * half, d), lambda i, j: (0, 0)),
                  vec, bvec, vec, bvec, bvec, rspec, rspec],
        out_specs=[pl.BlockSpec((1, tm, d), lambda i, j: (i, j, 0)),
                   pl.BlockSpec((1, 2, tm, d // 4), lambda i, j: (i, 0, j, 0)),
                   pl.BlockSpec((1, N_EXPERTS, tm), lambda i, j: (i, 0, j))],
        out_shape=[jax.ShapeDtypeStruct((b, l, d), F32),
                   jax.ShapeDtypeStruct((b, 2, l, d // 4), jnp.int32),
                   jax.ShapeDtypeStruct((b, N_EXPERTS, l), F32)],
        compiler_params=_cparams(("parallel", "parallel")),
        name="out_proj",
    )(a, b2, x, w_bf, g1.reshape(1, d), gate.reshape(b, 1, d), g2.reshape(1, d),
      shift.reshape(b, 1, d), scale.reshape(b, 1, d), rw_hi, rw_lo)


def _moe_body(x_ref, val_ref, wg_ref, wu_ref, wd_ref, o_ref, x_s, wg_s, wu_s, wd_s, *, chunk):
    f = pl.program_id(1)
    wg_s[...] = wg_ref[0, 0].astype(BF16)
    wu_s[...] = wu_ref[0, 0].astype(BF16)
    wd_s[...] = wd_ref[0, 0].astype(BF16)
    m = x_ref.shape[2]
    quarter = x_ref.shape[3]

    @pl.when(f == 0)
    def _():
        def unpack(c, carry):
            r = pl.multiple_of(c * chunk, chunk)
            for s in range(2):
                lo, hi = _unpack_bf16_pairs(x_ref[0, s, pl.ds(r, chunk), :])
                x_s[pl.ds(r, chunk), s * quarter:(s + 1) * quarter] = lo
                x_s[pl.ds(r, chunk), (2 + s) * quarter:(3 + s) * quarter] = hi
            o_ref[0, pl.ds(r, chunk), :] = jnp.zeros((chunk, 4 * quarter), F32)
            return carry

        lax.fori_loop(0, m // chunk, unpack, 0)

    def rows(c, carry):
        r = pl.multiple_of(c * chunk, chunk)
        xs = x_s[pl.ds(r, chunk), :]
        hid = (_silu(_dot(xs, wg_s[...])) * _dot(xs, wu_s[...])).astype(BF16)
        o_ref[0, pl.ds(r, chunk), :] += _dot(hid, wd_s[...])
        return carry

    lax.fori_loop(0, m // chunk, rows, 0, unroll=True)

    @pl.when(f == pl.num_programs(1) - 1)
    def _():
        o_ref[0] = o_ref[0] * val_ref[0]


def _expert_ffn(xg, vals, w_gate, w_up, w_down, layer, chunk):
    e, _, m, quarter = xg.shape
    d = 4 * quarter
    ff = w_gate.shape[-1]
    tf = 256
    return pl.pallas_call(
        functools.partial(_moe_body, chunk=chunk),
        grid=(e, ff // tf),
        in_specs=[pl.BlockSpec((1, 2, m, quarter), lambda i, f: (i, 0, 0, 0)),
                  pl.BlockSpec((1, m, 1), lambda i, f: (i, 0, 0)),
                  pl.BlockSpec((1, 1, d, tf), lambda i, f: (layer, i, 0, f)),
                  pl.BlockSpec((1, 1, d, tf), lambda i, f: (layer, i, 0, f)),
                  pl.BlockSpec((1, 1, tf, d), lambda i, f: (layer, i, f, 0))],
        out_specs=pl.BlockSpec((1, m, d), lambda i, f: (i, 0, 0)),
        out_shape=jax.ShapeDtypeStruct((e, m, d), F32),
        scratch_shapes=[pltpu.VMEM((m, d), BF16), pltpu.VMEM((d, tf), BF16), pltpu.VMEM((d, tf), BF16),
                        pltpu.VMEM((tf, d), BF16)],
        compiler_params=_cparams(("parallel", "arbitrary")),
        name="expert_ffn",
    )(xg, vals, w_gate, w_up, w_down)


def _resid_body(x_ref, f_ref, gate_ref, g_ref, o_ref):
    o_ref[0] = x_ref[0] + gate_ref[0] * _rms(f_ref[0], g_ref[...])


def _gated_residual(x, f, gate, g, tm):
    b, l, d = x.shape
    blk = pl.BlockSpec((1, tm, d), lambda i, j: (i, j, 0))
    return pl.pallas_call(
        _resid_body,
        grid=(b, l // tm),
        in_specs=[blk, blk, pl.BlockSpec((1, 1, d), lambda i, j: (i, 0, 0)),
                  pl.BlockSpec((1, d), lambda i, j: (0, 0))],
        out_specs=blk,
        out_shape=jax.ShapeDtypeStruct((b, l, d), F32),
        compiler_params=_cparams(("parallel", "parallel")),
        name="gated_residual",
    )(x, f, gate.reshape(b, 1, d), g.reshape(1, d))


def _swap_pairs(x):
    nf = GLA_DK // 4
    lane = lax.broadcasted_iota(jnp.int32, (1, LANES), 1)
    up = pltpu.roll(x, LANES - nf, 1)
    down = pltpu.roll(x, nf, 1)
    return jnp.where(lane % (2 * nf) < nf, up, down)


def _odd_in_body(x_ref, g_ref, sh_ref, sc_ref, w_ref, cos_ref, sin_ref, gw_ref, gb_ref,
                 pool_ref, qk_ref, v_ref, r_ref, gate_ref):
    h = (_rms(x_ref[0], g_ref[...]) * (1.0 + sc_ref[0]) + sh_ref[0]).astype(BF16)
    qk = GLA_HEADS * GLA_DK
    vd = GLA_HEADS * GLA_DV
    q0 = POOL_CH
    v0 = q0 + 2 * qk
    r0 = v0 + vd
    l0 = r0 + vd
    pool_ref[0] = _dot(h, w_ref[:, 0:q0])
    for s in range(2 * qk // LANES):
        raw = _dot(h, w_ref[:, q0 + s * LANES:q0 + (s + 1) * LANES])
        c = cos_ref[:, (s * LANES) % qk:(s * LANES) % qk + LANES]
        sn = sin_ref[:, (s * LANES) % qk:(s * LANES) % qk + LANES]
        rot = raw * c + _swap_pairs(raw) * sn
        if s * LANES < qk:
            rot = rot * (GLA_DK ** -0.5)
        qk_ref[0, :, s * LANES:(s + 1) * LANES] = rot
    v_ref[0] = _dot(h, w_ref[:, v0:r0]).astype(BF16)
    r_ref[0] = _dot(h, w_ref[:, r0:l0])
    lr = _dot(h, w_ref[:, l0:l0 + 2 * GLA_RANK])
    z = jnp.dot(lr, gw_ref[...], precision=HIGHEST, preferred_element_type=F32) + gb_ref[...]
    gate_ref[0] = (jnp.minimum(z, 0.0) - jnp.log1p(jnp.exp(-jnp.abs(z)))) * (1.0 / GLA_TAU)


def _odd_in(x, g, shift, scale, w_bf, cos_t, sin_t, gate_w, gate_b, tm):
    b, l, d = x.shape
    n = w_bf.shape[1]
    qk = GLA_HEADS * GLA_DK
    vd = GLA_HEADS * GLA_DV
    gw = jnp.zeros((2 * GLA_RANK, 2 * qk), F32)
    gw = gw.at[:GLA_RANK, :qk].set(gate_w[0]).at[GLA_RANK:, qk:].set(gate_w[1])
    gb = jnp.concatenate([gate_b[0], gate_b[1]]).reshape(1, 2 * qk)
    vec = pl.BlockSpec((1, 1, d), lambda i, j: (i, 0, 0))
    row = lambda w: pl.BlockSpec((1, tm, w), lambda i, j: (i, j, 0))
    return pl.pallas_call(
        _odd_in_body,
        grid=(b, l // tm),
        in_specs=[row(d), pl.BlockSpec((1, d), lambda i, j: (0, 0)), vec, vec,
                  pl.BlockSpec((d, n), lambda i, j: (0, 0)),
                  pl.BlockSpec((tm, qk), lambda i, j: (j, 0)),
                  pl.BlockSpec((tm, qk), lambda i, j: (j, 0)),
                  pl.BlockSpec((2 * GLA_RANK, 2 * qk), lambda i, j: (0, 0)),
                  pl.BlockSpec((1, 2 * qk), lambda i, j: (0, 0))],
        out_specs=[row(POOL_CH), row(2 * qk), row(vd), row(vd), row(2 * qk)],
        out_shape=[jax.ShapeDtypeStruct((b, l, POOL_CH), F32),
                   jax.ShapeDtypeStruct((b, l, 2 * qk), F32),
                   jax.ShapeDtypeStruct((b, l, vd), BF16),
                   jax.ShapeDtypeStruct((b, l, vd), F32),
                   jax.ShapeDtypeStruct((b, l, 2 * qk), F32)],
        compiler_params=_cparams(("parallel", "parallel")),
        name="odd_in",
    )(x, g.reshape(1, d), shift.reshape(b, 1, d), scale.reshape(b, 1, d), w_bf, cos_t, sin_t, gw, gb)


def _rope_tables(l):
    t = jnp.arange(l)
    pos_r = (t // GRID_W).astype(F32)
    pos_c = (t % GRID_W).astype(F32)
    nf = GLA_DK // 4
    inv = jnp.power(ROPE_BASE, -jnp.arange(nf, dtype=F32) / nf)
    ar = pos_r[:, None] * inv[None, :]
    ac = pos_c[:, None] * inv[None, :]
    cos_h = jnp.concatenate([jnp.cos(ar), jnp.cos(ar), jnp.cos(ac), jnp.cos(ac)], axis=-1)
    sin_h = jnp.concatenate([-jnp.sin(ar), jnp.sin(ar), -jnp.sin(ac), jnp.sin(ac)], axis=-1)
    return jnp.tile(cos_h, (1, GLA_HEADS)), jnp.tile(sin_h, (1, GLA_HEADS))


def _gla_tile(qk, v, g, s, reverse):
    hk = GLA_HEADS * GLA_DK
    hv = GLA_HEADS * GLA_DV
    c = GLA_CHUNK
    t = qk.shape[0]
    n = t // c
    last_row, mid_row = (0, c // 2) if reverse else (c - 1, c // 2 - 1)
    ii = lax.broadcasted_iota(jnp.int32, (t, t), 0)
    jj = lax.broadcasted_iota(jnp.int32, (t, t), 1)
    ordered = (jj >= ii) if reverse else (jj <= ii)
    tri = jnp.where(ordered & (ii // c == jj // c), 1.0, 0.0).astype(BF16)
    g_hi = g.astype(BF16)
    rem = g - g_hi.astype(F32)
    g_mid = rem.astype(BF16)
    g_lo = (rem - g_mid.astype(F32)).astype(BF16)
    bc = _dot(tri, g_hi) + (_dot(tri, g_mid) + _dot(tri, g_lo))
    spread = lambda row: jnp.concatenate(
        [jnp.broadcast_to(bc[i * c + row:i * c + row + 1, :], (c, hk)) for i in range(n)], axis=0)
    b_mid = spread(mid_row)
    b_last = spread(last_row)
    qt = qk[:, 0:hk] * jnp.exp(bc - b_mid)
    kt = qk[:, hk:2 * hk] * jnp.exp(b_mid - bc)
    qe = (qt * jnp.exp(b_mid)).astype(BF16)
    ke = kt * jnp.exp(b_last - b_mid)
    ktb = kt.astype(BF16)
    lane = lax.broadcasted_iota(jnp.int32, (1, hk), 1)
    ci = lax.broadcasted_iota(jnp.int32, (c, c), 0)
    cj = lax.broadcasted_iota(jnp.int32, (c, c), 1)
    causal = (cj >= ci) if reverse else (cj <= ci)
    blockdiag = (lax.broadcasted_iota(jnp.int32, (hk, hv), 0) // GLA_DK
                 == lax.broadcasted_iota(jnp.int32, (hk, hv), 1) // GLA_DV)
    intra, upd, decay = [], [], []
    for i in range(n):
        rows = slice(i * c, (i + 1) * c)
        qs = jnp.concatenate(
            [jnp.where((lane >= h * GLA_DK) & (lane < (h + 1) * GLA_DK), qt[rows], 0.0) for h in range(GLA_HEADS)],
            axis=0).astype(BF16)
        att = _dot_nt(qs, ktb[rows])
        intra.append(jnp.concatenate(
            [_dot(jnp.where(causal, att[h * c:(h + 1) * c], 0.0).astype(BF16),
                  v[rows, h * GLA_DV:(h + 1) * GLA_DV]) for h in range(GLA_HEADS)], axis=-1))
        upd.append(jnp.where(blockdiag, _dot(ke[rows].T.astype(BF16), v[rows]), 0.0))
        decay.append(jnp.exp(jnp.sum(g[rows].T, axis=1, keepdims=True)))
    outs = [None] * n
    for i in (reversed(range(n)) if reverse else range(n)):
        rows = slice(i * c, (i + 1) * c)
        outs[i] = _dot(qe[rows], s.astype(BF16)) + intra[i]
        s = decay[i] * s + upd[i]
    return jnp.concatenate(outs, axis=0), s


def _gla_body(qkf_ref, qkb_ref, vf_ref, vb_ref, gf_ref, gb_ref, s0f_ref, s0b_ref,
              of_ref, ob_ref, sff_ref, sbf_ref, sf_ref, sb_ref, *, tile):
    n = pl.program_id(1)
    hk = GLA_HEADS * GLA_DK
    hv = GLA_HEADS * GLA_DV

    @pl.when(n == 0)
    def _():
        sf_ref[...] = jnp.zeros((hk, hv), F32)
        sb_ref[...] = jnp.zeros((hk, hv), F32)
        for h in range(GLA_HEADS):
            sf_ref[h * GLA_DK:(h + 1) * GLA_DK, h * GLA_DV:(h + 1) * GLA_DV] = s0f_ref[0, h]
            sb_ref[h * GLA_DK:(h + 1) * GLA_DK, h * GLA_DV:(h + 1) * GLA_DV] = s0b_ref[0, h]

    of_ref[0], sf_ref[...] = _gla_tile(qkf_ref[0], vf_ref[0], gf_ref[0], sf_ref[...], False)
    ob_ref[0], sb_ref[...] = _gla_tile(qkb_ref[0], vb_ref[0], gb_ref[0], sb_ref[...], True)

    @pl.when(n == pl.num_programs(1) - 1)
    def _():
        for h in range(GLA_HEADS):
            sff_ref[0, h] = sf_ref[h * GLA_DK:(h + 1) * GLA_DK, h * GLA_DV:(h + 1) * GLA_DV]
            sbf_ref[0, h] = sb_ref[h * GLA_DK:(h + 1) * GLA_DK, h * GLA_DV:(h + 1) * GLA_DV]


def _gla(qk, v, gates, s0f, s0b, tile):
    b, l, _ = qk.shape
    hk = GLA_HEADS * GLA_DK
    hv = GLA_HEADS * GLA_DV
    nt = l // tile
    fwd = lambda w, col: pl.BlockSpec((1, tile, w), lambda i, n: (i, n, col))
    bwd = lambda w, col: pl.BlockSpec((1, tile, w), lambda i, n: (i, nt - 1 - n, col))
    st = pl.BlockSpec((1, GLA_HEADS, GLA_DK, GLA_DV), lambda i, n: (i, 0, 0, 0))
    return pl.pallas_call(
        functools.partial(_gla_body, tile=tile),
        grid=(b, nt),
        in_specs=[fwd(2 * hk, 0), bwd(2 * hk, 0), fwd(hv, 0), bwd(hv, 0), fwd(hk, 0), bwd(hk, 1), st, st],
        out_specs=[fwd(hv, 0), bwd(hv, 0), st, st],
        out_shape=[jax.ShapeDtypeStruct((b, l, hv), F32), jax.ShapeDtypeStruct((b, l, hv), F32),
                   jax.ShapeDtypeStruct((b, GLA_HEADS, GLA_DK, GLA_DV), F32),
                   jax.ShapeDtypeStruct((b, GLA_HEADS, GLA_DK, GLA_DV), F32)],
        scratch_shapes=[pltpu.VMEM((hk, hv), F32), pltpu.VMEM((hk, hv), F32)],
        compiler_params=_cparams(("parallel", "arbitrary")),
        name="gla_scan",
    )(qk, qk, v, v, gates, gates, s0f, s0b)


def _odd_mid_body(cur_ref, prev_ref, next_ref, of_ref, ob_ref, r_ref, hg_ref, pw_ref, ps_ref,
                  pool_ref, d_ref, buf_ref, *, tile, seq):
    j = pl.program_id(1)
    last = pl.num_programs(1) - 1
    hal = POOL_HALO
    buf_ref[0:hal, :] = jnp.where(j == 0, 0.0, prev_ref[0])
    buf_ref[hal:hal + tile, :] = cur_ref[0]
    buf_ref[hal + tile:hal + tile + hal, :] = jnp.where(j == last, 0.0, next_ref[0])
    t = j * tile + lax.broadcasted_iota(jnp.int32, (tile, 1), 0)
    for gi, win in enumerate(POOL_WINDOWS):
        cols = slice(gi * POOL_GROUP, (gi + 1) * POOL_GROUP)
        acc = jnp.zeros((tile, POOL_GROUP), F32)
        for off in range(-(win // 2), win - win // 2):
            acc = acc + buf_ref[hal + off:hal + off + tile, cols]
        cnt = jnp.minimum(t + (win - win // 2), seq) - jnp.maximum(t - win // 2, 0)
        diff = acc / cnt.astype(F32) - cur_ref[0, :, cols]
        pool_ref[0, :, cols] = _dot(diff.astype(BF16), pw_ref[gi]) * ps_ref[:, cols]
    for h in range(GLA_HEADS):
        cols = slice(h * GLA_DV, (h + 1) * GLA_DV)
        o = of_ref[0, :, cols] + ob_ref[0, :, cols]
        d_ref[0, :, cols] = _rms(o, hg_ref[:, cols]) * _silu(r_ref[0, :, cols])


def _odd_mid(pool_u, o_f, o_b, r, head_g, pool_w_bf, pool_scale, tile):
    b, l, c = pool_u.shape
    hal = POOL_HALO
    per = tile // hal
    nh = l // hal
    blk = pl.BlockSpec((1, tile, c), lambda i, j: (i, j, 0))
    vec = pl.BlockSpec((1, c), lambda i, j: (0, 0))
    return pl.pallas_call(
        functools.partial(_odd_mid_body, tile=tile, seq=l),
        grid=(b, l // tile),
        in_specs=[blk,
                  pl.BlockSpec((1, hal, c), lambda i, j: (i, jnp.maximum(j * per - 1, 0), 0)),
                  pl.BlockSpec((1, hal, c), lambda i, j: (i, jnp.minimum((j + 1) * per, nh - 1), 0)),
                  blk, blk, blk, vec,
                  pl.BlockSpec((len(POOL_WINDOWS), POOL_GROUP, POOL_GROUP), lambda i, j: (0, 0, 0)),
                  vec],
        out_specs=[blk, blk],
        out_shape=[jax.ShapeDtypeStruct((b, l, c), F32), jax.ShapeDtypeStruct((b, l, c), F32)],
        scratch_shapes=[pltpu.VMEM((tile + 2 * hal, c), F32)],
        compiler_params=_cparams(("parallel", "parallel")),
        name="odd_mid",
    )(pool_u, pool_u, pool_u, o_f, o_b, r, head_g.reshape(1, c), pool_w_bf, pool_scale.reshape(1, c))


def _route(aff_t, tok_base, row_base):
    b, e, n = aff_t.shape
    cap = EC_CAPACITY_FACTOR * n // N_EXPERTS
    vals, idx = lax.top_k(aff_t, cap)
    bi = jnp.arange(b, dtype=idx.dtype)[:, None, None]
    per_expert = lambda a: jnp.swapaxes(a, 0, 1).reshape(e, b * cap)
    rows0 = idx + row_base + 2 * bi * n
    return per_expert(vals), per_expert(idx + tok_base + bi * n), per_expert(rows0), per_expert(rows0 + n)


def _gather_rows(src, idx):
    window = SC_GATHER_WINDOW
    n = idx.shape[0]
    width = src.shape[1]
    assert 2 * window * width * 4 <= SC_TILE_VMEM_BUDGET, width
    assert n % (window * SC_CORES * SC_SUBCORES) == 0, n
    mesh = plsc.VectorSubcoreMesh(core_axis_name="core", subcore_axis_name="subcore",
                                  num_cores=SC_CORES, num_subcores=SC_SUBCORES)

    @pl.kernel(out_type=jax.ShapeDtypeStruct((n, width), src.dtype), mesh=mesh, scratch_types=[],
               name="gather_rows")
    def gather(src_hbm, idx_hbm, out_hbm):
        def body(idx_vmem, out_vmem):
            pltpu.sync_copy(src_hbm.at[idx_vmem.at[0]], out_vmem)

        pltpu.emit_pipeline(
            body,
            grid=(n // window,),
            in_specs=[pl.BlockSpec((1, window), lambda i: (0, i))],
            out_specs=[pl.BlockSpec((window, width), lambda i: (i, 0))],
            core_axis_name=("core", "subcore"),
            dimension_semantics=(pltpu.PARALLEL,),
        )(idx_hbm, out_hbm)

    return gather(src, idx.reshape(1, n))


def _moe(parts, w_gate, w_up, w_down, layer):
    quarter = parts[0][1].shape[-1]
    d = 4 * quarter
    sizes = [h.shape[0] * h.shape[2] for _, h in parts]
    bases = [sum(sizes[:i]) for i in range(len(parts))]
    routed = [_route(a, base, 2 * base) for (a, _), base in zip(parts, bases)]
    src = jnp.concatenate([h.reshape(-1, quarter) for _, h in parts], axis=0)
    vals, flat, rows0, rows1 = (jnp.concatenate([r[i] for r in routed], axis=1) for i in range(4))
    e, m = flat.shape
    pad = -m % LANES
    vals, flat, rows0, rows1 = (jnp.pad(a, ((0, 0), (0, pad))) for a in (vals, flat, rows0, rows1))
    m += pad
    rows = jnp.stack([rows0, rows1], axis=1)
    xg = _gather_rows(src, rows.reshape(-1)).reshape(e, 2, m, quarter)
    chunk = next(c for c in (512, 544, 384, 256, 128) if m % c == 0)
    y = _expert_ffn(xg, vals[..., None], w_gate, w_up, w_down, layer, chunk)
    out = jnp.zeros((sum(sizes), d), F32).at[flat.reshape(-1)].add(y.reshape(-1, d))
    return [out[base:base + size].reshape(h.shape[0], h.shape[2], d)
            for (_, h), base, size in zip(parts, bases, sizes)]


def kernel(x, c, ctx, c_ctx, w_mod, b_mod, norm_g, w_in_even, w_out_even, conv_w, conv_b, conv_ln_g,
           conv_ln_b, na_rpb, w_in_odd, w_out_odd, pool_w, pool_scale, gla_gate_w, gla_gate_b, gla_head_g,
           router_w, expert_w_gate, expert_w_up, expert_w_down):
    b, l, d = x.shape
    n_ctx = ctx.shape[1]
    tm = 512

    mod_rows = jnp.concatenate([c, c_ctx[None], jnp.zeros((8 - b - 1, d), F32)], axis=0)

    mod_all = _modulation(mod_rows, w_mod, b_mod)

    def modulation(i):
        mm = mod_all[i]
        m = mm[:b].reshape(b, 6, d)
        mc = jnp.broadcast_to(mm[b].reshape(1, 6, d), (b, 6, d))
        return m, mc

    m, mc = modulation(0)
    g = norm_g[0]
    w_in = w_in_even[0].astype(BF16)
    w_out = w_out_even[0].astype(BF16)
    glu, qkv = _even_in(x, g[0], m[:, 0], m[:, 1], w_in, tm)
    glu_c, qkv_c = _even_in(ctx, g[0], mc[:, 0], mc[:, 1], w_in, n_ctx)
    a_lat = _conv_branch(glu, conv_w[0], conv_b[0], conv_ln_g[0], conv_ln_b[0], 256)
    a_ctx = _conv_branch(glu_c, conv_w[0], conv_b[0], conv_ln_g[0], conv_ln_b[0], n_ctx)
    na = _neighbourhood_attention(qkv, qkv_c, na_rpb[0])
    att_c = _context_attention(qkv_c)
    x, h2, aff = _out_proj(a_lat, na, x, w_out, g[1], m[:, 2], g[2], m[:, 3], m[:, 4], router_w[0], tm)
    ctx, h2c, aff_c = _out_proj(a_ctx, att_c, ctx, w_out, g[1], mc[:, 2], g[2], mc[:, 3], mc[:, 4],
                                router_w[0], n_ctx)
    f, f_c = _moe([(aff, h2), (aff_c, h2c)], expert_w_gate, expert_w_up, expert_w_down, 0)
    x = _gated_residual(x, f, m[:, 5], g[3], tm)
    ctx = _gated_residual(ctx, f_c, mc[:, 5], g[3], n_ctx)

    m, mc = modulation(1)
    g = norm_g[1]
    w_in = w_in_odd[0].astype(BF16)
    w_out = w_out_odd[0].astype(BF16)
    cos_t, sin_t = _rope_tables(l)
    ones_t = jnp.ones((n_ctx, GLA_HEADS * GLA_DK), F32)
    _, qk_c, v_c, _, gate_c = _odd_in(ctx, g[0], mc[:, 0], mc[:, 1], w_in, ones_t, jnp.zeros_like(ones_t),
                                      gla_gate_w[0], gla_gate_b[0], n_ctx)
    s_zero = jnp.zeros((b, GLA_HEADS, GLA_DK, GLA_DV), F32)
    _, _, s_f, s_b = _gla(qk_c, v_c, gate_c, s_zero, s_zero, n_ctx)
    pool_u, qk, v, r, gate = _odd_in(x, g[0], m[:, 0], m[:, 1], w_in, cos_t, sin_t,
                                     gla_gate_w[0], gla_gate_b[0], tm)
    o_f, o_b, _, _ = _gla(qk, v, gate, s_f, s_b, 256)
    pool_y, d_lat = _odd_mid(pool_u, o_f, o_b, r, gla_head_g[0], pool_w[0].astype(BF16), pool_scale[0], 256)
    x, h2, aff = _out_proj(pool_y, d_lat, x, w_out, g[1], m[:, 2], g[2], m[:, 3], m[:, 4], router_w[1], tm)
    (f,) = _moe([(aff, h2)], expert_w_gate, expert_w_up, expert_w_down, 1)
    return _gated_residual(x, f, m[:, 5], g[3], tm)
```

```python
import functools

import jax
import jax.numpy as jnp
from jax import lax
from jax.experimental import pallas as pl
from jax.experimental.pallas import tpu as pltpu
from jax.experimental.pallas import tpu_sc as plsc

F32 = jnp.float32
BF16 = jnp.bfloat16
HIGHEST = lax.Precision.HIGHEST

D_MODEL = 1024
GRID_W = 64
EPS = 1e-6
CONV_CH = 512
CONV_WIDTH = 31
CONV_HALO = 16
NA_HEADS = 8
NA_HEAD_DIM = 64
NA_KR = 8
NA_KC = 16
NA_ROWS_PER_BLOCK = 4
NA_WIN_ROWS = 12
POOL_CH = 512
POOL_WINDOWS = (2, 4, 8, 16)
POOL_GROUP = 128
POOL_HALO = 8
GLA_HEADS = 4
GLA_DK = 64
GLA_DV = 128
GLA_RANK = 16
GLA_TAU = 16.0
GLA_CHUNK = 64
ROPE_BASE = 10000.0
N_EXPERTS = 16
EXPERT_FF = 2816
EC_CAPACITY_FACTOR = 2
LANES = 128
SUBLANES = 8
NEG_BIG = -1e30
VMEM_LIMIT = 56 * 1024 * 1024
SC_CORES = 2
SC_SUBCORES = 16
SC_TILE_VMEM_BUDGET = 400 * 1024
SC_GATHER_WINDOW = 128
COMBINE_TILE = 256
COMBINE_WINDOW = 128
COMBINE_COLS = 512
SLOT_ALIGN = 16


def _cparams(sem):
    return pltpu.CompilerParams(dimension_semantics=sem, vmem_limit_bytes=VMEM_LIMIT)


def _rms(x, g):
    return x * lax.rsqrt(jnp.mean(x * x, axis=-1, keepdims=True) + EPS) * g


def _sigmoid(x):
    return 1.0 / (1.0 + jnp.exp(-x))


def _silu(x):
    return x * _sigmoid(x)


def _dot(a, b):
    return jnp.dot(a, b, preferred_element_type=F32)


def _pack_bf16_pairs(h):
    half = h.shape[-1] // 2
    bits = lax.bitcast_convert_type(h.astype(BF16).astype(F32), jnp.uint32)
    packed = (bits[:, half:] & jnp.uint32(0xFFFF0000)) | (bits[:, :half] >> 16)
    return lax.bitcast_convert_type(packed, jnp.int32)


def _unpack_bf16_pairs(p):
    bits = lax.bitcast_convert_type(p, jnp.uint32)
    lo = lax.bitcast_convert_type(bits << 16, F32).astype(BF16)
    hi = lax.bitcast_convert_type(bits & jnp.uint32(0xFFFF0000), F32).astype(BF16)
    return lo, hi


def _dot_nt(a, b):
    return lax.dot_general(a, b, (((1,), (1,)), ((), ())), preferred_element_type=F32)


def _mod_body(c_ref, w_ref, b_ref, o_ref):
    o_ref[0] = jnp.dot(_silu(c_ref[...]), w_ref[0], precision=HIGHEST,
                       preferred_element_type=F32) + b_ref[0]


def _modulation(rows, w, b):
    depth, _, n = w.shape
    tn = 1536
    return pl.pallas_call(
        _mod_body,
        grid=(depth, n // tn),
        in_specs=[pl.BlockSpec((8, D_MODEL), lambda i, j: (0, 0)),
                  pl.BlockSpec((1, D_MODEL, tn), lambda i, j: (i, 0, j)),
                  pl.BlockSpec((1, 1, tn), lambda i, j: (i, 0, j))],
        out_specs=pl.BlockSpec((1, 8, tn), lambda i, j: (i, 0, j)),
        out_shape=jax.ShapeDtypeStruct((depth, 8, n), F32),
        compiler_params=_cparams(("parallel", "parallel")),
        name="modulation",
    )(rows, w, b.reshape(depth, 1, n))


def _even_in_body(x_ref, g_ref, sh_ref, sc_ref, w_ref, glu_ref, qkv_ref):
    h = (_rms(x_ref[0], g_ref[...]) * (1.0 + sc_ref[0]) + sh_ref[0]).astype(BF16)
    c = CONV_CH
    glu_ref[0] = _dot(h, w_ref[:, 0:c]) * _sigmoid(_dot(h, w_ref[:, c:2 * c]))
    hd = NA_HEADS * NA_HEAD_DIM
    q0 = 2 * c
    qkv_ref[0, :, 0:hd] = (_dot(h, w_ref[:, q0:q0 + hd]) * (NA_HEAD_DIM ** -0.5)).astype(BF16)
    qkv_ref[0, :, hd:3 * hd] = _dot(h, w_ref[:, q0 + hd:q0 + 3 * hd]).astype(BF16)


def _even_in(x, g, shift, scale, w_bf, tm):
    b, l, d = x.shape
    n = w_bf.shape[1]
    hd3 = 3 * NA_HEADS * NA_HEAD_DIM
    vec = pl.BlockSpec((1, 1, d), lambda i, j: (i, 0, 0))
    return pl.pallas_call(
        _even_in_body,
        grid=(b, l // tm),
        in_specs=[pl.BlockSpec((1, tm, d), lambda i, j: (i, j, 0)),
                  pl.BlockSpec((1, d), lambda i, j: (0, 0)),
                  vec, vec,
                  pl.BlockSpec((d, n), lambda i, j: (0, 0))],
        out_specs=[pl.BlockSpec((1, tm, CONV_CH), lambda i, j: (i, j, 0)),
                   pl.BlockSpec((1, tm, hd3), lambda i, j: (i, j, 0))],
        out_shape=[jax.ShapeDtypeStruct((b, l, CONV_CH), F32),
                   jax.ShapeDtypeStruct((b, l, hd3), BF16)],
        compiler_params=_cparams(("parallel", "parallel")),
        name="even_in",
    )(x, g.reshape(1, d), shift.reshape(b, 1, d), scale.reshape(b, 1, d), w_bf)


def _conv_body(cur_ref, prev_ref, next_ref, w_ref, b_ref, lg_ref, lb_ref, o_ref, buf_ref, sh_ref, *, tile, chunk):
    j = pl.program_id(1)
    last = pl.num_programs(1) - 1
    hal = CONV_HALO
    buf_ref[0:hal, :] = jnp.where(j == 0, 0.0, prev_ref[0])
    buf_ref[hal:hal + tile, :] = cur_ref[0]
    buf_ref[hal + tile:hal + tile + hal, :] = jnp.where(j == last, 0.0, next_ref[0])
    span = sh_ref.shape[1]
    for s in range(SUBLANES):
        sh_ref[s] = buf_ref[s:s + span, :]
    first = hal - CONV_WIDTH // 2
    reps = chunk // SUBLANES

    def rows(c, carry):
        r0 = pl.multiple_of(c * chunk, chunk)
        accs = [jnp.zeros((chunk, CONV_CH), F32) for _ in range(2)]
        for k in range(CONV_WIDTH):
            a, s = divmod(first + k, SUBLANES)
            wk = jnp.concatenate([w_ref[k]] * reps, axis=0)
            accs[k % 2] = accs[k % 2] + sh_ref[s, pl.ds(r0 + a * SUBLANES, chunk), :] * wk
        o_ref[0, pl.ds(r0, chunk), :] = accs[0] + accs[1]
        return carry

    lax.fori_loop(0, tile // chunk, rows, 0)
    y = o_ref[0] + b_ref[...]
    mu = jnp.mean(y, axis=-1, keepdims=True)
    yc = y - mu
    var = jnp.mean(yc * yc, axis=-1, keepdims=True)
    o_ref[0] = _silu(yc * lax.rsqrt(var + EPS) * lg_ref[...] + lb_ref[...])


def _conv_branch(glu, conv_w, conv_b, ln_g, ln_b, tile):
    b, l, c = glu.shape
    hal = CONV_HALO
    per = tile // hal
    nh = l // hal
    vec = pl.BlockSpec((1, c), lambda i, j: (0, 0))
    return pl.pallas_call(
        functools.partial(_conv_body, tile=tile, chunk=32),
        grid=(b, l // tile),
        in_specs=[pl.BlockSpec((1, tile, c), lambda i, j: (i, j, 0)),
                  pl.BlockSpec((1, hal, c), lambda i, j: (i, jnp.maximum(j * per - 1, 0), 0)),
                  pl.BlockSpec((1, hal, c), lambda i, j: (i, jnp.minimum((j + 1) * per, nh - 1), 0)),
                  pl.BlockSpec((CONV_WIDTH, SUBLANES, c), lambda i, j: (0, 0, 0)),
                  vec, vec, vec],
        out_specs=pl.BlockSpec((1, tile, c), lambda i, j: (i, j, 0)),
        out_shape=jax.ShapeDtypeStruct((b, l, c), F32),
        scratch_shapes=[pltpu.VMEM((tile + 2 * hal, c), F32),
                        pltpu.VMEM((SUBLANES, tile + 2 * hal - SUBLANES, c), F32)],
        compiler_params=_cparams(("parallel", "parallel")),
        name="conv_branch",
    )(glu, glu, glu, jnp.broadcast_to(conv_w[:, None, :], (CONV_WIDTH, SUBLANES, c)),
      conv_b.reshape(1, c), ln_g.reshape(1, c), ln_b.reshape(1, c))


def _na_window_start(j, rows):
    rb = NA_ROWS_PER_BLOCK
    return jnp.clip(j * rb - NA_KR // 2, 0, rows - NA_WIN_ROWS)


def _na_body(q_ref, k_ref, v_ref, kc_ref, vc_ref, tab_ref, o_ref, *, rows):
    j = pl.program_id(2)
    nkeys = NA_WIN_ROWS * GRID_W
    start = pl.multiple_of(_na_window_start(j, rows) * GRID_W, GRID_W)
    q = q_ref[0]
    kw = k_ref[0, pl.ds(start, nkeys), :]
    vw = v_ref[0, pl.ds(start, nkeys), :]
    kc = kc_ref[0]
    vc = vc_ref[0]
    lane = lax.broadcasted_iota(jnp.int32, (1, LANES), 1)
    out = jnp.zeros(q.shape, F32)
    for hh in range(LANES // NA_HEAD_DIM):
        in_head = (lane >= hh * NA_HEAD_DIM) & (lane < (hh + 1) * NA_HEAD_DIM)
        qh = jnp.where(in_head, q, jnp.zeros_like(q))
        s = _dot_nt(qh, kw) + tab_ref[0, hh]
        sc = _dot_nt(qh, kc)
        m = jnp.maximum(jnp.max(s, axis=-1, keepdims=True), jnp.max(sc, axis=-1, keepdims=True))
        p = jnp.exp(s - m)
        pc = jnp.exp(sc - m)
        denom = jnp.sum(p, axis=-1, keepdims=True) + jnp.sum(pc, axis=-1, keepdims=True)
        o = (_dot(p.astype(BF16), vw) + _dot(pc.astype(BF16), vc)) / denom
        out = jnp.where(in_head, o, out)
    o_ref[0] = out


def _na_tables(rpb, rows):
    rb = NA_ROWS_PER_BLOCK
    nblk = rows // rb
    wr = NA_WIN_ROWS
    qc = jnp.arange(GRID_W)
    cs = jnp.clip(qc - NA_KC // 2, 0, GRID_W - NA_KC)
    col_ok = (qc[None, :] >= cs[:, None]) & (qc[None, :] < cs[:, None] + NA_KC)
    col_off = qc[None, :] - qc[:, None] + NA_KC - 1
    onehot = (col_off[:, :, None] == jnp.arange(2 * NA_KC - 1)[None, None, :]).astype(F32)
    blocks = jnp.einsum('hrd,qkd->hrqk', rpb.astype(F32), onehot, precision=HIGHEST)
    blocks = jnp.where(col_ok[None, None], blocks, NEG_BIG)
    masked = jnp.full((NA_HEADS, GRID_W, GRID_W), NEG_BIG, F32)
    tabs = []
    for jb in (0, 1, nblk - 1):
        ws = min(max(jb * rb - NA_KR // 2, 0), rows - wr)
        q_rows = []
        for qr in range(jb * rb, (jb + 1) * rb):
            rs = min(max(qr - NA_KR // 2, 0), rows - NA_KR)
            row = [blocks[:, kr - qr + NA_KR - 1] if rs <= kr < rs + NA_KR else masked
                   for kr in range(ws, ws + wr)]
            q_rows.append(jnp.concatenate(row, axis=-1))
        tabs.append(jnp.concatenate(q_rows, axis=1))
    return jnp.stack(tabs)


def _neighbourhood_attention(qkv, qkv_ctx, rpb):
    b, l, _ = qkv.shape
    n_ctx = qkv_ctx.shape[1]
    rows = l // GRID_W
    rb = NA_ROWS_PER_BLOCK
    nblk = rows // rb
    tq = rb * GRID_W
    nkeys = NA_WIN_ROWS * GRID_W
    hp = NA_HEADS * NA_HEAD_DIM // LANES
    tabs = _na_tables(rpb, rows)

    def cls(j):
        return jnp.where(j == 0, 0, jnp.where(j == nblk - 1, 2, 1))

    return pl.pallas_call(
        functools.partial(_na_body, rows=rows),
        grid=(b, hp, nblk),
        in_specs=[pl.BlockSpec((1, tq, LANES), lambda i, h, j: (i, j, h)),
                  pl.BlockSpec((1, l, LANES), lambda i, h, j: (i, 0, hp + h)),
                  pl.BlockSpec((1, l, LANES), lambda i, h, j: (i, 0, 2 * hp + h)),
                  pl.BlockSpec((1, n_ctx, LANES), lambda i, h, j: (i, 0, hp + h)),
                  pl.BlockSpec((1, n_ctx, LANES), lambda i, h, j: (i, 0, 2 * hp + h)),
                  pl.BlockSpec((1, 2, tq, nkeys), lambda i, h, j: (cls(j), h, 0, 0))],
        out_specs=pl.BlockSpec((1, tq, LANES), lambda i, h, j: (i, j, h)),
        out_shape=jax.ShapeDtypeStruct((b, l, NA_HEADS * NA_HEAD_DIM), F32),
        compiler_params=_cparams(("parallel", "parallel", "arbitrary")),
        name="neighbourhood_attention",
    )(qkv, qkv, qkv, qkv_ctx, qkv_ctx, tabs)


def _ctx_attn_body(q_ref, k_ref, v_ref, o_ref):
    q = q_ref[0]
    k = k_ref[0]
    v = v_ref[0]
    lane = lax.broadcasted_iota(jnp.int32, (1, LANES), 1)
    out = jnp.zeros(q.shape, F32)
    for hh in range(LANES // NA_HEAD_DIM):
        in_head = (lane >= hh * NA_HEAD_DIM) & (lane < (hh + 1) * NA_HEAD_DIM)
        qh = jnp.where(in_head, q, jnp.zeros_like(q))
        s = _dot_nt(qh, k)
        p = jnp.exp(s - jnp.max(s, axis=-1, keepdims=True))
        o = _dot(p.astype(BF16), v) / jnp.sum(p, axis=-1, keepdims=True)
        out = jnp.where(in_head, o, out)
    o_ref[0] = out


def _context_attention(qkv_ctx):
    b, n, _ = qkv_ctx.shape
    hp = NA_HEADS * NA_HEAD_DIM // LANES
    return pl.pallas_call(
        _ctx_attn_body,
        grid=(b, hp),
        in_specs=[pl.BlockSpec((1, n, LANES), lambda i, h: (i, 0, h)),
                  pl.BlockSpec((1, n, LANES), lambda i, h: (i, 0, hp + h)),
                  pl.BlockSpec((1, n, LANES), lambda i, h: (i, 0, 2 * hp + h))],
        out_specs=pl.BlockSpec((1, n, LANES), lambda i, h: (i, 0, h)),
        out_shape=jax.ShapeDtypeStruct((b, n, NA_HEADS * NA_HEAD_DIM), F32),
        compiler_params=_cparams(("parallel", "parallel")),
        name="context_attention",
    )(qkv_ctx, qkv_ctx, qkv_ctx)


def _out_body(a_ref, b_ref, x_ref, w_ref, g1_ref, gate_ref, g2_ref, sh_ref, sc_ref, rwh_ref, rwl_ref,
              xo_ref, h_ref, aff_ref):
    half = a_ref.shape[-1]
    y = _dot(a_ref[0].astype(BF16), w_ref[0:half, :]) + _dot(b_ref[0].astype(BF16), w_ref[half:2 * half, :])
    xn = x_ref[0] + gate_ref[0] * _rms(y, g1_ref[...])
    xo_ref[0] = xn
    h = _rms(xn, g2_ref[...]) * (1.0 + sc_ref[0]) + sh_ref[0]
    packed = _pack_bf16_pairs(h)
    quarter = packed.shape[-1] // 2
    h_ref[0, 0] = packed[:, 0:quarter]
    h_ref[0, 1] = packed[:, quarter:2 * quarter]
    h_hi = h.astype(BF16)
    h_lo = (h - h_hi.astype(F32)).astype(BF16)
    logits = _dot(h_hi, rwh_ref[...]) + (_dot(h_lo, rwh_ref[...]) + _dot(h_hi, rwl_ref[...]))
    lane = lax.broadcasted_iota(jnp.int32, (1, LANES), 1)
    logits = jnp.where(lane < N_EXPERTS, logits, NEG_BIG)
    e = jnp.exp(logits - jnp.max(logits, axis=-1, keepdims=True))
    aff = e / jnp.sum(e, axis=-1, keepdims=True)
    aff_ref[0] = aff.T[0:N_EXPERTS, :]


def _out_proj(a, b2, x, w_bf, g1, gate, g2, shift, scale, router_w, tm):
    b, l, d = x.shape
    half = a.shape[-1]
    rw = jnp.pad(router_w, ((0, 0), (0, LANES - N_EXPERTS)))
    rw_hi = rw.astype(BF16)
    rw_lo = (rw - rw_hi.astype(F32)).astype(BF16)
    vec = pl.BlockSpec((1, d), lambda i, j: (0, 0))
    bvec = pl.BlockSpec((1, 1, d), lambda i, j: (i, 0, 0))
    rspec = pl.BlockSpec((d, LANES), lambda i, j: (0, 0))
    return pl.pallas_call(
        _out_body,
        grid=(b, l // tm),
        in_specs=[pl.BlockSpec((1, tm, half), lambda i, j: (i, j, 0)),
                  pl.BlockSpec((1, tm, half), lambda i, j: (i, j, 0)),
                  pl.BlockSpec((1, tm, d), lambda i, j: (i, j, 0)),
                  pl.BlockSpec((2 * half, d), lambda i, j: (0, 0)),
                  vec, bvec, vec, bvec, bvec, rspec, rspec],
        out_specs=[pl.BlockSpec((1, tm, d), lambda i, j: (i, j, 0)),
                   pl.BlockSpec((1, 2, tm, d // 4), lambda i, j: (i, 0, j, 0)),
                   pl.BlockSpec((1, N_EXPERTS, tm), lambda i, j: (i, 0, j))],
        out_shape=[jax.ShapeDtypeStruct((b, l, d), F32),
                   jax.ShapeDtypeStruct((b, 2, l, d // 4), jnp.int32),
                   jax.ShapeDtypeStruct((b, N_EXPERTS, l), F32)],
        compiler_params=_cparams(("parallel", "parallel")),
        name="out_proj",
    )(a, b2, x, w_bf, g1.reshape(1, d), gate.reshape(b, 1, d), g2.reshape(1, d),
      shift.reshape(b, 1, d), scale.reshape(b, 1, d), rw_hi, rw_lo)


def _moe_body(x_ref, val_ref, wg_ref, wu_ref, wd_ref, o_ref, acc_s, x_s, wg_s, wu_s, wd_s, *, chunk):
    f = pl.program_id(1)
    wg_s[...] = wg_ref[0, 0].astype(BF16)
    wu_s[...] = wu_ref[0, 0].astype(BF16)
    wd_s[...] = wd_ref[0, 0].astype(BF16)
    m = x_ref.shape[2]
    quarter = x_ref.shape[3]

    @pl.when(f == 0)
    def _():
        def unpack(c, carry):
            r = pl.multiple_of(c * chunk, chunk)
            for s in range(2):
                lo, hi = _unpack_bf16_pairs(x_ref[0, s, pl.ds(r, chunk), :])
                x_s[pl.ds(r, chunk), s * quarter:(s + 1) * quarter] = lo
                x_s[pl.ds(r, chunk), (2 + s) * quarter:(3 + s) * quarter] = hi
            acc_s[pl.ds(r, chunk), :] = jnp.zeros((chunk, 4 * quarter), F32)
            return carry

        lax.fori_loop(0, m // chunk, unpack, 0)

    def rows(c, carry):
        r = pl.multiple_of(c * chunk, chunk)
        xs = x_s[pl.ds(r, chunk), :]
        hid = (_silu(_dot(xs, wg_s[...])) * _dot(xs, wu_s[...])).astype(BF16)
        acc_s[pl.ds(r, chunk), :] += _dot(hid, wd_s[...])
        return carry

    lax.fori_loop(0, m // chunk, rows, 0, unroll=True)

    @pl.when(f == pl.num_programs(1) - 1)
    def _():
        o_ref[0] = (acc_s[...] * val_ref[0]).astype(o_ref.dtype)


def _expert_ffn(xg, vals, w_gate, w_up, w_down, layer, chunk):
    e, _, m, quarter = xg.shape
    d = 4 * quarter
    ff = w_gate.shape[-1]
    tf = 256
    return pl.pallas_call(
        functools.partial(_moe_body, chunk=chunk),
        grid=(e, ff // tf),
        in_specs=[pl.BlockSpec((1, 2, m, quarter), lambda i, f: (i, 0, 0, 0)),
                  pl.BlockSpec((1, m, 1), lambda i, f: (i, 0, 0)),
                  pl.BlockSpec((1, 1, d, tf), lambda i, f: (layer, i, 0, f)),
                  pl.BlockSpec((1, 1, d, tf), lambda i, f: (layer, i, 0, f)),
                  pl.BlockSpec((1, 1, tf, d), lambda i, f: (layer, i, f, 0))],
        out_specs=pl.BlockSpec((1, m, d), lambda i, f: (i, 0, 0)),
        out_shape=jax.ShapeDtypeStruct((e, m, d), BF16),
        scratch_shapes=[pltpu.VMEM((m, d), F32), pltpu.VMEM((m, d), BF16), pltpu.VMEM((d, tf), BF16),
                        pltpu.VMEM((d, tf), BF16), pltpu.VMEM((tf, d), BF16)],
        compiler_params=_cparams(("parallel", "arbitrary")),
        name="expert_ffn",
    )(xg, vals, w_gate, w_up, w_down)


def _resid_body(x_ref, f_ref, gate_ref, g_ref, o_ref):
    o_ref[0] = x_ref[0] + gate_ref[0] * _rms(f_ref[0], g_ref[...])


def _gated_residual(x, f, gate, g, tm):
    b, l, d = x.shape
    blk = pl.BlockSpec((1, tm, d), lambda i, j: (i, j, 0))
    return pl.pallas_call(
        _resid_body,
        grid=(b, l // tm),
        in_specs=[blk, blk, pl.BlockSpec((1, 1, d), lambda i, j: (i, 0, 0)),
                  pl.BlockSpec((1, d), lambda i, j: (0, 0))],
        out_specs=blk,
        out_shape=jax.ShapeDtypeStruct((b, l, d), F32),
        compiler_params=_cparams(("parallel", "parallel")),
        name="gated_residual",
    )(x, f, gate.reshape(b, 1, d), g.reshape(1, d))


def _swap_pairs(x):
    nf = GLA_DK // 4
    lane = lax.broadcasted_iota(jnp.int32, (1, LANES), 1)
    up = pltpu.roll(x, LANES - nf, 1)
    down = pltpu.roll(x, nf, 1)
    return jnp.where(lane % (2 * nf) < nf, up, down)


def _odd_in_body(x_ref, g_ref, sh_ref, sc_ref, w_ref, cos_ref, sin_ref, gw_ref, gb_ref,
                 pool_ref, qk_ref, v_ref, r_ref, gate_ref):
    h = (_rms(x_ref[0], g_ref[...]) * (1.0 + sc_ref[0]) + sh_ref[0]).astype(BF16)
    qk = GLA_HEADS * GLA_DK
    vd = GLA_HEADS * GLA_DV
    q0 = POOL_CH
    v0 = q0 + 2 * qk
    r0 = v0 + vd
    l0 = r0 + vd
    pool_ref[0] = _dot(h, w_ref[:, 0:q0])
    for s in range(2 * qk // LANES):
        raw = _dot(h, w_ref[:, q0 + s * LANES:q0 + (s + 1) * LANES])
        c = cos_ref[:, (s * LANES) % qk:(s * LANES) % qk + LANES]
        sn = sin_ref[:, (s * LANES) % qk:(s * LANES) % qk + LANES]
        rot = raw * c + _swap_pairs(raw) * sn
        if s * LANES < qk:
            rot = rot * (GLA_DK ** -0.5)
        qk_ref[0, :, s * LANES:(s + 1) * LANES] = rot
    v_ref[0] = _dot(h, w_ref[:, v0:r0]).astype(BF16)
    r_ref[0] = _dot(h, w_ref[:, r0:l0])
    lr = _dot(h, w_ref[:, l0:l0 + 2 * GLA_RANK])
    z = jnp.dot(lr, gw_ref[...], precision=HIGHEST, preferred_element_type=F32) + gb_ref[...]
    gate_ref[0] = (jnp.minimum(z, 0.0) - jnp.log1p(jnp.exp(-jnp.abs(z)))) * (1.0 / GLA_TAU)


def _odd_in(x, g, shift, scale, w_bf, cos_t, sin_t, gate_w, gate_b, tm):
    b, l, d = x.shape
    n = w_bf.shape[1]
    qk = GLA_HEADS * GLA_DK
    vd = GLA_HEADS * GLA_DV
    gw = jnp.zeros((2 * GLA_RANK, 2 * qk), F32)
    gw = gw.at[:GLA_RANK, :qk].set(gate_w[0]).at[GLA_RANK:, qk:].set(gate_w[1])
    gb = jnp.concatenate([gate_b[0], gate_b[1]]).reshape(1, 2 * qk)
    vec = pl.BlockSpec((1, 1, d), lambda i, j: (i, 0, 0))
    row = lambda w: pl.BlockSpec((1, tm, w), lambda i, j: (i, j, 0))
    return pl.pallas_call(
        _odd_in_body,
        grid=(b, l // tm),
        in_specs=[row(d), pl.BlockSpec((1, d), lambda i, j: (0, 0)), vec, vec,
                  pl.BlockSpec((d, n), lambda i, j: (0, 0)),
                  pl.BlockSpec((tm, qk), lambda i, j: (j, 0)),
                  pl.BlockSpec((tm, qk), lambda i, j: (j, 0)),
                  pl.BlockSpec((2 * GLA_RANK, 2 * qk), lambda i, j: (0, 0)),
                  pl.BlockSpec((1, 2 * qk), lambda i, j: (0, 0))],
        out_specs=[row(POOL_CH), row(2 * qk), row(vd), row(vd), row(2 * qk)],
        out_shape=[jax.ShapeDtypeStruct((b, l, POOL_CH), F32),
                   jax.ShapeDtypeStruct((b, l, 2 * qk), F32),
                   jax.ShapeDtypeStruct((b, l, vd), BF16),
                   jax.ShapeDtypeStruct((b, l, vd), F32),
                   jax.ShapeDtypeStruct((b, l, 2 * qk), F32)],
        compiler_params=_cparams(("parallel", "parallel")),
        name="odd_in",
    )(x, g.reshape(1, d), shift.reshape(b, 1, d), scale.reshape(b, 1, d), w_bf, cos_t, sin_t, gw, gb)


def _rope_tables(l):
    t = jnp.arange(l)
    pos_r = (t // GRID_W).astype(F32)
    pos_c = (t % GRID_W).astype(F32)
    nf = GLA_DK // 4
    inv = jnp.power(ROPE_BASE, -jnp.arange(nf, dtype=F32) / nf)
    ar = pos_r[:, None] * inv[None, :]
    ac = pos_c[:, None] * inv[None, :]
    cos_h = jnp.concatenate([jnp.cos(ar), jnp.cos(ar), jnp.cos(ac), jnp.cos(ac)], axis=-1)
    sin_h = jnp.concatenate([-jnp.sin(ar), jnp.sin(ar), -jnp.sin(ac), jnp.sin(ac)], axis=-1)
    return jnp.tile(cos_h, (1, GLA_HEADS)), jnp.tile(sin_h, (1, GLA_HEADS))


def _gla_tile(qk, v, g, s, reverse):
    hk = GLA_HEADS * GLA_DK
    hv = GLA_HEADS * GLA_DV
    c = GLA_CHUNK
    t = qk.shape[0]
    n = t // c
    last_row, mid_row = (0, c // 2) if reverse else (c - 1, c // 2 - 1)
    ii = lax.broadcasted_iota(jnp.int32, (t, t), 0)
    jj = lax.broadcasted_iota(jnp.int32, (t, t), 1)
    ordered = (jj >= ii) if reverse else (jj <= ii)
    tri = jnp.where(ordered & (ii // c == jj // c), 1.0, 0.0).astype(BF16)
    g_hi = g.astype(BF16)
    rem = g - g_hi.astype(F32)
    g_mid = rem.astype(BF16)
    g_lo = (rem - g_mid.astype(F32)).astype(BF16)
    bc = _dot(tri, g_hi) + (_dot(tri, g_mid) + _dot(tri, g_lo))
    spread = lambda row: jnp.concatenate(
        [jnp.broadcast_to(bc[i * c + row:i * c + row + 1, :], (c, hk)) for i in range(n)], axis=0)
    b_mid = spread(mid_row)
    b_last = spread(last_row)
    qt = qk[:, 0:hk] * jnp.exp(bc - b_mid)
    kt = qk[:, hk:2 * hk] * jnp.exp(b_mid - bc)
    qe = (qt * jnp.exp(b_mid)).astype(BF16)
    ke = kt * jnp.exp(b_last - b_mid)
    ktb = kt.astype(BF16)
    lane = lax.broadcasted_iota(jnp.int32, (1, hk), 1)
    ci = lax.broadcasted_iota(jnp.int32, (c, c), 0)
    cj = lax.broadcasted_iota(jnp.int32, (c, c), 1)
    causal = (cj >= ci) if reverse else (cj <= ci)
    blockdiag = (lax.broadcasted_iota(jnp.int32, (hk, hv), 0) // GLA_DK
                 == lax.broadcasted_iota(jnp.int32, (hk, hv), 1) // GLA_DV)
    intra, upd, decay = [], [], []
    for i in range(n):
        rows = slice(i * c, (i + 1) * c)
        qs = jnp.concatenate(
            [jnp.where((lane >= h * GLA_DK) & (lane < (h + 1) * GLA_DK), qt[rows], 0.0) for h in range(GLA_HEADS)],
            axis=0).astype(BF16)
        att = _dot_nt(qs, ktb[rows])
        intra.append(jnp.concatenate(
            [_dot(jnp.where(causal, att[h * c:(h + 1) * c], 0.0).astype(BF16),
                  v[rows, h * GLA_DV:(h + 1) * GLA_DV]) for h in range(GLA_HEADS)], axis=-1))
        upd.append(jnp.where(blockdiag, _dot(ke[rows].T.astype(BF16), v[rows]), 0.0))
        decay.append(jnp.exp(jnp.sum(g[rows].T, axis=1, keepdims=True)))
    outs = [None] * n
    for i in (reversed(range(n)) if reverse else range(n)):
        rows = slice(i * c, (i + 1) * c)
        outs[i] = _dot(qe[rows], s.astype(BF16)) + intra[i]
        s = decay[i] * s + upd[i]
    return jnp.concatenate(outs, axis=0), s


def _gla_body(qkf_ref, qkb_ref, vf_ref, vb_ref, gf_ref, gb_ref, s0f_ref, s0b_ref,
              of_ref, ob_ref, sff_ref, sbf_ref, sf_ref, sb_ref, *, tile):
    n = pl.program_id(1)
    hk = GLA_HEADS * GLA_DK
    hv = GLA_HEADS * GLA_DV

    @pl.when(n == 0)
    def _():
        sf_ref[...] = jnp.zeros((hk, hv), F32)
        sb_ref[...] = jnp.zeros((hk, hv), F32)
        for h in range(GLA_HEADS):
            sf_ref[h * GLA_DK:(h + 1) * GLA_DK, h * GLA_DV:(h + 1) * GLA_DV] = s0f_ref[0, h]
            sb_ref[h * GLA_DK:(h + 1) * GLA_DK, h * GLA_DV:(h + 1) * GLA_DV] = s0b_ref[0, h]

    of_ref[0], sf_ref[...] = _gla_tile(qkf_ref[0], vf_ref[0], gf_ref[0], sf_ref[...], False)
    ob_ref[0], sb_ref[...] = _gla_tile(qkb_ref[0], vb_ref[0], gb_ref[0], sb_ref[...], True)

    @pl.when(n == pl.num_programs(1) - 1)
    def _():
        for h in range(GLA_HEADS):
            sff_ref[0, h] = sf_ref[h * GLA_DK:(h + 1) * GLA_DK, h * GLA_DV:(h + 1) * GLA_DV]
            sbf_ref[0, h] = sb_ref[h * GLA_DK:(h + 1) * GLA_DK, h * GLA_DV:(h + 1) * GLA_DV]


def _gla(qk, v, gates, s0f, s0b, tile):
    b, l, _ = qk.shape
    hk = GLA_HEADS * GLA_DK
    hv = GLA_HEADS * GLA_DV
    nt = l // tile
    fwd = lambda w, col: pl.BlockSpec((1, tile, w), lambda i, n: (i, n, col))
    bwd = lambda w, col: pl.BlockSpec((1, tile, w), lambda i, n: (i, nt - 1 - n, col))
    st = pl.BlockSpec((1, GLA_HEADS, GLA_DK, GLA_DV), lambda i, n: (i, 0, 0, 0))
    return pl.pallas_call(
        functools.partial(_gla_body, tile=tile),
        grid=(b, nt),
        in_specs=[fwd(2 * hk, 0), bwd(2 * hk, 0), fwd(hv, 0), bwd(hv, 0), fwd(hk, 0), bwd(hk, 1), st, st],
        out_specs=[fwd(hv, 0), bwd(hv, 0), st, st],
        out_shape=[jax.ShapeDtypeStruct((b, l, hv), F32), jax.ShapeDtypeStruct((b, l, hv), F32),
                   jax.ShapeDtypeStruct((b, GLA_HEADS, GLA_DK, GLA_DV), F32),
                   jax.ShapeDtypeStruct((b, GLA_HEADS, GLA_DK, GLA_DV), F32)],
        scratch_shapes=[pltpu.VMEM((hk, hv), F32), pltpu.VMEM((hk, hv), F32)],
        compiler_params=_cparams(("parallel", "arbitrary")),
        name="gla_scan",
    )(qk, qk, v, v, gates, gates, s0f, s0b)


def _odd_mid_body(cur_ref, prev_ref, next_ref, of_ref, ob_ref, r_ref, hg_ref, pw_ref, ps_ref,
                  pool_ref, d_ref, buf_ref, *, tile, seq):
    j = pl.program_id(1)
    last = pl.num_programs(1) - 1
    hal = POOL_HALO
    buf_ref[0:hal, :] = jnp.where(j == 0, 0.0, prev_ref[0])
    buf_ref[hal:hal + tile, :] = cur_ref[0]
    buf_ref[hal + tile:hal + tile + hal, :] = jnp.where(j == last, 0.0, next_ref[0])
    t = j * tile + lax.broadcasted_iota(jnp.int32, (tile, 1), 0)
    for gi, win in enumerate(POOL_WINDOWS):
        cols = slice(gi * POOL_GROUP, (gi + 1) * POOL_GROUP)
        acc = jnp.zeros((tile, POOL_GROUP), F32)
        for off in range(-(win // 2), win - win // 2):
            acc = acc + buf_ref[hal + off:hal + off + tile, cols]
        cnt = jnp.minimum(t + (win - win // 2), seq) - jnp.maximum(t - win // 2, 0)
        diff = acc / cnt.astype(F32) - cur_ref[0, :, cols]
        pool_ref[0, :, cols] = _dot(diff.astype(BF16), pw_ref[gi]) * ps_ref[:, cols]
    for h in range(GLA_HEADS):
        cols = slice(h * GLA_DV, (h + 1) * GLA_DV)
        o = of_ref[0, :, cols] + ob_ref[0, :, cols]
        d_ref[0, :, cols] = _rms(o, hg_ref[:, cols]) * _silu(r_ref[0, :, cols])


def _odd_mid(pool_u, o_f, o_b, r, head_g, pool_w_bf, pool_scale, tile):
    b, l, c = pool_u.shape
    hal = POOL_HALO
    per = tile // hal
    nh = l // hal
    blk = pl.BlockSpec((1, tile, c), lambda i, j: (i, j, 0))
    vec = pl.BlockSpec((1, c), lambda i, j: (0, 0))
    return pl.pallas_call(
        functools.partial(_odd_mid_body, tile=tile, seq=l),
        grid=(b, l // tile),
        in_specs=[blk,
                  pl.BlockSpec((1, hal, c), lambda i, j: (i, jnp.maximum(j * per - 1, 0), 0)),
                  pl.BlockSpec((1, hal, c), lambda i, j: (i, jnp.minimum((j + 1) * per, nh - 1), 0)),
                  blk, blk, blk, vec,
                  pl.BlockSpec((len(POOL_WINDOWS), POOL_GROUP, POOL_GROUP), lambda i, j: (0, 0, 0)),
                  vec],
        out_specs=[blk, blk],
        out_shape=[jax.ShapeDtypeStruct((b, l, c), F32), jax.ShapeDtypeStruct((b, l, c), F32)],
        scratch_shapes=[pltpu.VMEM((tile + 2 * hal, c), F32)],
        compiler_params=_cparams(("parallel", "parallel")),
        name="odd_mid",
    )(pool_u, pool_u, pool_u, o_f, o_b, r, head_g.reshape(1, c), pool_w_bf, pool_scale.reshape(1, c))


def _route(aff_t, tok_base, row_base):
    b, e, n = aff_t.shape
    cap = EC_CAPACITY_FACTOR * n // N_EXPERTS
    vals, idx = lax.top_k(aff_t, cap)
    idx, vals = lax.sort((idx, vals), dimension=2, num_keys=1)
    bi = jnp.arange(b, dtype=idx.dtype)[:, None, None]
    per_expert = lambda a: jnp.swapaxes(a, 0, 1).reshape(e, b * cap)
    rows0 = idx + row_base + 2 * bi * n
    return (per_expert(vals), per_expert(idx + tok_base + bi * n), per_expert(rows0), per_expert(rows0 + n),
            per_expert(idx))


def _combine_body(offs_ref, tok_ref, y_ref, o_ref, *, n_tok, cap):
    bi = pl.program_id(0)
    e = pl.program_id(2)
    tt = COMBINE_TILE
    wn = COMBINE_WINDOW
    ntiles = n_tok // tt

    @pl.when(e == 0)
    def _():
        o_ref[...] = jnp.zeros(o_ref.shape, F32)

    lane = lax.broadcasted_iota(jnp.int32, (1, tt), 1)
    srow = lax.broadcasted_iota(jnp.int32, (wn, 1), 0)
    base = (bi * pl.num_programs(2) + e) * (ntiles + 1)

    def tile(j, carry):
        p0 = offs_ref[base + j]
        p1 = offs_ref[base + j + 1]
        start = (p0 // SLOT_ALIGN) * SLOT_ALIGN
        t0 = j * tt

        def window(w, carry):
            lo = start + w * wn
            cs = pl.multiple_of(jnp.minimum(lo, cap - wn), SLOT_ALIGN)
            tcol = tok_ref[0, pl.ds(cs, wn), :] - t0
            hit = (tcol == lane) & (cs + srow >= lo)
            onehot = jnp.where(hit, 1.0, 0.0).T.astype(BF16)
            rows = pl.ds(pl.multiple_of(t0, tt), tt)
            o_ref[0, rows, :] += _dot(onehot, y_ref[0, pl.ds(cs, wn), :])
            return carry

        lax.fori_loop(0, (p1 - start + wn - 1) // wn, window, 0)
        return carry

    lax.fori_loop(0, ntiles, tile, 0)


def _combine(y, tok, offs, seg0, b, n_tok, cap):
    e, _, d = y.shape
    cols = COMBINE_COLS
    ntiles = n_tok // COMBINE_TILE
    assert seg0 % cap == 0 and n_tok % COMBINE_TILE == 0 and cap % COMBINE_WINDOW == 0
    seg = seg0 // cap
    grid_spec = pltpu.PrefetchScalarGridSpec(
        num_scalar_prefetch=1,
        grid=(b, d // cols, e),
        in_specs=[pl.BlockSpec((1, cap, 1), lambda i, q, k, offs: (k, seg + i, 0)),
                  pl.BlockSpec((1, cap, cols), lambda i, q, k, offs: (k, seg + i, q))],
        out_specs=pl.BlockSpec((1, n_tok, cols), lambda i, q, k, offs: (i, 0, q)),
    )
    del ntiles
    return pl.pallas_call(
        functools.partial(_combine_body, n_tok=n_tok, cap=cap),
        grid_spec=grid_spec,
        out_shape=jax.ShapeDtypeStruct((b, n_tok, d), F32),
        compiler_params=_cparams(("parallel", "parallel", "arbitrary")),
        name="combine",
    )(offs, tok, y)


def _gather_rows(src, idx):
    window = SC_GATHER_WINDOW
    n = idx.shape[0]
    width = src.shape[1]
    assert 2 * window * width * 4 <= SC_TILE_VMEM_BUDGET, width
    assert n % (window * SC_CORES * SC_SUBCORES) == 0, n
    mesh = plsc.VectorSubcoreMesh(core_axis_name="core", subcore_axis_name="subcore",
                                  num_cores=SC_CORES, num_subcores=SC_SUBCORES)

    @pl.kernel(out_type=jax.ShapeDtypeStruct((n, width), src.dtype), mesh=mesh, scratch_types=[],
               name="gather_rows")
    def gather(src_hbm, idx_hbm, out_hbm):
        def body(idx_vmem, out_vmem):
            pltpu.sync_copy(src_hbm.at[idx_vmem.at[0]], out_vmem)

        pltpu.emit_pipeline(
            body,
            grid=(n // window,),
            in_specs=[pl.BlockSpec((1, window), lambda i: (0, i))],
            out_specs=[pl.BlockSpec((window, width), lambda i: (i, 0))],
            core_axis_name=("core", "subcore"),
            dimension_semantics=(pltpu.PARALLEL,),
        )(idx_hbm, out_hbm)

    return gather(src, idx.reshape(1, n))


def _moe(parts, w_gate, w_up, w_down, layer):
    quarter = parts[0][1].shape[-1]
    d = 4 * quarter
    sizes = [h.shape[0] * h.shape[2] for _, h in parts]
    bases = [sum(sizes[:i]) for i in range(len(parts))]
    routed = [_route(a, base, 2 * base) for (a, _), base in zip(parts, bases)]
    src = jnp.concatenate([h.reshape(-1, quarter) for _, h in parts], axis=0)
    vals, flat, rows0, rows1, tok = (jnp.concatenate([r[i] for r in routed], axis=1) for i in range(5))
    e, m = flat.shape
    pad = -m % LANES
    vals, rows0, rows1, tok = (jnp.pad(a, ((0, 0), (0, pad))) for a in (vals, rows0, rows1, tok))
    m += pad
    rows = jnp.stack([rows0, rows1], axis=1)
    xg = _gather_rows(src, rows.reshape(-1)).reshape(e, 2, m, quarter)
    chunk = next(c for c in (512, 544, 384, 256, 128) if m % c == 0)
    y = _expert_ffn(xg, vals[..., None], w_gate, w_up, w_down, layer, chunk)
    outs = []
    seg0 = 0
    for (aff_t, h), r, base in zip(parts, routed, bases):
        b, _, n = aff_t.shape
        cap = r[0].shape[1] // b
        if n % COMBINE_TILE == 0 and cap % COMBINE_WINDOW == 0 and seg0 % cap == 0:
            bounds = jnp.arange(n // COMBINE_TILE + 1, dtype=jnp.int32) * COMBINE_TILE
            local = r[4].reshape(e, b, cap)
            offs = jnp.sum((local[..., None] < bounds).astype(jnp.int32), axis=2)
            offs = jnp.swapaxes(offs, 0, 1).reshape(-1)
            outs.append(_combine(y, tok[..., None], offs, seg0, b, n, cap))
        else:
            ids = r[1] - base
            part = y[:, seg0:seg0 + b * cap].astype(F32)
            out = jnp.zeros((b * n, d), F32).at[ids.reshape(-1)].add(part.reshape(-1, d))
            outs.append(out.reshape(b, n, d))
        seg0 += b * cap
    return outs


def kernel(x, c, ctx, c_ctx, w_mod, b_mod, norm_g, w_in_even, w_out_even, conv_w, conv_b, conv_ln_g,
           conv_ln_b, na_rpb, w_in_odd, w_out_odd, pool_w, pool_scale, gla_gate_w, gla_gate_b, gla_head_g,
           router_w, expert_w_gate, expert_w_up, expert_w_down):
    b, l, d = x.shape
    n_ctx = ctx.shape[1]
    tm = 512

    mod_rows = jnp.concatenate([c, c_ctx[None], jnp.zeros((8 - b - 1, d), F32)], axis=0)

    mod_all = _modulation(mod_rows, w_mod, b_mod)

    def modulation(i):
        mm = mod_all[i]
        m = mm[:b].reshape(b, 6, d)
        mc = jnp.broadcast_to(mm[b].reshape(1, 6, d), (b, 6, d))
        return m, mc

    m, mc = modulation(0)
    g = norm_g[0]
    w_in = w_in_even[0].astype(BF16)
    w_out = w_out_even[0].astype(BF16)
    glu, qkv = _even_in(x, g[0], m[:, 0], m[:, 1], w_in, tm)
    glu_c, qkv_c = _even_in(ctx, g[0], mc[:, 0], mc[:, 1], w_in, n_ctx)
    a_lat = _conv_branch(glu, conv_w[0], conv_b[0], conv_ln_g[0], conv_ln_b[0], 256)
    a_ctx = _conv_branch(glu_c, conv_w[0], conv_b[0], conv_ln_g[0], conv_ln_b[0], n_ctx)
    na = _neighbourhood_attention(qkv, qkv_c, na_rpb[0])
    att_c = _context_attention(qkv_c)
    x, h2, aff = _out_proj(a_lat, na, x, w_out, g[1], m[:, 2], g[2], m[:, 3], m[:, 4], router_w[0], tm)
    ctx, h2c, aff_c = _out_proj(a_ctx, att_c, ctx, w_out, g[1], mc[:, 2], g[2], mc[:, 3], mc[:, 4],
                                router_w[0], n_ctx)
    f, f_c = _moe([(aff, h2), (aff_c, h2c)], expert_w_gate, expert_w_up, expert_w_down, 0)
    x = _gated_residual(x, f, m[:, 5], g[3], tm)
    ctx = _gated_residual(ctx, f_c, mc[:, 5], g[3], n_ctx)

    m, mc = modulation(1)
    g = norm_g[1]
    w_in = w_in_odd[0].astype(BF16)
    w_out = w_out_odd[0].astype(BF16)
    cos_t, sin_t = _rope_tables(l)
    ones_t = jnp.ones((n_ctx, GLA_HEADS * GLA_DK), F32)
    _, qk_c, v_c, _, gate_c = _odd_in(ctx, g[0], mc[:, 0], mc[:, 1], w_in, ones_t, jnp.zeros_like(ones_t),
                                      gla_gate_w[0], gla_gate_b[0], n_ctx)
    s_zero = jnp.zeros((b, GLA_HEADS, GLA_DK, GLA_DV), F32)
    _, _, s_f, s_b = _gla(qk_c, v_c, gate_c, s_zero, s_zero, n_ctx)
    pool_u, qk, v, r, gate = _odd_in(x, g[0], m[:, 0], m[:, 1], w_in, cos_t, sin_t,
                                     gla_gate_w[0], gla_gate_b[0], tm)
    o_f, o_b, _, _ = _gla(qk, v, gate, s_f, s_b, 256)
    pool_y, d_lat = _odd_mid(pool_u, o_f, o_b, r, gla_head_g[0], pool_w[0].astype(BF16), pool_scale[0], 256)
    x, h2, aff = _out_proj(pool_y, d_lat, x, w_out, g[1], m[:, 2], g[2], m[:, 3], m[:, 4], router_w[1], tm)
    (f,) = _moe([(aff, h2)], expert_w_gate, expert_w_up, expert_w_down, 1)
    return _gated_residual(x, f, m[:, 5], g[3], tm)
```

```python
import functools

import jax
import jax.numpy as jnp
from jax import lax
from jax.experimental import pallas as pl
from jax.experimental.pallas import tpu as pltpu
from jax.experimental.pallas import tpu_sc as plsc

F32 = jnp.float32
BF16 = jnp.bfloat16
HIGHEST = lax.Precision.HIGHEST

D_MODEL = 1024
GRID_W = 64
EPS = 1e-6
CONV_CH = 512
CONV_WIDTH = 31
CONV_HALO = 16
NA_HEADS = 8
NA_HEAD_DIM = 64
NA_KR = 8
NA_KC = 16
NA_ROWS_PER_BLOCK = 4
NA_WIN_ROWS = 12
POOL_CH = 512
POOL_WINDOWS = (2, 4, 8, 16)
POOL_GROUP = 128
POOL_HALO = 8
GLA_HEADS = 4
GLA_DK = 64
GLA_DV = 128
GLA_RANK = 16
GLA_TAU = 16.0
GLA_CHUNK = 64
ROPE_BASE = 10000.0
N_EXPERTS = 16
EXPERT_FF = 2816
EC_CAPACITY_FACTOR = 2
LANES = 128
SUBLANES = 8
NEG_BIG = -1e30
VMEM_LIMIT = 56 * 1024 * 1024
SC_CORES = 2
SC_SUBCORES = 16
SC_TILE_VMEM_BUDGET = 400 * 1024
SC_GATHER_WINDOW = 128
COMBINE_TILE = 256
COMBINE_WINDOW = 256
COMBINE_COLS = 512


def _cparams(sem):
    return pltpu.CompilerParams(dimension_semantics=sem, vmem_limit_bytes=VMEM_LIMIT)


def _rms(x, g):
    return x * lax.rsqrt(jnp.mean(x * x, axis=-1, keepdims=True) + EPS) * g


def _sigmoid(x):
    return 1.0 / (1.0 + jnp.exp(-x))


def _silu(x):
    return x * _sigmoid(x)


def _dot(a, b):
    return jnp.dot(a, b, preferred_element_type=F32)


def _pack_bf16_pairs(h):
    half = h.shape[-1] // 2
    bits = lax.bitcast_convert_type(h.astype(BF16).astype(F32), jnp.uint32)
    packed = (bits[:, half:] & jnp.uint32(0xFFFF0000)) | (bits[:, :half] >> 16)
    return lax.bitcast_convert_type(packed, jnp.int32)


def _unpack_bf16_pairs(p):
    bits = lax.bitcast_convert_type(p, jnp.uint32)
    lo = lax.bitcast_convert_type(bits << 16, F32).astype(BF16)
    hi = lax.bitcast_convert_type(bits & jnp.uint32(0xFFFF0000), F32).astype(BF16)
    return lo, hi


def _dot_nt(a, b):
    return lax.dot_general(a, b, (((1,), (1,)), ((), ())), preferred_element_type=F32)


def _mod_body(c_ref, w_ref, b_ref, o_ref):
    o_ref[0] = jnp.dot(_silu(c_ref[...]), w_ref[0], precision=HIGHEST,
                       preferred_element_type=F32) + b_ref[0]


def _modulation(rows, w, b):
    depth, _, n = w.shape
    tn = 1536
    return pl.pallas_call(
        _mod_body,
        grid=(depth, n // tn),
        in_specs=[pl.BlockSpec((8, D_MODEL), lambda i, j: (0, 0)),
                  pl.BlockSpec((1, D_MODEL, tn), lambda i, j: (i, 0, j)),
                  pl.BlockSpec((1, 1, tn), lambda i, j: (i, 0, j))],
        out_specs=pl.BlockSpec((1, 8, tn), lambda i, j: (i, 0, j)),
        out_shape=jax.ShapeDtypeStruct((depth, 8, n), F32),
        compiler_params=_cparams(("parallel", "parallel")),
        name="modulation",
    )(rows, w, b.reshape(depth, 1, n))


def _even_in_body(x_ref, g_ref, sh_ref, sc_ref, w_ref, glu_ref, qkv_ref):
    h = (_rms(x_ref[0], g_ref[...]) * (1.0 + sc_ref[0]) + sh_ref[0]).astype(BF16)
    c = CONV_CH
    glu_ref[0] = _dot(h, w_ref[:, 0:c]) * _sigmoid(_dot(h, w_ref[:, c:2 * c]))
    hd = NA_HEADS * NA_HEAD_DIM
    q0 = 2 * c
    qkv_ref[0, :, 0:hd] = (_dot(h, w_ref[:, q0:q0 + hd]) * (NA_HEAD_DIM ** -0.5)).astype(BF16)
    qkv_ref[0, :, hd:3 * hd] = _dot(h, w_ref[:, q0 + hd:q0 + 3 * hd]).astype(BF16)


def _even_in(x, g, shift, scale, w_bf, tm):
    b, l, d = x.shape
    n = w_bf.shape[1]
    hd3 = 3 * NA_HEADS * NA_HEAD_DIM
    vec = pl.BlockSpec((1, 1, d), lambda i, j: (i, 0, 0))
    return pl.pallas_call(
        _even_in_body,
        grid=(b, l // tm),
        in_specs=[pl.BlockSpec((1, tm, d), lambda i, j: (i, j, 0)),
                  pl.BlockSpec((1, d), lambda i, j: (0, 0)),
                  vec, vec,
                  pl.BlockSpec((d, n), lambda i, j: (0, 0))],
        out_specs=[pl.BlockSpec((1, tm, CONV_CH), lambda i, j: (i, j, 0)),
                   pl.BlockSpec((1, tm, hd3), lambda i, j: (i, j, 0))],
        out_shape=[jax.ShapeDtypeStruct((b, l, CONV_CH), F32),
                   jax.ShapeDtypeStruct((b, l, hd3), BF16)],
        compiler_params=_cparams(("parallel", "parallel")),
        name="even_in",
    )(x, g.reshape(1, d), shift.reshape(b, 1, d), scale.reshape(b, 1, d), w_bf)


def _conv_body(cur_ref, prev_ref, next_ref, w_ref, b_ref, lg_ref, lb_ref, o_ref, buf_ref, sh_ref, *, tile, chunk):
    j = pl.program_id(1)
    last = pl.num_programs(1) - 1
    hal = CONV_HALO
    buf_ref[0:hal, :] = jnp.where(j == 0, 0.0, prev_ref[0])
    buf_ref[hal:hal + tile, :] = cur_ref[0]
    buf_ref[hal + tile:hal + tile + hal, :] = jnp.where(j == last, 0.0, next_ref[0])
    span = sh_ref.shape[1]
    for s in range(SUBLANES):
        sh_ref[s] = buf_ref[s:s + span, :]
    first = hal - CONV_WIDTH // 2
    reps = chunk // SUBLANES

    def rows(c, carry):
        r0 = pl.multiple_of(c * chunk, chunk)
        accs = [jnp.zeros((chunk, CONV_CH), F32) for _ in range(2)]
        for k in range(CONV_WIDTH):
            a, s = divmod(first + k, SUBLANES)
            wk = jnp.concatenate([w_ref[k]] * reps, axis=0)
            accs[k % 2] = accs[k % 2] + sh_ref[s, pl.ds(r0 + a * SUBLANES, chunk), :] * wk
        o_ref[0, pl.ds(r0, chunk), :] = accs[0] + accs[1]
        return carry

    lax.fori_loop(0, tile // chunk, rows, 0)
    y = o_ref[0] + b_ref[...]
    mu = jnp.mean(y, axis=-1, keepdims=True)
    yc = y - mu
    var = jnp.mean(yc * yc, axis=-1, keepdims=True)
    o_ref[0] = _silu(yc * lax.rsqrt(var + EPS) * lg_ref[...] + lb_ref[...])


def _conv_branch(glu, conv_w, conv_b, ln_g, ln_b, tile):
    b, l, c = glu.shape
    hal = CONV_HALO
    per = tile // hal
    nh = l // hal
    vec = pl.BlockSpec((1, c), lambda i, j: (0, 0))
    return pl.pallas_call(
        functools.partial(_conv_body, tile=tile, chunk=32),
        grid=(b, l // tile),
        in_specs=[pl.BlockSpec((1, tile, c), lambda i, j: (i, j, 0)),
                  pl.BlockSpec((1, hal, c), lambda i, j: (i, jnp.maximum(j * per - 1, 0), 0)),
                  pl.BlockSpec((1, hal, c), lambda i, j: (i, jnp.minimum((j + 1) * per, nh - 1), 0)),
                  pl.BlockSpec((CONV_WIDTH, SUBLANES, c), lambda i, j: (0, 0, 0)),
                  vec, vec, vec],
        out_specs=pl.BlockSpec((1, tile, c), lambda i, j: (i, j, 0)),
        out_shape=jax.ShapeDtypeStruct((b, l, c), F32),
        scratch_shapes=[pltpu.VMEM((tile + 2 * hal, c), F32),
                        pltpu.VMEM((SUBLANES, tile + 2 * hal - SUBLANES, c), F32)],
        compiler_params=_cparams(("parallel", "parallel")),
        name="conv_branch",
    )(glu, glu, glu, jnp.broadcast_to(conv_w[:, None, :], (CONV_WIDTH, SUBLANES, c)),
      conv_b.reshape(1, c), ln_g.reshape(1, c), ln_b.reshape(1, c))


def _na_window_start(j, rows):
    rb = NA_ROWS_PER_BLOCK
    return jnp.clip(j * rb - NA_KR // 2, 0, rows - NA_WIN_ROWS)


def _na_body(q_ref, k_ref, v_ref, kc_ref, vc_ref, tab_ref, o_ref, *, rows):
    j = pl.program_id(2)
    nkeys = NA_WIN_ROWS * GRID_W
    start = pl.multiple_of(_na_window_start(j, rows) * GRID_W, GRID_W)
    q = q_ref[0]
    kw = k_ref[0, pl.ds(start, nkeys), :]
    vw = v_ref[0, pl.ds(start, nkeys), :]
    kc = kc_ref[0]
    vc = vc_ref[0]
    lane = lax.broadcasted_iota(jnp.int32, (1, LANES), 1)
    out = jnp.zeros(q.shape, F32)
    for hh in range(LANES // NA_HEAD_DIM):
        in_head = (lane >= hh * NA_HEAD_DIM) & (lane < (hh + 1) * NA_HEAD_DIM)
        qh = jnp.where(in_head, q, jnp.zeros_like(q))
        s = _dot_nt(qh, kw) + tab_ref[0, hh]
        sc = _dot_nt(qh, kc)
        m = jnp.maximum(jnp.max(s, axis=-1, keepdims=True), jnp.max(sc, axis=-1, keepdims=True))
        p = jnp.exp(s - m)
        pc = jnp.exp(sc - m)
        denom = jnp.sum(p, axis=-1, keepdims=True) + jnp.sum(pc, axis=-1, keepdims=True)
        o = (_dot(p.astype(BF16), vw) + _dot(pc.astype(BF16), vc)) / denom
        out = jnp.where(in_head, o, out)
    o_ref[0] = out


def _na_tables(rpb, rows):
    rb = NA_ROWS_PER_BLOCK
    nblk = rows // rb
    wr = NA_WIN_ROWS
    qc = jnp.arange(GRID_W)
    cs = jnp.clip(qc - NA_KC // 2, 0, GRID_W - NA_KC)
    col_ok = (qc[None, :] >= cs[:, None]) & (qc[None, :] < cs[:, None] + NA_KC)
    col_off = qc[None, :] - qc[:, None] + NA_KC - 1
    onehot = (col_off[:, :, None] == jnp.arange(2 * NA_KC - 1)[None, None, :]).astype(F32)
    blocks = jnp.einsum('hrd,qkd->hrqk', rpb.astype(F32), onehot, precision=HIGHEST)
    blocks = jnp.where(col_ok[None, None], blocks, NEG_BIG)
    masked = jnp.full((NA_HEADS, GRID_W, GRID_W), NEG_BIG, F32)
    tabs = []
    for jb in (0, 1, nblk - 1):
        ws = min(max(jb * rb - NA_KR // 2, 0), rows - wr)
        q_rows = []
        for qr in range(jb * rb, (jb + 1) * rb):
            rs = min(max(qr - NA_KR // 2, 0), rows - NA_KR)
            row = [blocks[:, kr - qr + NA_KR - 1] if rs <= kr < rs + NA_KR else masked
                   for kr in range(ws, ws + wr)]
            q_rows.append(jnp.concatenate(row, axis=-1))
        tabs.append(jnp.concatenate(q_rows, axis=1))
    return jnp.stack(tabs)


def _neighbourhood_attention(qkv, qkv_ctx, rpb):
    b, l, _ = qkv.shape
    n_ctx = qkv_ctx.shape[1]
    rows = l // GRID_W
    rb = NA_ROWS_PER_BLOCK
    nblk = rows // rb
    tq = rb * GRID_W
    nkeys = NA_WIN_ROWS * GRID_W
    hp = NA_HEADS * NA_HEAD_DIM // LANES
    tabs = _na_tables(rpb, rows)

    def cls(j):
        return jnp.where(j == 0, 0, jnp.where(j == nblk - 1, 2, 1))

    return pl.pallas_call(
        functools.partial(_na_body, rows=rows),
        grid=(b, hp, nblk),
        in_specs=[pl.BlockSpec((1, tq, LANES), lambda i, h, j: (i, j, h)),
                  pl.BlockSpec((1, l, LANES), lambda i, h, j: (i, 0, hp + h)),
                  pl.BlockSpec((1, l, LANES), lambda i, h, j: (i, 0, 2 * hp + h)),
                  pl.BlockSpec((1, n_ctx, LANES), lambda i, h, j: (i, 0, hp + h)),
                  pl.BlockSpec((1, n_ctx, LANES), lambda i, h, j: (i, 0, 2 * hp + h)),
                  pl.BlockSpec((1, 2, tq, nkeys), lambda i, h, j: (cls(j), h, 0, 0))],
        out_specs=pl.BlockSpec((1, tq, LANES), lambda i, h, j: (i, j, h)),
        out_shape=jax.ShapeDtypeStruct((b, l, NA_HEADS * NA_HEAD_DIM), F32),
        compiler_params=_cparams(("parallel", "parallel", "arbitrary")),
        name="neighbourhood_attention",
    )(qkv, qkv, qkv, qkv_ctx, qkv_ctx, tabs)


def _ctx_attn_body(q_ref, k_ref, v_ref, o_ref):
    q = q_ref[0]
    k = k_ref[0]
    v = v_ref[0]
    lane = lax.broadcasted_iota(jnp.int32, (1, LANES), 1)
    out = jnp.zeros(q.shape, F32)
    for hh in range(LANES // NA_HEAD_DIM):
        in_head = (lane >= hh * NA_HEAD_DIM) & (lane < (hh + 1) * NA_HEAD_DIM)
        qh = jnp.where(in_head, q, jnp.zeros_like(q))
        s = _dot_nt(qh, k)
        p = jnp.exp(s - jnp.max(s, axis=-1, keepdims=True))
        o = _dot(p.astype(BF16), v) / jnp.sum(p, axis=-1, keepdims=True)
        out = jnp.where(in_head, o, out)
    o_ref[0] = out


def _context_attention(qkv_ctx):
    b, n, _ = qkv_ctx.shape
    hp = NA_HEADS * NA_HEAD_DIM // LANES
    return pl.pallas_call(
        _ctx_attn_body,
        grid=(b, hp),
        in_specs=[pl.BlockSpec((1, n, LANES), lambda i, h: (i, 0, h)),
                  pl.BlockSpec((1, n, LANES), lambda i, h: (i, 0, hp + h)),
                  pl.BlockSpec((1, n, LANES), lambda i, h: (i, 0, 2 * hp + h))],
        out_specs=pl.BlockSpec((1, n, LANES), lambda i, h: (i, 0, h)),
        out_shape=jax.ShapeDtypeStruct((b, n, NA_HEADS * NA_HEAD_DIM), F32),
        compiler_params=_cparams(("parallel", "parallel")),
        name="context_attention",
    )(qkv_ctx, qkv_ctx, qkv_ctx)


def _out_body(a_ref, b_ref, x_ref, w_ref, g1_ref, gate_ref, g2_ref, sh_ref, sc_ref, rwh_ref, rwl_ref,
              xo_ref, h_ref, aff_ref):
    half = a_ref.shape[-1]
    y = _dot(a_ref[0].astype(BF16), w_ref[0:half, :]) + _dot(b_ref[0].astype(BF16), w_ref[half:2 * half, :])
    xn = x_ref[0] + gate_ref[0] * _rms(y, g1_ref[...])
    xo_ref[0] = xn
    h = _rms(xn, g2_ref[...]) * (1.0 + sc_ref[0]) + sh_ref[0]
    packed = _pack_bf16_pairs(h)
    quarter = packed.shape[-1] // 2
    h_ref[0, 0] = packed[:, 0:quarter]
    h_ref[0, 1] = packed[:, quarter:2 * quarter]
    h_hi = h.astype(BF16)
    h_lo = (h - h_hi.astype(F32)).astype(BF16)
    logits = _dot(h_hi, rwh_ref[...]) + (_dot(h_lo, rwh_ref[...]) + _dot(h_hi, rwl_ref[...]))
    lane = lax.broadcasted_iota(jnp.int32, (1, LANES), 1)
    logits = jnp.where(lane < N_EXPERTS, logits, NEG_BIG)
    e = jnp.exp(logits - jnp.max(logits, axis=-1, keepdims=True))
    aff = e / jnp.sum(e, axis=-1, keepdims=True)
    aff_ref[0] = aff.T[0:N_EXPERTS, :]


def _out_proj(a, b2, x, w_bf, g1, gate, g2, shift, scale, router_w, tm):
    b, l, d = x.shape
    half = a.shape[-1]
    rw = jnp.pad(router_w, ((0, 0), (0, LANES - N_EXPERTS)))
    rw_hi = rw.astype(BF16)
    rw_lo = (rw - rw_hi.astype(F32)).astype(BF16)
    vec = pl.BlockSpec((1, d), lambda i, j: (0, 0))
    bvec = pl.BlockSpec((1, 1, d), lambda i, j: (i, 0, 0))
    rspec = pl.BlockSpec((d, LANES), lambda i, j: (0, 0))
    return pl.pallas_call(
        _out_body,
        grid=(b, l // tm),
        in_specs=[pl.BlockSpec((1, tm, half), lambda i, j: (i, j, 0)),
                  pl.BlockSpec((1, tm, half), lambda i, j: (i, j, 0)),
                  pl.BlockSpec((1, tm, d), lambda i, j: (i, j, 0)),
                  pl.BlockSpec((2 * half, d), lambda i, j: (0, 0)),
                  vec, bvec, vec, bvec, bvec, rspec, rspec],
        out_specs=[pl.BlockSpec((1, tm, d), lambda i, j: (i, j, 0)),
                   pl.BlockSpec((1, 2, tm, d // 4), lambda i, j: (i, 0, j, 0)),
                   pl.BlockSpec((1, N_EXPERTS, tm), lambda i, j: (i, 0, j))],
        out_shape=[jax.ShapeDtypeStruct((b, l, d), F32),
                   jax.ShapeDtypeStruct((b, 2, l, d // 4), jnp.int32),
                   jax.ShapeDtypeStruct((b, N_EXPERTS, l), F32)],
        compiler_params=_cparams(("parallel", "parallel")),
        name="out_proj",
    )(a, b2, x, w_bf, g1.reshape(1, d), gate.reshape(b, 1, d), g2.reshape(1, d),
      shift.reshape(b, 1, d), scale.reshape(b, 1, d), rw_hi, rw_lo)


def _moe_body(x_ref, val_ref, wg_ref, wu_ref, wd_ref, o_ref, acc_s, x_s, wg_s, wu_s, wd_s, *, chunk):
    f = pl.program_id(1)
    wg_s[...] = wg_ref[0, 0].astype(BF16)
    wu_s[...] = wu_ref[0, 0].astype(BF16)
    wd_s[...] = wd_ref[0, 0].astype(BF16)
    m = x_ref.shape[2]
    quarter = x_ref.shape[3]

    @pl.when(f == 0)
    def _():
        def unpack(c, carry):
            r = pl.multiple_of(c * chunk, chunk)
            for s in range(2):
                lo, hi = _unpack_bf16_pairs(x_ref[0, s, pl.ds(r, chunk), :])
                x_s[pl.ds(r, chunk), s * quarter:(s + 1) * quarter] = lo
                x_s[pl.ds(r, chunk), (2 + s) * quarter:(3 + s) * quarter] = hi
            acc_s[pl.ds(r, chunk), :] = jnp.zeros((chunk, 4 * quarter), F32)
            return carry

        lax.fori_loop(0, m // chunk, unpack, 0)

    def rows(c, carry):
        r = pl.multiple_of(c * chunk, chunk)
        xs = x_s[pl.ds(r, chunk), :]
        hid = (_silu(_dot(xs, wg_s[...])) * _dot(xs, wu_s[...])).astype(BF16)
        acc_s[pl.ds(r, chunk), :] += _dot(hid, wd_s[...])
        return carry

    lax.fori_loop(0, m // chunk, rows, 0, unroll=True)

    @pl.when(f == pl.num_programs(1) - 1)
    def _():
        o_ref[0] = (acc_s[...] * val_ref[0]).astype(o_ref.dtype)


def _expert_ffn(xg, vals, w_gate, w_up, w_down, layer, chunk):
    e, _, m, quarter = xg.shape
    d = 4 * quarter
    ff = w_gate.shape[-1]
    tf = 256
    return pl.pallas_call(
        functools.partial(_moe_body, chunk=chunk),
        grid=(e, ff // tf),
        in_specs=[pl.BlockSpec((1, 2, m, quarter), lambda i, f: (i, 0, 0, 0)),
                  pl.BlockSpec((1, m, 1), lambda i, f: (i, 0, 0)),
                  pl.BlockSpec((1, 1, d, tf), lambda i, f: (layer, i, 0, f)),
                  pl.BlockSpec((1, 1, d, tf), lambda i, f: (layer, i, 0, f)),
                  pl.BlockSpec((1, 1, tf, d), lambda i, f: (layer, i, f, 0))],
        out_specs=pl.BlockSpec((1, m, d), lambda i, f: (i, 0, 0)),
        out_shape=jax.ShapeDtypeStruct((e, m, d), BF16),
        scratch_shapes=[pltpu.VMEM((m, d), F32), pltpu.VMEM((m, d), BF16), pltpu.VMEM((d, tf), BF16),
                        pltpu.VMEM((d, tf), BF16), pltpu.VMEM((tf, d), BF16)],
        compiler_params=_cparams(("parallel", "arbitrary")),
        name="expert_ffn",
    )(xg, vals, w_gate, w_up, w_down)


def _resid_body(x_ref, f_ref, gate_ref, g_ref, o_ref):
    o_ref[0] = x_ref[0] + gate_ref[0] * _rms(f_ref[0], g_ref[...])


def _gated_residual(x, f, gate, g, tm):
    b, l, d = x.shape
    blk = pl.BlockSpec((1, tm, d), lambda i, j: (i, j, 0))
    return pl.pallas_call(
        _resid_body,
        grid=(b, l // tm),
        in_specs=[blk, blk, pl.BlockSpec((1, 1, d), lambda i, j: (i, 0, 0)),
                  pl.BlockSpec((1, d), lambda i, j: (0, 0))],
        out_specs=blk,
        out_shape=jax.ShapeDtypeStruct((b, l, d), F32),
        compiler_params=_cparams(("parallel", "parallel")),
        name="gated_residual",
    )(x, f, gate.reshape(b, 1, d), g.reshape(1, d))


def _swap_pairs(x):
    nf = GLA_DK // 4
    lane = lax.broadcasted_iota(jnp.int32, (1, LANES), 1)
    up = pltpu.roll(x, LANES - nf, 1)
    down = pltpu.roll(x, nf, 1)
    return jnp.where(lane % (2 * nf) < nf, up, down)


def _odd_in_body(x_ref, g_ref, sh_ref, sc_ref, w_ref, cos_ref, sin_ref, gw_ref, gb_ref,
                 pool_ref, qk_ref, v_ref, r_ref, gate_ref):
    h = (_rms(x_ref[0], g_ref[...]) * (1.0 + sc_ref[0]) + sh_ref[0]).astype(BF16)
    qk = GLA_HEADS * GLA_DK
    vd = GLA_HEADS * GLA_DV
    q0 = POOL_CH
    v0 = q0 + 2 * qk
    r0 = v0 + vd
    l0 = r0 + vd
    pool_ref[0] = _dot(h, w_ref[:, 0:q0])
    for s in range(2 * qk // LANES):
        raw = _dot(h, w_ref[:, q0 + s * LANES:q0 + (s + 1) * LANES])
        c = cos_ref[:, (s * LANES) % qk:(s * LANES) % qk + LANES]
        sn = sin_ref[:, (s * LANES) % qk:(s * LANES) % qk + LANES]
        rot = raw * c + _swap_pairs(raw) * sn
        if s * LANES < qk:
            rot = rot * (GLA_DK ** -0.5)
        qk_ref[0, :, s * LANES:(s + 1) * LANES] = rot
    v_ref[0] = _dot(h, w_ref[:, v0:r0]).astype(BF16)
    r_ref[0] = _dot(h, w_ref[:, r0:l0])
    lr = _dot(h, w_ref[:, l0:l0 + 2 * GLA_RANK])
    z = jnp.dot(lr, gw_ref[...], precision=HIGHEST, preferred_element_type=F32) + gb_ref[...]
    gate_ref[0] = (jnp.minimum(z, 0.0) - jnp.log1p(jnp.exp(-jnp.abs(z)))) * (1.0 / GLA_TAU)


def _odd_in(x, g, shift, scale, w_bf, cos_t, sin_t, gate_w, gate_b, tm):
    b, l, d = x.shape
    n = w_bf.shape[1]
    qk = GLA_HEADS * GLA_DK
    vd = GLA_HEADS * GLA_DV
    gw = jnp.zeros((2 * GLA_RANK, 2 * qk), F32)
    gw = gw.at[:GLA_RANK, :qk].set(gate_w[0]).at[GLA_RANK:, qk:].set(gate_w[1])
    gb = jnp.concatenate([gate_b[0], gate_b[1]]).reshape(1, 2 * qk)
    vec = pl.BlockSpec((1, 1, d), lambda i, j: (i, 0, 0))
    row = lambda w: pl.BlockSpec((1, tm, w), lambda i, j: (i, j, 0))
    return pl.pallas_call(
        _odd_in_body,
        grid=(b, l // tm),
        in_specs=[row(d), pl.BlockSpec((1, d), lambda i, j: (0, 0)), vec, vec,
                  pl.BlockSpec((d, n), lambda i, j: (0, 0)),
                  pl.BlockSpec((tm, qk), lambda i, j: (j, 0)),
                  pl.BlockSpec((tm, qk), lambda i, j: (j, 0)),
                  pl.BlockSpec((2 * GLA_RANK, 2 * qk), lambda i, j: (0, 0)),
                  pl.BlockSpec((1, 2 * qk), lambda i, j: (0, 0))],
        out_specs=[row(POOL_CH), row(2 * qk), row(vd), row(vd), row(2 * qk)],
        out_shape=[jax.ShapeDtypeStruct((b, l, POOL_CH), F32),
                   jax.ShapeDtypeStruct((b, l, 2 * qk), F32),
                   jax.ShapeDtypeStruct((b, l, vd), BF16),
                   jax.ShapeDtypeStruct((b, l, vd), F32),
                   jax.ShapeDtypeStruct((b, l, 2 * qk), F32)],
        compiler_params=_cparams(("parallel", "parallel")),
        name="odd_in",
    )(x, g.reshape(1, d), shift.reshape(b, 1, d), scale.reshape(b, 1, d), w_bf, cos_t, sin_t, gw, gb)


def _rope_tables(l):
    t = jnp.arange(l)
    pos_r = (t // GRID_W).astype(F32)
    pos_c = (t % GRID_W).astype(F32)
    nf = GLA_DK // 4
    inv = jnp.power(ROPE_BASE, -jnp.arange(nf, dtype=F32) / nf)
    ar = pos_r[:, None] * inv[None, :]
    ac = pos_c[:, None] * inv[None, :]
    cos_h = jnp.concatenate([jnp.cos(ar), jnp.cos(ar), jnp.cos(ac), jnp.cos(ac)], axis=-1)
    sin_h = jnp.concatenate([-jnp.sin(ar), jnp.sin(ar), -jnp.sin(ac), jnp.sin(ac)], axis=-1)
    return jnp.tile(cos_h, (1, GLA_HEADS)), jnp.tile(sin_h, (1, GLA_HEADS))


def _gla_tile(qk, v, g, s, reverse):
    hk = GLA_HEADS * GLA_DK
    hv = GLA_HEADS * GLA_DV
    c = GLA_CHUNK
    t = qk.shape[0]
    n = t // c
    last_row, mid_row = (0, c // 2) if reverse else (c - 1, c // 2 - 1)
    ii = lax.broadcasted_iota(jnp.int32, (t, t), 0)
    jj = lax.broadcasted_iota(jnp.int32, (t, t), 1)
    ordered = (jj >= ii) if reverse else (jj <= ii)
    tri = jnp.where(ordered & (ii // c == jj // c), 1.0, 0.0).astype(BF16)
    g_hi = g.astype(BF16)
    rem = g - g_hi.astype(F32)
    g_mid = rem.astype(BF16)
    g_lo = (rem - g_mid.astype(F32)).astype(BF16)
    bc = _dot(tri, g_hi) + (_dot(tri, g_mid) + _dot(tri, g_lo))
    spread = lambda row: jnp.concatenate(
        [jnp.broadcast_to(bc[i * c + row:i * c + row + 1, :], (c, hk)) for i in range(n)], axis=0)
    b_mid = spread(mid_row)
    b_last = spread(last_row)
    qt = qk[:, 0:hk] * jnp.exp(bc - b_mid)
    kt = qk[:, hk:2 * hk] * jnp.exp(b_mid - bc)
    qe = (qt * jnp.exp(b_mid)).astype(BF16)
    ke = kt * jnp.exp(b_last - b_mid)
    ktb = kt.astype(BF16)
    lane = lax.broadcasted_iota(jnp.int32, (1, hk), 1)
    ci = lax.broadcasted_iota(jnp.int32, (c, c), 0)
    cj = lax.broadcasted_iota(jnp.int32, (c, c), 1)
    causal = (cj >= ci) if reverse else (cj <= ci)
    blockdiag = (lax.broadcasted_iota(jnp.int32, (hk, hv), 0) // GLA_DK
                 == lax.broadcasted_iota(jnp.int32, (hk, hv), 1) // GLA_DV)
    intra, upd, decay = [], [], []
    for i in range(n):
        rows = slice(i * c, (i + 1) * c)
        qs = jnp.concatenate(
            [jnp.where((lane >= h * GLA_DK) & (lane < (h + 1) * GLA_DK), qt[rows], 0.0) for h in range(GLA_HEADS)],
            axis=0).astype(BF16)
        att = _dot_nt(qs, ktb[rows])
        intra.append(jnp.concatenate(
            [_dot(jnp.where(causal, att[h * c:(h + 1) * c], 0.0).astype(BF16),
                  v[rows, h * GLA_DV:(h + 1) * GLA_DV]) for h in range(GLA_HEADS)], axis=-1))
        upd.append(jnp.where(blockdiag, _dot(ke[rows].T.astype(BF16), v[rows]), 0.0))
        decay.append(jnp.exp(jnp.sum(g[rows].T, axis=1, keepdims=True)))
    outs = [None] * n
    for i in (reversed(range(n)) if reverse else range(n)):
        rows = slice(i * c, (i + 1) * c)
        outs[i] = _dot(qe[rows], s.astype(BF16)) + intra[i]
        s = decay[i] * s + upd[i]
    return jnp.concatenate(outs, axis=0), s


def _gla_body(qkf_ref, qkb_ref, vf_ref, vb_ref, gf_ref, gb_ref, s0f_ref, s0b_ref,
              of_ref, ob_ref, sff_ref, sbf_ref, sf_ref, sb_ref, *, tile):
    n = pl.program_id(1)
    hk = GLA_HEADS * GLA_DK
    hv = GLA_HEADS * GLA_DV

    @pl.when(n == 0)
    def _():
        sf_ref[...] = jnp.zeros((hk, hv), F32)
        sb_ref[...] = jnp.zeros((hk, hv), F32)
        for h in range(GLA_HEADS):
            sf_ref[h * GLA_DK:(h + 1) * GLA_DK, h * GLA_DV:(h + 1) * GLA_DV] = s0f_ref[0, h]
            sb_ref[h * GLA_DK:(h + 1) * GLA_DK, h * GLA_DV:(h + 1) * GLA_DV] = s0b_ref[0, h]

    of_ref[0], sf_ref[...] = _gla_tile(qkf_ref[0], vf_ref[0], gf_ref[0], sf_ref[...], False)
    ob_ref[0], sb_ref[...] = _gla_tile(qkb_ref[0], vb_ref[0], gb_ref[0], sb_ref[...], True)

    @pl.when(n == pl.num_programs(1) - 1)
    def _():
        for h in range(GLA_HEADS):
            sff_ref[0, h] = sf_ref[h * GLA_DK:(h + 1) * GLA_DK, h * GLA_DV:(h + 1) * GLA_DV]
            sbf_ref[0, h] = sb_ref[h * GLA_DK:(h + 1) * GLA_DK, h * GLA_DV:(h + 1) * GLA_DV]


def _gla(qk, v, gates, s0f, s0b, tile):
    b, l, _ = qk.shape
    hk = GLA_HEADS * GLA_DK
    hv = GLA_HEADS * GLA_DV
    nt = l // tile
    fwd = lambda w, col: pl.BlockSpec((1, tile, w), lambda i, n: (i, n, col))
    bwd = lambda w, col: pl.BlockSpec((1, tile, w), lambda i, n: (i, nt - 1 - n, col))
    st = pl.BlockSpec((1, GLA_HEADS, GLA_DK, GLA_DV), lambda i, n: (i, 0, 0, 0))
    return pl.pallas_call(
        functools.partial(_gla_body, tile=tile),
        grid=(b, nt),
        in_specs=[fwd(2 * hk, 0), bwd(2 * hk, 0), fwd(hv, 0), bwd(hv, 0), fwd(hk, 0), bwd(hk, 1), st, st],
        out_specs=[fwd(hv, 0), bwd(hv, 0), st, st],
        out_shape=[jax.ShapeDtypeStruct((b, l, hv), F32), jax.ShapeDtypeStruct((b, l, hv), F32),
                   jax.ShapeDtypeStruct((b, GLA_HEADS, GLA_DK, GLA_DV), F32),
                   jax.ShapeDtypeStruct((b, GLA_HEADS, GLA_DK, GLA_DV), F32)],
        scratch_shapes=[pltpu.VMEM((hk, hv), F32), pltpu.VMEM((hk, hv), F32)],
        compiler_params=_cparams(("parallel", "arbitrary")),
        name="gla_scan",
    )(qk, qk, v, v, gates, gates, s0f, s0b)


def _odd_mid_body(cur_ref, prev_ref, next_ref, of_ref, ob_ref, r_ref, hg_ref, pw_ref, ps_ref,
                  pool_ref, d_ref, buf_ref, *, tile, seq):
    j = pl.program_id(1)
    last = pl.num_programs(1) - 1
    hal = POOL_HALO
    buf_ref[0:hal, :] = jnp.where(j == 0, 0.0, prev_ref[0])
    buf_ref[hal:hal + tile, :] = cur_ref[0]
    buf_ref[hal + tile:hal + tile + hal, :] = jnp.where(j == last, 0.0, next_ref[0])
    t = j * tile + lax.broadcasted_iota(jnp.int32, (tile, 1), 0)
    for gi, win in enumerate(POOL_WINDOWS):
        cols = slice(gi * POOL_GROUP, (gi + 1) * POOL_GROUP)
        acc = jnp.zeros((tile, POOL_GROUP), F32)
        for off in range(-(win // 2), win - win // 2):
            acc = acc + buf_ref[hal + off:hal + off + tile, cols]
        cnt = jnp.minimum(t + (win - win // 2), seq) - jnp.maximum(t - win // 2, 0)
        diff = acc / cnt.astype(F32) - cur_ref[0, :, cols]
        pool_ref[0, :, cols] = _dot(diff.astype(BF16), pw_ref[gi]) * ps_ref[:, cols]
    for h in range(GLA_HEADS):
        cols = slice(h * GLA_DV, (h + 1) * GLA_DV)
        o = of_ref[0, :, cols] + ob_ref[0, :, cols]
        d_ref[0, :, cols] = _rms(o, hg_ref[:, cols]) * _silu(r_ref[0, :, cols])


def _odd_mid(pool_u, o_f, o_b, r, head_g, pool_w_bf, pool_scale, tile):
    b, l, c = pool_u.shape
    hal = POOL_HALO
    per = tile // hal
    nh = l // hal
    blk = pl.BlockSpec((1, tile, c), lambda i, j: (i, j, 0))
    vec = pl.BlockSpec((1, c), lambda i, j: (0, 0))
    return pl.pallas_call(
        functools.partial(_odd_mid_body, tile=tile, seq=l),
        grid=(b, l // tile),
        in_specs=[blk,
                  pl.BlockSpec((1, hal, c), lambda i, j: (i, jnp.maximum(j * per - 1, 0), 0)),
                  pl.BlockSpec((1, hal, c), lambda i, j: (i, jnp.minimum((j + 1) * per, nh - 1), 0)),
                  blk, blk, blk, vec,
                  pl.BlockSpec((len(POOL_WINDOWS), POOL_GROUP, POOL_GROUP), lambda i, j: (0, 0, 0)),
                  vec],
        out_specs=[blk, blk],
        out_shape=[jax.ShapeDtypeStruct((b, l, c), F32), jax.ShapeDtypeStruct((b, l, c), F32)],
        scratch_shapes=[pltpu.VMEM((tile + 2 * hal, c), F32)],
        compiler_params=_cparams(("parallel", "parallel")),
        name="odd_mid",
    )(pool_u, pool_u, pool_u, o_f, o_b, r, head_g.reshape(1, c), pool_w_bf, pool_scale.reshape(1, c))


def _route(aff_t, tok_base, row_base):
    b, e, n = aff_t.shape
    cap = EC_CAPACITY_FACTOR * n // N_EXPERTS
    vals, idx = lax.top_k(aff_t, cap)
    idx, vals = lax.sort((idx, vals), dimension=2, num_keys=1)
    bi = jnp.arange(b, dtype=idx.dtype)[:, None, None]
    per_expert = lambda a: jnp.swapaxes(a, 0, 1).reshape(e, b * cap)
    rows0 = idx + row_base + 2 * bi * n
    return (per_expert(vals), per_expert(idx + tok_base + bi * n), per_expert(rows0), per_expert(rows0 + n),
            per_expert(idx))


def _combine_body(offs_ref, tok_ref, y_ref, o_ref, *, n_tok, cap):
    bi = pl.program_id(0)
    e = pl.program_id(2)
    tt = COMBINE_TILE
    wn = COMBINE_WINDOW
    ntiles = n_tok // tt

    @pl.when(e == 0)
    def _():
        o_ref[...] = jnp.zeros(o_ref.shape, F32)

    sub = lax.broadcasted_iota(jnp.int32, (tt, 1), 0)
    lane = lax.broadcasted_iota(jnp.int32, (1, wn), 1)
    base = (bi * pl.num_programs(2) + e) * (ntiles + 1)

    def add_window(j, lo):
        cs = pl.multiple_of(jnp.minimum(lo, cap - wn), LANES)
        tok = tok_ref[0, :, pl.ds(cs, wn)] - j * tt
        hit = (tok == sub) & (cs + lane >= lo)
        onehot = jnp.where(hit, 1.0, 0.0).astype(BF16)
        rows = pl.ds(pl.multiple_of(j * tt, tt), tt)
        o_ref[0, rows, :] += _dot(onehot, y_ref[0, pl.ds(cs, wn), :])

    def first_window(j, carry):
        add_window(j, (offs_ref[base + j] // LANES) * LANES)
        return carry

    lax.fori_loop(0, ntiles, first_window, 0, unroll=4)

    def more_windows(j, carry):
        start = (offs_ref[base + j] // LANES) * LANES
        count = (offs_ref[base + j + 1] - start + wn - 1) // wn

        def window(w, carry):
            add_window(j, start + w * wn)
            return carry

        lax.fori_loop(1, count, window, 0)
        return carry

    lax.fori_loop(0, ntiles, more_windows, 0)


def _combine(y, tok, offs, seg0, b, n_tok, cap):
    e, _, d = y.shape
    cols = COMBINE_COLS
    ntiles = n_tok // COMBINE_TILE
    assert seg0 % cap == 0 and n_tok % COMBINE_TILE == 0 and cap % COMBINE_WINDOW == 0
    seg = seg0 // cap
    grid_spec = pltpu.PrefetchScalarGridSpec(
        num_scalar_prefetch=1,
        grid=(b, d // cols, e),
        in_specs=[pl.BlockSpec((1, 1, cap), lambda i, q, k, offs: (k, 0, seg + i)),
                  pl.BlockSpec((1, cap, cols), lambda i, q, k, offs: (k, seg + i, q))],
        out_specs=pl.BlockSpec((1, n_tok, cols), lambda i, q, k, offs: (i, 0, q)),
    )
    del ntiles
    return pl.pallas_call(
        functools.partial(_combine_body, n_tok=n_tok, cap=cap),
        grid_spec=grid_spec,
        out_shape=jax.ShapeDtypeStruct((b, n_tok, d), F32),
        compiler_params=_cparams(("parallel", "parallel", "arbitrary")),
        name="combine",
    )(offs, tok, y)


def _gather_rows(src, idx):
    window = SC_GATHER_WINDOW
    n = idx.shape[0]
    width = src.shape[1]
    assert 2 * window * width * 4 <= SC_TILE_VMEM_BUDGET, width
    assert n % (window * SC_CORES * SC_SUBCORES) == 0, n
    mesh = plsc.VectorSubcoreMesh(core_axis_name="core", subcore_axis_name="subcore",
                                  num_cores=SC_CORES, num_subcores=SC_SUBCORES)

    @pl.kernel(out_type=jax.ShapeDtypeStruct((n, width), src.dtype), mesh=mesh, scratch_types=[],
               name="gather_rows")
    def gather(src_hbm, idx_hbm, out_hbm):
        def body(idx_vmem, out_vmem):
            pltpu.sync_copy(src_hbm.at[idx_vmem.at[0]], out_vmem)

        pltpu.emit_pipeline(
            body,
            grid=(n // window,),
            in_specs=[pl.BlockSpec((1, window), lambda i: (0, i))],
            out_specs=[pl.BlockSpec((window, width), lambda i: (i, 0))],
            core_axis_name=("core", "subcore"),
            dimension_semantics=(pltpu.PARALLEL,),
        )(idx_hbm, out_hbm)

    return gather(src, idx.reshape(1, n))


def _moe(parts, w_gate, w_up, w_down, layer):
    quarter = parts[0][1].shape[-1]
    d = 4 * quarter
    sizes = [h.shape[0] * h.shape[2] for _, h in parts]
    bases = [sum(sizes[:i]) for i in range(len(parts))]
    routed = [_route(a, base, 2 * base) for (a, _), base in zip(parts, bases)]
    src = jnp.concatenate([h.reshape(-1, quarter) for _, h in parts], axis=0)
    vals, flat, rows0, rows1, tok = (jnp.concatenate([r[i] for r in routed], axis=1) for i in range(5))
    e, m = flat.shape
    pad = -m % LANES
    vals, rows0, rows1, tok = (jnp.pad(a, ((0, 0), (0, pad))) for a in (vals, rows0, rows1, tok))
    m += pad
    rows = jnp.stack([rows0, rows1], axis=1)
    xg = _gather_rows(src, rows.reshape(-1)).reshape(e, 2, m, quarter)
    chunk = next(c for c in (512, 544, 384, 256, 128) if m % c == 0)
    y = _expert_ffn(xg, vals[..., None], w_gate, w_up, w_down, layer, chunk)
    outs = []
    seg0 = 0
    for (aff_t, h), r, base in zip(parts, routed, bases):
        b, _, n = aff_t.shape
        cap = r[0].shape[1] // b
        if n % COMBINE_TILE == 0 and cap % COMBINE_WINDOW == 0 and seg0 % cap == 0:
            bounds = jnp.arange(n // COMBINE_TILE + 1, dtype=jnp.int32) * COMBINE_TILE
            local = r[4].reshape(e, b, cap)
            offs = jnp.sum((local[..., None] < bounds).astype(jnp.int32), axis=2)
            offs = jnp.swapaxes(offs, 0, 1).reshape(-1)
            outs.append(_combine(y, tok[:, None, :], offs, seg0, b, n, cap))
        else:
            ids = r[1] - base
            part = y[:, seg0:seg0 + b * cap].astype(F32)
            out = jnp.zeros((b * n, d), F32).at[ids.reshape(-1)].add(part.reshape(-1, d))
            outs.append(out.reshape(b, n, d))
        seg0 += b * cap
    return outs


def kernel(x, c, ctx, c_ctx, w_mod, b_mod, norm_g, w_in_even, w_out_even, conv_w, conv_b, conv_ln_g,
           conv_ln_b, na_rpb, w_in_odd, w_out_odd, pool_w, pool_scale, gla_gate_w, gla_gate_b, gla_head_g,
           router_w, expert_w_gate, expert_w_up, expert_w_down):
    b, l, d = x.shape
    n_ctx = ctx.shape[1]
    tm = 512

    mod_rows = jnp.concatenate([c, c_ctx[None], jnp.zeros((8 - b - 1, d), F32)], axis=0)

    mod_all = _modulation(mod_rows, w_mod, b_mod)

    def modulation(i):
        mm = mod_all[i]
        m = mm[:b].reshape(b, 6, d)
        mc = jnp.broadcast_to(mm[b].reshape(1, 6, d), (b, 6, d))
        return m, mc

    m, mc = modulation(0)
    g = norm_g[0]
    w_in = w_in_even[0].astype(BF16)
    w_out = w_out_even[0].astype(BF16)
    glu, qkv = _even_in(x, g[0], m[:, 0], m[:, 1], w_in, tm)
    glu_c, qkv_c = _even_in(ctx, g[0], mc[:, 0], mc[:, 1], w_in, n_ctx)
    a_lat = _conv_branch(glu, conv_w[0], conv_b[0], conv_ln_g[0], conv_ln_b[0], 256)
    a_ctx = _conv_branch(glu_c, conv_w[0], conv_b[0], conv_ln_g[0], conv_ln_b[0], n_ctx)
    na = _neighbourhood_attention(qkv, qkv_c, na_rpb[0])
    att_c = _context_attention(qkv_c)
    x, h2, aff = _out_proj(a_lat, na, x, w_out, g[1], m[:, 2], g[2], m[:, 3], m[:, 4], router_w[0], tm)
    ctx, h2c, aff_c = _out_proj(a_ctx, att_c, ctx, w_out, g[1], mc[:, 2], g[2], mc[:, 3], mc[:, 4],
                                router_w[0], n_ctx)
    f, f_c = _moe([(aff, h2), (aff_c, h2c)], expert_w_gate, expert_w_up, expert_w_down, 0)
    x = _gated_residual(x, f, m[:, 5], g[3], tm)
    ctx = _gated_residual(ctx, f_c, mc[:, 5], g[3], n_ctx)

    m, mc = modulation(1)
    g = norm_g[1]
    w_in = w_in_odd[0].astype(BF16)
    w_out = w_out_odd[0].astype(BF16)
    cos_t, sin_t = _rope_tables(l)
    ones_t = jnp.ones((n_ctx, GLA_HEADS * GLA_DK), F32)
    _, qk_c, v_c, _, gate_c = _odd_in(ctx, g[0], mc[:, 0], mc[:, 1], w_in, ones_t, jnp.zeros_like(ones_t),
                                      gla_gate_w[0], gla_gate_b[0], n_ctx)
    s_zero = jnp.zeros((b, GLA_HEADS, GLA_DK, GLA_DV), F32)
    _, _, s_f, s_b = _gla(qk_c, v_c, gate_c, s_zero, s_zero, n_ctx)
    pool_u, qk, v, r, gate = _odd_in(x, g[0], m[:, 0], m[:, 1], w_in, cos_t, sin_t,
                                     gla_gate_w[0], gla_gate_b[0], tm)
    o_f, o_b, _, _ = _gla(qk, v, gate, s_f, s_b, 256)
    pool_y, d_lat = _odd_mid(pool_u, o_f, o_b, r, gla_head_g[0], pool_w[0].astype(BF16), pool_scale[0], 256)
    x, h2, aff = _out_proj(pool_y, d_lat, x, w_out, g[1], m[:, 2], g[2], m[:, 3], m[:, 4], router_w[1], tm)
    (f,) = _moe([(aff, h2)], expert_w_gate, expert_w_up, expert_w_down, 1)
    return _gated_residual(x, f, m[:, 5], g[3], tm)
```

```python
import functools

import jax
import jax.numpy as jnp
from jax import lax
from jax.experimental import pallas as pl
from jax.experimental.pallas import tpu as pltpu
from jax.experimental.pallas import tpu_sc as plsc

F32 = jnp.float32
BF16 = jnp.bfloat16
HIGHEST = lax.Precision.HIGHEST

D_MODEL = 1024
GRID_W = 64
EPS = 1e-6
CONV_CH = 512
CONV_WIDTH = 31
CONV_HALO = 16
NA_HEADS = 8
NA_HEAD_DIM = 64
NA_KR = 8
NA_KC = 16
NA_ROWS_PER_BLOCK = 4
NA_BLOCKS_PER_STEP = 2
NA_WIN_ROWS = 12
POOL_CH = 512
POOL_WINDOWS = (2, 4, 8, 16)
POOL_GROUP = 128
POOL_HALO = 8
GLA_HEADS = 4
GLA_DK = 64
GLA_DV = 128
GLA_RANK = 16
GLA_TAU = 16.0
GLA_CHUNK = 64
ROPE_BASE = 10000.0
N_EXPERTS = 16
EXPERT_FF = 2816
EC_CAPACITY_FACTOR = 2
LANES = 128
SUBLANES = 8
NEG_BIG = -1e30
VMEM_LIMIT = 56 * 1024 * 1024
SC_CORES = 2
SC_SUBCORES = 16
SC_TILE_VMEM_BUDGET = 400 * 1024
SC_GATHER_WINDOW = 128
COMBINE_TILE = 256
COMBINE_WINDOW = 256
COMBINE_FIRST = 64
BF16_ROWS = 16
COMBINE_COLS = 512


def _cparams(sem):
    return pltpu.CompilerParams(dimension_semantics=sem, vmem_limit_bytes=VMEM_LIMIT)


def _rms(x, g):
    return x * lax.rsqrt(jnp.mean(x * x, axis=-1, keepdims=True) + EPS) * g


def _sigmoid(x):
    return 1.0 / (1.0 + jnp.exp(-x))


def _silu(x):
    return x * _sigmoid(x)


def _dot(a, b):
    return jnp.dot(a, b, preferred_element_type=F32)


def _pack_bf16_pairs(h):
    half = h.shape[-1] // 2
    bits = lax.bitcast_convert_type(h.astype(BF16).astype(F32), jnp.uint32)
    packed = (bits[:, half:] & jnp.uint32(0xFFFF0000)) | (bits[:, :half] >> 16)
    return lax.bitcast_convert_type(packed, jnp.int32)


def _unpack_bf16_pairs(p):
    bits = lax.bitcast_convert_type(p, jnp.uint32)
    lo = lax.bitcast_convert_type(bits << 16, F32).astype(BF16)
    hi = lax.bitcast_convert_type(bits & jnp.uint32(0xFFFF0000), F32).astype(BF16)
    return lo, hi


def _dot_nt(a, b):
    return lax.dot_general(a, b, (((1,), (1,)), ((), ())), preferred_element_type=F32)


def _mod_body(c_ref, w_ref, b_ref, o_ref):
    o_ref[0] = jnp.dot(_silu(c_ref[...]), w_ref[0], precision=HIGHEST,
                       preferred_element_type=F32) + b_ref[0]


def _modulation(rows, w, b):
    depth, _, n = w.shape
    tn = 1536
    return pl.pallas_call(
        _mod_body,
        grid=(depth, n // tn),
        in_specs=[pl.BlockSpec((8, D_MODEL), lambda i, j: (0, 0)),
                  pl.BlockSpec((1, D_MODEL, tn), lambda i, j: (i, 0, j)),
                  pl.BlockSpec((1, 1, tn), lambda i, j: (i, 0, j))],
        out_specs=pl.BlockSpec((1, 8, tn), lambda i, j: (i, 0, j)),
        out_shape=jax.ShapeDtypeStruct((depth, 8, n), F32),
        compiler_params=_cparams(("parallel", "parallel")),
        name="modulation",
    )(rows, w, b.reshape(depth, 1, n))


def _even_in_body(x_ref, g_ref, sh_ref, sc_ref, w_ref, glu_ref, qkv_ref):
    h = (_rms(x_ref[0], g_ref[...]) * (1.0 + sc_ref[0]) + sh_ref[0]).astype(BF16)
    c = CONV_CH
    glu_ref[0] = _dot(h, w_ref[:, 0:c]) * _sigmoid(_dot(h, w_ref[:, c:2 * c]))
    hd = NA_HEADS * NA_HEAD_DIM
    q0 = 2 * c
    qkv_ref[0, :, 0:hd] = (_dot(h, w_ref[:, q0:q0 + hd]) * (NA_HEAD_DIM ** -0.5)).astype(BF16)
    qkv_ref[0, :, hd:3 * hd] = _dot(h, w_ref[:, q0 + hd:q0 + 3 * hd]).astype(BF16)


def _even_in(x, g, shift, scale, w_bf, tm):
    b, l, d = x.shape
    n = w_bf.shape[1]
    hd3 = 3 * NA_HEADS * NA_HEAD_DIM
    vec = pl.BlockSpec((1, 1, d), lambda i, j: (i, 0, 0))
    return pl.pallas_call(
        _even_in_body,
        grid=(b, l // tm),
        in_specs=[pl.BlockSpec((1, tm, d), lambda i, j: (i, j, 0)),
                  pl.BlockSpec((1, d), lambda i, j: (0, 0)),
                  vec, vec,
                  pl.BlockSpec((d, n), lambda i, j: (0, 0))],
        out_specs=[pl.BlockSpec((1, tm, CONV_CH), lambda i, j: (i, j, 0)),
                   pl.BlockSpec((1, tm, hd3), lambda i, j: (i, j, 0))],
        out_shape=[jax.ShapeDtypeStruct((b, l, CONV_CH), F32),
                   jax.ShapeDtypeStruct((b, l, hd3), BF16)],
        compiler_params=_cparams(("parallel", "parallel")),
        name="even_in",
    )(x, g.reshape(1, d), shift.reshape(b, 1, d), scale.reshape(b, 1, d), w_bf)


def _conv_body(cur_ref, prev_ref, next_ref, w_ref, b_ref, lg_ref, lb_ref, o_ref, buf_ref, sh_ref, *, tile, chunk):
    j = pl.program_id(1)
    last = pl.num_programs(1) - 1
    hal = CONV_HALO
    buf_ref[0:hal, :] = jnp.where(j == 0, 0.0, prev_ref[0])
    buf_ref[hal:hal + tile, :] = cur_ref[0]
    buf_ref[hal + tile:hal + tile + hal, :] = jnp.where(j == last, 0.0, next_ref[0])
    span = sh_ref.shape[1]
    for s in range(SUBLANES):
        sh_ref[s] = buf_ref[s:s + span, :]
    first = hal - CONV_WIDTH // 2
    reps = chunk // SUBLANES

    def rows(c, carry):
        r0 = pl.multiple_of(c * chunk, chunk)
        accs = [jnp.zeros((chunk, CONV_CH), F32) for _ in range(2)]
        for k in range(CONV_WIDTH):
            a, s = divmod(first + k, SUBLANES)
            wk = jnp.concatenate([w_ref[k]] * reps, axis=0)
            accs[k % 2] = accs[k % 2] + sh_ref[s, pl.ds(r0 + a * SUBLANES, chunk), :] * wk
        o_ref[0, pl.ds(r0, chunk), :] = accs[0] + accs[1]
        return carry

    lax.fori_loop(0, tile // chunk, rows, 0)
    y = o_ref[0] + b_ref[...]
    mu = jnp.mean(y, axis=-1, keepdims=True)
    yc = y - mu
    var = jnp.mean(yc * yc, axis=-1, keepdims=True)
    o_ref[0] = _silu(yc * lax.rsqrt(var + EPS) * lg_ref[...] + lb_ref[...])


def _conv_branch(glu, conv_w, conv_b, ln_g, ln_b, tile):
    b, l, c = glu.shape
    hal = CONV_HALO
    per = tile // hal
    nh = l // hal
    vec = pl.BlockSpec((1, c), lambda i, j: (0, 0))
    return pl.pallas_call(
        functools.partial(_conv_body, tile=tile, chunk=32),
        grid=(b, l // tile),
        in_specs=[pl.BlockSpec((1, tile, c), lambda i, j: (i, j, 0)),
                  pl.BlockSpec((1, hal, c), lambda i, j: (i, jnp.maximum(j * per - 1, 0), 0)),
                  pl.BlockSpec((1, hal, c), lambda i, j: (i, jnp.minimum((j + 1) * per, nh - 1), 0)),
                  pl.BlockSpec((CONV_WIDTH, SUBLANES, c), lambda i, j: (0, 0, 0)),
                  vec, vec, vec],
        out_specs=pl.BlockSpec((1, tile, c), lambda i, j: (i, j, 0)),
        out_shape=jax.ShapeDtypeStruct((b, l, c), F32),
        scratch_shapes=[pltpu.VMEM((tile + 2 * hal, c), F32),
                        pltpu.VMEM((SUBLANES, tile + 2 * hal - SUBLANES, c), F32)],
        compiler_params=_cparams(("parallel", "parallel")),
        name="conv_branch",
    )(glu, glu, glu, jnp.broadcast_to(conv_w[:, None, :], (CONV_WIDTH, SUBLANES, c)),
      conv_b.reshape(1, c), ln_g.reshape(1, c), ln_b.reshape(1, c))


def _na_window_start(j, rows):
    rb = NA_ROWS_PER_BLOCK
    return jnp.clip(j * rb - NA_KR // 2, 0, rows - NA_WIN_ROWS)


def _na_body(q_ref, k_ref, v_ref, kc_ref, vc_ref, *rest, rows):
    tab_refs, o_ref = rest[:-1], rest[-1]
    j = pl.program_id(2)
    nkeys = NA_WIN_ROWS * GRID_W
    tq = NA_ROWS_PER_BLOCK * GRID_W
    kc = kc_ref[0]
    vc = vc_ref[0]
    lane = lax.broadcasted_iota(jnp.int32, (1, LANES), 1)
    for sb, tab_ref in enumerate(tab_refs):
        start = pl.multiple_of(_na_window_start(j * len(tab_refs) + sb, rows) * GRID_W, GRID_W)
        q = q_ref[0, sb * tq:(sb + 1) * tq, :]
        kw = k_ref[0, pl.ds(start, nkeys), :]
        vw = v_ref[0, pl.ds(start, nkeys), :]
        out = jnp.zeros(q.shape, F32)
        for hh in range(LANES // NA_HEAD_DIM):
            in_head = (lane >= hh * NA_HEAD_DIM) & (lane < (hh + 1) * NA_HEAD_DIM)
            qh = jnp.where(in_head, q, jnp.zeros_like(q))
            s = _dot_nt(qh, kw) + tab_ref[0, hh]
            sc = _dot_nt(qh, kc)
            m = jnp.maximum(jnp.max(s, axis=-1, keepdims=True), jnp.max(sc, axis=-1, keepdims=True))
            p = jnp.exp(s - m)
            pc = jnp.exp(sc - m)
            denom = jnp.sum(p, axis=-1, keepdims=True) + jnp.sum(pc, axis=-1, keepdims=True)
            o = (_dot(p.astype(BF16), vw) + _dot(pc.astype(BF16), vc)) / denom
            out = jnp.where(in_head, o, out)
        o_ref[0, sb * tq:(sb + 1) * tq, :] = out


def _na_tables(rpb, rows):
    rb = NA_ROWS_PER_BLOCK
    nblk = rows // rb
    wr = NA_WIN_ROWS
    qc = jnp.arange(GRID_W)
    cs = jnp.clip(qc - NA_KC // 2, 0, GRID_W - NA_KC)
    col_ok = (qc[None, :] >= cs[:, None]) & (qc[None, :] < cs[:, None] + NA_KC)
    col_off = qc[None, :] - qc[:, None] + NA_KC - 1
    onehot = (col_off[:, :, None] == jnp.arange(2 * NA_KC - 1)[None, None, :]).astype(F32)
    blocks = jnp.einsum('hrd,qkd->hrqk', rpb.astype(F32), onehot, precision=HIGHEST)
    blocks = jnp.where(col_ok[None, None], blocks, NEG_BIG)
    masked = jnp.full((NA_HEADS, GRID_W, GRID_W), NEG_BIG, F32)
    tabs = []
    for jb in (0, 1, nblk - 1):
        ws = min(max(jb * rb - NA_KR // 2, 0), rows - wr)
        q_rows = []
        for qr in range(jb * rb, (jb + 1) * rb):
            rs = min(max(qr - NA_KR // 2, 0), rows - NA_KR)
            row = [blocks[:, kr - qr + NA_KR - 1] if rs <= kr < rs + NA_KR else masked
                   for kr in range(ws, ws + wr)]
            q_rows.append(jnp.concatenate(row, axis=-1))
        tabs.append(jnp.concatenate(q_rows, axis=1))
    return jnp.stack(tabs)


def _neighbourhood_attention(qkv, qkv_ctx, rpb):
    b, l, _ = qkv.shape
    n_ctx = qkv_ctx.shape[1]
    rows = l // GRID_W
    rb = NA_ROWS_PER_BLOCK
    nblk = rows // rb
    tq = rb * GRID_W
    nkeys = NA_WIN_ROWS * GRID_W
    hp = NA_HEADS * NA_HEAD_DIM // LANES
    tabs = _na_tables(rpb, rows)

    per = NA_BLOCKS_PER_STEP

    def cls(jb):
        return jnp.where(jb == 0, 0, jnp.where(jb == nblk - 1, 2, 1))

    def tab_spec(sb):
        return pl.BlockSpec((1, 2, tq, nkeys), lambda i, h, j: (cls(j * per + sb), h, 0, 0))

    return pl.pallas_call(
        functools.partial(_na_body, rows=rows),
        grid=(b, hp, nblk // per),
        in_specs=[pl.BlockSpec((1, per * tq, LANES), lambda i, h, j: (i, j, h)),
                  pl.BlockSpec((1, l, LANES), lambda i, h, j: (i, 0, hp + h)),
                  pl.BlockSpec((1, l, LANES), lambda i, h, j: (i, 0, 2 * hp + h)),
                  pl.BlockSpec((1, n_ctx, LANES), lambda i, h, j: (i, 0, hp + h)),
                  pl.BlockSpec((1, n_ctx, LANES), lambda i, h, j: (i, 0, 2 * hp + h))]
                 + [tab_spec(sb) for sb in range(per)],
        out_specs=pl.BlockSpec((1, per * tq, LANES), lambda i, h, j: (i, j, h)),
        out_shape=jax.ShapeDtypeStruct((b, l, NA_HEADS * NA_HEAD_DIM), F32),
        compiler_params=_cparams(("parallel", "parallel", "arbitrary")),
        name="neighbourhood_attention",
    )(qkv, qkv, qkv, qkv_ctx, qkv_ctx, *([tabs] * per))


def _ctx_attn_body(q_ref, k_ref, v_ref, o_ref):
    q = q_ref[0]
    k = k_ref[0]
    v = v_ref[0]
    lane = lax.broadcasted_iota(jnp.int32, (1, LANES), 1)
    out = jnp.zeros(q.shape, F32)
    for hh in range(LANES // NA_HEAD_DIM):
        in_head = (lane >= hh * NA_HEAD_DIM) & (lane < (hh + 1) * NA_HEAD_DIM)
        qh = jnp.where(in_head, q, jnp.zeros_like(q))
        s = _dot_nt(qh, k)
        p = jnp.exp(s - jnp.max(s, axis=-1, keepdims=True))
        o = _dot(p.astype(BF16), v) / jnp.sum(p, axis=-1, keepdims=True)
        out = jnp.where(in_head, o, out)
    o_ref[0] = out


def _context_attention(qkv_ctx):
    b, n, _ = qkv_ctx.shape
    hp = NA_HEADS * NA_HEAD_DIM // LANES
    return pl.pallas_call(
        _ctx_attn_body,
        grid=(b, hp),
        in_specs=[pl.BlockSpec((1, n, LANES), lambda i, h: (i, 0, h)),
                  pl.BlockSpec((1, n, LANES), lambda i, h: (i, 0, hp + h)),
                  pl.BlockSpec((1, n, LANES), lambda i, h: (i, 0, 2 * hp + h))],
        out_specs=pl.BlockSpec((1, n, LANES), lambda i, h: (i, 0, h)),
        out_shape=jax.ShapeDtypeStruct((b, n, NA_HEADS * NA_HEAD_DIM), F32),
        compiler_params=_cparams(("parallel", "parallel")),
        name="context_attention",
    )(qkv_ctx, qkv_ctx, qkv_ctx)


def _out_body(a_ref, b_ref, x_ref, w_ref, g1_ref, gate_ref, g2_ref, sh_ref, sc_ref, rwh_ref, rwl_ref,
              xo_ref, h_ref, aff_ref):
    half = a_ref.shape[-1]
    y = _dot(a_ref[0].astype(BF16), w_ref[0:half, :]) + _dot(b_ref[0].astype(BF16), w_ref[half:2 * half, :])
    xn = x_ref[0] + gate_ref[0] * _rms(y, g1_ref[...])
    xo_ref[0] = xn
    h = _rms(xn, g2_ref[...]) * (1.0 + sc_ref[0]) + sh_ref[0]
    packed = _pack_bf16_pairs(h)
    quarter = packed.shape[-1] // 2
    h_ref[0, 0] = packed[:, 0:quarter]
    h_ref[0, 1] = packed[:, quarter:2 * quarter]
    h_hi = h.astype(BF16)
    h_lo = (h - h_hi.astype(F32)).astype(BF16)
    logits = _dot(h_hi, rwh_ref[...]) + (_dot(h_lo, rwh_ref[...]) + _dot(h_hi, rwl_ref[...]))
    lane = lax.broadcasted_iota(jnp.int32, (1, LANES), 1)
    logits = jnp.where(lane < N_EXPERTS, logits, NEG_BIG)
    e = jnp.exp(logits - jnp.max(logits, axis=-1, keepdims=True))
    aff = e / jnp.sum(e, axis=-1, keepdims=True)
    aff_ref[0] = aff.T[0:N_EXPERTS, :]


def _out_proj(a, b2, x, w_bf, g1, gate, g2, shift, scale, router_w, tm):
    b, l, d = x.shape
    half = a.shape[-1]
    rw = jnp.pad(router_w, ((0, 0), (0, LANES - N_EXPERTS)))
    rw_hi = rw.astype(BF16)
    rw_lo = (rw - rw_hi.astype(F32)).astype(BF16)
    vec = pl.BlockSpec((1, d), lambda i, j: (0, 0))
    bvec = pl.BlockSpec((1, 1, d), lambda i, j: (i, 0, 0))
    rspec = pl.BlockSpec((d, LANES), lambda i, j: (0, 0))
    return pl.pallas_call(
        _out_body,
        grid=(b, l // tm),
        in_specs=[pl.BlockSpec((1, tm, half), lambda i, j: (i, j, 0)),
                  pl.BlockSpec((1, tm, half), lambda i, j: (i, j, 0)),
                  pl.BlockSpec((1, tm, d), lambda i, j: (i, j, 0)),
                  pl.BlockSpec((2 * half, d), lambda i, j: (0, 0)),
                  vec, bvec, vec, bvec, bvec, rspec, rspec],
        out_specs=[pl.BlockSpec((1, tm, d), lambda i, j: (i, j, 0)),
                   pl.BlockSpec((1, 2, tm, d // 4), lambda i, j: (i, 0, j, 0)),
                   pl.BlockSpec((1, N_EXPERTS, tm), lambda i, j: (i, 0, j))],
        out_shape=[jax.ShapeDtypeStruct((b, l, d), F32),
                   jax.ShapeDtypeStruct((b, 2, l, d // 4), jnp.int32),
                   jax.ShapeDtypeStruct((b, N_EXPERTS, l), F32)],
        compiler_params=_cparams(("parallel", "parallel")),
        name="out_proj",
    )(a, b2, x, w_bf, g1.reshape(1, d), gate.reshape(b, 1, d), g2.reshape(1, d),
      shift.reshape(b, 1, d), scale.reshape(b, 1, d), rw_hi, rw_lo)


def _moe_body(x_ref, val_ref, wg_ref, wu_ref, wd_ref, o_ref, acc_s, x_s, wg_s, wu_s, wd_s, *, chunk):
    f = pl.program_id(1)
    m = x_ref.shape[2]
    quarter = x_ref.shape[3]

    @pl.when(f == 0)
    def _():
        def unpack(c, carry):
            r = pl.multiple_of(c * chunk, chunk)
            for s in range(2):
                lo, hi = _unpack_bf16_pairs(x_ref[0, s, pl.ds(r, chunk), :])
                x_s[pl.ds(r, chunk), s * quarter:(s + 1) * quarter] = lo
                x_s[pl.ds(r, chunk), (2 + s) * quarter:(3 + s) * quarter] = hi
            acc_s[pl.ds(r, chunk), :] = jnp.zeros((chunk, 4 * quarter), F32)
            return carry

        lax.fori_loop(0, m // chunk, unpack, 0)

    wg_s[...] = wg_ref[0, 0].astype(BF16)
    wu_s[...] = wu_ref[0, 0].astype(BF16)
    wd_s[...] = wd_ref[0, 0].astype(BF16)

    def rows(c, carry):
        r = pl.multiple_of(c * chunk, chunk)
        xs = x_s[pl.ds(r, chunk), :]
        hid = (_silu(_dot(xs, wg_s[...])) * _dot(xs, wu_s[...])).astype(BF16)
        acc_s[pl.ds(r, chunk), :] += _dot(hid, wd_s[...])
        return carry

    lax.fori_loop(0, m // chunk, rows, 0, unroll=True)

    @pl.when(f == pl.num_programs(1) - 1)
    def _():
        o_ref[0] = (acc_s[...] * val_ref[0]).astype(o_ref.dtype)


def _expert_ffn(xg, vals, w_gate, w_up, w_down, layer, chunk):
    e, _, m, quarter = xg.shape
    d = 4 * quarter
    ff = w_gate.shape[-1]
    tf = 256
    return pl.pallas_call(
        functools.partial(_moe_body, chunk=chunk),
        grid=(e, ff // tf),
        in_specs=[pl.BlockSpec((1, 2, m, quarter), lambda i, f: (i, 0, 0, 0)),
                  pl.BlockSpec((1, m, 1), lambda i, f: (i, 0, 0)),
                  pl.BlockSpec((1, 1, d, tf), lambda i, f: (layer, i, 0, f)),
                  pl.BlockSpec((1, 1, d, tf), lambda i, f: (layer, i, 0, f)),
                  pl.BlockSpec((1, 1, tf, d), lambda i, f: (layer, i, f, 0))],
        out_specs=pl.BlockSpec((1, m, d), lambda i, f: (i, 0, 0)),
        out_shape=jax.ShapeDtypeStruct((e, m, d), BF16),
        scratch_shapes=[pltpu.VMEM((m, d), F32), pltpu.VMEM((m, d), BF16), pltpu.VMEM((d, tf), BF16),
                        pltpu.VMEM((d, tf), BF16), pltpu.VMEM((tf, d), BF16)],
        compiler_params=_cparams(("parallel", "arbitrary")),
        name="expert_ffn",
    )(xg, vals, w_gate, w_up, w_down)


def _resid_body(x_ref, f_ref, gate_ref, g_ref, o_ref):
    o_ref[0] = x_ref[0] + gate_ref[0] * _rms(f_ref[0], g_ref[...])


def _gated_residual(x, f, gate, g, tm):
    b, l, d = x.shape
    blk = pl.BlockSpec((1, tm, d), lambda i, j: (i, j, 0))
    return pl.pallas_call(
        _resid_body,
        grid=(b, l // tm),
        in_specs=[blk, blk, pl.BlockSpec((1, 1, d), lambda i, j: (i, 0, 0)),
                  pl.BlockSpec((1, d), lambda i, j: (0, 0))],
        out_specs=blk,
        out_shape=jax.ShapeDtypeStruct((b, l, d), F32),
        compiler_params=_cparams(("parallel", "parallel")),
        name="gated_residual",
    )(x, f, gate.reshape(b, 1, d), g.reshape(1, d))


def _swap_pairs(x):
    nf = GLA_DK // 4
    lane = lax.broadcasted_iota(jnp.int32, (1, LANES), 1)
    up = pltpu.roll(x, LANES - nf, 1)
    down = pltpu.roll(x, nf, 1)
    return jnp.where(lane % (2 * nf) < nf, up, down)


def _odd_in_body(x_ref, g_ref, sh_ref, sc_ref, w_ref, cos_ref, sin_ref, gw_ref, gb_ref,
                 pool_ref, qk_ref, v_ref, r_ref, gate_ref):
    h = (_rms(x_ref[0], g_ref[...]) * (1.0 + sc_ref[0]) + sh_ref[0]).astype(BF16)
    qk = GLA_HEADS * GLA_DK
    vd = GLA_HEADS * GLA_DV
    q0 = POOL_CH
    v0 = q0 + 2 * qk
    r0 = v0 + vd
    l0 = r0 + vd
    pool_ref[0] = _dot(h, w_ref[:, 0:q0])
    for s in range(2 * qk // LANES):
        raw = _dot(h, w_ref[:, q0 + s * LANES:q0 + (s + 1) * LANES])
        c = cos_ref[:, (s * LANES) % qk:(s * LANES) % qk + LANES]
        sn = sin_ref[:, (s * LANES) % qk:(s * LANES) % qk + LANES]
        rot = raw * c + _swap_pairs(raw) * sn
        if s * LANES < qk:
            rot = rot * (GLA_DK ** -0.5)
        qk_ref[0, :, s * LANES:(s + 1) * LANES] = rot
    v_ref[0] = _dot(h, w_ref[:, v0:r0]).astype(BF16)
    r_ref[0] = _dot(h, w_ref[:, r0:l0])
    lr = _dot(h, w_ref[:, l0:l0 + 2 * GLA_RANK])
    z = jnp.dot(lr, gw_ref[...], precision=HIGHEST, preferred_element_type=F32) + gb_ref[...]
    gate_ref[0] = (jnp.minimum(z, 0.0) - jnp.log1p(jnp.exp(-jnp.abs(z)))) * (1.0 / GLA_TAU)


def _odd_in(x, g, shift, scale, w_bf, cos_t, sin_t, gate_w, gate_b, tm):
    b, l, d = x.shape
    n = w_bf.shape[1]
    qk = GLA_HEADS * GLA_DK
    vd = GLA_HEADS * GLA_DV
    gw = jnp.zeros((2 * GLA_RANK, 2 * qk), F32)
    gw = gw.at[:GLA_RANK, :qk].set(gate_w[0]).at[GLA_RANK:, qk:].set(gate_w[1])
    gb = jnp.concatenate([gate_b[0], gate_b[1]]).reshape(1, 2 * qk)
    vec = pl.BlockSpec((1, 1, d), lambda i, j: (i, 0, 0))
    row = lambda w: pl.BlockSpec((1, tm, w), lambda i, j: (i, j, 0))
    return pl.pallas_call(
        _odd_in_body,
        grid=(b, l // tm),
        in_specs=[row(d), pl.BlockSpec((1, d), lambda i, j: (0, 0)), vec, vec,
                  pl.BlockSpec((d, n), lambda i, j: (0, 0)),
                  pl.BlockSpec((tm, qk), lambda i, j: (j, 0)),
                  pl.BlockSpec((tm, qk), lambda i, j: (j, 0)),
                  pl.BlockSpec((2 * GLA_RANK, 2 * qk), lambda i, j: (0, 0)),
                  pl.BlockSpec((1, 2 * qk), lambda i, j: (0, 0))],
        out_specs=[row(POOL_CH), row(2 * qk), row(vd), row(vd), row(2 * qk)],
        out_shape=[jax.ShapeDtypeStruct((b, l, POOL_CH), F32),
                   jax.ShapeDtypeStruct((b, l, 2 * qk), F32),
                   jax.ShapeDtypeStruct((b, l, vd), BF16),
                   jax.ShapeDtypeStruct((b, l, vd), F32),
                   jax.ShapeDtypeStruct((b, l, 2 * qk), F32)],
        compiler_params=_cparams(("parallel", "parallel")),
        name="odd_in",
    )(x, g.reshape(1, d), shift.reshape(b, 1, d), scale.reshape(b, 1, d), w_bf, cos_t, sin_t, gw, gb)


def _rope_tables(l):
    t = jnp.arange(l)
    pos_r = (t // GRID_W).astype(F32)
    pos_c = (t % GRID_W).astype(F32)
    nf = GLA_DK // 4
    inv = jnp.power(ROPE_BASE, -jnp.arange(nf, dtype=F32) / nf)
    ar = pos_r[:, None] * inv[None, :]
    ac = pos_c[:, None] * inv[None, :]
    cos_h = jnp.concatenate([jnp.cos(ar), jnp.cos(ar), jnp.cos(ac), jnp.cos(ac)], axis=-1)
    sin_h = jnp.concatenate([-jnp.sin(ar), jnp.sin(ar), -jnp.sin(ac), jnp.sin(ac)], axis=-1)
    return jnp.tile(cos_h, (1, GLA_HEADS)), jnp.tile(sin_h, (1, GLA_HEADS))


def _gla_tile(qk, v, g, s, reverse):
    hk = GLA_HEADS * GLA_DK
    hv = GLA_HEADS * GLA_DV
    c = GLA_CHUNK
    t = qk.shape[0]
    n = t // c
    last_row, mid_row = (0, c // 2) if reverse else (c - 1, c // 2 - 1)
    ii = lax.broadcasted_iota(jnp.int32, (t, t), 0)
    jj = lax.broadcasted_iota(jnp.int32, (t, t), 1)
    ordered = (jj >= ii) if reverse else (jj <= ii)
    tri = jnp.where(ordered & (ii // c == jj // c), 1.0, 0.0).astype(BF16)
    g_hi = g.astype(BF16)
    rem = g - g_hi.astype(F32)
    g_mid = rem.astype(BF16)
    g_lo = (rem - g_mid.astype(F32)).astype(BF16)
    bc = _dot(tri, g_hi) + (_dot(tri, g_mid) + _dot(tri, g_lo))
    spread = lambda row: jnp.concatenate(
        [jnp.broadcast_to(bc[i * c + row:i * c + row + 1, :], (c, hk)) for i in range(n)], axis=0)
    b_mid = spread(mid_row)
    b_last = spread(last_row)
    qt = qk[:, 0:hk] * jnp.exp(bc - b_mid)
    kt = qk[:, hk:2 * hk] * jnp.exp(b_mid - bc)
    qe = (qt * jnp.exp(b_mid)).astype(BF16)
    ke = kt * jnp.exp(b_last - b_mid)
    ktb = kt.astype(BF16)
    lane = lax.broadcasted_iota(jnp.int32, (1, hk), 1)
    ci = lax.broadcasted_iota(jnp.int32, (c, c), 0)
    cj = lax.broadcasted_iota(jnp.int32, (c, c), 1)
    causal = (cj >= ci) if reverse else (cj <= ci)
    blockdiag = (lax.broadcasted_iota(jnp.int32, (hk, hv), 0) // GLA_DK
                 == lax.broadcasted_iota(jnp.int32, (hk, hv), 1) // GLA_DV)
    intra, upd, decay = [], [], []
    for i in range(n):
        rows = slice(i * c, (i + 1) * c)
        qs = jnp.concatenate(
            [jnp.where((lane >= h * GLA_DK) & (lane < (h + 1) * GLA_DK), qt[rows], 0.0) for h in range(GLA_HEADS)],
            axis=0).astype(BF16)
        att = _dot_nt(qs, ktb[rows])
        intra.append(jnp.concatenate(
            [_dot(jnp.where(causal, att[h * c:(h + 1) * c], 0.0).astype(BF16),
                  v[rows, h * GLA_DV:(h + 1) * GLA_DV]) for h in range(GLA_HEADS)], axis=-1))
        upd.append(jnp.where(blockdiag, _dot(ke[rows].T.astype(BF16), v[rows]), 0.0))
        decay.append(jnp.exp(jnp.sum(g[rows].T, axis=1, keepdims=True)))
    outs = [None] * n
    for i in (reversed(range(n)) if reverse else range(n)):
        rows = slice(i * c, (i + 1) * c)
        outs[i] = _dot(qe[rows], s.astype(BF16)) + intra[i]
        s = decay[i] * s + upd[i]
    return jnp.concatenate(outs, axis=0), s


def _gla_body(qkf_ref, qkb_ref, vf_ref, vb_ref, gf_ref, gb_ref, s0f_ref, s0b_ref,
              of_ref, ob_ref, sff_ref, sbf_ref, sf_ref, sb_ref, *, tile):
    n = pl.program_id(1)
    hk = GLA_HEADS * GLA_DK
    hv = GLA_HEADS * GLA_DV

    @pl.when(n == 0)
    def _():
        sf_ref[...] = jnp.zeros((hk, hv), F32)
        sb_ref[...] = jnp.zeros((hk, hv), F32)
        for h in range(GLA_HEADS):
            sf_ref[h * GLA_DK:(h + 1) * GLA_DK, h * GLA_DV:(h + 1) * GLA_DV] = s0f_ref[0, h]
            sb_ref[h * GLA_DK:(h + 1) * GLA_DK, h * GLA_DV:(h + 1) * GLA_DV] = s0b_ref[0, h]

    of_ref[0], sf_ref[...] = _gla_tile(qkf_ref[0], vf_ref[0], gf_ref[0], sf_ref[...], False)
    ob_ref[0], sb_ref[...] = _gla_tile(qkb_ref[0], vb_ref[0], gb_ref[0], sb_ref[...], True)

    @pl.when(n == pl.num_programs(1) - 1)
    def _():
        for h in range(GLA_HEADS):
            sff_ref[0, h] = sf_ref[h * GLA_DK:(h + 1) * GLA_DK, h * GLA_DV:(h + 1) * GLA_DV]
            sbf_ref[0, h] = sb_ref[h * GLA_DK:(h + 1) * GLA_DK, h * GLA_DV:(h + 1) * GLA_DV]


def _gla(qk, v, gates, s0f, s0b, tile):
    b, l, _ = qk.shape
    hk = GLA_HEADS * GLA_DK
    hv = GLA_HEADS * GLA_DV
    nt = l // tile
    fwd = lambda w, col: pl.BlockSpec((1, tile, w), lambda i, n: (i, n, col))
    bwd = lambda w, col: pl.BlockSpec((1, tile, w), lambda i, n: (i, nt - 1 - n, col))
    st = pl.BlockSpec((1, GLA_HEADS, GLA_DK, GLA_DV), lambda i, n: (i, 0, 0, 0))
    return pl.pallas_call(
        functools.partial(_gla_body, tile=tile),
        grid=(b, nt),
        in_specs=[fwd(2 * hk, 0), bwd(2 * hk, 0), fwd(hv, 0), bwd(hv, 0), fwd(hk, 0), bwd(hk, 1), st, st],
        out_specs=[fwd(hv, 0), bwd(hv, 0), st, st],
        out_shape=[jax.ShapeDtypeStruct((b, l, hv), F32), jax.ShapeDtypeStruct((b, l, hv), F32),
                   jax.ShapeDtypeStruct((b, GLA_HEADS, GLA_DK, GLA_DV), F32),
                   jax.ShapeDtypeStruct((b, GLA_HEADS, GLA_DK, GLA_DV), F32)],
        scratch_shapes=[pltpu.VMEM((hk, hv), F32), pltpu.VMEM((hk, hv), F32)],
        compiler_params=_cparams(("parallel", "arbitrary")),
        name="gla_scan",
    )(qk, qk, v, v, gates, gates, s0f, s0b)


def _odd_mid_body(cur_ref, prev_ref, next_ref, of_ref, ob_ref, r_ref, hg_ref, pw_ref, ps_ref,
                  pool_ref, d_ref, buf_ref, *, tile, seq):
    j = pl.program_id(1)
    last = pl.num_programs(1) - 1
    hal = POOL_HALO
    buf_ref[0:hal, :] = jnp.where(j == 0, 0.0, prev_ref[0])
    buf_ref[hal:hal + tile, :] = cur_ref[0]
    buf_ref[hal + tile:hal + tile + hal, :] = jnp.where(j == last, 0.0, next_ref[0])
    t = j * tile + lax.broadcasted_iota(jnp.int32, (tile, 1), 0)
    for gi, win in enumerate(POOL_WINDOWS):
        cols = slice(gi * POOL_GROUP, (gi + 1) * POOL_GROUP)
        acc = jnp.zeros((tile, POOL_GROUP), F32)
        for off in range(-(win // 2), win - win // 2):
            acc = acc + buf_ref[hal + off:hal + off + tile, cols]
        cnt = jnp.minimum(t + (win - win // 2), seq) - jnp.maximum(t - win // 2, 0)
        diff = acc / cnt.astype(F32) - cur_ref[0, :, cols]
        pool_ref[0, :, cols] = _dot(diff.astype(BF16), pw_ref[gi]) * ps_ref[:, cols]
    for h in range(GLA_HEADS):
        cols = slice(h * GLA_DV, (h + 1) * GLA_DV)
        o = of_ref[0, :, cols] + ob_ref[0, :, cols]
        d_ref[0, :, cols] = _rms(o, hg_ref[:, cols]) * _silu(r_ref[0, :, cols])


def _odd_mid(pool_u, o_f, o_b, r, head_g, pool_w_bf, pool_scale, tile):
    b, l, c = pool_u.shape
    hal = POOL_HALO
    per = tile // hal
    nh = l // hal
    blk = pl.BlockSpec((1, tile, c), lambda i, j: (i, j, 0))
    vec = pl.BlockSpec((1, c), lambda i, j: (0, 0))
    return pl.pallas_call(
        functools.partial(_odd_mid_body, tile=tile, seq=l),
        grid=(b, l // tile),
        in_specs=[blk,
                  pl.BlockSpec((1, hal, c), lambda i, j: (i, jnp.maximum(j * per - 1, 0), 0)),
                  pl.BlockSpec((1, hal, c), lambda i, j: (i, jnp.minimum((j + 1) * per, nh - 1), 0)),
                  blk, blk, blk, vec,
                  pl.BlockSpec((len(POOL_WINDOWS), POOL_GROUP, POOL_GROUP), lambda i, j: (0, 0, 0)),
                  vec],
        out_specs=[blk, blk],
        out_shape=[jax.ShapeDtypeStruct((b, l, c), F32), jax.ShapeDtypeStruct((b, l, c), F32)],
        scratch_shapes=[pltpu.VMEM((tile + 2 * hal, c), F32)],
        compiler_params=_cparams(("parallel", "parallel")),
        name="odd_mid",
    )(pool_u, pool_u, pool_u, o_f, o_b, r, head_g.reshape(1, c), pool_w_bf, pool_scale.reshape(1, c))


def _route(aff_t, tok_base, row_base):
    b, e, n = aff_t.shape
    cap = EC_CAPACITY_FACTOR * n // N_EXPERTS
    vals, idx = lax.top_k(aff_t, cap)
    idx, vals = lax.sort((idx, vals), dimension=2, num_keys=1)
    bi = jnp.arange(b, dtype=idx.dtype)[:, None, None]
    per_expert = lambda a: jnp.swapaxes(a, 0, 1).reshape(e, b * cap)
    rows0 = idx + row_base + 2 * bi * n
    return (per_expert(vals), per_expert(idx + tok_base + bi * n), per_expert(rows0), per_expert(rows0 + n),
            per_expert(idx))


def _combine_first_start(p0, cap):
    return jnp.minimum((p0 // BF16_ROWS) * BF16_ROWS, cap - COMBINE_FIRST)


def _combine_body(offs_ref, spill_ref, tokwin_ref, tok_ref, y_ref, o_ref, *, n_tok, cap):
    bi = pl.program_id(0)
    j = pl.program_id(2)
    tt = COMBINE_TILE
    wf = COMBINE_FIRST
    wn = COMBINE_WINDOW
    ntiles = n_tok // tt
    n_exp = y_ref.shape[0]
    group = wn // wf
    sub = lax.broadcasted_iota(jnp.int32, (tt, 1), 0)
    lane = lax.broadcasted_iota(jnp.int32, (1, wn), 1)

    def slot_range(e):
        base = (bi * n_exp + e) * (ntiles + 1) + j
        return offs_ref[base], offs_ref[base + 1]

    acc = jnp.zeros(o_ref.shape[1:], F32)
    for g in range(n_exp // group):
        ys = []
        for e in range(g * group, (g + 1) * group):
            start = pl.multiple_of(_combine_first_start(slot_range(e)[0], cap), BF16_ROWS)
            ys.append(y_ref[e, pl.ds(start, wf), :])
        hit = tokwin_ref[0, 0, :, g * wn:(g + 1) * wn] == sub
        acc = acc + _dot(jnp.where(hit, 1.0, 0.0).astype(BF16), jnp.concatenate(ys, axis=0))
    o_ref[0] = acc

    def more_windows(e, carry):
        p0, p1 = slot_range(e)
        lo = _combine_first_start(p0, cap) + wf
        first = (lo // LANES) * LANES

        def extra(w, carry):
            cs = pl.multiple_of(jnp.minimum(first + w * wn, cap - wn), LANES)
            tok = tok_ref[e, :, pl.ds(cs, wn)] - j * tt
            hit = (tok == sub) & (cs + lane >= jnp.maximum(lo, first + w * wn))
            o_ref[0] += _dot(jnp.where(hit, 1.0, 0.0).astype(BF16), y_ref[e, pl.ds(cs, wn), :])
            return carry

        lax.fori_loop(0, (jnp.maximum(p1 - first, 0) + wn - 1) // wn * (p1 > lo).astype(jnp.int32), extra, 0)
        return carry

    @pl.when(spill_ref[bi * ntiles + j] != 0)
    def _():
        lax.fori_loop(0, n_exp, more_windows, 0)


def _combine(y, tok, local, seg0, n_tok):
    e, _, d = y.shape
    _, b, cap = local.shape
    cols = COMBINE_COLS
    tt = COMBINE_TILE
    wf = COMBINE_FIRST
    ntiles = n_tok // tt
    assert seg0 % cap == 0 and n_tok % tt == 0 and cap % COMBINE_WINDOW == 0 and e % (COMBINE_WINDOW // wf) == 0
    seg = seg0 // cap
    bounds = jnp.arange(ntiles + 1, dtype=jnp.int32) * tt
    offs = jnp.sum((local[..., None] < bounds).astype(jnp.int32), axis=2)
    starts = _combine_first_start(offs[..., :-1], cap)
    take = lambda row, s: lax.dynamic_slice(row, (s,), (wf,))
    tokwin = jax.vmap(jax.vmap(jax.vmap(take, (None, 0)), (0, 0)), (0, 0))(local, starts)
    tokwin = tokwin - bounds[:-1, None]
    tokwin = jnp.transpose(tokwin, (1, 2, 0, 3)).reshape(b, ntiles, 1, e * wf)
    spill = jnp.any(offs[..., 1:] > starts + wf, axis=0).astype(jnp.int32).reshape(-1)
    grid_spec = pltpu.PrefetchScalarGridSpec(
        num_scalar_prefetch=2,
        grid=(b, d // cols, ntiles),
        in_specs=[pl.BlockSpec((1, 1, 1, e * wf), lambda i, q, j, offs, spill: (i, j, 0, 0)),
                  pl.BlockSpec((e, 1, cap), lambda i, q, j, offs, spill: (0, 0, seg + i)),
                  pl.BlockSpec((e, cap, cols), lambda i, q, j, offs, spill: (0, seg + i, q))],
        out_specs=pl.BlockSpec((1, tt, cols), lambda i, q, j, offs, spill: (i, j, q)),
    )
    return pl.pallas_call(
        functools.partial(_combine_body, n_tok=n_tok, cap=cap),
        grid_spec=grid_spec,
        out_shape=jax.ShapeDtypeStruct((b, n_tok, d), F32),
        compiler_params=_cparams(("parallel", "parallel", "parallel")),
        name="combine",
    )(jnp.swapaxes(offs, 0, 1).reshape(-1), spill, tokwin, tok, y)


def _gather_rows(src, idx):
    window = SC_GATHER_WINDOW
    n = idx.shape[0]
    width = src.shape[1]
    assert 2 * window * width * 4 <= SC_TILE_VMEM_BUDGET, width
    assert n % (window * SC_CORES * SC_SUBCORES) == 0, n
    mesh = plsc.VectorSubcoreMesh(core_axis_name="core", subcore_axis_name="subcore",
                                  num_cores=SC_CORES, num_subcores=SC_SUBCORES)

    @pl.kernel(out_type=jax.ShapeDtypeStruct((n, width), src.dtype), mesh=mesh, scratch_types=[],
               name="gather_rows")
    def gather(src_hbm, idx_hbm, out_hbm):
        def body(idx_vmem, out_vmem):
            pltpu.sync_copy(src_hbm.at[idx_vmem.at[0]], out_vmem)

        pltpu.emit_pipeline(
            body,
            grid=(n // window,),
            in_specs=[pl.BlockSpec((1, window), lambda i: (0, i))],
            out_specs=[pl.BlockSpec((window, width), lambda i: (i, 0))],
            core_axis_name=("core", "subcore"),
            dimension_semantics=(pltpu.PARALLEL,),
        )(idx_hbm, out_hbm)

    return gather(src, idx.reshape(1, n))


def _moe(parts, w_gate, w_up, w_down, layer):
    quarter = parts[0][1].shape[-1]
    d = 4 * quarter
    sizes = [h.shape[0] * h.shape[2] for _, h in parts]
    bases = [sum(sizes[:i]) for i in range(len(parts))]
    routed = [_route(a, base, 2 * base) for (a, _), base in zip(parts, bases)]
    src = jnp.concatenate([h.reshape(-1, quarter) for _, h in parts], axis=0)
    vals, flat, rows0, rows1, tok = (jnp.concatenate([r[i] for r in routed], axis=1) for i in range(5))
    e, m = flat.shape
    pad = -m % LANES
    vals, rows0, rows1, tok = (jnp.pad(a, ((0, 0), (0, pad))) for a in (vals, rows0, rows1, tok))
    m += pad
    rows = jnp.stack([rows0, rows1], axis=1)
    xg = _gather_rows(src, rows.reshape(-1)).reshape(e, 2, m, quarter)
    chunk = next(c for c in (512, 544, 384, 256, 128) if m % c == 0)
    y = _expert_ffn(xg, vals[..., None], w_gate, w_up, w_down, layer, chunk)
    outs = []
    seg0 = 0
    for (aff_t, h), r, base in zip(parts, routed, bases):
        b, _, n = aff_t.shape
        cap = r[0].shape[1] // b
        if n % COMBINE_TILE == 0 and cap % COMBINE_WINDOW == 0 and seg0 % cap == 0:
            outs.append(_combine(y, tok[:, None, :], r[4].reshape(e, b, cap), seg0, n))
        else:
            ids = r[1] - base
            part = y[:, seg0:seg0 + b * cap].astype(F32)
            out = jnp.zeros((b * n, d), F32).at[ids.reshape(-1)].add(part.reshape(-1, d))
            outs.append(out.reshape(b, n, d))
        seg0 += b * cap
    return outs


def kernel(x, c, ctx, c_ctx, w_mod, b_mod, norm_g, w_in_even, w_out_even, conv_w, conv_b, conv_ln_g,
           conv_ln_b, na_rpb, w_in_odd, w_out_odd, pool_w, pool_scale, gla_gate_w, gla_gate_b, gla_head_g,
           router_w, expert_w_gate, expert_w_up, expert_w_down):
    b, l, d = x.shape
    n_ctx = ctx.shape[1]
    tm = 512

    mod_rows = jnp.concatenate([c, c_ctx[None], jnp.zeros((8 - b - 1, d), F32)], axis=0)

    mod_all = _modulation(mod_rows, w_mod, b_mod)

    def modulation(i):
        mm = mod_all[i]
        m = mm[:b].reshape(b, 6, d)
        mc = jnp.broadcast_to(mm[b].reshape(1, 6, d), (b, 6, d))
        return m, mc

    m, mc = modulation(0)
    g = norm_g[0]
    w_in = w_in_even[0].astype(BF16)
    w_out = w_out_even[0].astype(BF16)
    glu, qkv = _even_in(x, g[0], m[:, 0], m[:, 1], w_in, tm)
    glu_c, qkv_c = _even_in(ctx, g[0], mc[:, 0], mc[:, 1], w_in, n_ctx)
    a_lat = _conv_branch(glu, conv_w[0], conv_b[0], conv_ln_g[0], conv_ln_b[0], 256)
    a_ctx = _conv_branch(glu_c, conv_w[0], conv_b[0], conv_ln_g[0], conv_ln_b[0], n_ctx)
    na = _neighbourhood_attention(qkv, qkv_c, na_rpb[0])
    att_c = _context_attention(qkv_c)
    x, h2, aff = _out_proj(a_lat, na, x, w_out, g[1], m[:, 2], g[2], m[:, 3], m[:, 4], router_w[0], tm)
    ctx, h2c, aff_c = _out_proj(a_ctx, att_c, ctx, w_out, g[1], mc[:, 2], g[2], mc[:, 3], mc[:, 4],
                                router_w[0], n_ctx)
    f, f_c = _moe([(aff, h2), (aff_c, h2c)], expert_w_gate, expert_w_up, expert_w_down, 0)
    x = _gated_residual(x, f, m[:, 5], g[3], tm)
    ctx = _gated_residual(ctx, f_c, mc[:, 5], g[3], n_ctx)

    m, mc = modulation(1)
    g = norm_g[1]
    w_in = w_in_odd[0].astype(BF16)
    w_out = w_out_odd[0].astype(BF16)
    cos_t, sin_t = _rope_tables(l)
    ones_t = jnp.ones((n_ctx, GLA_HEADS * GLA_DK), F32)
    _, qk_c, v_c, _, gate_c = _odd_in(ctx, g[0], mc[:, 0], mc[:, 1], w_in, ones_t, jnp.zeros_like(ones_t),
                                      gla_gate_w[0], gla_gate_b[0], n_ctx)
    s_zero = jnp.zeros((b, GLA_HEADS, GLA_DK, GLA_DV), F32)
    _, _, s_f, s_b = _gla(qk_c, v_c, gate_c, s_zero, s_zero, n_ctx)
    pool_u, qk, v, r, gate = _odd_in(x, g[0], m[:, 0], m[:, 1], w_in, cos_t, sin_t,
                                     gla_gate_w[0], gla_gate_b[0], tm)
    o_f, o_b, _, _ = _gla(qk, v, gate, s_f, s_b, 256)
    pool_y, d_lat = _odd_mid(pool_u, o_f, o_b, r, gla_head_g[0], pool_w[0].astype(BF16), pool_scale[0], 256)
    x, h2, aff = _out_proj(pool_y, d_lat, x, w_out, g[1], m[:, 2], g[2], m[:, 3], m[:, 4], router_w[1], tm)
    (f,) = _moe([(aff, h2)], expert_w_gate, expert_w_up, expert_w_down, 1)
    return _gated_residual(x, f, m[:, 5], g[3], tm)
```

```python
import functools

import jax
import jax.numpy as jnp
from jax import lax
from jax.experimental import pallas as pl
from jax.experimental.pallas import tpu as pltpu
from jax.experimental.pallas import tpu_sc as plsc

F32 = jnp.float32
BF16 = jnp.bfloat16
HIGHEST = lax.Precision.HIGHEST

D_MODEL = 1024
GRID_W = 64
EPS = 1e-6
CONV_CH = 512
CONV_WIDTH = 31
CONV_HALO = 16
NA_HEADS = 8
NA_HEAD_DIM = 64
NA_KR = 8
NA_KC = 16
NA_ROWS_PER_BLOCK = 4
NA_BLOCKS_PER_STEP = 2
NA_WIN_ROWS = 12
POOL_CH = 512
POOL_WINDOWS = (2, 4, 8, 16)
POOL_GROUP = 128
POOL_HALO = 8
GLA_HEADS = 4
GLA_DK = 64
GLA_DV = 128
GLA_RANK = 16
GLA_TAU = 16.0
GLA_CHUNK = 64
ROPE_BASE = 10000.0
N_EXPERTS = 16
EXPERT_FF = 2816
EC_CAPACITY_FACTOR = 2
LANES = 128
SUBLANES = 8
NEG_BIG = -1e30
VMEM_LIMIT = 56 * 1024 * 1024
SC_CORES = 2
SC_SUBCORES = 16
SC_TILE_VMEM_BUDGET = 400 * 1024
SC_GATHER_WINDOW = 128
COMBINE_TILE = 256
COMBINE_WINDOW = 256
COMBINE_FIRST = 64
BF16_ROWS = 16
COMBINE_COLS = 512


def _cparams(sem):
    return pltpu.CompilerParams(dimension_semantics=sem, vmem_limit_bytes=VMEM_LIMIT)


def _rms(x, g):
    return x * lax.rsqrt(jnp.mean(x * x, axis=-1, keepdims=True) + EPS) * g


def _sigmoid(x):
    return 1.0 / (1.0 + jnp.exp(-x))


def _silu(x):
    return x * _sigmoid(x)


def _dot(a, b):
    return jnp.dot(a, b, preferred_element_type=F32)


def _pack_bf16_pairs(h):
    half = h.shape[-1] // 2
    bits = lax.bitcast_convert_type(h.astype(BF16).astype(F32), jnp.uint32)
    packed = (bits[:, half:] & jnp.uint32(0xFFFF0000)) | (bits[:, :half] >> 16)
    return lax.bitcast_convert_type(packed, jnp.int32)


def _unpack_bf16_pairs(p):
    bits = lax.bitcast_convert_type(p, jnp.uint32)
    lo = lax.bitcast_convert_type(bits << 16, F32).astype(BF16)
    hi = lax.bitcast_convert_type(bits & jnp.uint32(0xFFFF0000), F32).astype(BF16)
    return lo, hi


def _dot_nt(a, b):
    return lax.dot_general(a, b, (((1,), (1,)), ((), ())), preferred_element_type=F32)


def _mod_body(c_ref, w_ref, b_ref, o_ref):
    o_ref[0] = jnp.dot(_silu(c_ref[...]), w_ref[0], precision=HIGHEST,
                       preferred_element_type=F32) + b_ref[0]


def _modulation(rows, w, b):
    depth, _, n = w.shape
    tn = 1536
    return pl.pallas_call(
        _mod_body,
        grid=(depth, n // tn),
        in_specs=[pl.BlockSpec((8, D_MODEL), lambda i, j: (0, 0)),
                  pl.BlockSpec((1, D_MODEL, tn), lambda i, j: (i, 0, j)),
                  pl.BlockSpec((1, 1, tn), lambda i, j: (i, 0, j))],
        out_specs=pl.BlockSpec((1, 8, tn), lambda i, j: (i, 0, j)),
        out_shape=jax.ShapeDtypeStruct((depth, 8, n), F32),
        compiler_params=_cparams(("parallel", "parallel")),
        name="modulation",
    )(rows, w, b.reshape(depth, 1, n))


def _even_in_body(x_ref, g_ref, sh_ref, sc_ref, w_ref, glu_ref, qkv_ref):
    h = (_rms(x_ref[0], g_ref[...]) * (1.0 + sc_ref[0]) + sh_ref[0]).astype(BF16)
    c = CONV_CH
    glu_ref[0] = _dot(h, w_ref[:, 0:c]) * _sigmoid(_dot(h, w_ref[:, c:2 * c]))
    hd = NA_HEADS * NA_HEAD_DIM
    q0 = 2 * c
    qkv_ref[0, :, 0:hd] = (_dot(h, w_ref[:, q0:q0 + hd]) * (NA_HEAD_DIM ** -0.5)).astype(BF16)
    qkv_ref[0, :, hd:3 * hd] = _dot(h, w_ref[:, q0 + hd:q0 + 3 * hd]).astype(BF16)


def _even_in(x, g, shift, scale, w_bf, tm):
    b, l, d = x.shape
    n = w_bf.shape[1]
    hd3 = 3 * NA_HEADS * NA_HEAD_DIM
    vec = pl.BlockSpec((1, 1, d), lambda i, j: (i, 0, 0))
    return pl.pallas_call(
        _even_in_body,
        grid=(b, l // tm),
        in_specs=[pl.BlockSpec((1, tm, d), lambda i, j: (i, j, 0)),
                  pl.BlockSpec((1, d), lambda i, j: (0, 0)),
                  vec, vec,
                  pl.BlockSpec((d, n), lambda i, j: (0, 0))],
        out_specs=[pl.BlockSpec((1, tm, CONV_CH), lambda i, j: (i, j, 0)),
                   pl.BlockSpec((1, tm, hd3), lambda i, j: (i, j, 0))],
        out_shape=[jax.ShapeDtypeStruct((b, l, CONV_CH), F32),
                   jax.ShapeDtypeStruct((b, l, hd3), BF16)],
        compiler_params=_cparams(("parallel", "parallel")),
        name="even_in",
    )(x, g.reshape(1, d), shift.reshape(b, 1, d), scale.reshape(b, 1, d), w_bf)


def _conv_body(cur_ref, prev_ref, next_ref, w_ref, b_ref, lg_ref, lb_ref, o_ref, buf_ref, sh_ref, *, tile, chunk):
    j = pl.program_id(1)
    last = pl.num_programs(1) - 1
    hal = CONV_HALO
    buf_ref[0:hal, :] = jnp.where(j == 0, 0.0, prev_ref[0])
    buf_ref[hal:hal + tile, :] = cur_ref[0]
    buf_ref[hal + tile:hal + tile + hal, :] = jnp.where(j == last, 0.0, next_ref[0])
    span = sh_ref.shape[1]
    for s in range(SUBLANES):
        sh_ref[s] = buf_ref[s:s + span, :]
    first = hal - CONV_WIDTH // 2
    reps = chunk // SUBLANES

    def rows(c, carry):
        r0 = pl.multiple_of(c * chunk, chunk)
        accs = [jnp.zeros((chunk, CONV_CH), F32) for _ in range(2)]
        for k in range(CONV_WIDTH):
            a, s = divmod(first + k, SUBLANES)
            wk = jnp.concatenate([w_ref[k]] * reps, axis=0)
            accs[k % 2] = accs[k % 2] + sh_ref[s, pl.ds(r0 + a * SUBLANES, chunk), :] * wk
        o_ref[0, pl.ds(r0, chunk), :] = accs[0] + accs[1]
        return carry

    lax.fori_loop(0, tile // chunk, rows, 0)
    y = o_ref[0] + b_ref[...]
    mu = jnp.mean(y, axis=-1, keepdims=True)
    yc = y - mu
    var = jnp.mean(yc * yc, axis=-1, keepdims=True)
    o_ref[0] = _silu(yc * lax.rsqrt(var + EPS) * lg_ref[...] + lb_ref[...])


def _conv_branch(glu, conv_w, conv_b, ln_g, ln_b, tile):
    b, l, c = glu.shape
    hal = CONV_HALO
    per = tile // hal
    nh = l // hal
    vec = pl.BlockSpec((1, c), lambda i, j: (0, 0))
    return pl.pallas_call(
        functools.partial(_conv_body, tile=tile, chunk=32),
        grid=(b, l // tile),
        in_specs=[pl.BlockSpec((1, tile, c), lambda i, j: (i, j, 0)),
                  pl.BlockSpec((1, hal, c), lambda i, j: (i, jnp.maximum(j * per - 1, 0), 0)),
                  pl.BlockSpec((1, hal, c), lambda i, j: (i, jnp.minimum((j + 1) * per, nh - 1), 0)),
                  pl.BlockSpec((CONV_WIDTH, SUBLANES, c), lambda i, j: (0, 0, 0)),
                  vec, vec, vec],
        out_specs=pl.BlockSpec((1, tile, c), lambda i, j: (i, j, 0)),
        out_shape=jax.ShapeDtypeStruct((b, l, c), F32),
        scratch_shapes=[pltpu.VMEM((tile + 2 * hal, c), F32),
                        pltpu.VMEM((SUBLANES, tile + 2 * hal - SUBLANES, c), F32)],
        compiler_params=_cparams(("parallel", "parallel")),
        name="conv_branch",
    )(glu, glu, glu, jnp.broadcast_to(conv_w[:, None, :], (CONV_WIDTH, SUBLANES, c)),
      conv_b.reshape(1, c), ln_g.reshape(1, c), ln_b.reshape(1, c))


def _na_window_start(j, rows):
    rb = NA_ROWS_PER_BLOCK
    return jnp.clip(j * rb - NA_KR // 2, 0, rows - NA_WIN_ROWS)


def _na_body(q_ref, k_ref, v_ref, kc_ref, vc_ref, *rest, rows):
    tab_refs, o_ref = rest[:-1], rest[-1]
    j = pl.program_id(2)
    nkeys = NA_WIN_ROWS * GRID_W
    tq = NA_ROWS_PER_BLOCK * GRID_W
    kc = kc_ref[0]
    vc = vc_ref[0]
    lane = lax.broadcasted_iota(jnp.int32, (1, LANES), 1)
    for sb, tab_ref in enumerate(tab_refs):
        start = pl.multiple_of(_na_window_start(j * len(tab_refs) + sb, rows) * GRID_W, GRID_W)
        q = q_ref[0, sb * tq:(sb + 1) * tq, :]
        kw = k_ref[0, pl.ds(start, nkeys), :]
        vw = v_ref[0, pl.ds(start, nkeys), :]
        out = jnp.zeros(q.shape, F32)
        for hh in range(LANES // NA_HEAD_DIM):
            in_head = (lane >= hh * NA_HEAD_DIM) & (lane < (hh + 1) * NA_HEAD_DIM)
            qh = jnp.where(in_head, q, jnp.zeros_like(q))
            s = _dot_nt(qh, kw) + tab_ref[0, hh]
            sc = _dot_nt(qh, kc)
            m = jnp.maximum(jnp.max(s, axis=-1, keepdims=True), jnp.max(sc, axis=-1, keepdims=True))
            p = jnp.exp(s - m)
            pc = jnp.exp(sc - m)
            denom = jnp.sum(p, axis=-1, keepdims=True) + jnp.sum(pc, axis=-1, keepdims=True)
            o = (_dot(p.astype(BF16), vw) + _dot(pc.astype(BF16), vc)) / denom
            out = jnp.where(in_head, o, out)
        o_ref[0, sb * tq:(sb + 1) * tq, :] = out


def _na_tables(rpb, rows):
    rb = NA_ROWS_PER_BLOCK
    nblk = rows // rb
    wr = NA_WIN_ROWS
    qc = jnp.arange(GRID_W)
    cs = jnp.clip(qc - NA_KC // 2, 0, GRID_W - NA_KC)
    col_ok = (qc[None, :] >= cs[:, None]) & (qc[None, :] < cs[:, None] + NA_KC)
    col_off = qc[None, :] - qc[:, None] + NA_KC - 1
    onehot = (col_off[:, :, None] == jnp.arange(2 * NA_KC - 1)[None, None, :]).astype(F32)
    blocks = jnp.einsum('hrd,qkd->hrqk', rpb.astype(F32), onehot, precision=HIGHEST)
    blocks = jnp.where(col_ok[None, None], blocks, NEG_BIG)
    masked = jnp.full((NA_HEADS, GRID_W, GRID_W), NEG_BIG, F32)
    tabs = []
    for jb in (0, 1, nblk - 1):
        ws = min(max(jb * rb - NA_KR // 2, 0), rows - wr)
        q_rows = []
        for qr in range(jb * rb, (jb + 1) * rb):
            rs = min(max(qr - NA_KR // 2, 0), rows - NA_KR)
            row = [blocks[:, kr - qr + NA_KR - 1] if rs <= kr < rs + NA_KR else masked
                   for kr in range(ws, ws + wr)]
            q_rows.append(jnp.concatenate(row, axis=-1))
        tabs.append(jnp.concatenate(q_rows, axis=1))
    return jnp.stack(tabs)


def _neighbourhood_attention(qkv, qkv_ctx, rpb):
    b, l, _ = qkv.shape
    n_ctx = qkv_ctx.shape[1]
    rows = l // GRID_W
    rb = NA_ROWS_PER_BLOCK
    nblk = rows // rb
    tq = rb * GRID_W
    nkeys = NA_WIN_ROWS * GRID_W
    hp = NA_HEADS * NA_HEAD_DIM // LANES
    tabs = _na_tables(rpb, rows)

    per = NA_BLOCKS_PER_STEP

    def cls(jb):
        return jnp.where(jb == 0, 0, jnp.where(jb == nblk - 1, 2, 1))

    def tab_spec(sb):
        return pl.BlockSpec((1, 2, tq, nkeys), lambda i, h, j: (cls(j * per + sb), h, 0, 0))

    return pl.pallas_call(
        functools.partial(_na_body, rows=rows),
        grid=(b, hp, nblk // per),
        in_specs=[pl.BlockSpec((1, per * tq, LANES), lambda i, h, j: (i, j, h)),
                  pl.BlockSpec((1, l, LANES), lambda i, h, j: (i, 0, hp + h)),
                  pl.BlockSpec((1, l, LANES), lambda i, h, j: (i, 0, 2 * hp + h)),
                  pl.BlockSpec((1, n_ctx, LANES), lambda i, h, j: (i, 0, hp + h)),
                  pl.BlockSpec((1, n_ctx, LANES), lambda i, h, j: (i, 0, 2 * hp + h))]
                 + [tab_spec(sb) for sb in range(per)],
        out_specs=pl.BlockSpec((1, per * tq, LANES), lambda i, h, j: (i, j, h)),
        out_shape=jax.ShapeDtypeStruct((b, l, NA_HEADS * NA_HEAD_DIM), F32),
        compiler_params=_cparams(("parallel", "parallel", "arbitrary")),
        name="neighbourhood_attention",
    )(qkv, qkv, qkv, qkv_ctx, qkv_ctx, *([tabs] * per))


def _ctx_attn_body(q_ref, k_ref, v_ref, o_ref):
    q = q_ref[0]
    k = k_ref[0]
    v = v_ref[0]
    lane = lax.broadcasted_iota(jnp.int32, (1, LANES), 1)
    out = jnp.zeros(q.shape, F32)
    for hh in range(LANES // NA_HEAD_DIM):
        in_head = (lane >= hh * NA_HEAD_DIM) & (lane < (hh + 1) * NA_HEAD_DIM)
        qh = jnp.where(in_head, q, jnp.zeros_like(q))
        s = _dot_nt(qh, k)
        p = jnp.exp(s - jnp.max(s, axis=-1, keepdims=True))
        o = _dot(p.astype(BF16), v) / jnp.sum(p, axis=-1, keepdims=True)
        out = jnp.where(in_head, o, out)
    o_ref[0] = out


def _context_attention(qkv_ctx):
    b, n, _ = qkv_ctx.shape
    hp = NA_HEADS * NA_HEAD_DIM // LANES
    return pl.pallas_call(
        _ctx_attn_body,
        grid=(b, hp),
        in_specs=[pl.BlockSpec((1, n, LANES), lambda i, h: (i, 0, h)),
                  pl.BlockSpec((1, n, LANES), lambda i, h: (i, 0, hp + h)),
                  pl.BlockSpec((1, n, LANES), lambda i, h: (i, 0, 2 * hp + h))],
        out_specs=pl.BlockSpec((1, n, LANES), lambda i, h: (i, 0, h)),
        out_shape=jax.ShapeDtypeStruct((b, n, NA_HEADS * NA_HEAD_DIM), F32),
        compiler_params=_cparams(("parallel", "parallel")),
        name="context_attention",
    )(qkv_ctx, qkv_ctx, qkv_ctx)


def _out_body(a_ref, b_ref, x_ref, w_ref, g1_ref, gate_ref, g2_ref, sh_ref, sc_ref, rwh_ref, rwl_ref,
              xo_ref, h_ref, aff_ref):
    half = a_ref.shape[-1]
    y = _dot(a_ref[0].astype(BF16), w_ref[0:half, :]) + _dot(b_ref[0].astype(BF16), w_ref[half:2 * half, :])
    xn = x_ref[0] + gate_ref[0] * _rms(y, g1_ref[...])
    xo_ref[0] = xn
    h = _rms(xn, g2_ref[...]) * (1.0 + sc_ref[0]) + sh_ref[0]
    packed = _pack_bf16_pairs(h)
    quarter = packed.shape[-1] // 2
    h_ref[0, 0] = packed[:, 0:quarter]
    h_ref[0, 1] = packed[:, quarter:2 * quarter]
    h_hi = h.astype(BF16)
    h_lo = (h - h_hi.astype(F32)).astype(BF16)
    logits = _dot(h_hi, rwh_ref[...]) + (_dot(h_lo, rwh_ref[...]) + _dot(h_hi, rwl_ref[...]))
    lane = lax.broadcasted_iota(jnp.int32, (1, LANES), 1)
    logits = jnp.where(lane < N_EXPERTS, logits, NEG_BIG)
    e = jnp.exp(logits - jnp.max(logits, axis=-1, keepdims=True))
    aff = e / jnp.sum(e, axis=-1, keepdims=True)
    aff_ref[0] = aff.T[0:N_EXPERTS, :]


def _out_proj(a, b2, x, w_bf, g1, gate, g2, shift, scale, router_w, tm):
    b, l, d = x.shape
    half = a.shape[-1]
    rw = jnp.pad(router_w, ((0, 0), (0, LANES - N_EXPERTS)))
    rw_hi = rw.astype(BF16)
    rw_lo = (rw - rw_hi.astype(F32)).astype(BF16)
    vec = pl.BlockSpec((1, d), lambda i, j: (0, 0))
    bvec = pl.BlockSpec((1, 1, d), lambda i, j: (i, 0, 0))
    rspec = pl.BlockSpec((d, LANES), lambda i, j: (0, 0))
    return pl.pallas_call(
        _out_body,
        grid=(b, l // tm),
        in_specs=[pl.BlockSpec((1, tm, half), lambda i, j: (i, j, 0)),
                  pl.BlockSpec((1, tm, half), lambda i, j: (i, j, 0)),
                  pl.BlockSpec((1, tm, d), lambda i, j: (i, j, 0)),
                  pl.BlockSpec((2 * half, d), lambda i, j: (0, 0)),
                  vec, bvec, vec, bvec, bvec, rspec, rspec],
        out_specs=[pl.BlockSpec((1, tm, d), lambda i, j: (i, j, 0)),
                   pl.BlockSpec((1, 2, tm, d // 4), lambda i, j: (i, 0, j, 0)),
                   pl.BlockSpec((1, N_EXPERTS, tm), lambda i, j: (i, 0, j))],
        out_shape=[jax.ShapeDtypeStruct((b, l, d), F32),
                   jax.ShapeDtypeStruct((b, 2, l, d // 4), jnp.int32),
                   jax.ShapeDtypeStruct((b, N_EXPERTS, l), F32)],
        compiler_params=_cparams(("parallel", "parallel")),
        name="out_proj",
    )(a, b2, x, w_bf, g1.reshape(1, d), gate.reshape(b, 1, d), g2.reshape(1, d),
      shift.reshape(b, 1, d), scale.reshape(b, 1, d), rw_hi, rw_lo)


def _moe_body(x_ref, val_ref, wg_ref, wu_ref, wd_ref, o_ref, acc_s, x_s, wg_s, wu_s, wd_s, *, chunk):
    f = pl.program_id(1)
    m = x_ref.shape[2]
    quarter = x_ref.shape[3]

    @pl.when(f == 0)
    def _():
        def unpack(c, carry):
            r = pl.multiple_of(c * chunk, chunk)
            for s in range(2):
                lo, hi = _unpack_bf16_pairs(x_ref[0, s, pl.ds(r, chunk), :])
                x_s[pl.ds(r, chunk), s * quarter:(s + 1) * quarter] = lo
                x_s[pl.ds(r, chunk), (2 + s) * quarter:(3 + s) * quarter] = hi
            acc_s[pl.ds(r, chunk), :] = jnp.zeros((chunk, 4 * quarter), F32)
            return carry

        lax.fori_loop(0, m // chunk, unpack, 0)

    wg_s[...] = wg_ref[0, 0].astype(BF16)
    wu_s[...] = wu_ref[0, 0].astype(BF16)
    wd_s[...] = wd_ref[0, 0].astype(BF16)

    def rows(c, carry):
        r = pl.multiple_of(c * chunk, chunk)
        xs = x_s[pl.ds(r, chunk), :]
        hid = (_silu(_dot(xs, wg_s[...])) * _dot(xs, wu_s[...])).astype(BF16)
        acc_s[pl.ds(r, chunk), :] += _dot(hid, wd_s[...])
        return carry

    lax.fori_loop(0, m // chunk, rows, 0, unroll=True)

    @pl.when(f == pl.num_programs(1) - 1)
    def _():
        o_ref[0] = (acc_s[...] * val_ref[0]).astype(o_ref.dtype)


def _expert_ffn(xg, vals, w_gate, w_up, w_down, layer, chunk):
    e, _, m, quarter = xg.shape
    d = 4 * quarter
    ff = w_gate.shape[-1]
    tf = 256
    return pl.pallas_call(
        functools.partial(_moe_body, chunk=chunk),
        grid=(e, ff // tf),
        in_specs=[pl.BlockSpec((1, 2, m, quarter), lambda i, f: (i, 0, 0, 0)),
                  pl.BlockSpec((1, m, 1), lambda i, f: (i, 0, 0)),
                  pl.BlockSpec((1, 1, d, tf), lambda i, f: (layer, i, 0, f)),
                  pl.BlockSpec((1, 1, d, tf), lambda i, f: (layer, i, 0, f)),
                  pl.BlockSpec((1, 1, tf, d), lambda i, f: (layer, i, f, 0))],
        out_specs=pl.BlockSpec((1, m, d), lambda i, f: (i, 0, 0)),
        out_shape=jax.ShapeDtypeStruct((e, m, d), BF16),
        scratch_shapes=[pltpu.VMEM((m, d), F32), pltpu.VMEM((m, d), BF16), pltpu.VMEM((d, tf), BF16),
                        pltpu.VMEM((d, tf), BF16), pltpu.VMEM((tf, d), BF16)],
        compiler_params=_cparams(("parallel", "arbitrary")),
        name="expert_ffn",
    )(xg, vals, w_gate, w_up, w_down)


def _resid_body(x_ref, f_ref, gate_ref, g_ref, o_ref):
    o_ref[0] = x_ref[0] + gate_ref[0] * _rms(f_ref[0], g_ref[...])


def _gated_residual(x, f, gate, g, tm):
    b, l, d = x.shape
    blk = pl.BlockSpec((1, tm, d), lambda i, j: (i, j, 0))
    return pl.pallas_call(
        _resid_body,
        grid=(b, l // tm),
        in_specs=[blk, blk, pl.BlockSpec((1, 1, d), lambda i, j: (i, 0, 0)),
                  pl.BlockSpec((1, d), lambda i, j: (0, 0))],
        out_specs=blk,
        out_shape=jax.ShapeDtypeStruct((b, l, d), F32),
        compiler_params=_cparams(("parallel", "parallel")),
        name="gated_residual",
    )(x, f, gate.reshape(b, 1, d), g.reshape(1, d))


def _swap_pairs(x):
    nf = GLA_DK // 4
    lane = lax.broadcasted_iota(jnp.int32, (1, LANES), 1)
    up = pltpu.roll(x, LANES - nf, 1)
    down = pltpu.roll(x, nf, 1)
    return jnp.where(lane % (2 * nf) < nf, up, down)


def _odd_in_body(x_ref, g_ref, sh_ref, sc_ref, w_ref, cos_ref, sin_ref, gw_ref, gb_ref,
                 pool_ref, qk_ref, v_ref, r_ref, gate_ref):
    h = (_rms(x_ref[0], g_ref[...]) * (1.0 + sc_ref[0]) + sh_ref[0]).astype(BF16)
    qk = GLA_HEADS * GLA_DK
    vd = GLA_HEADS * GLA_DV
    q0 = POOL_CH
    v0 = q0 + 2 * qk
    r0 = v0 + vd
    l0 = r0 + vd
    pool_ref[0] = _dot(h, w_ref[:, 0:q0])
    for s in range(2 * qk // LANES):
        raw = _dot(h, w_ref[:, q0 + s * LANES:q0 + (s + 1) * LANES])
        c = cos_ref[:, (s * LANES) % qk:(s * LANES) % qk + LANES]
        sn = sin_ref[:, (s * LANES) % qk:(s * LANES) % qk + LANES]
        rot = raw * c + _swap_pairs(raw) * sn
        if s * LANES < qk:
            rot = rot * (GLA_DK ** -0.5)
        qk_ref[0, :, s * LANES:(s + 1) * LANES] = rot
    v_ref[0] = _dot(h, w_ref[:, v0:r0]).astype(BF16)
    r_ref[0] = _dot(h, w_ref[:, r0:l0])
    lr = _dot(h, w_ref[:, l0:l0 + 2 * GLA_RANK])
    z = jnp.dot(lr, gw_ref[...], precision=HIGHEST, preferred_element_type=F32) + gb_ref[...]
    gate_ref[0] = (jnp.minimum(z, 0.0) - jnp.log1p(jnp.exp(-jnp.abs(z)))) * (1.0 / GLA_TAU)


def _odd_in(x, g, shift, scale, w_bf, cos_t, sin_t, gate_w, gate_b, tm):
    b, l, d = x.shape
    n = w_bf.shape[1]
    qk = GLA_HEADS * GLA_DK
    vd = GLA_HEADS * GLA_DV
    gw = jnp.zeros((2 * GLA_RANK, 2 * qk), F32)
    gw = gw.at[:GLA_RANK, :qk].set(gate_w[0]).at[GLA_RANK:, qk:].set(gate_w[1])
    gb = jnp.concatenate([gate_b[0], gate_b[1]]).reshape(1, 2 * qk)
    vec = pl.BlockSpec((1, 1, d), lambda i, j: (i, 0, 0))
    row = lambda w: pl.BlockSpec((1, tm, w), lambda i, j: (i, j, 0))
    return pl.pallas_call(
        _odd_in_body,
        grid=(b, l // tm),
        in_specs=[row(d), pl.BlockSpec((1, d), lambda i, j: (0, 0)), vec, vec,
                  pl.BlockSpec((d, n), lambda i, j: (0, 0)),
                  pl.BlockSpec((tm, qk), lambda i, j: (j, 0)),
                  pl.BlockSpec((tm, qk), lambda i, j: (j, 0)),
                  pl.BlockSpec((2 * GLA_RANK, 2 * qk), lambda i, j: (0, 0)),
                  pl.BlockSpec((1, 2 * qk), lambda i, j: (0, 0))],
        out_specs=[row(POOL_CH), row(2 * qk), row(vd), row(vd), row(2 * qk)],
        out_shape=[jax.ShapeDtypeStruct((b, l, POOL_CH), F32),
                   jax.ShapeDtypeStruct((b, l, 2 * qk), F32),
                   jax.ShapeDtypeStruct((b, l, vd), BF16),
                   jax.ShapeDtypeStruct((b, l, vd), F32),
                   jax.ShapeDtypeStruct((b, l, 2 * qk), F32)],
        compiler_params=_cparams(("parallel", "parallel")),
        name="odd_in",
    )(x, g.reshape(1, d), shift.reshape(b, 1, d), scale.reshape(b, 1, d), w_bf, cos_t, sin_t, gw, gb)


def _rope_tables(l):
    t = jnp.arange(l)
    pos_r = (t // GRID_W).astype(F32)
    pos_c = (t % GRID_W).astype(F32)
    nf = GLA_DK // 4
    inv = jnp.power(ROPE_BASE, -jnp.arange(nf, dtype=F32) / nf)
    ar = pos_r[:, None] * inv[None, :]
    ac = pos_c[:, None] * inv[None, :]
    cos_h = jnp.concatenate([jnp.cos(ar), jnp.cos(ar), jnp.cos(ac), jnp.cos(ac)], axis=-1)
    sin_h = jnp.concatenate([-jnp.sin(ar), jnp.sin(ar), -jnp.sin(ac), jnp.sin(ac)], axis=-1)
    return jnp.tile(cos_h, (1, GLA_HEADS)), jnp.tile(sin_h, (1, GLA_HEADS))


def _gla_tile(qk, v, g, s, reverse):
    hk = GLA_HEADS * GLA_DK
    hv = GLA_HEADS * GLA_DV
    c = GLA_CHUNK
    t = qk.shape[0]
    n = t // c
    last_row, mid_row = (0, c // 2) if reverse else (c - 1, c // 2 - 1)
    ii = lax.broadcasted_iota(jnp.int32, (t, t), 0)
    jj = lax.broadcasted_iota(jnp.int32, (t, t), 1)
    ordered = (jj >= ii) if reverse else (jj <= ii)
    tri = jnp.where(ordered & (ii // c == jj // c), 1.0, 0.0).astype(BF16)
    g_hi = g.astype(BF16)
    rem = g - g_hi.astype(F32)
    g_mid = rem.astype(BF16)
    g_lo = (rem - g_mid.astype(F32)).astype(BF16)
    bc = _dot(tri, g_hi) + (_dot(tri, g_mid) + _dot(tri, g_lo))
    spread = lambda row: jnp.concatenate(
        [jnp.broadcast_to(bc[i * c + row:i * c + row + 1, :], (c, hk)) for i in range(n)], axis=0)
    b_mid = spread(mid_row)
    b_last = spread(last_row)
    qt = qk[:, 0:hk] * jnp.exp(bc - b_mid)
    kt = qk[:, hk:2 * hk] * jnp.exp(b_mid - bc)
    qe = (qt * jnp.exp(b_mid)).astype(BF16)
    ke = kt * jnp.exp(b_last - b_mid)
    ktb = kt.astype(BF16)
    lane = lax.broadcasted_iota(jnp.int32, (1, hk), 1)
    ci = lax.broadcasted_iota(jnp.int32, (c, c), 0)
    cj = lax.broadcasted_iota(jnp.int32, (c, c), 1)
    causal = (cj >= ci) if reverse else (cj <= ci)
    blockdiag = (lax.broadcasted_iota(jnp.int32, (hk, hv), 0) // GLA_DK
                 == lax.broadcasted_iota(jnp.int32, (hk, hv), 1) // GLA_DV)
    intra, upd, decay = [], [], []
    for i in range(n):
        rows = slice(i * c, (i + 1) * c)
        qs = jnp.concatenate(
            [jnp.where((lane >= h * GLA_DK) & (lane < (h + 1) * GLA_DK), qt[rows], 0.0) for h in range(GLA_HEADS)],
            axis=0).astype(BF16)
        att = _dot_nt(qs, ktb[rows])
        intra.append(jnp.concatenate(
            [_dot(jnp.where(causal, att[h * c:(h + 1) * c], 0.0).astype(BF16),
                  v[rows, h * GLA_DV:(h + 1) * GLA_DV]) for h in range(GLA_HEADS)], axis=-1))
        upd.append(jnp.where(blockdiag, _dot(ke[rows].T.astype(BF16), v[rows]), 0.0))
        decay.append(jnp.exp(jnp.sum(g[rows].T, axis=1, keepdims=True)))
    outs = [None] * n
    for i in (reversed(range(n)) if reverse else range(n)):
        rows = slice(i * c, (i + 1) * c)
        outs[i] = _dot(qe[rows], s.astype(BF16)) + intra[i]
        s = decay[i] * s + upd[i]
    return jnp.concatenate(outs, axis=0), s


def _gla_body(qkf_ref, qkb_ref, vf_ref, vb_ref, gf_ref, gb_ref, s0f_ref, s0b_ref,
              of_ref, ob_ref, sff_ref, sbf_ref, sf_ref, sb_ref, *, tile):
    n = pl.program_id(1)
    hk = GLA_HEADS * GLA_DK
    hv = GLA_HEADS * GLA_DV

    @pl.when(n == 0)
    def _():
        sf_ref[...] = jnp.zeros((hk, hv), F32)
        sb_ref[...] = jnp.zeros((hk, hv), F32)
        for h in range(GLA_HEADS):
            sf_ref[h * GLA_DK:(h + 1) * GLA_DK, h * GLA_DV:(h + 1) * GLA_DV] = s0f_ref[0, h]
            sb_ref[h * GLA_DK:(h + 1) * GLA_DK, h * GLA_DV:(h + 1) * GLA_DV] = s0b_ref[0, h]

    of_ref[0], sf_ref[...] = _gla_tile(qkf_ref[0], vf_ref[0], gf_ref[0], sf_ref[...], False)
    ob_ref[0], sb_ref[...] = _gla_tile(qkb_ref[0], vb_ref[0], gb_ref[0], sb_ref[...], True)

    @pl.when(n == pl.num_programs(1) - 1)
    def _():
        for h in range(GLA_HEADS):
            sff_ref[0, h] = sf_ref[h * GLA_DK:(h + 1) * GLA_DK, h * GLA_DV:(h + 1) * GLA_DV]
            sbf_ref[0, h] = sb_ref[h * GLA_DK:(h + 1) * GLA_DK, h * GLA_DV:(h + 1) * GLA_DV]


def _gla(qk, v, gates, s0f, s0b, tile):
    b, l, _ = qk.shape
    hk = GLA_HEADS * GLA_DK
    hv = GLA_HEADS * GLA_DV
    nt = l // tile
    fwd = lambda w, col: pl.BlockSpec((1, tile, w), lambda i, n: (i, n, col))
    bwd = lambda w, col: pl.BlockSpec((1, tile, w), lambda i, n: (i, nt - 1 - n, col))
    st = pl.BlockSpec((1, GLA_HEADS, GLA_DK, GLA_DV), lambda i, n: (i, 0, 0, 0))
    return pl.pallas_call(
        functools.partial(_gla_body, tile=tile),
        grid=(b, nt),
        in_specs=[fwd(2 * hk, 0), bwd(2 * hk, 0), fwd(hv, 0), bwd(hv, 0), fwd(hk, 0), bwd(hk, 1), st, st],
        out_specs=[fwd(hv, 0), bwd(hv, 0), st, st],
        out_shape=[jax.ShapeDtypeStruct((b, l, hv), F32), jax.ShapeDtypeStruct((b, l, hv), F32),
                   jax.ShapeDtypeStruct((b, GLA_HEADS, GLA_DK, GLA_DV), F32),
                   jax.ShapeDtypeStruct((b, GLA_HEADS, GLA_DK, GLA_DV), F32)],
        scratch_shapes=[pltpu.VMEM((hk, hv), F32), pltpu.VMEM((hk, hv), F32)],
        compiler_params=_cparams(("parallel", "arbitrary")),
        name="gla_scan",
    )(qk, qk, v, v, gates, gates, s0f, s0b)


def _odd_mid_body(cur_ref, prev_ref, next_ref, of_ref, ob_ref, r_ref, hg_ref, pw_ref, ps_ref,
                  pool_ref, d_ref, buf_ref, *, tile, seq):
    j = pl.program_id(1)
    last = pl.num_programs(1) - 1
    hal = POOL_HALO
    buf_ref[0:hal, :] = jnp.where(j == 0, 0.0, prev_ref[0])
    buf_ref[hal:hal + tile, :] = cur_ref[0]
    buf_ref[hal + tile:hal + tile + hal, :] = jnp.where(j == last, 0.0, next_ref[0])
    t = j * tile + lax.broadcasted_iota(jnp.int32, (tile, 1), 0)
    for gi, win in enumerate(POOL_WINDOWS):
        cols = slice(gi * POOL_GROUP, (gi + 1) * POOL_GROUP)
        acc = jnp.zeros((tile, POOL_GROUP), F32)
        for off in range(-(win // 2), win - win // 2):
            acc = acc + buf_ref[hal + off:hal + off + tile, cols]
        cnt = jnp.minimum(t + (win - win // 2), seq) - jnp.maximum(t - win // 2, 0)
        diff = acc / cnt.astype(F32) - cur_ref[0, :, cols]
        pool_ref[0, :, cols] = _dot(diff.astype(BF16), pw_ref[gi]) * ps_ref[:, cols]
    for h in range(GLA_HEADS):
        cols = slice(h * GLA_DV, (h + 1) * GLA_DV)
        o = of_ref[0, :, cols] + ob_ref[0, :, cols]
        d_ref[0, :, cols] = _rms(o, hg_ref[:, cols]) * _silu(r_ref[0, :, cols])


def _odd_mid(pool_u, o_f, o_b, r, head_g, pool_w_bf, pool_scale, tile):
    b, l, c = pool_u.shape
    hal = POOL_HALO
    per = tile // hal
    nh = l // hal
    blk = pl.BlockSpec((1, tile, c), lambda i, j: (i, j, 0))
    vec = pl.BlockSpec((1, c), lambda i, j: (0, 0))
    return pl.pallas_call(
        functools.partial(_odd_mid_body, tile=tile, seq=l),
        grid=(b, l // tile),
        in_specs=[blk,
                  pl.BlockSpec((1, hal, c), lambda i, j: (i, jnp.maximum(j * per - 1, 0), 0)),
                  pl.BlockSpec((1, hal, c), lambda i, j: (i, jnp.minimum((j + 1) * per, nh - 1), 0)),
                  blk, blk, blk, vec,
                  pl.BlockSpec((len(POOL_WINDOWS), POOL_GROUP, POOL_GROUP), lambda i, j: (0, 0, 0)),
                  vec],
        out_specs=[blk, blk],
        out_shape=[jax.ShapeDtypeStruct((b, l, c), F32), jax.ShapeDtypeStruct((b, l, c), F32)],
        scratch_shapes=[pltpu.VMEM((tile + 2 * hal, c), F32)],
        compiler_params=_cparams(("parallel", "parallel")),
        name="odd_mid",
    )(pool_u, pool_u, pool_u, o_f, o_b, r, head_g.reshape(1, c), pool_w_bf, pool_scale.reshape(1, c))


def _route(aff_t, tok_base, row_base):
    b, e, n = aff_t.shape
    cap = EC_CAPACITY_FACTOR * n // N_EXPERTS
    vals, idx = lax.top_k(aff_t, cap)
    idx, vals = lax.sort((idx, vals), dimension=2, num_keys=1)
    bi = jnp.arange(b, dtype=idx.dtype)[:, None, None]
    per_expert = lambda a: jnp.swapaxes(a, 0, 1).reshape(e, b * cap)
    rows0 = idx + row_base + 2 * bi * n
    return (per_expert(vals), per_expert(idx + tok_base + bi * n), per_expert(rows0), per_expert(rows0 + n),
            per_expert(idx))


def _combine_first_start(p0, cap):
    return jnp.minimum((p0 // BF16_ROWS) * BF16_ROWS, cap - COMBINE_FIRST)


def _combine_body(offs_ref, spill_ref, tok_ref, y_ref, o_ref, *, n_tok, cap):
    bi = pl.program_id(0)
    j = pl.program_id(2)
    tt = COMBINE_TILE
    wf = COMBINE_FIRST
    wn = COMBINE_WINDOW
    ntiles = n_tok // tt
    n_exp = y_ref.shape[0]
    group = wn // wf
    sub = lax.broadcasted_iota(jnp.int32, (tt, 1), 0)
    lane = lax.broadcasted_iota(jnp.int32, (1, wn), 1)

    def slot_range(e):
        base = (bi * n_exp + e) * (ntiles + 1) + j
        return offs_ref[base], offs_ref[base + 1]

    acc = jnp.zeros(o_ref.shape[1:], F32)
    for g in range(n_exp // group):
        ys = []
        toks = jnp.full((1, wn), -1, jnp.int32)
        for k in range(group):
            e = g * group + k
            start = pl.multiple_of(_combine_first_start(slot_range(e)[0], cap), BF16_ROWS)
            ys.append(y_ref[e, pl.ds(start, wf), :])
            cs = pl.multiple_of(jnp.minimum((start // LANES) * LANES, cap - wn), LANES)
            rolled = pltpu.roll(tok_ref[e, :, pl.ds(cs, wn)], (k * wf + wn - (start - cs)) % wn, 1)
            toks = jnp.where((lane >= k * wf) & (lane < (k + 1) * wf), rolled, toks)
        hit = (toks - j * tt) == sub
        acc = acc + _dot(jnp.where(hit, 1.0, 0.0).astype(BF16), jnp.concatenate(ys, axis=0))
    o_ref[0] = acc

    def more_windows(e, carry):
        p0, p1 = slot_range(e)
        lo = _combine_first_start(p0, cap) + wf
        first = (lo // LANES) * LANES

        def extra(w, carry):
            cs = pl.multiple_of(jnp.minimum(first + w * wn, cap - wn), LANES)
            tok = tok_ref[e, :, pl.ds(cs, wn)] - j * tt
            hit = (tok == sub) & (cs + lane >= jnp.maximum(lo, first + w * wn))
            o_ref[0] += _dot(jnp.where(hit, 1.0, 0.0).astype(BF16), y_ref[e, pl.ds(cs, wn), :])
            return carry

        lax.fori_loop(0, (jnp.maximum(p1 - first, 0) + wn - 1) // wn * (p1 > lo).astype(jnp.int32), extra, 0)
        return carry

    @pl.when(spill_ref[bi * ntiles + j] != 0)
    def _():
        lax.fori_loop(0, n_exp, more_windows, 0)


def _combine(y, tok, local, seg0, n_tok):
    e, _, d = y.shape
    _, b, cap = local.shape
    cols = COMBINE_COLS
    tt = COMBINE_TILE
    wf = COMBINE_FIRST
    ntiles = n_tok // tt
    assert seg0 % cap == 0 and n_tok % tt == 0 and cap % COMBINE_WINDOW == 0 and e % (COMBINE_WINDOW // wf) == 0
    seg = seg0 // cap
    bounds = jnp.arange(ntiles + 1, dtype=jnp.int32) * tt
    offs = jnp.sum((local[..., None] < bounds).astype(jnp.int32), axis=2)
    starts = _combine_first_start(offs[..., :-1], cap)
    spill = jnp.any(offs[..., 1:] > starts + wf, axis=0).astype(jnp.int32).reshape(-1)
    grid_spec = pltpu.PrefetchScalarGridSpec(
        num_scalar_prefetch=2,
        grid=(b, d // cols, ntiles),
        in_specs=[pl.BlockSpec((e, 1, cap), lambda i, q, j, offs, spill: (0, 0, seg + i)),
                  pl.BlockSpec((e, cap, cols), lambda i, q, j, offs, spill: (0, seg + i, q))],
        out_specs=pl.BlockSpec((1, tt, cols), lambda i, q, j, offs, spill: (i, j, q)),
    )
    return pl.pallas_call(
        functools.partial(_combine_body, n_tok=n_tok, cap=cap),
        grid_spec=grid_spec,
        out_shape=jax.ShapeDtypeStruct((b, n_tok, d), F32),
        compiler_params=_cparams(("parallel", "parallel", "parallel")),
        name="combine",
    )(jnp.swapaxes(offs, 0, 1).reshape(-1), spill, tok, y)


def _gather_rows(src, idx):
    window = SC_GATHER_WINDOW
    n = idx.shape[0]
    width = src.shape[1]
    assert 2 * window * width * 4 <= SC_TILE_VMEM_BUDGET, width
    assert n % (window * SC_CORES * SC_SUBCORES) == 0, n
    mesh = plsc.VectorSubcoreMesh(core_axis_name="core", subcore_axis_name="subcore",
                                  num_cores=SC_CORES, num_subcores=SC_SUBCORES)

    @pl.kernel(out_type=jax.ShapeDtypeStruct((n, width), src.dtype), mesh=mesh, scratch_types=[],
               name="gather_rows")
    def gather(src_hbm, idx_hbm, out_hbm):
        def body(idx_vmem, out_vmem):
            pltpu.sync_copy(src_hbm.at[idx_vmem.at[0]], out_vmem)

        pltpu.emit_pipeline(
            body,
            grid=(n // window,),
            in_specs=[pl.BlockSpec((1, window), lambda i: (0, i))],
            out_specs=[pl.BlockSpec((window, width), lambda i: (i, 0))],
            core_axis_name=("core", "subcore"),
            dimension_semantics=(pltpu.PARALLEL,),
        )(idx_hbm, out_hbm)

    return gather(src, idx.reshape(1, n))


def _moe(parts, w_gate, w_up, w_down, layer):
    quarter = parts[0][1].shape[-1]
    d = 4 * quarter
    sizes = [h.shape[0] * h.shape[2] for _, h in parts]
    bases = [sum(sizes[:i]) for i in range(len(parts))]
    routed = [_route(a, base, 2 * base) for (a, _), base in zip(parts, bases)]
    src = jnp.concatenate([h.reshape(-1, quarter) for _, h in parts], axis=0)
    vals, flat, rows0, rows1, tok = (jnp.concatenate([r[i] for r in routed], axis=1) for i in range(5))
    e, m = flat.shape
    pad = -m % LANES
    vals, rows0, rows1, tok = (jnp.pad(a, ((0, 0), (0, pad))) for a in (vals, rows0, rows1, tok))
    m += pad
    rows = jnp.stack([rows0, rows1], axis=1)
    xg = _gather_rows(src, rows.reshape(-1)).reshape(e, 2, m, quarter)
    chunk = next(c for c in (512, 544, 384, 256, 128) if m % c == 0)
    y = _expert_ffn(xg, vals[..., None], w_gate, w_up, w_down, layer, chunk)
    outs = []
    seg0 = 0
    for (aff_t, h), r, base in zip(parts, routed, bases):
        b, _, n = aff_t.shape
        cap = r[0].shape[1] // b
        if n % COMBINE_TILE == 0 and cap % COMBINE_WINDOW == 0 and seg0 % cap == 0:
            outs.append(_combine(y, tok[:, None, :], r[4].reshape(e, b, cap), seg0, n))
        else:
            ids = r[1] - base
            part = y[:, seg0:seg0 + b * cap].astype(F32)
            out = jnp.zeros((b * n, d), F32).at[ids.reshape(-1)].add(part.reshape(-1, d))
            outs.append(out.reshape(b, n, d))
        seg0 += b * cap
    return outs


def kernel(x, c, ctx, c_ctx, w_mod, b_mod, norm_g, w_in_even, w_out_even, conv_w, conv_b, conv_ln_g,
           conv_ln_b, na_rpb, w_in_odd, w_out_odd, pool_w, pool_scale, gla_gate_w, gla_gate_b, gla_head_g,
           router_w, expert_w_gate, expert_w_up, expert_w_down):
    b, l, d = x.shape
    n_ctx = ctx.shape[1]
    tm = 512

    mod_rows = jnp.concatenate([c, c_ctx[None], jnp.zeros((8 - b - 1, d), F32)], axis=0)

    mod_all = _modulation(mod_rows, w_mod, b_mod)

    def modulation(i):
        mm = mod_all[i]
        m = mm[:b].reshape(b, 6, d)
        mc = jnp.broadcast_to(mm[b].reshape(1, 6, d), (b, 6, d))
        return m, mc

    m, mc = modulation(0)
    g = norm_g[0]
    w_in = w_in_even[0].astype(BF16)
    w_out = w_out_even[0].astype(BF16)
    glu, qkv = _even_in(x, g[0], m[:, 0], m[:, 1], w_in, tm)
    glu_c, qkv_c = _even_in(ctx, g[0], mc[:, 0], mc[:, 1], w_in, n_ctx)
    a_lat = _conv_branch(glu, conv_w[0], conv_b[0], conv_ln_g[0], conv_ln_b[0], 256)
    a_ctx = _conv_branch(glu_c, conv_w[0], conv_b[0], conv_ln_g[0], conv_ln_b[0], n_ctx)
    na = _neighbourhood_attention(qkv, qkv_c, na_rpb[0])
    att_c = _context_attention(qkv_c)
    x, h2, aff = _out_proj(a_lat, na, x, w_out, g[1], m[:, 2], g[2], m[:, 3], m[:, 4], router_w[0], tm)
    ctx, h2c, aff_c = _out_proj(a_ctx, att_c, ctx, w_out, g[1], mc[:, 2], g[2], mc[:, 3], mc[:, 4],
                                router_w[0], n_ctx)
    f, f_c = _moe([(aff, h2), (aff_c, h2c)], expert_w_gate, expert_w_up, expert_w_down, 0)
    x = _gated_residual(x, f, m[:, 5], g[3], tm)
    ctx = _gated_residual(ctx, f_c, mc[:, 5], g[3], n_ctx)

    m, mc = modulation(1)
    g = norm_g[1]
    w_in = w_in_odd[0].astype(BF16)
    w_out = w_out_odd[0].astype(BF16)
    cos_t, sin_t = _rope_tables(l)
    ones_t = jnp.ones((n_ctx, GLA_HEADS * GLA_DK), F32)
    _, qk_c, v_c, _, gate_c = _odd_in(ctx, g[0], mc[:, 0], mc[:, 1], w_in, ones_t, jnp.zeros_like(ones_t),
                                      gla_gate_w[0], gla_gate_b[0], n_ctx)
    s_zero = jnp.zeros((b, GLA_HEADS, GLA_DK, GLA_DV), F32)
    _, _, s_f, s_b = _gla(qk_c, v_c, gate_c, s_zero, s_zero, n_ctx)
    pool_u, qk, v, r, gate = _odd_in(x, g[0], m[:, 0], m[:, 1], w_in, cos_t, sin_t,
                                     gla_gate_w[0], gla_gate_b[0], tm)
    o_f, o_b, _, _ = _gla(qk, v, gate, s_f, s_b, 256)
    pool_y, d_lat = _odd_mid(pool_u, o_f, o_b, r, gla_head_g[0], pool_w[0].astype(BF16), pool_scale[0], 256)
    x, h2, aff = _out_proj(pool_y, d_lat, x, w_out, g[1], m[:, 2], g[2], m[:, 3], m[:, 4], router_w[1], tm)
    (f,) = _moe([(aff, h2)], expert_w_gate, expert_w_up, expert_w_down, 1)
    return _gated_residual(x, f, m[:, 5], g[3], tm)
```

```python
import functools

import jax
import jax.numpy as jnp
from jax import lax
from jax.experimental import pallas as pl
from jax.experimental.pallas import tpu as pltpu
from jax.experimental.pallas import tpu_sc as plsc

F32 = jnp.float32
BF16 = jnp.bfloat16
HIGHEST = lax.Precision.HIGHEST

D_MODEL = 1024
GRID_W = 64
EPS = 1e-6
CONV_CH = 512
CONV_WIDTH = 31
CONV_HALO = 16
NA_HEADS = 8
NA_HEAD_DIM = 64
NA_KR = 8
NA_KC = 16
NA_ROWS_PER_BLOCK = 4
NA_BLOCKS_PER_STEP = 2
NA_WIN_ROWS = 12
POOL_CH = 512
POOL_WINDOWS = (2, 4, 8, 16)
POOL_GROUP = 128
POOL_HALO = 8
GLA_HEADS = 4
GLA_DK = 64
GLA_DV = 128
GLA_RANK = 16
GLA_TAU = 16.0
GLA_CHUNK = 64
ROPE_BASE = 10000.0
N_EXPERTS = 16
EXPERT_FF = 2816
EC_CAPACITY_FACTOR = 2
LANES = 128
SUBLANES = 8
NEG_BIG = -1e30
VMEM_LIMIT = 56 * 1024 * 1024
SC_CORES = 2
SC_SUBCORES = 16
SC_TILE_VMEM_BUDGET = 400 * 1024
SC_GATHER_WINDOW = 128
COMBINE_TILE = 256
COMBINE_WINDOW = 256
COMBINE_FIRST = 64
BF16_ROWS = 16


def _cparams(sem):
    return pltpu.CompilerParams(dimension_semantics=sem, vmem_limit_bytes=VMEM_LIMIT)


def _rms(x, g):
    return x * lax.rsqrt(jnp.mean(x * x, axis=-1, keepdims=True) + EPS) * g


def _sigmoid(x):
    return 1.0 / (1.0 + jnp.exp(-x))


def _silu(x):
    return x * _sigmoid(x)


def _dot(a, b):
    return jnp.dot(a, b, preferred_element_type=F32)


def _pack_bf16_pairs(h):
    half = h.shape[-1] // 2
    bits = lax.bitcast_convert_type(h.astype(BF16).astype(F32), jnp.uint32)
    packed = (bits[:, half:] & jnp.uint32(0xFFFF0000)) | (bits[:, :half] >> 16)
    return lax.bitcast_convert_type(packed, jnp.int32)


def _unpack_bf16_pairs(p):
    bits = lax.bitcast_convert_type(p, jnp.uint32)
    lo = lax.bitcast_convert_type(bits << 16, F32).astype(BF16)
    hi = lax.bitcast_convert_type(bits & jnp.uint32(0xFFFF0000), F32).astype(BF16)
    return lo, hi


def _dot_nt(a, b):
    return lax.dot_general(a, b, (((1,), (1,)), ((), ())), preferred_element_type=F32)


def _mod_body(c_ref, w_ref, b_ref, o_ref):
    o_ref[0] = jnp.dot(_silu(c_ref[...]), w_ref[0], precision=HIGHEST,
                       preferred_element_type=F32) + b_ref[0]


def _modulation(rows, w, b):
    depth, _, n = w.shape
    tn = 1536
    return pl.pallas_call(
        _mod_body,
        grid=(depth, n // tn),
        in_specs=[pl.BlockSpec((8, D_MODEL), lambda i, j: (0, 0)),
                  pl.BlockSpec((1, D_MODEL, tn), lambda i, j: (i, 0, j)),
                  pl.BlockSpec((1, 1, tn), lambda i, j: (i, 0, j))],
        out_specs=pl.BlockSpec((1, 8, tn), lambda i, j: (i, 0, j)),
        out_shape=jax.ShapeDtypeStruct((depth, 8, n), F32),
        compiler_params=_cparams(("parallel", "parallel")),
        name="modulation",
    )(rows, w, b.reshape(depth, 1, n))


def _even_in_body(x_ref, g_ref, sh_ref, sc_ref, w_ref, glu_ref, qkv_ref):
    h = (_rms(x_ref[0], g_ref[...]) * (1.0 + sc_ref[0]) + sh_ref[0]).astype(BF16)
    c = CONV_CH
    glu_ref[0] = _dot(h, w_ref[:, 0:c]) * _sigmoid(_dot(h, w_ref[:, c:2 * c]))
    hd = NA_HEADS * NA_HEAD_DIM
    q0 = 2 * c
    qkv_ref[0, :, 0:hd] = (_dot(h, w_ref[:, q0:q0 + hd]) * (NA_HEAD_DIM ** -0.5)).astype(BF16)
    qkv_ref[0, :, hd:3 * hd] = _dot(h, w_ref[:, q0 + hd:q0 + 3 * hd]).astype(BF16)


def _even_in(x, g, shift, scale, w_bf, tm):
    b, l, d = x.shape
    n = w_bf.shape[1]
    hd3 = 3 * NA_HEADS * NA_HEAD_DIM
    vec = pl.BlockSpec((1, 1, d), lambda i, j: (i, 0, 0))
    return pl.pallas_call(
        _even_in_body,
        grid=(b, l // tm),
        in_specs=[pl.BlockSpec((1, tm, d), lambda i, j: (i, j, 0)),
                  pl.BlockSpec((1, d), lambda i, j: (0, 0)),
                  vec, vec,
                  pl.BlockSpec((d, n), lambda i, j: (0, 0))],
        out_specs=[pl.BlockSpec((1, tm, CONV_CH), lambda i, j: (i, j, 0)),
                   pl.BlockSpec((1, tm, hd3), lambda i, j: (i, j, 0))],
        out_shape=[jax.ShapeDtypeStruct((b, l, CONV_CH), F32),
                   jax.ShapeDtypeStruct((b, l, hd3), BF16)],
        compiler_params=_cparams(("parallel", "parallel")),
        name="even_in",
    )(x, g.reshape(1, d), shift.reshape(b, 1, d), scale.reshape(b, 1, d), w_bf)


def _conv_body(cur_ref, prev_ref, next_ref, w_ref, b_ref, lg_ref, lb_ref, o_ref, buf_ref, sh_ref, *, tile, chunk):
    j = pl.program_id(1)
    last = pl.num_programs(1) - 1
    hal = CONV_HALO
    buf_ref[0:hal, :] = jnp.where(j == 0, 0.0, prev_ref[0])
    buf_ref[hal:hal + tile, :] = cur_ref[0]
    buf_ref[hal + tile:hal + tile + hal, :] = jnp.where(j == last, 0.0, next_ref[0])
    span = sh_ref.shape[1]
    for s in range(SUBLANES):
        sh_ref[s] = buf_ref[s:s + span, :]
    first = hal - CONV_WIDTH // 2
    reps = chunk // SUBLANES

    def rows(c, carry):
        r0 = pl.multiple_of(c * chunk, chunk)
        accs = [jnp.zeros((chunk, CONV_CH), F32) for _ in range(2)]
        for k in range(CONV_WIDTH):
            a, s = divmod(first + k, SUBLANES)
            wk = jnp.concatenate([w_ref[k]] * reps, axis=0)
            accs[k % 2] = accs[k % 2] + sh_ref[s, pl.ds(r0 + a * SUBLANES, chunk), :] * wk
        o_ref[0, pl.ds(r0, chunk), :] = accs[0] + accs[1]
        return carry

    lax.fori_loop(0, tile // chunk, rows, 0)
    y = o_ref[0] + b_ref[...]
    mu = jnp.mean(y, axis=-1, keepdims=True)
    yc = y - mu
    var = jnp.mean(yc * yc, axis=-1, keepdims=True)
    o_ref[0] = _silu(yc * lax.rsqrt(var + EPS) * lg_ref[...] + lb_ref[...])


def _conv_branch(glu, conv_w, conv_b, ln_g, ln_b, tile):
    b, l, c = glu.shape
    hal = CONV_HALO
    per = tile // hal
    nh = l // hal
    vec = pl.BlockSpec((1, c), lambda i, j: (0, 0))
    return pl.pallas_call(
        functools.partial(_conv_body, tile=tile, chunk=32),
        grid=(b, l // tile),
        in_specs=[pl.BlockSpec((1, tile, c), lambda i, j: (i, j, 0)),
                  pl.BlockSpec((1, hal, c), lambda i, j: (i, jnp.maximum(j * per - 1, 0), 0)),
                  pl.BlockSpec((1, hal, c), lambda i, j: (i, jnp.minimum((j + 1) * per, nh - 1), 0)),
                  pl.BlockSpec((CONV_WIDTH, SUBLANES, c), lambda i, j: (0, 0, 0)),
                  vec, vec, vec],
        out_specs=pl.BlockSpec((1, tile, c), lambda i, j: (i, j, 0)),
        out_shape=jax.ShapeDtypeStruct((b, l, c), F32),
        scratch_shapes=[pltpu.VMEM((tile + 2 * hal, c), F32),
                        pltpu.VMEM((SUBLANES, tile + 2 * hal - SUBLANES, c), F32)],
        compiler_params=_cparams(("parallel", "parallel")),
        name="conv_branch",
    )(glu, glu, glu, jnp.broadcast_to(conv_w[:, None, :], (CONV_WIDTH, SUBLANES, c)),
      conv_b.reshape(1, c), ln_g.reshape(1, c), ln_b.reshape(1, c))


def _na_window_start(j, rows):
    rb = NA_ROWS_PER_BLOCK
    return jnp.clip(j * rb - NA_KR // 2, 0, rows - NA_WIN_ROWS)


def _na_body(q_ref, k_ref, v_ref, kc_ref, vc_ref, *rest, rows):
    tab_refs, o_ref = rest[:-1], rest[-1]
    j = pl.program_id(2)
    nkeys = NA_WIN_ROWS * GRID_W
    tq = NA_ROWS_PER_BLOCK * GRID_W
    kc = kc_ref[0]
    vc = vc_ref[0]
    lane = lax.broadcasted_iota(jnp.int32, (1, LANES), 1)
    for sb, tab_ref in enumerate(tab_refs):
        start = pl.multiple_of(_na_window_start(j * len(tab_refs) + sb, rows) * GRID_W, GRID_W)
        q = q_ref[0, sb * tq:(sb + 1) * tq, :]
        kw = k_ref[0, pl.ds(start, nkeys), :]
        vw = v_ref[0, pl.ds(start, nkeys), :]
        out = jnp.zeros(q.shape, F32)
        for hh in range(LANES // NA_HEAD_DIM):
            in_head = (lane >= hh * NA_HEAD_DIM) & (lane < (hh + 1) * NA_HEAD_DIM)
            qh = jnp.where(in_head, q, jnp.zeros_like(q))
            s = _dot_nt(qh, kw) + tab_ref[0, hh]
            sc = _dot_nt(qh, kc)
            m = jnp.maximum(jnp.max(s, axis=-1, keepdims=True), jnp.max(sc, axis=-1, keepdims=True))
            p = jnp.exp(s - m)
            pc = jnp.exp(sc - m)
            denom = jnp.sum(p, axis=-1, keepdims=True) + jnp.sum(pc, axis=-1, keepdims=True)
            o = (_dot(p.astype(BF16), vw) + _dot(pc.astype(BF16), vc)) / denom
            out = jnp.where(in_head, o, out)
        o_ref[0, sb * tq:(sb + 1) * tq, :] = out


def _na_tables(rpb, rows):
    rb = NA_ROWS_PER_BLOCK
    nblk = rows // rb
    wr = NA_WIN_ROWS
    qc = jnp.arange(GRID_W)
    cs = jnp.clip(qc - NA_KC // 2, 0, GRID_W - NA_KC)
    col_ok = (qc[None, :] >= cs[:, None]) & (qc[None, :] < cs[:, None] + NA_KC)
    col_off = qc[None, :] - qc[:, None] + NA_KC - 1
    onehot = (col_off[:, :, None] == jnp.arange(2 * NA_KC - 1)[None, None, :]).astype(F32)
    blocks = jnp.einsum('hrd,qkd->hrqk', rpb.astype(F32), onehot, precision=HIGHEST)
    blocks = jnp.where(col_ok[None, None], blocks, NEG_BIG)
    masked = jnp.full((NA_HEADS, GRID_W, GRID_W), NEG_BIG, F32)
    tabs = []
    for jb in (0, 1, nblk - 1):
        ws = min(max(jb * rb - NA_KR // 2, 0), rows - wr)
        q_rows = []
        for qr in range(jb * rb, (jb + 1) * rb):
            rs = min(max(qr - NA_KR // 2, 0), rows - NA_KR)
            row = [blocks[:, kr - qr + NA_KR - 1] if rs <= kr < rs + NA_KR else masked
                   for kr in range(ws, ws + wr)]
            q_rows.append(jnp.concatenate(row, axis=-1))
        tabs.append(jnp.concatenate(q_rows, axis=1))
    return jnp.stack(tabs)


def _neighbourhood_attention(qkv, qkv_ctx, rpb):
    b, l, _ = qkv.shape
    n_ctx = qkv_ctx.shape[1]
    rows = l // GRID_W
    rb = NA_ROWS_PER_BLOCK
    nblk = rows // rb
    tq = rb * GRID_W
    nkeys = NA_WIN_ROWS * GRID_W
    hp = NA_HEADS * NA_HEAD_DIM // LANES
    tabs = _na_tables(rpb, rows)

    per = NA_BLOCKS_PER_STEP

    def cls(jb):
        return jnp.where(jb == 0, 0, jnp.where(jb == nblk - 1, 2, 1))

    def tab_spec(sb):
        return pl.BlockSpec((1, 2, tq, nkeys), lambda i, h, j: (cls(j * per + sb), h, 0, 0))

    return pl.pallas_call(
        functools.partial(_na_body, rows=rows),
        grid=(b, hp, nblk // per),
        in_specs=[pl.BlockSpec((1, per * tq, LANES), lambda i, h, j: (i, j, h)),
                  pl.BlockSpec((1, l, LANES), lambda i, h, j: (i, 0, hp + h)),
                  pl.BlockSpec((1, l, LANES), lambda i, h, j: (i, 0, 2 * hp + h)),
                  pl.BlockSpec((1, n_ctx, LANES), lambda i, h, j: (i, 0, hp + h)),
                  pl.BlockSpec((1, n_ctx, LANES), lambda i, h, j: (i, 0, 2 * hp + h))]
                 + [tab_spec(sb) for sb in range(per)],
        out_specs=pl.BlockSpec((1, per * tq, LANES), lambda i, h, j: (i, j, h)),
        out_shape=jax.ShapeDtypeStruct((b, l, NA_HEADS * NA_HEAD_DIM), F32),
        compiler_params=_cparams(("parallel", "parallel", "arbitrary")),
        name="neighbourhood_attention",
    )(qkv, qkv, qkv, qkv_ctx, qkv_ctx, *([tabs] * per))


def _ctx_attn_body(q_ref, k_ref, v_ref, o_ref):
    q = q_ref[0]
    k = k_ref[0]
    v = v_ref[0]
    lane = lax.broadcasted_iota(jnp.int32, (1, LANES), 1)
    out = jnp.zeros(q.shape, F32)
    for hh in range(LANES // NA_HEAD_DIM):
        in_head = (lane >= hh * NA_HEAD_DIM) & (lane < (hh + 1) * NA_HEAD_DIM)
        qh = jnp.where(in_head, q, jnp.zeros_like(q))
        s = _dot_nt(qh, k)
        p = jnp.exp(s - jnp.max(s, axis=-1, keepdims=True))
        o = _dot(p.astype(BF16), v) / jnp.sum(p, axis=-1, keepdims=True)
        out = jnp.where(in_head, o, out)
    o_ref[0] = out


def _context_attention(qkv_ctx):
    b, n, _ = qkv_ctx.shape
    hp = NA_HEADS * NA_HEAD_DIM // LANES
    return pl.pallas_call(
        _ctx_attn_body,
        grid=(b, hp),
        in_specs=[pl.BlockSpec((1, n, LANES), lambda i, h: (i, 0, h)),
                  pl.BlockSpec((1, n, LANES), lambda i, h: (i, 0, hp + h)),
                  pl.BlockSpec((1, n, LANES), lambda i, h: (i, 0, 2 * hp + h))],
        out_specs=pl.BlockSpec((1, n, LANES), lambda i, h: (i, 0, h)),
        out_shape=jax.ShapeDtypeStruct((b, n, NA_HEADS * NA_HEAD_DIM), F32),
        compiler_params=_cparams(("parallel", "parallel")),
        name="context_attention",
    )(qkv_ctx, qkv_ctx, qkv_ctx)


def _out_body(a_ref, b_ref, x_ref, w_ref, g1_ref, gate_ref, g2_ref, sh_ref, sc_ref, rwh_ref, rwl_ref,
              xo_ref, h_ref, aff_ref):
    half = a_ref.shape[-1]
    y = _dot(a_ref[0].astype(BF16), w_ref[0:half, :]) + _dot(b_ref[0].astype(BF16), w_ref[half:2 * half, :])
    xn = x_ref[0] + gate_ref[0] * _rms(y, g1_ref[...])
    xo_ref[0] = xn
    h = _rms(xn, g2_ref[...]) * (1.0 + sc_ref[0]) + sh_ref[0]
    packed = _pack_bf16_pairs(h)
    quarter = packed.shape[-1] // 2
    h_ref[0, 0] = packed[:, 0:quarter]
    h_ref[0, 1] = packed[:, quarter:2 * quarter]
    h_hi = h.astype(BF16)
    h_lo = (h - h_hi.astype(F32)).astype(BF16)
    logits = _dot(h_hi, rwh_ref[...]) + (_dot(h_lo, rwh_ref[...]) + _dot(h_hi, rwl_ref[...]))
    lane = lax.broadcasted_iota(jnp.int32, (1, LANES), 1)
    logits = jnp.where(lane < N_EXPERTS, logits, NEG_BIG)
    e = jnp.exp(logits - jnp.max(logits, axis=-1, keepdims=True))
    aff = e / jnp.sum(e, axis=-1, keepdims=True)
    aff_ref[0] = aff.T[0:N_EXPERTS, :]


def _out_proj(a, b2, x, w_bf, g1, gate, g2, shift, scale, router_w, tm):
    b, l, d = x.shape
    half = a.shape[-1]
    rw = jnp.pad(router_w, ((0, 0), (0, LANES - N_EXPERTS)))
    rw_hi = rw.astype(BF16)
    rw_lo = (rw - rw_hi.astype(F32)).astype(BF16)
    vec = pl.BlockSpec((1, d), lambda i, j: (0, 0))
    bvec = pl.BlockSpec((1, 1, d), lambda i, j: (i, 0, 0))
    rspec = pl.BlockSpec((d, LANES), lambda i, j: (0, 0))
    return pl.pallas_call(
        _out_body,
        grid=(b, l // tm),
        in_specs=[pl.BlockSpec((1, tm, half), lambda i, j: (i, j, 0)),
                  pl.BlockSpec((1, tm, half), lambda i, j: (i, j, 0)),
                  pl.BlockSpec((1, tm, d), lambda i, j: (i, j, 0)),
                  pl.BlockSpec((2 * half, d), lambda i, j: (0, 0)),
                  vec, bvec, vec, bvec, bvec, rspec, rspec],
        out_specs=[pl.BlockSpec((1, tm, d), lambda i, j: (i, j, 0)),
                   pl.BlockSpec((1, 2, tm, d // 4), lambda i, j: (i, 0, j, 0)),
                   pl.BlockSpec((1, N_EXPERTS, tm), lambda i, j: (i, 0, j))],
        out_shape=[jax.ShapeDtypeStruct((b, l, d), F32),
                   jax.ShapeDtypeStruct((b, 2, l, d // 4), jnp.int32),
                   jax.ShapeDtypeStruct((b, N_EXPERTS, l), F32)],
        compiler_params=_cparams(("parallel", "parallel")),
        name="out_proj",
    )(a, b2, x, w_bf, g1.reshape(1, d), gate.reshape(b, 1, d), g2.reshape(1, d),
      shift.reshape(b, 1, d), scale.reshape(b, 1, d), rw_hi, rw_lo)


def _moe_body(x_ref, val_ref, wg_ref, wu_ref, wd_ref, o_ref, acc_s, x_s, wg_s, wu_s, wd_s, *, chunk):
    f = pl.program_id(1)
    m = x_ref.shape[2]
    quarter = x_ref.shape[3]

    @pl.when(f == 0)
    def _():
        def unpack(c, carry):
            r = pl.multiple_of(c * chunk, chunk)
            for s in range(2):
                lo, hi = _unpack_bf16_pairs(x_ref[0, s, pl.ds(r, chunk), :])
                x_s[pl.ds(r, chunk), s * quarter:(s + 1) * quarter] = lo
                x_s[pl.ds(r, chunk), (2 + s) * quarter:(3 + s) * quarter] = hi
            acc_s[pl.ds(r, chunk), :] = jnp.zeros((chunk, 4 * quarter), F32)
            return carry

        lax.fori_loop(0, m // chunk, unpack, 0)

    wg_s[...] = wg_ref[0, 0].astype(BF16)
    wu_s[...] = wu_ref[0, 0].astype(BF16)
    wd_s[...] = wd_ref[0, 0].astype(BF16)

    def rows(c, carry):
        r = pl.multiple_of(c * chunk, chunk)
        xs = x_s[pl.ds(r, chunk), :]
        hid = (_silu(_dot(xs, wg_s[...])) * _dot(xs, wu_s[...])).astype(BF16)
        acc_s[pl.ds(r, chunk), :] += _dot(hid, wd_s[...])
        return carry

    lax.fori_loop(0, m // chunk, rows, 0, unroll=True)

    @pl.when(f == pl.num_programs(1) - 1)
    def _():
        o_ref[0] = (acc_s[...] * val_ref[0]).astype(o_ref.dtype)


def _expert_ffn(xg, vals, w_gate, w_up, w_down, layer, chunk):
    e, _, m, quarter = xg.shape
    d = 4 * quarter
    ff = w_gate.shape[-1]
    tf = 256
    return pl.pallas_call(
        functools.partial(_moe_body, chunk=chunk),
        grid=(e, ff // tf),
        in_specs=[pl.BlockSpec((1, 2, m, quarter), lambda i, f: (i, 0, 0, 0)),
                  pl.BlockSpec((1, m, 1), lambda i, f: (i, 0, 0)),
                  pl.BlockSpec((1, 1, d, tf), lambda i, f: (layer, i, 0, f)),
                  pl.BlockSpec((1, 1, d, tf), lambda i, f: (layer, i, 0, f)),
                  pl.BlockSpec((1, 1, tf, d), lambda i, f: (layer, i, f, 0))],
        out_specs=pl.BlockSpec((1, m, d), lambda i, f: (i, 0, 0)),
        out_shape=jax.ShapeDtypeStruct((e, m, d), BF16),
        scratch_shapes=[pltpu.VMEM((m, d), F32), pltpu.VMEM((m, d), BF16), pltpu.VMEM((d, tf), BF16),
                        pltpu.VMEM((d, tf), BF16), pltpu.VMEM((tf, d), BF16)],
        compiler_params=_cparams(("parallel", "arbitrary")),
        name="expert_ffn",
    )(xg, vals, w_gate, w_up, w_down)


def _resid_body(x_ref, f_ref, gate_ref, g_ref, o_ref):
    o_ref[0] = x_ref[0] + gate_ref[0] * _rms(f_ref[0], g_ref[...])


def _gated_residual(x, f, gate, g, tm):
    b, l, d = x.shape
    blk = pl.BlockSpec((1, tm, d), lambda i, j: (i, j, 0))
    return pl.pallas_call(
        _resid_body,
        grid=(b, l // tm),
        in_specs=[blk, blk, pl.BlockSpec((1, 1, d), lambda i, j: (i, 0, 0)),
                  pl.BlockSpec((1, d), lambda i, j: (0, 0))],
        out_specs=blk,
        out_shape=jax.ShapeDtypeStruct((b, l, d), F32),
        compiler_params=_cparams(("parallel", "parallel")),
        name="gated_residual",
    )(x, f, gate.reshape(b, 1, d), g.reshape(1, d))


def _swap_pairs(x):
    nf = GLA_DK // 4
    lane = lax.broadcasted_iota(jnp.int32, (1, LANES), 1)
    up = pltpu.roll(x, LANES - nf, 1)
    down = pltpu.roll(x, nf, 1)
    return jnp.where(lane % (2 * nf) < nf, up, down)


def _odd_in_body(x_ref, g_ref, sh_ref, sc_ref, w_ref, cos_ref, sin_ref, gw_ref, gb_ref,
                 pool_ref, qk_ref, v_ref, r_ref, gate_ref):
    h = (_rms(x_ref[0], g_ref[...]) * (1.0 + sc_ref[0]) + sh_ref[0]).astype(BF16)
    qk = GLA_HEADS * GLA_DK
    vd = GLA_HEADS * GLA_DV
    q0 = POOL_CH
    v0 = q0 + 2 * qk
    r0 = v0 + vd
    l0 = r0 + vd
    pool_ref[0] = _dot(h, w_ref[:, 0:q0])
    for s in range(2 * qk // LANES):
        raw = _dot(h, w_ref[:, q0 + s * LANES:q0 + (s + 1) * LANES])
        c = cos_ref[:, (s * LANES) % qk:(s * LANES) % qk + LANES]
        sn = sin_ref[:, (s * LANES) % qk:(s * LANES) % qk + LANES]
        rot = raw * c + _swap_pairs(raw) * sn
        if s * LANES < qk:
            rot = rot * (GLA_DK ** -0.5)
        qk_ref[0, :, s * LANES:(s + 1) * LANES] = rot
    v_ref[0] = _dot(h, w_ref[:, v0:r0]).astype(BF16)
    r_ref[0] = _dot(h, w_ref[:, r0:l0])
    lr = _dot(h, w_ref[:, l0:l0 + 2 * GLA_RANK])
    z = jnp.dot(lr, gw_ref[...], precision=HIGHEST, preferred_element_type=F32) + gb_ref[...]
    gate_ref[0] = (jnp.minimum(z, 0.0) - jnp.log1p(jnp.exp(-jnp.abs(z)))) * (1.0 / GLA_TAU)


def _odd_in(x, g, shift, scale, w_bf, cos_t, sin_t, gate_w, gate_b, tm):
    b, l, d = x.shape
    n = w_bf.shape[1]
    qk = GLA_HEADS * GLA_DK
    vd = GLA_HEADS * GLA_DV
    gw = jnp.zeros((2 * GLA_RANK, 2 * qk), F32)
    gw = gw.at[:GLA_RANK, :qk].set(gate_w[0]).at[GLA_RANK:, qk:].set(gate_w[1])
    gb = jnp.concatenate([gate_b[0], gate_b[1]]).reshape(1, 2 * qk)
    vec = pl.BlockSpec((1, 1, d), lambda i, j: (i, 0, 0))
    row = lambda w: pl.BlockSpec((1, tm, w), lambda i, j: (i, j, 0))
    return pl.pallas_call(
        _odd_in_body,
        grid=(b, l // tm),
        in_specs=[row(d), pl.BlockSpec((1, d), lambda i, j: (0, 0)), vec, vec,
                  pl.BlockSpec((d, n), lambda i, j: (0, 0)),
                  pl.BlockSpec((tm, qk), lambda i, j: (j, 0)),
                  pl.BlockSpec((tm, qk), lambda i, j: (j, 0)),
                  pl.BlockSpec((2 * GLA_RANK, 2 * qk), lambda i, j: (0, 0)),
                  pl.BlockSpec((1, 2 * qk), lambda i, j: (0, 0))],
        out_specs=[row(POOL_CH), row(2 * qk), row(vd), row(vd), row(2 * qk)],
        out_shape=[jax.ShapeDtypeStruct((b, l, POOL_CH), F32),
                   jax.ShapeDtypeStruct((b, l, 2 * qk), F32),
                   jax.ShapeDtypeStruct((b, l, vd), BF16),
                   jax.ShapeDtypeStruct((b, l, vd), F32),
                   jax.ShapeDtypeStruct((b, l, 2 * qk), F32)],
        compiler_params=_cparams(("parallel", "parallel")),
        name="odd_in",
    )(x, g.reshape(1, d), shift.reshape(b, 1, d), scale.reshape(b, 1, d), w_bf, cos_t, sin_t, gw, gb)


def _rope_tables(l):
    t = jnp.arange(l)
    pos_r = (t // GRID_W).astype(F32)
    pos_c = (t % GRID_W).astype(F32)
    nf = GLA_DK // 4
    inv = jnp.power(ROPE_BASE, -jnp.arange(nf, dtype=F32) / nf)
    ar = pos_r[:, None] * inv[None, :]
    ac = pos_c[:, None] * inv[None, :]
    cos_h = jnp.concatenate([jnp.cos(ar), jnp.cos(ar), jnp.cos(ac), jnp.cos(ac)], axis=-1)
    sin_h = jnp.concatenate([-jnp.sin(ar), jnp.sin(ar), -jnp.sin(ac), jnp.sin(ac)], axis=-1)
    return jnp.tile(cos_h, (1, GLA_HEADS)), jnp.tile(sin_h, (1, GLA_HEADS))


def _gla_tile(qk, v, g, s, reverse):
    hk = GLA_HEADS * GLA_DK
    hv = GLA_HEADS * GLA_DV
    c = GLA_CHUNK
    t = qk.shape[0]
    n = t // c
    last_row, mid_row = (0, c // 2) if reverse else (c - 1, c // 2 - 1)
    ii = lax.broadcasted_iota(jnp.int32, (t, t), 0)
    jj = lax.broadcasted_iota(jnp.int32, (t, t), 1)
    ordered = (jj >= ii) if reverse else (jj <= ii)
    tri = jnp.where(ordered & (ii // c == jj // c), 1.0, 0.0).astype(BF16)
    g_hi = g.astype(BF16)
    rem = g - g_hi.astype(F32)
    g_mid = rem.astype(BF16)
    g_lo = (rem - g_mid.astype(F32)).astype(BF16)
    bc = _dot(tri, g_hi) + (_dot(tri, g_mid) + _dot(tri, g_lo))
    spread = lambda row: jnp.concatenate(
        [jnp.broadcast_to(bc[i * c + row:i * c + row + 1, :], (c, hk)) for i in range(n)], axis=0)
    b_mid = spread(mid_row)
    b_last = spread(last_row)
    qt = qk[:, 0:hk] * jnp.exp(bc - b_mid)
    kt = qk[:, hk:2 * hk] * jnp.exp(b_mid - bc)
    qe = (qt * jnp.exp(b_mid)).astype(BF16)
    ke = kt * jnp.exp(b_last - b_mid)
    ktb = kt.astype(BF16)
    lane = lax.broadcasted_iota(jnp.int32, (1, hk), 1)
    ci = lax.broadcasted_iota(jnp.int32, (c, c), 0)
    cj = lax.broadcasted_iota(jnp.int32, (c, c), 1)
    causal = (cj >= ci) if reverse else (cj <= ci)
    blockdiag = (lax.broadcasted_iota(jnp.int32, (hk, hv), 0) // GLA_DK
                 == lax.broadcasted_iota(jnp.int32, (hk, hv), 1) // GLA_DV)
    intra, upd, decay = [], [], []
    for i in range(n):
        rows = slice(i * c, (i + 1) * c)
        qs = jnp.concatenate(
            [jnp.where((lane >= h * GLA_DK) & (lane < (h + 1) * GLA_DK), qt[rows], 0.0) for h in range(GLA_HEADS)],
            axis=0).astype(BF16)
        att = _dot_nt(qs, ktb[rows])
        intra.append(jnp.concatenate(
            [_dot(jnp.where(causal, att[h * c:(h + 1) * c], 0.0).astype(BF16),
                  v[rows, h * GLA_DV:(h + 1) * GLA_DV]) for h in range(GLA_HEADS)], axis=-1))
        upd.append(jnp.where(blockdiag, _dot(ke[rows].T.astype(BF16), v[rows]), 0.0))
        decay.append(jnp.exp(jnp.sum(g[rows].T, axis=1, keepdims=True)))
    outs = [None] * n
    for i in (reversed(range(n)) if reverse else range(n)):
        rows = slice(i * c, (i + 1) * c)
        outs[i] = _dot(qe[rows], s.astype(BF16)) + intra[i]
        s = decay[i] * s + upd[i]
    return jnp.concatenate(outs, axis=0), s


def _gla_body(qkf_ref, qkb_ref, vf_ref, vb_ref, gf_ref, gb_ref, s0f_ref, s0b_ref,
              of_ref, ob_ref, sff_ref, sbf_ref, sf_ref, sb_ref, *, tile):
    n = pl.program_id(1)
    hk = GLA_HEADS * GLA_DK
    hv = GLA_HEADS * GLA_DV

    @pl.when(n == 0)
    def _():
        sf_ref[...] = jnp.zeros((hk, hv), F32)
        sb_ref[...] = jnp.zeros((hk, hv), F32)
        for h in range(GLA_HEADS):
            sf_ref[h * GLA_DK:(h + 1) * GLA_DK, h * GLA_DV:(h + 1) * GLA_DV] = s0f_ref[0, h]
            sb_ref[h * GLA_DK:(h + 1) * GLA_DK, h * GLA_DV:(h + 1) * GLA_DV] = s0b_ref[0, h]

    of_ref[0], sf_ref[...] = _gla_tile(qkf_ref[0], vf_ref[0], gf_ref[0], sf_ref[...], False)
    ob_ref[0], sb_ref[...] = _gla_tile(qkb_ref[0], vb_ref[0], gb_ref[0], sb_ref[...], True)

    @pl.when(n == pl.num_programs(1) - 1)
    def _():
        for h in range(GLA_HEADS):
            sff_ref[0, h] = sf_ref[h * GLA_DK:(h + 1) * GLA_DK, h * GLA_DV:(h + 1) * GLA_DV]
            sbf_ref[0, h] = sb_ref[h * GLA_DK:(h + 1) * GLA_DK, h * GLA_DV:(h + 1) * GLA_DV]


def _gla(qk, v, gates, s0f, s0b, tile):
    b, l, _ = qk.shape
    hk = GLA_HEADS * GLA_DK
    hv = GLA_HEADS * GLA_DV
    nt = l // tile
    fwd = lambda w, col: pl.BlockSpec((1, tile, w), lambda i, n: (i, n, col))
    bwd = lambda w, col: pl.BlockSpec((1, tile, w), lambda i, n: (i, nt - 1 - n, col))
    st = pl.BlockSpec((1, GLA_HEADS, GLA_DK, GLA_DV), lambda i, n: (i, 0, 0, 0))
    return pl.pallas_call(
        functools.partial(_gla_body, tile=tile),
        grid=(b, nt),
        in_specs=[fwd(2 * hk, 0), bwd(2 * hk, 0), fwd(hv, 0), bwd(hv, 0), fwd(hk, 0), bwd(hk, 1), st, st],
        out_specs=[fwd(hv, 0), bwd(hv, 0), st, st],
        out_shape=[jax.ShapeDtypeStruct((b, l, hv), F32), jax.ShapeDtypeStruct((b, l, hv), F32),
                   jax.ShapeDtypeStruct((b, GLA_HEADS, GLA_DK, GLA_DV), F32),
                   jax.ShapeDtypeStruct((b, GLA_HEADS, GLA_DK, GLA_DV), F32)],
        scratch_shapes=[pltpu.VMEM((hk, hv), F32), pltpu.VMEM((hk, hv), F32)],
        compiler_params=_cparams(("parallel", "arbitrary")),
        name="gla_scan",
    )(qk, qk, v, v, gates, gates, s0f, s0b)


def _odd_mid_body(cur_ref, prev_ref, next_ref, of_ref, ob_ref, r_ref, hg_ref, pw_ref, ps_ref,
                  pool_ref, d_ref, buf_ref, *, tile, seq):
    j = pl.program_id(1)
    last = pl.num_programs(1) - 1
    hal = POOL_HALO
    buf_ref[0:hal, :] = jnp.where(j == 0, 0.0, prev_ref[0])
    buf_ref[hal:hal + tile, :] = cur_ref[0]
    buf_ref[hal + tile:hal + tile + hal, :] = jnp.where(j == last, 0.0, next_ref[0])
    t = j * tile + lax.broadcasted_iota(jnp.int32, (tile, 1), 0)
    for gi, win in enumerate(POOL_WINDOWS):
        cols = slice(gi * POOL_GROUP, (gi + 1) * POOL_GROUP)
        acc = jnp.zeros((tile, POOL_GROUP), F32)
        for off in range(-(win // 2), win - win // 2):
            acc = acc + buf_ref[hal + off:hal + off + tile, cols]
        cnt = jnp.minimum(t + (win - win // 2), seq) - jnp.maximum(t - win // 2, 0)
        diff = acc / cnt.astype(F32) - cur_ref[0, :, cols]
        pool_ref[0, :, cols] = _dot(diff.astype(BF16), pw_ref[gi]) * ps_ref[:, cols]
    for h in range(GLA_HEADS):
        cols = slice(h * GLA_DV, (h + 1) * GLA_DV)
        o = of_ref[0, :, cols] + ob_ref[0, :, cols]
        d_ref[0, :, cols] = _rms(o, hg_ref[:, cols]) * _silu(r_ref[0, :, cols])


def _odd_mid(pool_u, o_f, o_b, r, head_g, pool_w_bf, pool_scale, tile):
    b, l, c = pool_u.shape
    hal = POOL_HALO
    per = tile // hal
    nh = l // hal
    blk = pl.BlockSpec((1, tile, c), lambda i, j: (i, j, 0))
    vec = pl.BlockSpec((1, c), lambda i, j: (0, 0))
    return pl.pallas_call(
        functools.partial(_odd_mid_body, tile=tile, seq=l),
        grid=(b, l // tile),
        in_specs=[blk,
                  pl.BlockSpec((1, hal, c), lambda i, j: (i, jnp.maximum(j * per - 1, 0), 0)),
                  pl.BlockSpec((1, hal, c), lambda i, j: (i, jnp.minimum((j + 1) * per, nh - 1), 0)),
                  blk, blk, blk, vec,
                  pl.BlockSpec((len(POOL_WINDOWS), POOL_GROUP, POOL_GROUP), lambda i, j: (0, 0, 0)),
                  vec],
        out_specs=[blk, blk],
        out_shape=[jax.ShapeDtypeStruct((b, l, c), F32), jax.ShapeDtypeStruct((b, l, c), F32)],
        scratch_shapes=[pltpu.VMEM((tile + 2 * hal, c), F32)],
        compiler_params=_cparams(("parallel", "parallel")),
        name="odd_mid",
    )(pool_u, pool_u, pool_u, o_f, o_b, r, head_g.reshape(1, c), pool_w_bf, pool_scale.reshape(1, c))


def _route(aff_t, tok_base, row_base):
    b, e, n = aff_t.shape
    cap = EC_CAPACITY_FACTOR * n // N_EXPERTS
    vals, idx = lax.top_k(aff_t, cap)
    idx, vals = lax.sort((idx, vals), dimension=2, num_keys=1)
    bi = jnp.arange(b, dtype=idx.dtype)[:, None, None]
    per_expert = lambda a: jnp.swapaxes(a, 0, 1).reshape(e, b * cap)
    rows0 = idx + row_base + 2 * bi * n
    return (per_expert(vals), per_expert(idx + tok_base + bi * n), per_expert(rows0), per_expert(rows0 + n),
            per_expert(idx))


def _combine_first_start(p0, cap):
    return jnp.minimum((p0 // BF16_ROWS) * BF16_ROWS, cap - COMBINE_FIRST)


def _combine_body(offs_ref, spill_ref, tok_ref, y_ref, x_ref, gate_ref, g_ref, o_ref, f_s, *, n_tok, cap):
    bi = pl.program_id(0)
    j = pl.program_id(1)
    tt = COMBINE_TILE
    wf = COMBINE_FIRST
    wn = COMBINE_WINDOW
    ntiles = n_tok // tt
    n_exp = y_ref.shape[0]
    group = wn // wf
    sub = lax.broadcasted_iota(jnp.int32, (tt, 1), 0)
    lane = lax.broadcasted_iota(jnp.int32, (1, wn), 1)

    def slot_range(e):
        base = (bi * n_exp + e) * (ntiles + 1) + j
        return offs_ref[base], offs_ref[base + 1]

    acc = jnp.zeros(f_s.shape, F32)
    for g in range(n_exp // group):
        ys = []
        toks = jnp.full((1, wn), -1, jnp.int32)
        for k in range(group):
            e = g * group + k
            start = pl.multiple_of(_combine_first_start(slot_range(e)[0], cap), BF16_ROWS)
            ys.append(y_ref[e, pl.ds(start, wf), :])
            cs = pl.multiple_of(jnp.minimum((start // LANES) * LANES, cap - wn), LANES)
            rolled = pltpu.roll(tok_ref[e, :, pl.ds(cs, wn)], (k * wf + wn - (start - cs)) % wn, 1)
            toks = jnp.where((lane >= k * wf) & (lane < (k + 1) * wf), rolled, toks)
        hit = (toks - j * tt) == sub
        acc = acc + _dot(jnp.where(hit, 1.0, 0.0).astype(BF16), jnp.concatenate(ys, axis=0))
    f_s[...] = acc

    def more_windows(e, carry):
        p0, p1 = slot_range(e)
        lo = _combine_first_start(p0, cap) + wf
        first = (lo // LANES) * LANES

        def extra(w, carry):
            cs = pl.multiple_of(jnp.minimum(first + w * wn, cap - wn), LANES)
            tok = tok_ref[e, :, pl.ds(cs, wn)] - j * tt
            hit = (tok == sub) & (cs + lane >= jnp.maximum(lo, first + w * wn))
            f_s[...] += _dot(jnp.where(hit, 1.0, 0.0).astype(BF16), y_ref[e, pl.ds(cs, wn), :])
            return carry

        lax.fori_loop(0, (jnp.maximum(p1 - first, 0) + wn - 1) // wn * (p1 > lo).astype(jnp.int32), extra, 0)
        return carry

    @pl.when(spill_ref[bi * ntiles + j] != 0)
    def _():
        lax.fori_loop(0, n_exp, more_windows, 0)

    o_ref[0] = x_ref[0] + gate_ref[0] * _rms(f_s[...], g_ref[...])


def _combine(y, tok, local, seg0, x, gate, g):
    e, _, d = y.shape
    _, b, cap = local.shape
    n_tok = x.shape[1]
    tt = COMBINE_TILE
    wf = COMBINE_FIRST
    ntiles = n_tok // tt
    assert seg0 % cap == 0 and n_tok % tt == 0 and cap % COMBINE_WINDOW == 0 and e % (COMBINE_WINDOW // wf) == 0
    seg = seg0 // cap
    bounds = jnp.arange(ntiles + 1, dtype=jnp.int32) * tt
    offs = jnp.sum((local[..., None] < bounds).astype(jnp.int32), axis=2)
    starts = _combine_first_start(offs[..., :-1], cap)
    spill = jnp.any(offs[..., 1:] > starts + wf, axis=0).astype(jnp.int32).reshape(-1)
    row = pl.BlockSpec((1, tt, d), lambda i, j, offs, spill: (i, j, 0))
    grid_spec = pltpu.PrefetchScalarGridSpec(
        num_scalar_prefetch=2,
        grid=(b, ntiles),
        in_specs=[pl.BlockSpec((e, 1, cap), lambda i, j, offs, spill: (0, 0, seg + i)),
                  pl.BlockSpec((e, cap, d), lambda i, j, offs, spill: (0, seg + i, 0),
                               pipeline_mode=pl.Buffered(1)),
                  row,
                  pl.BlockSpec((1, 1, d), lambda i, j, offs, spill: (i, 0, 0)),
                  pl.BlockSpec((1, d), lambda i, j, offs, spill: (0, 0))],
        out_specs=row,
        scratch_shapes=[pltpu.VMEM((tt, d), F32)],
    )
    return pl.pallas_call(
        functools.partial(_combine_body, n_tok=n_tok, cap=cap),
        grid_spec=grid_spec,
        out_shape=jax.ShapeDtypeStruct((b, n_tok, d), F32),
        compiler_params=_cparams(("parallel", "arbitrary")),
        name="combine",
    )(jnp.swapaxes(offs, 0, 1).reshape(-1), spill, tok, y, x, gate.reshape(b, 1, d), g.reshape(1, d))


def _gather_rows(src, idx):
    window = SC_GATHER_WINDOW
    n = idx.shape[0]
    width = src.shape[1]
    assert 2 * window * width * 4 <= SC_TILE_VMEM_BUDGET, width
    assert n % (window * SC_CORES * SC_SUBCORES) == 0, n
    mesh = plsc.VectorSubcoreMesh(core_axis_name="core", subcore_axis_name="subcore",
                                  num_cores=SC_CORES, num_subcores=SC_SUBCORES)

    @pl.kernel(out_type=jax.ShapeDtypeStruct((n, width), src.dtype), mesh=mesh, scratch_types=[],
               name="gather_rows")
    def gather(src_hbm, idx_hbm, out_hbm):
        def body(idx_vmem, out_vmem):
            pltpu.sync_copy(src_hbm.at[idx_vmem.at[0]], out_vmem)

        pltpu.emit_pipeline(
            body,
            grid=(n // window,),
            in_specs=[pl.BlockSpec((1, window), lambda i: (0, i))],
            out_specs=[pl.BlockSpec((window, width), lambda i: (i, 0))],
            core_axis_name=("core", "subcore"),
            dimension_semantics=(pltpu.PARALLEL,),
        )(idx_hbm, out_hbm)

    return gather(src, idx.reshape(1, n))


def _moe(parts, g, w_gate, w_up, w_down, layer):
    quarter = parts[0][1].shape[-1]
    d = 4 * quarter
    sizes = [p[1].shape[0] * p[1].shape[2] for p in parts]
    bases = [sum(sizes[:i]) for i in range(len(parts))]
    routed = [_route(p[0], base, 2 * base) for p, base in zip(parts, bases)]
    src = jnp.concatenate([p[1].reshape(-1, quarter) for p in parts], axis=0)
    vals, flat, rows0, rows1, tok = (jnp.concatenate([r[i] for r in routed], axis=1) for i in range(5))
    e, m = flat.shape
    pad = -m % LANES
    vals, rows0, rows1, tok = (jnp.pad(a, ((0, 0), (0, pad))) for a in (vals, rows0, rows1, tok))
    m += pad
    rows = jnp.stack([rows0, rows1], axis=1)
    xg = _gather_rows(src, rows.reshape(-1)).reshape(e, 2, m, quarter)
    chunk = next(c for c in (512, 544, 384, 256, 128) if m % c == 0)
    y = _expert_ffn(xg, vals[..., None], w_gate, w_up, w_down, layer, chunk)
    outs = []
    seg0 = 0
    for (aff_t, _, x, gate), r, base in zip(parts, routed, bases):
        b, _, n = aff_t.shape
        cap = r[0].shape[1] // b
        if n % COMBINE_TILE == 0 and cap % COMBINE_WINDOW == 0 and seg0 % cap == 0:
            outs.append(_combine(y, tok[:, None, :], r[4].reshape(e, b, cap), seg0, x, gate, g))
        else:
            ids = r[1] - base
            part = y[:, seg0:seg0 + b * cap].astype(F32)
            f = jnp.zeros((b * n, d), F32).at[ids.reshape(-1)].add(part.reshape(-1, d))
            outs.append(_gated_residual(x, f.reshape(b, n, d), gate, g, n))
        seg0 += b * cap
    return outs


def kernel(x, c, ctx, c_ctx, w_mod, b_mod, norm_g, w_in_even, w_out_even, conv_w, conv_b, conv_ln_g,
           conv_ln_b, na_rpb, w_in_odd, w_out_odd, pool_w, pool_scale, gla_gate_w, gla_gate_b, gla_head_g,
           router_w, expert_w_gate, expert_w_up, expert_w_down):
    b, l, d = x.shape
    n_ctx = ctx.shape[1]
    tm = 512

    mod_rows = jnp.concatenate([c, c_ctx[None], jnp.zeros((8 - b - 1, d), F32)], axis=0)

    mod_all = _modulation(mod_rows, w_mod, b_mod)

    def modulation(i):
        mm = mod_all[i]
        m = mm[:b].reshape(b, 6, d)
        mc = jnp.broadcast_to(mm[b].reshape(1, 6, d), (b, 6, d))
        return m, mc

    m, mc = modulation(0)
    g = norm_g[0]
    w_in = w_in_even[0].astype(BF16)
    w_out = w_out_even[0].astype(BF16)
    glu, qkv = _even_in(x, g[0], m[:, 0], m[:, 1], w_in, tm)
    glu_c, qkv_c = _even_in(ctx, g[0], mc[:, 0], mc[:, 1], w_in, n_ctx)
    a_lat = _conv_branch(glu, conv_w[0], conv_b[0], conv_ln_g[0], conv_ln_b[0], 256)
    a_ctx = _conv_branch(glu_c, conv_w[0], conv_b[0], conv_ln_g[0], conv_ln_b[0], n_ctx)
    na = _neighbourhood_attention(qkv, qkv_c, na_rpb[0])
    att_c = _context_attention(qkv_c)
    x, h2, aff = _out_proj(a_lat, na, x, w_out, g[1], m[:, 2], g[2], m[:, 3], m[:, 4], router_w[0], tm)
    ctx, h2c, aff_c = _out_proj(a_ctx, att_c, ctx, w_out, g[1], mc[:, 2], g[2], mc[:, 3], mc[:, 4],
                                router_w[0], n_ctx)
    x, ctx = _moe([(aff, h2, x, m[:, 5]), (aff_c, h2c, ctx, mc[:, 5])], g[3],
                  expert_w_gate, expert_w_up, expert_w_down, 0)

    m, mc = modulation(1)
    g = norm_g[1]
    w_in = w_in_odd[0].astype(BF16)
    w_out = w_out_odd[0].astype(BF16)
    cos_t, sin_t = _rope_tables(l)
    ones_t = jnp.ones((n_ctx, GLA_HEADS * GLA_DK), F32)
    _, qk_c, v_c, _, gate_c = _odd_in(ctx, g[0], mc[:, 0], mc[:, 1], w_in, ones_t, jnp.zeros_like(ones_t),
                                      gla_gate_w[0], gla_gate_b[0], n_ctx)
    s_zero = jnp.zeros((b, GLA_HEADS, GLA_DK, GLA_DV), F32)
    _, _, s_f, s_b = _gla(qk_c, v_c, gate_c, s_zero, s_zero, n_ctx)
    pool_u, qk, v, r, gate = _odd_in(x, g[0], m[:, 0], m[:, 1], w_in, cos_t, sin_t,
                                     gla_gate_w[0], gla_gate_b[0], tm)
    o_f, o_b, _, _ = _gla(qk, v, gate, s_f, s_b, 256)
    pool_y, d_lat = _odd_mid(pool_u, o_f, o_b, r, gla_head_g[0], pool_w[0].astype(BF16), pool_scale[0], 256)
    x, h2, aff = _out_proj(pool_y, d_lat, x, w_out, g[1], m[:, 2], g[2], m[:, 3], m[:, 4], router_w[1], tm)
    (x,) = _moe([(aff, h2, x, m[:, 5])], g[3], expert_w_gate, expert_w_up, expert_w_down, 1)
    return x
```

```python
import functools

import jax
import jax.numpy as jnp
from jax import lax
from jax.experimental import pallas as pl
from jax.experimental.pallas import tpu as pltpu
from jax.experimental.pallas import tpu_sc as plsc

F32 = jnp.float32
BF16 = jnp.bfloat16
HIGHEST = lax.Precision.HIGHEST

D_MODEL = 1024
GRID_W = 64
EPS = 1e-6
CONV_CH = 512
CONV_WIDTH = 31
CONV_HALO = 16
NA_HEADS = 8
NA_HEAD_DIM = 64
NA_KR = 8
NA_KC = 16
NA_ROWS_PER_BLOCK = 4
NA_BLOCKS_PER_STEP = 2
NA_WIN_ROWS = 12
POOL_CH = 512
POOL_WINDOWS = (2, 4, 8, 16)
POOL_GROUP = 128
POOL_HALO = 8
GLA_HEADS = 4
GLA_DK = 64
GLA_DV = 128
GLA_RANK = 16
GLA_TAU = 16.0
GLA_CHUNK = 64
ROPE_BASE = 10000.0
N_EXPERTS = 16
EXPERT_FF = 2816
EC_CAPACITY_FACTOR = 2
LANES = 128
SUBLANES = 8
NEG_BIG = -1e30
VMEM_LIMIT = 56 * 1024 * 1024
SC_CORES = 2
SC_SUBCORES = 16
SC_LANES = 16
F32_INF_BITS = 0x7F800000
SC_TILE_VMEM_BUDGET = 400 * 1024
SC_GATHER_WINDOW = 128
COMBINE_TILE = 256
COMBINE_WINDOW = 256
COMBINE_FIRST = 64
BF16_ROWS = 16


def _cparams(sem):
    return pltpu.CompilerParams(dimension_semantics=sem, vmem_limit_bytes=VMEM_LIMIT)


def _rms(x, g):
    return x * lax.rsqrt(jnp.mean(x * x, axis=-1, keepdims=True) + EPS) * g


def _sigmoid(x):
    return 1.0 / (1.0 + jnp.exp(-x))


def _silu(x):
    return x * _sigmoid(x)


def _dot(a, b):
    return jnp.dot(a, b, preferred_element_type=F32)


def _pack_bf16_pairs(h):
    half = h.shape[-1] // 2
    bits = lax.bitcast_convert_type(h.astype(BF16).astype(F32), jnp.uint32)
    packed = (bits[:, half:] & jnp.uint32(0xFFFF0000)) | (bits[:, :half] >> 16)
    return lax.bitcast_convert_type(packed, jnp.int32)


def _unpack_bf16_pairs(p):
    bits = lax.bitcast_convert_type(p, jnp.uint32)
    lo = lax.bitcast_convert_type(bits << 16, F32).astype(BF16)
    hi = lax.bitcast_convert_type(bits & jnp.uint32(0xFFFF0000), F32).astype(BF16)
    return lo, hi


def _dot_nt(a, b):
    return lax.dot_general(a, b, (((1,), (1,)), ((), ())), preferred_element_type=F32)


def _mod_body(c_ref, w_ref, b_ref, o_ref):
    o_ref[0] = jnp.dot(_silu(c_ref[...]), w_ref[0], precision=HIGHEST,
                       preferred_element_type=F32) + b_ref[0]


def _modulation(rows, w, b):
    depth, _, n = w.shape
    tn = 1536
    return pl.pallas_call(
        _mod_body,
        grid=(depth, n // tn),
        in_specs=[pl.BlockSpec((8, D_MODEL), lambda i, j: (0, 0)),
                  pl.BlockSpec((1, D_MODEL, tn), lambda i, j: (i, 0, j)),
                  pl.BlockSpec((1, 1, tn), lambda i, j: (i, 0, j))],
        out_specs=pl.BlockSpec((1, 8, tn), lambda i, j: (i, 0, j)),
        out_shape=jax.ShapeDtypeStruct((depth, 8, n), F32),
        compiler_params=_cparams(("parallel", "parallel")),
        name="modulation",
    )(rows, w, b.reshape(depth, 1, n))


def _even_in_body(x_ref, g_ref, sh_ref, sc_ref, w_ref, glu_ref, qkv_ref):
    h = (_rms(x_ref[0], g_ref[...]) * (1.0 + sc_ref[0]) + sh_ref[0]).astype(BF16)
    c = CONV_CH
    glu_ref[0] = _dot(h, w_ref[:, 0:c]) * _sigmoid(_dot(h, w_ref[:, c:2 * c]))
    hd = NA_HEADS * NA_HEAD_DIM
    q0 = 2 * c
    qkv_ref[0, :, 0:hd] = (_dot(h, w_ref[:, q0:q0 + hd]) * (NA_HEAD_DIM ** -0.5)).astype(BF16)
    qkv_ref[0, :, hd:3 * hd] = _dot(h, w_ref[:, q0 + hd:q0 + 3 * hd]).astype(BF16)


def _even_in(x, g, shift, scale, w_bf, tm):
    b, l, d = x.shape
    n = w_bf.shape[1]
    hd3 = 3 * NA_HEADS * NA_HEAD_DIM
    vec = pl.BlockSpec((1, 1, d), lambda i, j: (i, 0, 0))
    return pl.pallas_call(
        _even_in_body,
        grid=(b, l // tm),
        in_specs=[pl.BlockSpec((1, tm, d), lambda i, j: (i, j, 0)),
                  pl.BlockSpec((1, d), lambda i, j: (0, 0)),
                  vec, vec,
                  pl.BlockSpec((d, n), lambda i, j: (0, 0))],
        out_specs=[pl.BlockSpec((1, tm, CONV_CH), lambda i, j: (i, j, 0)),
                   pl.BlockSpec((1, tm, hd3), lambda i, j: (i, j, 0))],
        out_shape=[jax.ShapeDtypeStruct((b, l, CONV_CH), F32),
                   jax.ShapeDtypeStruct((b, l, hd3), BF16)],
        compiler_params=_cparams(("parallel", "parallel")),
        name="even_in",
    )(x, g.reshape(1, d), shift.reshape(b, 1, d), scale.reshape(b, 1, d), w_bf)


def _conv_body(cur_ref, prev_ref, next_ref, w_ref, b_ref, lg_ref, lb_ref, o_ref, buf_ref, sh_ref, *, tile, chunk):
    j = pl.program_id(1)
    last = pl.num_programs(1) - 1
    hal = CONV_HALO
    buf_ref[0:hal, :] = jnp.where(j == 0, 0.0, prev_ref[0])
    buf_ref[hal:hal + tile, :] = cur_ref[0]
    buf_ref[hal + tile:hal + tile + hal, :] = jnp.where(j == last, 0.0, next_ref[0])
    span = sh_ref.shape[1]
    for s in range(SUBLANES):
        sh_ref[s] = buf_ref[s:s + span, :]
    first = hal - CONV_WIDTH // 2
    reps = chunk // SUBLANES

    def rows(c, carry):
        r0 = pl.multiple_of(c * chunk, chunk)
        accs = [jnp.zeros((chunk, CONV_CH), F32) for _ in range(2)]
        for k in range(CONV_WIDTH):
            a, s = divmod(first + k, SUBLANES)
            wk = jnp.concatenate([w_ref[k]] * reps, axis=0)
            accs[k % 2] = accs[k % 2] + sh_ref[s, pl.ds(r0 + a * SUBLANES, chunk), :] * wk
        o_ref[0, pl.ds(r0, chunk), :] = accs[0] + accs[1]
        return carry

    lax.fori_loop(0, tile // chunk, rows, 0)
    y = o_ref[0] + b_ref[...]
    mu = jnp.mean(y, axis=-1, keepdims=True)
    yc = y - mu
    var = jnp.mean(yc * yc, axis=-1, keepdims=True)
    o_ref[0] = _silu(yc * lax.rsqrt(var + EPS) * lg_ref[...] + lb_ref[...])


def _conv_branch(glu, conv_w, conv_b, ln_g, ln_b, tile):
    b, l, c = glu.shape
    hal = CONV_HALO
    per = tile // hal
    nh = l // hal
    vec = pl.BlockSpec((1, c), lambda i, j: (0, 0))
    return pl.pallas_call(
        functools.partial(_conv_body, tile=tile, chunk=32),
        grid=(b, l // tile),
        in_specs=[pl.BlockSpec((1, tile, c), lambda i, j: (i, j, 0)),
                  pl.BlockSpec((1, hal, c), lambda i, j: (i, jnp.maximum(j * per - 1, 0), 0)),
                  pl.BlockSpec((1, hal, c), lambda i, j: (i, jnp.minimum((j + 1) * per, nh - 1), 0)),
                  pl.BlockSpec((CONV_WIDTH, SUBLANES, c), lambda i, j: (0, 0, 0)),
                  vec, vec, vec],
        out_specs=pl.BlockSpec((1, tile, c), lambda i, j: (i, j, 0)),
        out_shape=jax.ShapeDtypeStruct((b, l, c), F32),
        scratch_shapes=[pltpu.VMEM((tile + 2 * hal, c), F32),
                        pltpu.VMEM((SUBLANES, tile + 2 * hal - SUBLANES, c), F32)],
        compiler_params=_cparams(("parallel", "parallel")),
        name="conv_branch",
    )(glu, glu, glu, jnp.broadcast_to(conv_w[:, None, :], (CONV_WIDTH, SUBLANES, c)),
      conv_b.reshape(1, c), ln_g.reshape(1, c), ln_b.reshape(1, c))


def _na_window_start(j, rows):
    rb = NA_ROWS_PER_BLOCK
    return jnp.clip(j * rb - NA_KR // 2, 0, rows - NA_WIN_ROWS)


def _na_body(q_ref, k_ref, v_ref, kc_ref, vc_ref, *rest, rows):
    tab_refs, o_ref = rest[:-1], rest[-1]
    j = pl.program_id(2)
    nkeys = NA_WIN_ROWS * GRID_W
    tq = NA_ROWS_PER_BLOCK * GRID_W
    kc = kc_ref[0]
    vc = vc_ref[0]
    lane = lax.broadcasted_iota(jnp.int32, (1, LANES), 1)
    for sb, tab_ref in enumerate(tab_refs):
        start = pl.multiple_of(_na_window_start(j * len(tab_refs) + sb, rows) * GRID_W, GRID_W)
        q = q_ref[0, sb * tq:(sb + 1) * tq, :]
        kw = k_ref[0, pl.ds(start, nkeys), :]
        vw = v_ref[0, pl.ds(start, nkeys), :]
        out = jnp.zeros(q.shape, F32)
        for hh in range(LANES // NA_HEAD_DIM):
            in_head = (lane >= hh * NA_HEAD_DIM) & (lane < (hh + 1) * NA_HEAD_DIM)
            qh = jnp.where(in_head, q, jnp.zeros_like(q))
            s = _dot_nt(qh, kw) + tab_ref[0, hh]
            sc = _dot_nt(qh, kc)
            m = jnp.maximum(jnp.max(s, axis=-1, keepdims=True), jnp.max(sc, axis=-1, keepdims=True))
            p = jnp.exp(s - m)
            pc = jnp.exp(sc - m)
            denom = jnp.sum(p, axis=-1, keepdims=True) + jnp.sum(pc, axis=-1, keepdims=True)
            o = (_dot(p.astype(BF16), vw) + _dot(pc.astype(BF16), vc)) / denom
            out = jnp.where(in_head, o, out)
        o_ref[0, sb * tq:(sb + 1) * tq, :] = out


def _na_tables(rpb, rows):
    rb = NA_ROWS_PER_BLOCK
    nblk = rows // rb
    wr = NA_WIN_ROWS
    qc = jnp.arange(GRID_W)
    cs = jnp.clip(qc - NA_KC // 2, 0, GRID_W - NA_KC)
    col_ok = (qc[None, :] >= cs[:, None]) & (qc[None, :] < cs[:, None] + NA_KC)
    col_off = qc[None, :] - qc[:, None] + NA_KC - 1
    onehot = (col_off[:, :, None] == jnp.arange(2 * NA_KC - 1)[None, None, :]).astype(F32)
    blocks = jnp.einsum('hrd,qkd->hrqk', rpb.astype(F32), onehot, precision=HIGHEST)
    blocks = jnp.where(col_ok[None, None], blocks, NEG_BIG)
    masked = jnp.full((NA_HEADS, GRID_W, GRID_W), NEG_BIG, F32)
    tabs = []
    for jb in (0, 1, nblk - 1):
        ws = min(max(jb * rb - NA_KR // 2, 0), rows - wr)
        q_rows = []
        for qr in range(jb * rb, (jb + 1) * rb):
            rs = min(max(qr - NA_KR // 2, 0), rows - NA_KR)
            row = [blocks[:, kr - qr + NA_KR - 1] if rs <= kr < rs + NA_KR else masked
                   for kr in range(ws, ws + wr)]
            q_rows.append(jnp.concatenate(row, axis=-1))
        tabs.append(jnp.concatenate(q_rows, axis=1))
    return jnp.stack(tabs)


def _neighbourhood_attention(qkv, qkv_ctx, rpb):
    b, l, _ = qkv.shape
    n_ctx = qkv_ctx.shape[1]
    rows = l // GRID_W
    rb = NA_ROWS_PER_BLOCK
    nblk = rows // rb
    tq = rb * GRID_W
    nkeys = NA_WIN_ROWS * GRID_W
    hp = NA_HEADS * NA_HEAD_DIM // LANES
    tabs = _na_tables(rpb, rows)

    per = NA_BLOCKS_PER_STEP

    def cls(jb):
        return jnp.where(jb == 0, 0, jnp.where(jb == nblk - 1, 2, 1))

    def tab_spec(sb):
        return pl.BlockSpec((1, 2, tq, nkeys), lambda i, h, j: (cls(j * per + sb), h, 0, 0))

    return pl.pallas_call(
        functools.partial(_na_body, rows=rows),
        grid=(b, hp, nblk // per),
        in_specs=[pl.BlockSpec((1, per * tq, LANES), lambda i, h, j: (i, j, h)),
                  pl.BlockSpec((1, l, LANES), lambda i, h, j: (i, 0, hp + h)),
                  pl.BlockSpec((1, l, LANES), lambda i, h, j: (i, 0, 2 * hp + h)),
                  pl.BlockSpec((1, n_ctx, LANES), lambda i, h, j: (i, 0, hp + h)),
                  pl.BlockSpec((1, n_ctx, LANES), lambda i, h, j: (i, 0, 2 * hp + h))]
                 + [tab_spec(sb) for sb in range(per)],
        out_specs=pl.BlockSpec((1, per * tq, LANES), lambda i, h, j: (i, j, h)),
        out_shape=jax.ShapeDtypeStruct((b, l, NA_HEADS * NA_HEAD_DIM), F32),
        compiler_params=_cparams(("parallel", "parallel", "arbitrary")),
        name="neighbourhood_attention",
    )(qkv, qkv, qkv, qkv_ctx, qkv_ctx, *([tabs] * per))


def _ctx_attn_body(q_ref, k_ref, v_ref, o_ref):
    q = q_ref[0]
    k = k_ref[0]
    v = v_ref[0]
    lane = lax.broadcasted_iota(jnp.int32, (1, LANES), 1)
    out = jnp.zeros(q.shape, F32)
    for hh in range(LANES // NA_HEAD_DIM):
        in_head = (lane >= hh * NA_HEAD_DIM) & (lane < (hh + 1) * NA_HEAD_DIM)
        qh = jnp.where(in_head, q, jnp.zeros_like(q))
        s = _dot_nt(qh, k)
        p = jnp.exp(s - jnp.max(s, axis=-1, keepdims=True))
        o = _dot(p.astype(BF16), v) / jnp.sum(p, axis=-1, keepdims=True)
        out = jnp.where(in_head, o, out)
    o_ref[0] = out


def _context_attention(qkv_ctx):
    b, n, _ = qkv_ctx.shape
    hp = NA_HEADS * NA_HEAD_DIM // LANES
    return pl.pallas_call(
        _ctx_attn_body,
        grid=(b, hp),
        in_specs=[pl.BlockSpec((1, n, LANES), lambda i, h: (i, 0, h)),
                  pl.BlockSpec((1, n, LANES), lambda i, h: (i, 0, hp + h)),
                  pl.BlockSpec((1, n, LANES), lambda i, h: (i, 0, 2 * hp + h))],
        out_specs=pl.BlockSpec((1, n, LANES), lambda i, h: (i, 0, h)),
        out_shape=jax.ShapeDtypeStruct((b, n, NA_HEADS * NA_HEAD_DIM), F32),
        compiler_params=_cparams(("parallel", "parallel")),
        name="context_attention",
    )(qkv_ctx, qkv_ctx, qkv_ctx)


def _out_body(a_ref, b_ref, x_ref, w_ref, g1_ref, gate_ref, g2_ref, sh_ref, sc_ref, rwh_ref, rwl_ref,
              xo_ref, h_ref, aff_ref):
    half = a_ref.shape[-1]
    y = _dot(a_ref[0].astype(BF16), w_ref[0:half, :]) + _dot(b_ref[0].astype(BF16), w_ref[half:2 * half, :])
    xn = x_ref[0] + gate_ref[0] * _rms(y, g1_ref[...])
    xo_ref[0] = xn
    h = _rms(xn, g2_ref[...]) * (1.0 + sc_ref[0]) + sh_ref[0]
    packed = _pack_bf16_pairs(h)
    quarter = packed.shape[-1] // 2
    h_ref[0, 0] = packed[:, 0:quarter]
    h_ref[0, 1] = packed[:, quarter:2 * quarter]
    h_hi = h.astype(BF16)
    h_lo = (h - h_hi.astype(F32)).astype(BF16)
    logits = _dot(h_hi, rwh_ref[...]) + (_dot(h_lo, rwh_ref[...]) + _dot(h_hi, rwl_ref[...]))
    lane = lax.broadcasted_iota(jnp.int32, (1, LANES), 1)
    logits = jnp.where(lane < N_EXPERTS, logits, NEG_BIG)
    e = jnp.exp(logits - jnp.max(logits, axis=-1, keepdims=True))
    aff = e / jnp.sum(e, axis=-1, keepdims=True)
    aff_ref[0] = aff.T[0:N_EXPERTS, :]


def _out_proj(a, b2, x, w_bf, g1, gate, g2, shift, scale, router_w, tm):
    b, l, d = x.shape
    half = a.shape[-1]
    rw = jnp.pad(router_w, ((0, 0), (0, LANES - N_EXPERTS)))
    rw_hi = rw.astype(BF16)
    rw_lo = (rw - rw_hi.astype(F32)).astype(BF16)
    vec = pl.BlockSpec((1, d), lambda i, j: (0, 0))
    bvec = pl.BlockSpec((1, 1, d), lambda i, j: (i, 0, 0))
    rspec = pl.BlockSpec((d, LANES), lambda i, j: (0, 0))
    return pl.pallas_call(
        _out_body,
        grid=(b, l // tm),
        in_specs=[pl.BlockSpec((1, tm, half), lambda i, j: (i, j, 0)),
                  pl.BlockSpec((1, tm, half), lambda i, j: (i, j, 0)),
                  pl.BlockSpec((1, tm, d), lambda i, j: (i, j, 0)),
                  pl.BlockSpec((2 * half, d), lambda i, j: (0, 0)),
                  vec, bvec, vec, bvec, bvec, rspec, rspec],
        out_specs=[pl.BlockSpec((1, tm, d), lambda i, j: (i, j, 0)),
                   pl.BlockSpec((1, 2, tm, d // 4), lambda i, j: (i, 0, j, 0)),
                   pl.BlockSpec((1, N_EXPERTS, tm), lambda i, j: (i, 0, j))],
        out_shape=[jax.ShapeDtypeStruct((b, l, d), F32),
                   jax.ShapeDtypeStruct((b, 2, l, d // 4), jnp.int32),
                   jax.ShapeDtypeStruct((b, N_EXPERTS, l), F32)],
        compiler_params=_cparams(("parallel", "parallel")),
        name="out_proj",
    )(a, b2, x, w_bf, g1.reshape(1, d), gate.reshape(b, 1, d), g2.reshape(1, d),
      shift.reshape(b, 1, d), scale.reshape(b, 1, d), rw_hi, rw_lo)


def _moe_body(x_ref, val_ref, wg_ref, wu_ref, wd_ref, o_ref, acc_s, x_s, wg_s, wu_s, wd_s, *, chunk):
    f = pl.program_id(1)
    m = x_ref.shape[2]
    quarter = x_ref.shape[3]

    @pl.when(f == 0)
    def _():
        def unpack(c, carry):
            r = pl.multiple_of(c * chunk, chunk)
            for s in range(2):
                lo, hi = _unpack_bf16_pairs(x_ref[0, s, pl.ds(r, chunk), :])
                x_s[pl.ds(r, chunk), s * quarter:(s + 1) * quarter] = lo
                x_s[pl.ds(r, chunk), (2 + s) * quarter:(3 + s) * quarter] = hi
            acc_s[pl.ds(r, chunk), :] = jnp.zeros((chunk, 4 * quarter), F32)
            return carry

        lax.fori_loop(0, m // chunk, unpack, 0)

    wg_s[...] = wg_ref[0, 0].astype(BF16)
    wu_s[...] = wu_ref[0, 0].astype(BF16)
    wd_s[...] = wd_ref[0, 0].astype(BF16)

    def rows(c, carry):
        r = pl.multiple_of(c * chunk, chunk)
        xs = x_s[pl.ds(r, chunk), :]
        hid = (_silu(_dot(xs, wg_s[...])) * _dot(xs, wu_s[...])).astype(BF16)
        acc_s[pl.ds(r, chunk), :] += _dot(hid, wd_s[...])
        return carry

    lax.fori_loop(0, m // chunk, rows, 0, unroll=True)

    @pl.when(f == pl.num_programs(1) - 1)
    def _():
        o_ref[0] = (acc_s[...] * val_ref[0]).astype(o_ref.dtype)


def _expert_ffn(xg, vals, w_gate, w_up, w_down, layer, chunk):
    e, _, m, quarter = xg.shape
    d = 4 * quarter
    ff = w_gate.shape[-1]
    tf = 256
    return pl.pallas_call(
        functools.partial(_moe_body, chunk=chunk),
        grid=(e, ff // tf),
        in_specs=[pl.BlockSpec((1, 2, m, quarter), lambda i, f: (i, 0, 0, 0)),
                  pl.BlockSpec((1, m, 1), lambda i, f: (i, 0, 0)),
                  pl.BlockSpec((1, 1, d, tf), lambda i, f: (layer, i, 0, f)),
                  pl.BlockSpec((1, 1, d, tf), lambda i, f: (layer, i, 0, f)),
                  pl.BlockSpec((1, 1, tf, d), lambda i, f: (layer, i, f, 0))],
        out_specs=pl.BlockSpec((1, m, d), lambda i, f: (i, 0, 0)),
        out_shape=jax.ShapeDtypeStruct((e, m, d), BF16),
        scratch_shapes=[pltpu.VMEM((m, d), F32), pltpu.VMEM((m, d), BF16), pltpu.VMEM((d, tf), BF16),
                        pltpu.VMEM((d, tf), BF16), pltpu.VMEM((tf, d), BF16)],
        compiler_params=_cparams(("parallel", "arbitrary")),
        name="expert_ffn",
    )(xg, vals, w_gate, w_up, w_down)


def _resid_body(x_ref, f_ref, gate_ref, g_ref, o_ref):
    o_ref[0] = x_ref[0] + gate_ref[0] * _rms(f_ref[0], g_ref[...])


def _gated_residual(x, f, gate, g, tm):
    b, l, d = x.shape
    blk = pl.BlockSpec((1, tm, d), lambda i, j: (i, j, 0))
    return pl.pallas_call(
        _resid_body,
        grid=(b, l // tm),
        in_specs=[blk, blk, pl.BlockSpec((1, 1, d), lambda i, j: (i, 0, 0)),
                  pl.BlockSpec((1, d), lambda i, j: (0, 0))],
        out_specs=blk,
        out_shape=jax.ShapeDtypeStruct((b, l, d), F32),
        compiler_params=_cparams(("parallel", "parallel")),
        name="gated_residual",
    )(x, f, gate.reshape(b, 1, d), g.reshape(1, d))


def _swap_pairs(x):
    nf = GLA_DK // 4
    lane = lax.broadcasted_iota(jnp.int32, (1, LANES), 1)
    up = pltpu.roll(x, LANES - nf, 1)
    down = pltpu.roll(x, nf, 1)
    return jnp.where(lane % (2 * nf) < nf, up, down)


def _odd_in_body(x_ref, g_ref, sh_ref, sc_ref, w_ref, cos_ref, sin_ref, gw_ref, gb_ref,
                 pool_ref, qk_ref, v_ref, r_ref, gate_ref):
    h = (_rms(x_ref[0], g_ref[...]) * (1.0 + sc_ref[0]) + sh_ref[0]).astype(BF16)
    qk = GLA_HEADS * GLA_DK
    vd = GLA_HEADS * GLA_DV
    q0 = POOL_CH
    v0 = q0 + 2 * qk
    r0 = v0 + vd
    l0 = r0 + vd
    pool_ref[0] = _dot(h, w_ref[:, 0:q0])
    for s in range(2 * qk // LANES):
        raw = _dot(h, w_ref[:, q0 + s * LANES:q0 + (s + 1) * LANES])
        c = cos_ref[:, (s * LANES) % qk:(s * LANES) % qk + LANES]
        sn = sin_ref[:, (s * LANES) % qk:(s * LANES) % qk + LANES]
        rot = raw * c + _swap_pairs(raw) * sn
        if s * LANES < qk:
            rot = rot * (GLA_DK ** -0.5)
        qk_ref[0, :, s * LANES:(s + 1) * LANES] = rot
    v_ref[0] = _dot(h, w_ref[:, v0:r0]).astype(BF16)
    r_ref[0] = _dot(h, w_ref[:, r0:l0])
    lr = _dot(h, w_ref[:, l0:l0 + 2 * GLA_RANK])
    z = jnp.dot(lr, gw_ref[...], precision=HIGHEST, preferred_element_type=F32) + gb_ref[...]
    gate_ref[0] = (jnp.minimum(z, 0.0) - jnp.log1p(jnp.exp(-jnp.abs(z)))) * (1.0 / GLA_TAU)


def _odd_in(x, g, shift, scale, w_bf, cos_t, sin_t, gate_w, gate_b, tm):
    b, l, d = x.shape
    n = w_bf.shape[1]
    qk = GLA_HEADS * GLA_DK
    vd = GLA_HEADS * GLA_DV
    gw = jnp.zeros((2 * GLA_RANK, 2 * qk), F32)
    gw = gw.at[:GLA_RANK, :qk].set(gate_w[0]).at[GLA_RANK:, qk:].set(gate_w[1])
    gb = jnp.concatenate([gate_b[0], gate_b[1]]).reshape(1, 2 * qk)
    vec = pl.BlockSpec((1, 1, d), lambda i, j: (i, 0, 0))
    row = lambda w: pl.BlockSpec((1, tm, w), lambda i, j: (i, j, 0))
    return pl.pallas_call(
        _odd_in_body,
        grid=(b, l // tm),
        in_specs=[row(d), pl.BlockSpec((1, d), lambda i, j: (0, 0)), vec, vec,
                  pl.BlockSpec((d, n), lambda i, j: (0, 0)),
                  pl.BlockSpec((tm, qk), lambda i, j: (j, 0)),
                  pl.BlockSpec((tm, qk), lambda i, j: (j, 0)),
                  pl.BlockSpec((2 * GLA_RANK, 2 * qk), lambda i, j: (0, 0)),
                  pl.BlockSpec((1, 2 * qk), lambda i, j: (0, 0))],
        out_specs=[row(POOL_CH), row(2 * qk), row(vd), row(vd), row(2 * qk)],
        out_shape=[jax.ShapeDtypeStruct((b, l, POOL_CH), F32),
                   jax.ShapeDtypeStruct((b, l, 2 * qk), F32),
                   jax.ShapeDtypeStruct((b, l, vd), BF16),
                   jax.ShapeDtypeStruct((b, l, vd), F32),
                   jax.ShapeDtypeStruct((b, l, 2 * qk), F32)],
        compiler_params=_cparams(("parallel", "parallel")),
        name="odd_in",
    )(x, g.reshape(1, d), shift.reshape(b, 1, d), scale.reshape(b, 1, d), w_bf, cos_t, sin_t, gw, gb)


def _rope_tables(l):
    t = jnp.arange(l)
    pos_r = (t // GRID_W).astype(F32)
    pos_c = (t % GRID_W).astype(F32)
    nf = GLA_DK // 4
    inv = jnp.power(ROPE_BASE, -jnp.arange(nf, dtype=F32) / nf)
    ar = pos_r[:, None] * inv[None, :]
    ac = pos_c[:, None] * inv[None, :]
    cos_h = jnp.concatenate([jnp.cos(ar), jnp.cos(ar), jnp.cos(ac), jnp.cos(ac)], axis=-1)
    sin_h = jnp.concatenate([-jnp.sin(ar), jnp.sin(ar), -jnp.sin(ac), jnp.sin(ac)], axis=-1)
    return jnp.tile(cos_h, (1, GLA_HEADS)), jnp.tile(sin_h, (1, GLA_HEADS))


def _gla_tile(qk, v, g, s, reverse):
    hk = GLA_HEADS * GLA_DK
    hv = GLA_HEADS * GLA_DV
    c = GLA_CHUNK
    t = qk.shape[0]
    n = t // c
    last_row, mid_row = (0, c // 2) if reverse else (c - 1, c // 2 - 1)
    ii = lax.broadcasted_iota(jnp.int32, (t, t), 0)
    jj = lax.broadcasted_iota(jnp.int32, (t, t), 1)
    ordered = (jj >= ii) if reverse else (jj <= ii)
    tri = jnp.where(ordered & (ii // c == jj // c), 1.0, 0.0).astype(BF16)
    g_hi = g.astype(BF16)
    rem = g - g_hi.astype(F32)
    g_mid = rem.astype(BF16)
    g_lo = (rem - g_mid.astype(F32)).astype(BF16)
    bc = _dot(tri, g_hi) + (_dot(tri, g_mid) + _dot(tri, g_lo))
    spread = lambda row: jnp.concatenate(
        [jnp.broadcast_to(bc[i * c + row:i * c + row + 1, :], (c, hk)) for i in range(n)], axis=0)
    b_mid = spread(mid_row)
    b_last = spread(last_row)
    qt = qk[:, 0:hk] * jnp.exp(bc - b_mid)
    kt = qk[:, hk:2 * hk] * jnp.exp(b_mid - bc)
    qe = (qt * jnp.exp(b_mid)).astype(BF16)
    ke = kt * jnp.exp(b_last - b_mid)
    ktb = kt.astype(BF16)
    lane = lax.broadcasted_iota(jnp.int32, (1, hk), 1)
    ci = lax.broadcasted_iota(jnp.int32, (c, c), 0)
    cj = lax.broadcasted_iota(jnp.int32, (c, c), 1)
    causal = (cj >= ci) if reverse else (cj <= ci)
    blockdiag = (lax.broadcasted_iota(jnp.int32, (hk, hv), 0) // GLA_DK
                 == lax.broadcasted_iota(jnp.int32, (hk, hv), 1) // GLA_DV)
    intra, upd, decay = [], [], []
    for i in range(n):
        rows = slice(i * c, (i + 1) * c)
        qs = jnp.concatenate(
            [jnp.where((lane >= h * GLA_DK) & (lane < (h + 1) * GLA_DK), qt[rows], 0.0) for h in range(GLA_HEADS)],
            axis=0).astype(BF16)
        att = _dot_nt(qs, ktb[rows])
        intra.append(jnp.concatenate(
            [_dot(jnp.where(causal, att[h * c:(h + 1) * c], 0.0).astype(BF16),
                  v[rows, h * GLA_DV:(h + 1) * GLA_DV]) for h in range(GLA_HEADS)], axis=-1))
        upd.append(jnp.where(blockdiag, _dot(ke[rows].T.astype(BF16), v[rows]), 0.0))
        decay.append(jnp.exp(jnp.sum(g[rows].T, axis=1, keepdims=True)))
    outs = [None] * n
    for i in (reversed(range(n)) if reverse else range(n)):
        rows = slice(i * c, (i + 1) * c)
        outs[i] = _dot(qe[rows], s.astype(BF16)) + intra[i]
        s = decay[i] * s + upd[i]
    return jnp.concatenate(outs, axis=0), s


def _gla_body(qkf_ref, qkb_ref, vf_ref, vb_ref, gf_ref, gb_ref, s0f_ref, s0b_ref,
              of_ref, ob_ref, sff_ref, sbf_ref, sf_ref, sb_ref, *, tile):
    n = pl.program_id(1)
    hk = GLA_HEADS * GLA_DK
    hv = GLA_HEADS * GLA_DV

    @pl.when(n == 0)
    def _():
        sf_ref[...] = jnp.zeros((hk, hv), F32)
        sb_ref[...] = jnp.zeros((hk, hv), F32)
        for h in range(GLA_HEADS):
            sf_ref[h * GLA_DK:(h + 1) * GLA_DK, h * GLA_DV:(h + 1) * GLA_DV] = s0f_ref[0, h]
            sb_ref[h * GLA_DK:(h + 1) * GLA_DK, h * GLA_DV:(h + 1) * GLA_DV] = s0b_ref[0, h]

    of_ref[0], sf_ref[...] = _gla_tile(qkf_ref[0], vf_ref[0], gf_ref[0], sf_ref[...], False)
    ob_ref[0], sb_ref[...] = _gla_tile(qkb_ref[0], vb_ref[0], gb_ref[0], sb_ref[...], True)

    @pl.when(n == pl.num_programs(1) - 1)
    def _():
        for h in range(GLA_HEADS):
            sff_ref[0, h] = sf_ref[h * GLA_DK:(h + 1) * GLA_DK, h * GLA_DV:(h + 1) * GLA_DV]
            sbf_ref[0, h] = sb_ref[h * GLA_DK:(h + 1) * GLA_DK, h * GLA_DV:(h + 1) * GLA_DV]


def _gla(qk, v, gates, s0f, s0b, tile):
    b, l, _ = qk.shape
    hk = GLA_HEADS * GLA_DK
    hv = GLA_HEADS * GLA_DV
    nt = l // tile
    fwd = lambda w, col: pl.BlockSpec((1, tile, w), lambda i, n: (i, n, col))
    bwd = lambda w, col: pl.BlockSpec((1, tile, w), lambda i, n: (i, nt - 1 - n, col))
    st = pl.BlockSpec((1, GLA_HEADS, GLA_DK, GLA_DV), lambda i, n: (i, 0, 0, 0))
    return pl.pallas_call(
        functools.partial(_gla_body, tile=tile),
        grid=(b, nt),
        in_specs=[fwd(2 * hk, 0), bwd(2 * hk, 0), fwd(hv, 0), bwd(hv, 0), fwd(hk, 0), bwd(hk, 1), st, st],
        out_specs=[fwd(hv, 0), bwd(hv, 0), st, st],
        out_shape=[jax.ShapeDtypeStruct((b, l, hv), F32), jax.ShapeDtypeStruct((b, l, hv), F32),
                   jax.ShapeDtypeStruct((b, GLA_HEADS, GLA_DK, GLA_DV), F32),
                   jax.ShapeDtypeStruct((b, GLA_HEADS, GLA_DK, GLA_DV), F32)],
        scratch_shapes=[pltpu.VMEM((hk, hv), F32), pltpu.VMEM((hk, hv), F32)],
        compiler_params=_cparams(("parallel", "arbitrary")),
        name="gla_scan",
    )(qk, qk, v, v, gates, gates, s0f, s0b)


def _odd_mid_body(cur_ref, prev_ref, next_ref, of_ref, ob_ref, r_ref, hg_ref, pw_ref, ps_ref,
                  pool_ref, d_ref, buf_ref, *, tile, seq):
    j = pl.program_id(1)
    last = pl.num_programs(1) - 1
    hal = POOL_HALO
    buf_ref[0:hal, :] = jnp.where(j == 0, 0.0, prev_ref[0])
    buf_ref[hal:hal + tile, :] = cur_ref[0]
    buf_ref[hal + tile:hal + tile + hal, :] = jnp.where(j == last, 0.0, next_ref[0])
    t = j * tile + lax.broadcasted_iota(jnp.int32, (tile, 1), 0)
    for gi, win in enumerate(POOL_WINDOWS):
        cols = slice(gi * POOL_GROUP, (gi + 1) * POOL_GROUP)
        acc = jnp.zeros((tile, POOL_GROUP), F32)
        for off in range(-(win // 2), win - win // 2):
            acc = acc + buf_ref[hal + off:hal + off + tile, cols]
        cnt = jnp.minimum(t + (win - win // 2), seq) - jnp.maximum(t - win // 2, 0)
        diff = acc / cnt.astype(F32) - cur_ref[0, :, cols]
        pool_ref[0, :, cols] = _dot(diff.astype(BF16), pw_ref[gi]) * ps_ref[:, cols]
    for h in range(GLA_HEADS):
        cols = slice(h * GLA_DV, (h + 1) * GLA_DV)
        o = of_ref[0, :, cols] + ob_ref[0, :, cols]
        d_ref[0, :, cols] = _rms(o, hg_ref[:, cols]) * _silu(r_ref[0, :, cols])


def _odd_mid(pool_u, o_f, o_b, r, head_g, pool_w_bf, pool_scale, tile):
    b, l, c = pool_u.shape
    hal = POOL_HALO
    per = tile // hal
    nh = l // hal
    blk = pl.BlockSpec((1, tile, c), lambda i, j: (i, j, 0))
    vec = pl.BlockSpec((1, c), lambda i, j: (0, 0))
    return pl.pallas_call(
        functools.partial(_odd_mid_body, tile=tile, seq=l),
        grid=(b, l // tile),
        in_specs=[blk,
                  pl.BlockSpec((1, hal, c), lambda i, j: (i, jnp.maximum(j * per - 1, 0), 0)),
                  pl.BlockSpec((1, hal, c), lambda i, j: (i, jnp.minimum((j + 1) * per, nh - 1), 0)),
                  blk, blk, blk, vec,
                  pl.BlockSpec((len(POOL_WINDOWS), POOL_GROUP, POOL_GROUP), lambda i, j: (0, 0, 0)),
                  vec],
        out_specs=[blk, blk],
        out_shape=[jax.ShapeDtypeStruct((b, l, c), F32), jax.ShapeDtypeStruct((b, l, c), F32)],
        scratch_shapes=[pltpu.VMEM((tile + 2 * hal, c), F32)],
        compiler_params=_cparams(("parallel", "parallel")),
        name="odd_mid",
    )(pool_u, pool_u, pool_u, o_f, o_b, r, head_g.reshape(1, c), pool_w_bf, pool_scale.reshape(1, c))


def _threshold_body(a_ref, thr_ref, need_ref, *, cap):
    bits = lax.bitcast_convert_type(a_ref[0], jnp.int32)
    rows = bits.shape[0]
    count_ge = lambda v: jnp.sum(jnp.where(bits >= v, 1.0, 0.0), axis=-1, keepdims=True)

    def step(_, carry):
        lo, hi = carry
        mid = lo + ((hi - lo + 1) >> 1)
        ok = count_ge(mid) >= cap
        return jnp.where(ok, mid, lo), jnp.where(ok, hi, mid - 1)

    lo, _ = lax.fori_loop(0, 31, step, (jnp.zeros((rows, 1), jnp.int32),
                                        jnp.full((rows, 1), F32_INF_BITS, jnp.int32)))
    above = jnp.sum(jnp.where(bits > lo, 1.0, 0.0), axis=-1, keepdims=True)
    width = thr_ref.shape[-1]
    thr_ref[0] = jnp.broadcast_to(lax.bitcast_convert_type(lo, F32), (rows, width))
    need_ref[0] = jnp.broadcast_to(cap - above.astype(jnp.int32), (rows, width))


def _route_threshold(aff_t, cap):
    b, e, n = aff_t.shape
    out = pl.BlockSpec((1, e, SC_LANES), lambda i: (i, 0, 0))
    return pl.pallas_call(
        functools.partial(_threshold_body, cap=cap),
        grid=(b,),
        in_specs=[pl.BlockSpec((1, e, n), lambda i: (i, 0, 0))],
        out_specs=[out, out],
        out_shape=[jax.ShapeDtypeStruct((b, e, SC_LANES), F32), jax.ShapeDtypeStruct((b, e, SC_LANES), jnp.int32)],
        compiler_params=_cparams(("parallel",)),
        name="route_threshold",
    )(aff_t)


def _route_compact(aff, thr, need, cap):
    r, n = aff.shape
    lanes = SC_LANES
    assert r == SC_CORES * SC_SUBCORES and n % lanes == 0
    mesh = plsc.VectorSubcoreMesh(core_axis_name="core", subcore_axis_name="subcore",
                                  num_cores=SC_CORES, num_subcores=SC_SUBCORES)

    @pl.kernel(out_type=[jax.ShapeDtypeStruct((r, cap), jnp.int32), jax.ShapeDtypeStruct((r, cap), F32)],
               mesh=mesh,
               scratch_types=[pltpu.VMEM((n,), F32), pltpu.VMEM((lanes,), F32), pltpu.VMEM((lanes,), jnp.int32),
                              pltpu.VMEM((cap,), jnp.int32), pltpu.VMEM((cap,), F32)],
               compiler_params=pltpu.CompilerParams(needs_layout_passes=False),
               name="route_compact")
    def compact(aff_hbm, thr_hbm, need_hbm, idx_hbm, val_hbm, row_v, thr_v, need_v, idx_v, val_v):
        w = lax.axis_index("subcore") * SC_CORES + lax.axis_index("core")
        pltpu.sync_copy(aff_hbm.at[w], row_v)
        pltpu.sync_copy(thr_hbm.at[w], thr_v)
        pltpu.sync_copy(need_hbm.at[w], need_v)
        thr = thr_v[...]
        need = need_v[...]
        lane = lax.iota(jnp.int32, lanes)
        ones = jnp.ones((lanes,), jnp.int32)

        def body(i, carry):
            n_out, n_eq = carry
            x = row_v[pl.ds(i * lanes, lanes)]
            eq = x == thr
            take = (x > thr) | (eq & (n_eq + plsc.cumsum(ones, mask=eq) <= need))
            pos = n_out + plsc.cumsum(ones, mask=take) - 1
            take = take & (pos < cap)
            plsc.store_scatter(idx_v, [pos], lane + i * lanes, mask=take)
            plsc.store_scatter(val_v, [pos], x, mask=take)
            return (n_out + plsc.all_reduce_population_count(take),
                    n_eq + plsc.all_reduce_population_count(eq))

        zero = jnp.zeros((lanes,), jnp.int32)
        lax.fori_loop(0, n // lanes, body, (zero, zero))
        pltpu.sync_copy(idx_v, idx_hbm.at[w])
        pltpu.sync_copy(val_v, val_hbm.at[w])

    return compact(aff, thr, need)


def _route(aff_t, tok_base, row_base):
    b, e, n = aff_t.shape
    cap = EC_CAPACITY_FACTOR * n // N_EXPERTS
    thr, need = _route_threshold(aff_t, cap)
    idx, vals = _route_compact(aff_t.reshape(b * e, n), thr.reshape(b * e, -1), need.reshape(b * e, -1), cap)
    idx = idx.reshape(b, e, cap)
    vals = vals.reshape(b, e, cap)
    bi = jnp.arange(b, dtype=idx.dtype)[:, None, None]
    per_expert = lambda a: jnp.swapaxes(a, 0, 1).reshape(e, b * cap)
    rows0 = idx + row_base + 2 * bi * n
    return (per_expert(vals), per_expert(idx + tok_base + bi * n), per_expert(rows0), per_expert(rows0 + n),
            per_expert(idx))


def _combine_first_start(p0, cap):
    return jnp.minimum((p0 // BF16_ROWS) * BF16_ROWS, cap - COMBINE_FIRST)


def _combine_body(offs_ref, spill_ref, tok_ref, y_ref, x_ref, gate_ref, g_ref, o_ref, f_s, *, n_tok, cap):
    bi = pl.program_id(0)
    j = pl.program_id(1)
    tt = COMBINE_TILE
    wf = COMBINE_FIRST
    wn = COMBINE_WINDOW
    ntiles = n_tok // tt
    n_exp = y_ref.shape[0]
    group = wn // wf
    sub = lax.broadcasted_iota(jnp.int32, (tt, 1), 0)
    lane = lax.broadcasted_iota(jnp.int32, (1, wn), 1)

    def slot_range(e):
        base = (bi * n_exp + e) * (ntiles + 1) + j
        return offs_ref[base], offs_ref[base + 1]

    acc = jnp.zeros(f_s.shape, F32)
    for g in range(n_exp // group):
        ys = []
        toks = jnp.full((1, wn), -1, jnp.int32)
        for k in range(group):
            e = g * group + k
            start = pl.multiple_of(_combine_first_start(slot_range(e)[0], cap), BF16_ROWS)
            ys.append(y_ref[e, pl.ds(start, wf), :])
            cs = pl.multiple_of(jnp.minimum((start // LANES) * LANES, cap - wn), LANES)
            rolled = pltpu.roll(tok_ref[e, :, pl.ds(cs, wn)], (k * wf + wn - (start - cs)) % wn, 1)
            toks = jnp.where((lane >= k * wf) & (lane < (k + 1) * wf), rolled, toks)
        hit = (toks - j * tt) == sub
        acc = acc + _dot(jnp.where(hit, 1.0, 0.0).astype(BF16), jnp.concatenate(ys, axis=0))
    f_s[...] = acc

    def more_windows(e, carry):
        p0, p1 = slot_range(e)
        lo = _combine_first_start(p0, cap) + wf
        first = (lo // LANES) * LANES

        def extra(w, carry):
            cs = pl.multiple_of(jnp.minimum(first + w * wn, cap - wn), LANES)
            tok = tok_ref[e, :, pl.ds(cs, wn)] - j * tt
            hit = (tok == sub) & (cs + lane >= jnp.maximum(lo, first + w * wn))
            f_s[...] += _dot(jnp.where(hit, 1.0, 0.0).astype(BF16), y_ref[e, pl.ds(cs, wn), :])
            return carry

        lax.fori_loop(0, (jnp.maximum(p1 - first, 0) + wn - 1) // wn * (p1 > lo).astype(jnp.int32), extra, 0)
        return carry

    @pl.when(spill_ref[bi * ntiles + j] != 0)
    def _():
        lax.fori_loop(0, n_exp, more_windows, 0)

    o_ref[0] = x_ref[0] + gate_ref[0] * _rms(f_s[...], g_ref[...])


def _combine(y, tok, local, seg0, x, gate, g):
    e, _, d = y.shape
    _, b, cap = local.shape
    n_tok = x.shape[1]
    tt = COMBINE_TILE
    wf = COMBINE_FIRST
    ntiles = n_tok // tt
    assert seg0 % cap == 0 and n_tok % tt == 0 and cap % COMBINE_WINDOW == 0 and e % (COMBINE_WINDOW // wf) == 0
    seg = seg0 // cap
    bounds = jnp.arange(ntiles + 1, dtype=jnp.int32) * tt
    offs = jnp.sum((local[..., None] < bounds).astype(jnp.int32), axis=2)
    starts = _combine_first_start(offs[..., :-1], cap)
    spill = jnp.any(offs[..., 1:] > starts + wf, axis=0).astype(jnp.int32).reshape(-1)
    row = pl.BlockSpec((1, tt, d), lambda i, j, offs, spill: (i, j, 0))
    grid_spec = pltpu.PrefetchScalarGridSpec(
        num_scalar_prefetch=2,
        grid=(b, ntiles),
        in_specs=[pl.BlockSpec((e, 1, cap), lambda i, j, offs, spill: (0, 0, seg + i)),
                  pl.BlockSpec((e, cap, d), lambda i, j, offs, spill: (0, seg + i, 0),
                               pipeline_mode=pl.Buffered(1)),
                  row,
                  pl.BlockSpec((1, 1, d), lambda i, j, offs, spill: (i, 0, 0)),
                  pl.BlockSpec((1, d), lambda i, j, offs, spill: (0, 0))],
        out_specs=row,
        scratch_shapes=[pltpu.VMEM((tt, d), F32)],
    )
    return pl.pallas_call(
        functools.partial(_combine_body, n_tok=n_tok, cap=cap),
        grid_spec=grid_spec,
        out_shape=jax.ShapeDtypeStruct((b, n_tok, d), F32),
        compiler_params=_cparams(("parallel", "arbitrary")),
        name="combine",
    )(jnp.swapaxes(offs, 0, 1).reshape(-1), spill, tok, y, x, gate.reshape(b, 1, d), g.reshape(1, d))


def _gather_rows(src, idx):
    window = SC_GATHER_WINDOW
    n = idx.shape[0]
    width = src.shape[1]
    assert 2 * window * width * 4 <= SC_TILE_VMEM_BUDGET, width
    assert n % (window * SC_CORES * SC_SUBCORES) == 0, n
    mesh = plsc.VectorSubcoreMesh(core_axis_name="core", subcore_axis_name="subcore",
                                  num_cores=SC_CORES, num_subcores=SC_SUBCORES)

    @pl.kernel(out_type=jax.ShapeDtypeStruct((n, width), src.dtype), mesh=mesh, scratch_types=[],
               name="gather_rows")
    def gather(src_hbm, idx_hbm, out_hbm):
        def body(idx_vmem, out_vmem):
            pltpu.sync_copy(src_hbm.at[idx_vmem.at[0]], out_vmem)

        pltpu.emit_pipeline(
            body,
            grid=(n // window,),
            in_specs=[pl.BlockSpec((1, window), lambda i: (0, i))],
            out_specs=[pl.BlockSpec((window, width), lambda i: (i, 0))],
            core_axis_name=("core", "subcore"),
            dimension_semantics=(pltpu.PARALLEL,),
        )(idx_hbm, out_hbm)

    return gather(src, idx.reshape(1, n))


def _moe(parts, g, w_gate, w_up, w_down, layer):
    quarter = parts[0][1].shape[-1]
    d = 4 * quarter
    sizes = [p[1].shape[0] * p[1].shape[2] for p in parts]
    bases = [sum(sizes[:i]) for i in range(len(parts))]
    routed = [_route(p[0], base, 2 * base) for p, base in zip(parts, bases)]
    src = jnp.concatenate([p[1].reshape(-1, quarter) for p in parts], axis=0)
    vals, flat, rows0, rows1, tok = (jnp.concatenate([r[i] for r in routed], axis=1) for i in range(5))
    e, m = flat.shape
    pad = -m % LANES
    vals, rows0, rows1, tok = (jnp.pad(a, ((0, 0), (0, pad))) for a in (vals, rows0, rows1, tok))
    m += pad
    rows = jnp.stack([rows0, rows1], axis=1)
    xg = _gather_rows(src, rows.reshape(-1)).reshape(e, 2, m, quarter)
    chunk = next(c for c in (512, 544, 384, 256, 128) if m % c == 0)
    y = _expert_ffn(xg, vals[..., None], w_gate, w_up, w_down, layer, chunk)
    outs = []
    seg0 = 0
    for (aff_t, _, x, gate), r, base in zip(parts, routed, bases):
        b, _, n = aff_t.shape
        cap = r[0].shape[1] // b
        if n % COMBINE_TILE == 0 and cap % COMBINE_WINDOW == 0 and seg0 % cap == 0:
            outs.append(_combine(y, tok[:, None, :], r[4].reshape(e, b, cap), seg0, x, gate, g))
        else:
            ids = r[1] - base
            part = y[:, seg0:seg0 + b * cap].astype(F32)
            f = jnp.zeros((b * n, d), F32).at[ids.reshape(-1)].add(part.reshape(-1, d))
            outs.append(_gated_residual(x, f.reshape(b, n, d), gate, g, n))
        seg0 += b * cap
    return outs


def kernel(x, c, ctx, c_ctx, w_mod, b_mod, norm_g, w_in_even, w_out_even, conv_w, conv_b, conv_ln_g,
           conv_ln_b, na_rpb, w_in_odd, w_out_odd, pool_w, pool_scale, gla_gate_w, gla_gate_b, gla_head_g,
           router_w, expert_w_gate, expert_w_up, expert_w_down):
    b, l, d = x.shape
    n_ctx = ctx.shape[1]
    tm = 512

    mod_rows = jnp.concatenate([c, c_ctx[None], jnp.zeros((8 - b - 1, d), F32)], axis=0)

    mod_all = _modulation(mod_rows, w_mod, b_mod)

    def modulation(i):
        mm = mod_all[i]
        m = mm[:b].reshape(b, 6, d)
        mc = jnp.broadcast_to(mm[b].reshape(1, 6, d), (b, 6, d))
        return m, mc

    m, mc = modulation(0)
    g = norm_g[0]
    w_in = w_in_even[0].astype(BF16)
    w_out = w_out_even[0].astype(BF16)
    glu, qkv = _even_in(x, g[0], m[:, 0], m[:, 1], w_in, tm)
    glu_c, qkv_c = _even_in(ctx, g[0], mc[:, 0], mc[:, 1], w_in, n_ctx)
    a_lat = _conv_branch(glu, conv_w[0], conv_b[0], conv_ln_g[0], conv_ln_b[0], 256)
    a_ctx = _conv_branch(glu_c, conv_w[0], conv_b[0], conv_ln_g[0], conv_ln_b[0], n_ctx)
    na = _neighbourhood_attention(qkv, qkv_c, na_rpb[0])
    att_c = _context_attention(qkv_c)
    x, h2, aff = _out_proj(a_lat, na, x, w_out, g[1], m[:, 2], g[2], m[:, 3], m[:, 4], router_w[0], tm)
    ctx, h2c, aff_c = _out_proj(a_ctx, att_c, ctx, w_out, g[1], mc[:, 2], g[2], mc[:, 3], mc[:, 4],
                                router_w[0], n_ctx)
    x, ctx = _moe([(aff, h2, x, m[:, 5]), (aff_c, h2c, ctx, mc[:, 5])], g[3],
                  expert_w_gate, expert_w_up, expert_w_down, 0)

    m, mc = modulation(1)
    g = norm_g[1]
    w_in = w_in_odd[0].astype(BF16)
    w_out = w_out_odd[0].astype(BF16)
    cos_t, sin_t = _rope_tables(l)
    ones_t = jnp.ones((n_ctx, GLA_HEADS * GLA_DK), F32)
    _, qk_c, v_c, _, gate_c = _odd_in(ctx, g[0], mc[:, 0], mc[:, 1], w_in, ones_t, jnp.zeros_like(ones_t),
                                      gla_gate_w[0], gla_gate_b[0], n_ctx)
    s_zero = jnp.zeros((b, GLA_HEADS, GLA_DK, GLA_DV), F32)
    _, _, s_f, s_b = _gla(qk_c, v_c, gate_c, s_zero, s_zero, n_ctx)
    pool_u, qk, v, r, gate = _odd_in(x, g[0], m[:, 0], m[:, 1], w_in, cos_t, sin_t,
                                     gla_gate_w[0], gla_gate_b[0], tm)
    o_f, o_b, _, _ = _gla(qk, v, gate, s_f, s_b, 256)
    pool_y, d_lat = _odd_mid(pool_u, o_f, o_b, r, gla_head_g[0], pool_w[0].astype(BF16), pool_scale[0], 256)
    x, h2, aff = _out_proj(pool_y, d_lat, x, w_out, g[1], m[:, 2], g[2], m[:, 3], m[:, 4], router_w[1], tm)
    (x,) = _moe([(aff, h2, x, m[:, 5])], g[3], expert_w_gate, expert_w_up, expert_w_down, 1)
    return x
```

```python
import functools

import jax
import jax.numpy as jnp
from jax import lax
from jax.experimental import pallas as pl
from jax.experimental.pallas import tpu as pltpu
from jax.experimental.pallas import tpu_sc as plsc

F32 = jnp.float32
BF16 = jnp.bfloat16
HIGHEST = lax.Precision.HIGHEST

D_MODEL = 1024
GRID_W = 64
EPS = 1e-6
CONV_CH = 512
CONV_WIDTH = 31
CONV_HALO = 16
NA_HEADS = 8
NA_HEAD_DIM = 64
NA_KR = 8
NA_KC = 16
NA_ROWS_PER_BLOCK = 4
NA_BLOCKS_PER_STEP = 2
NA_WIN_ROWS = 12
POOL_CH = 512
POOL_WINDOWS = (2, 4, 8, 16)
POOL_GROUP = 128
POOL_HALO = 8
GLA_HEADS = 4
GLA_DK = 64
GLA_DV = 128
GLA_RANK = 16
GLA_TAU = 16.0
GLA_CHUNK = 64
ROPE_BASE = 10000.0
N_EXPERTS = 16
EXPERT_FF = 2816
EC_CAPACITY_FACTOR = 2
LANES = 128
SUBLANES = 8
NEG_BIG = -1e30
VMEM_LIMIT = 56 * 1024 * 1024
SC_CORES = 2
SC_SUBCORES = 16
SC_LANES = 16
F32_INF_BITS = 0x7F800000
SC_TILE_VMEM_BUDGET = 400 * 1024
SC_GATHER_WINDOW = 128
COMBINE_TILE = 256
COMBINE_WINDOW = 256
COMBINE_FIRST = 64
BF16_ROWS = 16


def _cparams(sem):
    return pltpu.CompilerParams(dimension_semantics=sem, vmem_limit_bytes=VMEM_LIMIT)


def _rms(x, g):
    return x * lax.rsqrt(jnp.mean(x * x, axis=-1, keepdims=True) + EPS) * g


def _sigmoid(x):
    return 1.0 / (1.0 + jnp.exp(-x))


def _silu(x):
    return x * _sigmoid(x)


def _dot(a, b):
    return jnp.dot(a, b, preferred_element_type=F32)


def _pack_bf16_pairs(h):
    half = h.shape[-1] // 2
    bits = lax.bitcast_convert_type(h.astype(BF16).astype(F32), jnp.uint32)
    packed = (bits[:, half:] & jnp.uint32(0xFFFF0000)) | (bits[:, :half] >> 16)
    return lax.bitcast_convert_type(packed, jnp.int32)


def _unpack_bf16_pairs(p):
    bits = lax.bitcast_convert_type(p, jnp.uint32)
    lo = lax.bitcast_convert_type(bits << 16, F32).astype(BF16)
    hi = lax.bitcast_convert_type(bits & jnp.uint32(0xFFFF0000), F32).astype(BF16)
    return lo, hi


def _dot_nt(a, b):
    return lax.dot_general(a, b, (((1,), (1,)), ((), ())), preferred_element_type=F32)


def _mod_body(c_ref, w_ref, b_ref, o_ref):
    o_ref[0] = jnp.dot(_silu(c_ref[...]), w_ref[0], precision=HIGHEST,
                       preferred_element_type=F32) + b_ref[0]


def _modulation(rows, w, b):
    depth, _, n = w.shape
    tn = 1536
    return pl.pallas_call(
        _mod_body,
        grid=(depth, n // tn),
        in_specs=[pl.BlockSpec((8, D_MODEL), lambda i, j: (0, 0)),
                  pl.BlockSpec((1, D_MODEL, tn), lambda i, j: (i, 0, j)),
                  pl.BlockSpec((1, 1, tn), lambda i, j: (i, 0, j))],
        out_specs=pl.BlockSpec((1, 8, tn), lambda i, j: (i, 0, j)),
        out_shape=jax.ShapeDtypeStruct((depth, 8, n), F32),
        compiler_params=_cparams(("parallel", "parallel")),
        name="modulation",
    )(rows, w, b.reshape(depth, 1, n))


def _even_in_body(x_ref, g_ref, sh_ref, sc_ref, w_ref, glu_ref, qkv_ref):
    h = (_rms(x_ref[0], g_ref[...]) * (1.0 + sc_ref[0]) + sh_ref[0]).astype(BF16)
    c = CONV_CH
    glu_ref[0] = _dot(h, w_ref[:, 0:c]) * _sigmoid(_dot(h, w_ref[:, c:2 * c]))
    hd = NA_HEADS * NA_HEAD_DIM
    q0 = 2 * c
    qkv_ref[0, :, 0:hd] = (_dot(h, w_ref[:, q0:q0 + hd]) * (NA_HEAD_DIM ** -0.5)).astype(BF16)
    qkv_ref[0, :, hd:3 * hd] = _dot(h, w_ref[:, q0 + hd:q0 + 3 * hd]).astype(BF16)


def _even_in(x, g, shift, scale, w_bf, tm):
    b, l, d = x.shape
    n = w_bf.shape[1]
    hd3 = 3 * NA_HEADS * NA_HEAD_DIM
    vec = pl.BlockSpec((1, 1, d), lambda i, j: (i, 0, 0))
    return pl.pallas_call(
        _even_in_body,
        grid=(b, l // tm),
        in_specs=[pl.BlockSpec((1, tm, d), lambda i, j: (i, j, 0)),
                  pl.BlockSpec((1, d), lambda i, j: (0, 0)),
                  vec, vec,
                  pl.BlockSpec((d, n), lambda i, j: (0, 0))],
        out_specs=[pl.BlockSpec((1, tm, CONV_CH), lambda i, j: (i, j, 0)),
                   pl.BlockSpec((1, tm, hd3), lambda i, j: (i, j, 0))],
        out_shape=[jax.ShapeDtypeStruct((b, l, CONV_CH), F32),
                   jax.ShapeDtypeStruct((b, l, hd3), BF16)],
        compiler_params=_cparams(("parallel", "parallel")),
        name="even_in",
    )(x, g.reshape(1, d), shift.reshape(b, 1, d), scale.reshape(b, 1, d), w_bf)


def _conv_body(cur_ref, prev_ref, next_ref, w_ref, b_ref, lg_ref, lb_ref, o_ref, buf_ref, sh_ref, *, tile, chunk):
    j = pl.program_id(1)
    last = pl.num_programs(1) - 1
    hal = CONV_HALO
    buf_ref[0:hal, :] = jnp.where(j == 0, 0.0, prev_ref[0])
    buf_ref[hal:hal + tile, :] = cur_ref[0]
    buf_ref[hal + tile:hal + tile + hal, :] = jnp.where(j == last, 0.0, next_ref[0])
    span = sh_ref.shape[1]
    for s in range(SUBLANES):
        sh_ref[s] = buf_ref[s:s + span, :]
    first = hal - CONV_WIDTH // 2
    reps = chunk // SUBLANES

    def rows(c, carry):
        r0 = pl.multiple_of(c * chunk, chunk)
        accs = [jnp.zeros((chunk, CONV_CH), F32) for _ in range(2)]
        for k in range(CONV_WIDTH):
            a, s = divmod(first + k, SUBLANES)
            wk = jnp.concatenate([w_ref[k]] * reps, axis=0)
            accs[k % 2] = accs[k % 2] + sh_ref[s, pl.ds(r0 + a * SUBLANES, chunk), :] * wk
        o_ref[0, pl.ds(r0, chunk), :] = accs[0] + accs[1]
        return carry

    lax.fori_loop(0, tile // chunk, rows, 0)
    y = o_ref[0] + b_ref[...]
    mu = jnp.mean(y, axis=-1, keepdims=True)
    yc = y - mu
    var = jnp.mean(yc * yc, axis=-1, keepdims=True)
    o_ref[0] = _silu(yc * lax.rsqrt(var + EPS) * lg_ref[...] + lb_ref[...])


def _conv_branch(glu, conv_w, conv_b, ln_g, ln_b, tile):
    b, l, c = glu.shape
    hal = CONV_HALO
    per = tile // hal
    nh = l // hal
    vec = pl.BlockSpec((1, c), lambda i, j: (0, 0))
    return pl.pallas_call(
        functools.partial(_conv_body, tile=tile, chunk=32),
        grid=(b, l // tile),
        in_specs=[pl.BlockSpec((1, tile, c), lambda i, j: (i, j, 0)),
                  pl.BlockSpec((1, hal, c), lambda i, j: (i, jnp.maximum(j * per - 1, 0), 0)),
                  pl.BlockSpec((1, hal, c), lambda i, j: (i, jnp.minimum((j + 1) * per, nh - 1), 0)),
                  pl.BlockSpec((CONV_WIDTH, SUBLANES, c), lambda i, j: (0, 0, 0)),
                  vec, vec, vec],
        out_specs=pl.BlockSpec((1, tile, c), lambda i, j: (i, j, 0)),
        out_shape=jax.ShapeDtypeStruct((b, l, c), F32),
        scratch_shapes=[pltpu.VMEM((tile + 2 * hal, c), F32),
                        pltpu.VMEM((SUBLANES, tile + 2 * hal - SUBLANES, c), F32)],
        compiler_params=_cparams(("parallel", "parallel")),
        name="conv_branch",
    )(glu, glu, glu, jnp.broadcast_to(conv_w[:, None, :], (CONV_WIDTH, SUBLANES, c)),
      conv_b.reshape(1, c), ln_g.reshape(1, c), ln_b.reshape(1, c))


def _na_window_start(j, rows):
    rb = NA_ROWS_PER_BLOCK
    return jnp.clip(j * rb - NA_KR // 2, 0, rows - NA_WIN_ROWS)


def _na_body(q_ref, k_ref, v_ref, kc_ref, vc_ref, *rest, rows):
    tab_refs, o_ref = rest[:-1], rest[-1]
    j = pl.program_id(2)
    nkeys = NA_WIN_ROWS * GRID_W
    tq = NA_ROWS_PER_BLOCK * GRID_W
    kc = kc_ref[0]
    vc = vc_ref[0]
    lane = lax.broadcasted_iota(jnp.int32, (1, LANES), 1)
    for sb, tab_ref in enumerate(tab_refs):
        start = pl.multiple_of(_na_window_start(j * len(tab_refs) + sb, rows) * GRID_W, GRID_W)
        q = q_ref[0, sb * tq:(sb + 1) * tq, :]
        kw = k_ref[0, pl.ds(start, nkeys), :]
        vw = v_ref[0, pl.ds(start, nkeys), :]
        out = jnp.zeros(q.shape, F32)
        for hh in range(LANES // NA_HEAD_DIM):
            in_head = (lane >= hh * NA_HEAD_DIM) & (lane < (hh + 1) * NA_HEAD_DIM)
            qh = jnp.where(in_head, q, jnp.zeros_like(q))
            s = _dot_nt(qh, kw) + tab_ref[0, hh]
            sc = _dot_nt(qh, kc)
            m = jnp.maximum(jnp.max(s, axis=-1, keepdims=True), jnp.max(sc, axis=-1, keepdims=True))
            p = jnp.exp(s - m)
            pc = jnp.exp(sc - m)
            denom = jnp.sum(p, axis=-1, keepdims=True) + jnp.sum(pc, axis=-1, keepdims=True)
            o = (_dot(p.astype(BF16), vw) + _dot(pc.astype(BF16), vc)) / denom
            out = jnp.where(in_head, o, out)
        o_ref[0, sb * tq:(sb + 1) * tq, :] = out


def _na_tables(rpb, rows):
    rb = NA_ROWS_PER_BLOCK
    nblk = rows // rb
    wr = NA_WIN_ROWS
    qc = jnp.arange(GRID_W)
    cs = jnp.clip(qc - NA_KC // 2, 0, GRID_W - NA_KC)
    col_ok = (qc[None, :] >= cs[:, None]) & (qc[None, :] < cs[:, None] + NA_KC)
    col_off = qc[None, :] - qc[:, None] + NA_KC - 1
    onehot = (col_off[:, :, None] == jnp.arange(2 * NA_KC - 1)[None, None, :]).astype(F32)
    blocks = jnp.einsum('hrd,qkd->hrqk', rpb.astype(F32), onehot, precision=HIGHEST)
    blocks = jnp.where(col_ok[None, None], blocks, NEG_BIG)
    masked = jnp.full((NA_HEADS, GRID_W, GRID_W), NEG_BIG, F32)
    tabs = []
    for jb in (0, 1, nblk - 1):
        ws = min(max(jb * rb - NA_KR // 2, 0), rows - wr)
        q_rows = []
        for qr in range(jb * rb, (jb + 1) * rb):
            rs = min(max(qr - NA_KR // 2, 0), rows - NA_KR)
            row = [blocks[:, kr - qr + NA_KR - 1] if rs <= kr < rs + NA_KR else masked
                   for kr in range(ws, ws + wr)]
            q_rows.append(jnp.concatenate(row, axis=-1))
        tabs.append(jnp.concatenate(q_rows, axis=1))
    return jnp.stack(tabs)


def _neighbourhood_attention(qkv, qkv_ctx, rpb):
    b, l, _ = qkv.shape
    n_ctx = qkv_ctx.shape[1]
    rows = l // GRID_W
    rb = NA_ROWS_PER_BLOCK
    nblk = rows // rb
    tq = rb * GRID_W
    nkeys = NA_WIN_ROWS * GRID_W
    hp = NA_HEADS * NA_HEAD_DIM // LANES
    tabs = _na_tables(rpb, rows)

    per = NA_BLOCKS_PER_STEP

    def cls(jb):
        return jnp.where(jb == 0, 0, jnp.where(jb == nblk - 1, 2, 1))

    def tab_spec(sb):
        return pl.BlockSpec((1, 2, tq, nkeys), lambda i, h, j: (cls(j * per + sb), h, 0, 0))

    return pl.pallas_call(
        functools.partial(_na_body, rows=rows),
        grid=(b, hp, nblk // per),
        in_specs=[pl.BlockSpec((1, per * tq, LANES), lambda i, h, j: (i, j, h)),
                  pl.BlockSpec((1, l, LANES), lambda i, h, j: (i, 0, hp + h)),
                  pl.BlockSpec((1, l, LANES), lambda i, h, j: (i, 0, 2 * hp + h)),
                  pl.BlockSpec((1, n_ctx, LANES), lambda i, h, j: (i, 0, hp + h)),
                  pl.BlockSpec((1, n_ctx, LANES), lambda i, h, j: (i, 0, 2 * hp + h))]
                 + [tab_spec(sb) for sb in range(per)],
        out_specs=pl.BlockSpec((1, per * tq, LANES), lambda i, h, j: (i, j, h)),
        out_shape=jax.ShapeDtypeStruct((b, l, NA_HEADS * NA_HEAD_DIM), F32),
        compiler_params=_cparams(("parallel", "parallel", "arbitrary")),
        name="neighbourhood_attention",
    )(qkv, qkv, qkv, qkv_ctx, qkv_ctx, *([tabs] * per))


def _ctx_attn_body(q_ref, k_ref, v_ref, o_ref):
    q = q_ref[0]
    k = k_ref[0]
    v = v_ref[0]
    lane = lax.broadcasted_iota(jnp.int32, (1, LANES), 1)
    out = jnp.zeros(q.shape, F32)
    for hh in range(LANES // NA_HEAD_DIM):
        in_head = (lane >= hh * NA_HEAD_DIM) & (lane < (hh + 1) * NA_HEAD_DIM)
        qh = jnp.where(in_head, q, jnp.zeros_like(q))
        s = _dot_nt(qh, k)
        p = jnp.exp(s - jnp.max(s, axis=-1, keepdims=True))
        o = _dot(p.astype(BF16), v) / jnp.sum(p, axis=-1, keepdims=True)
        out = jnp.where(in_head, o, out)
    o_ref[0] = out


def _context_attention(qkv_ctx):
    b, n, _ = qkv_ctx.shape
    hp = NA_HEADS * NA_HEAD_DIM // LANES
    return pl.pallas_call(
        _ctx_attn_body,
        grid=(b, hp),
        in_specs=[pl.BlockSpec((1, n, LANES), lambda i, h: (i, 0, h)),
                  pl.BlockSpec((1, n, LANES), lambda i, h: (i, 0, hp + h)),
                  pl.BlockSpec((1, n, LANES), lambda i, h: (i, 0, 2 * hp + h))],
        out_specs=pl.BlockSpec((1, n, LANES), lambda i, h: (i, 0, h)),
        out_shape=jax.ShapeDtypeStruct((b, n, NA_HEADS * NA_HEAD_DIM), F32),
        compiler_params=_cparams(("parallel", "parallel")),
        name="context_attention",
    )(qkv_ctx, qkv_ctx, qkv_ctx)


def _out_body(a_ref, b_ref, x_ref, w_ref, g1_ref, gate_ref, g2_ref, sh_ref, sc_ref, rw_ref,
              xo_ref, h_ref, aff_ref):
    half = a_ref.shape[-1]
    y = _dot(a_ref[0].astype(BF16), w_ref[0:half, :]) + _dot(b_ref[0].astype(BF16), w_ref[half:2 * half, :])
    xn = x_ref[0] + gate_ref[0] * _rms(y, g1_ref[...])
    xo_ref[0] = xn
    h = _rms(xn, g2_ref[...]) * (1.0 + sc_ref[0]) + sh_ref[0]
    packed = _pack_bf16_pairs(h)
    quarter = packed.shape[-1] // 2
    h_ref[0, 0] = packed[:, 0:quarter]
    h_ref[0, 1] = packed[:, quarter:2 * quarter]
    h_hi = h.astype(BF16)
    h_lo = (h - h_hi.astype(F32)).astype(BF16)
    both = _dot(h_hi, rw_ref[...])
    logits = both[:, 0:LANES] + (both[:, LANES:2 * LANES] + _dot(h_lo, rw_ref[:, 0:LANES]))
    lane = lax.broadcasted_iota(jnp.int32, (1, LANES), 1)
    logits = jnp.where(lane < N_EXPERTS, logits, NEG_BIG)
    e = jnp.exp(logits - jnp.max(logits, axis=-1, keepdims=True))
    aff = e / jnp.sum(e, axis=-1, keepdims=True)
    aff_ref[0] = aff.T[0:N_EXPERTS, :]


def _out_proj(a, b2, x, w_bf, g1, gate, g2, shift, scale, router_w, tm):
    b, l, d = x.shape
    half = a.shape[-1]
    rw = jnp.pad(router_w, ((0, 0), (0, LANES - N_EXPERTS)))
    rw_hi = rw.astype(BF16)
    rw_cat = jnp.concatenate([rw_hi, (rw - rw_hi.astype(F32)).astype(BF16)], axis=1)
    vec = pl.BlockSpec((1, d), lambda i, j: (0, 0))
    bvec = pl.BlockSpec((1, 1, d), lambda i, j: (i, 0, 0))
    rspec = pl.BlockSpec((d, 2 * LANES), lambda i, j: (0, 0))
    return pl.pallas_call(
        _out_body,
        grid=(b, l // tm),
        in_specs=[pl.BlockSpec((1, tm, half), lambda i, j: (i, j, 0)),
                  pl.BlockSpec((1, tm, half), lambda i, j: (i, j, 0)),
                  pl.BlockSpec((1, tm, d), lambda i, j: (i, j, 0)),
                  pl.BlockSpec((2 * half, d), lambda i, j: (0, 0)),
                  vec, bvec, vec, bvec, bvec, rspec],
        out_specs=[pl.BlockSpec((1, tm, d), lambda i, j: (i, j, 0)),
                   pl.BlockSpec((1, 2, tm, d // 4), lambda i, j: (i, 0, j, 0)),
                   pl.BlockSpec((1, N_EXPERTS, tm), lambda i, j: (i, 0, j))],
        out_shape=[jax.ShapeDtypeStruct((b, l, d), F32),
                   jax.ShapeDtypeStruct((b, 2, l, d // 4), jnp.int32),
                   jax.ShapeDtypeStruct((b, N_EXPERTS, l), F32)],
        compiler_params=_cparams(("parallel", "parallel")),
        name="out_proj",
    )(a, b2, x, w_bf, g1.reshape(1, d), gate.reshape(b, 1, d), g2.reshape(1, d),
      shift.reshape(b, 1, d), scale.reshape(b, 1, d), rw_cat)


def _moe_body(x0_ref, x1_ref, val_ref, wg_ref, wu_ref, wd_ref, o_ref, acc_s, x_s, wg_s, wu_s, wd_s, *, chunk):
    f = pl.program_id(1)
    m, quarter = x0_ref.shape

    @pl.when(f == 0)
    def _():
        def unpack(c, carry):
            r = pl.multiple_of(c * chunk, chunk)
            for s, x_ref in enumerate((x0_ref, x1_ref)):
                lo, hi = _unpack_bf16_pairs(x_ref[pl.ds(r, chunk), :])
                x_s[pl.ds(r, chunk), s * quarter:(s + 1) * quarter] = lo
                x_s[pl.ds(r, chunk), (2 + s) * quarter:(3 + s) * quarter] = hi
            acc_s[pl.ds(r, chunk), :] = jnp.zeros((chunk, 4 * quarter), F32)
            return carry

        lax.fori_loop(0, m // chunk, unpack, 0)

    wg_s[...] = wg_ref[0, 0].astype(BF16)
    wu_s[...] = wu_ref[0, 0].astype(BF16)
    wd_s[...] = wd_ref[0, 0].astype(BF16)

    def rows(c, carry):
        r = pl.multiple_of(c * chunk, chunk)
        xs = x_s[pl.ds(r, chunk), :]
        hid = (_silu(_dot(xs, wg_s[...])) * _dot(xs, wu_s[...])).astype(BF16)
        acc_s[pl.ds(r, chunk), :] += _dot(hid, wd_s[...])
        return carry

    lax.fori_loop(0, m // chunk, rows, 0, unroll=True)

    @pl.when(f == pl.num_programs(1) - 1)
    def _():
        o_ref[0] = (acc_s[...] * val_ref[0]).astype(o_ref.dtype)


def _expert_ffn(xg, vals, w_gate, w_up, w_down, layer, chunk):
    e, m, _ = vals.shape
    quarter = xg.shape[1]
    d = 4 * quarter
    ff = w_gate.shape[-1]
    tf = 256
    return pl.pallas_call(
        functools.partial(_moe_body, chunk=chunk),
        grid=(e, ff // tf),
        in_specs=[pl.BlockSpec((m, quarter), lambda i, f: (2 * i, 0)),
                  pl.BlockSpec((m, quarter), lambda i, f: (2 * i + 1, 0)),
                  pl.BlockSpec((1, m, 1), lambda i, f: (i, 0, 0)),
                  pl.BlockSpec((1, 1, d, tf), lambda i, f: (layer, i, 0, f)),
                  pl.BlockSpec((1, 1, d, tf), lambda i, f: (layer, i, 0, f)),
                  pl.BlockSpec((1, 1, tf, d), lambda i, f: (layer, i, f, 0))],
        out_specs=pl.BlockSpec((1, m, d), lambda i, f: (i, 0, 0)),
        out_shape=jax.ShapeDtypeStruct((e, m, d), BF16),
        scratch_shapes=[pltpu.VMEM((m, d), F32), pltpu.VMEM((m, d), BF16), pltpu.VMEM((d, tf), BF16),
                        pltpu.VMEM((d, tf), BF16), pltpu.VMEM((tf, d), BF16)],
        compiler_params=_cparams(("parallel", "arbitrary")),
        name="expert_ffn",
    )(xg, xg, vals, w_gate, w_up, w_down)


def _resid_body(x_ref, f_ref, gate_ref, g_ref, o_ref):
    o_ref[0] = x_ref[0] + gate_ref[0] * _rms(f_ref[0], g_ref[...])


def _gated_residual(x, f, gate, g, tm):
    b, l, d = x.shape
    blk = pl.BlockSpec((1, tm, d), lambda i, j: (i, j, 0))
    return pl.pallas_call(
        _resid_body,
        grid=(b, l // tm),
        in_specs=[blk, blk, pl.BlockSpec((1, 1, d), lambda i, j: (i, 0, 0)),
                  pl.BlockSpec((1, d), lambda i, j: (0, 0))],
        out_specs=blk,
        out_shape=jax.ShapeDtypeStruct((b, l, d), F32),
        compiler_params=_cparams(("parallel", "parallel")),
        name="gated_residual",
    )(x, f, gate.reshape(b, 1, d), g.reshape(1, d))


def _swap_pairs(x):
    nf = GLA_DK // 4
    lane = lax.broadcasted_iota(jnp.int32, (1, LANES), 1)
    up = pltpu.roll(x, LANES - nf, 1)
    down = pltpu.roll(x, nf, 1)
    return jnp.where(lane % (2 * nf) < nf, up, down)


def _odd_in_body(x_ref, g_ref, sh_ref, sc_ref, w_ref, cos_ref, sin_ref, gw_ref, gb_ref,
                 pool_ref, qk_ref, v_ref, r_ref, gate_ref):
    h = (_rms(x_ref[0], g_ref[...]) * (1.0 + sc_ref[0]) + sh_ref[0]).astype(BF16)
    qk = GLA_HEADS * GLA_DK
    vd = GLA_HEADS * GLA_DV
    q0 = POOL_CH
    v0 = q0 + 2 * qk
    r0 = v0 + vd
    l0 = r0 + vd
    pool_ref[0] = _dot(h, w_ref[:, 0:q0])
    qk_raw = _dot(h, w_ref[:, q0:v0])
    for s in range(2 * qk // LANES):
        raw = qk_raw[:, s * LANES:(s + 1) * LANES]
        c = cos_ref[:, (s * LANES) % qk:(s * LANES) % qk + LANES]
        sn = sin_ref[:, (s * LANES) % qk:(s * LANES) % qk + LANES]
        rot = raw * c + _swap_pairs(raw) * sn
        if s * LANES < qk:
            rot = rot * (GLA_DK ** -0.5)
        qk_ref[0, :, s * LANES:(s + 1) * LANES] = rot
    v_ref[0] = _dot(h, w_ref[:, v0:r0]).astype(BF16)
    r_ref[0] = _dot(h, w_ref[:, r0:l0])
    lr = _dot(h, w_ref[:, l0:l0 + 2 * GLA_RANK])
    z = jnp.dot(lr, gw_ref[...], precision=HIGHEST, preferred_element_type=F32) + gb_ref[...]
    gate_ref[0] = (jnp.minimum(z, 0.0) - jnp.log1p(jnp.exp(-jnp.abs(z)))) * (1.0 / GLA_TAU)


def _odd_in(x, g, shift, scale, w_bf, cos_t, sin_t, gate_w, gate_b, tm):
    b, l, d = x.shape
    n = w_bf.shape[1]
    qk = GLA_HEADS * GLA_DK
    vd = GLA_HEADS * GLA_DV
    gw = jnp.zeros((2 * GLA_RANK, 2 * qk), F32)
    gw = gw.at[:GLA_RANK, :qk].set(gate_w[0]).at[GLA_RANK:, qk:].set(gate_w[1])
    gb = jnp.concatenate([gate_b[0], gate_b[1]]).reshape(1, 2 * qk)
    vec = pl.BlockSpec((1, 1, d), lambda i, j: (i, 0, 0))
    row = lambda w: pl.BlockSpec((1, tm, w), lambda i, j: (i, j, 0))
    return pl.pallas_call(
        _odd_in_body,
        grid=(b, l // tm),
        in_specs=[row(d), pl.BlockSpec((1, d), lambda i, j: (0, 0)), vec, vec,
                  pl.BlockSpec((d, n), lambda i, j: (0, 0)),
                  pl.BlockSpec((tm, qk), lambda i, j: (j, 0)),
                  pl.BlockSpec((tm, qk), lambda i, j: (j, 0)),
                  pl.BlockSpec((2 * GLA_RANK, 2 * qk), lambda i, j: (0, 0)),
                  pl.BlockSpec((1, 2 * qk), lambda i, j: (0, 0))],
        out_specs=[row(POOL_CH), row(2 * qk), row(vd), row(vd), row(2 * qk)],
        out_shape=[jax.ShapeDtypeStruct((b, l, POOL_CH), F32),
                   jax.ShapeDtypeStruct((b, l, 2 * qk), F32),
                   jax.ShapeDtypeStruct((b, l, vd), BF16),
                   jax.ShapeDtypeStruct((b, l, vd), F32),
                   jax.ShapeDtypeStruct((b, l, 2 * qk), F32)],
        compiler_params=_cparams(("parallel", "parallel")),
        name="odd_in",
    )(x, g.reshape(1, d), shift.reshape(b, 1, d), scale.reshape(b, 1, d), w_bf, cos_t, sin_t, gw, gb)


def _rope_tables(l):
    t = jnp.arange(l)
    pos_r = (t // GRID_W).astype(F32)
    pos_c = (t % GRID_W).astype(F32)
    nf = GLA_DK // 4
    inv = jnp.power(ROPE_BASE, -jnp.arange(nf, dtype=F32) / nf)
    ar = pos_r[:, None] * inv[None, :]
    ac = pos_c[:, None] * inv[None, :]
    cos_h = jnp.concatenate([jnp.cos(ar), jnp.cos(ar), jnp.cos(ac), jnp.cos(ac)], axis=-1)
    sin_h = jnp.concatenate([-jnp.sin(ar), jnp.sin(ar), -jnp.sin(ac), jnp.sin(ac)], axis=-1)
    return jnp.tile(cos_h, (1, GLA_HEADS)), jnp.tile(sin_h, (1, GLA_HEADS))


def _gla_tile(qk, v, g, s, reverse):
    hk = GLA_HEADS * GLA_DK
    hv = GLA_HEADS * GLA_DV
    c = GLA_CHUNK
    t = qk.shape[0]
    n = t // c
    last_row, mid_row = (0, c // 2) if reverse else (c - 1, c // 2 - 1)
    ii = lax.broadcasted_iota(jnp.int32, (t, t), 0)
    jj = lax.broadcasted_iota(jnp.int32, (t, t), 1)
    ordered = (jj >= ii) if reverse else (jj <= ii)
    tri = jnp.where(ordered & (ii // c == jj // c), 1.0, 0.0).astype(BF16)
    g_hi = g.astype(BF16)
    rem = g - g_hi.astype(F32)
    g_mid = rem.astype(BF16)
    g_lo = (rem - g_mid.astype(F32)).astype(BF16)
    bc = _dot(tri, g_hi) + (_dot(tri, g_mid) + _dot(tri, g_lo))
    spread = lambda row: jnp.concatenate(
        [jnp.broadcast_to(bc[i * c + row:i * c + row + 1, :], (c, hk)) for i in range(n)], axis=0)
    b_mid = spread(mid_row)
    b_last = spread(last_row)
    qt = qk[:, 0:hk] * jnp.exp(bc - b_mid)
    kt = qk[:, hk:2 * hk] * jnp.exp(b_mid - bc)
    qe = (qt * jnp.exp(b_mid)).astype(BF16)
    ke = kt * jnp.exp(b_last - b_mid)
    ktb = kt.astype(BF16)
    lane = lax.broadcasted_iota(jnp.int32, (1, hk), 1)
    ci = lax.broadcasted_iota(jnp.int32, (c, c), 0)
    cj = lax.broadcasted_iota(jnp.int32, (c, c), 1)
    causal = (cj >= ci) if reverse else (cj <= ci)
    blockdiag = (lax.broadcasted_iota(jnp.int32, (hk, hv), 0) // GLA_DK
                 == lax.broadcasted_iota(jnp.int32, (hk, hv), 1) // GLA_DV)
    intra, upd, decay = [], [], []
    for i in range(n):
        rows = slice(i * c, (i + 1) * c)
        qs = jnp.concatenate(
            [jnp.where((lane >= h * GLA_DK) & (lane < (h + 1) * GLA_DK), qt[rows], 0.0) for h in range(GLA_HEADS)],
            axis=0).astype(BF16)
        att = _dot_nt(qs, ktb[rows])
        intra.append(jnp.concatenate(
            [_dot(jnp.where(causal, att[h * c:(h + 1) * c], 0.0).astype(BF16),
                  v[rows, h * GLA_DV:(h + 1) * GLA_DV]) for h in range(GLA_HEADS)], axis=-1))
        upd.append(jnp.where(blockdiag, _dot(ke[rows].T.astype(BF16), v[rows]), 0.0))
        decay.append(jnp.exp(jnp.sum(g[rows].T, axis=1, keepdims=True)))
    outs = [None] * n
    for i in (reversed(range(n)) if reverse else range(n)):
        rows = slice(i * c, (i + 1) * c)
        outs[i] = _dot(qe[rows], s.astype(BF16)) + intra[i]
        s = decay[i] * s + upd[i]
    return jnp.concatenate(outs, axis=0), s


def _gla_body(qkf_ref, qkb_ref, vf_ref, vb_ref, gf_ref, gb_ref, s0f_ref, s0b_ref,
              of_ref, ob_ref, sff_ref, sbf_ref, sf_ref, sb_ref, *, tile):
    n = pl.program_id(1)
    hk = GLA_HEADS * GLA_DK
    hv = GLA_HEADS * GLA_DV

    @pl.when(n == 0)
    def _():
        sf_ref[...] = jnp.zeros((hk, hv), F32)
        sb_ref[...] = jnp.zeros((hk, hv), F32)
        for h in range(GLA_HEADS):
            sf_ref[h * GLA_DK:(h + 1) * GLA_DK, h * GLA_DV:(h + 1) * GLA_DV] = s0f_ref[0, h]
            sb_ref[h * GLA_DK:(h + 1) * GLA_DK, h * GLA_DV:(h + 1) * GLA_DV] = s0b_ref[0, h]

    of_ref[0], sf_ref[...] = _gla_tile(qkf_ref[0], vf_ref[0], gf_ref[0], sf_ref[...], False)
    ob_ref[0], sb_ref[...] = _gla_tile(qkb_ref[0], vb_ref[0], gb_ref[0], sb_ref[...], True)

    @pl.when(n == pl.num_programs(1) - 1)
    def _():
        for h in range(GLA_HEADS):
            sff_ref[0, h] = sf_ref[h * GLA_DK:(h + 1) * GLA_DK, h * GLA_DV:(h + 1) * GLA_DV]
            sbf_ref[0, h] = sb_ref[h * GLA_DK:(h + 1) * GLA_DK, h * GLA_DV:(h + 1) * GLA_DV]


def _gla(qk, v, gates, s0f, s0b, tile):
    b, l, _ = qk.shape
    hk = GLA_HEADS * GLA_DK
    hv = GLA_HEADS * GLA_DV
    nt = l // tile
    fwd = lambda w, col: pl.BlockSpec((1, tile, w), lambda i, n: (i, n, col))
    bwd = lambda w, col: pl.BlockSpec((1, tile, w), lambda i, n: (i, nt - 1 - n, col))
    st = pl.BlockSpec((1, GLA_HEADS, GLA_DK, GLA_DV), lambda i, n: (i, 0, 0, 0))
    return pl.pallas_call(
        functools.partial(_gla_body, tile=tile),
        grid=(b, nt),
        in_specs=[fwd(2 * hk, 0), bwd(2 * hk, 0), fwd(hv, 0), bwd(hv, 0), fwd(hk, 0), bwd(hk, 1), st, st],
        out_specs=[fwd(hv, 0), bwd(hv, 0), st, st],
        out_shape=[jax.ShapeDtypeStruct((b, l, hv), F32), jax.ShapeDtypeStruct((b, l, hv), F32),
                   jax.ShapeDtypeStruct((b, GLA_HEADS, GLA_DK, GLA_DV), F32),
                   jax.ShapeDtypeStruct((b, GLA_HEADS, GLA_DK, GLA_DV), F32)],
        scratch_shapes=[pltpu.VMEM((hk, hv), F32), pltpu.VMEM((hk, hv), F32)],
        compiler_params=_cparams(("parallel", "arbitrary")),
        name="gla_scan",
    )(qk, qk, v, v, gates, gates, s0f, s0b)


def _odd_mid_body(cur_ref, prev_ref, next_ref, of_ref, ob_ref, r_ref, hg_ref, pw_ref, ps_ref,
                  pool_ref, d_ref, buf_ref, *, tile, seq):
    j = pl.program_id(1)
    last = pl.num_programs(1) - 1
    hal = POOL_HALO
    buf_ref[0:hal, :] = jnp.where(j == 0, 0.0, prev_ref[0])
    buf_ref[hal:hal + tile, :] = cur_ref[0]
    buf_ref[hal + tile:hal + tile + hal, :] = jnp.where(j == last, 0.0, next_ref[0])
    t = j * tile + lax.broadcasted_iota(jnp.int32, (tile, 1), 0)
    for gi, win in enumerate(POOL_WINDOWS):
        cols = slice(gi * POOL_GROUP, (gi + 1) * POOL_GROUP)
        acc = jnp.zeros((tile, POOL_GROUP), F32)
        for off in range(-(win // 2), win - win // 2):
            acc = acc + buf_ref[hal + off:hal + off + tile, cols]
        cnt = jnp.minimum(t + (win - win // 2), seq) - jnp.maximum(t - win // 2, 0)
        diff = acc / cnt.astype(F32) - cur_ref[0, :, cols]
        pool_ref[0, :, cols] = _dot(diff.astype(BF16), pw_ref[gi]) * ps_ref[:, cols]
    for h in range(GLA_HEADS):
        cols = slice(h * GLA_DV, (h + 1) * GLA_DV)
        o = of_ref[0, :, cols] + ob_ref[0, :, cols]
        d_ref[0, :, cols] = _rms(o, hg_ref[:, cols]) * _silu(r_ref[0, :, cols])


def _odd_mid(pool_u, o_f, o_b, r, head_g, pool_w_bf, pool_scale, tile):
    b, l, c = pool_u.shape
    hal = POOL_HALO
    per = tile // hal
    nh = l // hal
    blk = pl.BlockSpec((1, tile, c), lambda i, j: (i, j, 0))
    vec = pl.BlockSpec((1, c), lambda i, j: (0, 0))
    return pl.pallas_call(
        functools.partial(_odd_mid_body, tile=tile, seq=l),
        grid=(b, l // tile),
        in_specs=[blk,
                  pl.BlockSpec((1, hal, c), lambda i, j: (i, jnp.maximum(j * per - 1, 0), 0)),
                  pl.BlockSpec((1, hal, c), lambda i, j: (i, jnp.minimum((j + 1) * per, nh - 1), 0)),
                  blk, blk, blk, vec,
                  pl.BlockSpec((len(POOL_WINDOWS), POOL_GROUP, POOL_GROUP), lambda i, j: (0, 0, 0)),
                  vec],
        out_specs=[blk, blk],
        out_shape=[jax.ShapeDtypeStruct((b, l, c), F32), jax.ShapeDtypeStruct((b, l, c), F32)],
        scratch_shapes=[pltpu.VMEM((tile + 2 * hal, c), F32)],
        compiler_params=_cparams(("parallel", "parallel")),
        name="odd_mid",
    )(pool_u, pool_u, pool_u, o_f, o_b, r, head_g.reshape(1, c), pool_w_bf, pool_scale.reshape(1, c))


def _threshold_body(a_ref, thr_ref, need_ref, *, cap):
    bits = lax.bitcast_convert_type(a_ref[0], jnp.int32)
    rows = bits.shape[0]
    count_ge = lambda v: jnp.sum(jnp.where(bits >= v, 1.0, 0.0), axis=-1, keepdims=True)

    def step(_, carry):
        lo, hi = carry
        mid = lo + ((hi - lo + 1) >> 1)
        ok = count_ge(mid) >= cap
        return jnp.where(ok, mid, lo), jnp.where(ok, hi, mid - 1)

    lo, _ = lax.fori_loop(0, 31, step, (jnp.zeros((rows, 1), jnp.int32),
                                        jnp.full((rows, 1), F32_INF_BITS, jnp.int32)))
    above = jnp.sum(jnp.where(bits > lo, 1.0, 0.0), axis=-1, keepdims=True)
    width = thr_ref.shape[-1]
    thr_ref[0] = jnp.broadcast_to(lax.bitcast_convert_type(lo, F32), (rows, width))
    need_ref[0] = jnp.broadcast_to(cap - above.astype(jnp.int32), (rows, width))


def _route_threshold(aff_t, cap):
    b, e, n = aff_t.shape
    out = pl.BlockSpec((1, e, SC_LANES), lambda i: (i, 0, 0))
    return pl.pallas_call(
        functools.partial(_threshold_body, cap=cap),
        grid=(b,),
        in_specs=[pl.BlockSpec((1, e, n), lambda i: (i, 0, 0))],
        out_specs=[out, out],
        out_shape=[jax.ShapeDtypeStruct((b, e, SC_LANES), F32), jax.ShapeDtypeStruct((b, e, SC_LANES), jnp.int32)],
        compiler_params=_cparams(("parallel",)),
        name="route_threshold",
    )(aff_t)


def _route_compact(aff, thr, need, cap):
    r, n = aff.shape
    lanes = SC_LANES
    assert r == SC_CORES * SC_SUBCORES and n % lanes == 0
    mesh = plsc.VectorSubcoreMesh(core_axis_name="core", subcore_axis_name="subcore",
                                  num_cores=SC_CORES, num_subcores=SC_SUBCORES)

    @pl.kernel(out_type=[jax.ShapeDtypeStruct((r, cap), jnp.int32), jax.ShapeDtypeStruct((r, cap), F32)],
               mesh=mesh,
               scratch_types=[pltpu.VMEM((n,), F32), pltpu.VMEM((lanes,), F32), pltpu.VMEM((lanes,), jnp.int32),
                              pltpu.VMEM((cap,), jnp.int32), pltpu.VMEM((cap,), F32)],
               compiler_params=pltpu.CompilerParams(needs_layout_passes=False),
               name="route_compact")
    def compact(aff_hbm, thr_hbm, need_hbm, idx_hbm, val_hbm, row_v, thr_v, need_v, idx_v, val_v):
        w = lax.axis_index("subcore") * SC_CORES + lax.axis_index("core")
        pltpu.sync_copy(aff_hbm.at[w], row_v)
        pltpu.sync_copy(thr_hbm.at[w], thr_v)
        pltpu.sync_copy(need_hbm.at[w], need_v)
        thr = thr_v[...]
        need = need_v[...]
        lane = lax.iota(jnp.int32, lanes)
        ones = jnp.ones((lanes,), jnp.int32)

        def body(i, carry):
            n_out, n_eq = carry
            x = row_v[pl.ds(i * lanes, lanes)]
            eq = x == thr
            take = (x > thr) | (eq & (n_eq + plsc.cumsum(ones, mask=eq) <= need))
            pos = n_out + plsc.cumsum(ones, mask=take) - 1
            take = take & (pos < cap)
            plsc.store_scatter(idx_v, [pos], lane + i * lanes, mask=take)
            plsc.store_scatter(val_v, [pos], x, mask=take)
            return (n_out + plsc.all_reduce_population_count(take),
                    n_eq + plsc.all_reduce_population_count(eq))

        zero = jnp.zeros((lanes,), jnp.int32)
        lax.fori_loop(0, n // lanes, body, (zero, zero))
        pltpu.sync_copy(idx_v, idx_hbm.at[w])
        pltpu.sync_copy(val_v, val_hbm.at[w])

    return compact(aff, thr, need)


def _route(aff_t, tok_base, row_base):
    b, e, n = aff_t.shape
    cap = EC_CAPACITY_FACTOR * n // N_EXPERTS
    thr, need = _route_threshold(aff_t, cap)
    idx, vals = _route_compact(aff_t.reshape(b * e, n), thr.reshape(b * e, -1), need.reshape(b * e, -1), cap)
    idx = idx.reshape(b, e, cap)
    vals = vals.reshape(b, e, cap)
    bi = jnp.arange(b, dtype=idx.dtype)[:, None, None]
    per_expert = lambda a: jnp.swapaxes(a, 0, 1).reshape(e, b * cap)
    rows0 = idx + row_base + 2 * bi * n
    return (per_expert(vals), per_expert(idx + tok_base + bi * n), per_expert(rows0), per_expert(rows0 + n),
            per_expert(idx))


def _combine_first_start(p0, cap):
    return jnp.minimum((p0 // BF16_ROWS) * BF16_ROWS, cap - COMBINE_FIRST)


def _combine_body(offs_ref, spill_ref, tok_ref, y_ref, x_ref, gate_ref, g_ref, o_ref, f_s, *, n_tok, cap):
    bi = pl.program_id(0)
    j = pl.program_id(1)
    tt = COMBINE_TILE
    wf = COMBINE_FIRST
    wn = COMBINE_WINDOW
    ntiles = n_tok // tt
    n_exp = y_ref.shape[0]
    group = wn // wf
    sub = lax.broadcasted_iota(jnp.int32, (tt, 1), 0)
    lane = lax.broadcasted_iota(jnp.int32, (1, wn), 1)

    def slot_range(e):
        base = (bi * n_exp + e) * (ntiles + 1) + j
        return offs_ref[base], offs_ref[base + 1]

    acc = jnp.zeros(f_s.shape, F32)
    for g in range(n_exp // group):
        ys = []
        toks = jnp.full((1, wn), -1, jnp.int32)
        for k in range(group):
            e = g * group + k
            start = pl.multiple_of(_combine_first_start(slot_range(e)[0], cap), BF16_ROWS)
            ys.append(y_ref[e, pl.ds(start, wf), :])
            cs = pl.multiple_of(jnp.minimum((start // LANES) * LANES, cap - wn), LANES)
            rolled = pltpu.roll(tok_ref[e, :, pl.ds(cs, wn)], (k * wf + wn - (start - cs)) % wn, 1)
            toks = jnp.where((lane >= k * wf) & (lane < (k + 1) * wf), rolled, toks)
        hit = (toks - j * tt) == sub
        acc = acc + _dot(jnp.where(hit, 1.0, 0.0).astype(BF16), jnp.concatenate(ys, axis=0))
    f_s[...] = acc

    def more_windows(e, carry):
        p0, p1 = slot_range(e)
        lo = _combine_first_start(p0, cap) + wf
        first = (lo // LANES) * LANES

        def extra(w, carry):
            cs = pl.multiple_of(jnp.minimum(first + w * wn, cap - wn), LANES)
            tok = tok_ref[e, :, pl.ds(cs, wn)] - j * tt
            hit = (tok == sub) & (cs + lane >= jnp.maximum(lo, first + w * wn))
            f_s[...] += _dot(jnp.where(hit, 1.0, 0.0).astype(BF16), y_ref[e, pl.ds(cs, wn), :])
            return carry

        lax.fori_loop(0, (jnp.maximum(p1 - first, 0) + wn - 1) // wn * (p1 > lo).astype(jnp.int32), extra, 0)
        return carry

    @pl.when(spill_ref[bi * ntiles + j] != 0)
    def _():
        lax.fori_loop(0, n_exp, more_windows, 0)

    o_ref[0] = x_ref[0] + gate_ref[0] * _rms(f_s[...], g_ref[...])


def _combine(y, tok, local, seg0, x, gate, g):
    e, _, d = y.shape
    _, b, cap = local.shape
    n_tok = x.shape[1]
    tt = COMBINE_TILE
    wf = COMBINE_FIRST
    ntiles = n_tok // tt
    assert seg0 % cap == 0 and n_tok % tt == 0 and cap % COMBINE_WINDOW == 0 and e % (COMBINE_WINDOW // wf) == 0
    seg = seg0 // cap
    bounds = jnp.arange(ntiles + 1, dtype=jnp.int32) * tt
    offs = jnp.sum((local[..., None] < bounds).astype(jnp.int32), axis=2)
    starts = _combine_first_start(offs[..., :-1], cap)
    spill = jnp.any(offs[..., 1:] > starts + wf, axis=0).astype(jnp.int32).reshape(-1)
    row = pl.BlockSpec((1, tt, d), lambda i, j, offs, spill: (i, j, 0))
    grid_spec = pltpu.PrefetchScalarGridSpec(
        num_scalar_prefetch=2,
        grid=(b, ntiles),
        in_specs=[pl.BlockSpec((e, 1, cap), lambda i, j, offs, spill: (0, 0, seg + i)),
                  pl.BlockSpec((e, cap, d), lambda i, j, offs, spill: (0, seg + i, 0),
                               pipeline_mode=pl.Buffered(1)),
                  row,
                  pl.BlockSpec((1, 1, d), lambda i, j, offs, spill: (i, 0, 0)),
                  pl.BlockSpec((1, d), lambda i, j, offs, spill: (0, 0))],
        out_specs=row,
        scratch_shapes=[pltpu.VMEM((tt, d), F32)],
    )
    return pl.pallas_call(
        functools.partial(_combine_body, n_tok=n_tok, cap=cap),
        grid_spec=grid_spec,
        out_shape=jax.ShapeDtypeStruct((b, n_tok, d), F32),
        compiler_params=_cparams(("parallel", "arbitrary")),
        name="combine",
    )(jnp.swapaxes(offs, 0, 1).reshape(-1), spill, tok, y, x, gate.reshape(b, 1, d), g.reshape(1, d))


def _gather_rows(src, idx):
    window = SC_GATHER_WINDOW
    n = idx.shape[0]
    width = src.shape[1]
    assert 2 * window * width * 4 <= SC_TILE_VMEM_BUDGET, width
    assert n % (window * SC_CORES * SC_SUBCORES) == 0, n
    mesh = plsc.VectorSubcoreMesh(core_axis_name="core", subcore_axis_name="subcore",
                                  num_cores=SC_CORES, num_subcores=SC_SUBCORES)

    @pl.kernel(out_type=jax.ShapeDtypeStruct((n, width), src.dtype), mesh=mesh, scratch_types=[],
               name="gather_rows")
    def gather(src_hbm, idx_hbm, out_hbm):
        def body(idx_vmem, out_vmem):
            pltpu.sync_copy(src_hbm.at[idx_vmem.at[0]], out_vmem)

        pltpu.emit_pipeline(
            body,
            grid=(n // window,),
            in_specs=[pl.BlockSpec((1, window), lambda i: (0, i))],
            out_specs=[pl.BlockSpec((window, width), lambda i: (i, 0))],
            core_axis_name=("core", "subcore"),
            dimension_semantics=(pltpu.PARALLEL,),
        )(idx_hbm, out_hbm)

    return gather(src, idx.reshape(1, n))


def _moe(parts, g, w_gate, w_up, w_down, layer):
    quarter = parts[0][1].shape[-1]
    d = 4 * quarter
    sizes = [p[1].shape[0] * p[1].shape[2] for p in parts]
    bases = [sum(sizes[:i]) for i in range(len(parts))]
    routed = [_route(p[0], base, 2 * base) for p, base in zip(parts, bases)]
    src = jnp.concatenate([p[1].reshape(-1, quarter) for p in parts], axis=0)
    vals, flat, rows0, rows1, tok = (jnp.concatenate([r[i] for r in routed], axis=1) for i in range(5))
    e, m = flat.shape
    rows = jnp.stack([rows0, rows1], axis=1).reshape(-1)
    unit = SC_GATHER_WINDOW * SC_CORES * SC_SUBCORES
    xg = _gather_rows(src, jnp.pad(rows, (0, -rows.shape[0] % unit)))
    chunk = next(c for c in (512, 528, 384, 320, 256, 128) if m % c == 0)
    y = _expert_ffn(xg, vals[..., None], w_gate, w_up, w_down, layer, chunk)
    outs = []
    seg0 = 0
    for (aff_t, _, x, gate), r, base in zip(parts, routed, bases):
        b, _, n = aff_t.shape
        cap = r[0].shape[1] // b
        if n % COMBINE_TILE == 0 and cap % COMBINE_WINDOW == 0 and seg0 % cap == 0:
            outs.append(_combine(y, tok[:, None, :], r[4].reshape(e, b, cap), seg0, x, gate, g))
        else:
            ids = r[1] - base
            part = y[:, seg0:seg0 + b * cap].astype(F32)
            f = jnp.zeros((b * n, d), F32).at[ids.reshape(-1)].add(part.reshape(-1, d))
            outs.append(_gated_residual(x, f.reshape(b, n, d), gate, g, n))
        seg0 += b * cap
    return outs


def kernel(x, c, ctx, c_ctx, w_mod, b_mod, norm_g, w_in_even, w_out_even, conv_w, conv_b, conv_ln_g,
           conv_ln_b, na_rpb, w_in_odd, w_out_odd, pool_w, pool_scale, gla_gate_w, gla_gate_b, gla_head_g,
           router_w, expert_w_gate, expert_w_up, expert_w_down):
    b, l, d = x.shape
    n_ctx = ctx.shape[1]
    tm = 512

    mod_rows = jnp.concatenate([c, c_ctx[None], jnp.zeros((8 - b - 1, d), F32)], axis=0)

    mod_all = _modulation(mod_rows, w_mod, b_mod)

    def modulation(i):
        mm = mod_all[i]
        m = mm[:b].reshape(b, 6, d)
        mc = jnp.broadcast_to(mm[b].reshape(1, 6, d), (b, 6, d))
        return m, mc

    m, mc = modulation(0)
    g = norm_g[0]
    w_in = w_in_even[0].astype(BF16)
    w_out = w_out_even[0].astype(BF16)
    glu, qkv = _even_in(x, g[0], m[:, 0], m[:, 1], w_in, tm)
    glu_c, qkv_c = _even_in(ctx, g[0], mc[:, 0], mc[:, 1], w_in, n_ctx)
    a_lat = _conv_branch(glu, conv_w[0], conv_b[0], conv_ln_g[0], conv_ln_b[0], 256)
    a_ctx = _conv_branch(glu_c, conv_w[0], conv_b[0], conv_ln_g[0], conv_ln_b[0], n_ctx)
    na = _neighbourhood_attention(qkv, qkv_c, na_rpb[0])
    att_c = _context_attention(qkv_c)
    x, h2, aff = _out_proj(a_lat, na, x, w_out, g[1], m[:, 2], g[2], m[:, 3], m[:, 4], router_w[0], tm)
    ctx, h2c, aff_c = _out_proj(a_ctx, att_c, ctx, w_out, g[1], mc[:, 2], g[2], mc[:, 3], mc[:, 4],
                                router_w[0], n_ctx)
    x, ctx = _moe([(aff, h2, x, m[:, 5]), (aff_c, h2c, ctx, mc[:, 5])], g[3],
                  expert_w_gate, expert_w_up, expert_w_down, 0)

    m, mc = modulation(1)
    g = norm_g[1]
    w_in = w_in_odd[0].astype(BF16)
    w_out = w_out_odd[0].astype(BF16)
    cos_t, sin_t = _rope_tables(l)
    ones_t = jnp.ones((n_ctx, GLA_HEADS * GLA_DK), F32)
    _, qk_c, v_c, _, gate_c = _odd_in(ctx, g[0], mc[:, 0], mc[:, 1], w_in, ones_t, jnp.zeros_like(ones_t),
                                      gla_gate_w[0], gla_gate_b[0], n_ctx)
    s_zero = jnp.zeros((b, GLA_HEADS, GLA_DK, GLA_DV), F32)
    _, _, s_f, s_b = _gla(qk_c, v_c, gate_c, s_zero, s_zero, n_ctx)
    pool_u, qk, v, r, gate = _odd_in(x, g[0], m[:, 0], m[:, 1], w_in, cos_t, sin_t,
                                     gla_gate_w[0], gla_gate_b[0], tm)
    o_f, o_b, _, _ = _gla(qk, v, gate, s_f, s_b, 256)
    pool_y, d_lat = _odd_mid(pool_u, o_f, o_b, r, gla_head_g[0], pool_w[0].astype(BF16), pool_scale[0], 256)
    x, h2, aff = _out_proj(pool_y, d_lat, x, w_out, g[1], m[:, 2], g[2], m[:, 3], m[:, 4], router_w[1], tm)
    (x,) = _moe([(aff, h2, x, m[:, 5])], g[3], expert_w_gate, expert_w_up, expert_w_down, 1)
    return x
```

```python
import functools

import jax
import jax.numpy as jnp
from jax import lax
from jax.experimental import pallas as pl
from jax.experimental.pallas import tpu as pltpu
from jax.experimental.pallas import tpu_sc as plsc

F32 = jnp.float32
BF16 = jnp.bfloat16
HIGHEST = lax.Precision.HIGHEST

D_MODEL = 1024
GRID_W = 64
EPS = 1e-6
CONV_CH = 512
CONV_WIDTH = 31
CONV_HALO = 16
NA_HEADS = 8
NA_HEAD_DIM = 64
NA_KR = 8
NA_KC = 16
NA_ROWS_PER_BLOCK = 4
NA_BLOCKS_PER_STEP = 2
NA_WIN_ROWS = 12
POOL_CH = 512
POOL_WINDOWS = (2, 4, 8, 16)
POOL_GROUP = 128
POOL_HALO = 8
GLA_HEADS = 4
GLA_DK = 64
GLA_DV = 128
GLA_RANK = 16
GLA_TAU = 16.0
GLA_CHUNK = 64
ROPE_BASE = 10000.0
N_EXPERTS = 16
EC_CAPACITY_FACTOR = 2
ROW_TILE = 512
SEQ_TILE = 256
MOD_COLS = 1536
EXPERT_FF_TILE = 256
LANES = 128
SUBLANES = 8
NEG_BIG = -1e30
VMEM_LIMIT = 56 * 1024 * 1024
SC_CORES = 2
SC_SUBCORES = 16
SC_LANES = 16
F32_INF_BITS = 0x7F800000
SC_TILE_VMEM_BUDGET = 400 * 1024
SC_GATHER_WINDOW = 128
COMBINE_TILE = 256
COMBINE_WINDOW = 256
COMBINE_FIRST = 64
BF16_ROWS = 16


def _cparams(sem):
    return pltpu.CompilerParams(dimension_semantics=sem, vmem_limit_bytes=VMEM_LIMIT)


def _rms(x, g):
    return x * lax.rsqrt(jnp.mean(x * x, axis=-1, keepdims=True) + EPS) * g


def _sigmoid(x):
    return 1.0 / (1.0 + jnp.exp(-x))


def _silu(x):
    return x * _sigmoid(x)


def _dot(a, b):
    return jnp.dot(a, b, preferred_element_type=F32)


def _pack_bf16_pairs(h):
    half = h.shape[-1] // 2
    bits = lax.bitcast_convert_type(h.astype(BF16).astype(F32), jnp.uint32)
    packed = (bits[:, half:] & jnp.uint32(0xFFFF0000)) | (bits[:, :half] >> 16)
    return lax.bitcast_convert_type(packed, jnp.int32)


def _unpack_bf16_pairs(p):
    bits = lax.bitcast_convert_type(p, jnp.uint32)
    lo = lax.bitcast_convert_type(bits << 16, F32).astype(BF16)
    hi = lax.bitcast_convert_type(bits & jnp.uint32(0xFFFF0000), F32).astype(BF16)
    return lo, hi


def _dot_nt(a, b):
    return lax.dot_general(a, b, (((1,), (1,)), ((), ())), preferred_element_type=F32)


def _mod_body(c_ref, w_ref, b_ref, o_ref):
    o_ref[0] = jnp.dot(_silu(c_ref[...]), w_ref[0], precision=HIGHEST,
                       preferred_element_type=F32) + b_ref[0]


def _modulation(rows, w, b):
    depth, _, n = w.shape
    tn = MOD_COLS
    return pl.pallas_call(
        _mod_body,
        grid=(depth, n // tn),
        in_specs=[pl.BlockSpec((8, D_MODEL), lambda i, j: (0, 0)),
                  pl.BlockSpec((1, D_MODEL, tn), lambda i, j: (i, 0, j)),
                  pl.BlockSpec((1, 1, tn), lambda i, j: (i, 0, j))],
        out_specs=pl.BlockSpec((1, 8, tn), lambda i, j: (i, 0, j)),
        out_shape=jax.ShapeDtypeStruct((depth, 8, n), F32),
        compiler_params=_cparams(("parallel", "parallel")),
        name="modulation",
    )(rows, w, b.reshape(depth, 1, n))


def _even_in_body(x_ref, g_ref, sh_ref, sc_ref, w_ref, glu_ref, qkv_ref):
    h = (_rms(x_ref[0], g_ref[...]) * (1.0 + sc_ref[0]) + sh_ref[0]).astype(BF16)
    c = CONV_CH
    glu_ref[0] = _dot(h, w_ref[:, 0:c]) * _sigmoid(_dot(h, w_ref[:, c:2 * c]))
    hd = NA_HEADS * NA_HEAD_DIM
    q0 = 2 * c
    qkv_ref[0, :, 0:hd] = (_dot(h, w_ref[:, q0:q0 + hd]) * (NA_HEAD_DIM ** -0.5)).astype(BF16)
    qkv_ref[0, :, hd:3 * hd] = _dot(h, w_ref[:, q0 + hd:q0 + 3 * hd]).astype(BF16)


def _even_in(x, g, shift, scale, w_bf, tm):
    b, l, d = x.shape
    n = w_bf.shape[1]
    hd3 = 3 * NA_HEADS * NA_HEAD_DIM
    vec = pl.BlockSpec((1, 1, d), lambda i, j: (i, 0, 0))
    return pl.pallas_call(
        _even_in_body,
        grid=(b, l // tm),
        in_specs=[pl.BlockSpec((1, tm, d), lambda i, j: (i, j, 0)),
                  pl.BlockSpec((1, d), lambda i, j: (0, 0)),
                  vec, vec,
                  pl.BlockSpec((d, n), lambda i, j: (0, 0))],
        out_specs=[pl.BlockSpec((1, tm, CONV_CH), lambda i, j: (i, j, 0)),
                   pl.BlockSpec((1, tm, hd3), lambda i, j: (i, j, 0))],
        out_shape=[jax.ShapeDtypeStruct((b, l, CONV_CH), F32),
                   jax.ShapeDtypeStruct((b, l, hd3), BF16)],
        compiler_params=_cparams(("parallel", "parallel")),
        name="even_in",
    )(x, g.reshape(1, d), shift.reshape(b, 1, d), scale.reshape(b, 1, d), w_bf)


def _conv_body(cur_ref, prev_ref, next_ref, w_ref, b_ref, lg_ref, lb_ref, o_ref, buf_ref, sh_ref, *, tile, chunk):
    j = pl.program_id(1)
    last = pl.num_programs(1) - 1
    hal = CONV_HALO
    buf_ref[0:hal, :] = jnp.where(j == 0, 0.0, prev_ref[0])
    buf_ref[hal:hal + tile, :] = cur_ref[0]
    buf_ref[hal + tile:hal + tile + hal, :] = jnp.where(j == last, 0.0, next_ref[0])
    span = sh_ref.shape[1]
    for s in range(SUBLANES):
        sh_ref[s] = buf_ref[s:s + span, :]
    first = hal - CONV_WIDTH // 2
    reps = chunk // SUBLANES

    def rows(c, carry):
        r0 = pl.multiple_of(c * chunk, chunk)
        accs = [jnp.zeros((chunk, CONV_CH), F32) for _ in range(2)]
        for k in range(CONV_WIDTH):
            a, s = divmod(first + k, SUBLANES)
            wk = jnp.concatenate([w_ref[k]] * reps, axis=0)
            accs[k % 2] = accs[k % 2] + sh_ref[s, pl.ds(r0 + a * SUBLANES, chunk), :] * wk
        o_ref[0, pl.ds(r0, chunk), :] = accs[0] + accs[1]
        return carry

    lax.fori_loop(0, tile // chunk, rows, 0)
    y = o_ref[0] + b_ref[...]
    mu = jnp.mean(y, axis=-1, keepdims=True)
    yc = y - mu
    var = jnp.mean(yc * yc, axis=-1, keepdims=True)
    o_ref[0] = _silu(yc * lax.rsqrt(var + EPS) * lg_ref[...] + lb_ref[...])


def _conv_branch(glu, conv_w, conv_b, ln_g, ln_b, tile):
    b, l, c = glu.shape
    hal = CONV_HALO
    per = tile // hal
    nh = l // hal
    vec = pl.BlockSpec((1, c), lambda i, j: (0, 0))
    return pl.pallas_call(
        functools.partial(_conv_body, tile=tile, chunk=32),
        grid=(b, l // tile),
        in_specs=[pl.BlockSpec((1, tile, c), lambda i, j: (i, j, 0)),
                  pl.BlockSpec((1, hal, c), lambda i, j: (i, jnp.maximum(j * per - 1, 0), 0)),
                  pl.BlockSpec((1, hal, c), lambda i, j: (i, jnp.minimum((j + 1) * per, nh - 1), 0)),
                  pl.BlockSpec((CONV_WIDTH, SUBLANES, c), lambda i, j: (0, 0, 0)),
                  vec, vec, vec],
        out_specs=pl.BlockSpec((1, tile, c), lambda i, j: (i, j, 0)),
        out_shape=jax.ShapeDtypeStruct((b, l, c), F32),
        scratch_shapes=[pltpu.VMEM((tile + 2 * hal, c), F32),
                        pltpu.VMEM((SUBLANES, tile + 2 * hal - SUBLANES, c), F32)],
        compiler_params=_cparams(("parallel", "parallel")),
        name="conv_branch",
    )(glu, glu, glu, jnp.broadcast_to(conv_w[:, None, :], (CONV_WIDTH, SUBLANES, c)),
      conv_b.reshape(1, c), ln_g.reshape(1, c), ln_b.reshape(1, c))


def _na_window_start(j, rows):
    rb = NA_ROWS_PER_BLOCK
    return jnp.clip(j * rb - NA_KR // 2, 0, rows - NA_WIN_ROWS)


def _na_body(q_ref, k_ref, v_ref, kc_ref, vc_ref, *rest, rows):
    tab_refs, o_ref = rest[:-1], rest[-1]
    j = pl.program_id(2)
    nkeys = NA_WIN_ROWS * GRID_W
    tq = NA_ROWS_PER_BLOCK * GRID_W
    kc = kc_ref[0]
    vc = vc_ref[0]
    lane = lax.broadcasted_iota(jnp.int32, (1, LANES), 1)
    for sb, tab_ref in enumerate(tab_refs):
        start = pl.multiple_of(_na_window_start(j * len(tab_refs) + sb, rows) * GRID_W, GRID_W)
        q = q_ref[0, sb * tq:(sb + 1) * tq, :]
        kw = k_ref[0, pl.ds(start, nkeys), :]
        vw = v_ref[0, pl.ds(start, nkeys), :]
        out = jnp.zeros(q.shape, F32)
        for hh in range(LANES // NA_HEAD_DIM):
            in_head = (lane >= hh * NA_HEAD_DIM) & (lane < (hh + 1) * NA_HEAD_DIM)
            qh = jnp.where(in_head, q, jnp.zeros_like(q))
            s = _dot_nt(qh, kw) + tab_ref[0, hh]
            sc = _dot_nt(qh, kc)
            m = jnp.maximum(jnp.max(s, axis=-1, keepdims=True), jnp.max(sc, axis=-1, keepdims=True))
            p = jnp.exp(s - m)
            pc = jnp.exp(sc - m)
            denom = jnp.sum(p, axis=-1, keepdims=True) + jnp.sum(pc, axis=-1, keepdims=True)
            o = (_dot(p.astype(BF16), vw) + _dot(pc.astype(BF16), vc)) / denom
            out = jnp.where(in_head, o, out)
        o_ref[0, sb * tq:(sb + 1) * tq, :] = out


def _na_tables(rpb, rows):
    rb = NA_ROWS_PER_BLOCK
    nblk = rows // rb
    wr = NA_WIN_ROWS
    qc = jnp.arange(GRID_W)
    cs = jnp.clip(qc - NA_KC // 2, 0, GRID_W - NA_KC)
    col_ok = (qc[None, :] >= cs[:, None]) & (qc[None, :] < cs[:, None] + NA_KC)
    col_off = qc[None, :] - qc[:, None] + NA_KC - 1
    onehot = (col_off[:, :, None] == jnp.arange(2 * NA_KC - 1)[None, None, :]).astype(F32)
    blocks = jnp.einsum('hrd,qkd->hrqk', rpb.astype(F32), onehot, precision=HIGHEST)
    blocks = jnp.where(col_ok[None, None], blocks, NEG_BIG)
    masked = jnp.full((NA_HEADS, GRID_W, GRID_W), NEG_BIG, F32)
    tabs = []
    for jb in (0, 1, nblk - 1):
        ws = min(max(jb * rb - NA_KR // 2, 0), rows - wr)
        q_rows = []
        for qr in range(jb * rb, (jb + 1) * rb):
            rs = min(max(qr - NA_KR // 2, 0), rows - NA_KR)
            row = [blocks[:, kr - qr + NA_KR - 1] if rs <= kr < rs + NA_KR else masked
                   for kr in range(ws, ws + wr)]
            q_rows.append(jnp.concatenate(row, axis=-1))
        tabs.append(jnp.concatenate(q_rows, axis=1))
    return jnp.stack(tabs)


def _neighbourhood_attention(qkv, qkv_ctx, rpb):
    b, l, _ = qkv.shape
    n_ctx = qkv_ctx.shape[1]
    rows = l // GRID_W
    rb = NA_ROWS_PER_BLOCK
    nblk = rows // rb
    tq = rb * GRID_W
    nkeys = NA_WIN_ROWS * GRID_W
    hp = NA_HEADS * NA_HEAD_DIM // LANES
    tabs = _na_tables(rpb, rows)

    per = NA_BLOCKS_PER_STEP

    def cls(jb):
        return jnp.where(jb == 0, 0, jnp.where(jb == nblk - 1, 2, 1))

    def tab_spec(sb):
        return pl.BlockSpec((1, 2, tq, nkeys), lambda i, h, j: (cls(j * per + sb), h, 0, 0))

    return pl.pallas_call(
        functools.partial(_na_body, rows=rows),
        grid=(b, hp, nblk // per),
        in_specs=[pl.BlockSpec((1, per * tq, LANES), lambda i, h, j: (i, j, h)),
                  pl.BlockSpec((1, l, LANES), lambda i, h, j: (i, 0, hp + h)),
                  pl.BlockSpec((1, l, LANES), lambda i, h, j: (i, 0, 2 * hp + h)),
                  pl.BlockSpec((1, n_ctx, LANES), lambda i, h, j: (i, 0, hp + h)),
                  pl.BlockSpec((1, n_ctx, LANES), lambda i, h, j: (i, 0, 2 * hp + h))]
                 + [tab_spec(sb) for sb in range(per)],
        out_specs=pl.BlockSpec((1, per * tq, LANES), lambda i, h, j: (i, j, h)),
        out_shape=jax.ShapeDtypeStruct((b, l, NA_HEADS * NA_HEAD_DIM), F32),
        compiler_params=_cparams(("parallel", "parallel", "arbitrary")),
        name="neighbourhood_attention",
    )(qkv, qkv, qkv, qkv_ctx, qkv_ctx, *([tabs] * per))


def _ctx_attn_body(q_ref, k_ref, v_ref, o_ref):
    q = q_ref[0]
    k = k_ref[0]
    v = v_ref[0]
    lane = lax.broadcasted_iota(jnp.int32, (1, LANES), 1)
    out = jnp.zeros(q.shape, F32)
    for hh in range(LANES // NA_HEAD_DIM):
        in_head = (lane >= hh * NA_HEAD_DIM) & (lane < (hh + 1) * NA_HEAD_DIM)
        qh = jnp.where(in_head, q, jnp.zeros_like(q))
        s = _dot_nt(qh, k)
        p = jnp.exp(s - jnp.max(s, axis=-1, keepdims=True))
        o = _dot(p.astype(BF16), v) / jnp.sum(p, axis=-1, keepdims=True)
        out = jnp.where(in_head, o, out)
    o_ref[0] = out


def _context_attention(qkv_ctx):
    b, n, _ = qkv_ctx.shape
    hp = NA_HEADS * NA_HEAD_DIM // LANES
    return pl.pallas_call(
        _ctx_attn_body,
        grid=(b, hp),
        in_specs=[pl.BlockSpec((1, n, LANES), lambda i, h: (i, 0, h)),
                  pl.BlockSpec((1, n, LANES), lambda i, h: (i, 0, hp + h)),
                  pl.BlockSpec((1, n, LANES), lambda i, h: (i, 0, 2 * hp + h))],
        out_specs=pl.BlockSpec((1, n, LANES), lambda i, h: (i, 0, h)),
        out_shape=jax.ShapeDtypeStruct((b, n, NA_HEADS * NA_HEAD_DIM), F32),
        compiler_params=_cparams(("parallel", "parallel")),
        name="context_attention",
    )(qkv_ctx, qkv_ctx, qkv_ctx)


def _out_body(a_ref, b_ref, x_ref, w_ref, g1_ref, gate_ref, g2_ref, sh_ref, sc_ref, rw_ref,
              xo_ref, h_ref, aff_ref):
    half = a_ref.shape[-1]
    y = _dot(a_ref[0].astype(BF16), w_ref[0:half, :]) + _dot(b_ref[0].astype(BF16), w_ref[half:2 * half, :])
    xn = x_ref[0] + gate_ref[0] * _rms(y, g1_ref[...])
    xo_ref[0] = xn
    h = _rms(xn, g2_ref[...]) * (1.0 + sc_ref[0]) + sh_ref[0]
    packed = _pack_bf16_pairs(h)
    quarter = packed.shape[-1] // 2
    h_ref[0, 0] = packed[:, 0:quarter]
    h_ref[0, 1] = packed[:, quarter:2 * quarter]
    h_hi = h.astype(BF16)
    h_lo = (h - h_hi.astype(F32)).astype(BF16)
    both = _dot(h_hi, rw_ref[...])
    logits = both[:, 0:LANES] + (both[:, LANES:2 * LANES] + _dot(h_lo, rw_ref[:, 0:LANES]))
    lane = lax.broadcasted_iota(jnp.int32, (1, LANES), 1)
    logits = jnp.where(lane < N_EXPERTS, logits, NEG_BIG)
    e = jnp.exp(logits - jnp.max(logits, axis=-1, keepdims=True))
    aff = e / jnp.sum(e, axis=-1, keepdims=True)
    aff_ref[0] = aff.T[0:N_EXPERTS, :]


def _out_proj(a, b2, x, w_bf, g1, gate, g2, shift, scale, router_w, tm):
    b, l, d = x.shape
    half = a.shape[-1]
    rw = jnp.pad(router_w, ((0, 0), (0, LANES - N_EXPERTS)))
    rw_hi = rw.astype(BF16)
    rw_cat = jnp.concatenate([rw_hi, (rw - rw_hi.astype(F32)).astype(BF16)], axis=1)
    vec = pl.BlockSpec((1, d), lambda i, j: (0, 0))
    bvec = pl.BlockSpec((1, 1, d), lambda i, j: (i, 0, 0))
    rspec = pl.BlockSpec((d, 2 * LANES), lambda i, j: (0, 0))
    return pl.pallas_call(
        _out_body,
        grid=(b, l // tm),
        in_specs=[pl.BlockSpec((1, tm, half), lambda i, j: (i, j, 0)),
                  pl.BlockSpec((1, tm, half), lambda i, j: (i, j, 0)),
                  pl.BlockSpec((1, tm, d), lambda i, j: (i, j, 0)),
                  pl.BlockSpec((2 * half, d), lambda i, j: (0, 0)),
                  vec, bvec, vec, bvec, bvec, rspec],
        out_specs=[pl.BlockSpec((1, tm, d), lambda i, j: (i, j, 0)),
                   pl.BlockSpec((1, 2, tm, d // 4), lambda i, j: (i, 0, j, 0)),
                   pl.BlockSpec((1, N_EXPERTS, tm), lambda i, j: (i, 0, j))],
        out_shape=[jax.ShapeDtypeStruct((b, l, d), F32),
                   jax.ShapeDtypeStruct((b, 2, l, d // 4), jnp.int32),
                   jax.ShapeDtypeStruct((b, N_EXPERTS, l), F32)],
        compiler_params=_cparams(("parallel", "parallel")),
        name="out_proj",
    )(a, b2, x, w_bf, g1.reshape(1, d), gate.reshape(b, 1, d), g2.reshape(1, d),
      shift.reshape(b, 1, d), scale.reshape(b, 1, d), rw_cat)


def _moe_body(x0_ref, x1_ref, val_ref, wg_ref, wu_ref, wd_ref, o_ref, acc_s, x_s, wg_s, wu_s, wd_s, *, chunk):
    f = pl.program_id(1)
    m, quarter = x0_ref.shape

    @pl.when(f == 0)
    def _():
        def unpack(c, carry):
            r = pl.multiple_of(c * chunk, chunk)
            for s, x_ref in enumerate((x0_ref, x1_ref)):
                lo, hi = _unpack_bf16_pairs(x_ref[pl.ds(r, chunk), :])
                x_s[pl.ds(r, chunk), s * quarter:(s + 1) * quarter] = lo
                x_s[pl.ds(r, chunk), (2 + s) * quarter:(3 + s) * quarter] = hi
            acc_s[pl.ds(r, chunk), :] = jnp.zeros((chunk, 4 * quarter), F32)
            return carry

        lax.fori_loop(0, m // chunk, unpack, 0)

    wg_s[...] = wg_ref[0, 0].astype(BF16)
    wu_s[...] = wu_ref[0, 0].astype(BF16)
    wd_s[...] = wd_ref[0, 0].astype(BF16)

    def rows(c, carry):
        r = pl.multiple_of(c * chunk, chunk)
        xs = x_s[pl.ds(r, chunk), :]
        hid = (_silu(_dot(xs, wg_s[...])) * _dot(xs, wu_s[...])).astype(BF16)
        acc_s[pl.ds(r, chunk), :] += _dot(hid, wd_s[...])
        return carry

    lax.fori_loop(0, m // chunk, rows, 0, unroll=True)

    @pl.when(f == pl.num_programs(1) - 1)
    def _():
        o_ref[0] = (acc_s[...] * val_ref[0]).astype(o_ref.dtype)


def _expert_ffn(xg, vals, w_gate, w_up, w_down, layer, chunk):
    e, m, _ = vals.shape
    quarter = xg.shape[1]
    d = 4 * quarter
    ff = w_gate.shape[-1]
    tf = EXPERT_FF_TILE
    return pl.pallas_call(
        functools.partial(_moe_body, chunk=chunk),
        grid=(e, ff // tf),
        in_specs=[pl.BlockSpec((m, quarter), lambda i, f: (2 * i, 0)),
                  pl.BlockSpec((m, quarter), lambda i, f: (2 * i + 1, 0)),
                  pl.BlockSpec((1, m, 1), lambda i, f: (i, 0, 0)),
                  pl.BlockSpec((1, 1, d, tf), lambda i, f: (layer, i, 0, f)),
                  pl.BlockSpec((1, 1, d, tf), lambda i, f: (layer, i, 0, f)),
                  pl.BlockSpec((1, 1, tf, d), lambda i, f: (layer, i, f, 0))],
        out_specs=pl.BlockSpec((1, m, d), lambda i, f: (i, 0, 0)),
        out_shape=jax.ShapeDtypeStruct((e, m, d), BF16),
        scratch_shapes=[pltpu.VMEM((m, d), F32), pltpu.VMEM((m, d), BF16), pltpu.VMEM((d, tf), BF16),
                        pltpu.VMEM((d, tf), BF16), pltpu.VMEM((tf, d), BF16)],
        compiler_params=_cparams(("parallel", "arbitrary")),
        name="expert_ffn",
    )(xg, xg, vals, w_gate, w_up, w_down)


def _resid_body(x_ref, f_ref, gate_ref, g_ref, o_ref):
    o_ref[0] = x_ref[0] + gate_ref[0] * _rms(f_ref[0], g_ref[...])


def _gated_residual(x, f, gate, g, tm):
    b, l, d = x.shape
    blk = pl.BlockSpec((1, tm, d), lambda i, j: (i, j, 0))
    return pl.pallas_call(
        _resid_body,
        grid=(b, l // tm),
        in_specs=[blk, blk, pl.BlockSpec((1, 1, d), lambda i, j: (i, 0, 0)),
                  pl.BlockSpec((1, d), lambda i, j: (0, 0))],
        out_specs=blk,
        out_shape=jax.ShapeDtypeStruct((b, l, d), F32),
        compiler_params=_cparams(("parallel", "parallel")),
        name="gated_residual",
    )(x, f, gate.reshape(b, 1, d), g.reshape(1, d))


def _swap_pairs(x):
    nf = GLA_DK // 4
    lane = lax.broadcasted_iota(jnp.int32, (1, LANES), 1)
    up = pltpu.roll(x, LANES - nf, 1)
    down = pltpu.roll(x, nf, 1)
    return jnp.where(lane % (2 * nf) < nf, up, down)


def _odd_in_body(x_ref, g_ref, sh_ref, sc_ref, w_ref, cos_ref, sin_ref, gw_ref, gb_ref,
                 pool_ref, qk_ref, v_ref, r_ref, gate_ref):
    h = (_rms(x_ref[0], g_ref[...]) * (1.0 + sc_ref[0]) + sh_ref[0]).astype(BF16)
    qk = GLA_HEADS * GLA_DK
    vd = GLA_HEADS * GLA_DV
    q0 = POOL_CH
    v0 = q0 + 2 * qk
    r0 = v0 + vd
    l0 = r0 + vd
    pool_ref[0] = _dot(h, w_ref[:, 0:q0])
    qk_raw = _dot(h, w_ref[:, q0:v0])
    for s in range(2 * qk // LANES):
        raw = qk_raw[:, s * LANES:(s + 1) * LANES]
        c = cos_ref[:, (s * LANES) % qk:(s * LANES) % qk + LANES]
        sn = sin_ref[:, (s * LANES) % qk:(s * LANES) % qk + LANES]
        rot = raw * c + _swap_pairs(raw) * sn
        if s * LANES < qk:
            rot = rot * (GLA_DK ** -0.5)
        qk_ref[0, :, s * LANES:(s + 1) * LANES] = rot
    v_ref[0] = _dot(h, w_ref[:, v0:r0]).astype(BF16)
    r_ref[0] = _dot(h, w_ref[:, r0:l0])
    lr = _dot(h, w_ref[:, l0:l0 + 2 * GLA_RANK])
    z = jnp.dot(lr, gw_ref[...], precision=HIGHEST, preferred_element_type=F32) + gb_ref[...]
    gate_ref[0] = (jnp.minimum(z, 0.0) - jnp.log1p(jnp.exp(-jnp.abs(z)))) * (1.0 / GLA_TAU)


def _odd_in(x, g, shift, scale, w_bf, cos_t, sin_t, gate_w, gate_b, tm):
    b, l, d = x.shape
    n = w_bf.shape[1]
    qk = GLA_HEADS * GLA_DK
    vd = GLA_HEADS * GLA_DV
    gw = jnp.zeros((2 * GLA_RANK, 2 * qk), F32)
    gw = gw.at[:GLA_RANK, :qk].set(gate_w[0]).at[GLA_RANK:, qk:].set(gate_w[1])
    gb = jnp.concatenate([gate_b[0], gate_b[1]]).reshape(1, 2 * qk)
    vec = pl.BlockSpec((1, 1, d), lambda i, j: (i, 0, 0))
    row = lambda w: pl.BlockSpec((1, tm, w), lambda i, j: (i, j, 0))
    return pl.pallas_call(
        _odd_in_body,
        grid=(b, l // tm),
        in_specs=[row(d), pl.BlockSpec((1, d), lambda i, j: (0, 0)), vec, vec,
                  pl.BlockSpec((d, n), lambda i, j: (0, 0)),
                  pl.BlockSpec((tm, qk), lambda i, j: (j, 0)),
                  pl.BlockSpec((tm, qk), lambda i, j: (j, 0)),
                  pl.BlockSpec((2 * GLA_RANK, 2 * qk), lambda i, j: (0, 0)),
                  pl.BlockSpec((1, 2 * qk), lambda i, j: (0, 0))],
        out_specs=[row(POOL_CH), row(2 * qk), row(vd), row(vd), row(2 * qk)],
        out_shape=[jax.ShapeDtypeStruct((b, l, POOL_CH), F32),
                   jax.ShapeDtypeStruct((b, l, 2 * qk), F32),
                   jax.ShapeDtypeStruct((b, l, vd), BF16),
                   jax.ShapeDtypeStruct((b, l, vd), F32),
                   jax.ShapeDtypeStruct((b, l, 2 * qk), F32)],
        compiler_params=_cparams(("parallel", "parallel")),
        name="odd_in",
    )(x, g.reshape(1, d), shift.reshape(b, 1, d), scale.reshape(b, 1, d), w_bf, cos_t, sin_t, gw, gb)


def _rope_tables(l):
    t = jnp.arange(l)
    pos_r = (t // GRID_W).astype(F32)
    pos_c = (t % GRID_W).astype(F32)
    nf = GLA_DK // 4
    inv = jnp.power(ROPE_BASE, -jnp.arange(nf, dtype=F32) / nf)
    ar = pos_r[:, None] * inv[None, :]
    ac = pos_c[:, None] * inv[None, :]
    cos_h = jnp.concatenate([jnp.cos(ar), jnp.cos(ar), jnp.cos(ac), jnp.cos(ac)], axis=-1)
    sin_h = jnp.concatenate([-jnp.sin(ar), jnp.sin(ar), -jnp.sin(ac), jnp.sin(ac)], axis=-1)
    return jnp.tile(cos_h, (1, GLA_HEADS)), jnp.tile(sin_h, (1, GLA_HEADS))


def _gla_tile(qk, v, g, s, reverse):
    hk = GLA_HEADS * GLA_DK
    hv = GLA_HEADS * GLA_DV
    c = GLA_CHUNK
    t = qk.shape[0]
    n = t // c
    last_row, mid_row = (0, c // 2) if reverse else (c - 1, c // 2 - 1)
    ii = lax.broadcasted_iota(jnp.int32, (t, t), 0)
    jj = lax.broadcasted_iota(jnp.int32, (t, t), 1)
    ordered = (jj >= ii) if reverse else (jj <= ii)
    tri = jnp.where(ordered & (ii // c == jj // c), 1.0, 0.0).astype(BF16)
    g_hi = g.astype(BF16)
    rem = g - g_hi.astype(F32)
    g_mid = rem.astype(BF16)
    g_lo = (rem - g_mid.astype(F32)).astype(BF16)
    bc = _dot(tri, g_hi) + (_dot(tri, g_mid) + _dot(tri, g_lo))
    spread = lambda row: jnp.concatenate(
        [jnp.broadcast_to(bc[i * c + row:i * c + row + 1, :], (c, hk)) for i in range(n)], axis=0)
    b_mid = spread(mid_row)
    b_last = spread(last_row)
    qt = qk[:, 0:hk] * jnp.exp(bc - b_mid)
    kt = qk[:, hk:2 * hk] * jnp.exp(b_mid - bc)
    qe = (qt * jnp.exp(b_mid)).astype(BF16)
    ke = kt * jnp.exp(b_last - b_mid)
    ktb = kt.astype(BF16)
    lane = lax.broadcasted_iota(jnp.int32, (1, hk), 1)
    ci = lax.broadcasted_iota(jnp.int32, (c, c), 0)
    cj = lax.broadcasted_iota(jnp.int32, (c, c), 1)
    causal = (cj >= ci) if reverse else (cj <= ci)
    blockdiag = (lax.broadcasted_iota(jnp.int32, (hk, hv), 0) // GLA_DK
                 == lax.broadcasted_iota(jnp.int32, (hk, hv), 1) // GLA_DV)
    intra, upd, decay = [], [], []
    for i in range(n):
        rows = slice(i * c, (i + 1) * c)
        qs = jnp.concatenate(
            [jnp.where((lane >= h * GLA_DK) & (lane < (h + 1) * GLA_DK), qt[rows], 0.0) for h in range(GLA_HEADS)],
            axis=0).astype(BF16)
        att = _dot_nt(qs, ktb[rows])
        intra.append(jnp.concatenate(
            [_dot(jnp.where(causal, att[h * c:(h + 1) * c], 0.0).astype(BF16),
                  v[rows, h * GLA_DV:(h + 1) * GLA_DV]) for h in range(GLA_HEADS)], axis=-1))
        upd.append(jnp.where(blockdiag, _dot(ke[rows].T.astype(BF16), v[rows]), 0.0))
        decay.append(jnp.exp(jnp.sum(g[rows].T, axis=1, keepdims=True)))
    outs = [None] * n
    for i in (reversed(range(n)) if reverse else range(n)):
        rows = slice(i * c, (i + 1) * c)
        outs[i] = _dot(qe[rows], s.astype(BF16)) + intra[i]
        s = decay[i] * s + upd[i]
    return jnp.concatenate(outs, axis=0), s


def _gla_body(qkf_ref, qkb_ref, vf_ref, vb_ref, gf_ref, gb_ref, s0f_ref, s0b_ref,
              of_ref, ob_ref, sff_ref, sbf_ref, sf_ref, sb_ref, *, tile):
    n = pl.program_id(1)
    hk = GLA_HEADS * GLA_DK
    hv = GLA_HEADS * GLA_DV

    @pl.when(n == 0)
    def _():
        sf_ref[...] = jnp.zeros((hk, hv), F32)
        sb_ref[...] = jnp.zeros((hk, hv), F32)
        for h in range(GLA_HEADS):
            sf_ref[h * GLA_DK:(h + 1) * GLA_DK, h * GLA_DV:(h + 1) * GLA_DV] = s0f_ref[0, h]
            sb_ref[h * GLA_DK:(h + 1) * GLA_DK, h * GLA_DV:(h + 1) * GLA_DV] = s0b_ref[0, h]

    of_ref[0], sf_ref[...] = _gla_tile(qkf_ref[0], vf_ref[0], gf_ref[0], sf_ref[...], False)
    ob_ref[0], sb_ref[...] = _gla_tile(qkb_ref[0], vb_ref[0], gb_ref[0], sb_ref[...], True)

    @pl.when(n == pl.num_programs(1) - 1)
    def _():
        for h in range(GLA_HEADS):
            sff_ref[0, h] = sf_ref[h * GLA_DK:(h + 1) * GLA_DK, h * GLA_DV:(h + 1) * GLA_DV]
            sbf_ref[0, h] = sb_ref[h * GLA_DK:(h + 1) * GLA_DK, h * GLA_DV:(h + 1) * GLA_DV]


def _gla(qk, v, gates, s0f, s0b, tile):
    b, l, _ = qk.shape
    hk = GLA_HEADS * GLA_DK
    hv = GLA_HEADS * GLA_DV
    nt = l // tile
    fwd = lambda w, col: pl.BlockSpec((1, tile, w), lambda i, n: (i, n, col))
    bwd = lambda w, col: pl.BlockSpec((1, tile, w), lambda i, n: (i, nt - 1 - n, col))
    st = pl.BlockSpec((1, GLA_HEADS, GLA_DK, GLA_DV), lambda i, n: (i, 0, 0, 0))
    return pl.pallas_call(
        functools.partial(_gla_body, tile=tile),
        grid=(b, nt),
        in_specs=[fwd(2 * hk, 0), bwd(2 * hk, 0), fwd(hv, 0), bwd(hv, 0), fwd(hk, 0), bwd(hk, 1), st, st],
        out_specs=[fwd(hv, 0), bwd(hv, 0), st, st],
        out_shape=[jax.ShapeDtypeStruct((b, l, hv), F32), jax.ShapeDtypeStruct((b, l, hv), F32),
                   jax.ShapeDtypeStruct((b, GLA_HEADS, GLA_DK, GLA_DV), F32),
                   jax.ShapeDtypeStruct((b, GLA_HEADS, GLA_DK, GLA_DV), F32)],
        scratch_shapes=[pltpu.VMEM((hk, hv), F32), pltpu.VMEM((hk, hv), F32)],
        compiler_params=_cparams(("parallel", "arbitrary")),
        name="gla_scan",
    )(qk, qk, v, v, gates, gates, s0f, s0b)


def _odd_mid_body(cur_ref, prev_ref, next_ref, of_ref, ob_ref, r_ref, hg_ref, pw_ref, ps_ref,
                  pool_ref, d_ref, buf_ref, *, tile, seq):
    j = pl.program_id(1)
    last = pl.num_programs(1) - 1
    hal = POOL_HALO
    buf_ref[0:hal, :] = jnp.where(j == 0, 0.0, prev_ref[0])
    buf_ref[hal:hal + tile, :] = cur_ref[0]
    buf_ref[hal + tile:hal + tile + hal, :] = jnp.where(j == last, 0.0, next_ref[0])
    t = j * tile + lax.broadcasted_iota(jnp.int32, (tile, 1), 0)
    for gi, win in enumerate(POOL_WINDOWS):
        cols = slice(gi * POOL_GROUP, (gi + 1) * POOL_GROUP)
        acc = jnp.zeros((tile, POOL_GROUP), F32)
        for off in range(-(win // 2), win - win // 2):
            acc = acc + buf_ref[hal + off:hal + off + tile, cols]
        cnt = jnp.minimum(t + (win - win // 2), seq) - jnp.maximum(t - win // 2, 0)
        diff = acc / cnt.astype(F32) - cur_ref[0, :, cols]
        pool_ref[0, :, cols] = _dot(diff.astype(BF16), pw_ref[gi]) * ps_ref[:, cols]
    for h in range(GLA_HEADS):
        cols = slice(h * GLA_DV, (h + 1) * GLA_DV)
        o = of_ref[0, :, cols] + ob_ref[0, :, cols]
        d_ref[0, :, cols] = _rms(o, hg_ref[:, cols]) * _silu(r_ref[0, :, cols])


def _odd_mid(pool_u, o_f, o_b, r, head_g, pool_w_bf, pool_scale, tile):
    b, l, c = pool_u.shape
    hal = POOL_HALO
    per = tile // hal
    nh = l // hal
    blk = pl.BlockSpec((1, tile, c), lambda i, j: (i, j, 0))
    vec = pl.BlockSpec((1, c), lambda i, j: (0, 0))
    return pl.pallas_call(
        functools.partial(_odd_mid_body, tile=tile, seq=l),
        grid=(b, l // tile),
        in_specs=[blk,
                  pl.BlockSpec((1, hal, c), lambda i, j: (i, jnp.maximum(j * per - 1, 0), 0)),
                  pl.BlockSpec((1, hal, c), lambda i, j: (i, jnp.minimum((j + 1) * per, nh - 1), 0)),
                  blk, blk, blk, vec,
                  pl.BlockSpec((len(POOL_WINDOWS), POOL_GROUP, POOL_GROUP), lambda i, j: (0, 0, 0)),
                  vec],
        out_specs=[blk, blk],
        out_shape=[jax.ShapeDtypeStruct((b, l, c), F32), jax.ShapeDtypeStruct((b, l, c), F32)],
        scratch_shapes=[pltpu.VMEM((tile + 2 * hal, c), F32)],
        compiler_params=_cparams(("parallel", "parallel")),
        name="odd_mid",
    )(pool_u, pool_u, pool_u, o_f, o_b, r, head_g.reshape(1, c), pool_w_bf, pool_scale.reshape(1, c))


def _threshold_body(a_ref, thr_ref, need_ref, *, cap):
    bits = lax.bitcast_convert_type(a_ref[0], jnp.int32)
    rows = bits.shape[0]
    count_ge = lambda v: jnp.sum(jnp.where(bits >= v, 1.0, 0.0), axis=-1, keepdims=True)

    def step(_, carry):
        lo, hi = carry
        mid = lo + ((hi - lo + 1) >> 1)
        ok = count_ge(mid) >= cap
        return jnp.where(ok, mid, lo), jnp.where(ok, hi, mid - 1)

    lo, _ = lax.fori_loop(0, 31, step, (jnp.zeros((rows, 1), jnp.int32),
                                        jnp.full((rows, 1), F32_INF_BITS, jnp.int32)))
    above = jnp.sum(jnp.where(bits > lo, 1.0, 0.0), axis=-1, keepdims=True)
    width = thr_ref.shape[-1]
    thr_ref[0] = jnp.broadcast_to(lax.bitcast_convert_type(lo, F32), (rows, width))
    need_ref[0] = jnp.broadcast_to(cap - above.astype(jnp.int32), (rows, width))


def _route_threshold(aff_t, cap):
    b, e, n = aff_t.shape
    out = pl.BlockSpec((1, e, SC_LANES), lambda i: (i, 0, 0))
    return pl.pallas_call(
        functools.partial(_threshold_body, cap=cap),
        grid=(b,),
        in_specs=[pl.BlockSpec((1, e, n), lambda i: (i, 0, 0))],
        out_specs=[out, out],
        out_shape=[jax.ShapeDtypeStruct((b, e, SC_LANES), F32), jax.ShapeDtypeStruct((b, e, SC_LANES), jnp.int32)],
        compiler_params=_cparams(("parallel",)),
        name="route_threshold",
    )(aff_t)


def _route_compact(aff, thr, need, cap):
    r, n = aff.shape
    lanes = SC_LANES
    assert r == SC_CORES * SC_SUBCORES and n % lanes == 0
    mesh = plsc.VectorSubcoreMesh(core_axis_name="core", subcore_axis_name="subcore",
                                  num_cores=SC_CORES, num_subcores=SC_SUBCORES)

    @pl.kernel(out_type=[jax.ShapeDtypeStruct((r, cap), jnp.int32), jax.ShapeDtypeStruct((r, cap), F32)],
               mesh=mesh,
               scratch_types=[pltpu.VMEM((n,), F32), pltpu.VMEM((lanes,), F32), pltpu.VMEM((lanes,), jnp.int32),
                              pltpu.VMEM((cap,), jnp.int32), pltpu.VMEM((cap,), F32)],
               compiler_params=pltpu.CompilerParams(needs_layout_passes=False),
               name="route_compact")
    def compact(aff_hbm, thr_hbm, need_hbm, idx_hbm, val_hbm, row_v, thr_v, need_v, idx_v, val_v):
        w = lax.axis_index("subcore") * SC_CORES + lax.axis_index("core")
        pltpu.sync_copy(aff_hbm.at[w], row_v)
        pltpu.sync_copy(thr_hbm.at[w], thr_v)
        pltpu.sync_copy(need_hbm.at[w], need_v)
        thr = thr_v[...]
        need = need_v[...]
        lane = lax.iota(jnp.int32, lanes)
        ones = jnp.ones((lanes,), jnp.int32)

        def body(i, carry):
            n_out, n_eq = carry
            x = row_v[pl.ds(i * lanes, lanes)]
            eq = x == thr
            take = (x > thr) | (eq & (n_eq + plsc.cumsum(ones, mask=eq) <= need))
            pos = n_out + plsc.cumsum(ones, mask=take) - 1
            take = take & (pos < cap)
            plsc.store_scatter(idx_v, [pos], lane + i * lanes, mask=take)
            plsc.store_scatter(val_v, [pos], x, mask=take)
            return (n_out + plsc.all_reduce_population_count(take),
                    n_eq + plsc.all_reduce_population_count(eq))

        zero = jnp.zeros((lanes,), jnp.int32)
        lax.fori_loop(0, n // lanes, body, (zero, zero))
        pltpu.sync_copy(idx_v, idx_hbm.at[w])
        pltpu.sync_copy(val_v, val_hbm.at[w])

    return compact(aff, thr, need)


def _route(aff_t, tok_base, row_base):
    b, e, n = aff_t.shape
    cap = EC_CAPACITY_FACTOR * n // N_EXPERTS
    thr, need = _route_threshold(aff_t, cap)
    idx, vals = _route_compact(aff_t.reshape(b * e, n), thr.reshape(b * e, -1), need.reshape(b * e, -1), cap)
    idx = idx.reshape(b, e, cap)
    vals = vals.reshape(b, e, cap)
    bi = jnp.arange(b, dtype=idx.dtype)[:, None, None]
    per_expert = lambda a: jnp.swapaxes(a, 0, 1).reshape(e, b * cap)
    rows0 = idx + row_base + 2 * bi * n
    return (per_expert(vals), per_expert(idx + tok_base + bi * n), per_expert(rows0), per_expert(rows0 + n),
            per_expert(idx))


def _combine_first_start(p0, cap):
    return jnp.minimum((p0 // BF16_ROWS) * BF16_ROWS, cap - COMBINE_FIRST)


def _combine_body(offs_ref, spill_ref, tok_ref, y_ref, x_ref, gate_ref, g_ref, o_ref, f_s, *, n_tok, cap):
    bi = pl.program_id(0)
    j = pl.program_id(1)
    tt = COMBINE_TILE
    wf = COMBINE_FIRST
    wn = COMBINE_WINDOW
    ntiles = n_tok // tt
    n_exp = y_ref.shape[0]
    group = wn // wf
    sub = lax.broadcasted_iota(jnp.int32, (tt, 1), 0)
    lane = lax.broadcasted_iota(jnp.int32, (1, wn), 1)

    def slot_range(e):
        base = (bi * n_exp + e) * (ntiles + 1) + j
        return offs_ref[base], offs_ref[base + 1]

    acc = jnp.zeros(f_s.shape, F32)
    for g in range(n_exp // group):
        ys = []
        toks = jnp.full((1, wn), -1, jnp.int32)
        for k in range(group):
            e = g * group + k
            start = pl.multiple_of(_combine_first_start(slot_range(e)[0], cap), BF16_ROWS)
            ys.append(y_ref[e, pl.ds(start, wf), :])
            cs = pl.multiple_of(jnp.minimum((start // LANES) * LANES, cap - wn), LANES)
            rolled = pltpu.roll(tok_ref[e, :, pl.ds(cs, wn)], (k * wf + wn - (start - cs)) % wn, 1)
            toks = jnp.where((lane >= k * wf) & (lane < (k + 1) * wf), rolled, toks)
        hit = (toks - j * tt) == sub
        acc = acc + _dot(jnp.where(hit, 1.0, 0.0).astype(BF16), jnp.concatenate(ys, axis=0))
    f_s[...] = acc

    def more_windows(e, carry):
        p0, p1 = slot_range(e)
        lo = _combine_first_start(p0, cap) + wf
        first = (lo // LANES) * LANES

        def extra(w, carry):
            cs = pl.multiple_of(jnp.minimum(first + w * wn, cap - wn), LANES)
            tok = tok_ref[e, :, pl.ds(cs, wn)] - j * tt
            hit = (tok == sub) & (cs + lane >= jnp.maximum(lo, first + w * wn))
            f_s[...] += _dot(jnp.where(hit, 1.0, 0.0).astype(BF16), y_ref[e, pl.ds(cs, wn), :])
            return carry

        lax.fori_loop(0, (jnp.maximum(p1 - first, 0) + wn - 1) // wn * (p1 > lo).astype(jnp.int32), extra, 0)
        return carry

    @pl.when(spill_ref[bi * ntiles + j] != 0)
    def _():
        lax.fori_loop(0, n_exp, more_windows, 0)

    o_ref[0] = x_ref[0] + gate_ref[0] * _rms(f_s[...], g_ref[...])


def _combine(y, tok, local, seg0, x, gate, g):
    e, _, d = y.shape
    _, b, cap = local.shape
    n_tok = x.shape[1]
    tt = COMBINE_TILE
    wf = COMBINE_FIRST
    ntiles = n_tok // tt
    assert seg0 % cap == 0 and n_tok % tt == 0 and cap % COMBINE_WINDOW == 0 and e % (COMBINE_WINDOW // wf) == 0
    seg = seg0 // cap
    bounds = jnp.arange(ntiles + 1, dtype=jnp.int32) * tt
    offs = jnp.sum((local[..., None] < bounds).astype(jnp.int32), axis=2)
    starts = _combine_first_start(offs[..., :-1], cap)
    spill = jnp.any(offs[..., 1:] > starts + wf, axis=0).astype(jnp.int32).reshape(-1)
    row = pl.BlockSpec((1, tt, d), lambda i, j, offs, spill: (i, j, 0))
    grid_spec = pltpu.PrefetchScalarGridSpec(
        num_scalar_prefetch=2,
        grid=(b, ntiles),
        in_specs=[pl.BlockSpec((e, 1, cap), lambda i, j, offs, spill: (0, 0, seg + i)),
                  pl.BlockSpec((e, cap, d), lambda i, j, offs, spill: (0, seg + i, 0),
                               pipeline_mode=pl.Buffered(1)),
                  row,
                  pl.BlockSpec((1, 1, d), lambda i, j, offs, spill: (i, 0, 0)),
                  pl.BlockSpec((1, d), lambda i, j, offs, spill: (0, 0))],
        out_specs=row,
        scratch_shapes=[pltpu.VMEM((tt, d), F32)],
    )
    return pl.pallas_call(
        functools.partial(_combine_body, n_tok=n_tok, cap=cap),
        grid_spec=grid_spec,
        out_shape=jax.ShapeDtypeStruct((b, n_tok, d), F32),
        compiler_params=_cparams(("parallel", "arbitrary")),
        name="combine",
    )(jnp.swapaxes(offs, 0, 1).reshape(-1), spill, tok, y, x, gate.reshape(b, 1, d), g.reshape(1, d))


def _gather_rows(src, idx):
    window = SC_GATHER_WINDOW
    n = idx.shape[0]
    width = src.shape[1]
    assert 2 * window * width * 4 <= SC_TILE_VMEM_BUDGET, width
    assert n % (window * SC_CORES * SC_SUBCORES) == 0, n
    mesh = plsc.VectorSubcoreMesh(core_axis_name="core", subcore_axis_name="subcore",
                                  num_cores=SC_CORES, num_subcores=SC_SUBCORES)

    @pl.kernel(out_type=jax.ShapeDtypeStruct((n, width), src.dtype), mesh=mesh, scratch_types=[],
               name="gather_rows")
    def gather(src_hbm, idx_hbm, out_hbm):
        def body(idx_vmem, out_vmem):
            pltpu.sync_copy(src_hbm.at[idx_vmem.at[0]], out_vmem)

        pltpu.emit_pipeline(
            body,
            grid=(n // window,),
            in_specs=[pl.BlockSpec((1, window), lambda i: (0, i))],
            out_specs=[pl.BlockSpec((window, width), lambda i: (i, 0))],
            core_axis_name=("core", "subcore"),
            dimension_semantics=(pltpu.PARALLEL,),
        )(idx_hbm, out_hbm)

    return gather(src, idx.reshape(1, n))


def _moe(parts, g, w_gate, w_up, w_down, layer):
    quarter = parts[0][1].shape[-1]
    d = 4 * quarter
    sizes = [p[1].shape[0] * p[1].shape[2] for p in parts]
    bases = [sum(sizes[:i]) for i in range(len(parts))]
    routed = [_route(p[0], base, 2 * base) for p, base in zip(parts, bases)]
    src = jnp.concatenate([p[1].reshape(-1, quarter) for p in parts], axis=0)
    vals, flat, rows0, rows1, tok = (jnp.concatenate([r[i] for r in routed], axis=1) for i in range(5))
    e, m = flat.shape
    rows = jnp.stack([rows0, rows1], axis=1).reshape(-1)
    unit = SC_GATHER_WINDOW * SC_CORES * SC_SUBCORES
    fill = jnp.arange(-rows.shape[0] % unit, dtype=rows.dtype)
    xg = _gather_rows(src, jnp.concatenate([rows, fill]))
    chunk = next(c for c in (512, 528, 384, 320, 256, 128) if m % c == 0)
    y = _expert_ffn(xg, vals[..., None], w_gate, w_up, w_down, layer, chunk)
    outs = []
    seg0 = 0
    for (aff_t, _, x, gate), r, base in zip(parts, routed, bases):
        b, _, n = aff_t.shape
        cap = r[0].shape[1] // b
        if n % COMBINE_TILE == 0 and cap % COMBINE_WINDOW == 0 and seg0 % cap == 0:
            outs.append(_combine(y, tok[:, None, :], r[4].reshape(e, b, cap), seg0, x, gate, g))
        else:
            ids = r[1] - base
            part = y[:, seg0:seg0 + b * cap].astype(F32)
            f = jnp.zeros((b * n, d), F32).at[ids.reshape(-1)].add(part.reshape(-1, d))
            outs.append(_gated_residual(x, f.reshape(b, n, d), gate, g, n))
        seg0 += b * cap
    return outs


def kernel(x, c, ctx, c_ctx, w_mod, b_mod, norm_g, w_in_even, w_out_even, conv_w, conv_b, conv_ln_g,
           conv_ln_b, na_rpb, w_in_odd, w_out_odd, pool_w, pool_scale, gla_gate_w, gla_gate_b, gla_head_g,
           router_w, expert_w_gate, expert_w_up, expert_w_down):
    b, l, d = x.shape
    n_ctx = ctx.shape[1]
    tm = ROW_TILE

    mod_rows = jnp.concatenate([c, c_ctx[None], jnp.zeros((8 - b - 1, d), F32)], axis=0)

    mod_all = _modulation(mod_rows, w_mod, b_mod)

    def modulation(i):
        mm = mod_all[i]
        m = mm[:b].reshape(b, 6, d)
        mc = jnp.broadcast_to(mm[b].reshape(1, 6, d), (b, 6, d))
        return m, mc

    m, mc = modulation(0)
    g = norm_g[0]
    w_in = w_in_even[0].astype(BF16)
    w_out = w_out_even[0].astype(BF16)
    glu, qkv = _even_in(x, g[0], m[:, 0], m[:, 1], w_in, tm)
    glu_c, qkv_c = _even_in(ctx, g[0], mc[:, 0], mc[:, 1], w_in, n_ctx)
    a_lat = _conv_branch(glu, conv_w[0], conv_b[0], conv_ln_g[0], conv_ln_b[0], SEQ_TILE)
    a_ctx = _conv_branch(glu_c, conv_w[0], conv_b[0], conv_ln_g[0], conv_ln_b[0], n_ctx)
    na = _neighbourhood_attention(qkv, qkv_c, na_rpb[0])
    att_c = _context_attention(qkv_c)
    x, h2, aff = _out_proj(a_lat, na, x, w_out, g[1], m[:, 2], g[2], m[:, 3], m[:, 4], router_w[0], tm)
    ctx, h2c, aff_c = _out_proj(a_ctx, att_c, ctx, w_out, g[1], mc[:, 2], g[2], mc[:, 3], mc[:, 4],
                                router_w[0], n_ctx)
    x, ctx = _moe([(aff, h2, x, m[:, 5]), (aff_c, h2c, ctx, mc[:, 5])], g[3],
                  expert_w_gate, expert_w_up, expert_w_down, 0)

    m, mc = modulation(1)
    g = norm_g[1]
    w_in = w_in_odd[0].astype(BF16)
    w_out = w_out_odd[0].astype(BF16)
    cos_t, sin_t = _rope_tables(l)
    ones_t = jnp.ones((n_ctx, GLA_HEADS * GLA_DK), F32)
    _, qk_c, v_c, _, gate_c = _odd_in(ctx, g[0], mc[:, 0], mc[:, 1], w_in, ones_t, jnp.zeros_like(ones_t),
                                      gla_gate_w[0], gla_gate_b[0], n_ctx)
    s_zero = jnp.zeros((b, GLA_HEADS, GLA_DK, GLA_DV), F32)
    _, _, s_f, s_b = _gla(qk_c, v_c, gate_c, s_zero, s_zero, n_ctx)
    pool_u, qk, v, r, gate = _odd_in(x, g[0], m[:, 0], m[:, 1], w_in, cos_t, sin_t,
                                     gla_gate_w[0], gla_gate_b[0], tm)
    o_f, o_b, _, _ = _gla(qk, v, gate, s_f, s_b, SEQ_TILE)
    pool_y, d_lat = _odd_mid(pool_u, o_f, o_b, r, gla_head_g[0], pool_w[0].astype(BF16), pool_scale[0],
                             SEQ_TILE)
    x, h2, aff = _out_proj(pool_y, d_lat, x, w_out, g[1], m[:, 2], g[2], m[:, 3], m[:, 4], router_w[1], tm)
    (x,) = _moe([(aff, h2, x, m[:, 5])], g[3], expert_w_gate, expert_w_up, expert_w_down, 1)
    return x
```

```python
import functools

import jax
import jax.numpy as jnp
from jax import lax
from jax.experimental import pallas as pl
from jax.experimental.pallas import tpu as pltpu
from jax.experimental.pallas import tpu_sc as plsc

F32 = jnp.float32
BF16 = jnp.bfloat16
HIGHEST = lax.Precision.HIGHEST

D_MODEL = 1024
GRID_W = 64
EPS = 1e-6
CONV_CH = 512
CONV_WIDTH = 31
CONV_HALO = 16
CONV_CHUNK = 32
NA_HEADS = 8
NA_HEAD_DIM = 64
NA_KR = 8
NA_KC = 16
NA_ROWS_PER_BLOCK = 4
NA_BLOCKS_PER_STEP = 2
NA_WIN_ROWS = 12
POOL_CH = 512
POOL_WINDOWS = (2, 4, 8, 16)
POOL_GROUP = 128
POOL_HALO = 8
GLA_HEADS = 4
GLA_DK = 64
GLA_DV = 128
GLA_RANK = 16
GLA_TAU = 16.0
GLA_CHUNK = 64
ROPE_BASE = 10000.0
N_EXPERTS = 16
EC_CAPACITY_FACTOR = 2
ROW_TILE = 512
SEQ_TILE = 256
MOD_COLS = 1536
EXPERT_FF_TILE = 256
LANES = 128
SUBLANES = 8
NEG_BIG = -1e30
VMEM_LIMIT = 56 * 1024 * 1024
SC_CORES = 2
SC_SUBCORES = 16
SC_LANES = 16
F32_INF_BITS = 0x7F800000
SC_TILE_VMEM_BUDGET = 400 * 1024
SC_GATHER_WINDOW = 128
COMBINE_TILE = 256
COMBINE_WINDOW = 256
COMBINE_FIRST = 64
BF16_ROWS = 16


def _cparams(sem):
    return pltpu.CompilerParams(dimension_semantics=sem, vmem_limit_bytes=VMEM_LIMIT)


def _rms(x, g):
    return x * lax.rsqrt(jnp.mean(x * x, axis=-1, keepdims=True) + EPS) * g


def _sigmoid(x):
    return 1.0 / (1.0 + jnp.exp(-x))


def _silu(x):
    return x * _sigmoid(x)


def _dot(a, b):
    return jnp.dot(a, b, preferred_element_type=F32)


def _pack_bf16_pairs(h):
    half = h.shape[-1] // 2
    bits = lax.bitcast_convert_type(h.astype(BF16).astype(F32), jnp.uint32)
    packed = (bits[:, half:] & jnp.uint32(0xFFFF0000)) | (bits[:, :half] >> 16)
    return lax.bitcast_convert_type(packed, jnp.int32)


def _unpack_bf16_pairs(p):
    bits = lax.bitcast_convert_type(p, jnp.uint32)
    lo = lax.bitcast_convert_type(bits << 16, F32).astype(BF16)
    hi = lax.bitcast_convert_type(bits & jnp.uint32(0xFFFF0000), F32).astype(BF16)
    return lo, hi


def _dot_nt(a, b):
    return lax.dot_general(a, b, (((1,), (1,)), ((), ())), preferred_element_type=F32)


def _mod_body(c_ref, w_ref, b_ref, o_ref):
    o_ref[0] = jnp.dot(_silu(c_ref[...]), w_ref[0], precision=HIGHEST,
                       preferred_element_type=F32) + b_ref[0]


def _modulation(rows, w, b):
    depth, _, n = w.shape
    tn = MOD_COLS
    return pl.pallas_call(
        _mod_body,
        grid=(depth, n // tn),
        in_specs=[pl.BlockSpec((8, D_MODEL), lambda i, j: (0, 0)),
                  pl.BlockSpec((1, D_MODEL, tn), lambda i, j: (i, 0, j)),
                  pl.BlockSpec((1, 1, tn), lambda i, j: (i, 0, j))],
        out_specs=pl.BlockSpec((1, 8, tn), lambda i, j: (i, 0, j)),
        out_shape=jax.ShapeDtypeStruct((depth, 8, n), F32),
        compiler_params=_cparams(("parallel", "parallel")),
        name="modulation",
    )(rows, w, b.reshape(depth, 1, n))


def _even_in_body(x_ref, g_ref, sh_ref, sc_ref, w_ref, glu_ref, qkv_ref):
    h = (_rms(x_ref[0], g_ref[...]) * (1.0 + sc_ref[0]) + sh_ref[0]).astype(BF16)
    c = CONV_CH
    glu_ref[0] = _dot(h, w_ref[:, 0:c]) * _sigmoid(_dot(h, w_ref[:, c:2 * c]))
    hd = NA_HEADS * NA_HEAD_DIM
    q0 = 2 * c
    qkv_ref[0, :, 0:hd] = (_dot(h, w_ref[:, q0:q0 + hd]) * (NA_HEAD_DIM ** -0.5)).astype(BF16)
    qkv_ref[0, :, hd:3 * hd] = _dot(h, w_ref[:, q0 + hd:q0 + 3 * hd]).astype(BF16)


def _even_in(x, g, shift, scale, w_bf, tm):
    b, l, d = x.shape
    n = w_bf.shape[1]
    hd3 = 3 * NA_HEADS * NA_HEAD_DIM
    vec = pl.BlockSpec((1, 1, d), lambda i, j: (i, 0, 0))
    return pl.pallas_call(
        _even_in_body,
        grid=(b, l // tm),
        in_specs=[pl.BlockSpec((1, tm, d), lambda i, j: (i, j, 0)),
                  pl.BlockSpec((1, d), lambda i, j: (0, 0)),
                  vec, vec,
                  pl.BlockSpec((d, n), lambda i, j: (0, 0))],
        out_specs=[pl.BlockSpec((1, tm, CONV_CH), lambda i, j: (i, j, 0)),
                   pl.BlockSpec((1, tm, hd3), lambda i, j: (i, j, 0))],
        out_shape=[jax.ShapeDtypeStruct((b, l, CONV_CH), F32),
                   jax.ShapeDtypeStruct((b, l, hd3), BF16)],
        compiler_params=_cparams(("parallel", "parallel")),
        name="even_in",
    )(x, g.reshape(1, d), shift.reshape(b, 1, d), scale.reshape(b, 1, d), w_bf)


def _conv_body(cur_ref, prev_ref, next_ref, w_ref, b_ref, lg_ref, lb_ref, o_ref, buf_ref, sh_ref, *, tile, chunk):
    j = pl.program_id(1)
    last = pl.num_programs(1) - 1
    hal = CONV_HALO
    buf_ref[0:hal, :] = jnp.where(j == 0, 0.0, prev_ref[0])
    buf_ref[hal:hal + tile, :] = cur_ref[0]
    buf_ref[hal + tile:hal + tile + hal, :] = jnp.where(j == last, 0.0, next_ref[0])
    span = sh_ref.shape[1]
    for s in range(SUBLANES):
        sh_ref[s] = buf_ref[s:s + span, :]
    first = hal - CONV_WIDTH // 2
    reps = chunk // SUBLANES

    def rows(c, carry):
        r0 = pl.multiple_of(c * chunk, chunk)
        accs = [jnp.zeros((chunk, CONV_CH), F32) for _ in range(2)]
        for k in range(CONV_WIDTH):
            a, s = divmod(first + k, SUBLANES)
            wk = jnp.concatenate([w_ref[k]] * reps, axis=0)
            accs[k % 2] = accs[k % 2] + sh_ref[s, pl.ds(r0 + a * SUBLANES, chunk), :] * wk
        o_ref[0, pl.ds(r0, chunk), :] = accs[0] + accs[1]
        return carry

    lax.fori_loop(0, tile // chunk, rows, 0)
    y = o_ref[0] + b_ref[...]
    mu = jnp.mean(y, axis=-1, keepdims=True)
    yc = y - mu
    var = jnp.mean(yc * yc, axis=-1, keepdims=True)
    o_ref[0] = _silu(yc * lax.rsqrt(var + EPS) * lg_ref[...] + lb_ref[...])


def _conv_operands(glu, conv_w, conv_b, ln_g, ln_b, tile):
    _, l, c = glu.shape
    hal = CONV_HALO
    per = tile // hal
    nh = l // hal
    vec = pl.BlockSpec((1, c), lambda i, j: (0, 0))
    specs = [pl.BlockSpec((1, tile, c), lambda i, j: (i, j, 0)),
             pl.BlockSpec((1, hal, c), lambda i, j: (i, jnp.maximum(j * per - 1, 0), 0)),
             pl.BlockSpec((1, hal, c), lambda i, j: (i, jnp.minimum((j + 1) * per, nh - 1), 0)),
             pl.BlockSpec((CONV_WIDTH, SUBLANES, c), lambda i, j: (0, 0, 0)),
             vec, vec, vec]
    args = [glu, glu, glu, jnp.broadcast_to(conv_w[:, None, :], (CONV_WIDTH, SUBLANES, c)),
            conv_b.reshape(1, c), ln_g.reshape(1, c), ln_b.reshape(1, c)]
    scratch = [pltpu.VMEM((tile + 2 * hal, c), F32),
               pltpu.VMEM((SUBLANES, tile + 2 * hal - SUBLANES, c), F32)]
    return args, specs, scratch


def _conv_branch(glu, conv_w, conv_b, ln_g, ln_b, tile):
    b, l, c = glu.shape
    args, specs, scratch = _conv_operands(glu, conv_w, conv_b, ln_g, ln_b, tile)
    return pl.pallas_call(
        functools.partial(_conv_body, tile=tile, chunk=CONV_CHUNK),
        grid=(b, l // tile),
        in_specs=specs,
        out_specs=pl.BlockSpec((1, tile, c), lambda i, j: (i, j, 0)),
        out_shape=jax.ShapeDtypeStruct((b, l, c), F32),
        scratch_shapes=scratch,
        compiler_params=_cparams(("parallel", "parallel")),
        name="conv_branch",
    )(*args)


def _na_window_start(j, rows):
    rb = NA_ROWS_PER_BLOCK
    return jnp.clip(j * rb - NA_KR // 2, 0, rows - NA_WIN_ROWS)


def _na_body(q_ref, k_ref, v_ref, kc_ref, vc_ref, *rest, rows):
    tab_refs, o_ref = rest[:-1], rest[-1]
    j = pl.program_id(2)
    nkeys = NA_WIN_ROWS * GRID_W
    tq = NA_ROWS_PER_BLOCK * GRID_W
    kc = kc_ref[0]
    vc = vc_ref[0]
    lane = lax.broadcasted_iota(jnp.int32, (1, LANES), 1)
    for sb, tab_ref in enumerate(tab_refs):
        start = pl.multiple_of(_na_window_start(j * len(tab_refs) + sb, rows) * GRID_W, GRID_W)
        q = q_ref[0, sb * tq:(sb + 1) * tq, :]
        kw = k_ref[0, pl.ds(start, nkeys), :]
        vw = v_ref[0, pl.ds(start, nkeys), :]
        out = jnp.zeros(q.shape, F32)
        for hh in range(LANES // NA_HEAD_DIM):
            in_head = (lane >= hh * NA_HEAD_DIM) & (lane < (hh + 1) * NA_HEAD_DIM)
            qh = jnp.where(in_head, q, jnp.zeros_like(q))
            s = _dot_nt(qh, kw) + tab_ref[0, hh]
            sc = _dot_nt(qh, kc)
            m = jnp.maximum(jnp.max(s, axis=-1, keepdims=True), jnp.max(sc, axis=-1, keepdims=True))
            p = jnp.exp(s - m)
            pc = jnp.exp(sc - m)
            denom = jnp.sum(p, axis=-1, keepdims=True) + jnp.sum(pc, axis=-1, keepdims=True)
            o = (_dot(p.astype(BF16), vw) + _dot(pc.astype(BF16), vc)) / denom
            out = jnp.where(in_head, o, out)
        o_ref[0, sb * tq:(sb + 1) * tq, :] = out


def _na_tables(rpb, rows):
    rb = NA_ROWS_PER_BLOCK
    nblk = rows // rb
    wr = NA_WIN_ROWS
    qc = jnp.arange(GRID_W)
    cs = jnp.clip(qc - NA_KC // 2, 0, GRID_W - NA_KC)
    col_ok = (qc[None, :] >= cs[:, None]) & (qc[None, :] < cs[:, None] + NA_KC)
    col_off = qc[None, :] - qc[:, None] + NA_KC - 1
    onehot = (col_off[:, :, None] == jnp.arange(2 * NA_KC - 1)[None, None, :]).astype(F32)
    blocks = jnp.einsum('hrd,qkd->hrqk', rpb.astype(F32), onehot, precision=HIGHEST)
    blocks = jnp.where(col_ok[None, None], blocks, NEG_BIG)
    masked = jnp.full((NA_HEADS, GRID_W, GRID_W), NEG_BIG, F32)
    tabs = []
    for jb in (0, 1, nblk - 1):
        ws = min(max(jb * rb - NA_KR // 2, 0), rows - wr)
        q_rows = []
        for qr in range(jb * rb, (jb + 1) * rb):
            rs = min(max(qr - NA_KR // 2, 0), rows - NA_KR)
            row = [blocks[:, kr - qr + NA_KR - 1] if rs <= kr < rs + NA_KR else masked
                   for kr in range(ws, ws + wr)]
            q_rows.append(jnp.concatenate(row, axis=-1))
        tabs.append(jnp.concatenate(q_rows, axis=1))
    return jnp.stack(tabs)


def _neighbourhood_attention(qkv, qkv_ctx, rpb):
    b, l, _ = qkv.shape
    n_ctx = qkv_ctx.shape[1]
    rows = l // GRID_W
    rb = NA_ROWS_PER_BLOCK
    nblk = rows // rb
    tq = rb * GRID_W
    nkeys = NA_WIN_ROWS * GRID_W
    hp = NA_HEADS * NA_HEAD_DIM // LANES
    tabs = _na_tables(rpb, rows)

    per = NA_BLOCKS_PER_STEP

    def cls(jb):
        return jnp.where(jb == 0, 0, jnp.where(jb == nblk - 1, 2, 1))

    def tab_spec(sb):
        return pl.BlockSpec((1, 2, tq, nkeys), lambda i, h, j: (cls(j * per + sb), h, 0, 0))

    return pl.pallas_call(
        functools.partial(_na_body, rows=rows),
        grid=(b, hp, nblk // per),
        in_specs=[pl.BlockSpec((1, per * tq, LANES), lambda i, h, j: (i, j, h)),
                  pl.BlockSpec((1, l, LANES), lambda i, h, j: (i, 0, hp + h)),
                  pl.BlockSpec((1, l, LANES), lambda i, h, j: (i, 0, 2 * hp + h)),
                  pl.BlockSpec((1, n_ctx, LANES), lambda i, h, j: (i, 0, hp + h)),
                  pl.BlockSpec((1, n_ctx, LANES), lambda i, h, j: (i, 0, 2 * hp + h))]
                 + [tab_spec(sb) for sb in range(per)],
        out_specs=pl.BlockSpec((1, per * tq, LANES), lambda i, h, j: (i, j, h)),
        out_shape=jax.ShapeDtypeStruct((b, l, NA_HEADS * NA_HEAD_DIM), F32),
        compiler_params=_cparams(("parallel", "parallel", "arbitrary")),
        name="neighbourhood_attention",
    )(qkv, qkv, qkv, qkv_ctx, qkv_ctx, *([tabs] * per))


def _ctx_attn_body(q_ref, k_ref, v_ref, o_ref):
    q = q_ref[0]
    k = k_ref[0]
    v = v_ref[0]
    lane = lax.broadcasted_iota(jnp.int32, (1, LANES), 1)
    out = jnp.zeros(q.shape, F32)
    for hh in range(LANES // NA_HEAD_DIM):
        in_head = (lane >= hh * NA_HEAD_DIM) & (lane < (hh + 1) * NA_HEAD_DIM)
        qh = jnp.where(in_head, q, jnp.zeros_like(q))
        s = _dot_nt(qh, k)
        p = jnp.exp(s - jnp.max(s, axis=-1, keepdims=True))
        o = _dot(p.astype(BF16), v) / jnp.sum(p, axis=-1, keepdims=True)
        out = jnp.where(in_head, o, out)
    o_ref[0] = out


def _context_attention(qkv_ctx):
    b, n, _ = qkv_ctx.shape
    hp = NA_HEADS * NA_HEAD_DIM // LANES
    return pl.pallas_call(
        _ctx_attn_body,
        grid=(b, hp),
        in_specs=[pl.BlockSpec((1, n, LANES), lambda i, h: (i, 0, h)),
                  pl.BlockSpec((1, n, LANES), lambda i, h: (i, 0, hp + h)),
                  pl.BlockSpec((1, n, LANES), lambda i, h: (i, 0, 2 * hp + h))],
        out_specs=pl.BlockSpec((1, n, LANES), lambda i, h: (i, 0, h)),
        out_shape=jax.ShapeDtypeStruct((b, n, NA_HEADS * NA_HEAD_DIM), F32),
        compiler_params=_cparams(("parallel", "parallel")),
        name="context_attention",
    )(qkv_ctx, qkv_ctx, qkv_ctx)


def _out_body(a_ref, b_ref, *rest):
    _out_tail(a_ref[0], b_ref[0], *rest)


def _out_tail(a, b2, x_ref, w_ref, g1_ref, gate_ref, g2_ref, sh_ref, sc_ref, rw_ref, xo_ref, h_ref, aff_ref):
    half = a.shape[-1]
    y = _dot(a.astype(BF16), w_ref[0:half, :]) + _dot(b2.astype(BF16), w_ref[half:2 * half, :])
    xn = x_ref[0] + gate_ref[0] * _rms(y, g1_ref[...])
    xo_ref[0] = xn
    h = _rms(xn, g2_ref[...]) * (1.0 + sc_ref[0]) + sh_ref[0]
    packed = _pack_bf16_pairs(h)
    quarter = packed.shape[-1] // 2
    h_ref[0, 0] = packed[:, 0:quarter]
    h_ref[0, 1] = packed[:, quarter:2 * quarter]
    h_hi = h.astype(BF16)
    h_lo = (h - h_hi.astype(F32)).astype(BF16)
    both = _dot(h_hi, rw_ref[...])
    logits = both[:, 0:LANES] + (both[:, LANES:2 * LANES] + _dot(h_lo, rw_ref[:, 0:LANES]))
    lane = lax.broadcasted_iota(jnp.int32, (1, LANES), 1)
    logits = jnp.where(lane < N_EXPERTS, logits, NEG_BIG)
    e = jnp.exp(logits - jnp.max(logits, axis=-1, keepdims=True))
    aff = e / jnp.sum(e, axis=-1, keepdims=True)
    aff_ref[0] = aff.T[0:N_EXPERTS, :]


def _out_call(body, branch_args, branch_specs, scratch, name, x, w_bf, g1, gate, g2, shift, scale, router_w, tm):
    b, l, d = x.shape
    rw = jnp.pad(router_w, ((0, 0), (0, LANES - N_EXPERTS)))
    rw_hi = rw.astype(BF16)
    rw_cat = jnp.concatenate([rw_hi, (rw - rw_hi.astype(F32)).astype(BF16)], axis=1)
    vec = pl.BlockSpec((1, d), lambda i, j: (0, 0))
    bvec = pl.BlockSpec((1, 1, d), lambda i, j: (i, 0, 0))
    rspec = pl.BlockSpec((d, 2 * LANES), lambda i, j: (0, 0))
    return pl.pallas_call(
        body,
        grid=(b, l // tm),
        in_specs=list(branch_specs) + [pl.BlockSpec((1, tm, d), lambda i, j: (i, j, 0)),
                                       pl.BlockSpec(w_bf.shape, lambda i, j: (0, 0)),
                                       vec, bvec, vec, bvec, bvec, rspec],
        out_specs=[pl.BlockSpec((1, tm, d), lambda i, j: (i, j, 0)),
                   pl.BlockSpec((1, 2, tm, d // 4), lambda i, j: (i, 0, j, 0)),
                   pl.BlockSpec((1, N_EXPERTS, tm), lambda i, j: (i, 0, j))],
        out_shape=[jax.ShapeDtypeStruct((b, l, d), F32),
                   jax.ShapeDtypeStruct((b, 2, l, d // 4), jnp.int32),
                   jax.ShapeDtypeStruct((b, N_EXPERTS, l), F32)],
        scratch_shapes=list(scratch),
        compiler_params=_cparams(("parallel", "parallel")),
        name=name,
    )(*branch_args, x, w_bf, g1.reshape(1, d), gate.reshape(b, 1, d), g2.reshape(1, d),
      shift.reshape(b, 1, d), scale.reshape(b, 1, d), rw_cat)


def _even_out_body(*refs, tile):
    n_conv = 7
    buf_ref, sh_ref, a_s = refs[-3:]
    _conv_body(*refs[:n_conv], a_s, buf_ref, sh_ref, tile=tile, chunk=CONV_CHUNK)
    _out_tail(a_s[0], refs[n_conv][0], *refs[n_conv + 1:-3])


def _even_out(glu, conv_w, conv_b, ln_g, ln_b, na, *common):
    tile = common[-1]
    c = glu.shape[-1]
    args, specs, scratch = _conv_operands(glu, conv_w, conv_b, ln_g, ln_b, tile)
    na_spec = pl.BlockSpec((1, tile, na.shape[-1]), lambda i, j: (i, j, 0))
    return _out_call(functools.partial(_even_out_body, tile=tile), args + [na], specs + [na_spec],
                     scratch + [pltpu.VMEM((1, tile, c), F32)], "even_out", *common)


def _out_proj(a, b2, *common):
    tm = common[-1]
    spec = pl.BlockSpec((1, tm, a.shape[-1]), lambda i, j: (i, j, 0))
    return _out_call(_out_body, [a, b2], [spec, spec], [], "out_proj", *common)


def _moe_body(x0_ref, x1_ref, val_ref, wg_ref, wu_ref, wd_ref, o_ref, acc_s, x_s, wg_s, wu_s, wd_s, *, chunk):
    f = pl.program_id(1)
    m, quarter = x0_ref.shape

    @pl.when(f == 0)
    def _():
        def unpack(c, carry):
            r = pl.multiple_of(c * chunk, chunk)
            for s, x_ref in enumerate((x0_ref, x1_ref)):
                lo, hi = _unpack_bf16_pairs(x_ref[pl.ds(r, chunk), :])
                x_s[pl.ds(r, chunk), s * quarter:(s + 1) * quarter] = lo
                x_s[pl.ds(r, chunk), (2 + s) * quarter:(3 + s) * quarter] = hi
            acc_s[pl.ds(r, chunk), :] = jnp.zeros((chunk, 4 * quarter), F32)
            return carry

        lax.fori_loop(0, m // chunk, unpack, 0)

    wg_s[...] = wg_ref[0, 0].astype(BF16)
    wu_s[...] = wu_ref[0, 0].astype(BF16)
    wd_s[...] = wd_ref[0, 0].astype(BF16)

    def rows(c, carry):
        r = pl.multiple_of(c * chunk, chunk)
        xs = x_s[pl.ds(r, chunk), :]
        hid = (_silu(_dot(xs, wg_s[...])) * _dot(xs, wu_s[...])).astype(BF16)
        acc_s[pl.ds(r, chunk), :] += _dot(hid, wd_s[...])
        return carry

    lax.fori_loop(0, m // chunk, rows, 0, unroll=True)

    @pl.when(f == pl.num_programs(1) - 1)
    def _():
        o_ref[0] = (acc_s[...] * val_ref[0]).astype(o_ref.dtype)


def _expert_ffn(xg, vals, w_gate, w_up, w_down, layer, chunk):
    e, m, _ = vals.shape
    quarter = xg.shape[1]
    d = 4 * quarter
    ff = w_gate.shape[-1]
    tf = EXPERT_FF_TILE
    return pl.pallas_call(
        functools.partial(_moe_body, chunk=chunk),
        grid=(e, ff // tf),
        in_specs=[pl.BlockSpec((m, quarter), lambda i, f: (2 * i, 0)),
                  pl.BlockSpec((m, quarter), lambda i, f: (2 * i + 1, 0)),
                  pl.BlockSpec((1, m, 1), lambda i, f: (i, 0, 0)),
                  pl.BlockSpec((1, 1, d, tf), lambda i, f: (layer, i, 0, f)),
                  pl.BlockSpec((1, 1, d, tf), lambda i, f: (layer, i, 0, f)),
                  pl.BlockSpec((1, 1, tf, d), lambda i, f: (layer, i, f, 0))],
        out_specs=pl.BlockSpec((1, m, d), lambda i, f: (i, 0, 0)),
        out_shape=jax.ShapeDtypeStruct((e, m, d), BF16),
        scratch_shapes=[pltpu.VMEM((m, d), F32), pltpu.VMEM((m, d), BF16), pltpu.VMEM((d, tf), BF16),
                        pltpu.VMEM((d, tf), BF16), pltpu.VMEM((tf, d), BF16)],
        compiler_params=_cparams(("parallel", "arbitrary")),
        name="expert_ffn",
    )(xg, xg, vals, w_gate, w_up, w_down)


def _resid_body(x_ref, f_ref, gate_ref, g_ref, o_ref):
    o_ref[0] = x_ref[0] + gate_ref[0] * _rms(f_ref[0], g_ref[...])


def _gated_residual(x, f, gate, g, tm):
    b, l, d = x.shape
    blk = pl.BlockSpec((1, tm, d), lambda i, j: (i, j, 0))
    return pl.pallas_call(
        _resid_body,
        grid=(b, l // tm),
        in_specs=[blk, blk, pl.BlockSpec((1, 1, d), lambda i, j: (i, 0, 0)),
                  pl.BlockSpec((1, d), lambda i, j: (0, 0))],
        out_specs=blk,
        out_shape=jax.ShapeDtypeStruct((b, l, d), F32),
        compiler_params=_cparams(("parallel", "parallel")),
        name="gated_residual",
    )(x, f, gate.reshape(b, 1, d), g.reshape(1, d))


def _swap_pairs(x):
    nf = GLA_DK // 4
    lane = lax.broadcasted_iota(jnp.int32, (1, LANES), 1)
    up = pltpu.roll(x, LANES - nf, 1)
    down = pltpu.roll(x, nf, 1)
    return jnp.where(lane % (2 * nf) < nf, up, down)


def _odd_in_body(x_ref, g_ref, sh_ref, sc_ref, w_ref, cos_ref, sin_ref, gw_ref, gb_ref,
                 pool_ref, qk_ref, v_ref, r_ref, gate_ref):
    h = (_rms(x_ref[0], g_ref[...]) * (1.0 + sc_ref[0]) + sh_ref[0]).astype(BF16)
    qk = GLA_HEADS * GLA_DK
    vd = GLA_HEADS * GLA_DV
    q0 = POOL_CH
    v0 = q0 + 2 * qk
    r0 = v0 + vd
    l0 = r0 + vd
    pool_ref[0] = _dot(h, w_ref[:, 0:q0])
    qk_raw = _dot(h, w_ref[:, q0:v0])
    for s in range(2 * qk // LANES):
        raw = qk_raw[:, s * LANES:(s + 1) * LANES]
        c = cos_ref[:, (s * LANES) % qk:(s * LANES) % qk + LANES]
        sn = sin_ref[:, (s * LANES) % qk:(s * LANES) % qk + LANES]
        rot = raw * c + _swap_pairs(raw) * sn
        if s * LANES < qk:
            rot = rot * (GLA_DK ** -0.5)
        qk_ref[0, :, s * LANES:(s + 1) * LANES] = rot
    v_ref[0] = _dot(h, w_ref[:, v0:r0]).astype(BF16)
    r_ref[0] = _dot(h, w_ref[:, r0:l0])
    lr = _dot(h, w_ref[:, l0:l0 + 2 * GLA_RANK])
    z = jnp.dot(lr, gw_ref[...], precision=HIGHEST, preferred_element_type=F32) + gb_ref[...]
    gate_ref[0] = (jnp.minimum(z, 0.0) - jnp.log1p(jnp.exp(-jnp.abs(z)))) * (1.0 / GLA_TAU)


def _odd_in(x, g, shift, scale, w_bf, cos_t, sin_t, gate_w, gate_b, tm):
    b, l, d = x.shape
    n = w_bf.shape[1]
    qk = GLA_HEADS * GLA_DK
    vd = GLA_HEADS * GLA_DV
    gw = jnp.zeros((2 * GLA_RANK, 2 * qk), F32)
    gw = gw.at[:GLA_RANK, :qk].set(gate_w[0]).at[GLA_RANK:, qk:].set(gate_w[1])
    gb = jnp.concatenate([gate_b[0], gate_b[1]]).reshape(1, 2 * qk)
    vec = pl.BlockSpec((1, 1, d), lambda i, j: (i, 0, 0))
    row = lambda w: pl.BlockSpec((1, tm, w), lambda i, j: (i, j, 0))
    return pl.pallas_call(
        _odd_in_body,
        grid=(b, l // tm),
        in_specs=[row(d), pl.BlockSpec((1, d), lambda i, j: (0, 0)), vec, vec,
                  pl.BlockSpec((d, n), lambda i, j: (0, 0)),
                  pl.BlockSpec((tm, qk), lambda i, j: (j, 0)),
                  pl.BlockSpec((tm, qk), lambda i, j: (j, 0)),
                  pl.BlockSpec((2 * GLA_RANK, 2 * qk), lambda i, j: (0, 0)),
                  pl.BlockSpec((1, 2 * qk), lambda i, j: (0, 0))],
        out_specs=[row(POOL_CH), row(2 * qk), row(vd), row(vd), row(2 * qk)],
        out_shape=[jax.ShapeDtypeStruct((b, l, POOL_CH), F32),
                   jax.ShapeDtypeStruct((b, l, 2 * qk), F32),
                   jax.ShapeDtypeStruct((b, l, vd), BF16),
                   jax.ShapeDtypeStruct((b, l, vd), F32),
                   jax.ShapeDtypeStruct((b, l, 2 * qk), F32)],
        compiler_params=_cparams(("parallel", "parallel")),
        name="odd_in",
    )(x, g.reshape(1, d), shift.reshape(b, 1, d), scale.reshape(b, 1, d), w_bf, cos_t, sin_t, gw, gb)


def _rope_tables(l):
    t = jnp.arange(l)
    pos_r = (t // GRID_W).astype(F32)
    pos_c = (t % GRID_W).astype(F32)
    nf = GLA_DK // 4
    inv = jnp.power(ROPE_BASE, -jnp.arange(nf, dtype=F32) / nf)
    ar = pos_r[:, None] * inv[None, :]
    ac = pos_c[:, None] * inv[None, :]
    cos_h = jnp.concatenate([jnp.cos(ar), jnp.cos(ar), jnp.cos(ac), jnp.cos(ac)], axis=-1)
    sin_h = jnp.concatenate([-jnp.sin(ar), jnp.sin(ar), -jnp.sin(ac), jnp.sin(ac)], axis=-1)
    return jnp.tile(cos_h, (1, GLA_HEADS)), jnp.tile(sin_h, (1, GLA_HEADS))


def _gla_tile(qk, v, g, s, reverse):
    hk = GLA_HEADS * GLA_DK
    hv = GLA_HEADS * GLA_DV
    c = GLA_CHUNK
    t = qk.shape[0]
    n = t // c
    last_row, mid_row = (0, c // 2) if reverse else (c - 1, c // 2 - 1)
    ii = lax.broadcasted_iota(jnp.int32, (t, t), 0)
    jj = lax.broadcasted_iota(jnp.int32, (t, t), 1)
    ordered = (jj >= ii) if reverse else (jj <= ii)
    tri = jnp.where(ordered & (ii // c == jj // c), 1.0, 0.0).astype(BF16)
    g_hi = g.astype(BF16)
    rem = g - g_hi.astype(F32)
    g_mid = rem.astype(BF16)
    g_lo = (rem - g_mid.astype(F32)).astype(BF16)
    bc = _dot(tri, g_hi) + (_dot(tri, g_mid) + _dot(tri, g_lo))
    spread = lambda row: jnp.concatenate(
        [jnp.broadcast_to(bc[i * c + row:i * c + row + 1, :], (c, hk)) for i in range(n)], axis=0)
    b_mid = spread(mid_row)
    b_last = spread(last_row)
    qt = qk[:, 0:hk] * jnp.exp(bc - b_mid)
    kt = qk[:, hk:2 * hk] * jnp.exp(b_mid - bc)
    qe = (qt * jnp.exp(b_mid)).astype(BF16)
    ke = kt * jnp.exp(b_last - b_mid)
    ktb = kt.astype(BF16)
    lane = lax.broadcasted_iota(jnp.int32, (1, hk), 1)
    ci = lax.broadcasted_iota(jnp.int32, (c, c), 0)
    cj = lax.broadcasted_iota(jnp.int32, (c, c), 1)
    causal = (cj >= ci) if reverse else (cj <= ci)
    blockdiag = (lax.broadcasted_iota(jnp.int32, (hk, hv), 0) // GLA_DK
                 == lax.broadcasted_iota(jnp.int32, (hk, hv), 1) // GLA_DV)
    intra, upd, decay = [], [], []
    for i in range(n):
        rows = slice(i * c, (i + 1) * c)
        qs = jnp.concatenate(
            [jnp.where((lane >= h * GLA_DK) & (lane < (h + 1) * GLA_DK), qt[rows], 0.0) for h in range(GLA_HEADS)],
            axis=0).astype(BF16)
        att = _dot_nt(qs, ktb[rows])
        intra.append(jnp.concatenate(
            [_dot(jnp.where(causal, att[h * c:(h + 1) * c], 0.0).astype(BF16),
                  v[rows, h * GLA_DV:(h + 1) * GLA_DV]) for h in range(GLA_HEADS)], axis=-1))
        upd.append(jnp.where(blockdiag, _dot(ke[rows].T.astype(BF16), v[rows]), 0.0))
        decay.append(jnp.exp(jnp.sum(g[rows].T, axis=1, keepdims=True)))
    outs = [None] * n
    for i in (reversed(range(n)) if reverse else range(n)):
        rows = slice(i * c, (i + 1) * c)
        outs[i] = _dot(qe[rows], s.astype(BF16)) + intra[i]
        s = decay[i] * s + upd[i]
    return jnp.concatenate(outs, axis=0), s


def _gla_body(qkf_ref, qkb_ref, vf_ref, vb_ref, gf_ref, gb_ref, s0f_ref, s0b_ref,
              of_ref, ob_ref, sff_ref, sbf_ref, sf_ref, sb_ref, *, tile):
    n = pl.program_id(1)
    hk = GLA_HEADS * GLA_DK
    hv = GLA_HEADS * GLA_DV

    @pl.when(n == 0)
    def _():
        sf_ref[...] = jnp.zeros((hk, hv), F32)
        sb_ref[...] = jnp.zeros((hk, hv), F32)
        for h in range(GLA_HEADS):
            sf_ref[h * GLA_DK:(h + 1) * GLA_DK, h * GLA_DV:(h + 1) * GLA_DV] = s0f_ref[0, h]
            sb_ref[h * GLA_DK:(h + 1) * GLA_DK, h * GLA_DV:(h + 1) * GLA_DV] = s0b_ref[0, h]

    of_ref[0], sf_ref[...] = _gla_tile(qkf_ref[0], vf_ref[0], gf_ref[0], sf_ref[...], False)
    ob_ref[0], sb_ref[...] = _gla_tile(qkb_ref[0], vb_ref[0], gb_ref[0], sb_ref[...], True)

    @pl.when(n == pl.num_programs(1) - 1)
    def _():
        for h in range(GLA_HEADS):
            sff_ref[0, h] = sf_ref[h * GLA_DK:(h + 1) * GLA_DK, h * GLA_DV:(h + 1) * GLA_DV]
            sbf_ref[0, h] = sb_ref[h * GLA_DK:(h + 1) * GLA_DK, h * GLA_DV:(h + 1) * GLA_DV]


def _gla(qk, v, gates, s0f, s0b, tile):
    b, l, _ = qk.shape
    hk = GLA_HEADS * GLA_DK
    hv = GLA_HEADS * GLA_DV
    nt = l // tile
    fwd = lambda w, col: pl.BlockSpec((1, tile, w), lambda i, n: (i, n, col))
    bwd = lambda w, col: pl.BlockSpec((1, tile, w), lambda i, n: (i, nt - 1 - n, col))
    st = pl.BlockSpec((1, GLA_HEADS, GLA_DK, GLA_DV), lambda i, n: (i, 0, 0, 0))
    return pl.pallas_call(
        functools.partial(_gla_body, tile=tile),
        grid=(b, nt),
        in_specs=[fwd(2 * hk, 0), bwd(2 * hk, 0), fwd(hv, 0), bwd(hv, 0), fwd(hk, 0), bwd(hk, 1), st, st],
        out_specs=[fwd(hv, 0), bwd(hv, 0), st, st],
        out_shape=[jax.ShapeDtypeStruct((b, l, hv), F32), jax.ShapeDtypeStruct((b, l, hv), F32),
                   jax.ShapeDtypeStruct((b, GLA_HEADS, GLA_DK, GLA_DV), F32),
                   jax.ShapeDtypeStruct((b, GLA_HEADS, GLA_DK, GLA_DV), F32)],
        scratch_shapes=[pltpu.VMEM((hk, hv), F32), pltpu.VMEM((hk, hv), F32)],
        compiler_params=_cparams(("parallel", "arbitrary")),
        name="gla_scan",
    )(qk, qk, v, v, gates, gates, s0f, s0b)


def _odd_mid_body(cur_ref, prev_ref, next_ref, of_ref, ob_ref, r_ref, hg_ref, pw_ref, ps_ref,
                  pool_ref, d_ref, buf_ref, *, tile, seq):
    j = pl.program_id(1)
    last = pl.num_programs(1) - 1
    hal = POOL_HALO
    buf_ref[0:hal, :] = jnp.where(j == 0, 0.0, prev_ref[0])
    buf_ref[hal:hal + tile, :] = cur_ref[0]
    buf_ref[hal + tile:hal + tile + hal, :] = jnp.where(j == last, 0.0, next_ref[0])
    t = j * tile + lax.broadcasted_iota(jnp.int32, (tile, 1), 0)
    for gi, win in enumerate(POOL_WINDOWS):
        cols = slice(gi * POOL_GROUP, (gi + 1) * POOL_GROUP)
        acc = jnp.zeros((tile, POOL_GROUP), F32)
        for off in range(-(win // 2), win - win // 2):
            acc = acc + buf_ref[hal + off:hal + off + tile, cols]
        cnt = jnp.minimum(t + (win - win // 2), seq) - jnp.maximum(t - win // 2, 0)
        diff = acc / cnt.astype(F32) - cur_ref[0, :, cols]
        pool_ref[0, :, cols] = _dot(diff.astype(BF16), pw_ref[gi]) * ps_ref[:, cols]
    for h in range(GLA_HEADS):
        cols = slice(h * GLA_DV, (h + 1) * GLA_DV)
        o = of_ref[0, :, cols] + ob_ref[0, :, cols]
        d_ref[0, :, cols] = _rms(o, hg_ref[:, cols]) * _silu(r_ref[0, :, cols])


def _odd_out_body(*refs, tile, seq):
    n_branch = 9
    buf_ref, pool_s, d_s = refs[-3:]
    _odd_mid_body(*refs[:n_branch], pool_s, d_s, buf_ref, tile=tile, seq=seq)
    _out_tail(pool_s[0], d_s[0], *refs[n_branch:-3])


def _odd_out(pool_u, o_f, o_b, r, head_g, pool_w_bf, pool_scale, *common):
    b, l, c = pool_u.shape
    tile = common[-1]
    hal = POOL_HALO
    per = tile // hal
    nh = l // hal
    blk = pl.BlockSpec((1, tile, c), lambda i, j: (i, j, 0))
    vec = pl.BlockSpec((1, c), lambda i, j: (0, 0))
    specs = [blk,
             pl.BlockSpec((1, hal, c), lambda i, j: (i, jnp.maximum(j * per - 1, 0), 0)),
             pl.BlockSpec((1, hal, c), lambda i, j: (i, jnp.minimum((j + 1) * per, nh - 1), 0)),
             blk, blk, blk, vec,
             pl.BlockSpec((len(POOL_WINDOWS), POOL_GROUP, POOL_GROUP), lambda i, j: (0, 0, 0)),
             vec]
    scratch = [pltpu.VMEM((tile + 2 * hal, c), F32), pltpu.VMEM((1, tile, c), F32), pltpu.VMEM((1, tile, c), F32)]
    args = [pool_u, pool_u, pool_u, o_f, o_b, r, head_g.reshape(1, c), pool_w_bf, pool_scale.reshape(1, c)]
    return _out_call(functools.partial(_odd_out_body, tile=tile, seq=l), args, specs, scratch, "odd_out", *common)


def _threshold_body(a_ref, thr_ref, need_ref, *, cap):
    bits = lax.bitcast_convert_type(a_ref[0], jnp.int32)
    rows = bits.shape[0]
    count_ge = lambda v: jnp.sum(jnp.where(bits >= v, 1.0, 0.0), axis=-1, keepdims=True)

    def step(_, carry):
        lo, hi = carry
        mid = lo + ((hi - lo + 1) >> 1)
        ok = count_ge(mid) >= cap
        return jnp.where(ok, mid, lo), jnp.where(ok, hi, mid - 1)

    lo, _ = lax.fori_loop(0, 31, step, (jnp.zeros((rows, 1), jnp.int32),
                                        jnp.full((rows, 1), F32_INF_BITS, jnp.int32)))
    above = jnp.sum(jnp.where(bits > lo, 1.0, 0.0), axis=-1, keepdims=True)
    width = thr_ref.shape[-1]
    thr_ref[0] = jnp.broadcast_to(lax.bitcast_convert_type(lo, F32), (rows, width))
    need_ref[0] = jnp.broadcast_to(cap - above.astype(jnp.int32), (rows, width))


def _route_threshold(aff_t, cap):
    b, e, n = aff_t.shape
    out = pl.BlockSpec((1, e, SC_LANES), lambda i: (i, 0, 0))
    return pl.pallas_call(
        functools.partial(_threshold_body, cap=cap),
        grid=(b,),
        in_specs=[pl.BlockSpec((1, e, n), lambda i: (i, 0, 0))],
        out_specs=[out, out],
        out_shape=[jax.ShapeDtypeStruct((b, e, SC_LANES), F32), jax.ShapeDtypeStruct((b, e, SC_LANES), jnp.int32)],
        compiler_params=_cparams(("parallel",)),
        name="route_threshold",
    )(aff_t)


def _route_compact(aff, thr, need, cap):
    r, n = aff.shape
    lanes = SC_LANES
    assert r == SC_CORES * SC_SUBCORES and n % lanes == 0
    mesh = plsc.VectorSubcoreMesh(core_axis_name="core", subcore_axis_name="subcore",
                                  num_cores=SC_CORES, num_subcores=SC_SUBCORES)

    @pl.kernel(out_type=[jax.ShapeDtypeStruct((r, cap), jnp.int32), jax.ShapeDtypeStruct((r, cap), F32)],
               mesh=mesh,
               scratch_types=[pltpu.VMEM((n,), F32), pltpu.VMEM((lanes,), F32), pltpu.VMEM((lanes,), jnp.int32),
                              pltpu.VMEM((cap,), jnp.int32), pltpu.VMEM((cap,), F32)],
               compiler_params=pltpu.CompilerParams(needs_layout_passes=False),
               name="route_compact")
    def compact(aff_hbm, thr_hbm, need_hbm, idx_hbm, val_hbm, row_v, thr_v, need_v, idx_v, val_v):
        w = lax.axis_index("subcore") * SC_CORES + lax.axis_index("core")
        pltpu.sync_copy(aff_hbm.at[w], row_v)
        pltpu.sync_copy(thr_hbm.at[w], thr_v)
        pltpu.sync_copy(need_hbm.at[w], need_v)
        thr = thr_v[...]
        need = need_v[...]
        lane = lax.iota(jnp.int32, lanes)
        ones = jnp.ones((lanes,), jnp.int32)

        def body(i, carry):
            n_out, n_eq = carry
            x = row_v[pl.ds(i * lanes, lanes)]
            eq = x == thr
            take = (x > thr) | (eq & (n_eq + plsc.cumsum(ones, mask=eq) <= need))
            pos = n_out + plsc.cumsum(ones, mask=take) - 1
            take = take & (pos < cap)
            plsc.store_scatter(idx_v, [pos], lane + i * lanes, mask=take)
            plsc.store_scatter(val_v, [pos], x, mask=take)
            return (n_out + plsc.all_reduce_population_count(take),
                    n_eq + plsc.all_reduce_population_count(eq))

        zero = jnp.zeros((lanes,), jnp.int32)
        lax.fori_loop(0, n // lanes, body, (zero, zero))
        pltpu.sync_copy(idx_v, idx_hbm.at[w])
        pltpu.sync_copy(val_v, val_hbm.at[w])

    return compact(aff, thr, need)


def _route(aff_t, tok_base, row_base):
    b, e, n = aff_t.shape
    cap = EC_CAPACITY_FACTOR * n // N_EXPERTS
    thr, need = _route_threshold(aff_t, cap)
    idx, vals = _route_compact(aff_t.reshape(b * e, n), thr.reshape(b * e, -1), need.reshape(b * e, -1), cap)
    idx = idx.reshape(b, e, cap)
    vals = vals.reshape(b, e, cap)
    bi = jnp.arange(b, dtype=idx.dtype)[:, None, None]
    per_expert = lambda a: jnp.swapaxes(a, 0, 1).reshape(e, b * cap)
    rows0 = idx + row_base + 2 * bi * n
    return (per_expert(vals), per_expert(idx + tok_base + bi * n), per_expert(rows0), per_expert(rows0 + n),
            per_expert(idx))


def _combine_first_start(p0, cap):
    return jnp.minimum((p0 // BF16_ROWS) * BF16_ROWS, cap - COMBINE_FIRST)


def _combine_body(offs_ref, spill_ref, tok_ref, y_ref, x_ref, gate_ref, g_ref, o_ref, f_s, *, n_tok, cap):
    bi = pl.program_id(0)
    j = pl.program_id(1)
    tt = COMBINE_TILE
    wf = COMBINE_FIRST
    wn = COMBINE_WINDOW
    ntiles = n_tok // tt
    n_exp = y_ref.shape[0]
    group = wn // wf
    sub = lax.broadcasted_iota(jnp.int32, (tt, 1), 0)
    lane = lax.broadcasted_iota(jnp.int32, (1, wn), 1)

    def slot_range(e):
        base = (bi * n_exp + e) * (ntiles + 1) + j
        return offs_ref[base], offs_ref[base + 1]

    acc = jnp.zeros(f_s.shape, F32)
    for g in range(n_exp // group):
        ys = []
        toks = jnp.full((1, wn), -1, jnp.int32)
        for k in range(group):
            e = g * group + k
            start = pl.multiple_of(_combine_first_start(slot_range(e)[0], cap), BF16_ROWS)
            ys.append(y_ref[e, pl.ds(start, wf), :])
            cs = pl.multiple_of(jnp.minimum((start // LANES) * LANES, cap - wn), LANES)
            rolled = pltpu.roll(tok_ref[e, :, pl.ds(cs, wn)], (k * wf + wn - (start - cs)) % wn, 1)
            toks = jnp.where((lane >= k * wf) & (lane < (k + 1) * wf), rolled, toks)
        hit = (toks - j * tt) == sub
        acc = acc + _dot(jnp.where(hit, 1.0, 0.0).astype(BF16), jnp.concatenate(ys, axis=0))
    f_s[...] = acc

    def more_windows(e, carry):
        p0, p1 = slot_range(e)
        lo = _combine_first_start(p0, cap) + wf
        first = (lo // LANES) * LANES

        def extra(w, carry):
            cs = pl.multiple_of(jnp.minimum(first + w * wn, cap - wn), LANES)
            tok = tok_ref[e, :, pl.ds(cs, wn)] - j * tt
            hit = (tok == sub) & (cs + lane >= jnp.maximum(lo, first + w * wn))
            f_s[...] += _dot(jnp.where(hit, 1.0, 0.0).astype(BF16), y_ref[e, pl.ds(cs, wn), :])
            return carry

        lax.fori_loop(0, (jnp.maximum(p1 - first, 0) + wn - 1) // wn * (p1 > lo).astype(jnp.int32), extra, 0)
        return carry

    @pl.when(spill_ref[bi * ntiles + j] != 0)
    def _():
        lax.fori_loop(0, n_exp, more_windows, 0)

    o_ref[0] = x_ref[0] + gate_ref[0] * _rms(f_s[...], g_ref[...])


def _combine(y, tok, local, seg0, x, gate, g):
    e, _, d = y.shape
    _, b, cap = local.shape
    n_tok = x.shape[1]
    tt = COMBINE_TILE
    wf = COMBINE_FIRST
    ntiles = n_tok // tt
    assert seg0 % cap == 0 and n_tok % tt == 0 and cap % COMBINE_WINDOW == 0 and e % (COMBINE_WINDOW // wf) == 0
    seg = seg0 // cap
    bounds = jnp.arange(ntiles + 1, dtype=jnp.int32) * tt
    offs = jnp.sum((local[..., None] < bounds).astype(jnp.int32), axis=2)
    starts = _combine_first_start(offs[..., :-1], cap)
    spill = jnp.any(offs[..., 1:] > starts + wf, axis=0).astype(jnp.int32).reshape(-1)
    row = pl.BlockSpec((1, tt, d), lambda i, j, offs, spill: (i, j, 0))
    grid_spec = pltpu.PrefetchScalarGridSpec(
        num_scalar_prefetch=2,
        grid=(b, ntiles),
        in_specs=[pl.BlockSpec((e, 1, cap), lambda i, j, offs, spill: (0, 0, seg + i)),
                  pl.BlockSpec((e, cap, d), lambda i, j, offs, spill: (0, seg + i, 0),
                               pipeline_mode=pl.Buffered(1)),
                  row,
                  pl.BlockSpec((1, 1, d), lambda i, j, offs, spill: (i, 0, 0)),
                  pl.BlockSpec((1, d), lambda i, j, offs, spill: (0, 0))],
        out_specs=row,
        scratch_shapes=[pltpu.VMEM((tt, d), F32)],
    )
    return pl.pallas_call(
        functools.partial(_combine_body, n_tok=n_tok, cap=cap),
        grid_spec=grid_spec,
        out_shape=jax.ShapeDtypeStruct((b, n_tok, d), F32),
        compiler_params=_cparams(("parallel", "arbitrary")),
        name="combine",
    )(jnp.swapaxes(offs, 0, 1).reshape(-1), spill, tok, y, x, gate.reshape(b, 1, d), g.reshape(1, d))


def _gather_rows(src, idx):
    window = SC_GATHER_WINDOW
    n = idx.shape[0]
    width = src.shape[1]
    assert 2 * window * width * 4 <= SC_TILE_VMEM_BUDGET, width
    assert n % (window * SC_CORES * SC_SUBCORES) == 0, n
    mesh = plsc.VectorSubcoreMesh(core_axis_name="core", subcore_axis_name="subcore",
                                  num_cores=SC_CORES, num_subcores=SC_SUBCORES)

    @pl.kernel(out_type=jax.ShapeDtypeStruct((n, width), src.dtype), mesh=mesh, scratch_types=[],
               name="gather_rows")
    def gather(src_hbm, idx_hbm, out_hbm):
        def body(idx_vmem, out_vmem):
            pltpu.sync_copy(src_hbm.at[idx_vmem.at[0]], out_vmem)

        pltpu.emit_pipeline(
            body,
            grid=(n // window,),
            in_specs=[pl.BlockSpec((1, window), lambda i: (0, i))],
            out_specs=[pl.BlockSpec((window, width), lambda i: (i, 0))],
            core_axis_name=("core", "subcore"),
            dimension_semantics=(pltpu.PARALLEL,),
        )(idx_hbm, out_hbm)

    return gather(src, idx.reshape(1, n))


def _moe(parts, g, w_gate, w_up, w_down, layer):
    quarter = parts[0][1].shape[-1]
    d = 4 * quarter
    sizes = [p[1].shape[0] * p[1].shape[2] for p in parts]
    bases = [sum(sizes[:i]) for i in range(len(parts))]
    routed = [_route(p[0], base, 2 * base) for p, base in zip(parts, bases)]
    src = jnp.concatenate([p[1].reshape(-1, quarter) for p in parts], axis=0)
    vals, flat, rows0, rows1, tok = (jnp.concatenate([r[i] for r in routed], axis=1) for i in range(5))
    e, m = flat.shape
    rows = jnp.stack([rows0, rows1], axis=1).reshape(-1)
    unit = SC_GATHER_WINDOW * SC_CORES * SC_SUBCORES
    fill = jnp.arange(-rows.shape[0] % unit, dtype=rows.dtype)
    xg = _gather_rows(src, jnp.concatenate([rows, fill]))
    chunk = next(c for c in (512, 528, 384, 320, 256, 128) if m % c == 0)
    y = _expert_ffn(xg, vals[..., None], w_gate, w_up, w_down, layer, chunk)
    outs = []
    seg0 = 0
    for (aff_t, _, x, gate), r, base in zip(parts, routed, bases):
        b, _, n = aff_t.shape
        cap = r[0].shape[1] // b
        if n % COMBINE_TILE == 0 and cap % COMBINE_WINDOW == 0 and seg0 % cap == 0:
            outs.append(_combine(y, tok[:, None, :], r[4].reshape(e, b, cap), seg0, x, gate, g))
        else:
            ids = r[1] - base
            part = y[:, seg0:seg0 + b * cap].astype(F32)
            f = jnp.zeros((b * n, d), F32).at[ids.reshape(-1)].add(part.reshape(-1, d))
            outs.append(_gated_residual(x, f.reshape(b, n, d), gate, g, n))
        seg0 += b * cap
    return outs


def kernel(x, c, ctx, c_ctx, w_mod, b_mod, norm_g, w_in_even, w_out_even, conv_w, conv_b, conv_ln_g,
           conv_ln_b, na_rpb, w_in_odd, w_out_odd, pool_w, pool_scale, gla_gate_w, gla_gate_b, gla_head_g,
           router_w, expert_w_gate, expert_w_up, expert_w_down):
    b, l, d = x.shape
    n_ctx = ctx.shape[1]
    tm = ROW_TILE

    mod_rows = jnp.concatenate([c, c_ctx[None], jnp.zeros((8 - b - 1, d), F32)], axis=0)

    mod_all = _modulation(mod_rows, w_mod, b_mod)

    def modulation(i):
        mm = mod_all[i]
        m = mm[:b].reshape(b, 6, d)
        mc = jnp.broadcast_to(mm[b].reshape(1, 6, d), (b, 6, d))
        return m, mc

    m, mc = modulation(0)
    g = norm_g[0]
    w_in = w_in_even[0].astype(BF16)
    w_out = w_out_even[0].astype(BF16)
    glu, qkv = _even_in(x, g[0], m[:, 0], m[:, 1], w_in, tm)
    glu_c, qkv_c = _even_in(ctx, g[0], mc[:, 0], mc[:, 1], w_in, n_ctx)
    a_ctx = _conv_branch(glu_c, conv_w[0], conv_b[0], conv_ln_g[0], conv_ln_b[0], n_ctx)
    na = _neighbourhood_attention(qkv, qkv_c, na_rpb[0])
    att_c = _context_attention(qkv_c)
    x, h2, aff = _even_out(glu, conv_w[0], conv_b[0], conv_ln_g[0], conv_ln_b[0], na,
                           x, w_out, g[1], m[:, 2], g[2], m[:, 3], m[:, 4], router_w[0], tm)
    ctx, h2c, aff_c = _out_proj(a_ctx, att_c, ctx, w_out, g[1], mc[:, 2], g[2], mc[:, 3], mc[:, 4],
                                router_w[0], n_ctx)
    x, ctx = _moe([(aff, h2, x, m[:, 5]), (aff_c, h2c, ctx, mc[:, 5])], g[3],
                  expert_w_gate, expert_w_up, expert_w_down, 0)

    m, mc = modulation(1)
    g = norm_g[1]
    w_in = w_in_odd[0].astype(BF16)
    w_out = w_out_odd[0].astype(BF16)
    cos_t, sin_t = _rope_tables(l)
    ones_t = jnp.ones((n_ctx, GLA_HEADS * GLA_DK), F32)
    _, qk_c, v_c, _, gate_c = _odd_in(ctx, g[0], mc[:, 0], mc[:, 1], w_in, ones_t, jnp.zeros_like(ones_t),
                                      gla_gate_w[0], gla_gate_b[0], n_ctx)
    s_zero = jnp.zeros((b, GLA_HEADS, GLA_DK, GLA_DV), F32)
    _, _, s_f, s_b = _gla(qk_c, v_c, gate_c, s_zero, s_zero, n_ctx)
    pool_u, qk, v, r, gate = _odd_in(x, g[0], m[:, 0], m[:, 1], w_in, cos_t, sin_t,
                                     gla_gate_w[0], gla_gate_b[0], tm)
    o_f, o_b, _, _ = _gla(qk, v, gate, s_f, s_b, SEQ_TILE)
    x, h2, aff = _odd_out(pool_u, o_f, o_b, r, gla_head_g[0], pool_w[0].astype(BF16), pool_scale[0],
                          x, w_out, g[1], m[:, 2], g[2], m[:, 3], m[:, 4], router_w[1], tm)
    (x,) = _moe([(aff, h2, x, m[:, 5])], g[3], expert_w_gate, expert_w_up, expert_w_down, 1)
    return x
```

```python
import functools

import jax
import jax.numpy as jnp
from jax import lax
from jax.experimental import pallas as pl
from jax.experimental.pallas import tpu as pltpu
from jax.experimental.pallas import tpu_sc as plsc

F32 = jnp.float32
BF16 = jnp.bfloat16
HIGHEST = lax.Precision.HIGHEST

D_MODEL = 1024
GRID_W = 64
EPS = 1e-6
CONV_CH = 512
CONV_WIDTH = 31
CONV_HALO = 16
CONV_CHUNK = 32
NA_HEADS = 8
NA_HEAD_DIM = 64
NA_KR = 8
NA_KC = 16
NA_ROWS_PER_BLOCK = 4
NA_BLOCKS_PER_STEP = 4
NA_WIN_ROWS = 12
POOL_CH = 512
POOL_WINDOWS = (2, 4, 8, 16)
POOL_GROUP = 128
POOL_HALO = 8
GLA_HEADS = 4
GLA_DK = 64
GLA_DV = 128
GLA_RANK = 16
GLA_TAU = 16.0
GLA_CHUNK = 64
ROPE_BASE = 10000.0
N_EXPERTS = 16
EC_CAPACITY_FACTOR = 2
ROW_TILE = 512
SEQ_TILE = 256
MOD_COLS = 1536
EXPERT_FF_TILE = 256
LANES = 128
SUBLANES = 8
NEG_BIG = -1e30
VMEM_LIMIT = 56 * 1024 * 1024
SC_CORES = 2
SC_SUBCORES = 16
SC_LANES = 16
F32_INF_BITS = 0x7F800000
SC_TILE_VMEM_BUDGET = 400 * 1024
SC_GATHER_WINDOW = 128
COMBINE_TILE = 256
COMBINE_WINDOW = 256
COMBINE_FIRST = 64
BF16_ROWS = 16


def _cparams(sem):
    return pltpu.CompilerParams(dimension_semantics=sem, vmem_limit_bytes=VMEM_LIMIT)


def _rms(x, g):
    return x * lax.rsqrt(jnp.mean(x * x, axis=-1, keepdims=True) + EPS) * g


def _sigmoid(x):
    return 1.0 / (1.0 + jnp.exp(-x))


def _silu(x):
    return x * _sigmoid(x)


def _dot(a, b):
    return jnp.dot(a, b, preferred_element_type=F32)


def _pack_bf16_pairs(h):
    half = h.shape[-1] // 2
    bits = lax.bitcast_convert_type(h.astype(BF16).astype(F32), jnp.uint32)
    packed = (bits[:, half:] & jnp.uint32(0xFFFF0000)) | (bits[:, :half] >> 16)
    return lax.bitcast_convert_type(packed, jnp.int32)


def _unpack_bf16_pairs(p):
    bits = lax.bitcast_convert_type(p, jnp.uint32)
    lo = lax.bitcast_convert_type(bits << 16, F32).astype(BF16)
    hi = lax.bitcast_convert_type(bits & jnp.uint32(0xFFFF0000), F32).astype(BF16)
    return lo, hi


def _dot_nt(a, b):
    return lax.dot_general(a, b, (((1,), (1,)), ((), ())), preferred_element_type=F32)


def _mod_body(c_ref, w_ref, b_ref, o_ref):
    o_ref[0] = jnp.dot(_silu(c_ref[...]), w_ref[0], precision=HIGHEST,
                       preferred_element_type=F32) + b_ref[0]


def _modulation(rows, w, b):
    depth, _, n = w.shape
    tn = MOD_COLS
    return pl.pallas_call(
        _mod_body,
        grid=(depth, n // tn),
        in_specs=[pl.BlockSpec((8, D_MODEL), lambda i, j: (0, 0)),
                  pl.BlockSpec((1, D_MODEL, tn), lambda i, j: (i, 0, j)),
                  pl.BlockSpec((1, 1, tn), lambda i, j: (i, 0, j))],
        out_specs=pl.BlockSpec((1, 8, tn), lambda i, j: (i, 0, j)),
        out_shape=jax.ShapeDtypeStruct((depth, 8, n), F32),
        compiler_params=_cparams(("parallel", "parallel")),
        name="modulation",
    )(rows, w, b.reshape(depth, 1, n))


def _even_in_body(x_ref, g_ref, sh_ref, sc_ref, w_ref, glu_ref, qkv_ref):
    h = (_rms(x_ref[0], g_ref[...]) * (1.0 + sc_ref[0]) + sh_ref[0]).astype(BF16)
    c = CONV_CH
    glu_ref[0] = _dot(h, w_ref[:, 0:c]) * _sigmoid(_dot(h, w_ref[:, c:2 * c]))
    hd = NA_HEADS * NA_HEAD_DIM
    q0 = 2 * c
    qkv_ref[0, :, 0:hd] = (_dot(h, w_ref[:, q0:q0 + hd]) * (NA_HEAD_DIM ** -0.5)).astype(BF16)
    qkv_ref[0, :, hd:3 * hd] = _dot(h, w_ref[:, q0 + hd:q0 + 3 * hd]).astype(BF16)


def _even_in(x, g, shift, scale, w_bf, tm):
    b, l, d = x.shape
    n = w_bf.shape[1]
    hd3 = 3 * NA_HEADS * NA_HEAD_DIM
    vec = pl.BlockSpec((1, 1, d), lambda i, j: (i, 0, 0))
    return pl.pallas_call(
        _even_in_body,
        grid=(b, l // tm),
        in_specs=[pl.BlockSpec((1, tm, d), lambda i, j: (i, j, 0)),
                  pl.BlockSpec((1, d), lambda i, j: (0, 0)),
                  vec, vec,
                  pl.BlockSpec((d, n), lambda i, j: (0, 0))],
        out_specs=[pl.BlockSpec((1, tm, CONV_CH), lambda i, j: (i, j, 0)),
                   pl.BlockSpec((1, tm, hd3), lambda i, j: (i, j, 0))],
        out_shape=[jax.ShapeDtypeStruct((b, l, CONV_CH), F32),
                   jax.ShapeDtypeStruct((b, l, hd3), BF16)],
        compiler_params=_cparams(("parallel", "parallel")),
        name="even_in",
    )(x, g.reshape(1, d), shift.reshape(b, 1, d), scale.reshape(b, 1, d), w_bf)


def _conv_body(cur_ref, prev_ref, next_ref, w_ref, b_ref, lg_ref, lb_ref, o_ref, buf_ref, sh_ref, *, tile, chunk):
    j = pl.program_id(1)
    last = pl.num_programs(1) - 1
    hal = CONV_HALO
    buf_ref[0:hal, :] = jnp.where(j == 0, 0.0, prev_ref[0])
    buf_ref[hal:hal + tile, :] = cur_ref[0]
    buf_ref[hal + tile:hal + tile + hal, :] = jnp.where(j == last, 0.0, next_ref[0])
    span = sh_ref.shape[1]
    for s in range(SUBLANES):
        sh_ref[s] = buf_ref[s:s + span, :]
    first = hal - CONV_WIDTH // 2
    reps = chunk // SUBLANES

    last_a = (first + CONV_WIDTH - 1) // SUBLANES

    def rows(c, carry):
        r0 = pl.multiple_of(c * chunk, chunk)
        acc = jnp.zeros((chunk, CONV_CH), F32)
        for s in range(SUBLANES):
            rows_s = sh_ref[s, pl.ds(r0, chunk + last_a * SUBLANES), :]
            for a in range(last_a + 1):
                k = a * SUBLANES + s - first
                if 0 <= k < CONV_WIDTH:
                    wk = jnp.concatenate([w_ref[k]] * reps, axis=0)
                    acc = acc + rows_s[a * SUBLANES:a * SUBLANES + chunk, :] * wk
        o_ref[0, pl.ds(r0, chunk), :] = acc
        return carry

    lax.fori_loop(0, tile // chunk, rows, 0)
    y = o_ref[0] + b_ref[...]
    mu = jnp.mean(y, axis=-1, keepdims=True)
    yc = y - mu
    var = jnp.mean(yc * yc, axis=-1, keepdims=True)
    o_ref[0] = _silu(yc * lax.rsqrt(var + EPS) * lg_ref[...] + lb_ref[...])


def _conv_operands(glu, conv_w, conv_b, ln_g, ln_b, tile):
    _, l, c = glu.shape
    hal = CONV_HALO
    per = tile // hal
    nh = l // hal
    vec = pl.BlockSpec((1, c), lambda i, j: (0, 0))
    specs = [pl.BlockSpec((1, tile, c), lambda i, j: (i, j, 0)),
             pl.BlockSpec((1, hal, c), lambda i, j: (i, jnp.maximum(j * per - 1, 0), 0)),
             pl.BlockSpec((1, hal, c), lambda i, j: (i, jnp.minimum((j + 1) * per, nh - 1), 0)),
             pl.BlockSpec((CONV_WIDTH, SUBLANES, c), lambda i, j: (0, 0, 0)),
             vec, vec, vec]
    args = [glu, glu, glu, jnp.broadcast_to(conv_w[:, None, :], (CONV_WIDTH, SUBLANES, c)),
            conv_b.reshape(1, c), ln_g.reshape(1, c), ln_b.reshape(1, c)]
    scratch = [pltpu.VMEM((tile + 2 * hal, c), F32),
               pltpu.VMEM((SUBLANES, tile + 2 * hal - SUBLANES, c), F32)]
    return args, specs, scratch


def _conv_branch(glu, conv_w, conv_b, ln_g, ln_b, tile):
    b, l, c = glu.shape
    args, specs, scratch = _conv_operands(glu, conv_w, conv_b, ln_g, ln_b, tile)
    return pl.pallas_call(
        functools.partial(_conv_body, tile=tile, chunk=CONV_CHUNK),
        grid=(b, l // tile),
        in_specs=specs,
        out_specs=pl.BlockSpec((1, tile, c), lambda i, j: (i, j, 0)),
        out_shape=jax.ShapeDtypeStruct((b, l, c), F32),
        scratch_shapes=scratch,
        compiler_params=_cparams(("parallel", "parallel")),
        name="conv_branch",
    )(*args)


def _na_window_start(j, rows):
    rb = NA_ROWS_PER_BLOCK
    return jnp.clip(j * rb - NA_KR // 2, 0, rows - NA_WIN_ROWS)


def _na_body(q_ref, k_ref, v_ref, kc_ref, vc_ref, *rest, rows):
    tab_refs, o_ref = rest[:-1], rest[-1]
    j = pl.program_id(2)
    nkeys = NA_WIN_ROWS * GRID_W
    tq = NA_ROWS_PER_BLOCK * GRID_W
    kc = kc_ref[0]
    vc = vc_ref[0]
    lane = lax.broadcasted_iota(jnp.int32, (1, LANES), 1)
    for sb, tab_ref in enumerate(tab_refs):
        start = pl.multiple_of(_na_window_start(j * len(tab_refs) + sb, rows) * GRID_W, GRID_W)
        q = q_ref[0, sb * tq:(sb + 1) * tq, :]
        kw = k_ref[0, pl.ds(start, nkeys), :]
        vw = v_ref[0, pl.ds(start, nkeys), :]
        out = jnp.zeros(q.shape, F32)
        for hh in range(LANES // NA_HEAD_DIM):
            in_head = (lane >= hh * NA_HEAD_DIM) & (lane < (hh + 1) * NA_HEAD_DIM)
            qh = jnp.where(in_head, q, jnp.zeros_like(q))
            s = _dot_nt(qh, kw) + tab_ref[0, hh]
            sc = _dot_nt(qh, kc)
            m = jnp.maximum(jnp.max(s, axis=-1, keepdims=True), jnp.max(sc, axis=-1, keepdims=True))
            p = jnp.exp(s - m)
            pc = jnp.exp(sc - m)
            denom = jnp.sum(p, axis=-1, keepdims=True) + jnp.sum(pc, axis=-1, keepdims=True)
            o = (_dot(p.astype(BF16), vw) + _dot(pc.astype(BF16), vc)) / denom
            out = jnp.where(in_head, o, out)
        o_ref[0, sb * tq:(sb + 1) * tq, :] = out


def _na_tables(rpb, rows):
    rb = NA_ROWS_PER_BLOCK
    nblk = rows // rb
    wr = NA_WIN_ROWS
    qc = jnp.arange(GRID_W)
    cs = jnp.clip(qc - NA_KC // 2, 0, GRID_W - NA_KC)
    col_ok = (qc[None, :] >= cs[:, None]) & (qc[None, :] < cs[:, None] + NA_KC)
    col_off = qc[None, :] - qc[:, None] + NA_KC - 1
    onehot = (col_off[:, :, None] == jnp.arange(2 * NA_KC - 1)[None, None, :]).astype(F32)
    blocks = jnp.einsum('hrd,qkd->hrqk', rpb.astype(F32), onehot, precision=HIGHEST)
    blocks = jnp.where(col_ok[None, None], blocks, NEG_BIG)
    masked = jnp.full((NA_HEADS, GRID_W, GRID_W), NEG_BIG, F32)
    tabs = []
    for jb in (0, 1, nblk - 1):
        ws = min(max(jb * rb - NA_KR // 2, 0), rows - wr)
        q_rows = []
        for qr in range(jb * rb, (jb + 1) * rb):
            rs = min(max(qr - NA_KR // 2, 0), rows - NA_KR)
            row = [blocks[:, kr - qr + NA_KR - 1] if rs <= kr < rs + NA_KR else masked
                   for kr in range(ws, ws + wr)]
            q_rows.append(jnp.concatenate(row, axis=-1))
        tabs.append(jnp.concatenate(q_rows, axis=1))
    return jnp.stack(tabs)


def _neighbourhood_attention(qkv, qkv_ctx, rpb):
    b, l, _ = qkv.shape
    n_ctx = qkv_ctx.shape[1]
    rows = l // GRID_W
    rb = NA_ROWS_PER_BLOCK
    nblk = rows // rb
    tq = rb * GRID_W
    nkeys = NA_WIN_ROWS * GRID_W
    hp = NA_HEADS * NA_HEAD_DIM // LANES
    tabs = _na_tables(rpb, rows)

    per = NA_BLOCKS_PER_STEP

    def cls(jb):
        return jnp.where(jb == 0, 0, jnp.where(jb == nblk - 1, 2, 1))

    def tab_spec(sb):
        return pl.BlockSpec((1, 2, tq, nkeys), lambda i, h, j: (cls(j * per + sb), h, 0, 0))

    return pl.pallas_call(
        functools.partial(_na_body, rows=rows),
        grid=(b, hp, nblk // per),
        in_specs=[pl.BlockSpec((1, per * tq, LANES), lambda i, h, j: (i, j, h)),
                  pl.BlockSpec((1, l, LANES), lambda i, h, j: (i, 0, hp + h)),
                  pl.BlockSpec((1, l, LANES), lambda i, h, j: (i, 0, 2 * hp + h)),
                  pl.BlockSpec((1, n_ctx, LANES), lambda i, h, j: (i, 0, hp + h)),
                  pl.BlockSpec((1, n_ctx, LANES), lambda i, h, j: (i, 0, 2 * hp + h))]
                 + [tab_spec(sb) for sb in range(per)],
        out_specs=pl.BlockSpec((1, per * tq, LANES), lambda i, h, j: (i, j, h)),
        out_shape=jax.ShapeDtypeStruct((b, l, NA_HEADS * NA_HEAD_DIM), F32),
        compiler_params=_cparams(("parallel", "parallel", "arbitrary")),
        name="neighbourhood_attention",
    )(qkv, qkv, qkv, qkv_ctx, qkv_ctx, *([tabs] * per))


def _ctx_attn_body(q_ref, k_ref, v_ref, o_ref):
    q = q_ref[0]
    k = k_ref[0]
    v = v_ref[0]
    lane = lax.broadcasted_iota(jnp.int32, (1, LANES), 1)
    out = jnp.zeros(q.shape, F32)
    for hh in range(LANES // NA_HEAD_DIM):
        in_head = (lane >= hh * NA_HEAD_DIM) & (lane < (hh + 1) * NA_HEAD_DIM)
        qh = jnp.where(in_head, q, jnp.zeros_like(q))
        s = _dot_nt(qh, k)
        p = jnp.exp(s - jnp.max(s, axis=-1, keepdims=True))
        o = _dot(p.astype(BF16), v) / jnp.sum(p, axis=-1, keepdims=True)
        out = jnp.where(in_head, o, out)
    o_ref[0] = out


def _context_attention(qkv_ctx):
    b, n, _ = qkv_ctx.shape
    hp = NA_HEADS * NA_HEAD_DIM // LANES
    return pl.pallas_call(
        _ctx_attn_body,
        grid=(b, hp),
        in_specs=[pl.BlockSpec((1, n, LANES), lambda i, h: (i, 0, h)),
                  pl.BlockSpec((1, n, LANES), lambda i, h: (i, 0, hp + h)),
                  pl.BlockSpec((1, n, LANES), lambda i, h: (i, 0, 2 * hp + h))],
        out_specs=pl.BlockSpec((1, n, LANES), lambda i, h: (i, 0, h)),
        out_shape=jax.ShapeDtypeStruct((b, n, NA_HEADS * NA_HEAD_DIM), F32),
        compiler_params=_cparams(("parallel", "parallel")),
        name="context_attention",
    )(qkv_ctx, qkv_ctx, qkv_ctx)


def _out_body(a_ref, b_ref, *rest):
    _out_tail(a_ref[0], b_ref[0], *rest)


def _out_tail(a, b2, x_ref, w_ref, g1_ref, gate_ref, g2_ref, sh_ref, sc_ref, rw_ref, xo_ref, h_ref, aff_ref):
    half = a.shape[-1]
    y = _dot(a.astype(BF16), w_ref[0:half, :]) + _dot(b2.astype(BF16), w_ref[half:2 * half, :])
    xn = x_ref[0] + gate_ref[0] * _rms(y, g1_ref[...])
    xo_ref[0] = xn
    h = _rms(xn, g2_ref[...]) * (1.0 + sc_ref[0]) + sh_ref[0]
    packed = _pack_bf16_pairs(h)
    quarter = packed.shape[-1] // 2
    h_ref[0, 0] = packed[:, 0:quarter]
    h_ref[0, 1] = packed[:, quarter:2 * quarter]
    h_hi = h.astype(BF16)
    h_lo = (h - h_hi.astype(F32)).astype(BF16)
    both = _dot(h_hi, rw_ref[...])
    logits = both[:, 0:LANES] + (both[:, LANES:2 * LANES] + _dot(h_lo, rw_ref[:, 0:LANES]))
    lane = lax.broadcasted_iota(jnp.int32, (1, LANES), 1)
    logits = jnp.where(lane < N_EXPERTS, logits, NEG_BIG)
    e = jnp.exp(logits - jnp.max(logits, axis=-1, keepdims=True))
    aff = e / jnp.sum(e, axis=-1, keepdims=True)
    aff_ref[0] = aff.T[0:N_EXPERTS, :]


def _out_call(body, branch_args, branch_specs, scratch, name, x, w_bf, g1, gate, g2, shift, scale, router_w, tm):
    b, l, d = x.shape
    rw = jnp.pad(router_w, ((0, 0), (0, LANES - N_EXPERTS)))
    rw_hi = rw.astype(BF16)
    rw_cat = jnp.concatenate([rw_hi, (rw - rw_hi.astype(F32)).astype(BF16)], axis=1)
    vec = pl.BlockSpec((1, d), lambda i, j: (0, 0))
    bvec = pl.BlockSpec((1, 1, d), lambda i, j: (i, 0, 0))
    rspec = pl.BlockSpec((d, 2 * LANES), lambda i, j: (0, 0))
    return pl.pallas_call(
        body,
        grid=(b, l // tm),
        in_specs=list(branch_specs) + [pl.BlockSpec((1, tm, d), lambda i, j: (i, j, 0)),
                                       pl.BlockSpec(w_bf.shape, lambda i, j: (0, 0)),
                                       vec, bvec, vec, bvec, bvec, rspec],
        out_specs=[pl.BlockSpec((1, tm, d), lambda i, j: (i, j, 0)),
                   pl.BlockSpec((1, 2, tm, d // 4), lambda i, j: (i, 0, j, 0)),
                   pl.BlockSpec((1, N_EXPERTS, tm), lambda i, j: (i, 0, j))],
        out_shape=[jax.ShapeDtypeStruct((b, l, d), F32),
                   jax.ShapeDtypeStruct((b, 2, l, d // 4), jnp.int32),
                   jax.ShapeDtypeStruct((b, N_EXPERTS, l), F32)],
        scratch_shapes=list(scratch),
        compiler_params=_cparams(("parallel", "parallel")),
        name=name,
    )(*branch_args, x, w_bf, g1.reshape(1, d), gate.reshape(b, 1, d), g2.reshape(1, d),
      shift.reshape(b, 1, d), scale.reshape(b, 1, d), rw_cat)


def _even_out_body(*refs, tile):
    n_conv = 7
    buf_ref, sh_ref, a_s = refs[-3:]
    _conv_body(*refs[:n_conv], a_s, buf_ref, sh_ref, tile=tile, chunk=CONV_CHUNK)
    _out_tail(a_s[0], refs[n_conv][0], *refs[n_conv + 1:-3])


def _even_out(glu, conv_w, conv_b, ln_g, ln_b, na, *common):
    tile = common[-1]
    c = glu.shape[-1]
    args, specs, scratch = _conv_operands(glu, conv_w, conv_b, ln_g, ln_b, tile)
    na_spec = pl.BlockSpec((1, tile, na.shape[-1]), lambda i, j: (i, j, 0))
    return _out_call(functools.partial(_even_out_body, tile=tile), args + [na], specs + [na_spec],
                     scratch + [pltpu.VMEM((1, tile, c), F32)], "even_out", *common)


def _out_proj(a, b2, *common):
    tm = common[-1]
    spec = pl.BlockSpec((1, tm, a.shape[-1]), lambda i, j: (i, j, 0))
    return _out_call(_out_body, [a, b2], [spec, spec], [], "out_proj", *common)


def _moe_body(x0_ref, x1_ref, val_ref, wg_ref, wu_ref, wd_ref, o_ref, acc_s, x_s, wg_s, wu_s, wd_s, *, chunk):
    f = pl.program_id(1)
    m, quarter = x0_ref.shape

    @pl.when(f == 0)
    def _():
        def unpack(c, carry):
            r = pl.multiple_of(c * chunk, chunk)
            for s, x_ref in enumerate((x0_ref, x1_ref)):
                lo, hi = _unpack_bf16_pairs(x_ref[pl.ds(r, chunk), :])
                x_s[pl.ds(r, chunk), s * quarter:(s + 1) * quarter] = lo
                x_s[pl.ds(r, chunk), (2 + s) * quarter:(3 + s) * quarter] = hi
            acc_s[pl.ds(r, chunk), :] = jnp.zeros((chunk, 4 * quarter), F32)
            return carry

        lax.fori_loop(0, m // chunk, unpack, 0)

    wg_s[...] = wg_ref[0, 0].astype(BF16)
    wu_s[...] = wu_ref[0, 0].astype(BF16)
    wd_s[...] = wd_ref[0, 0].astype(BF16)

    def rows(c, carry):
        r = pl.multiple_of(c * chunk, chunk)
        xs = x_s[pl.ds(r, chunk), :]
        hid = (_silu(_dot(xs, wg_s[...])) * _dot(xs, wu_s[...])).astype(BF16)
        acc_s[pl.ds(r, chunk), :] += _dot(hid, wd_s[...])
        return carry

    lax.fori_loop(0, m // chunk, rows, 0, unroll=True)

    @pl.when(f == pl.num_programs(1) - 1)
    def _():
        o_ref[0] = (acc_s[...] * val_ref[0]).astype(o_ref.dtype)


def _expert_ffn(xg, vals, w_gate, w_up, w_down, layer, chunk):
    e, m, _ = vals.shape
    quarter = xg.shape[1]
    d = 4 * quarter
    ff = w_gate.shape[-1]
    tf = EXPERT_FF_TILE
    return pl.pallas_call(
        functools.partial(_moe_body, chunk=chunk),
        grid=(e, ff // tf),
        in_specs=[pl.BlockSpec((m, quarter), lambda i, f: (2 * i, 0)),
                  pl.BlockSpec((m, quarter), lambda i, f: (2 * i + 1, 0)),
                  pl.BlockSpec((1, m, 1), lambda i, f: (i, 0, 0)),
                  pl.BlockSpec((1, 1, d, tf), lambda i, f: (layer, i, 0, f)),
                  pl.BlockSpec((1, 1, d, tf), lambda i, f: (layer, i, 0, f)),
                  pl.BlockSpec((1, 1, tf, d), lambda i, f: (layer, i, f, 0))],
        out_specs=pl.BlockSpec((1, m, d), lambda i, f: (i, 0, 0)),
        out_shape=jax.ShapeDtypeStruct((e, m, d), BF16),
        scratch_shapes=[pltpu.VMEM((m, d), F32), pltpu.VMEM((m, d), BF16), pltpu.VMEM((d, tf), BF16),
                        pltpu.VMEM((d, tf), BF16), pltpu.VMEM((tf, d), BF16)],
        compiler_params=_cparams(("parallel", "arbitrary")),
        name="expert_ffn",
    )(xg, xg, vals, w_gate, w_up, w_down)


def _resid_body(x_ref, f_ref, gate_ref, g_ref, o_ref):
    o_ref[0] = x_ref[0] + gate_ref[0] * _rms(f_ref[0], g_ref[...])


def _gated_residual(x, f, gate, g, tm):
    b, l, d = x.shape
    blk = pl.BlockSpec((1, tm, d), lambda i, j: (i, j, 0))
    return pl.pallas_call(
        _resid_body,
        grid=(b, l // tm),
        in_specs=[blk, blk, pl.BlockSpec((1, 1, d), lambda i, j: (i, 0, 0)),
                  pl.BlockSpec((1, d), lambda i, j: (0, 0))],
        out_specs=blk,
        out_shape=jax.ShapeDtypeStruct((b, l, d), F32),
        compiler_params=_cparams(("parallel", "parallel")),
        name="gated_residual",
    )(x, f, gate.reshape(b, 1, d), g.reshape(1, d))


def _swap_pairs(x):
    nf = GLA_DK // 4
    lane = lax.broadcasted_iota(jnp.int32, (1, LANES), 1)
    up = pltpu.roll(x, LANES - nf, 1)
    down = pltpu.roll(x, nf, 1)
    return jnp.where(lane % (2 * nf) < nf, up, down)


def _odd_in_body(x_ref, g_ref, sh_ref, sc_ref, w_ref, cos_ref, sin_ref, gw_ref, gb_ref,
                 pool_ref, qk_ref, v_ref, r_ref, gate_ref):
    h = (_rms(x_ref[0], g_ref[...]) * (1.0 + sc_ref[0]) + sh_ref[0]).astype(BF16)
    qk = GLA_HEADS * GLA_DK
    vd = GLA_HEADS * GLA_DV
    q0 = POOL_CH
    v0 = q0 + 2 * qk
    r0 = v0 + vd
    l0 = r0 + vd
    pool_ref[0] = _dot(h, w_ref[:, 0:q0])
    qk_raw = _dot(h, w_ref[:, q0:v0])
    for s in range(2 * qk // LANES):
        raw = qk_raw[:, s * LANES:(s + 1) * LANES]
        c = cos_ref[:, (s * LANES) % qk:(s * LANES) % qk + LANES]
        sn = sin_ref[:, (s * LANES) % qk:(s * LANES) % qk + LANES]
        rot = raw * c + _swap_pairs(raw) * sn
        if s * LANES < qk:
            rot = rot * (GLA_DK ** -0.5)
        qk_ref[0, :, s * LANES:(s + 1) * LANES] = rot
    v_ref[0] = _dot(h, w_ref[:, v0:r0]).astype(BF16)
    r_ref[0] = _dot(h, w_ref[:, r0:l0])
    lr = _dot(h, w_ref[:, l0:l0 + 2 * GLA_RANK])
    z = jnp.dot(lr, gw_ref[...], precision=HIGHEST, preferred_element_type=F32) + gb_ref[...]
    gate_ref[0] = (jnp.minimum(z, 0.0) - jnp.log1p(jnp.exp(-jnp.abs(z)))) * (1.0 / GLA_TAU)


def _odd_in(x, g, shift, scale, w_bf, cos_t, sin_t, gate_w, gate_b, tm):
    b, l, d = x.shape
    n = w_bf.shape[1]
    qk = GLA_HEADS * GLA_DK
    vd = GLA_HEADS * GLA_DV
    gw = jnp.zeros((2 * GLA_RANK, 2 * qk), F32)
    gw = gw.at[:GLA_RANK, :qk].set(gate_w[0]).at[GLA_RANK:, qk:].set(gate_w[1])
    gb = jnp.concatenate([gate_b[0], gate_b[1]]).reshape(1, 2 * qk)
    vec = pl.BlockSpec((1, 1, d), lambda i, j: (i, 0, 0))
    row = lambda w: pl.BlockSpec((1, tm, w), lambda i, j: (i, j, 0))
    return pl.pallas_call(
        _odd_in_body,
        grid=(b, l // tm),
        in_specs=[row(d), pl.BlockSpec((1, d), lambda i, j: (0, 0)), vec, vec,
                  pl.BlockSpec((d, n), lambda i, j: (0, 0)),
                  pl.BlockSpec((tm, qk), lambda i, j: (j, 0)),
                  pl.BlockSpec((tm, qk), lambda i, j: (j, 0)),
                  pl.BlockSpec((2 * GLA_RANK, 2 * qk), lambda i, j: (0, 0)),
                  pl.BlockSpec((1, 2 * qk), lambda i, j: (0, 0))],
        out_specs=[row(POOL_CH), row(2 * qk), row(vd), row(vd), row(2 * qk)],
        out_shape=[jax.ShapeDtypeStruct((b, l, POOL_CH), F32),
                   jax.ShapeDtypeStruct((b, l, 2 * qk), F32),
                   jax.ShapeDtypeStruct((b, l, vd), BF16),
                   jax.ShapeDtypeStruct((b, l, vd), F32),
                   jax.ShapeDtypeStruct((b, l, 2 * qk), F32)],
        compiler_params=_cparams(("parallel", "parallel")),
        name="odd_in",
    )(x, g.reshape(1, d), shift.reshape(b, 1, d), scale.reshape(b, 1, d), w_bf, cos_t, sin_t, gw, gb)


def _rope_tables(l):
    t = jnp.arange(l)
    pos_r = (t // GRID_W).astype(F32)
    pos_c = (t % GRID_W).astype(F32)
    nf = GLA_DK // 4
    inv = jnp.power(ROPE_BASE, -jnp.arange(nf, dtype=F32) / nf)
    ar = pos_r[:, None] * inv[None, :]
    ac = pos_c[:, None] * inv[None, :]
    cos_h = jnp.concatenate([jnp.cos(ar), jnp.cos(ar), jnp.cos(ac), jnp.cos(ac)], axis=-1)
    sin_h = jnp.concatenate([-jnp.sin(ar), jnp.sin(ar), -jnp.sin(ac), jnp.sin(ac)], axis=-1)
    return jnp.tile(cos_h, (1, GLA_HEADS)), jnp.tile(sin_h, (1, GLA_HEADS))


def _gla_tile(qk, v, g, s, reverse):
    hk = GLA_HEADS * GLA_DK
    hv = GLA_HEADS * GLA_DV
    c = GLA_CHUNK
    t = qk.shape[0]
    n = t // c
    last_row, mid_row = (0, c // 2) if reverse else (c - 1, c // 2 - 1)
    ii = lax.broadcasted_iota(jnp.int32, (t, t), 0)
    jj = lax.broadcasted_iota(jnp.int32, (t, t), 1)
    ordered = (jj >= ii) if reverse else (jj <= ii)
    tri = jnp.where(ordered & (ii // c == jj // c), 1.0, 0.0).astype(BF16)
    g_hi = g.astype(BF16)
    rem = g - g_hi.astype(F32)
    g_mid = rem.astype(BF16)
    g_lo = (rem - g_mid.astype(F32)).astype(BF16)
    bc = _dot(tri, g_hi) + (_dot(tri, g_mid) + _dot(tri, g_lo))
    spread = lambda row: jnp.concatenate(
        [jnp.broadcast_to(bc[i * c + row:i * c + row + 1, :], (c, hk)) for i in range(n)], axis=0)
    b_mid = spread(mid_row)
    b_last = spread(last_row)
    qt = qk[:, 0:hk] * jnp.exp(bc - b_mid)
    kt = qk[:, hk:2 * hk] * jnp.exp(b_mid - bc)
    qe = (qt * jnp.exp(b_mid)).astype(BF16)
    ke = kt * jnp.exp(b_last - b_mid)
    ktb = kt.astype(BF16)
    lane = lax.broadcasted_iota(jnp.int32, (1, hk), 1)
    ci = lax.broadcasted_iota(jnp.int32, (c, c), 0)
    cj = lax.broadcasted_iota(jnp.int32, (c, c), 1)
    causal = (cj >= ci) if reverse else (cj <= ci)
    blockdiag = (lax.broadcasted_iota(jnp.int32, (hk, hv), 0) // GLA_DK
                 == lax.broadcasted_iota(jnp.int32, (hk, hv), 1) // GLA_DV)
    intra, upd, decay = [], [], []
    for i in range(n):
        rows = slice(i * c, (i + 1) * c)
        qs = jnp.concatenate(
            [jnp.where((lane >= h * GLA_DK) & (lane < (h + 1) * GLA_DK), qt[rows], 0.0) for h in range(GLA_HEADS)],
            axis=0).astype(BF16)
        att = _dot_nt(qs, ktb[rows])
        intra.append(jnp.concatenate(
            [_dot(jnp.where(causal, att[h * c:(h + 1) * c], 0.0).astype(BF16),
                  v[rows, h * GLA_DV:(h + 1) * GLA_DV]) for h in range(GLA_HEADS)], axis=-1))
        upd.append(jnp.where(blockdiag, _dot(ke[rows].T.astype(BF16), v[rows]), 0.0))
        decay.append(jnp.exp(jnp.sum(g[rows].T, axis=1, keepdims=True)))
    outs = [None] * n
    for i in (reversed(range(n)) if reverse else range(n)):
        rows = slice(i * c, (i + 1) * c)
        outs[i] = _dot(qe[rows], s.astype(BF16)) + intra[i]
        s = decay[i] * s + upd[i]
    return jnp.concatenate(outs, axis=0), s


def _gla_body(qkf_ref, qkb_ref, vf_ref, vb_ref, gf_ref, gb_ref, s0f_ref, s0b_ref,
              of_ref, ob_ref, sff_ref, sbf_ref, sf_ref, sb_ref, *, tile):
    n = pl.program_id(1)
    hk = GLA_HEADS * GLA_DK
    hv = GLA_HEADS * GLA_DV

    @pl.when(n == 0)
    def _():
        sf_ref[...] = jnp.zeros((hk, hv), F32)
        sb_ref[...] = jnp.zeros((hk, hv), F32)
        for h in range(GLA_HEADS):
            sf_ref[h * GLA_DK:(h + 1) * GLA_DK, h * GLA_DV:(h + 1) * GLA_DV] = s0f_ref[0, h]
            sb_ref[h * GLA_DK:(h + 1) * GLA_DK, h * GLA_DV:(h + 1) * GLA_DV] = s0b_ref[0, h]

    of_ref[0], sf_ref[...] = _gla_tile(qkf_ref[0], vf_ref[0], gf_ref[0], sf_ref[...], False)
    ob_ref[0], sb_ref[...] = _gla_tile(qkb_ref[0], vb_ref[0], gb_ref[0], sb_ref[...], True)

    @pl.when(n == pl.num_programs(1) - 1)
    def _():
        for h in range(GLA_HEADS):
            sff_ref[0, h] = sf_ref[h * GLA_DK:(h + 1) * GLA_DK, h * GLA_DV:(h + 1) * GLA_DV]
            sbf_ref[0, h] = sb_ref[h * GLA_DK:(h + 1) * GLA_DK, h * GLA_DV:(h + 1) * GLA_DV]


def _gla(qk, v, gates, s0f, s0b, tile):
    b, l, _ = qk.shape
    hk = GLA_HEADS * GLA_DK
    hv = GLA_HEADS * GLA_DV
    nt = l // tile
    fwd = lambda w, col: pl.BlockSpec((1, tile, w), lambda i, n: (i, n, col))
    bwd = lambda w, col: pl.BlockSpec((1, tile, w), lambda i, n: (i, nt - 1 - n, col))
    st = pl.BlockSpec((1, GLA_HEADS, GLA_DK, GLA_DV), lambda i, n: (i, 0, 0, 0))
    return pl.pallas_call(
        functools.partial(_gla_body, tile=tile),
        grid=(b, nt),
        in_specs=[fwd(2 * hk, 0), bwd(2 * hk, 0), fwd(hv, 0), bwd(hv, 0), fwd(hk, 0), bwd(hk, 1), st, st],
        out_specs=[fwd(hv, 0), bwd(hv, 0), st, st],
        out_shape=[jax.ShapeDtypeStruct((b, l, hv), F32), jax.ShapeDtypeStruct((b, l, hv), F32),
                   jax.ShapeDtypeStruct((b, GLA_HEADS, GLA_DK, GLA_DV), F32),
                   jax.ShapeDtypeStruct((b, GLA_HEADS, GLA_DK, GLA_DV), F32)],
        scratch_shapes=[pltpu.VMEM((hk, hv), F32), pltpu.VMEM((hk, hv), F32)],
        compiler_params=_cparams(("parallel", "arbitrary")),
        name="gla_scan",
    )(qk, qk, v, v, gates, gates, s0f, s0b)


def _odd_mid_body(cur_ref, prev_ref, next_ref, of_ref, ob_ref, r_ref, hg_ref, pw_ref, ps_ref,
                  pool_ref, d_ref, buf_ref, *, tile, seq):
    j = pl.program_id(1)
    last = pl.num_programs(1) - 1
    hal = POOL_HALO
    buf_ref[0:hal, :] = jnp.where(j == 0, 0.0, prev_ref[0])
    buf_ref[hal:hal + tile, :] = cur_ref[0]
    buf_ref[hal + tile:hal + tile + hal, :] = jnp.where(j == last, 0.0, next_ref[0])
    t = j * tile + lax.broadcasted_iota(jnp.int32, (tile, 1), 0)
    for gi, win in enumerate(POOL_WINDOWS):
        cols = slice(gi * POOL_GROUP, (gi + 1) * POOL_GROUP)
        acc = jnp.zeros((tile, POOL_GROUP), F32)
        for off in range(-(win // 2), win - win // 2):
            acc = acc + buf_ref[hal + off:hal + off + tile, cols]
        cnt = jnp.minimum(t + (win - win // 2), seq) - jnp.maximum(t - win // 2, 0)
        diff = acc / cnt.astype(F32) - cur_ref[0, :, cols]
        pool_ref[0, :, cols] = _dot(diff.astype(BF16), pw_ref[gi]) * ps_ref[:, cols]
    for h in range(GLA_HEADS):
        cols = slice(h * GLA_DV, (h + 1) * GLA_DV)
        o = of_ref[0, :, cols] + ob_ref[0, :, cols]
        d_ref[0, :, cols] = _rms(o, hg_ref[:, cols]) * _silu(r_ref[0, :, cols])


def _odd_out_body(*refs, tile, seq):
    n_branch = 9
    buf_ref, pool_s, d_s = refs[-3:]
    _odd_mid_body(*refs[:n_branch], pool_s, d_s, buf_ref, tile=tile, seq=seq)
    _out_tail(pool_s[0], d_s[0], *refs[n_branch:-3])


def _odd_out(pool_u, o_f, o_b, r, head_g, pool_w_bf, pool_scale, *common):
    b, l, c = pool_u.shape
    tile = common[-1]
    hal = POOL_HALO
    per = tile // hal
    nh = l // hal
    blk = pl.BlockSpec((1, tile, c), lambda i, j: (i, j, 0))
    vec = pl.BlockSpec((1, c), lambda i, j: (0, 0))
    specs = [blk,
             pl.BlockSpec((1, hal, c), lambda i, j: (i, jnp.maximum(j * per - 1, 0), 0)),
             pl.BlockSpec((1, hal, c), lambda i, j: (i, jnp.minimum((j + 1) * per, nh - 1), 0)),
             blk, blk, blk, vec,
             pl.BlockSpec((len(POOL_WINDOWS), POOL_GROUP, POOL_GROUP), lambda i, j: (0, 0, 0)),
             vec]
    scratch = [pltpu.VMEM((tile + 2 * hal, c), F32), pltpu.VMEM((1, tile, c), F32), pltpu.VMEM((1, tile, c), F32)]
    args = [pool_u, pool_u, pool_u, o_f, o_b, r, head_g.reshape(1, c), pool_w_bf, pool_scale.reshape(1, c)]
    return _out_call(functools.partial(_odd_out_body, tile=tile, seq=l), args, specs, scratch, "odd_out", *common)


def _threshold_body(a_ref, thr_ref, need_ref, *, cap):
    bits = lax.bitcast_convert_type(a_ref[0], jnp.int32)
    rows = bits.shape[0]
    count_ge = lambda v: jnp.sum(jnp.where(bits >= v, 1.0, 0.0), axis=-1, keepdims=True)

    def step(_, carry):
        lo, hi = carry
        mid = lo + ((hi - lo + 1) >> 1)
        ok = count_ge(mid) >= cap
        return jnp.where(ok, mid, lo), jnp.where(ok, hi, mid - 1)

    lo, _ = lax.fori_loop(0, 31, step, (jnp.zeros((rows, 1), jnp.int32),
                                        jnp.full((rows, 1), F32_INF_BITS, jnp.int32)))
    above = jnp.sum(jnp.where(bits > lo, 1.0, 0.0), axis=-1, keepdims=True)
    width = thr_ref.shape[-1]
    thr_ref[0] = jnp.broadcast_to(lax.bitcast_convert_type(lo, F32), (rows, width))
    need_ref[0] = jnp.broadcast_to(cap - above.astype(jnp.int32), (rows, width))


def _route_threshold(aff_t, cap):
    b, e, n = aff_t.shape
    out = pl.BlockSpec((1, e, SC_LANES), lambda i: (i, 0, 0))
    return pl.pallas_call(
        functools.partial(_threshold_body, cap=cap),
        grid=(b,),
        in_specs=[pl.BlockSpec((1, e, n), lambda i: (i, 0, 0))],
        out_specs=[out, out],
        out_shape=[jax.ShapeDtypeStruct((b, e, SC_LANES), F32), jax.ShapeDtypeStruct((b, e, SC_LANES), jnp.int32)],
        compiler_params=_cparams(("parallel",)),
        name="route_threshold",
    )(aff_t)


def _route_compact(aff, thr, need, cap):
    r, n = aff.shape
    lanes = SC_LANES
    assert r == SC_CORES * SC_SUBCORES and n % lanes == 0
    mesh = plsc.VectorSubcoreMesh(core_axis_name="core", subcore_axis_name="subcore",
                                  num_cores=SC_CORES, num_subcores=SC_SUBCORES)

    @pl.kernel(out_type=[jax.ShapeDtypeStruct((r, cap), jnp.int32), jax.ShapeDtypeStruct((r, cap), F32)],
               mesh=mesh,
               scratch_types=[pltpu.VMEM((n,), F32), pltpu.VMEM((lanes,), F32), pltpu.VMEM((lanes,), jnp.int32),
                              pltpu.VMEM((cap,), jnp.int32), pltpu.VMEM((cap,), F32)],
               compiler_params=pltpu.CompilerParams(needs_layout_passes=False),
               name="route_compact")
    def compact(aff_hbm, thr_hbm, need_hbm, idx_hbm, val_hbm, row_v, thr_v, need_v, idx_v, val_v):
        w = lax.axis_index("subcore") * SC_CORES + lax.axis_index("core")
        pltpu.sync_copy(aff_hbm.at[w], row_v)
        pltpu.sync_copy(thr_hbm.at[w], thr_v)
        pltpu.sync_copy(need_hbm.at[w], need_v)
        thr = thr_v[...]
        need = need_v[...]
        lane = lax.iota(jnp.int32, lanes)
        ones = jnp.ones((lanes,), jnp.int32)

        def body(i, carry):
            n_out, n_eq = carry
            x = row_v[pl.ds(i * lanes, lanes)]
            eq = x == thr
            take = (x > thr) | (eq & (n_eq + plsc.cumsum(ones, mask=eq) <= need))
            pos = n_out + plsc.cumsum(ones, mask=take) - 1
            take = take & (pos < cap)
            plsc.store_scatter(idx_v, [pos], lane + i * lanes, mask=take)
            plsc.store_scatter(val_v, [pos], x, mask=take)
            return (n_out + plsc.all_reduce_population_count(take),
                    n_eq + plsc.all_reduce_population_count(eq))

        zero = jnp.zeros((lanes,), jnp.int32)
        lax.fori_loop(0, n // lanes, body, (zero, zero))
        pltpu.sync_copy(idx_v, idx_hbm.at[w])
        pltpu.sync_copy(val_v, val_hbm.at[w])

    return compact(aff, thr, need)


def _route(aff_t, tok_base, row_base):
    b, e, n = aff_t.shape
    cap = EC_CAPACITY_FACTOR * n // N_EXPERTS
    thr, need = _route_threshold(aff_t, cap)
    idx, vals = _route_compact(aff_t.reshape(b * e, n), thr.reshape(b * e, -1), need.reshape(b * e, -1), cap)
    idx = idx.reshape(b, e, cap)
    vals = vals.reshape(b, e, cap)
    bi = jnp.arange(b, dtype=idx.dtype)[:, None, None]
    per_expert = lambda a: jnp.swapaxes(a, 0, 1).reshape(e, b * cap)
    rows0 = idx + row_base + 2 * bi * n
    return (per_expert(vals), per_expert(idx + tok_base + bi * n), per_expert(rows0), per_expert(rows0 + n),
            per_expert(idx))


def _combine_first_start(p0, cap):
    return jnp.minimum((p0 // BF16_ROWS) * BF16_ROWS, cap - COMBINE_FIRST)


def _combine_body(offs_ref, spill_ref, tok_ref, y_ref, x_ref, gate_ref, g_ref, o_ref, f_s, *, n_tok, cap):
    bi = pl.program_id(0)
    j = pl.program_id(1)
    tt = COMBINE_TILE
    wf = COMBINE_FIRST
    wn = COMBINE_WINDOW
    ntiles = n_tok // tt
    n_exp = y_ref.shape[0]
    group = wn // wf
    sub = lax.broadcasted_iota(jnp.int32, (tt, 1), 0)
    lane = lax.broadcasted_iota(jnp.int32, (1, wn), 1)

    def slot_range(e):
        base = (bi * n_exp + e) * (ntiles + 1) + j
        return offs_ref[base], offs_ref[base + 1]

    acc = jnp.zeros(f_s.shape, F32)
    for g in range(n_exp // group):
        ys = []
        toks = jnp.full((1, wn), -1, jnp.int32)
        for k in range(group):
            e = g * group + k
            start = pl.multiple_of(_combine_first_start(slot_range(e)[0], cap), BF16_ROWS)
            ys.append(y_ref[e, pl.ds(start, wf), :])
            cs = pl.multiple_of(jnp.minimum((start // LANES) * LANES, cap - wn), LANES)
            rolled = pltpu.roll(tok_ref[e, :, pl.ds(cs, wn)], (k * wf + wn - (start - cs)) % wn, 1)
            toks = jnp.where((lane >= k * wf) & (lane < (k + 1) * wf), rolled, toks)
        hit = (toks - j * tt) == sub
        acc = acc + _dot(jnp.where(hit, 1.0, 0.0).astype(BF16), jnp.concatenate(ys, axis=0))
    f_s[...] = acc

    def more_windows(e, carry):
        p0, p1 = slot_range(e)
        lo = _combine_first_start(p0, cap) + wf
        first = (lo // LANES) * LANES

        def extra(w, carry):
            cs = pl.multiple_of(jnp.minimum(first + w * wn, cap - wn), LANES)
            tok = tok_ref[e, :, pl.ds(cs, wn)] - j * tt
            hit = (tok == sub) & (cs + lane >= jnp.maximum(lo, first + w * wn))
            f_s[...] += _dot(jnp.where(hit, 1.0, 0.0).astype(BF16), y_ref[e, pl.ds(cs, wn), :])
            return carry

        lax.fori_loop(0, (jnp.maximum(p1 - first, 0) + wn - 1) // wn * (p1 > lo).astype(jnp.int32), extra, 0)
        return carry

    @pl.when(spill_ref[bi * ntiles + j] != 0)
    def _():
        lax.fori_loop(0, n_exp, more_windows, 0)

    o_ref[0] = x_ref[0] + gate_ref[0] * _rms(f_s[...], g_ref[...])


def _combine(y, tok, local, seg0, x, gate, g):
    e, _, d = y.shape
    _, b, cap = local.shape
    n_tok = x.shape[1]
    tt = COMBINE_TILE
    wf = COMBINE_FIRST
    ntiles = n_tok // tt
    assert seg0 % cap == 0 and n_tok % tt == 0 and cap % COMBINE_WINDOW == 0 and e % (COMBINE_WINDOW // wf) == 0
    seg = seg0 // cap
    bounds = jnp.arange(ntiles + 1, dtype=jnp.int32) * tt
    offs = jnp.sum((local[..., None] < bounds).astype(jnp.int32), axis=2)
    starts = _combine_first_start(offs[..., :-1], cap)
    spill = jnp.any(offs[..., 1:] > starts + wf, axis=0).astype(jnp.int32).reshape(-1)
    row = pl.BlockSpec((1, tt, d), lambda i, j, offs, spill: (i, j, 0))
    grid_spec = pltpu.PrefetchScalarGridSpec(
        num_scalar_prefetch=2,
        grid=(b, ntiles),
        in_specs=[pl.BlockSpec((e, 1, cap), lambda i, j, offs, spill: (0, 0, seg + i)),
                  pl.BlockSpec((e, cap, d), lambda i, j, offs, spill: (0, seg + i, 0),
                               pipeline_mode=pl.Buffered(1)),
                  row,
                  pl.BlockSpec((1, 1, d), lambda i, j, offs, spill: (i, 0, 0)),
                  pl.BlockSpec((1, d), lambda i, j, offs, spill: (0, 0))],
        out_specs=row,
        scratch_shapes=[pltpu.VMEM((tt, d), F32)],
    )
    return pl.pallas_call(
        functools.partial(_combine_body, n_tok=n_tok, cap=cap),
        grid_spec=grid_spec,
        out_shape=jax.ShapeDtypeStruct((b, n_tok, d), F32),
        compiler_params=_cparams(("parallel", "arbitrary")),
        name="combine",
    )(jnp.swapaxes(offs, 0, 1).reshape(-1), spill, tok, y, x, gate.reshape(b, 1, d), g.reshape(1, d))


def _gather_rows(src, idx):
    window = SC_GATHER_WINDOW
    n = idx.shape[0]
    width = src.shape[1]
    assert 2 * window * width * 4 <= SC_TILE_VMEM_BUDGET, width
    assert n % (window * SC_CORES * SC_SUBCORES) == 0, n
    mesh = plsc.VectorSubcoreMesh(core_axis_name="core", subcore_axis_name="subcore",
                                  num_cores=SC_CORES, num_subcores=SC_SUBCORES)

    @pl.kernel(out_type=jax.ShapeDtypeStruct((n, width), src.dtype), mesh=mesh, scratch_types=[],
               name="gather_rows")
    def gather(src_hbm, idx_hbm, out_hbm):
        def body(idx_vmem, out_vmem):
            pltpu.sync_copy(src_hbm.at[idx_vmem.at[0]], out_vmem)

        pltpu.emit_pipeline(
            body,
            grid=(n // window,),
            in_specs=[pl.BlockSpec((1, window), lambda i: (0, i))],
            out_specs=[pl.BlockSpec((window, width), lambda i: (i, 0))],
            core_axis_name=("core", "subcore"),
            dimension_semantics=(pltpu.PARALLEL,),
        )(idx_hbm, out_hbm)

    return gather(src, idx.reshape(1, n))


def _moe(parts, g, w_gate, w_up, w_down, layer):
    quarter = parts[0][1].shape[-1]
    d = 4 * quarter
    sizes = [p[1].shape[0] * p[1].shape[2] for p in parts]
    bases = [sum(sizes[:i]) for i in range(len(parts))]
    routed = [_route(p[0], base, 2 * base) for p, base in zip(parts, bases)]
    src = jnp.concatenate([p[1].reshape(-1, quarter) for p in parts], axis=0)
    vals, flat, rows0, rows1, tok = (jnp.concatenate([r[i] for r in routed], axis=1) for i in range(5))
    e, m = flat.shape
    rows = jnp.stack([rows0, rows1], axis=1).reshape(-1)
    unit = SC_GATHER_WINDOW * SC_CORES * SC_SUBCORES
    fill = jnp.arange(-rows.shape[0] % unit, dtype=rows.dtype)
    xg = _gather_rows(src, jnp.concatenate([rows, fill]))
    chunk = next(c for c in (512, 528, 384, 320, 256, 128) if m % c == 0)
    y = _expert_ffn(xg, vals[..., None], w_gate, w_up, w_down, layer, chunk)
    outs = []
    seg0 = 0
    for (aff_t, _, x, gate), r, base in zip(parts, routed, bases):
        b, _, n = aff_t.shape
        cap = r[0].shape[1] // b
        if n % COMBINE_TILE == 0 and cap % COMBINE_WINDOW == 0 and seg0 % cap == 0:
            outs.append(_combine(y, tok[:, None, :], r[4].reshape(e, b, cap), seg0, x, gate, g))
        else:
            ids = r[1] - base
            part = y[:, seg0:seg0 + b * cap].astype(F32)
            f = jnp.zeros((b * n, d), F32).at[ids.reshape(-1)].add(part.reshape(-1, d))
            outs.append(_gated_residual(x, f.reshape(b, n, d), gate, g, n))
        seg0 += b * cap
    return outs


def kernel(x, c, ctx, c_ctx, w_mod, b_mod, norm_g, w_in_even, w_out_even, conv_w, conv_b, conv_ln_g,
           conv_ln_b, na_rpb, w_in_odd, w_out_odd, pool_w, pool_scale, gla_gate_w, gla_gate_b, gla_head_g,
           router_w, expert_w_gate, expert_w_up, expert_w_down):
    b, l, d = x.shape
    n_ctx = ctx.shape[1]
    tm = ROW_TILE

    mod_rows = jnp.concatenate([c, c_ctx[None], jnp.zeros((8 - b - 1, d), F32)], axis=0)

    mod_all = _modulation(mod_rows, w_mod, b_mod)

    def modulation(i):
        mm = mod_all[i]
        m = mm[:b].reshape(b, 6, d)
        mc = jnp.broadcast_to(mm[b].reshape(1, 6, d), (b, 6, d))
        return m, mc

    m, mc = modulation(0)
    g = norm_g[0]
    w_in = w_in_even[0].astype(BF16)
    w_out = w_out_even[0].astype(BF16)
    glu, qkv = _even_in(x, g[0], m[:, 0], m[:, 1], w_in, tm)
    glu_c, qkv_c = _even_in(ctx, g[0], mc[:, 0], mc[:, 1], w_in, n_ctx)
    a_ctx = _conv_branch(glu_c, conv_w[0], conv_b[0], conv_ln_g[0], conv_ln_b[0], n_ctx)
    na = _neighbourhood_attention(qkv, qkv_c, na_rpb[0])
    att_c = _context_attention(qkv_c)
    x, h2, aff = _even_out(glu, conv_w[0], conv_b[0], conv_ln_g[0], conv_ln_b[0], na,
                           x, w_out, g[1], m[:, 2], g[2], m[:, 3], m[:, 4], router_w[0], tm)
    ctx, h2c, aff_c = _out_proj(a_ctx, att_c, ctx, w_out, g[1], mc[:, 2], g[2], mc[:, 3], mc[:, 4],
                                router_w[0], n_ctx)
    x, ctx = _moe([(aff, h2, x, m[:, 5]), (aff_c, h2c, ctx, mc[:, 5])], g[3],
                  expert_w_gate, expert_w_up, expert_w_down, 0)

    m, mc = modulation(1)
    g = norm_g[1]
    w_in = w_in_odd[0].astype(BF16)
    w_out = w_out_odd[0].astype(BF16)
    cos_t, sin_t = _rope_tables(l)
    ones_t = jnp.ones((n_ctx, GLA_HEADS * GLA_DK), F32)
    _, qk_c, v_c, _, gate_c = _odd_in(ctx, g[0], mc[:, 0], mc[:, 1], w_in, ones_t, jnp.zeros_like(ones_t),
                                      gla_gate_w[0], gla_gate_b[0], n_ctx)
    s_zero = jnp.zeros((b, GLA_HEADS, GLA_DK, GLA_DV), F32)
    _, _, s_f, s_b = _gla(qk_c, v_c, gate_c, s_zero, s_zero, n_ctx)
    pool_u, qk, v, r, gate = _odd_in(x, g[0], m[:, 0], m[:, 1], w_in, cos_t, sin_t,
                                     gla_gate_w[0], gla_gate_b[0], tm)
    o_f, o_b, _, _ = _gla(qk, v, gate, s_f, s_b, SEQ_TILE)
    x, h2, aff = _odd_out(pool_u, o_f, o_b, r, gla_head_g[0], pool_w[0].astype(BF16), pool_scale[0],
                          x, w_out, g[1], m[:, 2], g[2], m[:, 3], m[:, 4], router_w[1], tm)
    (x,) = _moe([(aff, h2, x, m[:, 5])], g[3], expert_w_gate, expert_w_up, expert_w_down, 1)
    return x
```

```python
import functools

import jax
import jax.numpy as jnp
import numpy as np
from jax import lax
from jax.experimental import pallas as pl
from jax.experimental.pallas import tpu as pltpu
from jax.experimental.pallas import tpu_sc as plsc

F32 = jnp.float32
BF16 = jnp.bfloat16
HIGHEST = lax.Precision.HIGHEST

D_MODEL = 1024
GRID_W = 64
EPS = 1e-6
CONV_CH = 512
CONV_WIDTH = 31
CONV_HALO = 16
CONV_CHUNK = 32
NA_HEADS = 8
NA_HEAD_DIM = 64
NA_KR = 8
NA_KC = 16
NA_ROWS_PER_BLOCK = 4
NA_BLOCKS_PER_STEP = 4
NA_WIN_ROWS = 12
POOL_CH = 512
POOL_WINDOWS = (2, 4, 8, 16)
POOL_GROUP = 128
POOL_HALO = 8
GLA_HEADS = 4
GLA_DK = 64
GLA_DV = 128
GLA_RANK = 16
GLA_TAU = 16.0
GLA_CHUNK = 64
ROPE_BASE = 10000.0
N_EXPERTS = 16
EC_CAPACITY_FACTOR = 2
ROW_TILE = 512
SEQ_TILE = 256
MOD_COLS = 1536
EXPERT_FF_TILE = 256
LANES = 128
SUBLANES = 8
NEG_BIG = -1e30
VMEM_LIMIT = 56 * 1024 * 1024
SC_CORES = 2
SC_SUBCORES = 16
SC_LANES = 16
F32_INF_BITS = 0x7F800000
SC_TILE_VMEM_BUDGET = 400 * 1024
SC_GATHER_WINDOW = 128
COMBINE_TILE = 256
COMBINE_WINDOW = 256
COMBINE_FIRST = 64
BF16_ROWS = 16


def _cparams(sem):
    return pltpu.CompilerParams(dimension_semantics=sem, vmem_limit_bytes=VMEM_LIMIT)


def _rms(x, g):
    return x * lax.rsqrt(jnp.mean(x * x, axis=-1, keepdims=True) + EPS) * g


def _sigmoid(x):
    return 1.0 / (1.0 + jnp.exp(-x))


def _silu(x):
    return x * _sigmoid(x)


def _dot(a, b):
    return jnp.dot(a, b, preferred_element_type=F32)


def _pack_bf16_pairs(h):
    half = h.shape[-1] // 2
    bits = lax.bitcast_convert_type(h.astype(BF16).astype(F32), jnp.uint32)
    packed = (bits[:, half:] & jnp.uint32(0xFFFF0000)) | (bits[:, :half] >> 16)
    return lax.bitcast_convert_type(packed, jnp.int32)


def _unpack_bf16_pairs(p):
    bits = lax.bitcast_convert_type(p, jnp.uint32)
    lo = lax.bitcast_convert_type(bits << 16, F32).astype(BF16)
    hi = lax.bitcast_convert_type(bits & jnp.uint32(0xFFFF0000), F32).astype(BF16)
    return lo, hi


def _dot_nt(a, b):
    return lax.dot_general(a, b, (((1,), (1,)), ((), ())), preferred_element_type=F32)


def _mod_body(c_ref, w_ref, b_ref, o_ref):
    o_ref[0] = jnp.dot(_silu(c_ref[...]), w_ref[0], precision=HIGHEST,
                       preferred_element_type=F32) + b_ref[0]


def _modulation(rows, w, b):
    depth, _, n = w.shape
    tn = MOD_COLS
    return pl.pallas_call(
        _mod_body,
        grid=(depth, n // tn),
        in_specs=[pl.BlockSpec((8, D_MODEL), lambda i, j: (0, 0)),
                  pl.BlockSpec((1, D_MODEL, tn), lambda i, j: (i, 0, j)),
                  pl.BlockSpec((1, 1, tn), lambda i, j: (i, 0, j))],
        out_specs=pl.BlockSpec((1, 8, tn), lambda i, j: (i, 0, j)),
        out_shape=jax.ShapeDtypeStruct((depth, 8, n), F32),
        compiler_params=_cparams(("parallel", "parallel")),
        name="modulation",
    )(rows, w, b.reshape(depth, 1, n))


def _even_in_body(x_ref, g_ref, sh_ref, sc_ref, w_ref, glu_ref, qkv_ref):
    h = (_rms(x_ref[0], g_ref[...]) * (1.0 + sc_ref[0]) + sh_ref[0]).astype(BF16)
    c = CONV_CH
    glu_ref[0] = _dot(h, w_ref[:, 0:c]) * _sigmoid(_dot(h, w_ref[:, c:2 * c]))
    hd = NA_HEADS * NA_HEAD_DIM
    q0 = 2 * c
    qkv_ref[0, :, 0:hd] = (_dot(h, w_ref[:, q0:q0 + hd]) * (NA_HEAD_DIM ** -0.5)).astype(BF16)
    qkv_ref[0, :, hd:3 * hd] = _dot(h, w_ref[:, q0 + hd:q0 + 3 * hd]).astype(BF16)


def _even_in(x, g, shift, scale, w_bf, tm):
    b, l, d = x.shape
    n = w_bf.shape[1]
    hd3 = 3 * NA_HEADS * NA_HEAD_DIM
    vec = pl.BlockSpec((1, 1, d), lambda i, j: (i, 0, 0))
    return pl.pallas_call(
        _even_in_body,
        grid=(b, l // tm),
        in_specs=[pl.BlockSpec((1, tm, d), lambda i, j: (i, j, 0)),
                  pl.BlockSpec((1, d), lambda i, j: (0, 0)),
                  vec, vec,
                  pl.BlockSpec((d, n), lambda i, j: (0, 0))],
        out_specs=[pl.BlockSpec((1, tm, CONV_CH), lambda i, j: (i, j, 0)),
                   pl.BlockSpec((1, tm, hd3), lambda i, j: (i, j, 0))],
        out_shape=[jax.ShapeDtypeStruct((b, l, CONV_CH), F32),
                   jax.ShapeDtypeStruct((b, l, hd3), BF16)],
        compiler_params=_cparams(("parallel", "parallel")),
        name="even_in",
    )(x, g.reshape(1, d), shift.reshape(b, 1, d), scale.reshape(b, 1, d), w_bf)


def _conv_body(cur_ref, prev_ref, next_ref, w_ref, b_ref, lg_ref, lb_ref, o_ref, buf_ref, sh_ref, *, tile, chunk):
    j = pl.program_id(1)
    last = pl.num_programs(1) - 1
    hal = CONV_HALO
    buf_ref[0:hal, :] = jnp.where(j == 0, 0.0, prev_ref[0])
    buf_ref[hal:hal + tile, :] = cur_ref[0]
    buf_ref[hal + tile:hal + tile + hal, :] = jnp.where(j == last, 0.0, next_ref[0])
    span = sh_ref.shape[1]
    for s in range(SUBLANES):
        sh_ref[s] = buf_ref[s:s + span, :]
    first = hal - CONV_WIDTH // 2
    reps = chunk // SUBLANES

    last_a = (first + CONV_WIDTH - 1) // SUBLANES

    def rows(c, carry):
        r0 = pl.multiple_of(c * chunk, chunk)
        acc = jnp.zeros((chunk, CONV_CH), F32)
        for s in range(SUBLANES):
            rows_s = sh_ref[s, pl.ds(r0, chunk + last_a * SUBLANES), :]
            for a in range(last_a + 1):
                k = a * SUBLANES + s - first
                if 0 <= k < CONV_WIDTH:
                    wk = jnp.concatenate([w_ref[k]] * reps, axis=0)
                    acc = acc + rows_s[a * SUBLANES:a * SUBLANES + chunk, :] * wk
        o_ref[0, pl.ds(r0, chunk), :] = acc
        return carry

    lax.fori_loop(0, tile // chunk, rows, 0)
    y = o_ref[0] + b_ref[...]
    mu = jnp.mean(y, axis=-1, keepdims=True)
    yc = y - mu
    var = jnp.mean(yc * yc, axis=-1, keepdims=True)
    o_ref[0] = _silu(yc * lax.rsqrt(var + EPS) * lg_ref[...] + lb_ref[...])


def _conv_operands(glu, conv_w, conv_b, ln_g, ln_b, tile):
    _, l, c = glu.shape
    hal = CONV_HALO
    per = tile // hal
    nh = l // hal
    vec = pl.BlockSpec((1, c), lambda i, j: (0, 0))
    specs = [pl.BlockSpec((1, tile, c), lambda i, j: (i, j, 0)),
             pl.BlockSpec((1, hal, c), lambda i, j: (i, jnp.maximum(j * per - 1, 0), 0)),
             pl.BlockSpec((1, hal, c), lambda i, j: (i, jnp.minimum((j + 1) * per, nh - 1), 0)),
             pl.BlockSpec((CONV_WIDTH, SUBLANES, c), lambda i, j: (0, 0, 0)),
             vec, vec, vec]
    args = [glu, glu, glu, jnp.broadcast_to(conv_w[:, None, :], (CONV_WIDTH, SUBLANES, c)),
            conv_b.reshape(1, c), ln_g.reshape(1, c), ln_b.reshape(1, c)]
    scratch = [pltpu.VMEM((tile + 2 * hal, c), F32),
               pltpu.VMEM((SUBLANES, tile + 2 * hal - SUBLANES, c), F32)]
    return args, specs, scratch


def _conv_branch(glu, conv_w, conv_b, ln_g, ln_b, tile):
    b, l, c = glu.shape
    args, specs, scratch = _conv_operands(glu, conv_w, conv_b, ln_g, ln_b, tile)
    return pl.pallas_call(
        functools.partial(_conv_body, tile=tile, chunk=CONV_CHUNK),
        grid=(b, l // tile),
        in_specs=specs,
        out_specs=pl.BlockSpec((1, tile, c), lambda i, j: (i, j, 0)),
        out_shape=jax.ShapeDtypeStruct((b, l, c), F32),
        scratch_shapes=scratch,
        compiler_params=_cparams(("parallel", "parallel")),
        name="conv_branch",
    )(*args)


def _na_window_start(j, rows):
    rb = NA_ROWS_PER_BLOCK
    return jnp.clip(j * rb - NA_KR // 2, 0, rows - NA_WIN_ROWS)


def _na_body(q_ref, k_ref, v_ref, kc_ref, vc_ref, *rest, rows):
    tab_refs, o_ref = rest[:-1], rest[-1]
    j = pl.program_id(2)
    nkeys = NA_WIN_ROWS * GRID_W
    tq = NA_ROWS_PER_BLOCK * GRID_W
    kc = kc_ref[0]
    vc = vc_ref[0]
    lane = lax.broadcasted_iota(jnp.int32, (1, LANES), 1)
    for sb, tab_ref in enumerate(tab_refs):
        start = pl.multiple_of(_na_window_start(j * len(tab_refs) + sb, rows) * GRID_W, GRID_W)
        q = q_ref[0, sb * tq:(sb + 1) * tq, :]
        kw = k_ref[0, pl.ds(start, nkeys), :]
        vw = v_ref[0, pl.ds(start, nkeys), :]
        out = jnp.zeros(q.shape, F32)
        for hh in range(LANES // NA_HEAD_DIM):
            in_head = (lane >= hh * NA_HEAD_DIM) & (lane < (hh + 1) * NA_HEAD_DIM)
            qh = jnp.where(in_head, q, jnp.zeros_like(q))
            s = _dot_nt(qh, kw) + tab_ref[0, hh]
            sc = _dot_nt(qh, kc)
            m = jnp.maximum(jnp.max(s, axis=-1, keepdims=True), jnp.max(sc, axis=-1, keepdims=True))
            p = jnp.exp(s - m)
            pc = jnp.exp(sc - m)
            denom = jnp.sum(p, axis=-1, keepdims=True) + jnp.sum(pc, axis=-1, keepdims=True)
            o = (_dot(p.astype(BF16), vw) + _dot(pc.astype(BF16), vc)) / denom
            out = jnp.where(in_head, o, out)
        o_ref[0, sb * tq:(sb + 1) * tq, :] = out


def _na_tables(rpb, rows):
    rb = NA_ROWS_PER_BLOCK
    nblk = rows // rb
    wr = NA_WIN_ROWS
    qc = jnp.arange(GRID_W)
    cs = jnp.clip(qc - NA_KC // 2, 0, GRID_W - NA_KC)
    col_ok = (qc[None, :] >= cs[:, None]) & (qc[None, :] < cs[:, None] + NA_KC)
    col_off = qc[None, :] - qc[:, None] + NA_KC - 1
    onehot = (col_off[:, :, None] == jnp.arange(2 * NA_KC - 1)[None, None, :]).astype(F32)
    blocks = jnp.einsum('hrd,qkd->hrqk', rpb.astype(F32), onehot, precision=HIGHEST)
    blocks = jnp.where(col_ok[None, None], blocks, NEG_BIG)
    masked = jnp.full((NA_HEADS, GRID_W, GRID_W), NEG_BIG, F32)
    tabs = []
    for jb in (0, 1, nblk - 1):
        ws = min(max(jb * rb - NA_KR // 2, 0), rows - wr)
        q_rows = []
        for qr in range(jb * rb, (jb + 1) * rb):
            rs = min(max(qr - NA_KR // 2, 0), rows - NA_KR)
            row = [blocks[:, kr - qr + NA_KR - 1] if rs <= kr < rs + NA_KR else masked
                   for kr in range(ws, ws + wr)]
            q_rows.append(jnp.concatenate(row, axis=-1))
        tabs.append(jnp.concatenate(q_rows, axis=1))
    return jnp.stack(tabs)


def _neighbourhood_attention(qkv, qkv_ctx, rpb):
    b, l, _ = qkv.shape
    n_ctx = qkv_ctx.shape[1]
    rows = l // GRID_W
    rb = NA_ROWS_PER_BLOCK
    nblk = rows // rb
    tq = rb * GRID_W
    nkeys = NA_WIN_ROWS * GRID_W
    hp = NA_HEADS * NA_HEAD_DIM // LANES
    tabs = _na_tables(rpb, rows)

    per = NA_BLOCKS_PER_STEP

    def cls(jb):
        return jnp.where(jb == 0, 0, jnp.where(jb == nblk - 1, 2, 1))

    def tab_spec(sb):
        return pl.BlockSpec((1, 2, tq, nkeys), lambda i, h, j: (cls(j * per + sb), h, 0, 0))

    return pl.pallas_call(
        functools.partial(_na_body, rows=rows),
        grid=(b, hp, nblk // per),
        in_specs=[pl.BlockSpec((1, per * tq, LANES), lambda i, h, j: (i, j, h)),
                  pl.BlockSpec((1, l, LANES), lambda i, h, j: (i, 0, hp + h)),
                  pl.BlockSpec((1, l, LANES), lambda i, h, j: (i, 0, 2 * hp + h)),
                  pl.BlockSpec((1, n_ctx, LANES), lambda i, h, j: (i, 0, hp + h)),
                  pl.BlockSpec((1, n_ctx, LANES), lambda i, h, j: (i, 0, 2 * hp + h))]
                 + [tab_spec(sb) for sb in range(per)],
        out_specs=pl.BlockSpec((1, per * tq, LANES), lambda i, h, j: (i, j, h)),
        out_shape=jax.ShapeDtypeStruct((b, l, NA_HEADS * NA_HEAD_DIM), F32),
        compiler_params=_cparams(("parallel", "parallel", "arbitrary")),
        name="neighbourhood_attention",
    )(qkv, qkv, qkv, qkv_ctx, qkv_ctx, *([tabs] * per))


def _ctx_attn_body(q_ref, k_ref, v_ref, o_ref):
    q = q_ref[0]
    k = k_ref[0]
    v = v_ref[0]
    lane = lax.broadcasted_iota(jnp.int32, (1, LANES), 1)
    out = jnp.zeros(q.shape, F32)
    for hh in range(LANES // NA_HEAD_DIM):
        in_head = (lane >= hh * NA_HEAD_DIM) & (lane < (hh + 1) * NA_HEAD_DIM)
        qh = jnp.where(in_head, q, jnp.zeros_like(q))
        s = _dot_nt(qh, k)
        p = jnp.exp(s - jnp.max(s, axis=-1, keepdims=True))
        o = _dot(p.astype(BF16), v) / jnp.sum(p, axis=-1, keepdims=True)
        out = jnp.where(in_head, o, out)
    o_ref[0] = out


def _context_attention(qkv_ctx):
    b, n, _ = qkv_ctx.shape
    hp = NA_HEADS * NA_HEAD_DIM // LANES
    return pl.pallas_call(
        _ctx_attn_body,
        grid=(b, hp),
        in_specs=[pl.BlockSpec((1, n, LANES), lambda i, h: (i, 0, h)),
                  pl.BlockSpec((1, n, LANES), lambda i, h: (i, 0, hp + h)),
                  pl.BlockSpec((1, n, LANES), lambda i, h: (i, 0, 2 * hp + h))],
        out_specs=pl.BlockSpec((1, n, LANES), lambda i, h: (i, 0, h)),
        out_shape=jax.ShapeDtypeStruct((b, n, NA_HEADS * NA_HEAD_DIM), F32),
        compiler_params=_cparams(("parallel", "parallel")),
        name="context_attention",
    )(qkv_ctx, qkv_ctx, qkv_ctx)


def _out_body(a_ref, b_ref, *rest):
    _out_tail(a_ref[0], b_ref[0], *rest)


def _out_tail(a, b2, x_ref, w_ref, g1_ref, gate_ref, g2_ref, sh_ref, sc_ref, rw_ref, xo_ref, h_ref, aff_ref):
    half = a.shape[-1]
    y = _dot(a.astype(BF16), w_ref[0:half, :]) + _dot(b2.astype(BF16), w_ref[half:2 * half, :])
    xn = x_ref[0] + gate_ref[0] * _rms(y, g1_ref[...])
    xo_ref[0] = xn
    h = _rms(xn, g2_ref[...]) * (1.0 + sc_ref[0]) + sh_ref[0]
    packed = _pack_bf16_pairs(h)
    quarter = packed.shape[-1] // 2
    h_ref[0, 0] = packed[:, 0:quarter]
    h_ref[0, 1] = packed[:, quarter:2 * quarter]
    h_hi = h.astype(BF16)
    h_lo = (h - h_hi.astype(F32)).astype(BF16)
    both = _dot(h_hi, rw_ref[...])
    logits = both[:, 0:LANES] + (both[:, LANES:2 * LANES] + _dot(h_lo, rw_ref[:, 0:LANES]))
    lane = lax.broadcasted_iota(jnp.int32, (1, LANES), 1)
    logits = jnp.where(lane < N_EXPERTS, logits, NEG_BIG)
    e = jnp.exp(logits - jnp.max(logits, axis=-1, keepdims=True))
    aff = e / jnp.sum(e, axis=-1, keepdims=True)
    aff_ref[0] = aff.T[0:N_EXPERTS, :]


def _out_call(body, branch_args, branch_specs, scratch, name, x, w_bf, g1, gate, g2, shift, scale, router_w, tm):
    b, l, d = x.shape
    rw = jnp.pad(router_w, ((0, 0), (0, LANES - N_EXPERTS)))
    rw_hi = rw.astype(BF16)
    rw_cat = jnp.concatenate([rw_hi, (rw - rw_hi.astype(F32)).astype(BF16)], axis=1)
    vec = pl.BlockSpec((1, d), lambda i, j: (0, 0))
    bvec = pl.BlockSpec((1, 1, d), lambda i, j: (i, 0, 0))
    rspec = pl.BlockSpec((d, 2 * LANES), lambda i, j: (0, 0))
    return pl.pallas_call(
        body,
        grid=(b, l // tm),
        in_specs=list(branch_specs) + [pl.BlockSpec((1, tm, d), lambda i, j: (i, j, 0)),
                                       pl.BlockSpec(w_bf.shape, lambda i, j: (0, 0)),
                                       vec, bvec, vec, bvec, bvec, rspec],
        out_specs=[pl.BlockSpec((1, tm, d), lambda i, j: (i, j, 0)),
                   pl.BlockSpec((1, 2, tm, d // 4), lambda i, j: (i, 0, j, 0)),
                   pl.BlockSpec((1, N_EXPERTS, tm), lambda i, j: (i, 0, j))],
        out_shape=[jax.ShapeDtypeStruct((b, l, d), F32),
                   jax.ShapeDtypeStruct((b, 2, l, d // 4), jnp.int32),
                   jax.ShapeDtypeStruct((b, N_EXPERTS, l), F32)],
        scratch_shapes=list(scratch),
        compiler_params=_cparams(("parallel", "parallel")),
        name=name,
    )(*branch_args, x, w_bf, g1.reshape(1, d), gate.reshape(b, 1, d), g2.reshape(1, d),
      shift.reshape(b, 1, d), scale.reshape(b, 1, d), rw_cat)


def _even_out_body(*refs, tile):
    n_conv = 7
    buf_ref, sh_ref, a_s = refs[-3:]
    _conv_body(*refs[:n_conv], a_s, buf_ref, sh_ref, tile=tile, chunk=CONV_CHUNK)
    _out_tail(a_s[0], refs[n_conv][0], *refs[n_conv + 1:-3])


def _even_out(glu, conv_w, conv_b, ln_g, ln_b, na, *common):
    tile = common[-1]
    c = glu.shape[-1]
    args, specs, scratch = _conv_operands(glu, conv_w, conv_b, ln_g, ln_b, tile)
    na_spec = pl.BlockSpec((1, tile, na.shape[-1]), lambda i, j: (i, j, 0))
    return _out_call(functools.partial(_even_out_body, tile=tile), args + [na], specs + [na_spec],
                     scratch + [pltpu.VMEM((1, tile, c), F32)], "even_out", *common)


def _out_proj(a, b2, *common):
    tm = common[-1]
    spec = pl.BlockSpec((1, tm, a.shape[-1]), lambda i, j: (i, j, 0))
    return _out_call(_out_body, [a, b2], [spec, spec], [], "out_proj", *common)


def _moe_body(x0_ref, x1_ref, val_ref, wg_ref, wu_ref, wd_ref, o_ref, acc_s, x_s, wg_s, wu_s, wd_s, *, chunk):
    f = pl.program_id(1)
    m, quarter = x0_ref.shape

    @pl.when(f == 0)
    def _():
        def unpack(c, carry):
            r = pl.multiple_of(c * chunk, chunk)
            for s, x_ref in enumerate((x0_ref, x1_ref)):
                lo, hi = _unpack_bf16_pairs(x_ref[pl.ds(r, chunk), :])
                x_s[pl.ds(r, chunk), s * quarter:(s + 1) * quarter] = lo
                x_s[pl.ds(r, chunk), (2 + s) * quarter:(3 + s) * quarter] = hi
            acc_s[pl.ds(r, chunk), :] = jnp.zeros((chunk, 4 * quarter), F32)
            return carry

        lax.fori_loop(0, m // chunk, unpack, 0)

    wg_s[...] = wg_ref[0, 0].astype(BF16)
    wu_s[...] = wu_ref[0, 0].astype(BF16)
    wd_s[...] = wd_ref[0, 0].astype(BF16)

    def rows(c, carry):
        r = pl.multiple_of(c * chunk, chunk)
        xs = x_s[pl.ds(r, chunk), :]
        hid = (_silu(_dot(xs, wg_s[...])) * _dot(xs, wu_s[...])).astype(BF16)
        acc_s[pl.ds(r, chunk), :] += _dot(hid, wd_s[...])
        return carry

    lax.fori_loop(0, m // chunk, rows, 0, unroll=True)

    @pl.when(f == pl.num_programs(1) - 1)
    def _():
        for g0 in range(0, m, LANES):
            w = min(LANES, m - g0)
            eye = lax.broadcasted_iota(jnp.int32, (w, w), 0) == lax.broadcasted_iota(jnp.int32, (w, w), 1)
            col = jnp.sum(jnp.where(eye, val_ref[0, :, g0:g0 + w], 0.0), axis=1, keepdims=True)
            o_ref[0, g0:g0 + w, :] = (acc_s[g0:g0 + w, :] * col).astype(o_ref.dtype)


def _expert_ffn(xg, vals, w_gate, w_up, w_down, layer, chunk):
    e, _, m = vals.shape
    quarter = xg.shape[1]
    d = 4 * quarter
    ff = w_gate.shape[-1]
    tf = EXPERT_FF_TILE
    return pl.pallas_call(
        functools.partial(_moe_body, chunk=chunk),
        grid=(e, ff // tf),
        in_specs=[pl.BlockSpec((m, quarter), lambda i, f: (2 * i, 0)),
                  pl.BlockSpec((m, quarter), lambda i, f: (2 * i + 1, 0)),
                  pl.BlockSpec((1, 1, m), lambda i, f: (i, 0, 0)),
                  pl.BlockSpec((1, 1, d, tf), lambda i, f: (layer, i, 0, f)),
                  pl.BlockSpec((1, 1, d, tf), lambda i, f: (layer, i, 0, f)),
                  pl.BlockSpec((1, 1, tf, d), lambda i, f: (layer, i, f, 0))],
        out_specs=pl.BlockSpec((1, m, d), lambda i, f: (i, 0, 0)),
        out_shape=jax.ShapeDtypeStruct((e, m, d), BF16),
        scratch_shapes=[pltpu.VMEM((m, d), F32), pltpu.VMEM((m, d), BF16), pltpu.VMEM((d, tf), BF16),
                        pltpu.VMEM((d, tf), BF16), pltpu.VMEM((tf, d), BF16)],
        compiler_params=_cparams(("parallel", "arbitrary")),
        name="expert_ffn",
    )(xg, xg, vals, w_gate, w_up, w_down)


def _resid_body(x_ref, f_ref, gate_ref, g_ref, o_ref):
    o_ref[0] = x_ref[0] + gate_ref[0] * _rms(f_ref[0], g_ref[...])


def _gated_residual(x, f, gate, g, tm):
    b, l, d = x.shape
    blk = pl.BlockSpec((1, tm, d), lambda i, j: (i, j, 0))
    return pl.pallas_call(
        _resid_body,
        grid=(b, l // tm),
        in_specs=[blk, blk, pl.BlockSpec((1, 1, d), lambda i, j: (i, 0, 0)),
                  pl.BlockSpec((1, d), lambda i, j: (0, 0))],
        out_specs=blk,
        out_shape=jax.ShapeDtypeStruct((b, l, d), F32),
        compiler_params=_cparams(("parallel", "parallel")),
        name="gated_residual",
    )(x, f, gate.reshape(b, 1, d), g.reshape(1, d))


def _swap_pairs(x):
    nf = GLA_DK // 4
    lane = lax.broadcasted_iota(jnp.int32, (1, LANES), 1)
    up = pltpu.roll(x, LANES - nf, 1)
    down = pltpu.roll(x, nf, 1)
    return jnp.where(lane % (2 * nf) < nf, up, down)


def _odd_in_body(x_ref, g_ref, sh_ref, sc_ref, w_ref, cos_ref, sin_ref, gw_ref, gb_ref,
                 pool_ref, qk_ref, v_ref, r_ref, gate_ref):
    h = (_rms(x_ref[0], g_ref[...]) * (1.0 + sc_ref[0]) + sh_ref[0]).astype(BF16)
    qk = GLA_HEADS * GLA_DK
    vd = GLA_HEADS * GLA_DV
    q0 = POOL_CH
    v0 = q0 + 2 * qk
    r0 = v0 + vd
    l0 = r0 + vd
    pool_ref[0] = _dot(h, w_ref[:, 0:q0])
    qk_raw = _dot(h, w_ref[:, q0:v0])
    for s in range(2 * qk // LANES):
        raw = qk_raw[:, s * LANES:(s + 1) * LANES]
        c = cos_ref[:, (s * LANES) % qk:(s * LANES) % qk + LANES]
        sn = sin_ref[:, (s * LANES) % qk:(s * LANES) % qk + LANES]
        rot = raw * c + _swap_pairs(raw) * sn
        if s * LANES < qk:
            rot = rot * (GLA_DK ** -0.5)
        qk_ref[0, :, s * LANES:(s + 1) * LANES] = rot
    v_ref[0] = _dot(h, w_ref[:, v0:r0]).astype(BF16)
    r_ref[0] = _dot(h, w_ref[:, r0:l0])
    lr = _dot(h, w_ref[:, l0:l0 + 2 * GLA_RANK])
    z = jnp.dot(lr, gw_ref[...], precision=HIGHEST, preferred_element_type=F32) + gb_ref[...]
    gate_ref[0] = (jnp.minimum(z, 0.0) - jnp.log1p(jnp.exp(-jnp.abs(z)))) * (1.0 / GLA_TAU)


def _odd_in(x, g, shift, scale, w_bf, cos_t, sin_t, gate_w, gate_b, tm):
    b, l, d = x.shape
    n = w_bf.shape[1]
    qk = GLA_HEADS * GLA_DK
    vd = GLA_HEADS * GLA_DV
    gw = jnp.zeros((2 * GLA_RANK, 2 * qk), F32)
    gw = gw.at[:GLA_RANK, :qk].set(gate_w[0]).at[GLA_RANK:, qk:].set(gate_w[1])
    gb = jnp.concatenate([gate_b[0], gate_b[1]]).reshape(1, 2 * qk)
    vec = pl.BlockSpec((1, 1, d), lambda i, j: (i, 0, 0))
    row = lambda w: pl.BlockSpec((1, tm, w), lambda i, j: (i, j, 0))
    return pl.pallas_call(
        _odd_in_body,
        grid=(b, l // tm),
        in_specs=[row(d), pl.BlockSpec((1, d), lambda i, j: (0, 0)), vec, vec,
                  pl.BlockSpec((d, n), lambda i, j: (0, 0)),
                  pl.BlockSpec((tm, qk), lambda i, j: (j, 0)),
                  pl.BlockSpec((tm, qk), lambda i, j: (j, 0)),
                  pl.BlockSpec((2 * GLA_RANK, 2 * qk), lambda i, j: (0, 0)),
                  pl.BlockSpec((1, 2 * qk), lambda i, j: (0, 0))],
        out_specs=[row(POOL_CH), row(2 * qk), row(vd), row(vd), row(2 * qk)],
        out_shape=[jax.ShapeDtypeStruct((b, l, POOL_CH), F32),
                   jax.ShapeDtypeStruct((b, l, 2 * qk), F32),
                   jax.ShapeDtypeStruct((b, l, vd), BF16),
                   jax.ShapeDtypeStruct((b, l, vd), F32),
                   jax.ShapeDtypeStruct((b, l, 2 * qk), F32)],
        compiler_params=_cparams(("parallel", "parallel")),
        name="odd_in",
    )(x, g.reshape(1, d), shift.reshape(b, 1, d), scale.reshape(b, 1, d), w_bf, cos_t, sin_t, gw, gb)


def _rope_tables(l):
    t = np.arange(l)
    nf = GLA_DK // 4
    inv = np.power(ROPE_BASE, -np.arange(nf, dtype=np.float64) / nf)
    ar = (t // GRID_W)[:, None] * inv[None, :]
    ac = (t % GRID_W)[:, None] * inv[None, :]
    cos_h = np.concatenate([np.cos(ar), np.cos(ar), np.cos(ac), np.cos(ac)], axis=-1)
    sin_h = np.concatenate([-np.sin(ar), np.sin(ar), -np.sin(ac), np.sin(ac)], axis=-1)
    as_table = lambda a: jnp.asarray(np.tile(a, (1, GLA_HEADS)).astype(np.float32))
    return as_table(cos_h), as_table(sin_h)


def _gla_tile(qk, v, g, s, reverse):
    hk = GLA_HEADS * GLA_DK
    hv = GLA_HEADS * GLA_DV
    c = GLA_CHUNK
    t = qk.shape[0]
    n = t // c
    last_row, mid_row = (0, c // 2) if reverse else (c - 1, c // 2 - 1)
    ii = lax.broadcasted_iota(jnp.int32, (t, t), 0)
    jj = lax.broadcasted_iota(jnp.int32, (t, t), 1)
    ordered = (jj >= ii) if reverse else (jj <= ii)
    tri = jnp.where(ordered & (ii // c == jj // c), 1.0, 0.0).astype(BF16)
    g_hi = g.astype(BF16)
    rem = g - g_hi.astype(F32)
    g_mid = rem.astype(BF16)
    g_lo = (rem - g_mid.astype(F32)).astype(BF16)
    bc = _dot(tri, g_hi) + (_dot(tri, g_mid) + _dot(tri, g_lo))
    spread = lambda row: jnp.concatenate(
        [jnp.broadcast_to(bc[i * c + row:i * c + row + 1, :], (c, hk)) for i in range(n)], axis=0)
    b_mid = spread(mid_row)
    b_last = spread(last_row)
    qt = qk[:, 0:hk] * jnp.exp(bc - b_mid)
    kt = qk[:, hk:2 * hk] * jnp.exp(b_mid - bc)
    qe = (qt * jnp.exp(b_mid)).astype(BF16)
    ke = kt * jnp.exp(b_last - b_mid)
    ktb = kt.astype(BF16)
    lane = lax.broadcasted_iota(jnp.int32, (1, hk), 1)
    ci = lax.broadcasted_iota(jnp.int32, (c, c), 0)
    cj = lax.broadcasted_iota(jnp.int32, (c, c), 1)
    causal = (cj >= ci) if reverse else (cj <= ci)
    blockdiag = (lax.broadcasted_iota(jnp.int32, (hk, hv), 0) // GLA_DK
                 == lax.broadcasted_iota(jnp.int32, (hk, hv), 1) // GLA_DV)
    intra, upd, decay = [], [], []
    for i in range(n):
        rows = slice(i * c, (i + 1) * c)
        qs = jnp.concatenate(
            [jnp.where((lane >= h * GLA_DK) & (lane < (h + 1) * GLA_DK), qt[rows], 0.0) for h in range(GLA_HEADS)],
            axis=0).astype(BF16)
        att = _dot_nt(qs, ktb[rows])
        intra.append(jnp.concatenate(
            [_dot(jnp.where(causal, att[h * c:(h + 1) * c], 0.0).astype(BF16),
                  v[rows, h * GLA_DV:(h + 1) * GLA_DV]) for h in range(GLA_HEADS)], axis=-1))
        upd.append(jnp.where(blockdiag, _dot(ke[rows].T.astype(BF16), v[rows]), 0.0))
        decay.append(jnp.exp(jnp.sum(g[rows].T, axis=1, keepdims=True)))
    outs = [None] * n
    for i in (reversed(range(n)) if reverse else range(n)):
        rows = slice(i * c, (i + 1) * c)
        outs[i] = _dot(qe[rows], s.astype(BF16)) + intra[i]
        s = decay[i] * s + upd[i]
    return jnp.concatenate(outs, axis=0), s


def _gla_body(qkf_ref, qkb_ref, vf_ref, vb_ref, gf_ref, gb_ref, s0f_ref, s0b_ref,
              of_ref, ob_ref, sff_ref, sbf_ref, sf_ref, sb_ref, *, tile):
    n = pl.program_id(1)
    hk = GLA_HEADS * GLA_DK
    hv = GLA_HEADS * GLA_DV

    @pl.when(n == 0)
    def _():
        sf_ref[...] = jnp.zeros((hk, hv), F32)
        sb_ref[...] = jnp.zeros((hk, hv), F32)
        for h in range(GLA_HEADS):
            sf_ref[h * GLA_DK:(h + 1) * GLA_DK, h * GLA_DV:(h + 1) * GLA_DV] = s0f_ref[0, h]
            sb_ref[h * GLA_DK:(h + 1) * GLA_DK, h * GLA_DV:(h + 1) * GLA_DV] = s0b_ref[0, h]

    of_ref[0], sf_ref[...] = _gla_tile(qkf_ref[0], vf_ref[0], gf_ref[0], sf_ref[...], False)
    ob_ref[0], sb_ref[...] = _gla_tile(qkb_ref[0], vb_ref[0], gb_ref[0], sb_ref[...], True)

    @pl.when(n == pl.num_programs(1) - 1)
    def _():
        for h in range(GLA_HEADS):
            sff_ref[0, h] = sf_ref[h * GLA_DK:(h + 1) * GLA_DK, h * GLA_DV:(h + 1) * GLA_DV]
            sbf_ref[0, h] = sb_ref[h * GLA_DK:(h + 1) * GLA_DK, h * GLA_DV:(h + 1) * GLA_DV]


def _gla(qk, v, gates, s0f, s0b, tile):
    b, l, _ = qk.shape
    hk = GLA_HEADS * GLA_DK
    hv = GLA_HEADS * GLA_DV
    nt = l // tile
    fwd = lambda w, col: pl.BlockSpec((1, tile, w), lambda i, n: (i, n, col))
    bwd = lambda w, col: pl.BlockSpec((1, tile, w), lambda i, n: (i, nt - 1 - n, col))
    st = pl.BlockSpec((1, GLA_HEADS, GLA_DK, GLA_DV), lambda i, n: (i, 0, 0, 0))
    return pl.pallas_call(
        functools.partial(_gla_body, tile=tile),
        grid=(b, nt),
        in_specs=[fwd(2 * hk, 0), bwd(2 * hk, 0), fwd(hv, 0), bwd(hv, 0), fwd(hk, 0), bwd(hk, 1), st, st],
        out_specs=[fwd(hv, 0), bwd(hv, 0), st, st],
        out_shape=[jax.ShapeDtypeStruct((b, l, hv), F32), jax.ShapeDtypeStruct((b, l, hv), F32),
                   jax.ShapeDtypeStruct((b, GLA_HEADS, GLA_DK, GLA_DV), F32),
                   jax.ShapeDtypeStruct((b, GLA_HEADS, GLA_DK, GLA_DV), F32)],
        scratch_shapes=[pltpu.VMEM((hk, hv), F32), pltpu.VMEM((hk, hv), F32)],
        compiler_params=_cparams(("parallel", "arbitrary")),
        name="gla_scan",
    )(qk, qk, v, v, gates, gates, s0f, s0b)


def _odd_mid_body(cur_ref, prev_ref, next_ref, of_ref, ob_ref, r_ref, hg_ref, pw_ref, ps_ref,
                  pool_ref, d_ref, buf_ref, *, tile, seq):
    j = pl.program_id(1)
    last = pl.num_programs(1) - 1
    hal = POOL_HALO
    buf_ref[0:hal, :] = jnp.where(j == 0, 0.0, prev_ref[0])
    buf_ref[hal:hal + tile, :] = cur_ref[0]
    buf_ref[hal + tile:hal + tile + hal, :] = jnp.where(j == last, 0.0, next_ref[0])
    t = j * tile + lax.broadcasted_iota(jnp.int32, (tile, 1), 0)
    for gi, win in enumerate(POOL_WINDOWS):
        cols = slice(gi * POOL_GROUP, (gi + 1) * POOL_GROUP)
        acc = jnp.zeros((tile, POOL_GROUP), F32)
        for off in range(-(win // 2), win - win // 2):
            acc = acc + buf_ref[hal + off:hal + off + tile, cols]
        cnt = jnp.minimum(t + (win - win // 2), seq) - jnp.maximum(t - win // 2, 0)
        diff = acc / cnt.astype(F32) - cur_ref[0, :, cols]
        pool_ref[0, :, cols] = _dot(diff.astype(BF16), pw_ref[gi]) * ps_ref[:, cols]
    for h in range(GLA_HEADS):
        cols = slice(h * GLA_DV, (h + 1) * GLA_DV)
        o = of_ref[0, :, cols] + ob_ref[0, :, cols]
        d_ref[0, :, cols] = _rms(o, hg_ref[:, cols]) * _silu(r_ref[0, :, cols])


def _odd_out_body(*refs, tile, seq):
    n_branch = 9
    buf_ref, pool_s, d_s = refs[-3:]
    _odd_mid_body(*refs[:n_branch], pool_s, d_s, buf_ref, tile=tile, seq=seq)
    _out_tail(pool_s[0], d_s[0], *refs[n_branch:-3])


def _odd_out(pool_u, o_f, o_b, r, head_g, pool_w_bf, pool_scale, *common):
    b, l, c = pool_u.shape
    tile = common[-1]
    hal = POOL_HALO
    per = tile // hal
    nh = l // hal
    blk = pl.BlockSpec((1, tile, c), lambda i, j: (i, j, 0))
    vec = pl.BlockSpec((1, c), lambda i, j: (0, 0))
    specs = [blk,
             pl.BlockSpec((1, hal, c), lambda i, j: (i, jnp.maximum(j * per - 1, 0), 0)),
             pl.BlockSpec((1, hal, c), lambda i, j: (i, jnp.minimum((j + 1) * per, nh - 1), 0)),
             blk, blk, blk, vec,
             pl.BlockSpec((len(POOL_WINDOWS), POOL_GROUP, POOL_GROUP), lambda i, j: (0, 0, 0)),
             vec]
    scratch = [pltpu.VMEM((tile + 2 * hal, c), F32), pltpu.VMEM((1, tile, c), F32), pltpu.VMEM((1, tile, c), F32)]
    args = [pool_u, pool_u, pool_u, o_f, o_b, r, head_g.reshape(1, c), pool_w_bf, pool_scale.reshape(1, c)]
    return _out_call(functools.partial(_odd_out_body, tile=tile, seq=l), args, specs, scratch, "odd_out", *common)


def _threshold_body(a_ref, thr_ref, need_ref, *, cap):
    bits = lax.bitcast_convert_type(a_ref[0], jnp.int32)
    rows = bits.shape[0]
    count_ge = lambda v: jnp.sum(jnp.where(bits >= v, 1.0, 0.0), axis=-1, keepdims=True)

    def step(_, carry):
        lo, hi = carry
        mid = lo + ((hi - lo + 1) >> 1)
        ok = count_ge(mid) >= cap
        return jnp.where(ok, mid, lo), jnp.where(ok, hi, mid - 1)

    lo, _ = lax.fori_loop(0, 31, step, (jnp.zeros((rows, 1), jnp.int32),
                                        jnp.full((rows, 1), F32_INF_BITS, jnp.int32)))
    above = jnp.sum(jnp.where(bits > lo, 1.0, 0.0), axis=-1, keepdims=True)
    width = thr_ref.shape[-1]
    thr_ref[0] = jnp.broadcast_to(lax.bitcast_convert_type(lo, F32), (rows, width))
    need_ref[0] = jnp.broadcast_to(cap - above.astype(jnp.int32), (rows, width))


def _route_threshold(aff_t, cap):
    b, e, n = aff_t.shape
    out = pl.BlockSpec((1, e, SC_LANES), lambda i: (i, 0, 0))
    return pl.pallas_call(
        functools.partial(_threshold_body, cap=cap),
        grid=(b,),
        in_specs=[pl.BlockSpec((1, e, n), lambda i: (i, 0, 0))],
        out_specs=[out, out],
        out_shape=[jax.ShapeDtypeStruct((b, e, SC_LANES), F32), jax.ShapeDtypeStruct((b, e, SC_LANES), jnp.int32)],
        compiler_params=_cparams(("parallel",)),
        name="route_threshold",
    )(aff_t)


def _route_compact(aff, thr, need, cap):
    r, n = aff.shape
    lanes = SC_LANES
    assert r == SC_CORES * SC_SUBCORES and n % lanes == 0
    mesh = plsc.VectorSubcoreMesh(core_axis_name="core", subcore_axis_name="subcore",
                                  num_cores=SC_CORES, num_subcores=SC_SUBCORES)

    @pl.kernel(out_type=[jax.ShapeDtypeStruct((r, cap), jnp.int32), jax.ShapeDtypeStruct((r, cap), F32)],
               mesh=mesh,
               scratch_types=[pltpu.VMEM((n,), F32), pltpu.VMEM((lanes,), F32), pltpu.VMEM((lanes,), jnp.int32),
                              pltpu.VMEM((cap,), jnp.int32), pltpu.VMEM((cap,), F32)],
               compiler_params=pltpu.CompilerParams(needs_layout_passes=False),
               name="route_compact")
    def compact(aff_hbm, thr_hbm, need_hbm, idx_hbm, val_hbm, row_v, thr_v, need_v, idx_v, val_v):
        w = lax.axis_index("subcore") * SC_CORES + lax.axis_index("core")
        pltpu.sync_copy(aff_hbm.at[w], row_v)
        pltpu.sync_copy(thr_hbm.at[w], thr_v)
        pltpu.sync_copy(need_hbm.at[w], need_v)
        thr = thr_v[...]
        need = need_v[...]
        lane = lax.iota(jnp.int32, lanes)
        ones = jnp.ones((lanes,), jnp.int32)

        def body(i, carry):
            n_out, n_eq = carry
            x = row_v[pl.ds(i * lanes, lanes)]
            eq = x == thr
            take = (x > thr) | (eq & (n_eq + plsc.cumsum(ones, mask=eq) <= need))
            pos = n_out + plsc.cumsum(ones, mask=take) - 1
            take = take & (pos < cap)
            plsc.store_scatter(idx_v, [pos], lane + i * lanes, mask=take)
            plsc.store_scatter(val_v, [pos], x, mask=take)
            return (n_out + plsc.all_reduce_population_count(take),
                    n_eq + plsc.all_reduce_population_count(eq))

        zero = jnp.zeros((lanes,), jnp.int32)
        lax.fori_loop(0, n // lanes, body, (zero, zero))
        pltpu.sync_copy(idx_v, idx_hbm.at[w])
        pltpu.sync_copy(val_v, val_hbm.at[w])

    return compact(aff, thr, need)


def _route(aff_t, tok_base, row_base):
    b, e, n = aff_t.shape
    cap = EC_CAPACITY_FACTOR * n // N_EXPERTS
    thr, need = _route_threshold(aff_t, cap)
    idx, vals = _route_compact(aff_t.reshape(b * e, n), thr.reshape(b * e, -1), need.reshape(b * e, -1), cap)
    idx = idx.reshape(b, e, cap)
    vals = vals.reshape(b, e, cap)
    bi = jnp.arange(b, dtype=idx.dtype)[:, None, None]
    per_expert = lambda a: jnp.swapaxes(a, 0, 1).reshape(e, b * cap)
    rows0 = idx + row_base + 2 * bi * n
    return (per_expert(vals), per_expert(idx + tok_base + bi * n), per_expert(rows0), per_expert(rows0 + n),
            per_expert(idx))


def _combine_first_start(p0, cap):
    return jnp.minimum((p0 // BF16_ROWS) * BF16_ROWS, cap - COMBINE_FIRST)


def _combine_body(offs_ref, spill_ref, tok_ref, y_ref, x_ref, gate_ref, g_ref, o_ref, f_s, *, n_tok, cap):
    bi = pl.program_id(0)
    j = pl.program_id(1)
    tt = COMBINE_TILE
    wf = COMBINE_FIRST
    wn = COMBINE_WINDOW
    ntiles = n_tok // tt
    n_exp = y_ref.shape[0]
    group = wn // wf
    sub = lax.broadcasted_iota(jnp.int32, (tt, 1), 0)
    lane = lax.broadcasted_iota(jnp.int32, (1, wn), 1)

    def slot_range(e):
        base = (bi * n_exp + e) * (ntiles + 1) + j
        return offs_ref[base], offs_ref[base + 1]

    acc = jnp.zeros(f_s.shape, F32)
    for g in range(n_exp // group):
        ys = []
        toks = jnp.full((1, wn), -1, jnp.int32)
        for k in range(group):
            e = g * group + k
            start = pl.multiple_of(_combine_first_start(slot_range(e)[0], cap), BF16_ROWS)
            ys.append(y_ref[e, pl.ds(start, wf), :])
            cs = pl.multiple_of(jnp.minimum((start // LANES) * LANES, cap - wn), LANES)
            rolled = pltpu.roll(tok_ref[e, :, pl.ds(cs, wn)], (k * wf + wn - (start - cs)) % wn, 1)
            toks = jnp.where((lane >= k * wf) & (lane < (k + 1) * wf), rolled, toks)
        hit = (toks - j * tt) == sub
        acc = acc + _dot(jnp.where(hit, 1.0, 0.0).astype(BF16), jnp.concatenate(ys, axis=0))
    f_s[...] = acc

    def more_windows(e, carry):
        p0, p1 = slot_range(e)
        lo = _combine_first_start(p0, cap) + wf
        first = (lo // LANES) * LANES

        def extra(w, carry):
            cs = pl.multiple_of(jnp.minimum(first + w * wn, cap - wn), LANES)
            tok = tok_ref[e, :, pl.ds(cs, wn)] - j * tt
            hit = (tok == sub) & (cs + lane >= jnp.maximum(lo, first + w * wn))
            f_s[...] += _dot(jnp.where(hit, 1.0, 0.0).astype(BF16), y_ref[e, pl.ds(cs, wn), :])
            return carry

        lax.fori_loop(0, (jnp.maximum(p1 - first, 0) + wn - 1) // wn * (p1 > lo).astype(jnp.int32), extra, 0)
        return carry

    @pl.when(spill_ref[bi * ntiles + j] != 0)
    def _():
        lax.fori_loop(0, n_exp, more_windows, 0)

    o_ref[0] = x_ref[0] + gate_ref[0] * _rms(f_s[...], g_ref[...])


def _combine(y, tok, local, seg0, x, gate, g):
    e, _, d = y.shape
    _, b, cap = local.shape
    n_tok = x.shape[1]
    tt = COMBINE_TILE
    wf = COMBINE_FIRST
    ntiles = n_tok // tt
    assert seg0 % cap == 0 and n_tok % tt == 0 and cap % COMBINE_WINDOW == 0 and e % (COMBINE_WINDOW // wf) == 0
    seg = seg0 // cap
    bounds = jnp.arange(ntiles + 1, dtype=jnp.int32) * tt
    offs = jnp.sum((local[..., None] < bounds).astype(jnp.int32), axis=2)
    starts = _combine_first_start(offs[..., :-1], cap)
    spill = jnp.any(offs[..., 1:] > starts + wf, axis=0).astype(jnp.int32).reshape(-1)
    row = pl.BlockSpec((1, tt, d), lambda i, j, offs, spill: (i, j, 0))
    grid_spec = pltpu.PrefetchScalarGridSpec(
        num_scalar_prefetch=2,
        grid=(b, ntiles),
        in_specs=[pl.BlockSpec((e, 1, cap), lambda i, j, offs, spill: (0, 0, seg + i)),
                  pl.BlockSpec((e, cap, d), lambda i, j, offs, spill: (0, seg + i, 0),
                               pipeline_mode=pl.Buffered(1)),
                  row,
                  pl.BlockSpec((1, 1, d), lambda i, j, offs, spill: (i, 0, 0)),
                  pl.BlockSpec((1, d), lambda i, j, offs, spill: (0, 0))],
        out_specs=row,
        scratch_shapes=[pltpu.VMEM((tt, d), F32)],
    )
    return pl.pallas_call(
        functools.partial(_combine_body, n_tok=n_tok, cap=cap),
        grid_spec=grid_spec,
        out_shape=jax.ShapeDtypeStruct((b, n_tok, d), F32),
        compiler_params=_cparams(("parallel", "arbitrary")),
        name="combine",
    )(jnp.swapaxes(offs, 0, 1).reshape(-1), spill, tok, y, x, gate.reshape(b, 1, d), g.reshape(1, d))


def _gather_rows(src, idx):
    window = SC_GATHER_WINDOW
    n = idx.shape[0]
    width = src.shape[1]
    assert 2 * window * width * 4 <= SC_TILE_VMEM_BUDGET, width
    assert n % (window * SC_CORES * SC_SUBCORES) == 0, n
    mesh = plsc.VectorSubcoreMesh(core_axis_name="core", subcore_axis_name="subcore",
                                  num_cores=SC_CORES, num_subcores=SC_SUBCORES)

    @pl.kernel(out_type=jax.ShapeDtypeStruct((n, width), src.dtype), mesh=mesh, scratch_types=[],
               name="gather_rows")
    def gather(src_hbm, idx_hbm, out_hbm):
        def body(idx_vmem, out_vmem):
            pltpu.sync_copy(src_hbm.at[idx_vmem.at[0]], out_vmem)

        pltpu.emit_pipeline(
            body,
            grid=(n // window,),
            in_specs=[pl.BlockSpec((1, window), lambda i: (0, i))],
            out_specs=[pl.BlockSpec((window, width), lambda i: (i, 0))],
            core_axis_name=("core", "subcore"),
            dimension_semantics=(pltpu.PARALLEL,),
        )(idx_hbm, out_hbm)

    return gather(src, idx.reshape(1, n))


def _moe(parts, g, w_gate, w_up, w_down, layer):
    quarter = parts[0][1].shape[-1]
    d = 4 * quarter
    sizes = [p[1].shape[0] * p[1].shape[2] for p in parts]
    bases = [sum(sizes[:i]) for i in range(len(parts))]
    routed = [_route(p[0], base, 2 * base) for p, base in zip(parts, bases)]
    src = jnp.concatenate([p[1].reshape(-1, quarter) for p in parts], axis=0)
    vals, flat, rows0, rows1, tok = (jnp.concatenate([r[i] for r in routed], axis=1) for i in range(5))
    e, m = flat.shape
    rows = jnp.stack([rows0, rows1], axis=1).reshape(-1)
    unit = SC_GATHER_WINDOW * SC_CORES * SC_SUBCORES
    fill = jnp.arange(-rows.shape[0] % unit, dtype=rows.dtype)
    xg = _gather_rows(src, jnp.concatenate([rows, fill]))
    chunk = next(c for c in (512, 528, 384, 320, 256, 128) if m % c == 0)
    y = _expert_ffn(xg, vals[:, None, :], w_gate, w_up, w_down, layer, chunk)
    outs = []
    seg0 = 0
    for (aff_t, _, x, gate), r, base in zip(parts, routed, bases):
        b, _, n = aff_t.shape
        cap = r[0].shape[1] // b
        if n % COMBINE_TILE == 0 and cap % COMBINE_WINDOW == 0 and seg0 % cap == 0:
            outs.append(_combine(y, tok[:, None, :], r[4].reshape(e, b, cap), seg0, x, gate, g))
        else:
            ids = r[1] - base
            part = y[:, seg0:seg0 + b * cap].astype(F32)
            f = jnp.zeros((b * n, d), F32).at[ids.reshape(-1)].add(part.reshape(-1, d))
            outs.append(_gated_residual(x, f.reshape(b, n, d), gate, g, n))
        seg0 += b * cap
    return outs


def kernel(x, c, ctx, c_ctx, w_mod, b_mod, norm_g, w_in_even, w_out_even, conv_w, conv_b, conv_ln_g,
           conv_ln_b, na_rpb, w_in_odd, w_out_odd, pool_w, pool_scale, gla_gate_w, gla_gate_b, gla_head_g,
           router_w, expert_w_gate, expert_w_up, expert_w_down):
    b, l, d = x.shape
    n_ctx = ctx.shape[1]
    tm = ROW_TILE

    mod_rows = jnp.concatenate([c, c_ctx[None], jnp.zeros((8 - b - 1, d), F32)], axis=0)

    mod_all = _modulation(mod_rows, w_mod, b_mod)

    def modulation(i):
        mm = mod_all[i]
        m = mm[:b].reshape(b, 6, d)
        mc = jnp.broadcast_to(mm[b].reshape(1, 6, d), (b, 6, d))
        return m, mc

    m, mc = modulation(0)
    g = norm_g[0]
    w_in = w_in_even[0].astype(BF16)
    w_out = w_out_even[0].astype(BF16)
    glu, qkv = _even_in(x, g[0], m[:, 0], m[:, 1], w_in, tm)
    glu_c, qkv_c = _even_in(ctx, g[0], mc[:, 0], mc[:, 1], w_in, n_ctx)
    a_ctx = _conv_branch(glu_c, conv_w[0], conv_b[0], conv_ln_g[0], conv_ln_b[0], n_ctx)
    na = _neighbourhood_attention(qkv, qkv_c, na_rpb[0])
    att_c = _context_attention(qkv_c)
    x, h2, aff = _even_out(glu, conv_w[0], conv_b[0], conv_ln_g[0], conv_ln_b[0], na,
                           x, w_out, g[1], m[:, 2], g[2], m[:, 3], m[:, 4], router_w[0], tm)
    ctx, h2c, aff_c = _out_proj(a_ctx, att_c, ctx, w_out, g[1], mc[:, 2], g[2], mc[:, 3], mc[:, 4],
                                router_w[0], n_ctx)
    x, ctx = _moe([(aff, h2, x, m[:, 5]), (aff_c, h2c, ctx, mc[:, 5])], g[3],
                  expert_w_gate, expert_w_up, expert_w_down, 0)

    m, mc = modulation(1)
    g = norm_g[1]
    w_in = w_in_odd[0].astype(BF16)
    w_out = w_out_odd[0].astype(BF16)
    cos_t, sin_t = _rope_tables(l)
    ones_t = jnp.ones((n_ctx, GLA_HEADS * GLA_DK), F32)
    _, qk_c, v_c, _, gate_c = _odd_in(ctx, g[0], mc[:, 0], mc[:, 1], w_in, ones_t, jnp.zeros_like(ones_t),
                                      gla_gate_w[0], gla_gate_b[0], n_ctx)
    s_zero = jnp.zeros((b, GLA_HEADS, GLA_DK, GLA_DV), F32)
    _, _, s_f, s_b = _gla(qk_c, v_c, gate_c, s_zero, s_zero, n_ctx)
    pool_u, qk, v, r, gate = _odd_in(x, g[0], m[:, 0], m[:, 1], w_in, cos_t, sin_t,
                                     gla_gate_w[0], gla_gate_b[0], tm)
    o_f, o_b, _, _ = _gla(qk, v, gate, s_f, s_b, SEQ_TILE)
    x, h2, aff = _odd_out(pool_u, o_f, o_b, r, gla_head_g[0], pool_w[0].astype(BF16), pool_scale[0],
                          x, w_out, g[1], m[:, 2], g[2], m[:, 3], m[:, 4], router_w[1], tm)
    (x,) = _moe([(aff, h2, x, m[:, 5])], g[3], expert_w_gate, expert_w_up, expert_w_down, 1)
    return x
```

```python
import functools

import jax
import jax.numpy as jnp
import numpy as np
from jax import lax
from jax.experimental import pallas as pl
from jax.experimental.pallas import tpu as pltpu
from jax.experimental.pallas import tpu_sc as plsc

F32 = jnp.float32
BF16 = jnp.bfloat16
HIGHEST = lax.Precision.HIGHEST

D_MODEL = 1024
GRID_W = 64
EPS = 1e-6
CONV_CH = 512
CONV_WIDTH = 31
CONV_HALO = 16
CONV_CHUNK = 32
NA_HEADS = 8
NA_HEAD_DIM = 64
NA_KR = 8
NA_KC = 16
NA_ROWS_PER_BLOCK = 4
NA_BLOCKS_PER_STEP = 4
NA_WIN_ROWS = 12
POOL_CH = 512
POOL_WINDOWS = (2, 4, 8, 16)
POOL_GROUP = 128
POOL_HALO = 8
GLA_HEADS = 4
GLA_DK = 64
GLA_DV = 128
GLA_RANK = 16
GLA_TAU = 16.0
GLA_CHUNK = 64
ROPE_BASE = 10000.0
N_EXPERTS = 16
EC_CAPACITY_FACTOR = 2
ROW_TILE = 512
SEQ_TILE = 256
MOD_COLS = 1536
EXPERT_FF_TILE = 256
LANES = 128
SUBLANES = 8
NEG_BIG = -1e30
VMEM_LIMIT = 56 * 1024 * 1024
SC_CORES = 2
SC_SUBCORES = 16
SC_LANES = 16
F32_INF_BITS = 0x7F800000
SC_TILE_VMEM_BUDGET = 400 * 1024
SC_GATHER_WINDOW = 128
COMBINE_TILE = 256
COMBINE_WINDOW = 256
COMBINE_FIRST = 64
BF16_ROWS = 16


def _cparams(sem):
    return pltpu.CompilerParams(dimension_semantics=sem, vmem_limit_bytes=VMEM_LIMIT)


def _rms(x, g):
    return x * lax.rsqrt(jnp.mean(x * x, axis=-1, keepdims=True) + EPS) * g


def _sigmoid(x):
    return 1.0 / (1.0 + jnp.exp(-x))


def _silu(x):
    return x * _sigmoid(x)


def _dot(a, b):
    return jnp.dot(a, b, preferred_element_type=F32)


def _pack_bf16_pairs(h):
    half = h.shape[-1] // 2
    bits = lax.bitcast_convert_type(h.astype(BF16).astype(F32), jnp.uint32)
    packed = (bits[:, half:] & jnp.uint32(0xFFFF0000)) | (bits[:, :half] >> 16)
    return lax.bitcast_convert_type(packed, jnp.int32)


def _unpack_bf16_pairs(p):
    bits = lax.bitcast_convert_type(p, jnp.uint32)
    lo = lax.bitcast_convert_type(bits << 16, F32).astype(BF16)
    hi = lax.bitcast_convert_type(bits & jnp.uint32(0xFFFF0000), F32).astype(BF16)
    return lo, hi


def _dot_nt(a, b):
    return lax.dot_general(a, b, (((1,), (1,)), ((), ())), preferred_element_type=F32)


def _mod_body(c_ref, w_ref, b_ref, o_ref):
    o_ref[0] = jnp.dot(_silu(c_ref[...]), w_ref[0], precision=HIGHEST,
                       preferred_element_type=F32) + b_ref[0]


def _modulation(rows, w, b):
    depth, _, n = w.shape
    tn = MOD_COLS
    return pl.pallas_call(
        _mod_body,
        grid=(depth, n // tn),
        in_specs=[pl.BlockSpec((8, D_MODEL), lambda i, j: (0, 0)),
                  pl.BlockSpec((1, D_MODEL, tn), lambda i, j: (i, 0, j)),
                  pl.BlockSpec((1, 1, tn), lambda i, j: (i, 0, j))],
        out_specs=pl.BlockSpec((1, 8, tn), lambda i, j: (i, 0, j)),
        out_shape=jax.ShapeDtypeStruct((depth, 8, n), F32),
        compiler_params=_cparams(("parallel", "parallel")),
        name="modulation",
    )(rows, w, b.reshape(depth, 1, n))


def _even_in_body(x_ref, g_ref, sh_ref, sc_ref, w_ref, glu_ref, qkv_ref):
    h = (_rms(x_ref[0], g_ref[...]) * (1.0 + sc_ref[0]) + sh_ref[0]).astype(BF16)
    c = CONV_CH
    glu_ref[0] = _dot(h, w_ref[:, 0:c]) * _sigmoid(_dot(h, w_ref[:, c:2 * c]))
    hd = NA_HEADS * NA_HEAD_DIM
    q0 = 2 * c
    qkv_ref[0, :, 0:hd] = (_dot(h, w_ref[:, q0:q0 + hd]) * (NA_HEAD_DIM ** -0.5)).astype(BF16)
    qkv_ref[0, :, hd:3 * hd] = _dot(h, w_ref[:, q0 + hd:q0 + 3 * hd]).astype(BF16)


def _even_in(x, g, shift, scale, w_bf, tm):
    b, l, d = x.shape
    n = w_bf.shape[1]
    hd3 = 3 * NA_HEADS * NA_HEAD_DIM
    vec = pl.BlockSpec((1, 1, d), lambda i, j: (i, 0, 0))
    return pl.pallas_call(
        _even_in_body,
        grid=(b, l // tm),
        in_specs=[pl.BlockSpec((1, tm, d), lambda i, j: (i, j, 0)),
                  pl.BlockSpec((1, d), lambda i, j: (0, 0)),
                  vec, vec,
                  pl.BlockSpec((d, n), lambda i, j: (0, 0))],
        out_specs=[pl.BlockSpec((1, tm, CONV_CH), lambda i, j: (i, j, 0)),
                   pl.BlockSpec((1, tm, hd3), lambda i, j: (i, j, 0))],
        out_shape=[jax.ShapeDtypeStruct((b, l, CONV_CH), F32),
                   jax.ShapeDtypeStruct((b, l, hd3), BF16)],
        compiler_params=_cparams(("parallel", "parallel")),
        name="even_in",
    )(x, g.reshape(1, d), shift.reshape(b, 1, d), scale.reshape(b, 1, d), w_bf)


def _conv_body(cur_ref, prev_ref, next_ref, w_ref, b_ref, lg_ref, lb_ref, o_ref, buf_ref, sh_ref, *, tile, chunk):
    j = pl.program_id(1)
    last = pl.num_programs(1) - 1
    hal = CONV_HALO
    buf_ref[0:hal, :] = jnp.where(j == 0, 0.0, prev_ref[0])
    buf_ref[hal:hal + tile, :] = cur_ref[0]
    buf_ref[hal + tile:hal + tile + hal, :] = jnp.where(j == last, 0.0, next_ref[0])
    span = sh_ref.shape[1]
    for s in range(SUBLANES):
        sh_ref[s] = buf_ref[s:s + span, :]
    first = hal - CONV_WIDTH // 2
    reps = chunk // SUBLANES

    last_a = (first + CONV_WIDTH - 1) // SUBLANES

    def rows(c, carry):
        r0 = pl.multiple_of(c * chunk, chunk)
        acc = jnp.zeros((chunk, CONV_CH), F32)
        for s in range(SUBLANES):
            rows_s = sh_ref[s, pl.ds(r0, chunk + last_a * SUBLANES), :]
            for a in range(last_a + 1):
                k = a * SUBLANES + s - first
                if 0 <= k < CONV_WIDTH:
                    wk = jnp.concatenate([w_ref[k]] * reps, axis=0)
                    acc = acc + rows_s[a * SUBLANES:a * SUBLANES + chunk, :] * wk
        o_ref[0, pl.ds(r0, chunk), :] = acc
        return carry

    lax.fori_loop(0, tile // chunk, rows, 0)
    y = o_ref[0] + b_ref[...]
    mu = jnp.mean(y, axis=-1, keepdims=True)
    yc = y - mu
    var = jnp.mean(yc * yc, axis=-1, keepdims=True)
    o_ref[0] = _silu(yc * lax.rsqrt(var + EPS) * lg_ref[...] + lb_ref[...])


def _conv_operands(glu, conv_w, conv_b, ln_g, ln_b, tile):
    _, l, c = glu.shape
    hal = CONV_HALO
    per = tile // hal
    nh = l // hal
    vec = pl.BlockSpec((1, c), lambda i, j: (0, 0))
    specs = [pl.BlockSpec((1, tile, c), lambda i, j: (i, j, 0)),
             pl.BlockSpec((1, hal, c), lambda i, j: (i, jnp.maximum(j * per - 1, 0), 0)),
             pl.BlockSpec((1, hal, c), lambda i, j: (i, jnp.minimum((j + 1) * per, nh - 1), 0)),
             pl.BlockSpec((CONV_WIDTH, SUBLANES, c), lambda i, j: (0, 0, 0)),
             vec, vec, vec]
    args = [glu, glu, glu, jnp.broadcast_to(conv_w[:, None, :], (CONV_WIDTH, SUBLANES, c)),
            conv_b.reshape(1, c), ln_g.reshape(1, c), ln_b.reshape(1, c)]
    scratch = [pltpu.VMEM((tile + 2 * hal, c), F32),
               pltpu.VMEM((SUBLANES, tile + 2 * hal - SUBLANES, c), F32)]
    return args, specs, scratch


def _conv_branch(glu, conv_w, conv_b, ln_g, ln_b, tile):
    b, l, c = glu.shape
    args, specs, scratch = _conv_operands(glu, conv_w, conv_b, ln_g, ln_b, tile)
    return pl.pallas_call(
        functools.partial(_conv_body, tile=tile, chunk=CONV_CHUNK),
        grid=(b, l // tile),
        in_specs=specs,
        out_specs=pl.BlockSpec((1, tile, c), lambda i, j: (i, j, 0)),
        out_shape=jax.ShapeDtypeStruct((b, l, c), F32),
        scratch_shapes=scratch,
        compiler_params=_cparams(("parallel", "parallel")),
        name="conv_branch",
    )(*args)


def _na_window_start(j, rows):
    rb = NA_ROWS_PER_BLOCK
    return jnp.clip(j * rb - NA_KR // 2, 0, rows - NA_WIN_ROWS)


def _na_body(q_ref, k_ref, v_ref, kc_ref, vc_ref, *rest, rows):
    tab_refs, o_ref = rest[:-1], rest[-1]
    j = pl.program_id(2)
    nkeys = NA_WIN_ROWS * GRID_W
    tq = NA_ROWS_PER_BLOCK * GRID_W
    kc = kc_ref[0]
    vc = vc_ref[0]
    lane = lax.broadcasted_iota(jnp.int32, (1, LANES), 1)
    for sb, tab_ref in enumerate(tab_refs):
        start = pl.multiple_of(_na_window_start(j * len(tab_refs) + sb, rows) * GRID_W, GRID_W)
        q = q_ref[0, sb * tq:(sb + 1) * tq, :]
        kw = k_ref[0, pl.ds(start, nkeys), :]
        vw = v_ref[0, pl.ds(start, nkeys), :]
        out = jnp.zeros(q.shape, F32)
        for hh in range(LANES // NA_HEAD_DIM):
            in_head = (lane >= hh * NA_HEAD_DIM) & (lane < (hh + 1) * NA_HEAD_DIM)
            qh = jnp.where(in_head, q, jnp.zeros_like(q))
            s = _dot_nt(qh, kw) + tab_ref[0, hh]
            sc = _dot_nt(qh, kc)
            m = jnp.maximum(jnp.max(s, axis=-1, keepdims=True), jnp.max(sc, axis=-1, keepdims=True))
            p = jnp.exp(s - m)
            pc = jnp.exp(sc - m)
            denom = jnp.sum(p, axis=-1, keepdims=True) + jnp.sum(pc, axis=-1, keepdims=True)
            o = (_dot(p.astype(BF16), vw) + _dot(pc.astype(BF16), vc)) / denom
            out = jnp.where(in_head, o, out)
        o_ref[0, sb * tq:(sb + 1) * tq, :] = out


def _na_tables(rpb, rows):
    rb = NA_ROWS_PER_BLOCK
    nblk = rows // rb
    wr = NA_WIN_ROWS
    qc = jnp.arange(GRID_W)
    cs = jnp.clip(qc - NA_KC // 2, 0, GRID_W - NA_KC)
    col_ok = (qc[None, :] >= cs[:, None]) & (qc[None, :] < cs[:, None] + NA_KC)
    col_off = qc[None, :] - qc[:, None] + NA_KC - 1
    onehot = (col_off[:, :, None] == jnp.arange(2 * NA_KC - 1)[None, None, :]).astype(F32)
    blocks = jnp.einsum('hrd,qkd->hrqk', rpb.astype(F32), onehot, precision=HIGHEST)
    blocks = jnp.where(col_ok[None, None], blocks, NEG_BIG)
    masked = jnp.full((NA_HEADS, GRID_W, GRID_W), NEG_BIG, F32)
    tabs = []
    for jb in (0, 1, nblk - 1):
        ws = min(max(jb * rb - NA_KR // 2, 0), rows - wr)
        q_rows = []
        for qr in range(jb * rb, (jb + 1) * rb):
            rs = min(max(qr - NA_KR // 2, 0), rows - NA_KR)
            row = [blocks[:, kr - qr + NA_KR - 1] if rs <= kr < rs + NA_KR else masked
                   for kr in range(ws, ws + wr)]
            q_rows.append(jnp.concatenate(row, axis=-1))
        tabs.append(jnp.concatenate(q_rows, axis=1))
    return jnp.stack(tabs)


def _neighbourhood_attention(qkv, qkv_ctx, rpb):
    b, l, _ = qkv.shape
    n_ctx = qkv_ctx.shape[1]
    rows = l // GRID_W
    rb = NA_ROWS_PER_BLOCK
    nblk = rows // rb
    tq = rb * GRID_W
    nkeys = NA_WIN_ROWS * GRID_W
    hp = NA_HEADS * NA_HEAD_DIM // LANES
    tabs = _na_tables(rpb, rows)

    per = NA_BLOCKS_PER_STEP

    def cls(jb):
        return jnp.where(jb == 0, 0, jnp.where(jb == nblk - 1, 2, 1))

    def tab_spec(sb):
        return pl.BlockSpec((1, 2, tq, nkeys), lambda i, h, j: (cls(j * per + sb), h, 0, 0))

    return pl.pallas_call(
        functools.partial(_na_body, rows=rows),
        grid=(b, hp, nblk // per),
        in_specs=[pl.BlockSpec((1, per * tq, LANES), lambda i, h, j: (i, j, h)),
                  pl.BlockSpec((1, l, LANES), lambda i, h, j: (i, 0, hp + h)),
                  pl.BlockSpec((1, l, LANES), lambda i, h, j: (i, 0, 2 * hp + h)),
                  pl.BlockSpec((1, n_ctx, LANES), lambda i, h, j: (i, 0, hp + h)),
                  pl.BlockSpec((1, n_ctx, LANES), lambda i, h, j: (i, 0, 2 * hp + h))]
                 + [tab_spec(sb) for sb in range(per)],
        out_specs=pl.BlockSpec((1, per * tq, LANES), lambda i, h, j: (i, j, h)),
        out_shape=jax.ShapeDtypeStruct((b, l, NA_HEADS * NA_HEAD_DIM), F32),
        compiler_params=_cparams(("parallel", "parallel", "arbitrary")),
        name="neighbourhood_attention",
    )(qkv, qkv, qkv, qkv_ctx, qkv_ctx, *([tabs] * per))


def _ctx_attn_body(q_ref, k_ref, v_ref, o_ref):
    q = q_ref[0]
    k = k_ref[0]
    v = v_ref[0]
    lane = lax.broadcasted_iota(jnp.int32, (1, LANES), 1)
    out = jnp.zeros(q.shape, F32)
    for hh in range(LANES // NA_HEAD_DIM):
        in_head = (lane >= hh * NA_HEAD_DIM) & (lane < (hh + 1) * NA_HEAD_DIM)
        qh = jnp.where(in_head, q, jnp.zeros_like(q))
        s = _dot_nt(qh, k)
        p = jnp.exp(s - jnp.max(s, axis=-1, keepdims=True))
        o = _dot(p.astype(BF16), v) / jnp.sum(p, axis=-1, keepdims=True)
        out = jnp.where(in_head, o, out)
    o_ref[0] = out


def _context_attention(qkv_ctx):
    b, n, _ = qkv_ctx.shape
    hp = NA_HEADS * NA_HEAD_DIM // LANES
    return pl.pallas_call(
        _ctx_attn_body,
        grid=(b, hp),
        in_specs=[pl.BlockSpec((1, n, LANES), lambda i, h: (i, 0, h)),
                  pl.BlockSpec((1, n, LANES), lambda i, h: (i, 0, hp + h)),
                  pl.BlockSpec((1, n, LANES), lambda i, h: (i, 0, 2 * hp + h))],
        out_specs=pl.BlockSpec((1, n, LANES), lambda i, h: (i, 0, h)),
        out_shape=jax.ShapeDtypeStruct((b, n, NA_HEADS * NA_HEAD_DIM), F32),
        compiler_params=_cparams(("parallel", "parallel")),
        name="context_attention",
    )(qkv_ctx, qkv_ctx, qkv_ctx)


def _out_tail(a, b2, x_ref, w_ref, g1_ref, gate_ref, g2_ref, sh_ref, sc_ref, rw_ref, xo_ref, h_ref, aff_ref):
    half = a.shape[-1]
    y = _dot(a.astype(BF16), w_ref[0:half, :]) + _dot(b2.astype(BF16), w_ref[half:2 * half, :])
    xn = x_ref[0] + gate_ref[0] * _rms(y, g1_ref[...])
    xo_ref[0] = xn
    h = _rms(xn, g2_ref[...]) * (1.0 + sc_ref[0]) + sh_ref[0]
    packed = _pack_bf16_pairs(h)
    quarter = packed.shape[-1] // 2
    h_ref[0, 0] = packed[:, 0:quarter]
    h_ref[0, 1] = packed[:, quarter:2 * quarter]
    h_hi = h.astype(BF16)
    h_lo = (h - h_hi.astype(F32)).astype(BF16)
    both = _dot(h_hi, rw_ref[...])
    logits = both[:, 0:LANES] + (both[:, LANES:2 * LANES] + _dot(h_lo, rw_ref[:, 0:LANES]))
    lane = lax.broadcasted_iota(jnp.int32, (1, LANES), 1)
    logits = jnp.where(lane < N_EXPERTS, logits, NEG_BIG)
    e = jnp.exp(logits - jnp.max(logits, axis=-1, keepdims=True))
    aff = e / jnp.sum(e, axis=-1, keepdims=True)
    aff_ref[0] = aff.T[0:N_EXPERTS, :]


def _out_call(body, branch_args, branch_specs, scratch, name, x, w_bf, g1, gate, g2, shift, scale, router_w, tm,
              h_rows=None, h_row0=0, aliases=None):
    b, l, d = x.shape
    h_rows = l if h_rows is None else h_rows
    assert h_row0 % tm == 0
    blk0 = h_row0 // tm
    rw = jnp.pad(router_w, ((0, 0), (0, LANES - N_EXPERTS)))
    rw_hi = rw.astype(BF16)
    rw_cat = jnp.concatenate([rw_hi, (rw - rw_hi.astype(F32)).astype(BF16)], axis=1)
    vec = pl.BlockSpec((1, d), lambda i, j: (0, 0))
    bvec = pl.BlockSpec((1, 1, d), lambda i, j: (i, 0, 0))
    rspec = pl.BlockSpec((d, 2 * LANES), lambda i, j: (0, 0))
    return pl.pallas_call(
        body,
        grid=(b, l // tm),
        in_specs=list(branch_specs) + [pl.BlockSpec((1, tm, d), lambda i, j: (i, j, 0)),
                                       pl.BlockSpec(w_bf.shape, lambda i, j: (0, 0)),
                                       vec, bvec, vec, bvec, bvec, rspec],
        out_specs=[pl.BlockSpec((1, tm, d), lambda i, j: (i, j, 0)),
                   pl.BlockSpec((1, 2, tm, d // 4), lambda i, j: (i, 0, blk0 + j, 0)),
                   pl.BlockSpec((1, N_EXPERTS, tm), lambda i, j: (i, 0, j))],
        out_shape=[jax.ShapeDtypeStruct((b, l, d), F32),
                   jax.ShapeDtypeStruct((b, 2, h_rows, d // 4), jnp.int32),
                   jax.ShapeDtypeStruct((b, N_EXPERTS, l), F32)],
        scratch_shapes=list(scratch),
        input_output_aliases=aliases or {},
        compiler_params=_cparams(("parallel", "parallel")),
        name=name,
    )(*branch_args, x, w_bf, g1.reshape(1, d), gate.reshape(b, 1, d), g2.reshape(1, d),
      shift.reshape(b, 1, d), scale.reshape(b, 1, d), rw_cat)


def _even_out_body(*refs, tile):
    n_conv = 7
    buf_ref, sh_ref, a_s = refs[-3:]
    _conv_body(*refs[:n_conv], a_s, buf_ref, sh_ref, tile=tile, chunk=CONV_CHUNK)
    _out_tail(a_s[0], refs[n_conv][0], *refs[n_conv + 1:-3])


def _even_out(glu, conv_w, conv_b, ln_g, ln_b, na, *common, h_rows):
    tile = common[-1]
    c = glu.shape[-1]
    args, specs, scratch = _conv_operands(glu, conv_w, conv_b, ln_g, ln_b, tile)
    na_spec = pl.BlockSpec((1, tile, na.shape[-1]), lambda i, j: (i, j, 0))
    return _out_call(functools.partial(_even_out_body, tile=tile), args + [na], specs + [na_spec],
                     scratch + [pltpu.VMEM((1, tile, c), F32)], "even_out", *common, h_rows=h_rows)


def _out_into_body(h_all_ref, a_ref, b_ref, *rest):
    del h_all_ref
    _out_tail(a_ref[0], b_ref[0], *rest)


def _out_proj_into(h_all, h_row0, a, b2, *common):
    tm = common[-1]
    spec = pl.BlockSpec((1, tm, a.shape[-1]), lambda i, j: (i, j, 0))
    return _out_call(_out_into_body, [h_all, a, b2], [pl.BlockSpec(memory_space=pl.ANY), spec, spec], [],
                     "out_proj", *common, h_rows=h_all.shape[2], h_row0=h_row0, aliases={0: 1})


def _moe_body(x0_ref, x1_ref, val_ref, wg_ref, wu_ref, wd_ref, o_ref, acc_s, x_s, wg_s, wu_s, wd_s, *, chunk):
    f = pl.program_id(1)
    m, quarter = x0_ref.shape

    @pl.when(f == 0)
    def _():
        def unpack(c, carry):
            r = pl.multiple_of(c * chunk, chunk)
            for s, x_ref in enumerate((x0_ref, x1_ref)):
                lo, hi = _unpack_bf16_pairs(x_ref[pl.ds(r, chunk), :])
                x_s[pl.ds(r, chunk), s * quarter:(s + 1) * quarter] = lo
                x_s[pl.ds(r, chunk), (2 + s) * quarter:(3 + s) * quarter] = hi
            acc_s[pl.ds(r, chunk), :] = jnp.zeros((chunk, 4 * quarter), F32)
            return carry

        lax.fori_loop(0, m // chunk, unpack, 0)

    wg_s[...] = wg_ref[0, 0].astype(BF16)
    wu_s[...] = wu_ref[0, 0].astype(BF16)
    wd_s[...] = wd_ref[0, 0].astype(BF16)

    def rows(c, carry):
        r = pl.multiple_of(c * chunk, chunk)
        xs = x_s[pl.ds(r, chunk), :]
        hid = (_silu(_dot(xs, wg_s[...])) * _dot(xs, wu_s[...])).astype(BF16)
        acc_s[pl.ds(r, chunk), :] += _dot(hid, wd_s[...])
        return carry

    lax.fori_loop(0, m // chunk, rows, 0, unroll=True)

    @pl.when(f == pl.num_programs(1) - 1)
    def _():
        for g0 in range(0, m, LANES):
            w = min(LANES, m - g0)
            eye = lax.broadcasted_iota(jnp.int32, (w, w), 0) == lax.broadcasted_iota(jnp.int32, (w, w), 1)
            col = jnp.sum(jnp.where(eye, val_ref[0, :, g0:g0 + w], 0.0), axis=1, keepdims=True)
            o_ref[0, g0:g0 + w, :] = (acc_s[g0:g0 + w, :] * col).astype(o_ref.dtype)


def _expert_ffn(xg, vals, w_gate, w_up, w_down, layer, chunk):
    e, _, m = vals.shape
    quarter = xg.shape[1]
    d = 4 * quarter
    ff = w_gate.shape[-1]
    tf = EXPERT_FF_TILE
    return pl.pallas_call(
        functools.partial(_moe_body, chunk=chunk),
        grid=(e, ff // tf),
        in_specs=[pl.BlockSpec((m, quarter), lambda i, f: (2 * i, 0)),
                  pl.BlockSpec((m, quarter), lambda i, f: (2 * i + 1, 0)),
                  pl.BlockSpec((1, 1, m), lambda i, f: (i, 0, 0)),
                  pl.BlockSpec((1, 1, d, tf), lambda i, f: (layer, i, 0, f)),
                  pl.BlockSpec((1, 1, d, tf), lambda i, f: (layer, i, 0, f)),
                  pl.BlockSpec((1, 1, tf, d), lambda i, f: (layer, i, f, 0))],
        out_specs=pl.BlockSpec((1, m, d), lambda i, f: (i, 0, 0)),
        out_shape=jax.ShapeDtypeStruct((e, m, d), BF16),
        scratch_shapes=[pltpu.VMEM((m, d), F32), pltpu.VMEM((m, d), BF16), pltpu.VMEM((d, tf), BF16),
                        pltpu.VMEM((d, tf), BF16), pltpu.VMEM((tf, d), BF16)],
        compiler_params=_cparams(("parallel", "arbitrary")),
        name="expert_ffn",
    )(xg, xg, vals, w_gate, w_up, w_down)


def _resid_body(x_ref, f_ref, gate_ref, g_ref, o_ref):
    o_ref[0] = x_ref[0] + gate_ref[0] * _rms(f_ref[0], g_ref[...])


def _gated_residual(x, f, gate, g, tm):
    b, l, d = x.shape
    blk = pl.BlockSpec((1, tm, d), lambda i, j: (i, j, 0))
    return pl.pallas_call(
        _resid_body,
        grid=(b, l // tm),
        in_specs=[blk, blk, pl.BlockSpec((1, 1, d), lambda i, j: (i, 0, 0)),
                  pl.BlockSpec((1, d), lambda i, j: (0, 0))],
        out_specs=blk,
        out_shape=jax.ShapeDtypeStruct((b, l, d), F32),
        compiler_params=_cparams(("parallel", "parallel")),
        name="gated_residual",
    )(x, f, gate.reshape(b, 1, d), g.reshape(1, d))


def _swap_pairs(x):
    nf = GLA_DK // 4
    lane = lax.broadcasted_iota(jnp.int32, (1, LANES), 1)
    up = pltpu.roll(x, LANES - nf, 1)
    down = pltpu.roll(x, nf, 1)
    return jnp.where(lane % (2 * nf) < nf, up, down)


def _odd_in_body(x_ref, g_ref, sh_ref, sc_ref, w_ref, cos_ref, sin_ref, gw_ref, gb_ref,
                 pool_ref, qk_ref, v_ref, r_ref, gate_ref):
    h = (_rms(x_ref[0], g_ref[...]) * (1.0 + sc_ref[0]) + sh_ref[0]).astype(BF16)
    qk = GLA_HEADS * GLA_DK
    vd = GLA_HEADS * GLA_DV
    q0 = POOL_CH
    v0 = q0 + 2 * qk
    r0 = v0 + vd
    l0 = r0 + vd
    pool_ref[0] = _dot(h, w_ref[:, 0:q0])
    qk_raw = _dot(h, w_ref[:, q0:v0])
    for s in range(2 * qk // LANES):
        raw = qk_raw[:, s * LANES:(s + 1) * LANES]
        c = cos_ref[:, (s * LANES) % qk:(s * LANES) % qk + LANES]
        sn = sin_ref[:, (s * LANES) % qk:(s * LANES) % qk + LANES]
        rot = raw * c + _swap_pairs(raw) * sn
        if s * LANES < qk:
            rot = rot * (GLA_DK ** -0.5)
        qk_ref[0, :, s * LANES:(s + 1) * LANES] = rot
    v_ref[0] = _dot(h, w_ref[:, v0:r0]).astype(BF16)
    r_ref[0] = _dot(h, w_ref[:, r0:l0])
    lr = _dot(h, w_ref[:, l0:l0 + 2 * GLA_RANK])
    z = jnp.dot(lr, gw_ref[...], precision=HIGHEST, preferred_element_type=F32) + gb_ref[...]
    gate_ref[0] = (jnp.minimum(z, 0.0) - jnp.log1p(jnp.exp(-jnp.abs(z)))) * (1.0 / GLA_TAU)


def _odd_in(x, g, shift, scale, w_bf, cos_t, sin_t, gate_w, gate_b, tm):
    b, l, d = x.shape
    n = w_bf.shape[1]
    qk = GLA_HEADS * GLA_DK
    vd = GLA_HEADS * GLA_DV
    gw = jnp.zeros((2 * GLA_RANK, 2 * qk), F32)
    gw = gw.at[:GLA_RANK, :qk].set(gate_w[0]).at[GLA_RANK:, qk:].set(gate_w[1])
    gb = jnp.concatenate([gate_b[0], gate_b[1]]).reshape(1, 2 * qk)
    vec = pl.BlockSpec((1, 1, d), lambda i, j: (i, 0, 0))
    row = lambda w: pl.BlockSpec((1, tm, w), lambda i, j: (i, j, 0))
    return pl.pallas_call(
        _odd_in_body,
        grid=(b, l // tm),
        in_specs=[row(d), pl.BlockSpec((1, d), lambda i, j: (0, 0)), vec, vec,
                  pl.BlockSpec((d, n), lambda i, j: (0, 0)),
                  pl.BlockSpec((tm, qk), lambda i, j: (j, 0)),
                  pl.BlockSpec((tm, qk), lambda i, j: (j, 0)),
                  pl.BlockSpec((2 * GLA_RANK, 2 * qk), lambda i, j: (0, 0)),
                  pl.BlockSpec((1, 2 * qk), lambda i, j: (0, 0))],
        out_specs=[row(POOL_CH), row(2 * qk), row(vd), row(vd), row(2 * qk)],
        out_shape=[jax.ShapeDtypeStruct((b, l, POOL_CH), F32),
                   jax.ShapeDtypeStruct((b, l, 2 * qk), F32),
                   jax.ShapeDtypeStruct((b, l, vd), BF16),
                   jax.ShapeDtypeStruct((b, l, vd), F32),
                   jax.ShapeDtypeStruct((b, l, 2 * qk), F32)],
        compiler_params=_cparams(("parallel", "parallel")),
        name="odd_in",
    )(x, g.reshape(1, d), shift.reshape(b, 1, d), scale.reshape(b, 1, d), w_bf, cos_t, sin_t, gw, gb)


def _rope_tables(l):
    t = np.arange(l)
    nf = GLA_DK // 4
    inv = np.power(ROPE_BASE, -np.arange(nf, dtype=np.float64) / nf)
    ar = (t // GRID_W)[:, None] * inv[None, :]
    ac = (t % GRID_W)[:, None] * inv[None, :]
    cos_h = np.concatenate([np.cos(ar), np.cos(ar), np.cos(ac), np.cos(ac)], axis=-1)
    sin_h = np.concatenate([-np.sin(ar), np.sin(ar), -np.sin(ac), np.sin(ac)], axis=-1)
    as_table = lambda a: jnp.asarray(np.tile(a, (1, GLA_HEADS)).astype(np.float32))
    return as_table(cos_h), as_table(sin_h)


def _gla_tile(qk, v, g, s, reverse):
    hk = GLA_HEADS * GLA_DK
    hv = GLA_HEADS * GLA_DV
    c = GLA_CHUNK
    t = qk.shape[0]
    n = t // c
    last_row, mid_row = (0, c // 2) if reverse else (c - 1, c // 2 - 1)
    ii = lax.broadcasted_iota(jnp.int32, (t, t), 0)
    jj = lax.broadcasted_iota(jnp.int32, (t, t), 1)
    ordered = (jj >= ii) if reverse else (jj <= ii)
    tri = jnp.where(ordered & (ii // c == jj // c), 1.0, 0.0).astype(BF16)
    g_hi = g.astype(BF16)
    rem = g - g_hi.astype(F32)
    g_mid = rem.astype(BF16)
    g_lo = (rem - g_mid.astype(F32)).astype(BF16)
    bc = _dot(tri, g_hi) + (_dot(tri, g_mid) + _dot(tri, g_lo))
    spread = lambda row: jnp.concatenate(
        [jnp.broadcast_to(bc[i * c + row:i * c + row + 1, :], (c, hk)) for i in range(n)], axis=0)
    b_mid = spread(mid_row)
    b_last = spread(last_row)
    qt = qk[:, 0:hk] * jnp.exp(bc - b_mid)
    kt = qk[:, hk:2 * hk] * jnp.exp(b_mid - bc)
    qe = (qt * jnp.exp(b_mid)).astype(BF16)
    ke = kt * jnp.exp(b_last - b_mid)
    ktb = kt.astype(BF16)
    lane = lax.broadcasted_iota(jnp.int32, (1, hk), 1)
    ci = lax.broadcasted_iota(jnp.int32, (c, c), 0)
    cj = lax.broadcasted_iota(jnp.int32, (c, c), 1)
    causal = (cj >= ci) if reverse else (cj <= ci)
    blockdiag = (lax.broadcasted_iota(jnp.int32, (hk, hv), 0) // GLA_DK
                 == lax.broadcasted_iota(jnp.int32, (hk, hv), 1) // GLA_DV)
    intra, upd, decay = [], [], []
    for i in range(n):
        rows = slice(i * c, (i + 1) * c)
        qs = jnp.concatenate(
            [jnp.where((lane >= h * GLA_DK) & (lane < (h + 1) * GLA_DK), qt[rows], 0.0) for h in range(GLA_HEADS)],
            axis=0).astype(BF16)
        att = _dot_nt(qs, ktb[rows])
        intra.append(jnp.concatenate(
            [_dot(jnp.where(causal, att[h * c:(h + 1) * c], 0.0).astype(BF16),
                  v[rows, h * GLA_DV:(h + 1) * GLA_DV]) for h in range(GLA_HEADS)], axis=-1))
        upd.append(jnp.where(blockdiag, _dot(ke[rows].T.astype(BF16), v[rows]), 0.0))
        decay.append(jnp.exp(jnp.sum(g[rows].T, axis=1, keepdims=True)))
    outs = [None] * n
    for i in (reversed(range(n)) if reverse else range(n)):
        rows = slice(i * c, (i + 1) * c)
        outs[i] = _dot(qe[rows], s.astype(BF16)) + intra[i]
        s = decay[i] * s + upd[i]
    return jnp.concatenate(outs, axis=0), s


def _gla_body(qkf_ref, qkb_ref, vf_ref, vb_ref, gf_ref, gb_ref, s0f_ref, s0b_ref,
              of_ref, ob_ref, sff_ref, sbf_ref, sf_ref, sb_ref, *, tile):
    n = pl.program_id(1)
    hk = GLA_HEADS * GLA_DK
    hv = GLA_HEADS * GLA_DV

    @pl.when(n == 0)
    def _():
        sf_ref[...] = jnp.zeros((hk, hv), F32)
        sb_ref[...] = jnp.zeros((hk, hv), F32)
        for h in range(GLA_HEADS):
            sf_ref[h * GLA_DK:(h + 1) * GLA_DK, h * GLA_DV:(h + 1) * GLA_DV] = s0f_ref[0, h]
            sb_ref[h * GLA_DK:(h + 1) * GLA_DK, h * GLA_DV:(h + 1) * GLA_DV] = s0b_ref[0, h]

    of_ref[0], sf_ref[...] = _gla_tile(qkf_ref[0], vf_ref[0], gf_ref[0], sf_ref[...], False)
    ob_ref[0], sb_ref[...] = _gla_tile(qkb_ref[0], vb_ref[0], gb_ref[0], sb_ref[...], True)

    @pl.when(n == pl.num_programs(1) - 1)
    def _():
        for h in range(GLA_HEADS):
            sff_ref[0, h] = sf_ref[h * GLA_DK:(h + 1) * GLA_DK, h * GLA_DV:(h + 1) * GLA_DV]
            sbf_ref[0, h] = sb_ref[h * GLA_DK:(h + 1) * GLA_DK, h * GLA_DV:(h + 1) * GLA_DV]


def _gla(qk, v, gates, s0f, s0b, tile):
    b, l, _ = qk.shape
    hk = GLA_HEADS * GLA_DK
    hv = GLA_HEADS * GLA_DV
    nt = l // tile
    fwd = lambda w, col: pl.BlockSpec((1, tile, w), lambda i, n: (i, n, col))
    bwd = lambda w, col: pl.BlockSpec((1, tile, w), lambda i, n: (i, nt - 1 - n, col))
    st = pl.BlockSpec((1, GLA_HEADS, GLA_DK, GLA_DV), lambda i, n: (i, 0, 0, 0))
    return pl.pallas_call(
        functools.partial(_gla_body, tile=tile),
        grid=(b, nt),
        in_specs=[fwd(2 * hk, 0), bwd(2 * hk, 0), fwd(hv, 0), bwd(hv, 0), fwd(hk, 0), bwd(hk, 1), st, st],
        out_specs=[fwd(hv, 0), bwd(hv, 0), st, st],
        out_shape=[jax.ShapeDtypeStruct((b, l, hv), F32), jax.ShapeDtypeStruct((b, l, hv), F32),
                   jax.ShapeDtypeStruct((b, GLA_HEADS, GLA_DK, GLA_DV), F32),
                   jax.ShapeDtypeStruct((b, GLA_HEADS, GLA_DK, GLA_DV), F32)],
        scratch_shapes=[pltpu.VMEM((hk, hv), F32), pltpu.VMEM((hk, hv), F32)],
        compiler_params=_cparams(("parallel", "arbitrary")),
        name="gla_scan",
    )(qk, qk, v, v, gates, gates, s0f, s0b)


def _odd_mid_body(cur_ref, prev_ref, next_ref, of_ref, ob_ref, r_ref, hg_ref, pw_ref, ps_ref,
                  pool_ref, d_ref, buf_ref, *, tile, seq):
    j = pl.program_id(1)
    last = pl.num_programs(1) - 1
    hal = POOL_HALO
    buf_ref[0:hal, :] = jnp.where(j == 0, 0.0, prev_ref[0])
    buf_ref[hal:hal + tile, :] = cur_ref[0]
    buf_ref[hal + tile:hal + tile + hal, :] = jnp.where(j == last, 0.0, next_ref[0])
    t = j * tile + lax.broadcasted_iota(jnp.int32, (tile, 1), 0)
    for gi, win in enumerate(POOL_WINDOWS):
        cols = slice(gi * POOL_GROUP, (gi + 1) * POOL_GROUP)
        acc = jnp.zeros((tile, POOL_GROUP), F32)
        for off in range(-(win // 2), win - win // 2):
            acc = acc + buf_ref[hal + off:hal + off + tile, cols]
        cnt = jnp.minimum(t + (win - win // 2), seq) - jnp.maximum(t - win // 2, 0)
        diff = acc / cnt.astype(F32) - cur_ref[0, :, cols]
        pool_ref[0, :, cols] = _dot(diff.astype(BF16), pw_ref[gi]) * ps_ref[:, cols]
    for h in range(GLA_HEADS):
        cols = slice(h * GLA_DV, (h + 1) * GLA_DV)
        o = of_ref[0, :, cols] + ob_ref[0, :, cols]
        d_ref[0, :, cols] = _rms(o, hg_ref[:, cols]) * _silu(r_ref[0, :, cols])


def _odd_out_body(*refs, tile, seq):
    n_branch = 9
    buf_ref, pool_s, d_s = refs[-3:]
    _odd_mid_body(*refs[:n_branch], pool_s, d_s, buf_ref, tile=tile, seq=seq)
    _out_tail(pool_s[0], d_s[0], *refs[n_branch:-3])


def _odd_out(pool_u, o_f, o_b, r, head_g, pool_w_bf, pool_scale, *common):
    b, l, c = pool_u.shape
    tile = common[-1]
    hal = POOL_HALO
    per = tile // hal
    nh = l // hal
    blk = pl.BlockSpec((1, tile, c), lambda i, j: (i, j, 0))
    vec = pl.BlockSpec((1, c), lambda i, j: (0, 0))
    specs = [blk,
             pl.BlockSpec((1, hal, c), lambda i, j: (i, jnp.maximum(j * per - 1, 0), 0)),
             pl.BlockSpec((1, hal, c), lambda i, j: (i, jnp.minimum((j + 1) * per, nh - 1), 0)),
             blk, blk, blk, vec,
             pl.BlockSpec((len(POOL_WINDOWS), POOL_GROUP, POOL_GROUP), lambda i, j: (0, 0, 0)),
             vec]
    scratch = [pltpu.VMEM((tile + 2 * hal, c), F32), pltpu.VMEM((1, tile, c), F32), pltpu.VMEM((1, tile, c), F32)]
    args = [pool_u, pool_u, pool_u, o_f, o_b, r, head_g.reshape(1, c), pool_w_bf, pool_scale.reshape(1, c)]
    return _out_call(functools.partial(_odd_out_body, tile=tile, seq=l), args, specs, scratch, "odd_out", *common)


def _threshold_body(a_ref, thr_ref, need_ref, *, cap):
    bits = lax.bitcast_convert_type(a_ref[0], jnp.int32)
    rows = bits.shape[0]
    count_ge = lambda v: jnp.sum(jnp.where(bits >= v, 1.0, 0.0), axis=-1, keepdims=True)

    def step(_, carry):
        lo, hi = carry
        mid = lo + ((hi - lo + 1) >> 1)
        ok = count_ge(mid) >= cap
        return jnp.where(ok, mid, lo), jnp.where(ok, hi, mid - 1)

    lo, _ = lax.fori_loop(0, 31, step, (jnp.zeros((rows, 1), jnp.int32),
                                        jnp.full((rows, 1), F32_INF_BITS, jnp.int32)))
    above = jnp.sum(jnp.where(bits > lo, 1.0, 0.0), axis=-1, keepdims=True)
    width = thr_ref.shape[-1]
    thr_ref[0] = jnp.broadcast_to(lax.bitcast_convert_type(lo, F32), (rows, width))
    need_ref[0] = jnp.broadcast_to(cap - above.astype(jnp.int32), (rows, width))


def _route_threshold(aff_t, cap):
    b, e, n = aff_t.shape
    out = pl.BlockSpec((1, e, SC_LANES), lambda i: (i, 0, 0))
    return pl.pallas_call(
        functools.partial(_threshold_body, cap=cap),
        grid=(b,),
        in_specs=[pl.BlockSpec((1, e, n), lambda i: (i, 0, 0))],
        out_specs=[out, out],
        out_shape=[jax.ShapeDtypeStruct((b, e, SC_LANES), F32), jax.ShapeDtypeStruct((b, e, SC_LANES), jnp.int32)],
        compiler_params=_cparams(("parallel",)),
        name="route_threshold",
    )(aff_t)


def _route_compact(aff, thr, need, cap):
    r, n = aff.shape
    lanes = SC_LANES
    assert r == SC_CORES * SC_SUBCORES and n % lanes == 0
    mesh = plsc.VectorSubcoreMesh(core_axis_name="core", subcore_axis_name="subcore",
                                  num_cores=SC_CORES, num_subcores=SC_SUBCORES)

    @pl.kernel(out_type=[jax.ShapeDtypeStruct((r, cap), jnp.int32), jax.ShapeDtypeStruct((r, cap), F32)],
               mesh=mesh,
               scratch_types=[pltpu.VMEM((n,), F32), pltpu.VMEM((lanes,), F32), pltpu.VMEM((lanes,), jnp.int32),
                              pltpu.VMEM((cap,), jnp.int32), pltpu.VMEM((cap,), F32)],
               compiler_params=pltpu.CompilerParams(needs_layout_passes=False),
               name="route_compact")
    def compact(aff_hbm, thr_hbm, need_hbm, idx_hbm, val_hbm, row_v, thr_v, need_v, idx_v, val_v):
        w = lax.axis_index("subcore") * SC_CORES + lax.axis_index("core")
        pltpu.sync_copy(aff_hbm.at[w], row_v)
        pltpu.sync_copy(thr_hbm.at[w], thr_v)
        pltpu.sync_copy(need_hbm.at[w], need_v)
        thr = thr_v[...]
        need = need_v[...]
        lane = lax.iota(jnp.int32, lanes)
        ones = jnp.ones((lanes,), jnp.int32)

        def body(i, carry):
            n_out, n_eq = carry
            x = row_v[pl.ds(i * lanes, lanes)]
            eq = x == thr
            take = (x > thr) | (eq & (n_eq + plsc.cumsum(ones, mask=eq) <= need))
            pos = n_out + plsc.cumsum(ones, mask=take) - 1
            take = take & (pos < cap)
            plsc.store_scatter(idx_v, [pos], lane + i * lanes, mask=take)
            plsc.store_scatter(val_v, [pos], x, mask=take)
            return (n_out + plsc.all_reduce_population_count(take),
                    n_eq + plsc.all_reduce_population_count(eq))

        zero = jnp.zeros((lanes,), jnp.int32)
        lax.fori_loop(0, n // lanes, body, (zero, zero))
        pltpu.sync_copy(idx_v, idx_hbm.at[w])
        pltpu.sync_copy(val_v, val_hbm.at[w])

    return compact(aff, thr, need)


def _route(aff_t, tok_base, row0, stride):
    b, e, n = aff_t.shape
    cap = EC_CAPACITY_FACTOR * n // N_EXPERTS
    thr, need = _route_threshold(aff_t, cap)
    idx, vals = _route_compact(aff_t.reshape(b * e, n), thr.reshape(b * e, -1), need.reshape(b * e, -1), cap)
    idx = idx.reshape(b, e, cap)
    vals = vals.reshape(b, e, cap)
    bi = jnp.arange(b, dtype=idx.dtype)[:, None, None]
    per_expert = lambda a: jnp.swapaxes(a, 0, 1).reshape(e, b * cap)
    rows0 = idx + row0 + 2 * bi * stride
    return (per_expert(vals), per_expert(idx + tok_base + bi * n), per_expert(rows0), per_expert(rows0 + stride),
            per_expert(idx))


def _combine_first_start(p0, cap):
    return jnp.minimum((p0 // BF16_ROWS) * BF16_ROWS, cap - COMBINE_FIRST)


def _combine_body(offs_ref, spill_ref, tok_ref, y_ref, x_ref, gate_ref, g_ref, o_ref, f_s, *, n_tok, cap):
    bi = pl.program_id(0)
    j = pl.program_id(1)
    tt = COMBINE_TILE
    wf = COMBINE_FIRST
    wn = COMBINE_WINDOW
    ntiles = n_tok // tt
    n_exp = y_ref.shape[0]
    group = wn // wf
    sub = lax.broadcasted_iota(jnp.int32, (tt, 1), 0)
    lane = lax.broadcasted_iota(jnp.int32, (1, wn), 1)

    def slot_range(e):
        base = (bi * n_exp + e) * (ntiles + 1) + j
        return offs_ref[base], offs_ref[base + 1]

    acc = jnp.zeros(f_s.shape, F32)
    for g in range(n_exp // group):
        ys = []
        toks = jnp.full((1, wn), -1, jnp.int32)
        for k in range(group):
            e = g * group + k
            start = pl.multiple_of(_combine_first_start(slot_range(e)[0], cap), BF16_ROWS)
            ys.append(y_ref[e, pl.ds(start, wf), :])
            cs = pl.multiple_of(jnp.minimum((start // LANES) * LANES, cap - wn), LANES)
            rolled = pltpu.roll(tok_ref[e, :, pl.ds(cs, wn)], (k * wf + wn - (start - cs)) % wn, 1)
            toks = jnp.where((lane >= k * wf) & (lane < (k + 1) * wf), rolled, toks)
        hit = (toks - j * tt) == sub
        acc = acc + _dot(jnp.where(hit, 1.0, 0.0).astype(BF16), jnp.concatenate(ys, axis=0))
    f_s[...] = acc

    def more_windows(e, carry):
        p0, p1 = slot_range(e)
        lo = _combine_first_start(p0, cap) + wf
        first = (lo // LANES) * LANES

        def extra(w, carry):
            cs = pl.multiple_of(jnp.minimum(first + w * wn, cap - wn), LANES)
            tok = tok_ref[e, :, pl.ds(cs, wn)] - j * tt
            hit = (tok == sub) & (cs + lane >= jnp.maximum(lo, first + w * wn))
            f_s[...] += _dot(jnp.where(hit, 1.0, 0.0).astype(BF16), y_ref[e, pl.ds(cs, wn), :])
            return carry

        lax.fori_loop(0, (jnp.maximum(p1 - first, 0) + wn - 1) // wn * (p1 > lo).astype(jnp.int32), extra, 0)
        return carry

    @pl.when(spill_ref[bi * ntiles + j] != 0)
    def _():
        lax.fori_loop(0, n_exp, more_windows, 0)

    o_ref[0] = x_ref[0] + gate_ref[0] * _rms(f_s[...], g_ref[...])


def _combine(y, tok, local, seg0, x, gate, g):
    e, _, d = y.shape
    _, b, cap = local.shape
    n_tok = x.shape[1]
    tt = COMBINE_TILE
    wf = COMBINE_FIRST
    ntiles = n_tok // tt
    assert seg0 % cap == 0 and n_tok % tt == 0 and cap % COMBINE_WINDOW == 0 and e % (COMBINE_WINDOW // wf) == 0
    seg = seg0 // cap
    bounds = jnp.arange(ntiles + 1, dtype=jnp.int32) * tt
    offs = jnp.sum((local[..., None] < bounds).astype(jnp.int32), axis=2)
    starts = _combine_first_start(offs[..., :-1], cap)
    spill = jnp.any(offs[..., 1:] > starts + wf, axis=0).astype(jnp.int32).reshape(-1)
    row = pl.BlockSpec((1, tt, d), lambda i, j, offs, spill: (i, j, 0))
    grid_spec = pltpu.PrefetchScalarGridSpec(
        num_scalar_prefetch=2,
        grid=(b, ntiles),
        in_specs=[pl.BlockSpec((e, 1, cap), lambda i, j, offs, spill: (0, 0, seg + i)),
                  pl.BlockSpec((e, cap, d), lambda i, j, offs, spill: (0, seg + i, 0),
                               pipeline_mode=pl.Buffered(1)),
                  row,
                  pl.BlockSpec((1, 1, d), lambda i, j, offs, spill: (i, 0, 0)),
                  pl.BlockSpec((1, d), lambda i, j, offs, spill: (0, 0))],
        out_specs=row,
        scratch_shapes=[pltpu.VMEM((tt, d), F32)],
    )
    return pl.pallas_call(
        functools.partial(_combine_body, n_tok=n_tok, cap=cap),
        grid_spec=grid_spec,
        out_shape=jax.ShapeDtypeStruct((b, n_tok, d), F32),
        compiler_params=_cparams(("parallel", "arbitrary")),
        name="combine",
    )(jnp.swapaxes(offs, 0, 1).reshape(-1), spill, tok, y, x, gate.reshape(b, 1, d), g.reshape(1, d))


def _gather_rows(src, idx):
    window = SC_GATHER_WINDOW
    n = idx.shape[0]
    width = src.shape[1]
    assert 2 * window * width * 4 <= SC_TILE_VMEM_BUDGET, width
    assert n % (window * SC_CORES * SC_SUBCORES) == 0, n
    mesh = plsc.VectorSubcoreMesh(core_axis_name="core", subcore_axis_name="subcore",
                                  num_cores=SC_CORES, num_subcores=SC_SUBCORES)

    @pl.kernel(out_type=jax.ShapeDtypeStruct((n, width), src.dtype), mesh=mesh, scratch_types=[],
               name="gather_rows")
    def gather(src_hbm, idx_hbm, out_hbm):
        def body(idx_vmem, out_vmem):
            pltpu.sync_copy(src_hbm.at[idx_vmem.at[0]], out_vmem)

        pltpu.emit_pipeline(
            body,
            grid=(n // window,),
            in_specs=[pl.BlockSpec((1, window), lambda i: (0, i))],
            out_specs=[pl.BlockSpec((window, width), lambda i: (i, 0))],
            core_axis_name=("core", "subcore"),
            dimension_semantics=(pltpu.PARALLEL,),
        )(idx_hbm, out_hbm)

    return gather(src, idx.reshape(1, n))


def _moe(parts, h, g, w_gate, w_up, w_down, layer):
    quarter = h.shape[-1]
    d = 4 * quarter
    sizes = [p[0].shape[0] * p[0].shape[2] for p in parts]
    bases = [sum(sizes[:i]) for i in range(len(parts))]
    routed = [_route(p[0], base, p[1], h.shape[2]) for p, base in zip(parts, bases)]
    src = h.reshape(-1, quarter)
    vals, flat, rows0, rows1, tok = (jnp.concatenate([r[i] for r in routed], axis=1) for i in range(5))
    e, m = flat.shape
    rows = jnp.stack([rows0, rows1], axis=1).reshape(-1)
    unit = SC_GATHER_WINDOW * SC_CORES * SC_SUBCORES
    fill = jnp.arange(-rows.shape[0] % unit, dtype=rows.dtype)
    xg = _gather_rows(src, jnp.concatenate([rows, fill]))
    chunk = next(c for c in (512, 528, 384, 320, 256, 128) if m % c == 0)
    y = _expert_ffn(xg, vals[:, None, :], w_gate, w_up, w_down, layer, chunk)
    outs = []
    seg0 = 0
    for (aff_t, _, x, gate), r, base in zip(parts, routed, bases):
        b, _, n = aff_t.shape
        cap = r[0].shape[1] // b
        if n % COMBINE_TILE == 0 and cap % COMBINE_WINDOW == 0 and seg0 % cap == 0:
            outs.append(_combine(y, tok[:, None, :], r[4].reshape(e, b, cap), seg0, x, gate, g))
        else:
            ids = r[1] - base
            part = y[:, seg0:seg0 + b * cap].astype(F32)
            f = jnp.zeros((b * n, d), F32).at[ids.reshape(-1)].add(part.reshape(-1, d))
            outs.append(_gated_residual(x, f.reshape(b, n, d), gate, g, n))
        seg0 += b * cap
    return outs


def kernel(x, c, ctx, c_ctx, w_mod, b_mod, norm_g, w_in_even, w_out_even, conv_w, conv_b, conv_ln_g,
           conv_ln_b, na_rpb, w_in_odd, w_out_odd, pool_w, pool_scale, gla_gate_w, gla_gate_b, gla_head_g,
           router_w, expert_w_gate, expert_w_up, expert_w_down):
    b, l, d = x.shape
    n_ctx = ctx.shape[1]
    tm = ROW_TILE

    mod_rows = jnp.concatenate([c, c_ctx[None], jnp.zeros((8 - b - 1, d), F32)], axis=0)

    mod_all = _modulation(mod_rows, w_mod, b_mod)

    def modulation(i):
        mm = mod_all[i]
        m = mm[:b].reshape(b, 6, d)
        mc = jnp.broadcast_to(mm[b].reshape(1, 6, d), (b, 6, d))
        return m, mc

    m, mc = modulation(0)
    g = norm_g[0]
    w_in = w_in_even[0].astype(BF16)
    w_out = w_out_even[0].astype(BF16)
    glu, qkv = _even_in(x, g[0], m[:, 0], m[:, 1], w_in, tm)
    glu_c, qkv_c = _even_in(ctx, g[0], mc[:, 0], mc[:, 1], w_in, n_ctx)
    a_ctx = _conv_branch(glu_c, conv_w[0], conv_b[0], conv_ln_g[0], conv_ln_b[0], n_ctx)
    na = _neighbourhood_attention(qkv, qkv_c, na_rpb[0])
    att_c = _context_attention(qkv_c)
    x, h2, aff = _even_out(glu, conv_w[0], conv_b[0], conv_ln_g[0], conv_ln_b[0], na,
                           x, w_out, g[1], m[:, 2], g[2], m[:, 3], m[:, 4], router_w[0], tm, h_rows=l + n_ctx)
    ctx, h2, aff_c = _out_proj_into(h2, l, a_ctx, att_c, ctx, w_out, g[1], mc[:, 2], g[2], mc[:, 3], mc[:, 4],
                                    router_w[0], n_ctx)
    x, ctx = _moe([(aff, 0, x, m[:, 5]), (aff_c, l, ctx, mc[:, 5])], h2, g[3],
                  expert_w_gate, expert_w_up, expert_w_down, 0)

    m, mc = modulation(1)
    g = norm_g[1]
    w_in = w_in_odd[0].astype(BF16)
    w_out = w_out_odd[0].astype(BF16)
    cos_t, sin_t = _rope_tables(l)
    ones_t = jnp.ones((n_ctx, GLA_HEADS * GLA_DK), F32)
    _, qk_c, v_c, _, gate_c = _odd_in(ctx, g[0], mc[:, 0], mc[:, 1], w_in, ones_t, jnp.zeros_like(ones_t),
                                      gla_gate_w[0], gla_gate_b[0], n_ctx)
    s_zero = jnp.zeros((b, GLA_HEADS, GLA_DK, GLA_DV), F32)
    _, _, s_f, s_b = _gla(qk_c, v_c, gate_c, s_zero, s_zero, n_ctx)
    pool_u, qk, v, r, gate = _odd_in(x, g[0], m[:, 0], m[:, 1], w_in, cos_t, sin_t,
                                     gla_gate_w[0], gla_gate_b[0], tm)
    o_f, o_b, _, _ = _gla(qk, v, gate, s_f, s_b, SEQ_TILE)
    x, h2, aff = _odd_out(pool_u, o_f, o_b, r, gla_head_g[0], pool_w[0].astype(BF16), pool_scale[0],
                          x, w_out, g[1], m[:, 2], g[2], m[:, 3], m[:, 4], router_w[1], tm)
    (x,) = _moe([(aff, 0, x, m[:, 5])], h2, g[3], expert_w_gate, expert_w_up, expert_w_down, 1)
    return x
```

```python
import functools

import jax
import jax.numpy as jnp
import numpy as np
from jax import lax
from jax.experimental import pallas as pl
from jax.experimental.pallas import tpu as pltpu
from jax.experimental.pallas import tpu_sc as plsc

F32 = jnp.float32
BF16 = jnp.bfloat16
HIGHEST = lax.Precision.HIGHEST

D_MODEL = 1024
GRID_W = 64
EPS = 1e-6
CONV_CH = 512
CONV_WIDTH = 31
CONV_HALO = 16
CONV_CHUNK = 32
NA_HEADS = 8
NA_HEAD_DIM = 64
NA_KR = 8
NA_KC = 16
NA_ROWS_PER_BLOCK = 4
NA_BLOCKS_PER_STEP = 4
NA_WIN_ROWS = 12
POOL_CH = 512
POOL_WINDOWS = (2, 4, 8, 16)
POOL_GROUP = 128
POOL_HALO = 8
GLA_HEADS = 4
GLA_DK = 64
GLA_DV = 128
GLA_RANK = 16
GLA_TAU = 16.0
GLA_CHUNK = 64
ROPE_BASE = 10000.0
N_EXPERTS = 16
EC_CAPACITY_FACTOR = 2
ROW_TILE = 512
SEQ_TILE = 256
MOD_COLS = 1536
EXPERT_FF_TILE = 256
LANES = 128
SUBLANES = 8
NEG_BIG = -1e30
VMEM_LIMIT = 56 * 1024 * 1024
SC_CORES = 2
SC_SUBCORES = 16
SC_LANES = 16
F32_INF_BITS = 0x7F800000
SC_TILE_VMEM_BUDGET = 400 * 1024
SC_GATHER_WINDOW = 128
COMBINE_TILE = 256
COMBINE_WINDOW = 256
COMBINE_FIRST = 64
BF16_ROWS = 16


def _cparams(sem):
    return pltpu.CompilerParams(dimension_semantics=sem, vmem_limit_bytes=VMEM_LIMIT)


def _rms(x, g):
    return x * lax.rsqrt(jnp.mean(x * x, axis=-1, keepdims=True) + EPS) * g


def _sigmoid(x):
    return 1.0 / (1.0 + jnp.exp(-x))


def _silu(x):
    return x * _sigmoid(x)


def _dot(a, b):
    return jnp.dot(a, b, preferred_element_type=F32)


def _pack_bf16_pairs(h):
    half = h.shape[-1] // 2
    bits = lax.bitcast_convert_type(h.astype(BF16).astype(F32), jnp.uint32)
    packed = (bits[:, half:] & jnp.uint32(0xFFFF0000)) | (bits[:, :half] >> 16)
    return lax.bitcast_convert_type(packed, jnp.int32)


def _unpack_bf16_pairs(p):
    bits = lax.bitcast_convert_type(p, jnp.uint32)
    lo = lax.bitcast_convert_type(bits << 16, F32).astype(BF16)
    hi = lax.bitcast_convert_type(bits & jnp.uint32(0xFFFF0000), F32).astype(BF16)
    return lo, hi


def _dot_nt(a, b):
    return lax.dot_general(a, b, (((1,), (1,)), ((), ())), preferred_element_type=F32)


def _mod_body(c_ref, w_ref, b_ref, o_ref):
    o_ref[0] = jnp.dot(_silu(c_ref[...]), w_ref[0], precision=HIGHEST,
                       preferred_element_type=F32) + b_ref[0]


def _modulation(rows, w, b):
    depth, _, n = w.shape
    tn = MOD_COLS
    return pl.pallas_call(
        _mod_body,
        grid=(depth, n // tn),
        in_specs=[pl.BlockSpec((8, D_MODEL), lambda i, j: (0, 0)),
                  pl.BlockSpec((1, D_MODEL, tn), lambda i, j: (i, 0, j)),
                  pl.BlockSpec((1, 1, tn), lambda i, j: (i, 0, j))],
        out_specs=pl.BlockSpec((1, 8, tn), lambda i, j: (i, 0, j)),
        out_shape=jax.ShapeDtypeStruct((depth, 8, n), F32),
        compiler_params=_cparams(("parallel", "parallel")),
        name="modulation",
    )(rows, w, b.reshape(depth, 1, n))


def _even_in_body(x_ref, g_ref, sh_ref, sc_ref, w_ref, glu_ref, qkv_ref):
    h = (_rms(x_ref[0], g_ref[...]) * (1.0 + sc_ref[0]) + sh_ref[0]).astype(BF16)
    c = CONV_CH
    glu_ref[0] = _dot(h, w_ref[:, 0:c]) * _sigmoid(_dot(h, w_ref[:, c:2 * c]))
    hd = NA_HEADS * NA_HEAD_DIM
    q0 = 2 * c
    qkv_ref[0, :, 0:hd] = (_dot(h, w_ref[:, q0:q0 + hd]) * (NA_HEAD_DIM ** -0.5)).astype(BF16)
    qkv_ref[0, :, hd:3 * hd] = _dot(h, w_ref[:, q0 + hd:q0 + 3 * hd]).astype(BF16)


def _even_in(x, g, shift, scale, w_bf, tm):
    b, l, d = x.shape
    n = w_bf.shape[1]
    hd3 = 3 * NA_HEADS * NA_HEAD_DIM
    vec = pl.BlockSpec((1, 1, d), lambda i, j: (i, 0, 0))
    return pl.pallas_call(
        _even_in_body,
        grid=(b, l // tm),
        in_specs=[pl.BlockSpec((1, tm, d), lambda i, j: (i, j, 0)),
                  pl.BlockSpec((1, d), lambda i, j: (0, 0)),
                  vec, vec,
                  pl.BlockSpec((d, n), lambda i, j: (0, 0))],
        out_specs=[pl.BlockSpec((1, tm, CONV_CH), lambda i, j: (i, j, 0)),
                   pl.BlockSpec((1, tm, hd3), lambda i, j: (i, j, 0))],
        out_shape=[jax.ShapeDtypeStruct((b, l, CONV_CH), F32),
                   jax.ShapeDtypeStruct((b, l, hd3), BF16)],
        compiler_params=_cparams(("parallel", "parallel")),
        name="even_in",
    )(x, g.reshape(1, d), shift.reshape(b, 1, d), scale.reshape(b, 1, d), w_bf)


def _conv_body(cur_ref, prev_ref, next_ref, w_ref, b_ref, lg_ref, lb_ref, o_ref, buf_ref, sh_ref, *, tile, chunk):
    j = pl.program_id(1)
    last = pl.num_programs(1) - 1
    hal = CONV_HALO
    buf_ref[0:hal, :] = jnp.where(j == 0, 0.0, prev_ref[0])
    buf_ref[hal:hal + tile, :] = cur_ref[0]
    buf_ref[hal + tile:hal + tile + hal, :] = jnp.where(j == last, 0.0, next_ref[0])
    span = sh_ref.shape[1]
    for s in range(SUBLANES):
        sh_ref[s] = buf_ref[s:s + span, :]
    first = hal - CONV_WIDTH // 2
    reps = chunk // SUBLANES

    last_a = (first + CONV_WIDTH - 1) // SUBLANES

    def rows(c, carry):
        r0 = pl.multiple_of(c * chunk, chunk)
        acc = jnp.zeros((chunk, CONV_CH), F32)
        for s in range(SUBLANES):
            rows_s = sh_ref[s, pl.ds(r0, chunk + last_a * SUBLANES), :]
            for a in range(last_a + 1):
                k = a * SUBLANES + s - first
                if 0 <= k < CONV_WIDTH:
                    wk = jnp.concatenate([w_ref[k]] * reps, axis=0)
                    acc = acc + rows_s[a * SUBLANES:a * SUBLANES + chunk, :] * wk
        o_ref[0, pl.ds(r0, chunk), :] = acc
        return carry

    lax.fori_loop(0, tile // chunk, rows, 0)
    y = o_ref[0] + b_ref[...]
    mu = jnp.mean(y, axis=-1, keepdims=True)
    yc = y - mu
    var = jnp.mean(yc * yc, axis=-1, keepdims=True)
    o_ref[0] = _silu(yc * lax.rsqrt(var + EPS) * lg_ref[...] + lb_ref[...])


def _conv_operands(glu, conv_w, conv_b, ln_g, ln_b, tile):
    _, l, c = glu.shape
    hal = CONV_HALO
    per = tile // hal
    nh = l // hal
    vec = pl.BlockSpec((1, c), lambda i, j: (0, 0))
    specs = [pl.BlockSpec((1, tile, c), lambda i, j: (i, j, 0)),
             pl.BlockSpec((1, hal, c), lambda i, j: (i, jnp.maximum(j * per - 1, 0), 0)),
             pl.BlockSpec((1, hal, c), lambda i, j: (i, jnp.minimum((j + 1) * per, nh - 1), 0)),
             pl.BlockSpec((CONV_WIDTH, SUBLANES, c), lambda i, j: (0, 0, 0)),
             vec, vec, vec]
    args = [glu, glu, glu, jnp.broadcast_to(conv_w[:, None, :], (CONV_WIDTH, SUBLANES, c)),
            conv_b.reshape(1, c), ln_g.reshape(1, c), ln_b.reshape(1, c)]
    scratch = [pltpu.VMEM((tile + 2 * hal, c), F32),
               pltpu.VMEM((SUBLANES, tile + 2 * hal - SUBLANES, c), F32)]
    return args, specs, scratch


def _conv_branch(glu, conv_w, conv_b, ln_g, ln_b, tile):
    b, l, c = glu.shape
    args, specs, scratch = _conv_operands(glu, conv_w, conv_b, ln_g, ln_b, tile)
    return pl.pallas_call(
        functools.partial(_conv_body, tile=tile, chunk=CONV_CHUNK),
        grid=(b, l // tile),
        in_specs=specs,
        out_specs=pl.BlockSpec((1, tile, c), lambda i, j: (i, j, 0)),
        out_shape=jax.ShapeDtypeStruct((b, l, c), F32),
        scratch_shapes=scratch,
        compiler_params=_cparams(("parallel", "parallel")),
        name="conv_branch",
    )(*args)


def _na_window_start(j, rows):
    rb = NA_ROWS_PER_BLOCK
    return jnp.clip(j * rb - NA_KR // 2, 0, rows - NA_WIN_ROWS)


def _na_body(q_ref, k_ref, v_ref, kc_ref, vc_ref, *rest, rows):
    tab_refs, o_ref = rest[:-1], rest[-1]
    j = pl.program_id(2)
    nkeys = NA_WIN_ROWS * GRID_W
    tq = NA_ROWS_PER_BLOCK * GRID_W
    kc = kc_ref[0]
    vc = vc_ref[0]
    lane = lax.broadcasted_iota(jnp.int32, (1, LANES), 1)
    for sb, tab_ref in enumerate(tab_refs):
        start = pl.multiple_of(_na_window_start(j * len(tab_refs) + sb, rows) * GRID_W, GRID_W)
        q = q_ref[0, sb * tq:(sb + 1) * tq, :]
        kw = k_ref[0, pl.ds(start, nkeys), :]
        vw = v_ref[0, pl.ds(start, nkeys), :]
        out = jnp.zeros(q.shape, F32)
        for hh in range(LANES // NA_HEAD_DIM):
            in_head = (lane >= hh * NA_HEAD_DIM) & (lane < (hh + 1) * NA_HEAD_DIM)
            qh = jnp.where(in_head, q, jnp.zeros_like(q))
            s = _dot_nt(qh, kw) + tab_ref[0, hh]
            sc = _dot_nt(qh, kc)
            m = jnp.maximum(jnp.max(s, axis=-1, keepdims=True), jnp.max(sc, axis=-1, keepdims=True))
            p = jnp.exp(s - m)
            pc = jnp.exp(sc - m)
            denom = jnp.sum(p, axis=-1, keepdims=True) + jnp.sum(pc, axis=-1, keepdims=True)
            o = (_dot(p.astype(BF16), vw) + _dot(pc.astype(BF16), vc)) / denom
            out = jnp.where(in_head, o, out)
        o_ref[0, sb * tq:(sb + 1) * tq, :] = out


def _na_tables(rpb, rows):
    rb = NA_ROWS_PER_BLOCK
    nblk = rows // rb
    wr = NA_WIN_ROWS
    qc = jnp.arange(GRID_W)
    cs = jnp.clip(qc - NA_KC // 2, 0, GRID_W - NA_KC)
    col_ok = (qc[None, :] >= cs[:, None]) & (qc[None, :] < cs[:, None] + NA_KC)
    col_off = qc[None, :] - qc[:, None] + NA_KC - 1
    onehot = (col_off[:, :, None] == jnp.arange(2 * NA_KC - 1)[None, None, :]).astype(F32)
    blocks = jnp.einsum('hrd,qkd->hrqk', rpb.astype(F32), onehot, precision=HIGHEST)
    blocks = jnp.where(col_ok[None, None], blocks, NEG_BIG)
    masked = jnp.full((NA_HEADS, GRID_W, GRID_W), NEG_BIG, F32)
    tabs = []
    for jb in (0, 1, nblk - 1):
        ws = min(max(jb * rb - NA_KR // 2, 0), rows - wr)
        q_rows = []
        for qr in range(jb * rb, (jb + 1) * rb):
            rs = min(max(qr - NA_KR // 2, 0), rows - NA_KR)
            row = [blocks[:, kr - qr + NA_KR - 1] if rs <= kr < rs + NA_KR else masked
                   for kr in range(ws, ws + wr)]
            q_rows.append(jnp.concatenate(row, axis=-1))
        tabs.append(jnp.concatenate(q_rows, axis=1))
    return jnp.stack(tabs)


def _neighbourhood_attention(qkv, qkv_ctx, rpb):
    b, l, _ = qkv.shape
    n_ctx = qkv_ctx.shape[1]
    rows = l // GRID_W
    rb = NA_ROWS_PER_BLOCK
    nblk = rows // rb
    tq = rb * GRID_W
    nkeys = NA_WIN_ROWS * GRID_W
    hp = NA_HEADS * NA_HEAD_DIM // LANES
    tabs = _na_tables(rpb, rows)

    per = NA_BLOCKS_PER_STEP

    def cls(jb):
        return jnp.where(jb == 0, 0, jnp.where(jb == nblk - 1, 2, 1))

    def tab_spec(sb):
        return pl.BlockSpec((1, 2, tq, nkeys), lambda i, h, j: (cls(j * per + sb), h, 0, 0))

    return pl.pallas_call(
        functools.partial(_na_body, rows=rows),
        grid=(b, hp, nblk // per),
        in_specs=[pl.BlockSpec((1, per * tq, LANES), lambda i, h, j: (i, j, h)),
                  pl.BlockSpec((1, l, LANES), lambda i, h, j: (i, 0, hp + h)),
                  pl.BlockSpec((1, l, LANES), lambda i, h, j: (i, 0, 2 * hp + h)),
                  pl.BlockSpec((1, n_ctx, LANES), lambda i, h, j: (i, 0, hp + h)),
                  pl.BlockSpec((1, n_ctx, LANES), lambda i, h, j: (i, 0, 2 * hp + h))]
                 + [tab_spec(sb) for sb in range(per)],
        out_specs=pl.BlockSpec((1, per * tq, LANES), lambda i, h, j: (i, j, h)),
        out_shape=jax.ShapeDtypeStruct((b, l, NA_HEADS * NA_HEAD_DIM), F32),
        compiler_params=_cparams(("parallel", "parallel", "arbitrary")),
        name="neighbourhood_attention",
    )(qkv, qkv, qkv, qkv_ctx, qkv_ctx, *([tabs] * per))


def _ctx_attn_body(q_ref, k_ref, v_ref, o_ref):
    q = q_ref[0]
    k = k_ref[0]
    v = v_ref[0]
    lane = lax.broadcasted_iota(jnp.int32, (1, LANES), 1)
    out = jnp.zeros(q.shape, F32)
    for hh in range(LANES // NA_HEAD_DIM):
        in_head = (lane >= hh * NA_HEAD_DIM) & (lane < (hh + 1) * NA_HEAD_DIM)
        qh = jnp.where(in_head, q, jnp.zeros_like(q))
        s = _dot_nt(qh, k)
        p = jnp.exp(s - jnp.max(s, axis=-1, keepdims=True))
        o = _dot(p.astype(BF16), v) / jnp.sum(p, axis=-1, keepdims=True)
        out = jnp.where(in_head, o, out)
    o_ref[0] = out


def _context_attention(qkv_ctx):
    b, n, _ = qkv_ctx.shape
    hp = NA_HEADS * NA_HEAD_DIM // LANES
    return pl.pallas_call(
        _ctx_attn_body,
        grid=(b, hp),
        in_specs=[pl.BlockSpec((1, n, LANES), lambda i, h: (i, 0, h)),
                  pl.BlockSpec((1, n, LANES), lambda i, h: (i, 0, hp + h)),
                  pl.BlockSpec((1, n, LANES), lambda i, h: (i, 0, 2 * hp + h))],
        out_specs=pl.BlockSpec((1, n, LANES), lambda i, h: (i, 0, h)),
        out_shape=jax.ShapeDtypeStruct((b, n, NA_HEADS * NA_HEAD_DIM), F32),
        compiler_params=_cparams(("parallel", "parallel")),
        name="context_attention",
    )(qkv_ctx, qkv_ctx, qkv_ctx)


def _out_body(a_ref, b_ref, *rest):
    _out_tail(a_ref[0], b_ref[0], *rest)


def _out_tail(a, b2, x_ref, w_ref, g1_ref, gate_ref, g2_ref, sh_ref, sc_ref, rw_ref, xo_ref, h_ref, aff_ref):
    half = a.shape[-1]
    y = _dot(a.astype(BF16), w_ref[0:half, :]) + _dot(b2.astype(BF16), w_ref[half:2 * half, :])
    xn = x_ref[0] + gate_ref[0] * _rms(y, g1_ref[...])
    xo_ref[0] = xn
    h = _rms(xn, g2_ref[...]) * (1.0 + sc_ref[0]) + sh_ref[0]
    packed = _pack_bf16_pairs(h)
    quarter = packed.shape[-1] // 2
    h_ref[0, 0] = packed[:, 0:quarter]
    h_ref[0, 1] = packed[:, quarter:2 * quarter]
    h_hi = h.astype(BF16)
    h_lo = (h - h_hi.astype(F32)).astype(BF16)
    both = _dot(h_hi, rw_ref[...])
    logits = both[:, 0:LANES] + (both[:, LANES:2 * LANES] + _dot(h_lo, rw_ref[:, 0:LANES]))
    lane = lax.broadcasted_iota(jnp.int32, (1, LANES), 1)
    logits = jnp.where(lane < N_EXPERTS, logits, NEG_BIG)
    e = jnp.exp(logits - jnp.max(logits, axis=-1, keepdims=True))
    aff = e / jnp.sum(e, axis=-1, keepdims=True)
    aff_ref[0] = aff.T[0:N_EXPERTS, :]


def _out_call(body, branch_args, branch_specs, scratch, name, x, w_bf, g1, gate, g2, shift, scale, router_w, tm):
    b, l, d = x.shape
    rw = jnp.pad(router_w, ((0, 0), (0, LANES - N_EXPERTS)))
    rw_hi = rw.astype(BF16)
    rw_cat = jnp.concatenate([rw_hi, (rw - rw_hi.astype(F32)).astype(BF16)], axis=1)
    vec = pl.BlockSpec((1, d), lambda i, j: (0, 0))
    bvec = pl.BlockSpec((1, 1, d), lambda i, j: (i, 0, 0))
    rspec = pl.BlockSpec((d, 2 * LANES), lambda i, j: (0, 0))
    return pl.pallas_call(
        body,
        grid=(b, l // tm),
        in_specs=list(branch_specs) + [pl.BlockSpec((1, tm, d), lambda i, j: (i, j, 0)),
                                       pl.BlockSpec(w_bf.shape, lambda i, j: (0, 0)),
                                       vec, bvec, vec, bvec, bvec, rspec],
        out_specs=[pl.BlockSpec((1, tm, d), lambda i, j: (i, j, 0)),
                   pl.BlockSpec((1, 2, tm, d // 4), lambda i, j: (i, 0, j, 0)),
                   pl.BlockSpec((1, N_EXPERTS, tm), lambda i, j: (i, 0, j))],
        out_shape=[jax.ShapeDtypeStruct((b, l, d), F32),
                   jax.ShapeDtypeStruct((b, 2, l, d // 4), jnp.int32),
                   jax.ShapeDtypeStruct((b, N_EXPERTS, l), F32)],
        scratch_shapes=list(scratch),
        compiler_params=_cparams(("parallel", "parallel")),
        name=name,
    )(*branch_args, x, w_bf, g1.reshape(1, d), gate.reshape(b, 1, d), g2.reshape(1, d),
      shift.reshape(b, 1, d), scale.reshape(b, 1, d), rw_cat)


def _even_out_body(*refs, tile):
    n_conv = 7
    buf_ref, sh_ref, a_s = refs[-3:]
    _conv_body(*refs[:n_conv], a_s, buf_ref, sh_ref, tile=tile, chunk=CONV_CHUNK)
    _out_tail(a_s[0], refs[n_conv][0], *refs[n_conv + 1:-3])


def _even_out(glu, conv_w, conv_b, ln_g, ln_b, na, *common):
    tile = common[-1]
    c = glu.shape[-1]
    args, specs, scratch = _conv_operands(glu, conv_w, conv_b, ln_g, ln_b, tile)
    na_spec = pl.BlockSpec((1, tile, na.shape[-1]), lambda i, j: (i, j, 0))
    return _out_call(functools.partial(_even_out_body, tile=tile), args + [na], specs + [na_spec],
                     scratch + [pltpu.VMEM((1, tile, c), F32)], "even_out", *common)


def _out_proj(a, b2, *common):
    tm = common[-1]
    spec = pl.BlockSpec((1, tm, a.shape[-1]), lambda i, j: (i, j, 0))
    return _out_call(_out_body, [a, b2], [spec, spec], [], "out_proj", *common)


def _moe_body(x0_ref, x1_ref, val_ref, wg_ref, wu_ref, wd_ref, o_ref, acc_s, x_s, wg_s, wu_s, wd_s, *, chunk):
    f = pl.program_id(1)
    m, quarter = x0_ref.shape

    @pl.when(f == 0)
    def _():
        def unpack(c, carry):
            r = pl.multiple_of(c * chunk, chunk)
            for s, x_ref in enumerate((x0_ref, x1_ref)):
                lo, hi = _unpack_bf16_pairs(x_ref[pl.ds(r, chunk), :])
                x_s[pl.ds(r, chunk), s * quarter:(s + 1) * quarter] = lo
                x_s[pl.ds(r, chunk), (2 + s) * quarter:(3 + s) * quarter] = hi
            acc_s[pl.ds(r, chunk), :] = jnp.zeros((chunk, 4 * quarter), F32)
            return carry

        lax.fori_loop(0, m // chunk, unpack, 0)

    wg_s[...] = wg_ref[0, 0].astype(BF16)
    wu_s[...] = wu_ref[0, 0].astype(BF16)
    wd_s[...] = wd_ref[0, 0].astype(BF16)

    def rows(c, carry):
        r = pl.multiple_of(c * chunk, chunk)
        xs = x_s[pl.ds(r, chunk), :]
        hid = (_silu(_dot(xs, wg_s[...])) * _dot(xs, wu_s[...])).astype(BF16)
        acc_s[pl.ds(r, chunk), :] += _dot(hid, wd_s[...])
        return carry

    lax.fori_loop(0, m // chunk, rows, 0, unroll=True)

    @pl.when(f == pl.num_programs(1) - 1)
    def _():
        for g0 in range(0, m, LANES):
            w = min(LANES, m - g0)
            eye = lax.broadcasted_iota(jnp.int32, (w, w), 0) == lax.broadcasted_iota(jnp.int32, (w, w), 1)
            col = jnp.sum(jnp.where(eye, val_ref[0, :, g0:g0 + w], 0.0), axis=1, keepdims=True)
            o_ref[0, g0:g0 + w, :] = (acc_s[g0:g0 + w, :] * col).astype(o_ref.dtype)


def _expert_ffn(xg, vals, w_gate, w_up, w_down, layer, chunk):
    e, _, m = vals.shape
    quarter = xg.shape[1]
    d = 4 * quarter
    ff = w_gate.shape[-1]
    tf = EXPERT_FF_TILE
    return pl.pallas_call(
        functools.partial(_moe_body, chunk=chunk),
        grid=(e, ff // tf),
        in_specs=[pl.BlockSpec((m, quarter), lambda i, f: (2 * i, 0)),
                  pl.BlockSpec((m, quarter), lambda i, f: (2 * i + 1, 0)),
                  pl.BlockSpec((1, 1, m), lambda i, f: (i, 0, 0)),
                  pl.BlockSpec((1, 1, d, tf), lambda i, f: (layer, i, 0, f)),
                  pl.BlockSpec((1, 1, d, tf), lambda i, f: (layer, i, 0, f)),
                  pl.BlockSpec((1, 1, tf, d), lambda i, f: (layer, i, f, 0))],
        out_specs=pl.BlockSpec((1, m, d), lambda i, f: (i, 0, 0)),
        out_shape=jax.ShapeDtypeStruct((e, m, d), BF16),
        scratch_shapes=[pltpu.VMEM((m, d), F32), pltpu.VMEM((m, d), BF16), pltpu.VMEM((d, tf), BF16),
                        pltpu.VMEM((d, tf), BF16), pltpu.VMEM((tf, d), BF16)],
        compiler_params=_cparams(("parallel", "arbitrary")),
        name="expert_ffn",
    )(xg, xg, vals, w_gate, w_up, w_down)


def _resid_body(x_ref, f_ref, gate_ref, g_ref, o_ref):
    o_ref[0] = x_ref[0] + gate_ref[0] * _rms(f_ref[0], g_ref[...])


def _gated_residual(x, f, gate, g, tm):
    b, l, d = x.shape
    blk = pl.BlockSpec((1, tm, d), lambda i, j: (i, j, 0))
    return pl.pallas_call(
        _resid_body,
        grid=(b, l // tm),
        in_specs=[blk, blk, pl.BlockSpec((1, 1, d), lambda i, j: (i, 0, 0)),
                  pl.BlockSpec((1, d), lambda i, j: (0, 0))],
        out_specs=blk,
        out_shape=jax.ShapeDtypeStruct((b, l, d), F32),
        compiler_params=_cparams(("parallel", "parallel")),
        name="gated_residual",
    )(x, f, gate.reshape(b, 1, d), g.reshape(1, d))


def _swap_pairs(x):
    nf = GLA_DK // 4
    lane = lax.broadcasted_iota(jnp.int32, (1, LANES), 1)
    up = pltpu.roll(x, LANES - nf, 1)
    down = pltpu.roll(x, nf, 1)
    return jnp.where(lane % (2 * nf) < nf, up, down)


def _odd_in_body(x_ref, g_ref, sh_ref, sc_ref, w_ref, cos_ref, sin_ref, gw_ref, gb_ref,
                 pool_ref, qk_ref, v_ref, r_ref, gate_ref):
    h = (_rms(x_ref[0], g_ref[...]) * (1.0 + sc_ref[0]) + sh_ref[0]).astype(BF16)
    qk = GLA_HEADS * GLA_DK
    vd = GLA_HEADS * GLA_DV
    q0 = POOL_CH
    v0 = q0 + 2 * qk
    r0 = v0 + vd
    l0 = r0 + vd
    pool_ref[0] = _dot(h, w_ref[:, 0:q0])
    qk_raw = _dot(h, w_ref[:, q0:v0])
    for s in range(2 * qk // LANES):
        raw = qk_raw[:, s * LANES:(s + 1) * LANES]
        c = cos_ref[:, (s * LANES) % qk:(s * LANES) % qk + LANES]
        sn = sin_ref[:, (s * LANES) % qk:(s * LANES) % qk + LANES]
        rot = raw * c + _swap_pairs(raw) * sn
        if s * LANES < qk:
            rot = rot * (GLA_DK ** -0.5)
        qk_ref[0, :, s * LANES:(s + 1) * LANES] = rot
    v_ref[0] = _dot(h, w_ref[:, v0:r0]).astype(BF16)
    r_ref[0] = _dot(h, w_ref[:, r0:l0])
    lr = _dot(h, w_ref[:, l0:l0 + 2 * GLA_RANK])
    z = jnp.dot(lr, gw_ref[...], precision=HIGHEST, preferred_element_type=F32) + gb_ref[...]
    gate_ref[0] = (jnp.minimum(z, 0.0) - jnp.log1p(jnp.exp(-jnp.abs(z)))) * (1.0 / GLA_TAU)


def _odd_in(x, g, shift, scale, w_bf, cos_t, sin_t, gate_w, gate_b, tm):
    b, l, d = x.shape
    n = w_bf.shape[1]
    qk = GLA_HEADS * GLA_DK
    vd = GLA_HEADS * GLA_DV
    gw = jnp.zeros((2 * GLA_RANK, 2 * qk), F32)
    gw = gw.at[:GLA_RANK, :qk].set(gate_w[0]).at[GLA_RANK:, qk:].set(gate_w[1])
    gb = jnp.concatenate([gate_b[0], gate_b[1]]).reshape(1, 2 * qk)
    vec = pl.BlockSpec((1, 1, d), lambda i, j: (i, 0, 0))
    row = lambda w: pl.BlockSpec((1, tm, w), lambda i, j: (i, j, 0))
    return pl.pallas_call(
        _odd_in_body,
        grid=(b, l // tm),
        in_specs=[row(d), pl.BlockSpec((1, d), lambda i, j: (0, 0)), vec, vec,
                  pl.BlockSpec((d, n), lambda i, j: (0, 0)),
                  pl.BlockSpec((tm, qk), lambda i, j: (j, 0)),
                  pl.BlockSpec((tm, qk), lambda i, j: (j, 0)),
                  pl.BlockSpec((2 * GLA_RANK, 2 * qk), lambda i, j: (0, 0)),
                  pl.BlockSpec((1, 2 * qk), lambda i, j: (0, 0))],
        out_specs=[row(POOL_CH), row(2 * qk), row(vd), row(vd), row(2 * qk)],
        out_shape=[jax.ShapeDtypeStruct((b, l, POOL_CH), F32),
                   jax.ShapeDtypeStruct((b, l, 2 * qk), F32),
                   jax.ShapeDtypeStruct((b, l, vd), BF16),
                   jax.ShapeDtypeStruct((b, l, vd), F32),
                   jax.ShapeDtypeStruct((b, l, 2 * qk), F32)],
        compiler_params=_cparams(("parallel", "parallel")),
        name="odd_in",
    )(x, g.reshape(1, d), shift.reshape(b, 1, d), scale.reshape(b, 1, d), w_bf, cos_t, sin_t, gw, gb)


def _rope_tables(l):
    t = np.arange(l)
    nf = GLA_DK // 4
    inv = np.power(ROPE_BASE, -np.arange(nf, dtype=np.float64) / nf)
    ar = (t // GRID_W)[:, None] * inv[None, :]
    ac = (t % GRID_W)[:, None] * inv[None, :]
    cos_h = np.concatenate([np.cos(ar), np.cos(ar), np.cos(ac), np.cos(ac)], axis=-1)
    sin_h = np.concatenate([-np.sin(ar), np.sin(ar), -np.sin(ac), np.sin(ac)], axis=-1)
    as_table = lambda a: jnp.asarray(np.tile(a, (1, GLA_HEADS)).astype(np.float32))
    return as_table(cos_h), as_table(sin_h)


def _gla_tile(qk, v, g, s, reverse):
    hk = GLA_HEADS * GLA_DK
    hv = GLA_HEADS * GLA_DV
    c = GLA_CHUNK
    t = qk.shape[0]
    n = t // c
    last_row, mid_row = (0, c // 2) if reverse else (c - 1, c // 2 - 1)
    ii = lax.broadcasted_iota(jnp.int32, (t, t), 0)
    jj = lax.broadcasted_iota(jnp.int32, (t, t), 1)
    ordered = (jj >= ii) if reverse else (jj <= ii)
    tri = jnp.where(ordered & (ii // c == jj // c), 1.0, 0.0).astype(BF16)
    g_hi = g.astype(BF16)
    rem = g - g_hi.astype(F32)
    g_mid = rem.astype(BF16)
    g_lo = (rem - g_mid.astype(F32)).astype(BF16)
    bc = _dot(tri, g_hi) + (_dot(tri, g_mid) + _dot(tri, g_lo))
    spread = lambda row: jnp.concatenate(
        [jnp.broadcast_to(bc[i * c + row:i * c + row + 1, :], (c, hk)) for i in range(n)], axis=0)
    b_mid = spread(mid_row)
    b_last = spread(last_row)
    qt = qk[:, 0:hk] * jnp.exp(bc - b_mid)
    kt = qk[:, hk:2 * hk] * jnp.exp(b_mid - bc)
    qe = (qt * jnp.exp(b_mid)).astype(BF16)
    ke = kt * jnp.exp(b_last - b_mid)
    ktb = kt.astype(BF16)
    lane = lax.broadcasted_iota(jnp.int32, (1, hk), 1)
    ci = lax.broadcasted_iota(jnp.int32, (c, c), 0)
    cj = lax.broadcasted_iota(jnp.int32, (c, c), 1)
    causal = (cj >= ci) if reverse else (cj <= ci)
    blockdiag = (lax.broadcasted_iota(jnp.int32, (hk, hv), 0) // GLA_DK
                 == lax.broadcasted_iota(jnp.int32, (hk, hv), 1) // GLA_DV)
    intra, upd, decay = [], [], []
    for i in range(n):
        rows = slice(i * c, (i + 1) * c)
        qs = jnp.concatenate(
            [jnp.where((lane >= h * GLA_DK) & (lane < (h + 1) * GLA_DK), qt[rows], 0.0) for h in range(GLA_HEADS)],
            axis=0).astype(BF16)
        att = _dot_nt(qs, ktb[rows])
        intra.append(jnp.concatenate(
            [_dot(jnp.where(causal, att[h * c:(h + 1) * c], 0.0).astype(BF16),
                  v[rows, h * GLA_DV:(h + 1) * GLA_DV]) for h in range(GLA_HEADS)], axis=-1))
        upd.append(jnp.where(blockdiag, _dot(ke[rows].T.astype(BF16), v[rows]), 0.0))
        decay.append(jnp.exp(jnp.sum(g[rows].T, axis=1, keepdims=True)))
    outs = [None] * n
    for i in (reversed(range(n)) if reverse else range(n)):
        rows = slice(i * c, (i + 1) * c)
        outs[i] = _dot(qe[rows], s.astype(BF16)) + intra[i]
        s = decay[i] * s + upd[i]
    return jnp.concatenate(outs, axis=0), s


def _gla_body(qkf_ref, qkb_ref, vf_ref, vb_ref, gf_ref, gb_ref, s0f_ref, s0b_ref,
              of_ref, ob_ref, sff_ref, sbf_ref, sf_ref, sb_ref, *, tile):
    n = pl.program_id(1)
    hk = GLA_HEADS * GLA_DK
    hv = GLA_HEADS * GLA_DV

    @pl.when(n == 0)
    def _():
        sf_ref[...] = jnp.zeros((hk, hv), F32)
        sb_ref[...] = jnp.zeros((hk, hv), F32)
        for h in range(GLA_HEADS):
            sf_ref[h * GLA_DK:(h + 1) * GLA_DK, h * GLA_DV:(h + 1) * GLA_DV] = s0f_ref[0, h]
            sb_ref[h * GLA_DK:(h + 1) * GLA_DK, h * GLA_DV:(h + 1) * GLA_DV] = s0b_ref[0, h]

    of_ref[0], sf_ref[...] = _gla_tile(qkf_ref[0], vf_ref[0], gf_ref[0], sf_ref[...], False)
    ob_ref[0], sb_ref[...] = _gla_tile(qkb_ref[0], vb_ref[0], gb_ref[0], sb_ref[...], True)

    @pl.when(n == pl.num_programs(1) - 1)
    def _():
        for h in range(GLA_HEADS):
            sff_ref[0, h] = sf_ref[h * GLA_DK:(h + 1) * GLA_DK, h * GLA_DV:(h + 1) * GLA_DV]
            sbf_ref[0, h] = sb_ref[h * GLA_DK:(h + 1) * GLA_DK, h * GLA_DV:(h + 1) * GLA_DV]


def _gla(qk, v, gates, s0f, s0b, tile):
    b, l, _ = qk.shape
    hk = GLA_HEADS * GLA_DK
    hv = GLA_HEADS * GLA_DV
    nt = l // tile
    fwd = lambda w, col: pl.BlockSpec((1, tile, w), lambda i, n: (i, n, col))
    bwd = lambda w, col: pl.BlockSpec((1, tile, w), lambda i, n: (i, nt - 1 - n, col))
    st = pl.BlockSpec((1, GLA_HEADS, GLA_DK, GLA_DV), lambda i, n: (i, 0, 0, 0))
    return pl.pallas_call(
        functools.partial(_gla_body, tile=tile),
        grid=(b, nt),
        in_specs=[fwd(2 * hk, 0), bwd(2 * hk, 0), fwd(hv, 0), bwd(hv, 0), fwd(hk, 0), bwd(hk, 1), st, st],
        out_specs=[fwd(hv, 0), bwd(hv, 0), st, st],
        out_shape=[jax.ShapeDtypeStruct((b, l, hv), F32), jax.ShapeDtypeStruct((b, l, hv), F32),
                   jax.ShapeDtypeStruct((b, GLA_HEADS, GLA_DK, GLA_DV), F32),
                   jax.ShapeDtypeStruct((b, GLA_HEADS, GLA_DK, GLA_DV), F32)],
        scratch_shapes=[pltpu.VMEM((hk, hv), F32), pltpu.VMEM((hk, hv), F32)],
        compiler_params=_cparams(("parallel", "arbitrary")),
        name="gla_scan",
    )(qk, qk, v, v, gates, gates, s0f, s0b)


def _odd_mid_body(cur_ref, prev_ref, next_ref, of_ref, ob_ref, r_ref, hg_ref, pw_ref, ps_ref,
                  pool_ref, d_ref, buf_ref, *, tile, seq):
    j = pl.program_id(1)
    last = pl.num_programs(1) - 1
    hal = POOL_HALO
    buf_ref[0:hal, :] = jnp.where(j == 0, 0.0, prev_ref[0])
    buf_ref[hal:hal + tile, :] = cur_ref[0]
    buf_ref[hal + tile:hal + tile + hal, :] = jnp.where(j == last, 0.0, next_ref[0])
    t = j * tile + lax.broadcasted_iota(jnp.int32, (tile, 1), 0)
    for gi, win in enumerate(POOL_WINDOWS):
        cols = slice(gi * POOL_GROUP, (gi + 1) * POOL_GROUP)
        acc = jnp.zeros((tile, POOL_GROUP), F32)
        for off in range(-(win // 2), win - win // 2):
            acc = acc + buf_ref[hal + off:hal + off + tile, cols]
        cnt = jnp.minimum(t + (win - win // 2), seq) - jnp.maximum(t - win // 2, 0)
        diff = acc / cnt.astype(F32) - cur_ref[0, :, cols]
        pool_ref[0, :, cols] = _dot(diff.astype(BF16), pw_ref[gi]) * ps_ref[:, cols]
    for h in range(GLA_HEADS):
        cols = slice(h * GLA_DV, (h + 1) * GLA_DV)
        o = of_ref[0, :, cols] + ob_ref[0, :, cols]
        d_ref[0, :, cols] = _rms(o, hg_ref[:, cols]) * _silu(r_ref[0, :, cols])


def _odd_out_body(*refs, tile, seq):
    n_branch = 9
    buf_ref, pool_s, d_s = refs[-3:]
    _odd_mid_body(*refs[:n_branch], pool_s, d_s, buf_ref, tile=tile, seq=seq)
    _out_tail(pool_s[0], d_s[0], *refs[n_branch:-3])


def _odd_out(pool_u, o_f, o_b, r, head_g, pool_w_bf, pool_scale, *common):
    b, l, c = pool_u.shape
    tile = common[-1]
    hal = POOL_HALO
    per = tile // hal
    nh = l // hal
    blk = pl.BlockSpec((1, tile, c), lambda i, j: (i, j, 0))
    vec = pl.BlockSpec((1, c), lambda i, j: (0, 0))
    specs = [blk,
             pl.BlockSpec((1, hal, c), lambda i, j: (i, jnp.maximum(j * per - 1, 0), 0)),
             pl.BlockSpec((1, hal, c), lambda i, j: (i, jnp.minimum((j + 1) * per, nh - 1), 0)),
             blk, blk, blk, vec,
             pl.BlockSpec((len(POOL_WINDOWS), POOL_GROUP, POOL_GROUP), lambda i, j: (0, 0, 0)),
             vec]
    scratch = [pltpu.VMEM((tile + 2 * hal, c), F32), pltpu.VMEM((1, tile, c), F32), pltpu.VMEM((1, tile, c), F32)]
    args = [pool_u, pool_u, pool_u, o_f, o_b, r, head_g.reshape(1, c), pool_w_bf, pool_scale.reshape(1, c)]
    return _out_call(functools.partial(_odd_out_body, tile=tile, seq=l), args, specs, scratch, "odd_out", *common)


def _threshold_body(a_ref, thr_ref, need_ref, *, cap):
    bits = lax.bitcast_convert_type(a_ref[0], jnp.int32)
    rows = bits.shape[0]
    count_ge = lambda v: jnp.sum(jnp.where(bits >= v, 1.0, 0.0), axis=-1, keepdims=True)

    def step(_, carry):
        lo, hi = carry
        mid = lo + ((hi - lo + 1) >> 1)
        ok = count_ge(mid) >= cap
        return jnp.where(ok, mid, lo), jnp.where(ok, hi, mid - 1)

    lo, _ = lax.fori_loop(0, 31, step, (jnp.zeros((rows, 1), jnp.int32),
                                        jnp.full((rows, 1), F32_INF_BITS, jnp.int32)))
    above = jnp.sum(jnp.where(bits > lo, 1.0, 0.0), axis=-1, keepdims=True)
    width = thr_ref.shape[-1]
    thr_ref[0] = jnp.broadcast_to(lax.bitcast_convert_type(lo, F32), (rows, width))
    need_ref[0] = jnp.broadcast_to(cap - above.astype(jnp.int32), (rows, width))


def _route_threshold(aff_t, cap):
    b, e, n = aff_t.shape
    out = pl.BlockSpec((1, e, SC_LANES), lambda i: (i, 0, 0))
    return pl.pallas_call(
        functools.partial(_threshold_body, cap=cap),
        grid=(b,),
        in_specs=[pl.BlockSpec((1, e, n), lambda i: (i, 0, 0))],
        out_specs=[out, out],
        out_shape=[jax.ShapeDtypeStruct((b, e, SC_LANES), F32), jax.ShapeDtypeStruct((b, e, SC_LANES), jnp.int32)],
        compiler_params=_cparams(("parallel",)),
        name="route_threshold",
    )(aff_t)


def _route_compact(aff, thr, need, cap):
    r, n = aff.shape
    lanes = SC_LANES
    assert r == SC_CORES * SC_SUBCORES and n % lanes == 0
    mesh = plsc.VectorSubcoreMesh(core_axis_name="core", subcore_axis_name="subcore",
                                  num_cores=SC_CORES, num_subcores=SC_SUBCORES)

    @pl.kernel(out_type=[jax.ShapeDtypeStruct((r, cap), jnp.int32), jax.ShapeDtypeStruct((r, cap), F32)],
               mesh=mesh,
               scratch_types=[pltpu.VMEM((n,), F32), pltpu.VMEM((lanes,), F32), pltpu.VMEM((lanes,), jnp.int32),
                              pltpu.VMEM((cap,), jnp.int32), pltpu.VMEM((cap,), F32)],
               compiler_params=pltpu.CompilerParams(needs_layout_passes=False),
               name="route_compact")
    def compact(aff_hbm, thr_hbm, need_hbm, idx_hbm, val_hbm, row_v, thr_v, need_v, idx_v, val_v):
        w = lax.axis_index("subcore") * SC_CORES + lax.axis_index("core")
        pltpu.sync_copy(aff_hbm.at[w], row_v)
        pltpu.sync_copy(thr_hbm.at[w], thr_v)
        pltpu.sync_copy(need_hbm.at[w], need_v)
        thr = thr_v[...]
        need = need_v[...]
        lane = lax.iota(jnp.int32, lanes)
        ones = jnp.ones((lanes,), jnp.int32)

        def body(i, carry):
            n_out, n_eq = carry
            x = row_v[pl.ds(i * lanes, lanes)]
            eq = x == thr
            take = (x > thr) | (eq & (n_eq + plsc.cumsum(ones, mask=eq) <= need))
            pos = n_out + plsc.cumsum(ones, mask=take) - 1
            take = take & (pos < cap)
            plsc.store_scatter(idx_v, [pos], lane + i * lanes, mask=take)
            plsc.store_scatter(val_v, [pos], x, mask=take)
            return (n_out + plsc.all_reduce_population_count(take),
                    n_eq + plsc.all_reduce_population_count(eq))

        zero = jnp.zeros((lanes,), jnp.int32)
        lax.fori_loop(0, n // lanes, body, (zero, zero))
        pltpu.sync_copy(idx_v, idx_hbm.at[w])
        pltpu.sync_copy(val_v, val_hbm.at[w])

    return compact(aff, thr, need)


def _route(aff_t, tok_base, row_base):
    b, e, n = aff_t.shape
    cap = EC_CAPACITY_FACTOR * n // N_EXPERTS
    thr, need = _route_threshold(aff_t, cap)
    idx, vals = _route_compact(aff_t.reshape(b * e, n), thr.reshape(b * e, -1), need.reshape(b * e, -1), cap)
    idx = idx.reshape(b, e, cap)
    vals = vals.reshape(b, e, cap)
    bi = jnp.arange(b, dtype=idx.dtype)[:, None, None]
    per_expert = lambda a: jnp.swapaxes(a, 0, 1).reshape(e, b * cap)
    rows0 = idx + row_base + 2 * bi * n
    return (per_expert(vals), per_expert(idx + tok_base + bi * n), per_expert(rows0), per_expert(rows0 + n),
            per_expert(idx))


def _combine_first_start(p0, cap):
    return jnp.minimum((p0 // BF16_ROWS) * BF16_ROWS, cap - COMBINE_FIRST)


def _combine_body(offs_ref, spill_ref, tok_ref, y_ref, x_ref, gate_ref, g_ref, o_ref, f_s, *, n_tok, cap):
    bi = pl.program_id(0)
    j = pl.program_id(1)
    tt = COMBINE_TILE
    wf = COMBINE_FIRST
    wn = COMBINE_WINDOW
    ntiles = n_tok // tt
    n_exp = y_ref.shape[0]
    group = wn // wf
    sub = lax.broadcasted_iota(jnp.int32, (tt, 1), 0)
    lane = lax.broadcasted_iota(jnp.int32, (1, wn), 1)

    def slot_range(e):
        base = (bi * n_exp + e) * (ntiles + 1) + j
        return offs_ref[base], offs_ref[base + 1]

    acc = jnp.zeros(f_s.shape, F32)
    for g in range(n_exp // group):
        ys = []
        toks = jnp.full((1, wn), -1, jnp.int32)
        for k in range(group):
            e = g * group + k
            start = pl.multiple_of(_combine_first_start(slot_range(e)[0], cap), BF16_ROWS)
            ys.append(y_ref[e, pl.ds(start, wf), :])
            cs = pl.multiple_of(jnp.minimum((start // LANES) * LANES, cap - wn), LANES)
            rolled = pltpu.roll(tok_ref[e, :, pl.ds(cs, wn)], (k * wf + wn - (start - cs)) % wn, 1)
            toks = jnp.where((lane >= k * wf) & (lane < (k + 1) * wf), rolled, toks)
        hit = (toks - j * tt) == sub
        acc = acc + _dot(jnp.where(hit, 1.0, 0.0).astype(BF16), jnp.concatenate(ys, axis=0))
    f_s[...] = acc

    def more_windows(e, carry):
        p0, p1 = slot_range(e)
        lo = _combine_first_start(p0, cap) + wf
        first = (lo // LANES) * LANES

        def extra(w, carry):
            cs = pl.multiple_of(jnp.minimum(first + w * wn, cap - wn), LANES)
            tok = tok_ref[e, :, pl.ds(cs, wn)] - j * tt
            hit = (tok == sub) & (cs + lane >= jnp.maximum(lo, first + w * wn))
            f_s[...] += _dot(jnp.where(hit, 1.0, 0.0).astype(BF16), y_ref[e, pl.ds(cs, wn), :])
            return carry

        lax.fori_loop(0, (jnp.maximum(p1 - first, 0) + wn - 1) // wn * (p1 > lo).astype(jnp.int32), extra, 0)
        return carry

    @pl.when(spill_ref[bi * ntiles + j] != 0)
    def _():
        lax.fori_loop(0, n_exp, more_windows, 0)

    o_ref[0] = x_ref[0] + gate_ref[0] * _rms(f_s[...], g_ref[...])


def _combine(y, tok, local, seg0, x, gate, g):
    e, _, d = y.shape
    _, b, cap = local.shape
    n_tok = x.shape[1]
    tt = COMBINE_TILE
    wf = COMBINE_FIRST
    ntiles = n_tok // tt
    assert seg0 % cap == 0 and n_tok % tt == 0 and cap % COMBINE_WINDOW == 0 and e % (COMBINE_WINDOW // wf) == 0
    seg = seg0 // cap
    bounds = jnp.arange(ntiles + 1, dtype=jnp.int32) * tt
    offs = jnp.sum((local[..., None] < bounds).astype(jnp.int32), axis=2)
    starts = _combine_first_start(offs[..., :-1], cap)
    spill = jnp.any(offs[..., 1:] > starts + wf, axis=0).astype(jnp.int32).reshape(-1)
    row = pl.BlockSpec((1, tt, d), lambda i, j, offs, spill: (i, j, 0))
    grid_spec = pltpu.PrefetchScalarGridSpec(
        num_scalar_prefetch=2,
        grid=(b, ntiles),
        in_specs=[pl.BlockSpec((e, 1, cap), lambda i, j, offs, spill: (0, 0, seg + i)),
                  pl.BlockSpec((e, cap, d), lambda i, j, offs, spill: (0, seg + i, 0),
                               pipeline_mode=pl.Buffered(1)),
                  row,
                  pl.BlockSpec((1, 1, d), lambda i, j, offs, spill: (i, 0, 0)),
                  pl.BlockSpec((1, d), lambda i, j, offs, spill: (0, 0))],
        out_specs=row,
        scratch_shapes=[pltpu.VMEM((tt, d), F32)],
    )
    return pl.pallas_call(
        functools.partial(_combine_body, n_tok=n_tok, cap=cap),
        grid_spec=grid_spec,
        out_shape=jax.ShapeDtypeStruct((b, n_tok, d), F32),
        compiler_params=_cparams(("parallel", "arbitrary")),
        name="combine",
    )(jnp.swapaxes(offs, 0, 1).reshape(-1), spill, tok, y, x, gate.reshape(b, 1, d), g.reshape(1, d))


def _gather_rows(src, idx):
    window = SC_GATHER_WINDOW
    n = idx.shape[0]
    width = src.shape[1]
    assert 2 * window * width * 4 <= SC_TILE_VMEM_BUDGET, width
    assert n % (window * SC_CORES * SC_SUBCORES) == 0, n
    mesh = plsc.VectorSubcoreMesh(core_axis_name="core", subcore_axis_name="subcore",
                                  num_cores=SC_CORES, num_subcores=SC_SUBCORES)

    @pl.kernel(out_type=jax.ShapeDtypeStruct((n, width), src.dtype), mesh=mesh, scratch_types=[],
               name="gather_rows")
    def gather(src_hbm, idx_hbm, out_hbm):
        def body(idx_vmem, out_vmem):
            pltpu.sync_copy(src_hbm.at[idx_vmem.at[0]], out_vmem)

        pltpu.emit_pipeline(
            body,
            grid=(n // window,),
            in_specs=[pl.BlockSpec((1, window), lambda i: (0, i))],
            out_specs=[pl.BlockSpec((window, width), lambda i: (i, 0))],
            core_axis_name=("core", "subcore"),
            dimension_semantics=(pltpu.PARALLEL,),
        )(idx_hbm, out_hbm)

    return gather(src, idx.reshape(1, n))


def _moe(parts, g, w_gate, w_up, w_down, layer):
    quarter = parts[0][1].shape[-1]
    d = 4 * quarter
    sizes = [p[1].shape[0] * p[1].shape[2] for p in parts]
    bases = [sum(sizes[:i]) for i in range(len(parts))]
    routed = [_route(p[0], base, 2 * base) for p, base in zip(parts, bases)]
    src = jnp.concatenate([p[1].reshape(-1, quarter) for p in parts], axis=0)
    vals, flat, rows0, rows1, tok = (jnp.concatenate([r[i] for r in routed], axis=1) for i in range(5))
    e, m = flat.shape
    rows = jnp.stack([rows0, rows1], axis=1).reshape(-1)
    unit = SC_GATHER_WINDOW * SC_CORES * SC_SUBCORES
    fill = jnp.arange(-rows.shape[0] % unit, dtype=rows.dtype)
    xg = _gather_rows(src, jnp.concatenate([rows, fill]))
    chunk = next(c for c in (512, 528, 384, 320, 256, 128) if m % c == 0)
    y = _expert_ffn(xg, vals[:, None, :], w_gate, w_up, w_down, layer, chunk)
    outs = []
    seg0 = 0
    for (aff_t, _, x, gate), r, base in zip(parts, routed, bases):
        b, _, n = aff_t.shape
        cap = r[0].shape[1] // b
        if n % COMBINE_TILE == 0 and cap % COMBINE_WINDOW == 0 and seg0 % cap == 0:
            outs.append(_combine(y, tok[:, None, :], r[4].reshape(e, b, cap), seg0, x, gate, g))
        else:
            ids = r[1] - base
            part = y[:, seg0:seg0 + b * cap].astype(F32)
            f = jnp.zeros((b * n, d), F32).at[ids.reshape(-1)].add(part.reshape(-1, d))
            outs.append(_gated_residual(x, f.reshape(b, n, d), gate, g, n))
        seg0 += b * cap
    return outs


def kernel(x, c, ctx, c_ctx, w_mod, b_mod, norm_g, w_in_even, w_out_even, conv_w, conv_b, conv_ln_g,
           conv_ln_b, na_rpb, w_in_odd, w_out_odd, pool_w, pool_scale, gla_gate_w, gla_gate_b, gla_head_g,
           router_w, expert_w_gate, expert_w_up, expert_w_down):
    b, l, d = x.shape
    n_ctx = ctx.shape[1]
    tm = ROW_TILE

    mod_rows = jnp.concatenate([c, c_ctx[None], jnp.zeros((8 - b - 1, d), F32)], axis=0)

    mod_all = _modulation(mod_rows, w_mod, b_mod)

    def modulation(i):
        mm = mod_all[i]
        m = mm[:b].reshape(b, 6, d)
        mc = jnp.broadcast_to(mm[b].reshape(1, 6, d), (b, 6, d))
        return m, mc

    m, mc = modulation(0)
    g = norm_g[0]
    w_in = w_in_even[0].astype(BF16)
    w_out = w_out_even[0].astype(BF16)
    glu, qkv = _even_in(x, g[0], m[:, 0], m[:, 1], w_in, 2 * tm)
    glu_c, qkv_c = _even_in(ctx, g[0], mc[:, 0], mc[:, 1], w_in, n_ctx)
    a_ctx = _conv_branch(glu_c, conv_w[0], conv_b[0], conv_ln_g[0], conv_ln_b[0], n_ctx)
    na = _neighbourhood_attention(qkv, qkv_c, na_rpb[0])
    att_c = _context_attention(qkv_c)
    x, h2, aff = _even_out(glu, conv_w[0], conv_b[0], conv_ln_g[0], conv_ln_b[0], na,
                           x, w_out, g[1], m[:, 2], g[2], m[:, 3], m[:, 4], router_w[0], tm)
    ctx, h2c, aff_c = _out_proj(a_ctx, att_c, ctx, w_out, g[1], mc[:, 2], g[2], mc[:, 3], mc[:, 4],
                                router_w[0], n_ctx)
    x, ctx = _moe([(aff, h2, x, m[:, 5]), (aff_c, h2c, ctx, mc[:, 5])], g[3],
                  expert_w_gate, expert_w_up, expert_w_down, 0)

    m, mc = modulation(1)
    g = norm_g[1]
    w_in = w_in_odd[0].astype(BF16)
    w_out = w_out_odd[0].astype(BF16)
    cos_t, sin_t = _rope_tables(l)
    ones_t = jnp.ones((n_ctx, GLA_HEADS * GLA_DK), F32)
    _, qk_c, v_c, _, gate_c = _odd_in(ctx, g[0], mc[:, 0], mc[:, 1], w_in, ones_t, jnp.zeros_like(ones_t),
                                      gla_gate_w[0], gla_gate_b[0], n_ctx)
    s_zero = jnp.zeros((b, GLA_HEADS, GLA_DK, GLA_DV), F32)
    _, _, s_f, s_b = _gla(qk_c, v_c, gate_c, s_zero, s_zero, n_ctx)
    pool_u, qk, v, r, gate = _odd_in(x, g[0], m[:, 0], m[:, 1], w_in, cos_t, sin_t,
                                     gla_gate_w[0], gla_gate_b[0], 2 * tm)
    o_f, o_b, _, _ = _gla(qk, v, gate, s_f, s_b, SEQ_TILE)
    x, h2, aff = _odd_out(pool_u, o_f, o_b, r, gla_head_g[0], pool_w[0].astype(BF16), pool_scale[0],
                          x, w_out, g[1], m[:, 2], g[2], m[:, 3], m[:, 4], router_w[1], tm)
    (x,) = _moe([(aff, h2, x, m[:, 5])], g[3], expert_w_gate, expert_w_up, expert_w_down, 1)
    return x
```
